```python
import jax, jax.numpy as jnp
from jax import lax
import numpy as np

D_MODEL = 1024
BATCH = 8
SEQ = 8192
DEPTH = 2

EXPAND = 2
D_INNER = EXPAND * D_MODEL
EPS = 1e-6
NEG = -1e30

A_WIDTH = D_INNER // 2
POOL_SIZES = (2, 4, 8, 16)
A_GROUPS = len(POOL_SIZES)
A_GROUP_DIM = A_WIDTH // A_GROUPS
B_WIDTH = D_INNER - A_WIDTH
B_GROUPS = 4
B_GROUP_DIM = B_WIDTH // B_GROUPS
CHUNK = 128
EVEN_IN = 2 * A_WIDTH + 3 * B_WIDTH

DILATED = ((128, 1), (512, 4), (2048, 16))
N_DIL = len(DILATED)
HEAD_DIM = 128
C_SLOTS = 8
C_HEADS = C_SLOTS * N_DIL
C_WIDTH = C_SLOTS * HEAD_DIM
D_WIDTH = D_INNER - C_WIDTH
CONV_W = 3
ATTN_BLOCK = 128
QKV_WIDTH = 3 * C_HEADS * HEAD_DIM
ODD_IN = QKV_WIDTH + C_WIDTH + 4 * D_WIDTH

N_EVEN = (DEPTH + 1) // 2
N_ODD = DEPTH // 2

kernel_name = "hybrid_pool_gmlp_dilattn_shortconv"


def rmsnorm(x, g):
    xf = x.astype(jnp.float32)
    y = xf * lax.rsqrt(jnp.mean(xf * xf, axis=-1, keepdims=True) + EPS)
    return (y * g.astype(jnp.float32)).astype(x.dtype)


def alibi_slopes(n):
    return jnp.asarray(2.0 ** (-8.0 * (np.arange(n) + 1) / n), dtype=jnp.float32)


def multiscale_pool(a, pool_w, pool_scale):
    Bn, S, _ = a.shape
    ag = a.reshape(Bn, S, A_GROUPS, A_GROUP_DIM).astype(jnp.float32)
    cs = jnp.pad(jnp.cumsum(ag, axis=1), ((0, 0), (1, 0), (0, 0), (0, 0)))
    t = jnp.arange(S)
    means = []
    for g, w in enumerate(POOL_SIZES):
        lo = jnp.maximum(t + 1 - w, 0)
        cnt = (t + 1 - lo).astype(jnp.float32)
        means.append((cs[:, 1:, g] - cs[:, lo, g]) / cnt[None, :, None])
    pooled = (jnp.stack(means, axis=2) - ag).astype(a.dtype)
    mixed = jnp.einsum('bsgc,gcd->bsgd', pooled, pool_w)
    return mixed.reshape(Bn, S, A_WIDTH) * pool_scale


def chunk_spatial_gate(u, v, ws, bs):
    Bn, S, _ = v.shape
    nc = S // CHUNK
    vg = v.reshape(Bn, nc, CHUNK, B_GROUPS, B_GROUP_DIM)
    causal = jnp.tril(jnp.ones((CHUNK, CHUNK), dtype=bool))
    w = jnp.where(causal[None], ws, jnp.zeros_like(ws))
    mixed = jnp.einsum('gts,bnsgc->bntgc', w, vg) + bs.T[None, None, :, :, None]
    return u * mixed.reshape(Bn, S, B_WIDTH)


def dilated_group_attention(q, k, v, window, dilation, slopes):
    Bn, S, H, Dh = q.shape
    unit = dilation * ATTN_BLOCK
    Sp = -(-S // unit) * unit
    pad = Sp - S
    L = Sp // dilation
    nb = L // ATTN_BLOCK
    span = window // dilation

    def to_sub(x):
        x = jnp.pad(x, ((0, 0), (0, pad), (0, 0), (0, 0)))
        return x.reshape(Bn, L, dilation, H, Dh).transpose(0, 2, 1, 3, 4)

    def band(x):
        x = jnp.pad(x, ((0, 0), (0, 0), (ATTN_BLOCK, 0), (0, 0), (0, 0)))
        x = x.reshape(Bn, dilation, nb + 1, ATTN_BLOCK, H, Dh)
        return jnp.concatenate([x[:, :, :-1], x[:, :, 1:]], axis=3)

    qb = to_sub(q).reshape(Bn, dilation, nb, ATTN_BLOCK, H, Dh)
    kb = band(to_sub(k))
    vb = band(to_sub(v))
    s = jnp.einsum('bdnqhe,bdnkhe->bdnhqk', qb, kb,
                   preferred_element_type=jnp.float32) * (Dh ** -0.5)
    qi = jnp.arange(ATTN_BLOCK)[:, None] + ATTN_BLOCK
    ki = jnp.arange(2 * ATTN_BLOCK)[None, :]
    steps = qi - ki
    band_ok = (steps >= 0) & (steps <= span)
    blk = jnp.arange(nb)[:, None, None]
    valid = band_ok[None] & ((blk > 0) | (ki >= ATTN_BLOCK)[None])
    dist = (steps * dilation).astype(jnp.float32)
    s = s - slopes[:, None, None] * dist
    s = jnp.where(valid[:, None], s, NEG)
    lse = jax.nn.logsumexp(s, axis=-1)
    p = jnp.exp(s - lse[..., None])
    o = jnp.einsum('bdnhqk,bdnkhe->bdnqhe', p.astype(v.dtype), vb)
    o = o.reshape(Bn, dilation, L, H, Dh).transpose(0, 2, 1, 3, 4).reshape(Bn, Sp, H, Dh)[:, :S]
    lse = lse.transpose(0, 1, 2, 4, 3).reshape(Bn, dilation, L, H)
    lse = lse.transpose(0, 2, 1, 3).reshape(Bn, Sp, H)[:, :S]
    return o, lse


def short_gated_conv(gb, gc, xt, conv_w):
    S = xt.shape[1]
    z = jnp.pad(gc * xt, ((0, 0), (CONV_W - 1, 0), (0, 0)))
    conv = conv_w[0] * z[:, 0:S]
    for j in range(1, CONV_W):
        conv = conv + conv_w[j] * z[:, j:j + S]
    return gb * conv


def even_layer(x, norm_g, w_in, pool_w, pool_scale, ws, bs, w_out):
    h = rmsnorm(x, norm_g)
    z = h @ w_in
    a, g_a, u, v, g_b = jnp.split(
        z, [A_WIDTH, 2 * A_WIDTH, 2 * A_WIDTH + B_WIDTH, 2 * A_WIDTH + 2 * B_WIDTH], axis=-1)
    y_a = multiscale_pool(a, pool_w, pool_scale) * jax.nn.silu(g_a)
    y_b = chunk_spatial_gate(u, v, ws, bs) * jax.nn.silu(g_b)
    return x + jnp.concatenate([y_a, y_b], axis=-1) @ w_out


def odd_layer(x, norm_g, w_in, conv_w, w_out):
    Bn, S, _ = x.shape
    h = rmsnorm(x, norm_g)
    z = h @ w_in
    o1 = QKV_WIDTH
    o2 = o1 + C_WIDTH
    qkv, g_c, d_b, d_c, d_x, g_d = jnp.split(
        z, [o1, o2, o2 + D_WIDTH, o2 + 2 * D_WIDTH, o2 + 3 * D_WIDTH], axis=-1)
    qkv = qkv.reshape(Bn, S, 3, N_DIL, C_SLOTS, HEAD_DIM)
    slopes = alibi_slopes(C_HEADS).reshape(N_DIL, C_SLOTS)
    outs, lses = [], []
    for gi, (window, dil) in enumerate(DILATED):
        o, l = dilated_group_attention(qkv[:, :, 0, gi], qkv[:, :, 1, gi], qkv[:, :, 2, gi],
                                       window, dil, slopes[gi])
        outs.append(o)
        lses.append(l)
    alpha = jax.nn.softmax(jnp.stack(lses, axis=0), axis=0)
    y_c = jnp.einsum('gbsh,gbshe->bshe', alpha.astype(x.dtype), jnp.stack(outs, axis=0))
    y_c = y_c.reshape(Bn, S, C_WIDTH) * jax.nn.silu(g_c)
    y_d = short_gated_conv(d_b, d_c, d_x, conv_w) * jax.nn.silu(g_d)
    return x + jnp.concatenate([y_c, y_d], axis=-1) @ w_out


def _fwd_setup_inputs(seed: int = 0) -> dict:
    key = jax.random.key(seed)
    ks = jax.random.split(key, 16)
    nrm = jax.random.normal
    f32 = jnp.float32
    return {
        "x": nrm(ks[0], (BATCH, SEQ, D_MODEL), f32),
        "even_norm": 1.0 + 0.05 * nrm(ks[1], (N_EVEN, D_MODEL), f32),
        "even_w_in": nrm(ks[2], (N_EVEN, D_MODEL, EVEN_IN), f32) * D_MODEL ** -0.5,
        "even_pool_w": nrm(ks[3], (N_EVEN, A_GROUPS, A_GROUP_DIM, A_GROUP_DIM), f32) * A_GROUP_DIM ** -0.5,
        "even_pool_scale": 1.0 + 0.1 * nrm(ks[4], (N_EVEN, A_WIDTH), f32),
        "even_ws": nrm(ks[5], (N_EVEN, B_GROUPS, CHUNK, CHUNK), f32) * CHUNK ** -0.5,
        "even_bs": 1.0 + 0.1 * nrm(ks[6], (N_EVEN, B_GROUPS, CHUNK), f32),
        "even_w_out": nrm(ks[7], (N_EVEN, D_INNER, D_MODEL), f32) * D_INNER ** -0.5,
        "odd_norm": 1.0 + 0.05 * nrm(ks[8], (N_ODD, D_MODEL), f32),
        "odd_w_in": nrm(ks[9], (N_ODD, D_MODEL, ODD_IN), f32) * D_MODEL ** -0.5,
        "odd_conv_w": nrm(ks[10], (N_ODD, CONV_W, D_WIDTH), f32) * CONV_W ** -0.5,
        "odd_w_out": nrm(ks[11], (N_ODD, D_INNER, D_MODEL), f32) * D_INNER ** -0.5,
        "final_norm": 1.0 + 0.05 * nrm(ks[12], (D_MODEL,), f32),
    }


def _fwd_reference(x, even_norm, even_w_in, even_pool_w, even_pool_scale, even_ws, even_bs,
              even_w_out, odd_norm, odd_w_in, odd_conv_w, odd_w_out, final_norm):
    h = x
    for layer in range(DEPTH):
        i = layer // 2
        if layer % 2 == 0:
            h = even_layer(h, even_norm[i], even_w_in[i], even_pool_w[i], even_pool_scale[i],
                           even_ws[i], even_bs[i], even_w_out[i])
        else:
            h = odd_layer(h, odd_norm[i], odd_w_in[i], odd_conv_w[i], odd_w_out[i])
    return rmsnorm(h, final_norm)


import jax as _jax
import jax.numpy as _jnp

TWIN_FORMAT = 'train_step'
FWD_PARAMS = ['x', 'even_norm', 'even_w_in', 'even_pool_w', 'even_pool_scale', 'even_ws', 'even_bs', 'even_w_out', 'odd_norm', 'odd_w_in', 'odd_conv_w', 'odd_w_out', 'final_norm']
TWIN_WEIGHTS = ['even_norm', 'even_w_in', 'even_pool_w', 'even_pool_scale', 'even_ws', 'even_bs', 'even_w_out', 'odd_norm', 'odd_w_in', 'odd_conv_w', 'odd_w_out', 'final_norm']
TWIN_DIFF_INPUT = 'x'
TWIN_INPUTS = ['x', 'even_norm', 'even_w_in', 'even_pool_w', 'even_pool_scale', 'even_ws', 'even_bs', 'even_w_out', 'odd_norm', 'odd_w_in', 'odd_conv_w', 'odd_w_out', 'final_norm', 'loss_target', 'm_even_norm', 'm_even_w_in', 'm_even_pool_w', 'm_even_pool_scale', 'm_even_ws', 'm_even_bs', 'm_even_w_out', 'm_odd_norm', 'm_odd_w_in', 'm_odd_conv_w', 'm_odd_w_out', 'm_final_norm', 'v_even_norm', 'v_even_w_in', 'v_even_pool_w', 'v_even_pool_scale', 'v_even_ws', 'v_even_bs', 'v_even_w_out', 'v_odd_norm', 'v_odd_w_in', 'v_odd_conv_w', 'v_odd_w_out', 'v_final_norm']
TWIN_OUTPUTS = ['loss', 'grad_x', 'grad_even_norm', 'grad_even_w_in', 'grad_even_pool_w', 'grad_even_pool_scale', 'grad_even_ws', 'grad_even_bs', 'grad_even_w_out', 'grad_odd_norm', 'grad_odd_w_in', 'grad_odd_conv_w', 'grad_odd_w_out', 'grad_final_norm', 'delta_even_norm', 'delta_even_w_in', 'delta_even_pool_w', 'delta_even_pool_scale', 'delta_even_ws', 'delta_even_bs', 'delta_even_w_out', 'delta_odd_norm', 'delta_odd_w_in', 'delta_odd_conv_w', 'delta_odd_w_out', 'delta_final_norm', 'new_m_even_norm', 'new_m_even_w_in', 'new_m_even_pool_w', 'new_m_even_pool_scale', 'new_m_even_ws', 'new_m_even_bs', 'new_m_even_w_out', 'new_m_odd_norm', 'new_m_odd_w_in', 'new_m_odd_conv_w', 'new_m_odd_w_out', 'new_m_final_norm', 'new_v_even_norm', 'new_v_even_w_in', 'new_v_even_pool_w', 'new_v_even_pool_scale', 'new_v_even_ws', 'new_v_even_bs', 'new_v_even_w_out', 'new_v_odd_norm', 'new_v_odd_w_in', 'new_v_odd_conv_w', 'new_v_odd_w_out', 'new_v_final_norm']
TWIN_LEAF_KINDS = {'loss': 'loss', 'grad_x': 'grad_x', 'grad_even_norm': 'grad_w', 'grad_even_w_in': 'grad_w', 'grad_even_pool_w': 'grad_w', 'grad_even_pool_scale': 'grad_w', 'grad_even_ws': 'grad_w', 'grad_even_bs': 'grad_w', 'grad_even_w_out': 'grad_w', 'grad_odd_norm': 'grad_w', 'grad_odd_w_in': 'grad_w', 'grad_odd_conv_w': 'grad_w', 'grad_odd_w_out': 'grad_w', 'grad_final_norm': 'grad_w', 'delta_even_norm': 'delta_w', 'delta_even_w_in': 'delta_w', 'delta_even_pool_w': 'delta_w', 'delta_even_pool_scale': 'delta_w', 'delta_even_ws': 'delta_w', 'delta_even_bs': 'delta_w', 'delta_even_w_out': 'delta_w', 'delta_odd_norm': 'delta_w', 'delta_odd_w_in': 'delta_w', 'delta_odd_conv_w': 'delta_w', 'delta_odd_w_out': 'delta_w', 'delta_final_norm': 'delta_w', 'new_m_even_norm': 'new_m', 'new_m_even_w_in': 'new_m', 'new_m_even_pool_w': 'new_m', 'new_m_even_pool_scale': 'new_m', 'new_m_even_ws': 'new_m', 'new_m_even_bs': 'new_m', 'new_m_even_w_out': 'new_m', 'new_m_odd_norm': 'new_m', 'new_m_odd_w_in': 'new_m', 'new_m_odd_conv_w': 'new_m', 'new_m_odd_w_out': 'new_m', 'new_m_final_norm': 'new_m', 'new_v_even_norm': 'new_v', 'new_v_even_w_in': 'new_v', 'new_v_even_pool_w': 'new_v', 'new_v_even_pool_scale': 'new_v', 'new_v_even_ws': 'new_v', 'new_v_even_bs': 'new_v', 'new_v_even_w_out': 'new_v', 'new_v_odd_norm': 'new_v', 'new_v_odd_w_in': 'new_v', 'new_v_odd_conv_w': 'new_v', 'new_v_odd_w_out': 'new_v', 'new_v_final_norm': 'new_v'}


def _forward(args):
    return _fwd_reference(*[args[k] for k in FWD_PARAMS])


def _output_shape():
    def fwd():
        inp = _fwd_setup_inputs(0)
        return _fwd_reference(*[inp[k] for k in FWD_PARAMS])
    out = _jax.eval_shape(fwd)
    return out.shape, out.dtype

N_MICROBATCH = 1
ADAM_LR = 0.001
ADAM_B1 = 0.9
ADAM_B2 = 0.999
ADAM_EPS = 1e-08
ADAM_WD = 0.01
ADAM_STEP = 10
PER_EXAMPLE_BATCH_AXIS = {'x': 0, 'loss_target': 0}
SHARED_INPUTS = []
_WEIGHT_DTYPES = {'even_norm': _jnp.float32, 'even_w_in': _jnp.float32, 'even_pool_w': _jnp.float32, 'even_pool_scale': _jnp.float32, 'even_ws': _jnp.float32, 'even_bs': _jnp.float32, 'even_w_out': _jnp.float32, 'odd_norm': _jnp.float32, 'odd_w_in': _jnp.float32, 'odd_conv_w': _jnp.float32, 'odd_w_out': _jnp.float32, 'final_norm': _jnp.float32}
MOMENT_SCALE = {'even_norm': 2.473276e-01, 'even_w_in': 1.074088e-01, 'even_pool_w': 9.338478e-02, 'even_pool_scale': 9.263726e-02, 'even_ws': 1.037396e-01, 'even_bs': 1.453866e-01, 'even_w_out': 1.593118e-01, 'odd_norm': 1.755205e-01, 'odd_w_in': 4.727812e-02, 'odd_conv_w': 8.740064e-02, 'odd_w_out': 8.812573e-02, 'final_norm': 6.421959e+01}


def _to_microbatches(a, axis):
    t = _jnp.moveaxis(a, axis, 0)
    t = t.reshape((N_MICROBATCH, t.shape[0] // N_MICROBATCH) + t.shape[1:])
    return _jnp.moveaxis(t, 1, axis + 1)


def setup_inputs(seed: int = 0) -> dict:
    inp = _fwd_setup_inputs(seed)
    key = _jax.random.fold_in(_jax.random.key(seed), 7919)
    shape, _ = _output_shape()
    out = dict(inp)
    out["loss_target"] = _jax.random.normal(_jax.random.fold_in(key, 0), shape, _jnp.float32)
    for i, name in enumerate(TWIN_WEIGHTS):
        w = inp[name].astype(_jnp.float32)
        if MOMENT_SCALE is None:
            s = _jnp.sqrt(_jnp.mean(_jnp.square(w)) + 1e-30)
        else:
            s = MOMENT_SCALE[name]
        km, kv = _jax.random.split(_jax.random.fold_in(key, i + 1))
        out[name] = w
        out["m_" + name] = s * _jax.random.normal(km, w.shape, _jnp.float32)
        out["v_" + name] = (s * s) * _jax.random.uniform(kv, w.shape, _jnp.float32, 0.5, 1.5)
    if N_MICROBATCH > 1:
        for name, axis in PER_EXAMPLE_BATCH_AXIS.items():
            out[name] = _to_microbatches(out[name], axis)
    return {'x': out['x'], 'even_norm': out['even_norm'], 'even_w_in': out['even_w_in'], 'even_pool_w': out['even_pool_w'], 'even_pool_scale': out['even_pool_scale'], 'even_ws': out['even_ws'], 'even_bs': out['even_bs'], 'even_w_out': out['even_w_out'], 'odd_norm': out['odd_norm'], 'odd_w_in': out['odd_w_in'], 'odd_conv_w': out['odd_conv_w'], 'odd_w_out': out['odd_w_out'], 'final_norm': out['final_norm'], 'loss_target': out['loss_target'], 'm_even_norm': out['m_even_norm'], 'm_even_w_in': out['m_even_w_in'], 'm_even_pool_w': out['m_even_pool_w'], 'm_even_pool_scale': out['m_even_pool_scale'], 'm_even_ws': out['m_even_ws'], 'm_even_bs': out['m_even_bs'], 'm_even_w_out': out['m_even_w_out'], 'm_odd_norm': out['m_odd_norm'], 'm_odd_w_in': out['m_odd_w_in'], 'm_odd_conv_w': out['m_odd_conv_w'], 'm_odd_w_out': out['m_odd_w_out'], 'm_final_norm': out['m_final_norm'], 'v_even_norm': out['v_even_norm'], 'v_even_w_in': out['v_even_w_in'], 'v_even_pool_w': out['v_even_pool_w'], 'v_even_pool_scale': out['v_even_pool_scale'], 'v_even_ws': out['v_even_ws'], 'v_even_bs': out['v_even_bs'], 'v_even_w_out': out['v_even_w_out'], 'v_odd_norm': out['v_odd_norm'], 'v_odd_w_in': out['v_odd_w_in'], 'v_odd_conv_w': out['v_odd_conv_w'], 'v_odd_w_out': out['v_odd_w_out'], 'v_final_norm': out['v_final_norm']}


def _loss(weights, diff, rest, loss_target):
    with _jax.named_scope("forward"):
        args = {**rest, TWIN_DIFF_INPUT: diff, **{k: w.astype(_WEIGHT_DTYPES[k]) for k, w in weights.items()}}
        y = _forward(args)
    with _jax.named_scope("loss_head"):
        err = _jnp.square(y.astype(_jnp.float32) - loss_target)
        return 0.5 * _jnp.sum(_jnp.mean(err, axis=-1)) if err.ndim else 0.5 * err


def _adamw(w, g, m, v):
    m = ADAM_B1 * m + (1.0 - ADAM_B1) * g
    v = ADAM_B2 * v + (1.0 - ADAM_B2) * _jnp.square(g)
    m_hat = m / (1.0 - ADAM_B1 ** ADAM_STEP)
    v_hat = v / (1.0 - ADAM_B2 ** ADAM_STEP)
    delta = -ADAM_LR * (m_hat / (_jnp.sqrt(v_hat) + ADAM_EPS) + ADAM_WD * w)
    return delta, m, v


def reference(x, even_norm, even_w_in, even_pool_w, even_pool_scale, even_ws, even_bs, even_w_out, odd_norm, odd_w_in, odd_conv_w, odd_w_out, final_norm, loss_target, m_even_norm, m_even_w_in, m_even_pool_w, m_even_pool_scale, m_even_ws, m_even_bs, m_even_w_out, m_odd_norm, m_odd_w_in, m_odd_conv_w, m_odd_w_out, m_final_norm, v_even_norm, v_even_w_in, v_even_pool_w, v_even_pool_scale, v_even_ws, v_even_bs, v_even_w_out, v_odd_norm, v_odd_w_in, v_odd_conv_w, v_odd_w_out, v_final_norm):
    given = dict(x=x, even_norm=even_norm, even_w_in=even_w_in, even_pool_w=even_pool_w, even_pool_scale=even_pool_scale, even_ws=even_ws, even_bs=even_bs, even_w_out=even_w_out, odd_norm=odd_norm, odd_w_in=odd_w_in, odd_conv_w=odd_conv_w, odd_w_out=odd_w_out, final_norm=final_norm, loss_target=loss_target, m_even_norm=m_even_norm, m_even_w_in=m_even_w_in, m_even_pool_w=m_even_pool_w, m_even_pool_scale=m_even_pool_scale, m_even_ws=m_even_ws, m_even_bs=m_even_bs, m_even_w_out=m_even_w_out, m_odd_norm=m_odd_norm, m_odd_w_in=m_odd_w_in, m_odd_conv_w=m_odd_conv_w, m_odd_w_out=m_odd_w_out, m_final_norm=m_final_norm, v_even_norm=v_even_norm, v_even_w_in=v_even_w_in, v_even_pool_w=v_even_pool_w, v_even_pool_scale=v_even_pool_scale, v_even_ws=v_even_ws, v_even_bs=v_even_bs, v_even_w_out=v_even_w_out, v_odd_norm=v_odd_norm, v_odd_w_in=v_odd_w_in, v_odd_conv_w=v_odd_conv_w, v_odd_w_out=v_odd_w_out, v_final_norm=v_final_norm)
    weights = {n: given[n] for n in TWIN_WEIGHTS}
    shared = {n: given[n] for n in SHARED_INPUTS}
    per_example = {n: given[n] for n in ['x']}
    grad_fn = _jax.value_and_grad(_loss, argnums=(0, 1))

    def one_microbatch(ex, loss_target):
        ex = dict(ex)
        diff = ex.pop(TWIN_DIFF_INPUT)
        return grad_fn(weights, diff, {**shared, **ex}, loss_target)

    if N_MICROBATCH == 1:
        loss, (grad_w, grad_x) = one_microbatch(per_example, given["loss_target"])
    else:
        def body(carry, xs):
            loss_sum, grad_sum = carry
            l_k, (gw_k, gx_k) = one_microbatch(xs[0], xs[1])
            with _jax.named_scope("update"):
                return (loss_sum + l_k, _jax.tree.map(_jnp.add, grad_sum, gw_k)), gx_k

        init = (_jnp.zeros((), _jnp.float32), _jax.tree.map(_jnp.zeros_like, weights))
        (loss, grad_w), grad_x = _jax.lax.scan(body, init, (per_example, given["loss_target"]))
    with _jax.named_scope("update"):
        delta_w, new_m, new_v = {}, {}, {}
        for n in TWIN_WEIGHTS:
            delta_w[n], new_m[n], new_v[n] = _adamw(weights[n], grad_w[n], given["m_" + n], given["v_" + n])
    return (loss, grad_x, *[grad_w[n] for n in TWIN_WEIGHTS], *[delta_w[n] for n in TWIN_WEIGHTS],
            *[new_m[n] for n in TWIN_WEIGHTS], *[new_v[n] for n in TWIN_WEIGHTS])
```

```python
import functools
from typing import NamedTuple

import jax
import jax.numpy as jnp
from jax import lax
from jax.experimental import pallas as pl
from jax.experimental.pallas import tpu as pltpu

F32, BF16 = jnp.float32, jnp.bfloat16

D_MODEL = 1024
EPS = 1e-6
NEG = -1e30
POOL_SIZES = (2, 4, 8, 16)
GROUP_W = 256
CHUNK = 128
DILATIONS = (1, 4, 16)
N_SLOTS = 8
HEAD_DIM = 128
ATTN_BLOCK = 128
SCALE = HEAD_DIM ** -0.5
EVEN_IN = 5120
ODD_IN = 14336
QKV_BLOCKS = 9
ODD_BLOCKS = ODD_IN // D_MODEL
SLOPES = tuple(tuple(2.0 ** (-8.0 * (g * N_SLOTS + s + 1) / (3 * N_SLOTS)) for s in range(N_SLOTS)) for g in range(3))

ADAM_LR, ADAM_B1, ADAM_B2, ADAM_EPS, ADAM_WD, ADAM_STEP = 0.001, 0.9, 0.999, 1e-08, 0.01, 10

HALO = 16
TS = 512
TM = 256
MIB = 1 << 20
MESH = pl.DeviceIdType.MESH
ANY = pl.BlockSpec(memory_space=pl.ANY)


def _cp(sem, vmem_mib):
    return pltpu.CompilerParams(dimension_semantics=sem, vmem_limit_bytes=vmem_mib * MIB)


def _sig(x):
    return 1.0 / (1.0 + jnp.exp(-x))


def _win_sum(e, w, forward):
    n = e.shape[0]
    k = 1
    while k < w:
        e = e + pltpu.roll(e, (n - k) if forward else k, 0)
        k *= 2
    return e


def _mm_nn(name, a, b, tm, tn, out_dtype, resid=None):
    m, k = a.shape
    n = b.shape[1]

    def body(*refs):
        a_ref, b_ref = refs[0], refs[1]
        acc = jnp.dot(a_ref[...].astype(BF16), b_ref[...], preferred_element_type=F32)
        if resid is not None:
            acc = acc + refs[2][...]
        refs[-1][...] = acc.astype(out_dtype)

    in_specs = [pl.BlockSpec((tm, k), lambda j, i: (i, 0)), pl.BlockSpec((k, tn), lambda j, i: (0, j))]
    args = [a, b]
    if resid is not None:
        in_specs.append(pl.BlockSpec((tm, tn), lambda j, i: (i, j)))
        args.append(resid)
    return pl.pallas_call(
        body, name=name, grid=(n // tn, m // tm), in_specs=in_specs,
        out_specs=pl.BlockSpec((tm, tn), lambda j, i: (i, j)),
        out_shape=jax.ShapeDtypeStruct((m, n), out_dtype),
        compiler_params=_cp(("parallel", "parallel"), 48),
    )(*args)


def _mm_nt(name, a, b, tm, tk, out_dtype):
    m, k = a.shape
    n = b.shape[0]
    nk = k // tk

    def body(a_ref, b_ref, o_ref, acc_ref):
        kk = pl.program_id(1)
        p = lax.dot_general(a_ref[...].astype(BF16), b_ref[...], (((1,), (1,)), ((), ())), preferred_element_type=F32)
        if nk == 1:
            o_ref[...] = p.astype(out_dtype)
        else:
            @pl.when(kk == 0)
            def _():
                acc_ref[...] = p

            @pl.when(kk > 0)
            def _():
                acc_ref[...] += p

            @pl.when(kk == nk - 1)
            def _():
                o_ref[...] = acc_ref[...].astype(out_dtype)

    return pl.pallas_call(
        body, name=name, grid=(m // tm, nk),
        in_specs=[pl.BlockSpec((tm, tk), lambda i, kk: (i, kk)), pl.BlockSpec((n, tk), lambda i, kk: (0, kk))],
        out_specs=pl.BlockSpec((tm, n), lambda i, kk: (i, 0)),
        out_shape=jax.ShapeDtypeStruct((m, n), out_dtype),
        scratch_shapes=[pltpu.VMEM((tm, n) if nk > 1 else (8, 128), F32)],
        compiler_params=_cp(("parallel", "arbitrary"), 56),
    )(a, b)


def _mm_tn(name, a, g, tn, ts):
    s, ka = a.shape
    n = g.shape[1]

    def body(a_ref, g_ref, o_ref):
        st = pl.program_id(1)
        p = lax.dot_general(a_ref[...], g_ref[...].astype(BF16), (((0,), (0,)), ((), ())), preferred_element_type=F32)

        @pl.when(st == 0)
        def _():
            o_ref[...] = p

        @pl.when(st > 0)
        def _():
            o_ref[...] += p

    return pl.pallas_call(
        body, name=name, grid=(n // tn, s // ts),
        in_specs=[pl.BlockSpec((ts, ka), lambda j, st: (st, 0)), pl.BlockSpec((ts, tn), lambda j, st: (st, j))],
        out_specs=pl.BlockSpec((ka, tn), lambda j, st: (0, j)),
        out_shape=jax.ShapeDtypeStruct((ka, n), F32),
        compiler_params=_cp(("parallel", "arbitrary"), 56),
    )(a, g)


def _rms_fwd(name, x, g):
    s = x.shape[0]

    def body(x_ref, g_ref, h_ref):
        xf = x_ref[...]
        r = lax.rsqrt(jnp.mean(xf * xf, axis=-1, keepdims=True) + EPS)
        h_ref[...] = (xf * r * g_ref[...]).astype(BF16)

    row = pl.BlockSpec((TS, D_MODEL), lambda i: (i, 0))
    return pl.pallas_call(
        body, name=name, grid=(s // TS,), in_specs=[row, pl.BlockSpec((1, D_MODEL), lambda i: (0, 0))], out_specs=row,
        out_shape=jax.ShapeDtypeStruct((s, D_MODEL), BF16), compiler_params=_cp(("parallel",), 32),
    )(x, g)


def _rms_bwd(name, dh, x, g, dres):
    s = x.shape[0]

    def body(dh_ref, x_ref, g_ref, dres_ref, dx_ref, dg_ref):
        xf = x_ref[...]
        r = lax.rsqrt(jnp.mean(xf * xf, axis=-1, keepdims=True) + EPS)
        xh = xf * r
        dhf = dh_ref[...]
        dxh = dhf * g_ref[...]
        dx_ref[...] = dres_ref[...] + r * (dxh - xh * jnp.mean(dxh * xh, axis=-1, keepdims=True))
        part = jnp.sum(dhf * xh, axis=0, keepdims=True)

        @pl.when(pl.program_id(0) == 0)
        def _():
            dg_ref[...] = part

        @pl.when(pl.program_id(0) > 0)
        def _():
            dg_ref[...] += part

    row = pl.BlockSpec((TS, D_MODEL), lambda i: (i, 0))
    vec = pl.BlockSpec((1, D_MODEL), lambda i: (0, 0))
    return pl.pallas_call(
        body, name=name, grid=(s // TS,), in_specs=[row, row, vec, row], out_specs=[row, vec],
        out_shape=[jax.ShapeDtypeStruct((s, D_MODEL), F32), jax.ShapeDtypeStruct((1, D_MODEL), F32)],
        compiler_params=_cp(("arbitrary",), 40),
    )(dh, x, g, dres)


def _final_loss(x, g, target):
    s = x.shape[0]

    def body(x_ref, g_ref, t_ref, dx_ref, loss_ref, dg_ref):
        xf = x_ref[...]
        gg = g_ref[...]
        r = lax.rsqrt(jnp.mean(xf * xf, axis=-1, keepdims=True) + EPS)
        xh = xf * r
        e = xh * gg - t_ref[...]
        dy = e * (1.0 / D_MODEL)
        dxh = dy * gg
        dx_ref[...] = r * (dxh - xh * jnp.mean(dxh * xh, axis=-1, keepdims=True))
        lpart = 0.5 * jnp.sum(jnp.mean(e * e, axis=-1, keepdims=True), axis=0, keepdims=True)
        lpart = jnp.broadcast_to(lpart, (8, 128))
        gpart = jnp.sum(dy * xh, axis=0, keepdims=True)

        @pl.when(pl.program_id(0) == 0)
        def _():
            loss_ref[...] = lpart
            dg_ref[...] = gpart

        @pl.when(pl.program_id(0) > 0)
        def _():
            loss_ref[...] += lpart
            dg_ref[...] += gpart

    row = pl.BlockSpec((TS, D_MODEL), lambda i: (i, 0))
    vec = pl.BlockSpec((1, D_MODEL), lambda i: (0, 0))
    return pl.pallas_call(
        body, name="final_loss", grid=(s // TS,), in_specs=[row, vec, row],
        out_specs=[row, pl.BlockSpec((8, 128), lambda i: (0, 0)), vec],
        out_shape=[jax.ShapeDtypeStruct((s, D_MODEL), F32), jax.ShapeDtypeStruct((8, 128), F32),
                   jax.ShapeDtypeStruct((1, D_MODEL), F32)],
        compiler_params=_cp(("arbitrary",), 40),
    )(x, g, target)


def _zcol(c, tm=TM):
    return pl.BlockSpec((tm, D_MODEL), lambda i, c=c: (i, c))


def _prev_halo(c, tm=TM):
    return pl.BlockSpec((HALO, D_MODEL), lambda i, c=c: (jnp.maximum(i * (tm // HALO) - 1, 0), c))


def _next_halo(c, n_rows, tm=TM):
    last = n_rows // HALO - 1
    return pl.BlockSpec((HALO, D_MODEL), lambda i, c=c: (jnp.minimum((i + 1) * (tm // HALO), last), c))


def _full(shape):
    return pl.BlockSpec(shape, lambda i: (0,) * len(shape))


def _inv_count(first_row, n, w):
    t = first_row + lax.broadcasted_iota(jnp.int32, (n, 1), 0)
    return 1.0 / jnp.minimum(t + 1, w).astype(F32)


def _even_mix_fwd(z, pw, ps, wt, bs):
    s = z.shape[0]

    def body(a_ref, ga_ref, u_ref, v_ref, gb_ref, halo_ref, pw_ref, ps_ref, wt_ref, bs_ref, y_ref):
        i = pl.program_id(0)
        a = a_ref[...].astype(F32)
        halo = jnp.where(i > 0, halo_ref[...].astype(F32), 0.0)
        ext = jnp.concatenate([halo, a], axis=0)
        ga = ga_ref[...].astype(F32)
        sga = ga * _sig(ga)
        for g, w in enumerate(POOL_SIZES):
            cs = slice(g * GROUP_W, (g + 1) * GROUP_W)
            win = _win_sum(ext[:, cs], w, False)[HALO:]
            pooled = win * _inv_count(i * TM, TM, w) - a[:, cs]
            mixed = jnp.dot(pooled.astype(BF16), pw_ref[g], preferred_element_type=F32)
            y_ref[:, cs] = (mixed * ps_ref[:, cs] * sga[:, cs]).astype(BF16)
        gb = gb_ref[...].astype(F32)
        gate = u_ref[...].astype(F32) * (gb * _sig(gb))
        for ch in range(TM // CHUNK):
            rs = slice(ch * CHUNK, (ch + 1) * CHUNK)
            for g in range(4):
                cs = slice(g * GROUP_W, (g + 1) * GROUP_W)
                mixb = jnp.dot(wt_ref[g], v_ref[rs, cs], preferred_element_type=F32) + bs_ref[g]
                y_ref[rs, D_MODEL + g * GROUP_W:D_MODEL + (g + 1) * GROUP_W] = (gate[rs, cs] * mixb).astype(BF16)

    return pl.pallas_call(
        body, name="even_mix_fwd", grid=(s // TM,),
        in_specs=[_zcol(0), _zcol(1), _zcol(2), _zcol(3), _zcol(4), _prev_halo(0),
                  _full((4, GROUP_W, GROUP_W)), _full((1, D_MODEL)), _full((4, CHUNK, CHUNK)), _full((4, CHUNK, 1))],
        out_specs=pl.BlockSpec((TM, 2 * D_MODEL), lambda i: (i, 0)),
        out_shape=jax.ShapeDtypeStruct((s, 2 * D_MODEL), BF16),
        compiler_params=_cp(("parallel",), 48),
    )(z, z, z, z, z, z, pw, ps, wt, bs)


def _even_mix_bwd(dy, z, pw, ps, wt, wtt, bs):
    s = z.shape[0]
    n_tiles = s // TM

    def body(dy_ref, a_ref, ga_ref, u_ref, v_ref, gb_ref, halo_ref, dyn_ref, gan_ref, pw_ref, ps_ref, wt_ref, wtt_ref,
             bs_ref, dz_ref, dpw_ref, dps_ref, dws_ref, dbs_ref):
        i = pl.program_id(0)

        @pl.when(i == 0)
        def _():
            dpw_ref[...] = jnp.zeros_like(dpw_ref)
            dps_ref[...] = jnp.zeros_like(dps_ref)
            dws_ref[...] = jnp.zeros_like(dws_ref)
            dbs_ref[...] = jnp.zeros_like(dbs_ref)

        a = a_ref[...].astype(F32)
        halo = jnp.where(i > 0, halo_ref[...].astype(F32), 0.0)
        ext = jnp.concatenate([halo, a], axis=0)
        ga = ga_ref[...].astype(F32)
        sg = _sig(ga)
        sga = ga * sg
        dsga = sg * (1.0 + ga * (1.0 - sg))
        dya = dy_ref[:, :D_MODEL].astype(F32)
        gan = gan_ref[...].astype(F32)
        dmn_all = jnp.where(i < n_tiles - 1, dyn_ref[...].astype(F32) * ps_ref[...] * (gan * _sig(gan)), 0.0)
        for g, w in enumerate(POOL_SIZES):
            cs = slice(g * GROUP_W, (g + 1) * GROUP_W)
            inv = _inv_count(i * TM, TM, w)
            pooled = _win_sum(ext[:, cs], w, False)[HALO:] * inv - a[:, cs]
            pb = pooled.astype(BF16)
            mixed = jnp.dot(pb, pw_ref[g], preferred_element_type=F32)
            dyg = dya[:, cs]
            psg = ps_ref[:, cs]
            dm = (dyg * psg * sga[:, cs]).astype(BF16)
            dz_ref[:, D_MODEL + g * GROUP_W:D_MODEL + (g + 1) * GROUP_W] = (dyg * mixed * psg * dsga[:, cs]).astype(BF16)
            dps_ref[:, cs] += jnp.sum(dyg * mixed * sga[:, cs], axis=0, keepdims=True)
            dpw_ref[g] += lax.dot_general(pb, dm, (((0,), (0,)), ((), ())), preferred_element_type=F32)
            nt = (((1,), (1,)), ((), ()))
            dpool = lax.dot_general(dm, pw_ref[g], nt, preferred_element_type=F32)
            dpool_n = lax.dot_general(dmn_all[:, cs].astype(BF16), pw_ref[g], nt, preferred_element_type=F32)
            e = jnp.concatenate([dpool * inv, dpool_n * _inv_count((i + 1) * TM, HALO, w)], axis=0)
            dz_ref[:, cs] = (_win_sum(e, w, True)[:TM] - dpool).astype(BF16)

        gb = gb_ref[...].astype(F32)
        sg = _sig(gb)
        sgb = gb * sg
        dsgb = sg * (1.0 + gb * (1.0 - sg))
        u = u_ref[...].astype(F32)
        dyb = dy_ref[:, D_MODEL:].astype(F32)
        tril = lax.broadcasted_iota(jnp.int32, (CHUNK, CHUNK), 0) >= lax.broadcasted_iota(jnp.int32, (CHUNK, CHUNK), 1)
        lane = lax.broadcasted_iota(jnp.int32, (CHUNK, 128), 1)
        for ch in range(TM // CHUNK):
            rs = slice(ch * CHUNK, (ch + 1) * CHUNK)
            for g in range(4):
                cs = slice(g * GROUP_W, (g + 1) * GROUP_W)
                vb = v_ref[rs, cs]
                mixb = jnp.dot(wt_ref[g], vb, preferred_element_type=F32) + bs_ref[g]
                dyu = dyb[rs, cs] * u[rs, cs]
                dmix = dyu * sgb[rs, cs]
                dmb = dmix.astype(BF16)
                o = g * GROUP_W
                dz_ref[rs, 2 * D_MODEL + o:2 * D_MODEL + o + GROUP_W] = (dyb[rs, cs] * mixb * sgb[rs, cs]).astype(BF16)
                dz_ref[rs, 3 * D_MODEL + o:3 * D_MODEL + o + GROUP_W] = jnp.dot(
                    wtt_ref[g], dmb, preferred_element_type=F32).astype(BF16)
                dz_ref[rs, 4 * D_MODEL + o:4 * D_MODEL + o + GROUP_W] = (dyu * mixb * dsgb[rs, cs]).astype(BF16)
                dws = lax.dot_general(dmb, vb, (((1,), (1,)), ((), ())), preferred_element_type=F32)
                dws_ref[g] += jnp.where(tril, dws, 0.0)
                dbs_ref[...] += jnp.where(lane == g, jnp.sum(dmix, axis=1, keepdims=True), 0.0)

    return pl.pallas_call(
        body, name="even_mix_bwd", grid=(n_tiles,),
        in_specs=[pl.BlockSpec((TM, 2 * D_MODEL), lambda i: (i, 0)), _zcol(0), _zcol(1), _zcol(2), _zcol(3), _zcol(4),
                  _prev_halo(0), _next_halo(0, s), _next_halo(1, s),
                  _full((4, GROUP_W, GROUP_W)), _full((1, D_MODEL)), _full((4, CHUNK, CHUNK)), _full((4, CHUNK, CHUNK)),
                  _full((4, CHUNK, 1))],
        out_specs=[pl.BlockSpec((TM, EVEN_IN), lambda i: (i, 0)), _full((4, GROUP_W, GROUP_W)), _full((1, D_MODEL)),
                   _full((4, CHUNK, CHUNK)), _full((CHUNK, 128))],
        out_shape=[jax.ShapeDtypeStruct((s, EVEN_IN), BF16), jax.ShapeDtypeStruct((4, GROUP_W, GROUP_W), F32),
                   jax.ShapeDtypeStruct((1, D_MODEL), F32), jax.ShapeDtypeStruct((4, CHUNK, CHUNK), F32),
                   jax.ShapeDtypeStruct((CHUNK, 128), F32)],
        compiler_params=_cp(("arbitrary",), 56),
    )(dy, z, z, z, z, z, z, dy, z, pw, ps, wt, wtt, bs)


def _band():
    row = lax.broadcasted_iota(jnp.int32, (ATTN_BLOCK, 2 * ATTN_BLOCK), 0)
    col = lax.broadcasted_iota(jnp.int32, (ATTN_BLOCK, 2 * ATTN_BLOCK), 1)
    steps = row + ATTN_BLOCK - col
    return steps, col


def _qkv_specs(gi):
    def spec(which, prev):
        cb = which * 3 + gi
        if prev:
            return pl.BlockSpec((ATTN_BLOCK, D_MODEL), lambda r, i: (jnp.maximum(i - 1, 0), r * ODD_BLOCKS + cb))
        return pl.BlockSpec((ATTN_BLOCK, D_MODEL), lambda r, i: (i, r * ODD_BLOCKS + cb))
    return spec


def _attn_fwd(z, gi):
    s = z.shape[0]
    d = DILATIONS[gi]
    length = s // d
    nb = length // ATTN_BLOCK
    spec = _qkv_specs(gi)

    def body(q_ref, kp_ref, kc_ref, vp_ref, vc_ref, o_ref, lse_ref):
        i = pl.program_id(1)
        steps, col = _band()
        valid = (steps >= 0) & (steps <= ATTN_BLOCK) & ((i > 0) | (col >= ATTN_BLOCK))
        negdist = -(steps * d).astype(F32)
        for h in range(N_SLOTS):
            sl = slice(h * HEAD_DIM, (h + 1) * HEAD_DIM)
            k = jnp.concatenate([kp_ref[:, sl], kc_ref[:, sl]], axis=0)
            v = jnp.concatenate([vp_ref[:, sl], vc_ref[:, sl]], axis=0)
            sc = lax.dot_general(q_ref[:, sl], k, (((1,), (1,)), ((), ())), preferred_element_type=F32) * SCALE
            sc = jnp.where(valid, sc + SLOPES[gi][h] * negdist, NEG)
            m = jnp.max(sc, axis=-1, keepdims=True)
            p = jnp.exp(sc - m)
            l = jnp.sum(p, axis=-1, keepdims=True)
            o = jnp.dot((p / l).astype(BF16), v, preferred_element_type=F32)
            o_ref[:, sl] = o.astype(BF16)
            lse_ref[:, sl] = jnp.broadcast_to(m + jnp.log(l), (ATTN_BLOCK, HEAD_DIM))

    out = pl.BlockSpec((ATTN_BLOCK, D_MODEL), lambda r, i: (i, r))
    o, lse = pl.pallas_call(
        body, name=f"attn_fwd_d{d}", grid=(d, nb),
        in_specs=[spec(0, False), spec(1, True), spec(1, False), spec(2, True), spec(2, False)],
        out_specs=[out, out],
        out_shape=[jax.ShapeDtypeStruct((length, d * D_MODEL), BF16), jax.ShapeDtypeStruct((length, d * D_MODEL), F32)],
        compiler_params=_cp(("parallel", "parallel"), 32),
    )(*([z.reshape(length, d * ODD_IN)] * 5))
    return o.reshape(s, D_MODEL), lse.reshape(s, D_MODEL)


def _attn_bwd(z, dyc, ltot, dst, dz, gi):
    s = z.shape[0]
    d = DILATIONS[gi]
    length = s // d
    nb = length // ATTN_BLOCK
    n_steps = d * nb

    def rev(col_of_r, prev=False):
        if prev:
            return pl.BlockSpec((ATTN_BLOCK, D_MODEL), lambda r, n: (jnp.maximum(nb - 2 - n, 0), col_of_r(r)))
        return pl.BlockSpec((ATTN_BLOCK, D_MODEL), lambda r, n: (nb - 1 - n, col_of_r(r)))

    def zc(which):
        return lambda r: r * ODD_BLOCKS + which * 3 + gi

    def body(q_ref, kp_ref, kc_ref, vp_ref, vc_ref, dy_ref, l_ref, d_ref, dz_in, dz_out, dq_s, dk_s, dv_s, ck_s, cv_s, sems):
        del dz_in
        r = pl.program_id(0)
        n = pl.program_id(1)
        i = nb - 1 - n
        step = r * nb + n

        def out_copy(src, which):
            rows = pl.ds(pl.multiple_of(i * ATTN_BLOCK, ATTN_BLOCK), ATTN_BLOCK)
            cols = pl.ds(pl.multiple_of((r * ODD_BLOCKS + which * 3 + gi) * D_MODEL, D_MODEL), D_MODEL)
            return pltpu.make_async_copy(src, dz_out.at[rows, cols], sems.at[which])

        copies = [out_copy(dq_s, 0), out_copy(dk_s, 1), out_copy(dv_s, 2)]

        @pl.when(step > 0)
        def _():
            for cp in copies:
                cp.wait()

        @pl.when(n == 0)
        def _():
            ck_s[...] = jnp.zeros_like(ck_s)
            cv_s[...] = jnp.zeros_like(cv_s)

        steps, col = _band()
        valid = (steps >= 0) & (steps <= ATTN_BLOCK) & ((i > 0) | (col >= ATTN_BLOCK))
        negdist = -(steps * d).astype(F32)
        nt = (((1,), (1,)), ((), ()))
        tn = (((0,), (0,)), ((), ()))
        for h in range(N_SLOTS):
            sl = slice(h * HEAD_DIM, (h + 1) * HEAD_DIM)
            q = q_ref[:, sl]
            k = jnp.concatenate([kp_ref[:, sl], kc_ref[:, sl]], axis=0)
            v = jnp.concatenate([vp_ref[:, sl], vc_ref[:, sl]], axis=0)
            dy = dy_ref[:, sl]
            sc = lax.dot_general(q, k, nt, preferred_element_type=F32) * SCALE + SLOPES[gi][h] * negdist
            lt = l_ref[:, sl]
            p = jnp.where(valid, jnp.exp(sc - jnp.concatenate([lt, lt], axis=1)), 0.0)
            dp = lax.dot_general(dy, v, nt, preferred_element_type=F32)
            dd = d_ref[:, sl]
            ds = (p * (dp - jnp.concatenate([dd, dd], axis=1))).astype(BF16)
            dq_s[:, sl] = (jnp.dot(ds, k, preferred_element_type=F32) * SCALE).astype(BF16)
            dk = lax.dot_general(ds, q, tn, preferred_element_type=F32) * SCALE
            dv = lax.dot_general(p.astype(BF16), dy, tn, preferred_element_type=F32)
            dk_s[:, sl] = (ck_s[:, sl] + dk[ATTN_BLOCK:]).astype(BF16)
            dv_s[:, sl] = (cv_s[:, sl] + dv[ATTN_BLOCK:]).astype(BF16)
            ck_s[:, sl] = dk[:ATTN_BLOCK]
            cv_s[:, sl] = dv[:ATTN_BLOCK]

        for cp in copies:
            cp.start()

        @pl.when(step == n_steps - 1)
        def _():
            for cp in copies:
                cp.wait()

    stage = pltpu.VMEM((ATTN_BLOCK, D_MODEL), BF16)
    carry = pltpu.VMEM((ATTN_BLOCK, D_MODEL), F32)
    zv = z.reshape(length, d * ODD_IN)
    out = pl.pallas_call(
        body, name=f"attn_bwd_d{d}", grid=(d, nb),
        in_specs=[rev(zc(0)), rev(zc(1), True), rev(zc(1)), rev(zc(2), True), rev(zc(2)),
                  rev(lambda r: r), rev(lambda r: r), rev(lambda r: r), ANY],
        out_specs=ANY,
        out_shape=jax.ShapeDtypeStruct((length, d * ODD_IN), BF16),
        scratch_shapes=[stage, stage, stage, carry, carry, pltpu.SemaphoreType.DMA((3,))],
        input_output_aliases={8: 0},
        compiler_params=_cp(("arbitrary", "arbitrary"), 32),
    )(zv, zv, zv, zv, zv, dyc.reshape(length, d * D_MODEL), ltot.reshape(length, d * D_MODEL),
      dst.reshape(length, d * D_MODEL), dz.reshape(length, d * ODD_IN))
    return out.reshape(s, ODD_IN)


def _odd_mix_fwd(z, os_, lses, cw):
    s = z.shape[0]

    def body(o0, o1, o2, l0, l1, l2, gc_ref, db_ref, dc_ref, dx_ref, gd_ref, hc_ref, hx_ref, cw_ref, y_ref, yc_ref, lt_ref):
        i = pl.program_id(0)
        ls = [l0[...], l1[...], l2[...]]
        lmax = jnp.maximum(jnp.maximum(ls[0], ls[1]), ls[2])
        es = [jnp.exp(l - lmax) for l in ls]
        den = es[0] + es[1] + es[2]
        yc = (es[0] * o0[...].astype(F32) + es[1] * o1[...].astype(F32) + es[2] * o2[...].astype(F32)) / den
        lt_ref[...] = lmax + jnp.log(den)
        yc_ref[...] = yc.astype(BF16)
        gc = gc_ref[...].astype(F32)
        y_ref[:, :D_MODEL] = (yc * (gc * _sig(gc))).astype(BF16)

        zc = dc_ref[...].astype(F32) * dx_ref[...].astype(F32)
        halo = jnp.where(i > 0, hc_ref[...].astype(F32) * hx_ref[...].astype(F32), 0.0)
        ext = jnp.concatenate([halo, zc], axis=0)
        z1 = pltpu.roll(ext, 1, 0)[HALO:]
        z2 = pltpu.roll(ext, 2, 0)[HALO:]
        conv = cw_ref[0:1, :] * z2 + cw_ref[1:2, :] * z1 + cw_ref[2:3, :] * zc
        gd = gd_ref[...].astype(F32)
        y_ref[:, D_MODEL:] = (db_ref[...].astype(F32) * conv * (gd * _sig(gd))).astype(BF16)

    row = pl.BlockSpec((TM, D_MODEL), lambda i: (i, 0))
    return pl.pallas_call(
        body, name="odd_mix_fwd", grid=(s // TM,),
        in_specs=[row] * 6 + [_zcol(9), _zcol(10), _zcol(11), _zcol(12), _zcol(13), _prev_halo(11), _prev_halo(12),
                              _full((3, D_MODEL))],
        out_specs=[pl.BlockSpec((TM, 2 * D_MODEL), lambda i: (i, 0)), row, row],
        out_shape=[jax.ShapeDtypeStruct((s, 2 * D_MODEL), BF16), jax.ShapeDtypeStruct((s, D_MODEL), BF16),
                   jax.ShapeDtypeStruct((s, D_MODEL), F32)],
        compiler_params=_cp(("parallel",), 48),
    )(*os_, *lses, z, z, z, z, z, z, z, cw)


def _odd_mix_bwd(dy, z, ycr, cw):
    s = z.shape[0]
    n_tiles = s // TM
    rest = ODD_IN - QKV_BLOCKS * D_MODEL

    def body(dy_ref, yc_ref, gc_ref, db_ref, dc_ref, dx_ref, gd_ref, hc_ref, hx_ref, dyn_ref, dbn_ref, gdn_ref, cw_ref,
             dz_ref, dyc_ref, dd_ref, dcw_ref, stage, sem):
        i = pl.program_id(0)
        out = pltpu.make_async_copy(
            stage, dz_ref.at[pl.ds(pl.multiple_of(i * TM, TM), TM), pl.ds(QKV_BLOCKS * D_MODEL, rest)], sem)

        @pl.when(i > 0)
        def _():
            out.wait()

        dyc_in = dy_ref[:, :D_MODEL].astype(F32)
        gc = gc_ref[...].astype(F32)
        sg = _sig(gc)
        yc = yc_ref[...].astype(F32)
        dyc = dyc_in * (gc * sg)
        dyc_ref[...] = dyc.astype(BF16)
        stage[:, 0:D_MODEL] = (dyc_in * yc * (sg * (1.0 + gc * (1.0 - sg)))).astype(BF16)
        prod = dyc * yc
        for h in range(N_SLOTS):
            sl = slice(h * HEAD_DIM, (h + 1) * HEAD_DIM)
            dd_ref[:, sl] = jnp.broadcast_to(jnp.sum(prod[:, sl], axis=-1, keepdims=True), (TM, HEAD_DIM))

        dc = dc_ref[...].astype(F32)
        dx = dx_ref[...].astype(F32)
        zc = dc * dx
        halo = jnp.where(i > 0, hc_ref[...].astype(F32) * hx_ref[...].astype(F32), 0.0)
        ext = jnp.concatenate([halo, zc], axis=0)
        z1 = pltpu.roll(ext, 1, 0)[HALO:]
        z2 = pltpu.roll(ext, 2, 0)[HALO:]
        w0, w1, w2 = cw_ref[0:1, :], cw_ref[1:2, :], cw_ref[2:3, :]
        conv = w0 * z2 + w1 * z1 + w2 * zc
        gd = gd_ref[...].astype(F32)
        sg = _sig(gd)
        sgd = gd * sg
        db = db_ref[...].astype(F32)
        dyd = dy_ref[:, D_MODEL:].astype(F32)
        dconv = dyd * db * sgd
        gdn = gdn_ref[...].astype(F32)
        dconv_n = jnp.where(i < n_tiles - 1, dyn_ref[...].astype(F32) * dbn_ref[...].astype(F32) * (gdn * _sig(gdn)), 0.0)
        extn = jnp.concatenate([dconv, dconv_n], axis=0)
        nrow = TM + HALO
        dzc = w2 * dconv + w1 * pltpu.roll(extn, nrow - 1, 0)[:TM] + w0 * pltpu.roll(extn, nrow - 2, 0)[:TM]
        stage[:, D_MODEL:2 * D_MODEL] = (dyd * conv * sgd).astype(BF16)
        stage[:, 2 * D_MODEL:3 * D_MODEL] = (dzc * dx).astype(BF16)
        stage[:, 3 * D_MODEL:4 * D_MODEL] = (dzc * dc).astype(BF16)
        stage[:, 4 * D_MODEL:5 * D_MODEL] = (dyd * db * conv * (sg * (1.0 + gd * (1.0 - sg)))).astype(BF16)
        @pl.when(i == 0)
        def _():
            dcw_ref[...] = jnp.zeros_like(dcw_ref)

        for tap, shifted in enumerate((z2, z1, zc)):
            dcw_ref[tap:tap + 1, :] += jnp.sum(dconv * shifted, axis=0, keepdims=True)

        out.start()

        @pl.when(i == n_tiles - 1)
        def _():
            out.wait()

    row = pl.BlockSpec((TM, D_MODEL), lambda i: (i, 0))
    return pl.pallas_call(
        body, name="odd_mix_bwd", grid=(n_tiles,),
        in_specs=[pl.BlockSpec((TM, 2 * D_MODEL), lambda i: (i, 0)), row, _zcol(9), _zcol(10), _zcol(11), _zcol(12), _zcol(13),
                  _prev_halo(11), _prev_halo(12), _next_halo(1, s), _next_halo(10, s), _next_halo(13, s), _full((3, D_MODEL))],
        out_specs=[ANY, row, row, _full((3, D_MODEL))],
        out_shape=[jax.ShapeDtypeStruct((s, ODD_IN), BF16), jax.ShapeDtypeStruct((s, D_MODEL), BF16),
                   jax.ShapeDtypeStruct((s, D_MODEL), F32), jax.ShapeDtypeStruct((3, D_MODEL), F32)],
        scratch_shapes=[pltpu.VMEM((TM, rest), BF16), pltpu.SemaphoreType.DMA(())],
        compiler_params=_cp(("arbitrary",), 48),
    )(dy, ycr, z, z, z, z, z, z, z, dy, z, z, cw)


def _local_step(x, target, w):
    tril = jnp.tril(jnp.ones((CHUNK, CHUNK), bool))
    wt = jnp.where(tril[None], w["ws"], 0.0).astype(BF16)
    wtt = jnp.swapaxes(wt, 1, 2)
    bs = w["bs"].reshape(4, CHUNK, 1)

    h_e = _rms_fwd("rms_fwd_even", x, w["even_norm"])
    z_e = _mm_nn("even_in_proj", h_e, w["w_in_e"], 512, 1280, BF16)
    y_e = _even_mix_fwd(z_e, w["pool_w"], w["pool_scale"], wt, bs)
    x1 = _mm_nn("even_out_proj", y_e, w["w_out_e"], 512, 1024, F32, resid=x)
    h_o = _rms_fwd("rms_fwd_odd", x1, w["odd_norm"])
    z_o = _mm_nn("odd_in_proj", h_o, w["w_in_o"], 512, 1792, BF16)
    att = [_attn_fwd(z_o, gi) for gi in range(3)]
    y_o, ycr, ltot = _odd_mix_fwd(z_o, [a[0] for a in att], [a[1] for a in att], w["conv_w"])
    x2 = _mm_nn("odd_out_proj", y_o, w["w_out_o"], 512, 1024, F32, resid=x1)
    dx2, loss8, g_final = _final_loss(x2, w["final_norm"], target)

    g_w_out_o = _mm_tn("odd_out_proj_dw", y_o, dx2, 1024, 512)
    dy_o = _mm_nt("odd_out_proj_dy", dx2, w["w_out_o"], 512, 1024, BF16)
    dz_o, dyc, dst, g_conv = _odd_mix_bwd(dy_o, z_o, ycr, w["conv_w"])
    for gi in range(3):
        dz_o = _attn_bwd(z_o, dyc, ltot, dst, dz_o, gi)
    g_w_in_o = _mm_tn("odd_in_proj_dw", h_o, dz_o, 1792, 512)
    dh_o = _mm_nt("odd_in_proj_dh", dz_o, w["w_in_o"], 1024, 1792, F32)
    dx1, g_odd_norm = _rms_bwd("rms_bwd_odd", dh_o, x1, w["odd_norm"], dx2)
    g_w_out_e = _mm_tn("even_out_proj_dw", y_e, dx1, 1024, 512)
    dy_e = _mm_nt("even_out_proj_dy", dx1, w["w_out_e"], 512, 1024, BF16)
    dz_e, g_pw, g_ps, g_ws, g_bs = _even_mix_bwd(dy_e, z_e, w["pool_w"], w["pool_scale"], wt, wtt, bs)
    g_w_in_e = _mm_tn("even_in_proj_dw", h_e, dz_e, 1280, 512)
    dh_e = _mm_nt("even_in_proj_dh", dz_e, w["w_in_e"], 1024, 1280, F32)
    dx0, g_even_norm = _rms_bwd("rms_bwd_even", dh_e, x, w["even_norm"], dx1)

    grads = dict(w_in_e=g_w_in_e, pool_w=g_pw, w_out_e=g_w_out_e, w_in_o=g_w_in_o, w_out_o=g_w_out_o,
                 even_norm=g_even_norm, pool_scale=g_ps, ws=g_ws, bs=g_bs[:, :4].T, final_norm=g_final,
                 odd_norm=g_odd_norm, conv_w=g_conv)
    return loss8[0, 0], dx0, grads


class _Big(NamedTuple):
    name: str
    full: tuple
    haxis: int
    kaxis: int
    sub: int


BIGS = (
    _Big("w_in_e", (1024, 5120), 0, 1, 2),
    _Big("pool_w", (4, 256, 256), 0, 1, 1),
    _Big("w_out_e", (2048, 1024), 1, 0, 1),
    _Big("w_in_o", (1024, 14336), 0, 1, 4),
    _Big("w_out_o", (2048, 1024), 1, 0, 1),
)
N_BIG = len(BIGS)


def _shape(b, half=False, shard=False):
    return tuple(n // (2 if (half and ax == b.haxis) else 1) // (4 if (shard and ax == b.kaxis) else 1)
                 for ax, n in enumerate(b.full))


def _at(ref, b, h=None, k=None):
    idx = []
    for ax, n in enumerate(b.full):
        if ax == b.haxis and h is not None:
            idx.append(pl.ds(h * (n // 2), n // 2))
        elif ax == b.kaxis and k is not None:
            idx.append(pl.ds(k * (n // 4), n // 4))
        else:
            idx.append(slice(None))
    return ref.at[tuple(idx)]


def _place():
    x, y, c = lax.axis_index("x"), lax.axis_index("y"), lax.axis_index("c")
    chips = [(1 - x, y), (x, 1 - y), (1 - x, 1 - y)]
    return x, y, c, 2 * x + y, chips, [2 * cx + cy for cx, cy in chips]


def _gather_weights(shards, tiny):
    def body(*refs):
        ins, tiny_in = refs[:N_BIG], refs[N_BIG]
        outs, tiny_out = refs[N_BIG + 1:2 * N_BIG + 1], refs[2 * N_BIG + 1]
        send, recv, loc = refs[2 * N_BIG + 2:]
        x, y, c, k_me, chips, ks = _place()
        sib = (x, y, 1 - c)

        def rc(src, dst, sem, to):
            return pltpu.make_async_remote_copy(src_ref=src, dst_ref=dst, send_sem=send.at[sem], recv_sem=recv.at[sem],
                                                device_id=to, device_id_type=MESH)

        started = []
        for a, b in enumerate(BIGS):
            own = pltpu.make_async_copy(ins[a], _at(outs[a], b, k=k_me), loc.at[a])
            own.start()
            started.append(own)
        own = pltpu.make_async_copy(tiny_in, tiny_out.at[k_me], loc.at[N_BIG])
        own.start()
        started.append(own)
        sends = []
        for j, chip in enumerate(chips):
            for a, b in enumerate(BIGS):
                sends.append(rc(_at(ins[a], b, h=c), _at(outs[a], b, h=c, k=k_me), 6 * a + j, (*chip, c)))
            sends.append(rc(tiny_in, tiny_out.at[k_me], 6 * N_BIG + j, (*chip, c)))
        for cp in sends:
            cp.start()
        for j in range(3):
            for a, b in enumerate(BIGS):
                win = _at(outs[a], b, h=c, k=ks[j])
                rc(win, win, 6 * a + j, sib).wait_recv()
                fwd = rc(win, win, 6 * a + 3 + j, sib)
                fwd.start()
                sends.append(fwd)
            rc(tiny_in, tiny_out.at[ks[j]], 6 * N_BIG + j, sib).wait_recv()
        for j in range(3):
            for a, b in enumerate(BIGS):
                win = _at(outs[a], b, h=1 - c, k=ks[j])
                rc(win, win, 6 * a + 3 + j, sib).wait_recv()
        for cp in sends:
            cp.wait_send()
        for cp in started:
            cp.wait()

    n_sem = 6 * N_BIG + 3
    return pl.pallas_call(
        body, name="gather_weights",
        in_specs=[ANY] * (N_BIG + 1), out_specs=[ANY] * (N_BIG + 1),
        out_shape=[jax.ShapeDtypeStruct(b.full, BF16) for b in BIGS] + [jax.ShapeDtypeStruct((4,) + tiny.shape, F32)],
        scratch_shapes=[pltpu.SemaphoreType.DMA((n_sem,)), pltpu.SemaphoreType.DMA((n_sem,)),
                        pltpu.SemaphoreType.DMA((N_BIG + 1,))],
    )(*shards, tiny)


def _swap_halves(grads):
    def body(*refs):
        ins, outs, send, recv = refs[:N_BIG], refs[N_BIG:2 * N_BIG], refs[2 * N_BIG], refs[2 * N_BIG + 1]
        x, y, c, _, _, _ = _place()
        copies = [pltpu.make_async_remote_copy(src_ref=_at(ins[a], b, h=1 - c), dst_ref=outs[a], send_sem=send.at[a],
                                               recv_sem=recv.at[a], device_id=(x, y, 1 - c), device_id_type=MESH)
                  for a, b in enumerate(BIGS)]
        for cp in copies:
            cp.start()
        for cp in copies:
            cp.wait()

    return pl.pallas_call(
        body, name="rs_swap_halves", in_specs=[ANY] * N_BIG, out_specs=[ANY] * N_BIG,
        out_shape=[jax.ShapeDtypeStruct(_shape(b, half=True), F32) for b in BIGS],
        scratch_shapes=[pltpu.SemaphoreType.DMA((N_BIG,)), pltpu.SemaphoreType.DMA((N_BIG,))],
    )(*grads)


def _blk(b):
    win = _shape(b, half=True, shard=True)
    return (win[0] // b.sub,) + win[1:]


def _bidx(b, h, k, st):
    idx = [0] * len(b.full)
    idx[b.haxis] = h
    idx[b.kaxis] = k
    idx[0] = idx[0] * b.sub + st
    return tuple(idx)


def _chip_sum(b, g, got, c_arr):
    blk = _blk(b)

    def body(c_ref, g_ref, r_ref, o_ref):
        del c_ref
        o_ref[...] = (g_ref[...] + r_ref[...]).astype(BF16)

    half = pl.BlockSpec(blk, lambda k, st, c_ref: _bidx(b, 0, k, st))
    return pl.pallas_call(
        body, name=f"rs_chip_sum_{b.name}",
        grid_spec=pltpu.PrefetchScalarGridSpec(
            num_scalar_prefetch=1, grid=(4, b.sub),
            in_specs=[pl.BlockSpec(blk, lambda k, st, c_ref: _bidx(b, c_ref[0], k, st)), half], out_specs=half),
        out_shape=jax.ShapeDtypeStruct(_shape(b, half=True), BF16),
        compiler_params=_cp(("arbitrary", "arbitrary"), 40),
    )(c_arr, g, got)


def _send_partials(sums):
    def body(*refs):
        ins, outs, send, recv = refs[:N_BIG], refs[N_BIG:2 * N_BIG], refs[2 * N_BIG], refs[2 * N_BIG + 1]
        _, _, c, _, chips, ks = _place()
        copies = [pltpu.make_async_remote_copy(src_ref=_at(ins[a], b, k=ks[j]), dst_ref=outs[a].at[j],
                                               send_sem=send.at[3 * a + j], recv_sem=recv.at[3 * a + j],
                                               device_id=(*chips[j], c), device_id_type=MESH)
                  for j in range(3) for a, b in enumerate(BIGS)]
        for cp in copies:
            cp.start()
        for cp in copies:
            cp.wait()

    return pl.pallas_call(
        body, name="rs_send_partials", in_specs=[ANY] * N_BIG, out_specs=[ANY] * N_BIG,
        out_shape=[jax.ShapeDtypeStruct((3,) + _shape(b, half=True, shard=True), BF16) for b in BIGS],
        scratch_shapes=[pltpu.SemaphoreType.DMA((3 * N_BIG,)), pltpu.SemaphoreType.DMA((3 * N_BIG,))],
    )(*sums)


def _shard_sum(b, mine, got, ck_arr):
    blk = _blk(b)

    def body(ck_ref, m_ref, r0, r1, r2, o_ref):
        del ck_ref
        o_ref[...] = (m_ref[...].astype(F32) + r0[...].astype(F32)) + (r1[...].astype(F32) + r2[...].astype(F32))

    def peer(j):
        return pl.BlockSpec((None,) + blk, lambda st, ck: (j,) + _bidx(b, 0, 0, st))

    return pl.pallas_call(
        body, name=f"rs_shard_sum_{b.name}",
        grid_spec=pltpu.PrefetchScalarGridSpec(
            num_scalar_prefetch=1, grid=(b.sub,),
            in_specs=[pl.BlockSpec(blk, lambda st, ck: _bidx(b, 0, ck[1], st)), peer(0), peer(1), peer(2)],
            out_specs=pl.BlockSpec(blk, lambda st, ck: _bidx(b, ck[0], 0, st))),
        out_shape=jax.ShapeDtypeStruct(_shape(b, shard=True), F32),
        compiler_params=_cp(("arbitrary",), 40),
    )(ck_arr, mine, got, got, got)


def _share_halves(gs):
    def body(*refs):
        ins, outs, send, recv = refs[:N_BIG], refs[N_BIG:2 * N_BIG], refs[2 * N_BIG], refs[2 * N_BIG + 1]
        del ins
        x, y, c, _, _, _ = _place()
        copies = [pltpu.make_async_remote_copy(src_ref=_at(outs[a], b, h=c), dst_ref=_at(outs[a], b, h=c),
                                               send_sem=send.at[a], recv_sem=recv.at[a], device_id=(x, y, 1 - c),
                                               device_id_type=MESH)
                  for a, b in enumerate(BIGS)]
        for cp in copies:
            cp.start()
        for cp in copies:
            cp.wait()

    return pl.pallas_call(
        body, name="rs_share_halves", in_specs=[ANY] * N_BIG, out_specs=[ANY] * N_BIG,
        out_shape=[jax.ShapeDtypeStruct(_shape(b, shard=True), F32) for b in BIGS],
        scratch_shapes=[pltpu.SemaphoreType.DMA((N_BIG,)), pltpu.SemaphoreType.DMA((N_BIG,))],
        input_output_aliases={a: a for a in range(N_BIG)},
    )(*gs)


def _gather_small(block):
    m_per, n = block.shape

    def body(x_ref, out_ref, send_sems, recv_sems, local_sem):
        x, y, c = lax.axis_index("x"), lax.axis_index("y"), lax.axis_index("c")
        me, sibling = (x, y, c), (x, y, 1 - c)
        chips = [(1 - x, y), (x, 1 - y), (1 - x, 1 - y)]

        def rows(px, py, pc):
            return out_ref.at[pl.ds((4 * px + 2 * py + pc) * m_per, m_per), :]

        def copy(k, blk, to, src=None):
            return pltpu.make_async_remote_copy(
                src_ref=rows(*blk) if src is None else src, dst_ref=rows(*blk), send_sem=send_sems.at[k],
                recv_sem=recv_sems.at[k], device_id=to, device_id_type=MESH)

        mine = pltpu.make_async_copy(x_ref, rows(*me), local_sem)
        mine.start()
        first = [copy(0, me, sibling, src=x_ref)]
        first += [copy(1 + j, me, (*chip, c), src=x_ref) for j, chip in enumerate(chips)]
        for cp in first:
            cp.start()
        passed = [copy(4 + j, (*chip, c), sibling) for j, chip in enumerate(chips)]
        for j, chip in enumerate(chips):
            copy(1 + j, (*chip, c), me).wait_recv()
            passed[j].start()
        copy(0, sibling, me).wait_recv()
        for j, chip in enumerate(chips):
            copy(4 + j, (*chip, 1 - c), me).wait_recv()
        for cp in first + passed:
            cp.wait_send()
        mine.wait()

    return pl.pallas_call(
        body, name="gather_small_grads",
        out_shape=jax.ShapeDtypeStruct((8 * m_per, n), block.dtype),
        in_specs=[pl.BlockSpec(memory_space=pltpu.VMEM)], out_specs=pl.BlockSpec(memory_space=pltpu.VMEM),
        scratch_shapes=[pltpu.SemaphoreType.DMA((7,)), pltpu.SemaphoreType.DMA((7,)), pltpu.SemaphoreType.DMA],
    )(block)


def _sum_small(stack):
    _, m_per, n = stack.shape

    def body(x_ref, o_ref):
        acc = x_ref[0]
        for dev in range(1, 8):
            acc = acc + x_ref[dev]
        o_ref[...] = acc

    return pl.pallas_call(body, name="sum_small_grads", out_shape=jax.ShapeDtypeStruct((m_per, n), F32))(stack)


def _adamw(name, w, g, m, v, rows):
    shape = w.shape

    def body(w_ref, g_ref, m_ref, v_ref, d_ref, mo_ref, vo_ref):
        gg = g_ref[...]
        mn = ADAM_B1 * m_ref[...] + (1.0 - ADAM_B1) * gg
        vn = ADAM_B2 * v_ref[...] + (1.0 - ADAM_B2) * (gg * gg)
        m_hat = mn / (1.0 - ADAM_B1 ** ADAM_STEP)
        v_hat = vn / (1.0 - ADAM_B2 ** ADAM_STEP)
        d_ref[...] = -ADAM_LR * (m_hat / (jnp.sqrt(v_hat) + ADAM_EPS) + ADAM_WD * w_ref[...])
        mo_ref[...] = mn
        vo_ref[...] = vn

    spec = pl.BlockSpec((rows,) + shape[1:], lambda i: (i,) + (0,) * (len(shape) - 1))
    return pl.pallas_call(
        body, name=name, grid=(shape[0] // rows,), in_specs=[spec] * 4, out_specs=[spec] * 3,
        out_shape=[jax.ShapeDtypeStruct(shape, F32)] * 3, compiler_params=_cp(("parallel",), 48),
    )(w, g, m, v)


ADAM_ROWS = dict(w_in_e=256, pool_w=4, w_out_e=256, w_in_o=128, w_out_o=256)


def _pack(parts, rows):
    flat = jnp.concatenate([p.reshape(-1).astype(F32) for p in parts])
    return jnp.pad(flat, (0, rows * 128 - flat.shape[0])).reshape(rows, 128)


def _unpack(buf, shapes):
    flat = buf.reshape(-1)
    out, off = [], 0
    for shp in shapes:
        n = 1
        for dim in shp:
            n *= dim
        out.append(flat[off:off + n].reshape(shp))
        off += n
    return out


WEIGHTS = ("even_norm", "even_w_in", "even_pool_w", "even_pool_scale", "even_ws", "even_bs", "even_w_out", "odd_norm",
           "odd_w_in", "odd_conv_w", "odd_w_out", "final_norm")
BIG_OF = dict(w_in_e="even_w_in", pool_w="even_pool_w", w_out_e="even_w_out", w_in_o="odd_w_in", w_out_o="odd_w_out")
SMALL = ("even_norm", "even_pool_scale", "even_ws", "even_bs", "final_norm", "odd_norm", "odd_conv_w")
SMALL_GRAD_ROWS = 576
SMALL_STATE_ROWS = 552


def kernel(x, even_norm, even_w_in, even_pool_w, even_pool_scale, even_ws, even_bs, even_w_out, odd_norm, odd_w_in, odd_conv_w, odd_w_out, final_norm, loss_target, m_even_norm, m_even_w_in, m_even_pool_w, m_even_pool_scale, m_even_ws, m_even_bs, m_even_w_out, m_odd_norm, m_odd_w_in, m_odd_conv_w, m_odd_w_out, m_final_norm, v_even_norm, v_even_w_in, v_even_pool_w, v_even_pool_scale, v_even_ws, v_even_bs, v_even_w_out, v_odd_norm, v_odd_w_in, v_odd_conv_w, v_odd_w_out, v_final_norm):
    wv = dict(zip(WEIGHTS, (even_norm, even_w_in, even_pool_w, even_pool_scale, even_ws, even_bs, even_w_out, odd_norm,
                            odd_w_in, odd_conv_w, odd_w_out, final_norm)))
    mv = dict(zip(WEIGHTS, (m_even_norm, m_even_w_in, m_even_pool_w, m_even_pool_scale, m_even_ws, m_even_bs,
                            m_even_w_out, m_odd_norm, m_odd_w_in, m_odd_conv_w, m_odd_w_out, m_final_norm)))
    vv = dict(zip(WEIGHTS, (v_even_norm, v_even_w_in, v_even_pool_w, v_even_pool_scale, v_even_ws, v_even_bs,
                            v_even_w_out, v_odd_norm, v_odd_w_in, v_odd_conv_w, v_odd_w_out, v_final_norm)))
    c = lax.axis_index("c")
    k_me = 2 * lax.axis_index("x") + lax.axis_index("y")

    shards = [wv[BIG_OF[b.name]][0].astype(BF16) for b in BIGS]
    tiny = jnp.concatenate([odd_conv_w[0], odd_norm], axis=0)
    *full, tiny_all = _gather_weights(shards, tiny)
    tiny_full = jnp.transpose(tiny_all, (1, 0, 2)).reshape(4, D_MODEL)
    w = dict(zip((b.name for b in BIGS), full))
    w.update(even_norm=even_norm, pool_scale=even_pool_scale, ws=even_ws[0], bs=even_bs[0],
             final_norm=final_norm.reshape(1, D_MODEL), conv_w=tiny_full[:3], odd_norm=tiny_full[3:4])

    loss, dx, g = _local_step(x[0], loss_target[0], w)
    loss = lax.psum(loss, ("x", "y", "c"))

    got = _swap_halves([g[b.name] for b in BIGS])
    c_arr = jnp.reshape(c, (1,)).astype(jnp.int32)
    sums = [_chip_sum(b, g[b.name], r, c_arr) for b, r in zip(BIGS, got)]
    parts = _send_partials(sums)
    ck_arr = jnp.stack([c, k_me]).astype(jnp.int32)
    halves = [_shard_sum(b, sm, p, ck_arr) for b, sm, p in zip(BIGS, sums, parts)]
    g_shard = dict(zip((b.name for b in BIGS), _share_halves(halves)))

    small_g = _pack([g["even_norm"], g["pool_scale"], g["ws"], g["bs"], g["final_norm"], g["odd_norm"], g["conv_w"]],
                    SMALL_GRAD_ROWS)
    small_g = _sum_small(_gather_small(small_g).reshape(8, SMALL_GRAD_ROWS, 128))
    g_en, g_ps, g_ws, g_bs, g_fn, g_on, g_cw = _unpack(
        small_g, [(1, D_MODEL), (1, D_MODEL), (1, 4, CHUNK, CHUNK), (1, 4, CHUNK), (D_MODEL,), (1, D_MODEL), (1, 3, D_MODEL)])
    g_on = lax.dynamic_slice(g_on, (0, k_me * 256), (1, 256))
    g_cw = lax.dynamic_slice(g_cw, (0, 0, k_me * 256), (1, 3, 256))
    grad = dict(even_norm=g_en, even_pool_scale=g_ps, even_ws=g_ws, even_bs=g_bs, final_norm=g_fn, odd_norm=g_on,
                odd_conv_w=g_cw)
    for b in BIGS:
        grad[BIG_OF[b.name]] = g_shard[b.name][None]

    delta, new_m, new_v = {}, {}, {}
    for b in BIGS:
        n = BIG_OF[b.name]
        d_, m_, v_ = _adamw(f"adamw_{b.name}", wv[n][0], g_shard[b.name], mv[n][0], vv[n][0], ADAM_ROWS[b.name])
        delta[n], new_m[n], new_v[n] = d_[None], m_[None], v_[None]
    shapes = [wv[n].shape for n in SMALL]
    packed = [_pack([src[n] for n in SMALL], SMALL_STATE_ROWS) for src in (wv, grad, mv, vv)]
    outs = _adamw("adamw_small", *packed, SMALL_STATE_ROWS)
    for dst, buf in zip((delta, new_m, new_v), outs):
        for n, arr in zip(SMALL, _unpack(buf, shapes)):
            dst[n] = arr

    return (loss, dx[None], *[grad[n] for n in WEIGHTS], *[delta[n] for n in WEIGHTS], *[new_m[n] for n in WEIGHTS],
            *[new_v[n] for n in WEIGHTS])
```

```python
from typing import NamedTuple

import jax
import jax.numpy as jnp
from jax import lax
from jax.experimental import pallas as pl
from jax.experimental.pallas import tpu as pltpu

F32, BF16 = jnp.float32, jnp.bfloat16

D_MODEL = 1024
EPS = 1e-6
NEG = -1e30
POOL_SIZES = (2, 4, 8, 16)
GROUP_W = 256
CHUNK = 128
DILATIONS = (1, 4, 16)
N_SLOTS = 8
HEAD_DIM = 128
ATTN_BLOCK = 128
SCALE = HEAD_DIM ** -0.5
EVEN_IN = 5120
ODD_IN = 14336
QKV_BLOCKS = 9
ODD_BLOCKS = ODD_IN // D_MODEL
SLOPES = tuple(tuple(2.0 ** (-8.0 * (g * N_SLOTS + s + 1) / (3 * N_SLOTS)) for s in range(N_SLOTS)) for g in range(3))

ADAM_LR, ADAM_B1, ADAM_B2, ADAM_EPS, ADAM_WD, ADAM_STEP = 0.001, 0.9, 0.999, 1e-08, 0.01, 10

HALO = 16
TS = 512
TM = 256
MIB = 1 << 20
MESH = pl.DeviceIdType.MESH
ANY = pl.BlockSpec(memory_space=pl.ANY)


def _cp(sem, vmem_mib):
    return pltpu.CompilerParams(dimension_semantics=sem, vmem_limit_bytes=vmem_mib * MIB)


def _sig(x):
    return 1.0 / (1.0 + jnp.exp(-x))


def _win_sum(e, w, forward):
    n = e.shape[0]
    k = 1
    while k < w:
        e = e + pltpu.roll(e, (n - k) if forward else k, 0)
        k *= 2
    return e


def _mm_nn(name, a, b, tm, tn, out_dtype, resid=None, col_map=None, n_cols=None, into=None):
    m, k = a.shape
    n = b.shape[1]
    if col_map is None:
        col_map, n_cols = (lambda j: j), n // tn

    def body(*refs):
        a_ref, b_ref = refs[0], refs[1]
        acc = jnp.dot(a_ref[...].astype(BF16), b_ref[...], preferred_element_type=F32)
        if resid is not None:
            acc = acc + refs[2][...]
        o_ref = refs[-1]
        o_ref[...] = acc.astype(out_dtype)

    in_specs = [pl.BlockSpec((tm, k), lambda j, i: (i, 0)), pl.BlockSpec((k, tn), lambda j, i: (0, col_map(j)))]
    args = [a, b]
    if resid is not None:
        in_specs.append(pl.BlockSpec((tm, tn), lambda j, i: (i, col_map(j))))
        args.append(resid)
    aliases = {}
    if into is not None:
        aliases = {len(args): 0}
        in_specs.append(ANY)
        args.append(into)
    return pl.pallas_call(
        body, name=name, grid=(n_cols, m // tm), in_specs=in_specs,
        out_specs=pl.BlockSpec((tm, tn), lambda j, i: (i, col_map(j))),
        out_shape=jax.ShapeDtypeStruct((m, n), out_dtype), input_output_aliases=aliases,
        compiler_params=_cp(("parallel", "parallel"), 48),
    )(*args)


def _mm_nt(name, a, b, tm, tk, out_dtype, k_map=None, nk=None):
    m, k = a.shape
    n = b.shape[0]
    if k_map is None:
        k_map, nk = (lambda kk: kk), k // tk

    def body(a_ref, b_ref, o_ref, acc_ref):
        kk = pl.program_id(1)
        p = lax.dot_general(a_ref[...].astype(BF16), b_ref[...], (((1,), (1,)), ((), ())), preferred_element_type=F32)
        if nk == 1:
            o_ref[...] = p.astype(out_dtype)
        else:
            @pl.when(kk == 0)
            def _():
                acc_ref[...] = p

            @pl.when(kk > 0)
            def _():
                acc_ref[...] += p

            @pl.when(kk == nk - 1)
            def _():
                o_ref[...] = acc_ref[...].astype(out_dtype)

    return pl.pallas_call(
        body, name=name, grid=(m // tm, nk),
        in_specs=[pl.BlockSpec((tm, tk), lambda i, kk: (i, k_map(kk))), pl.BlockSpec((n, tk), lambda i, kk: (0, k_map(kk)))],
        out_specs=pl.BlockSpec((tm, n), lambda i, kk: (i, 0)),
        out_shape=jax.ShapeDtypeStruct((m, n), out_dtype),
        scratch_shapes=[pltpu.VMEM((tm, n) if nk > 1 else (8, 128), F32)],
        compiler_params=_cp(("parallel", "arbitrary"), 56),
    )(a, b)


def _mm_tn(name, a, g, tn, ts, col_map=None, n_cols=None, into=None):
    s, ka = a.shape
    n = g.shape[1]
    if col_map is None:
        col_map, n_cols = (lambda j: j), n // tn

    def body(a_ref, g_ref, *rest):
        o_ref = rest[-1]
        st = pl.program_id(1)
        p = lax.dot_general(a_ref[...], g_ref[...].astype(BF16), (((0,), (0,)), ((), ())), preferred_element_type=F32)

        @pl.when(st == 0)
        def _():
            o_ref[...] = p

        @pl.when(st > 0)
        def _():
            o_ref[...] += p

    in_specs = [pl.BlockSpec((ts, ka), lambda j, st: (st, 0)), pl.BlockSpec((ts, tn), lambda j, st: (st, col_map(j)))]
    args = [a, g]
    aliases = {}
    if into is not None:
        aliases = {2: 0}
        in_specs.append(ANY)
        args.append(into)
    return pl.pallas_call(
        body, name=name, grid=(n_cols, s // ts), in_specs=in_specs,
        out_specs=pl.BlockSpec((ka, tn), lambda j, st: (0, col_map(j))),
        out_shape=jax.ShapeDtypeStruct((ka, n), F32), input_output_aliases=aliases,
        compiler_params=_cp(("parallel", "arbitrary"), 56),
    )(*args)


def _rms_fwd(name, x, g):
    s = x.shape[0]

    def body(x_ref, g_ref, h_ref):
        xf = x_ref[...]
        r = lax.rsqrt(jnp.mean(xf * xf, axis=-1, keepdims=True) + EPS)
        h_ref[...] = (xf * r * g_ref[...]).astype(BF16)

    row = pl.BlockSpec((TS, D_MODEL), lambda i: (i, 0))
    return pl.pallas_call(
        body, name=name, grid=(s // TS,), in_specs=[row, pl.BlockSpec((1, D_MODEL), lambda i: (0, 0))], out_specs=row,
        out_shape=jax.ShapeDtypeStruct((s, D_MODEL), BF16), compiler_params=_cp(("parallel",), 32),
    )(x, g)


def _class_major(a, d):
    return a.reshape(d, a.shape[0] // d, a.shape[1])


def _class_spec(d, tile, width):
    return pl.BlockSpec((d, tile // d, width), lambda i: (0, i, 0))


LANES = 128


def _token_scratch(tile, width):
    return pltpu.VMEM((width // LANES, tile, LANES), F32)


def _put(scr, val):
    for c in range(scr.shape[0]):
        scr[c] = val[:, c * LANES:(c + 1) * LANES]


def _get(scr):
    return jnp.concatenate([scr[c] for c in range(scr.shape[0])], axis=1)


def _to_classes(ref3, scr, d, dtype):
    n = ref3.shape[1]
    for c in range(scr.shape[0]):
        for r in range(d):
            ref3[r, :, c * LANES:(c + 1) * LANES] = scr.at[c][pl.ds(r, n, stride=d), :].astype(dtype)


def _from_classes(scr, ref3, d):
    n = ref3.shape[1]
    for c in range(scr.shape[0]):
        for r in range(d):
            scr.at[c][pl.ds(r, n, stride=d), :] = ref3[r, :, c * LANES:(c + 1) * LANES].astype(F32)


def _rms_fwd_orders(name, x, g):
    s = x.shape[0]

    def body(x_ref, g_ref, h_ref, h4_ref, h16_ref, scr):
        xf = x_ref[...]
        r = lax.rsqrt(jnp.mean(xf * xf, axis=-1, keepdims=True) + EPS)
        h = xf * r * g_ref[...]
        h_ref[...] = h.astype(BF16)
        _put(scr, h)
        _to_classes(h4_ref, scr, 4, BF16)
        _to_classes(h16_ref, scr, 16, BF16)

    row = pl.BlockSpec((TS, D_MODEL), lambda i: (i, 0))
    h, h4, h16 = pl.pallas_call(
        body, name=name, grid=(s // TS,), in_specs=[row, pl.BlockSpec((1, D_MODEL), lambda i: (0, 0))],
        out_specs=[row, _class_spec(4, TS, D_MODEL), _class_spec(16, TS, D_MODEL)],
        out_shape=[jax.ShapeDtypeStruct((s, D_MODEL), BF16), jax.ShapeDtypeStruct((4, s // 4, D_MODEL), BF16),
                   jax.ShapeDtypeStruct((16, s // 16, D_MODEL), BF16)],
        scratch_shapes=[_token_scratch(TS, D_MODEL)],
        compiler_params=_cp(("parallel",), 32),
    )(x, g)
    return h, h4.reshape(s, D_MODEL), h16.reshape(s, D_MODEL)


def _rms_bwd(name, dh, x, g, dres, dh4=None, dh16=None):
    s = x.shape[0]
    extra = dh4 is not None

    def body(dh_ref, x_ref, g_ref, dres_ref, *rest):
        if extra:
            dh4_ref, dh16_ref, dx_ref, dg_ref, scr = rest
        else:
            dx_ref, dg_ref = rest
        xf = x_ref[...]
        r = lax.rsqrt(jnp.mean(xf * xf, axis=-1, keepdims=True) + EPS)
        xh = xf * r
        dhf = dh_ref[...]
        if extra:
            _from_classes(scr, dh4_ref, 4)
            dhf = dhf + _get(scr)
            _from_classes(scr, dh16_ref, 16)
            dhf = dhf + _get(scr)
        dxh = dhf * g_ref[...]
        dx_ref[...] = dres_ref[...] + r * (dxh - xh * jnp.mean(dxh * xh, axis=-1, keepdims=True))
        part = jnp.sum(dhf * xh, axis=0, keepdims=True)

        @pl.when(pl.program_id(0) == 0)
        def _():
            dg_ref[...] = part

        @pl.when(pl.program_id(0) > 0)
        def _():
            dg_ref[...] += part

    row = pl.BlockSpec((TS, D_MODEL), lambda i: (i, 0))
    vec = pl.BlockSpec((1, D_MODEL), lambda i: (0, 0))
    in_specs, args, scratch = [row, row, vec, row], [dh, x, g, dres], []
    if extra:
        in_specs += [_class_spec(4, TS, D_MODEL), _class_spec(16, TS, D_MODEL)]
        args += [_class_major(dh4, 4), _class_major(dh16, 16)]
        scratch = [_token_scratch(TS, D_MODEL)]
    return pl.pallas_call(
        body, name=name, grid=(s // TS,), in_specs=in_specs, out_specs=[row, vec],
        out_shape=[jax.ShapeDtypeStruct((s, D_MODEL), F32), jax.ShapeDtypeStruct((1, D_MODEL), F32)],
        scratch_shapes=scratch, compiler_params=_cp(("arbitrary",), 40),
    )(*args)


def _final_loss(x, g, target):
    s = x.shape[0]

    def body(x_ref, g_ref, t_ref, dx_ref, loss_ref, dg_ref):
        xf = x_ref[...]
        gg = g_ref[...]
        r = lax.rsqrt(jnp.mean(xf * xf, axis=-1, keepdims=True) + EPS)
        xh = xf * r
        e = xh * gg - t_ref[...]
        dy = e * (1.0 / D_MODEL)
        dxh = dy * gg
        dx_ref[...] = r * (dxh - xh * jnp.mean(dxh * xh, axis=-1, keepdims=True))
        lpart = 0.5 * jnp.sum(jnp.mean(e * e, axis=-1, keepdims=True), axis=0, keepdims=True)
        lpart = jnp.broadcast_to(lpart, (8, 128))
        gpart = jnp.sum(dy * xh, axis=0, keepdims=True)

        @pl.when(pl.program_id(0) == 0)
        def _():
            loss_ref[...] = lpart
            dg_ref[...] = gpart

        @pl.when(pl.program_id(0) > 0)
        def _():
            loss_ref[...] += lpart
            dg_ref[...] += gpart

    row = pl.BlockSpec((TS, D_MODEL), lambda i: (i, 0))
    vec = pl.BlockSpec((1, D_MODEL), lambda i: (0, 0))
    return pl.pallas_call(
        body, name="final_loss", grid=(s // TS,), in_specs=[row, vec, row],
        out_specs=[row, pl.BlockSpec((8, 128), lambda i: (0, 0)), vec],
        out_shape=[jax.ShapeDtypeStruct((s, D_MODEL), F32), jax.ShapeDtypeStruct((8, 128), F32),
                   jax.ShapeDtypeStruct((1, D_MODEL), F32)],
        compiler_params=_cp(("arbitrary",), 40),
    )(x, g, target)


def _zcol(c, tm=TM):
    return pl.BlockSpec((tm, D_MODEL), lambda i, c=c: (i, c))


def _prev_halo(c, tm=TM):
    return pl.BlockSpec((HALO, D_MODEL), lambda i, c=c: (jnp.maximum(i * (tm // HALO) - 1, 0), c))


def _next_halo(c, n_rows, tm=TM):
    last = n_rows // HALO - 1
    return pl.BlockSpec((HALO, D_MODEL), lambda i, c=c: (jnp.minimum((i + 1) * (tm // HALO), last), c))


def _full(shape):
    return pl.BlockSpec(shape, lambda i: (0,) * len(shape))


def _inv_count(first_row, n, w):
    t = first_row + lax.broadcasted_iota(jnp.int32, (n, 1), 0)
    return 1.0 / jnp.minimum(t + 1, w).astype(F32)


def _even_mix_fwd(z, pw, ps, wt, bs):
    s = z.shape[0]

    def body(a_ref, ga_ref, u_ref, v_ref, gb_ref, halo_ref, pw_ref, ps_ref, wt_ref, bs_ref, y_ref):
        i = pl.program_id(0)
        a = a_ref[...].astype(F32)
        halo = jnp.where(i > 0, halo_ref[...].astype(F32), 0.0)
        ext = jnp.concatenate([halo, a], axis=0)
        ga = ga_ref[...].astype(F32)
        sga = ga * _sig(ga)
        for g, w in enumerate(POOL_SIZES):
            cs = slice(g * GROUP_W, (g + 1) * GROUP_W)
            win = _win_sum(ext[:, cs], w, False)[HALO:]
            pooled = win * _inv_count(i * TM, TM, w) - a[:, cs]
            mixed = jnp.dot(pooled.astype(BF16), pw_ref[g], preferred_element_type=F32)
            y_ref[:, cs] = (mixed * ps_ref[:, cs] * sga[:, cs]).astype(BF16)
        gb = gb_ref[...].astype(F32)
        gate = u_ref[...].astype(F32) * (gb * _sig(gb))
        for ch in range(TM // CHUNK):
            rs = slice(ch * CHUNK, (ch + 1) * CHUNK)
            for g in range(4):
                cs = slice(g * GROUP_W, (g + 1) * GROUP_W)
                mixb = jnp.dot(wt_ref[g], v_ref[rs, cs], preferred_element_type=F32) + bs_ref[g]
                y_ref[rs, D_MODEL + g * GROUP_W:D_MODEL + (g + 1) * GROUP_W] = (gate[rs, cs] * mixb).astype(BF16)

    return pl.pallas_call(
        body, name="even_mix_fwd", grid=(s // TM,),
        in_specs=[_zcol(0), _zcol(1), _zcol(2), _zcol(3), _zcol(4), _prev_halo(0),
                  _full((4, GROUP_W, GROUP_W)), _full((1, D_MODEL)), _full((4, CHUNK, CHUNK)), _full((4, CHUNK, 1))],
        out_specs=pl.BlockSpec((TM, 2 * D_MODEL), lambda i: (i, 0)),
        out_shape=jax.ShapeDtypeStruct((s, 2 * D_MODEL), BF16),
        compiler_params=_cp(("parallel",), 48),
    )(z, z, z, z, z, z, pw, ps, wt, bs)


def _even_mix_bwd(dy, z, pw, ps, wt, wtt, bs):
    s = z.shape[0]
    n_tiles = s // TM

    def body(dy_ref, a_ref, ga_ref, u_ref, v_ref, gb_ref, halo_ref, dyn_ref, gan_ref, pw_ref, ps_ref, wt_ref, wtt_ref,
             bs_ref, dz_ref, dpw_ref, dps_ref, dws_ref, dbs_ref):
        i = pl.program_id(0)

        @pl.when(i == 0)
        def _():
            dpw_ref[...] = jnp.zeros_like(dpw_ref)
            dps_ref[...] = jnp.zeros_like(dps_ref)
            dws_ref[...] = jnp.zeros_like(dws_ref)
            dbs_ref[...] = jnp.zeros_like(dbs_ref)

        a = a_ref[...].astype(F32)
        halo = jnp.where(i > 0, halo_ref[...].astype(F32), 0.0)
        ext = jnp.concatenate([halo, a], axis=0)
        ga = ga_ref[...].astype(F32)
        sg = _sig(ga)
        sga = ga * sg
        dsga = sg * (1.0 + ga * (1.0 - sg))
        dya = dy_ref[:, :D_MODEL].astype(F32)
        gan = gan_ref[...].astype(F32)
        dmn_all = jnp.where(i < n_tiles - 1, dyn_ref[...].astype(F32) * ps_ref[...] * (gan * _sig(gan)), 0.0)
        for g, w in enumerate(POOL_SIZES):
            cs = slice(g * GROUP_W, (g + 1) * GROUP_W)
            inv = _inv_count(i * TM, TM, w)
            pooled = _win_sum(ext[:, cs], w, False)[HALO:] * inv - a[:, cs]
            pb = pooled.astype(BF16)
            mixed = jnp.dot(pb, pw_ref[g], preferred_element_type=F32)
            dyg = dya[:, cs]
            psg = ps_ref[:, cs]
            dm = (dyg * psg * sga[:, cs]).astype(BF16)
            dz_ref[:, D_MODEL + g * GROUP_W:D_MODEL + (g + 1) * GROUP_W] = (dyg * mixed * psg * dsga[:, cs]).astype(BF16)
            dps_ref[:, cs] += jnp.sum(dyg * mixed * sga[:, cs], axis=0, keepdims=True)
            dpw_ref[g] += lax.dot_general(pb, dm, (((0,), (0,)), ((), ())), preferred_element_type=F32)
            nt = (((1,), (1,)), ((), ()))
            dpool = lax.dot_general(dm, pw_ref[g], nt, preferred_element_type=F32)
            dpool_n = lax.dot_general(dmn_all[:, cs].astype(BF16), pw_ref[g], nt, preferred_element_type=F32)
            e = jnp.concatenate([dpool * inv, dpool_n * _inv_count((i + 1) * TM, HALO, w)], axis=0)
            dz_ref[:, cs] = (_win_sum(e, w, True)[:TM] - dpool).astype(BF16)

        gb = gb_ref[...].astype(F32)
        sg = _sig(gb)
        sgb = gb * sg
        dsgb = sg * (1.0 + gb * (1.0 - sg))
        u = u_ref[...].astype(F32)
        dyb = dy_ref[:, D_MODEL:].astype(F32)
        tril = lax.broadcasted_iota(jnp.int32, (CHUNK, CHUNK), 0) >= lax.broadcasted_iota(jnp.int32, (CHUNK, CHUNK), 1)
        lane = lax.broadcasted_iota(jnp.int32, (CHUNK, 128), 1)
        for ch in range(TM // CHUNK):
            rs = slice(ch * CHUNK, (ch + 1) * CHUNK)
            for g in range(4):
                cs = slice(g * GROUP_W, (g + 1) * GROUP_W)
                vb = v_ref[rs, cs]
                mixb = jnp.dot(wt_ref[g], vb, preferred_element_type=F32) + bs_ref[g]
                dyu = dyb[rs, cs] * u[rs, cs]
                dmix = dyu * sgb[rs, cs]
                dmb = dmix.astype(BF16)
                o = g * GROUP_W
                dz_ref[rs, 2 * D_MODEL + o:2 * D_MODEL + o + GROUP_W] = (dyb[rs, cs] * mixb * sgb[rs, cs]).astype(BF16)
                dz_ref[rs, 3 * D_MODEL + o:3 * D_MODEL + o + GROUP_W] = jnp.dot(
                    wtt_ref[g], dmb, preferred_element_type=F32).astype(BF16)
                dz_ref[rs, 4 * D_MODEL + o:4 * D_MODEL + o + GROUP_W] = (dyu * mixb * dsgb[rs, cs]).astype(BF16)
                dws = lax.dot_general(dmb, vb, (((1,), (1,)), ((), ())), preferred_element_type=F32)
                dws_ref[g] += jnp.where(tril, dws, 0.0)
                dbs_ref[...] += jnp.where(lane == g, jnp.sum(dmix, axis=1, keepdims=True), 0.0)

    return pl.pallas_call(
        body, name="even_mix_bwd", grid=(n_tiles,),
        in_specs=[pl.BlockSpec((TM, 2 * D_MODEL), lambda i: (i, 0)), _zcol(0), _zcol(1), _zcol(2), _zcol(3), _zcol(4),
                  _prev_halo(0), _next_halo(0, s), _next_halo(1, s),
                  _full((4, GROUP_W, GROUP_W)), _full((1, D_MODEL)), _full((4, CHUNK, CHUNK)), _full((4, CHUNK, CHUNK)),
                  _full((4, CHUNK, 1))],
        out_specs=[pl.BlockSpec((TM, EVEN_IN), lambda i: (i, 0)), _full((4, GROUP_W, GROUP_W)), _full((1, D_MODEL)),
                   _full((4, CHUNK, CHUNK)), _full((CHUNK, 128))],
        out_shape=[jax.ShapeDtypeStruct((s, EVEN_IN), BF16), jax.ShapeDtypeStruct((4, GROUP_W, GROUP_W), F32),
                   jax.ShapeDtypeStruct((1, D_MODEL), F32), jax.ShapeDtypeStruct((4, CHUNK, CHUNK), F32),
                   jax.ShapeDtypeStruct((CHUNK, 128), F32)],
        compiler_params=_cp(("arbitrary",), 56),
    )(dy, z, z, z, z, z, z, dy, z, pw, ps, wt, wtt, bs)


STAT_W = 128


def _band():
    row = lax.broadcasted_iota(jnp.int32, (ATTN_BLOCK, 2 * ATTN_BLOCK), 0)
    col = lax.broadcasted_iota(jnp.int32, (ATTN_BLOCK, 2 * ATTN_BLOCK), 1)
    steps = row + ATTN_BLOCK - col
    return steps, col


def _attn_fwd(z, gi):
    s = z.shape[0]
    d = DILATIONS[gi]
    nb = s // d // ATTN_BLOCK

    def spec(which, prev=False):
        cb = which * 3 + gi
        if prev:
            return pl.BlockSpec((ATTN_BLOCK, D_MODEL), lambda r, i: (r * nb + jnp.maximum(i - 1, 0), cb))
        return pl.BlockSpec((ATTN_BLOCK, D_MODEL), lambda r, i: (r * nb + i, cb))

    def body(q_ref, kp_ref, kc_ref, vp_ref, vc_ref, o_ref, lse_ref):
        i = pl.program_id(1)
        steps, col = _band()
        valid = (steps >= 0) & (steps <= ATTN_BLOCK) & ((i > 0) | (col >= ATTN_BLOCK))
        negdist = -(steps * d).astype(F32)
        lane = lax.broadcasted_iota(jnp.int32, (ATTN_BLOCK, STAT_W), 1)
        stat = jnp.zeros((ATTN_BLOCK, STAT_W), F32)
        for h in range(N_SLOTS):
            sl = slice(h * HEAD_DIM, (h + 1) * HEAD_DIM)
            k = jnp.concatenate([kp_ref[:, sl], kc_ref[:, sl]], axis=0)
            v = jnp.concatenate([vp_ref[:, sl], vc_ref[:, sl]], axis=0)
            sc = lax.dot_general(q_ref[:, sl], k, (((1,), (1,)), ((), ())), preferred_element_type=F32) * SCALE
            sc = jnp.where(valid, sc + SLOPES[gi][h] * negdist, NEG)
            m = jnp.max(sc, axis=-1, keepdims=True)
            p = jnp.exp(sc - m)
            l = jnp.sum(p, axis=-1, keepdims=True)
            o = jnp.dot((p / l).astype(BF16), v, preferred_element_type=F32)
            o_ref[:, sl] = o.astype(BF16)
            stat = jnp.where(lane == h, m + jnp.log(l), stat)
        lse_ref[...] = stat

    return pl.pallas_call(
        body, name=f"attn_fwd_d{d}", grid=(d, nb),
        in_specs=[spec(0), spec(1, True), spec(1), spec(2, True), spec(2)],
        out_specs=[pl.BlockSpec((ATTN_BLOCK, D_MODEL), lambda r, i: (r * nb + i, 0)),
                   pl.BlockSpec((ATTN_BLOCK, STAT_W), lambda r, i: (r * nb + i, 0))],
        out_shape=[jax.ShapeDtypeStruct((s, D_MODEL), BF16), jax.ShapeDtypeStruct((s, STAT_W), F32)],
        compiler_params=_cp(("parallel", "parallel"), 32),
    )(z, z, z, z, z)


def _attn_bwd(z, dyc, ltot, dst, dz, gi):
    s = z.shape[0]
    d = DILATIONS[gi]
    nb = s // d // ATTN_BLOCK
    n_steps = d * nb

    def rev(cb, width=D_MODEL, prev=False):
        if prev:
            return pl.BlockSpec((ATTN_BLOCK, width), lambda r, n: (r * nb + jnp.maximum(nb - 2 - n, 0), cb))
        return pl.BlockSpec((ATTN_BLOCK, width), lambda r, n: (r * nb + nb - 1 - n, cb))

    def body(q_ref, kp_ref, kc_ref, vp_ref, vc_ref, dy_ref, l_ref, d_ref, dz_in, dz_out, dq_s, dk_s, dv_s, ck_s, cv_s, sems):
        del dz_in
        r = pl.program_id(0)
        n = pl.program_id(1)
        i = nb - 1 - n
        step = r * nb + n

        def out_copy(src, which):
            rows = pl.ds(pl.multiple_of((r * nb + i) * ATTN_BLOCK, ATTN_BLOCK), ATTN_BLOCK)
            return pltpu.make_async_copy(src, dz_out.at[rows, pl.ds((which * 3 + gi) * D_MODEL, D_MODEL)], sems.at[which])

        copies = [out_copy(dq_s, 0), out_copy(dk_s, 1), out_copy(dv_s, 2)]

        @pl.when(step > 0)
        def _():
            for cp in copies:
                cp.wait()

        @pl.when(n == 0)
        def _():
            ck_s[...] = jnp.zeros_like(ck_s)
            cv_s[...] = jnp.zeros_like(cv_s)

        steps, col = _band()
        valid = (steps >= 0) & (steps <= ATTN_BLOCK) & ((i > 0) | (col >= ATTN_BLOCK))
        negdist = -(steps * d).astype(F32)
        nt = (((1,), (1,)), ((), ()))
        tn = (((0,), (0,)), ((), ()))
        for h in range(N_SLOTS):
            sl = slice(h * HEAD_DIM, (h + 1) * HEAD_DIM)
            q = q_ref[:, sl]
            k = jnp.concatenate([kp_ref[:, sl], kc_ref[:, sl]], axis=0)
            v = jnp.concatenate([vp_ref[:, sl], vc_ref[:, sl]], axis=0)
            dy = dy_ref[:, sl]
            sc = lax.dot_general(q, k, nt, preferred_element_type=F32) * SCALE + SLOPES[gi][h] * negdist
            p = jnp.where(valid, jnp.exp(sc - l_ref[:, h:h + 1]), 0.0)
            dp = lax.dot_general(dy, v, nt, preferred_element_type=F32)
            ds = (p * (dp - d_ref[:, h:h + 1])).astype(BF16)
            dq_s[:, sl] = (jnp.dot(ds, k, preferred_element_type=F32) * SCALE).astype(BF16)
            dk = lax.dot_general(ds, q, tn, preferred_element_type=F32) * SCALE
            dv = lax.dot_general(p.astype(BF16), dy, tn, preferred_element_type=F32)
            dk_s[:, sl] = (ck_s[:, sl] + dk[ATTN_BLOCK:]).astype(BF16)
            dv_s[:, sl] = (cv_s[:, sl] + dv[ATTN_BLOCK:]).astype(BF16)
            ck_s[:, sl] = dk[:ATTN_BLOCK]
            cv_s[:, sl] = dv[:ATTN_BLOCK]

        for cp in copies:
            cp.start()

        @pl.when(step == n_steps - 1)
        def _():
            for cp in copies:
                cp.wait()

    stage = pltpu.VMEM((ATTN_BLOCK, D_MODEL), BF16)
    carry = pltpu.VMEM((ATTN_BLOCK, D_MODEL), F32)
    return pl.pallas_call(
        body, name=f"attn_bwd_d{d}", grid=(d, nb),
        in_specs=[rev(gi), rev(3 + gi, prev=True), rev(3 + gi), rev(6 + gi, prev=True), rev(6 + gi),
                  rev(0), rev(0, STAT_W), rev(0, STAT_W), ANY],
        out_specs=ANY,
        out_shape=jax.ShapeDtypeStruct((s, ODD_IN), BF16),
        scratch_shapes=[stage, stage, stage, carry, carry, pltpu.SemaphoreType.DMA((3,))],
        input_output_aliases={8: 0},
        compiler_params=_cp(("arbitrary", "arbitrary"), 32),
    )(z, z, z, z, z, dyc, ltot, dst, dz)


def _odd_mix_fwd(z, os_, lses, cw):
    s = z.shape[0]

    def body(o0, o1, o2, l0, l1, l2, gc_ref, db_ref, dc_ref, dx_ref, gd_ref, hc_ref, hx_ref, cw_ref, y_ref, yc_ref, lt_ref,
             lt4_ref, lt16_ref, scr_o, scr_l):
        i = pl.program_id(0)
        _from_classes(scr_l, l1, 4)
        lse1 = _get(scr_l)
        _from_classes(scr_l, l2, 16)
        ls = [l0[...], lse1, _get(scr_l)]
        lmax = jnp.maximum(jnp.maximum(ls[0], ls[1]), ls[2])
        es = [jnp.exp(l - lmax) for l in ls]
        den = es[0] + es[1] + es[2]
        alpha = [e / den for e in es]
        ltot = lmax + jnp.log(den)
        lt_ref[...] = ltot
        _put(scr_l, ltot)
        _to_classes(lt4_ref, scr_l, 4, F32)
        _to_classes(lt16_ref, scr_l, 16, F32)
        _from_classes(scr_o, o1, 4)
        og1 = _get(scr_o)
        _from_classes(scr_o, o2, 16)
        og = [o0[...].astype(F32), og1, _get(scr_o)]
        gc = gc_ref[...].astype(F32)
        gate = gc * _sig(gc)
        for h in range(N_SLOTS):
            sl = slice(h * HEAD_DIM, (h + 1) * HEAD_DIM)
            yc = (alpha[0][:, h:h + 1] * og[0][:, sl] + alpha[1][:, h:h + 1] * og[1][:, sl]
                  + alpha[2][:, h:h + 1] * og[2][:, sl])
            yc_ref[:, sl] = yc.astype(BF16)
            y_ref[:, sl] = (yc * gate[:, sl]).astype(BF16)

        zc = dc_ref[...].astype(F32) * dx_ref[...].astype(F32)
        halo = jnp.where(i > 0, hc_ref[...].astype(F32) * hx_ref[...].astype(F32), 0.0)
        ext = jnp.concatenate([halo, zc], axis=0)
        z1 = pltpu.roll(ext, 1, 0)[HALO:]
        z2 = pltpu.roll(ext, 2, 0)[HALO:]
        conv = cw_ref[0:1, :] * z2 + cw_ref[1:2, :] * z1 + cw_ref[2:3, :] * zc
        gd = gd_ref[...].astype(F32)
        y_ref[:, D_MODEL:] = (db_ref[...].astype(F32) * conv * (gd * _sig(gd))).astype(BF16)

    row = pl.BlockSpec((TM, D_MODEL), lambda i: (i, 0))
    stat = pl.BlockSpec((TM, STAT_W), lambda i: (i, 0))
    y, ycr, lt, lt4, lt16 = pl.pallas_call(
        body, name="odd_mix_fwd", grid=(s // TM,),
        in_specs=[row, _class_spec(4, TM, D_MODEL), _class_spec(16, TM, D_MODEL),
                  stat, _class_spec(4, TM, STAT_W), _class_spec(16, TM, STAT_W),
                  _zcol(9), _zcol(10), _zcol(11), _zcol(12), _zcol(13), _prev_halo(11), _prev_halo(12), _full((3, D_MODEL))],
        out_specs=[pl.BlockSpec((TM, 2 * D_MODEL), lambda i: (i, 0)), row, stat, _class_spec(4, TM, STAT_W),
                   _class_spec(16, TM, STAT_W)],
        out_shape=[jax.ShapeDtypeStruct((s, 2 * D_MODEL), BF16), jax.ShapeDtypeStruct((s, D_MODEL), BF16),
                   jax.ShapeDtypeStruct((s, STAT_W), F32), jax.ShapeDtypeStruct((4, s // 4, STAT_W), F32),
                   jax.ShapeDtypeStruct((16, s // 16, STAT_W), F32)],
        scratch_shapes=[_token_scratch(TM, D_MODEL), _token_scratch(TM, STAT_W)],
        compiler_params=_cp(("parallel",), 48),
    )(os_[0], _class_major(os_[1], 4), _class_major(os_[2], 16), lses[0], _class_major(lses[1], 4),
      _class_major(lses[2], 16), z, z, z, z, z, z, z, cw)
    return y, ycr, [lt, lt4.reshape(s, STAT_W), lt16.reshape(s, STAT_W)]


def _odd_mix_bwd(dy, z, ycr, cw):
    s = z.shape[0]
    n_tiles = s // TM
    rest = ODD_IN - QKV_BLOCKS * D_MODEL

    def body(dy_ref, yc_ref, gc_ref, db_ref, dc_ref, dx_ref, gd_ref, hc_ref, hx_ref, dyn_ref, dbn_ref, gdn_ref, cw_ref,
             dz_ref, dyc_ref, dyc4_ref, dyc16_ref, dd_ref, dd4_ref, dd16_ref, dcw_ref, stage, sem, scr_o, scr_l):
        i = pl.program_id(0)
        out = pltpu.make_async_copy(
            stage, dz_ref.at[pl.ds(pl.multiple_of(i * TM, TM), TM), pl.ds(QKV_BLOCKS * D_MODEL, rest)], sem)

        @pl.when(i > 0)
        def _():
            out.wait()

        dyc_in = dy_ref[:, :D_MODEL].astype(F32)
        gc = gc_ref[...].astype(F32)
        sg = _sig(gc)
        yc = yc_ref[...].astype(F32)
        dyc = dyc_in * (gc * sg)
        dyc_ref[...] = dyc.astype(BF16)
        _put(scr_o, dyc)
        _to_classes(dyc4_ref, scr_o, 4, BF16)
        _to_classes(dyc16_ref, scr_o, 16, BF16)
        stage[:, 0:D_MODEL] = (dyc_in * yc * (sg * (1.0 + gc * (1.0 - sg)))).astype(BF16)
        prod = dyc * yc
        lane = lax.broadcasted_iota(jnp.int32, (TM, STAT_W), 1)
        stat = jnp.zeros((TM, STAT_W), F32)
        for h in range(N_SLOTS):
            part = jnp.sum(prod[:, h * HEAD_DIM:(h + 1) * HEAD_DIM], axis=-1, keepdims=True)
            stat = jnp.where(lane == h, part, stat)
        dd_ref[...] = stat
        _put(scr_l, stat)
        _to_classes(dd4_ref, scr_l, 4, F32)
        _to_classes(dd16_ref, scr_l, 16, F32)

        dc = dc_ref[...].astype(F32)
        dx = dx_ref[...].astype(F32)
        zc = dc * dx
        halo = jnp.where(i > 0, hc_ref[...].astype(F32) * hx_ref[...].astype(F32), 0.0)
        ext = jnp.concatenate([halo, zc], axis=0)
        z1 = pltpu.roll(ext, 1, 0)[HALO:]
        z2 = pltpu.roll(ext, 2, 0)[HALO:]
        w0, w1, w2 = cw_ref[0:1, :], cw_ref[1:2, :], cw_ref[2:3, :]
        conv = w0 * z2 + w1 * z1 + w2 * zc
        gd = gd_ref[...].astype(F32)
        sg = _sig(gd)
        sgd = gd * sg
        db = db_ref[...].astype(F32)
        dyd = dy_ref[:, D_MODEL:].astype(F32)
        dconv = dyd * db * sgd
        gdn = gdn_ref[...].astype(F32)
        dconv_n = jnp.where(i < n_tiles - 1, dyn_ref[...].astype(F32) * dbn_ref[...].astype(F32) * (gdn * _sig(gdn)), 0.0)
        extn = jnp.concatenate([dconv, dconv_n], axis=0)
        nrow = TM + HALO
        dzc = w2 * dconv + w1 * pltpu.roll(extn, nrow - 1, 0)[:TM] + w0 * pltpu.roll(extn, nrow - 2, 0)[:TM]
        stage[:, D_MODEL:2 * D_MODEL] = (dyd * conv * sgd).astype(BF16)
        stage[:, 2 * D_MODEL:3 * D_MODEL] = (dzc * dx).astype(BF16)
        stage[:, 3 * D_MODEL:4 * D_MODEL] = (dzc * dc).astype(BF16)
        stage[:, 4 * D_MODEL:5 * D_MODEL] = (dyd * db * conv * (sg * (1.0 + gd * (1.0 - sg)))).astype(BF16)
        @pl.when(i == 0)
        def _():
            dcw_ref[...] = jnp.zeros_like(dcw_ref)

        for tap, shifted in enumerate((z2, z1, zc)):
            dcw_ref[tap:tap + 1, :] += jnp.sum(dconv * shifted, axis=0, keepdims=True)

        out.start()

        @pl.when(i == n_tiles - 1)
        def _():
            out.wait()

    row = pl.BlockSpec((TM, D_MODEL), lambda i: (i, 0))
    stat = pl.BlockSpec((TM, STAT_W), lambda i: (i, 0))
    dz, dyc, dyc4, dyc16, dd, dd4, dd16, g_conv = pl.pallas_call(
        body, name="odd_mix_bwd", grid=(n_tiles,),
        in_specs=[pl.BlockSpec((TM, 2 * D_MODEL), lambda i: (i, 0)), row, _zcol(9), _zcol(10), _zcol(11), _zcol(12), _zcol(13),
                  _prev_halo(11), _prev_halo(12), _next_halo(1, s), _next_halo(10, s), _next_halo(13, s), _full((3, D_MODEL))],
        out_specs=[ANY, row, _class_spec(4, TM, D_MODEL), _class_spec(16, TM, D_MODEL),
                   stat, _class_spec(4, TM, STAT_W), _class_spec(16, TM, STAT_W), _full((3, D_MODEL))],
        out_shape=[jax.ShapeDtypeStruct((s, ODD_IN), BF16), jax.ShapeDtypeStruct((s, D_MODEL), BF16),
                   jax.ShapeDtypeStruct((4, s // 4, D_MODEL), BF16), jax.ShapeDtypeStruct((16, s // 16, D_MODEL), BF16),
                   jax.ShapeDtypeStruct((s, STAT_W), F32), jax.ShapeDtypeStruct((4, s // 4, STAT_W), F32),
                   jax.ShapeDtypeStruct((16, s // 16, STAT_W), F32), jax.ShapeDtypeStruct((3, D_MODEL), F32)],
        scratch_shapes=[pltpu.VMEM((TM, rest), BF16), pltpu.SemaphoreType.DMA(()), _token_scratch(TM, D_MODEL),
                        _token_scratch(TM, STAT_W)],
        compiler_params=_cp(("arbitrary",), 48),
    )(dy, ycr, z, z, z, z, z, z, z, dy, z, z, cw)
    dyc = [dyc, dyc4.reshape(s, D_MODEL), dyc16.reshape(s, D_MODEL)]
    dd = [dd, dd4.reshape(s, STAT_W), dd16.reshape(s, STAT_W)]
    return dz, dyc, dd, g_conv


def _cols_of_order(order):
    if order == 0:
        return (lambda j: jnp.where(j < 3, 3 * j, j + 6)), 8
    return (lambda j: 3 * j + order), 3


def _local_step(x, target, w):
    tril = jnp.tril(jnp.ones((CHUNK, CHUNK), bool))
    wt = jnp.where(tril[None], w["ws"], 0.0).astype(BF16)
    wtt = jnp.swapaxes(wt, 1, 2)
    bs = w["bs"].reshape(4, CHUNK, 1)

    h_e = _rms_fwd("rms_fwd_even", x, w["even_norm"])
    z_e = _mm_nn("even_in_proj", h_e, w["w_in_e"], 512, 1280, BF16)
    y_e = _even_mix_fwd(z_e, w["pool_w"], w["pool_scale"], wt, bs)
    x1 = _mm_nn("even_out_proj", y_e, w["w_out_e"], 512, 1024, F32, resid=x)
    h_o = _rms_fwd_orders("rms_fwd_odd", x1, w["odd_norm"])
    z_o = None
    for o in range(3):
        cols, n_cols = _cols_of_order(o)
        z_o = _mm_nn(f"odd_in_proj_o{o}", h_o[o], w["w_in_o"], 512, D_MODEL, BF16, col_map=cols, n_cols=n_cols, into=z_o)
    att = [_attn_fwd(z_o, gi) for gi in range(3)]
    y_o, ycr, ltot = _odd_mix_fwd(z_o, [a[0] for a in att], [a[1] for a in att], w["conv_w"])
    x2 = _mm_nn("odd_out_proj", y_o, w["w_out_o"], 512, 1024, F32, resid=x1)
    dx2, loss8, g_final = _final_loss(x2, w["final_norm"], target)

    g_w_out_o = _mm_tn("odd_out_proj_dw", y_o, dx2, 1024, 512)
    dy_o = _mm_nt("odd_out_proj_dy", dx2, w["w_out_o"], 512, 1024, BF16)
    dz_o, dyc, dst, g_conv = _odd_mix_bwd(dy_o, z_o, ycr, w["conv_w"])
    for gi in range(3):
        dz_o = _attn_bwd(z_o, dyc[gi], ltot[gi], dst[gi], dz_o, gi)
    g_w_in_o, dh_o = None, []
    for o in range(3):
        cols, n_cols = _cols_of_order(o)
        g_w_in_o = _mm_tn(f"odd_in_proj_dw_o{o}", h_o[o], dz_o, D_MODEL, 512, col_map=cols, n_cols=n_cols, into=g_w_in_o)
        dh_o.append(_mm_nt(f"odd_in_proj_dh_o{o}", dz_o, w["w_in_o"], 1024, D_MODEL, F32, k_map=cols, nk=n_cols))
    dx1, g_odd_norm = _rms_bwd("rms_bwd_odd", dh_o[0], x1, w["odd_norm"], dx2, dh4=dh_o[1], dh16=dh_o[2])
    g_w_out_e = _mm_tn("even_out_proj_dw", y_e, dx1, 1024, 512)
    dy_e = _mm_nt("even_out_proj_dy", dx1, w["w_out_e"], 512, 1024, BF16)
    dz_e, g_pw, g_ps, g_ws, g_bs = _even_mix_bwd(dy_e, z_e, w["pool_w"], w["pool_scale"], wt, wtt, bs)
    g_w_in_e = _mm_tn("even_in_proj_dw", h_e, dz_e, 1280, 512)
    dh_e = _mm_nt("even_in_proj_dh", dz_e, w["w_in_e"], 1024, 1280, F32)
    dx0, g_even_norm = _rms_bwd("rms_bwd_even", dh_e, x, w["even_norm"], dx1)

    grads = dict(w_in_e=g_w_in_e, pool_w=g_pw, w_out_e=g_w_out_e, w_in_o=g_w_in_o, w_out_o=g_w_out_o,
                 even_norm=g_even_norm, pool_scale=g_ps, ws=g_ws, bs=g_bs[:, :4].T, final_norm=g_final,
                 odd_norm=g_odd_norm, conv_w=g_conv)
    return loss8[0, 0], dx0, grads


class _Big(NamedTuple):
    name: str
    full: tuple
    haxis: int
    kaxis: int
    sub: int


BIGS = (
    _Big("w_in_e", (1024, 5120), 0, 1, 2),
    _Big("pool_w", (4, 256, 256), 0, 1, 1),
    _Big("w_out_e", (2048, 1024), 1, 0, 1),
    _Big("w_in_o", (1024, 14336), 0, 1, 4),
    _Big("w_out_o", (2048, 1024), 1, 0, 1),
)
N_BIG = len(BIGS)


def _shape(b, half=False, shard=False):
    return tuple(n // (2 if (half and ax == b.haxis) else 1) // (4 if (shard and ax == b.kaxis) else 1)
                 for ax, n in enumerate(b.full))


def _at(ref, b, h=None, k=None):
    idx = []
    for ax, n in enumerate(b.full):
        if ax == b.haxis and h is not None:
            idx.append(pl.ds(h * (n // 2), n // 2))
        elif ax == b.kaxis and k is not None:
            idx.append(pl.ds(k * (n // 4), n // 4))
        else:
            idx.append(slice(None))
    return ref.at[tuple(idx)]


def _place():
    x, y, c = lax.axis_index("x"), lax.axis_index("y"), lax.axis_index("c")
    chips = [(1 - x, y), (x, 1 - y), (1 - x, 1 - y)]
    return x, y, c, 2 * x + y, chips, [2 * cx + cy for cx, cy in chips]


def _gather_weights(shards, tiny):
    def body(*refs):
        ins, tiny_in = refs[:N_BIG], refs[N_BIG]
        outs, tiny_out = refs[N_BIG + 1:2 * N_BIG + 1], refs[2 * N_BIG + 1]
        send, recv, loc = refs[2 * N_BIG + 2:]
        x, y, c, k_me, chips, ks = _place()
        sib = (x, y, 1 - c)

        def rc(src, dst, sem, to):
            return pltpu.make_async_remote_copy(src_ref=src, dst_ref=dst, send_sem=send.at[sem], recv_sem=recv.at[sem],
                                                device_id=to, device_id_type=MESH)

        started = []
        for a, b in enumerate(BIGS):
            own = pltpu.make_async_copy(ins[a], _at(outs[a], b, k=k_me), loc.at[a])
            own.start()
            started.append(own)
        own = pltpu.make_async_copy(tiny_in, tiny_out.at[k_me], loc.at[N_BIG])
        own.start()
        started.append(own)
        sends = []
        for j, chip in enumerate(chips):
            for a, b in enumerate(BIGS):
                sends.append(rc(_at(ins[a], b, h=c), _at(outs[a], b, h=c, k=k_me), 6 * a + j, (*chip, c)))
            sends.append(rc(tiny_in, tiny_out.at[k_me], 6 * N_BIG + j, (*chip, c)))
        for cp in sends:
            cp.start()
        for j in range(3):
            for a, b in enumerate(BIGS):
                win = _at(outs[a], b, h=c, k=ks[j])
                rc(win, win, 6 * a + j, sib).wait_recv()
                fwd = rc(win, win, 6 * a + 3 + j, sib)
                fwd.start()
                sends.append(fwd)
            rc(tiny_in, tiny_out.at[ks[j]], 6 * N_BIG + j, sib).wait_recv()
        for j in range(3):
            for a, b in enumerate(BIGS):
                win = _at(outs[a], b, h=1 - c, k=ks[j])
                rc(win, win, 6 * a + 3 + j, sib).wait_recv()
        for cp in sends:
            cp.wait_send()
        for cp in started:
            cp.wait()

    n_sem = 6 * N_BIG + 3
    return pl.pallas_call(
        body, name="gather_weights",
        in_specs=[ANY] * (N_BIG + 1), out_specs=[ANY] * (N_BIG + 1),
        out_shape=[jax.ShapeDtypeStruct(b.full, BF16) for b in BIGS] + [jax.ShapeDtypeStruct((4,) + tiny.shape, F32)],
        scratch_shapes=[pltpu.SemaphoreType.DMA((n_sem,)), pltpu.SemaphoreType.DMA((n_sem,)),
                        pltpu.SemaphoreType.DMA((N_BIG + 1,))],
    )(*shards, tiny)


def _swap_halves(grads):
    def body(*refs):
        ins, outs, send, recv = refs[:N_BIG], refs[N_BIG:2 * N_BIG], refs[2 * N_BIG], refs[2 * N_BIG + 1]
        x, y, c, _, _, _ = _place()
        copies = [pltpu.make_async_remote_copy(src_ref=_at(ins[a], b, h=1 - c), dst_ref=outs[a], send_sem=send.at[a],
                                               recv_sem=recv.at[a], device_id=(x, y, 1 - c), device_id_type=MESH)
                  for a, b in enumerate(BIGS)]
        for cp in copies:
            cp.start()
        for cp in copies:
            cp.wait()

    return pl.pallas_call(
        body, name="rs_swap_halves", in_specs=[ANY] * N_BIG, out_specs=[ANY] * N_BIG,
        out_shape=[jax.ShapeDtypeStruct(_shape(b, half=True), F32) for b in BIGS],
        scratch_shapes=[pltpu.SemaphoreType.DMA((N_BIG,)), pltpu.SemaphoreType.DMA((N_BIG,))],
    )(*grads)


def _blk(b):
    win = _shape(b, half=True, shard=True)
    return (win[0] // b.sub,) + win[1:]


def _bidx(b, h, k, st):
    idx = [0] * len(b.full)
    idx[b.haxis] = h
    idx[b.kaxis] = k
    idx[0] = idx[0] * b.sub + st
    return tuple(idx)


def _chip_sum(b, g, got, c_arr):
    blk = _blk(b)

    def body(c_ref, g_ref, r_ref, o_ref):
        del c_ref
        o_ref[...] = (g_ref[...] + r_ref[...]).astype(BF16)

    half = pl.BlockSpec(blk, lambda k, st, c_ref: _bidx(b, 0, k, st))
    return pl.pallas_call(
        body, name=f"rs_chip_sum_{b.name}",
        grid_spec=pltpu.PrefetchScalarGridSpec(
            num_scalar_prefetch=1, grid=(4, b.sub),
            in_specs=[pl.BlockSpec(blk, lambda k, st, c_ref: _bidx(b, c_ref[0], k, st)), half], out_specs=half),
        out_shape=jax.ShapeDtypeStruct(_shape(b, half=True), BF16),
        compiler_params=_cp(("arbitrary", "arbitrary"), 40),
    )(c_arr, g, got)


def _send_partials(sums):
    def body(*refs):
        ins, outs, send, recv = refs[:N_BIG], refs[N_BIG:2 * N_BIG], refs[2 * N_BIG], refs[2 * N_BIG + 1]
        _, _, c, _, chips, ks = _place()
        copies = [pltpu.make_async_remote_copy(src_ref=_at(ins[a], b, k=ks[j]), dst_ref=outs[a].at[j],
                                               send_sem=send.at[3 * a + j], recv_sem=recv.at[3 * a + j],
                                               device_id=(*chips[j], c), device_id_type=MESH)
                  for j in range(3) for a, b in enumerate(BIGS)]
        for cp in copies:
            cp.start()
        for cp in copies:
            cp.wait()

    return pl.pallas_call(
        body, name="rs_send_partials", in_specs=[ANY] * N_BIG, out_specs=[ANY] * N_BIG,
        out_shape=[jax.ShapeDtypeStruct((3,) + _shape(b, half=True, shard=True), BF16) for b in BIGS],
        scratch_shapes=[pltpu.SemaphoreType.DMA((3 * N_BIG,)), pltpu.SemaphoreType.DMA((3 * N_BIG,))],
    )(*sums)


def _shard_sum(b, mine, got, ck_arr):
    blk = _blk(b)

    def body(ck_ref, m_ref, r0, r1, r2, o_ref):
        del ck_ref
        o_ref[...] = (m_ref[...].astype(F32) + r0[...].astype(F32)) + (r1[...].astype(F32) + r2[...].astype(F32))

    def peer(j):
        return pl.BlockSpec((None,) + blk, lambda st, ck: (j,) + _bidx(b, 0, 0, st))

    return pl.pallas_call(
        body, name=f"rs_shard_sum_{b.name}",
        grid_spec=pltpu.PrefetchScalarGridSpec(
            num_scalar_prefetch=1, grid=(b.sub,),
            in_specs=[pl.BlockSpec(blk, lambda st, ck: _bidx(b, 0, ck[1], st)), peer(0), peer(1), peer(2)],
            out_specs=pl.BlockSpec(blk, lambda st, ck: _bidx(b, ck[0], 0, st))),
        out_shape=jax.ShapeDtypeStruct(_shape(b, shard=True), F32),
        compiler_params=_cp(("arbitrary",), 40),
    )(ck_arr, mine, got, got, got)


def _share_halves(gs):
    def body(*refs):
        ins, outs, send, recv = refs[:N_BIG], refs[N_BIG:2 * N_BIG], refs[2 * N_BIG], refs[2 * N_BIG + 1]
        del ins
        x, y, c, _, _, _ = _place()
        copies = [pltpu.make_async_remote_copy(src_ref=_at(outs[a], b, h=c), dst_ref=_at(outs[a], b, h=c),
                                               send_sem=send.at[a], recv_sem=recv.at[a], device_id=(x, y, 1 - c),
                                               device_id_type=MESH)
                  for a, b in enumerate(BIGS)]
        for cp in copies:
            cp.start()
        for cp in copies:
            cp.wait()

    return pl.pallas_call(
        body, name="rs_share_halves", in_specs=[ANY] * N_BIG, out_specs=[ANY] * N_BIG,
        out_shape=[jax.ShapeDtypeStruct(_shape(b, shard=True), F32) for b in BIGS],
        scratch_shapes=[pltpu.SemaphoreType.DMA((N_BIG,)), pltpu.SemaphoreType.DMA((N_BIG,))],
        input_output_aliases={a: a for a in range(N_BIG)},
    )(*gs)


def _gather_small(block):
    m_per, n = block.shape

    def body(x_ref, out_ref, send_sems, recv_sems, local_sem):
        x, y, c = lax.axis_index("x"), lax.axis_index("y"), lax.axis_index("c")
        me, sibling = (x, y, c), (x, y, 1 - c)
        chips = [(1 - x, y), (x, 1 - y), (1 - x, 1 - y)]

        def rows(px, py, pc):
            return out_ref.at[pl.ds((4 * px + 2 * py + pc) * m_per, m_per), :]

        def copy(k, blk, to, src=None):
            return pltpu.make_async_remote_copy(
                src_ref=rows(*blk) if src is None else src, dst_ref=rows(*blk), send_sem=send_sems.at[k],
                recv_sem=recv_sems.at[k], device_id=to, device_id_type=MESH)

        mine = pltpu.make_async_copy(x_ref, rows(*me), local_sem)
        mine.start()
        first = [copy(0, me, sibling, src=x_ref)]
        first += [copy(1 + j, me, (*chip, c), src=x_ref) for j, chip in enumerate(chips)]
        for cp in first:
            cp.start()
        passed = [copy(4 + j, (*chip, c), sibling) for j, chip in enumerate(chips)]
        for j, chip in enumerate(chips):
            copy(1 + j, (*chip, c), me).wait_recv()
            passed[j].start()
        copy(0, sibling, me).wait_recv()
        for j, chip in enumerate(chips):
            copy(4 + j, (*chip, 1 - c), me).wait_recv()
        for cp in first + passed:
            cp.wait_send()
        mine.wait()

    return pl.pallas_call(
        body, name="gather_small_grads",
        out_shape=jax.ShapeDtypeStruct((8 * m_per, n), block.dtype),
        in_specs=[pl.BlockSpec(memory_space=pltpu.VMEM)], out_specs=pl.BlockSpec(memory_space=pltpu.VMEM),
        scratch_shapes=[pltpu.SemaphoreType.DMA((7,)), pltpu.SemaphoreType.DMA((7,)), pltpu.SemaphoreType.DMA],
    )(block)


def _sum_small(stack):
    _, m_per, n = stack.shape

    def body(x_ref, o_ref):
        acc = x_ref[0]
        for dev in range(1, 8):
            acc = acc + x_ref[dev]
        o_ref[...] = acc

    return pl.pallas_call(body, name="sum_small_grads", out_shape=jax.ShapeDtypeStruct((m_per, n), F32))(stack)


def _adamw(name, w, g, m, v, rows):
    shape = w.shape

    def body(w_ref, g_ref, m_ref, v_ref, d_ref, mo_ref, vo_ref):
        gg = g_ref[...]
        mn = ADAM_B1 * m_ref[...] + (1.0 - ADAM_B1) * gg
        vn = ADAM_B2 * v_ref[...] + (1.0 - ADAM_B2) * (gg * gg)
        m_hat = mn / (1.0 - ADAM_B1 ** ADAM_STEP)
        v_hat = vn / (1.0 - ADAM_B2 ** ADAM_STEP)
        d_ref[...] = -ADAM_LR * (m_hat / (jnp.sqrt(v_hat) + ADAM_EPS) + ADAM_WD * w_ref[...])
        mo_ref[...] = mn
        vo_ref[...] = vn

    spec = pl.BlockSpec((rows,) + shape[1:], lambda i: (i,) + (0,) * (len(shape) - 1))
    return pl.pallas_call(
        body, name=name, grid=(shape[0] // rows,), in_specs=[spec] * 4, out_specs=[spec] * 3,
        out_shape=[jax.ShapeDtypeStruct(shape, F32)] * 3, compiler_params=_cp(("parallel",), 48),
    )(w, g, m, v)


ADAM_ROWS = dict(w_in_e=256, pool_w=4, w_out_e=256, w_in_o=128, w_out_o=256)


def _pack(parts, rows):
    flat = jnp.concatenate([p.reshape(-1).astype(F32) for p in parts])
    return jnp.pad(flat, (0, rows * 128 - flat.shape[0])).reshape(rows, 128)


def _unpack(buf, shapes):
    flat = buf.reshape(-1)
    out, off = [], 0
    for shp in shapes:
        n = 1
        for dim in shp:
            n *= dim
        out.append(flat[off:off + n].reshape(shp))
        off += n
    return out


WEIGHTS = ("even_norm", "even_w_in", "even_pool_w", "even_pool_scale", "even_ws", "even_bs", "even_w_out", "odd_norm",
           "odd_w_in", "odd_conv_w", "odd_w_out", "final_norm")
BIG_OF = dict(w_in_e="even_w_in", pool_w="even_pool_w", w_out_e="even_w_out", w_in_o="odd_w_in", w_out_o="odd_w_out")
SMALL = ("even_norm", "even_pool_scale", "even_ws", "even_bs", "final_norm", "odd_norm", "odd_conv_w")
SMALL_GRAD_ROWS = 576
SMALL_STATE_ROWS = 552


def kernel(x, even_norm, even_w_in, even_pool_w, even_pool_scale, even_ws, even_bs, even_w_out, odd_norm, odd_w_in, odd_conv_w, odd_w_out, final_norm, loss_target, m_even_norm, m_even_w_in, m_even_pool_w, m_even_pool_scale, m_even_ws, m_even_bs, m_even_w_out, m_odd_norm, m_odd_w_in, m_odd_conv_w, m_odd_w_out, m_final_norm, v_even_norm, v_even_w_in, v_even_pool_w, v_even_pool_scale, v_even_ws, v_even_bs, v_even_w_out, v_odd_norm, v_odd_w_in, v_odd_conv_w, v_odd_w_out, v_final_norm):
    wv = dict(zip(WEIGHTS, (even_norm, even_w_in, even_pool_w, even_pool_scale, even_ws, even_bs, even_w_out, odd_norm,
                            odd_w_in, odd_conv_w, odd_w_out, final_norm)))
    mv = dict(zip(WEIGHTS, (m_even_norm, m_even_w_in, m_even_pool_w, m_even_pool_scale, m_even_ws, m_even_bs,
                            m_even_w_out, m_odd_norm, m_odd_w_in, m_odd_conv_w, m_odd_w_out, m_final_norm)))
    vv = dict(zip(WEIGHTS, (v_even_norm, v_even_w_in, v_even_pool_w, v_even_pool_scale, v_even_ws, v_even_bs,
                            v_even_w_out, v_odd_norm, v_odd_w_in, v_odd_conv_w, v_odd_w_out, v_final_norm)))
    c = lax.axis_index("c")
    k_me = 2 * lax.axis_index("x") + lax.axis_index("y")

    shards = [wv[BIG_OF[b.name]][0].astype(BF16) for b in BIGS]
    tiny = jnp.concatenate([odd_conv_w[0], odd_norm], axis=0)
    *full, tiny_all = _gather_weights(shards, tiny)
    tiny_full = jnp.transpose(tiny_all, (1, 0, 2)).reshape(4, D_MODEL)
    w = dict(zip((b.name for b in BIGS), full))
    w.update(even_norm=even_norm, pool_scale=even_pool_scale, ws=even_ws[0], bs=even_bs[0],
             final_norm=final_norm.reshape(1, D_MODEL), conv_w=tiny_full[:3], odd_norm=tiny_full[3:4])

    loss, dx, g = _local_step(x[0], loss_target[0], w)
    loss = lax.psum(loss, ("x", "y", "c"))

    got = _swap_halves([g[b.name] for b in BIGS])
    c_arr = jnp.reshape(c, (1,)).astype(jnp.int32)
    sums = [_chip_sum(b, g[b.name], r, c_arr) for b, r in zip(BIGS, got)]
    parts = _send_partials(sums)
    ck_arr = jnp.stack([c, k_me]).astype(jnp.int32)
    halves = [_shard_sum(b, sm, p, ck_arr) for b, sm, p in zip(BIGS, sums, parts)]
    g_shard = dict(zip((b.name for b in BIGS), _share_halves(halves)))

    small_g = _pack([g["even_norm"], g["pool_scale"], g["ws"], g["bs"], g["final_norm"], g["odd_norm"], g["conv_w"]],
                    SMALL_GRAD_ROWS)
    small_g = _sum_small(_gather_small(small_g).reshape(8, SMALL_GRAD_ROWS, 128))
    g_en, g_ps, g_ws, g_bs, g_fn, g_on, g_cw = _unpack(
        small_g, [(1, D_MODEL), (1, D_MODEL), (1, 4, CHUNK, CHUNK), (1, 4, CHUNK), (D_MODEL,), (1, D_MODEL), (1, 3, D_MODEL)])
    g_on = lax.dynamic_slice(g_on, (0, k_me * 256), (1, 256))
    g_cw = lax.dynamic_slice(g_cw, (0, 0, k_me * 256), (1, 3, 256))
    grad = dict(even_norm=g_en, even_pool_scale=g_ps, even_ws=g_ws, even_bs=g_bs, final_norm=g_fn, odd_norm=g_on,
                odd_conv_w=g_cw)
    for b in BIGS:
        grad[BIG_OF[b.name]] = g_shard[b.name][None]

    delta, new_m, new_v = {}, {}, {}
    for b in BIGS:
        n = BIG_OF[b.name]
        d_, m_, v_ = _adamw(f"adamw_{b.name}", wv[n][0], g_shard[b.name], mv[n][0], vv[n][0], ADAM_ROWS[b.name])
        delta[n], new_m[n], new_v[n] = d_[None], m_[None], v_[None]
    shapes = [wv[n].shape for n in SMALL]
    packed = [_pack([src[n] for n in SMALL], SMALL_STATE_ROWS) for src in (wv, grad, mv, vv)]
    outs = _adamw("adamw_small", *packed, SMALL_STATE_ROWS)
    for dst, buf in zip((delta, new_m, new_v), outs):
        for n, arr in zip(SMALL, _unpack(buf, shapes)):
            dst[n] = arr

    return (loss, dx[None], *[grad[n] for n in WEIGHTS], *[delta[n] for n in WEIGHTS], *[new_m[n] for n in WEIGHTS],
            *[new_v[n] for n in WEIGHTS])
```

```python
from typing import NamedTuple

import jax
import jax.numpy as jnp
from jax import lax
from jax.experimental import pallas as pl
from jax.experimental.pallas import tpu as pltpu

F32, BF16 = jnp.float32, jnp.bfloat16

D_MODEL = 1024
EPS = 1e-6
NEG = -1e30
POOL_SIZES = (2, 4, 8, 16)
GROUP_W = 256
CHUNK = 128
DILATIONS = (1, 4, 16)
N_SLOTS = 8
HEAD_DIM = 128
ATTN_BLOCK = 128
SCALE = HEAD_DIM ** -0.5
EVEN_IN = 5120
ODD_IN = 14336
QKV_BLOCKS = 9
ODD_BLOCKS = ODD_IN // D_MODEL
SLOPES = tuple(tuple(2.0 ** (-8.0 * (g * N_SLOTS + s + 1) / (3 * N_SLOTS)) for s in range(N_SLOTS)) for g in range(3))

ADAM_LR, ADAM_B1, ADAM_B2, ADAM_EPS, ADAM_WD, ADAM_STEP = 0.001, 0.9, 0.999, 1e-08, 0.01, 10

HALO = 16
TS = 512
TM = 256
MM_ROWS = 1024
DW_IN_TOKENS = 2048
DW_OUT_TOKENS = 1024
MIB = 1 << 20
MESH = pl.DeviceIdType.MESH
ANY = pl.BlockSpec(memory_space=pl.ANY)


def _cp(sem, vmem_mib):
    return pltpu.CompilerParams(dimension_semantics=sem, vmem_limit_bytes=vmem_mib * MIB)


def _sig(x):
    return 1.0 / (1.0 + jnp.exp(-x))


def _win_sum(e, w, forward):
    n = e.shape[0]
    k = 1
    while k < w:
        e = e + pltpu.roll(e, (n - k) if forward else k, 0)
        k *= 2
    return e


def _mm_nn(name, a, b, tm, tn, out_dtype, resid=None, col_map=None, n_cols=None, into=None):
    m, k = a.shape
    n = b.shape[1]
    if col_map is None:
        col_map, n_cols = (lambda j: j), n // tn

    def body(*refs):
        a_ref, b_ref = refs[0], refs[1]
        acc = jnp.dot(a_ref[...].astype(BF16), b_ref[...], preferred_element_type=F32)
        if resid is not None:
            acc = acc + refs[2][...]
        o_ref = refs[-1]
        o_ref[...] = acc.astype(out_dtype)

    in_specs = [pl.BlockSpec((tm, k), lambda j, i: (i, 0)), pl.BlockSpec((k, tn), lambda j, i: (0, col_map(j)))]
    args = [a, b]
    if resid is not None:
        in_specs.append(pl.BlockSpec((tm, tn), lambda j, i: (i, col_map(j))))
        args.append(resid)
    aliases = {}
    if into is not None:
        aliases = {len(args): 0}
        in_specs.append(ANY)
        args.append(into)
    return pl.pallas_call(
        body, name=name, grid=(n_cols, m // tm), in_specs=in_specs,
        out_specs=pl.BlockSpec((tm, tn), lambda j, i: (i, col_map(j))),
        out_shape=jax.ShapeDtypeStruct((m, n), out_dtype), input_output_aliases=aliases,
        compiler_params=_cp(("parallel", "parallel"), 48),
    )(*args)


def _mm_nt(name, a, b, tm, tk, out_dtype, k_map=None, nk=None):
    m, k = a.shape
    n = b.shape[0]
    if k_map is None:
        k_map, nk = (lambda kk: kk), k // tk

    def body(a_ref, b_ref, o_ref, acc_ref):
        kk = pl.program_id(1)
        p = lax.dot_general(a_ref[...].astype(BF16), b_ref[...], (((1,), (1,)), ((), ())), preferred_element_type=F32)
        if nk == 1:
            o_ref[...] = p.astype(out_dtype)
        else:
            @pl.when(kk == 0)
            def _():
                acc_ref[...] = p

            @pl.when(kk > 0)
            def _():
                acc_ref[...] += p

            @pl.when(kk == nk - 1)
            def _():
                o_ref[...] = acc_ref[...].astype(out_dtype)

    return pl.pallas_call(
        body, name=name, grid=(m // tm, nk),
        in_specs=[pl.BlockSpec((tm, tk), lambda i, kk: (i, k_map(kk))), pl.BlockSpec((n, tk), lambda i, kk: (0, k_map(kk)))],
        out_specs=pl.BlockSpec((tm, n), lambda i, kk: (i, 0)),
        out_shape=jax.ShapeDtypeStruct((m, n), out_dtype),
        scratch_shapes=[pltpu.VMEM((tm, n) if nk > 1 else (8, 128), F32)],
        compiler_params=_cp(("parallel", "arbitrary"), 56),
    )(a, b)


def _mm_tn(name, a, g, tn, ts, col_map=None, n_cols=None, into=None):
    s, ka = a.shape
    n = g.shape[1]
    if col_map is None:
        col_map, n_cols = (lambda j: j), n // tn

    def body(a_ref, g_ref, *rest):
        o_ref = rest[-1]
        st = pl.program_id(1)
        p = lax.dot_general(a_ref[...], g_ref[...].astype(BF16), (((0,), (0,)), ((), ())), preferred_element_type=F32)

        @pl.when(st == 0)
        def _():
            o_ref[...] = p

        @pl.when(st > 0)
        def _():
            o_ref[...] += p

    in_specs = [pl.BlockSpec((ts, ka), lambda j, st: (st, 0)), pl.BlockSpec((ts, tn), lambda j, st: (st, col_map(j)))]
    args = [a, g]
    aliases = {}
    if into is not None:
        aliases = {2: 0}
        in_specs.append(ANY)
        args.append(into)
    return pl.pallas_call(
        body, name=name, grid=(n_cols, s // ts), in_specs=in_specs,
        out_specs=pl.BlockSpec((ka, tn), lambda j, st: (0, col_map(j))),
        out_shape=jax.ShapeDtypeStruct((ka, n), F32), input_output_aliases=aliases,
        compiler_params=_cp(("parallel", "arbitrary"), 56),
    )(*args)


def _rms_fwd(name, x, g):
    s = x.shape[0]

    def body(x_ref, g_ref, h_ref):
        xf = x_ref[...]
        r = lax.rsqrt(jnp.mean(xf * xf, axis=-1, keepdims=True) + EPS)
        h_ref[...] = (xf * r * g_ref[...]).astype(BF16)

    row = pl.BlockSpec((TS, D_MODEL), lambda i: (i, 0))
    return pl.pallas_call(
        body, name=name, grid=(s // TS,), in_specs=[row, pl.BlockSpec((1, D_MODEL), lambda i: (0, 0))], out_specs=row,
        out_shape=jax.ShapeDtypeStruct((s, D_MODEL), BF16), compiler_params=_cp(("parallel",), 32),
    )(x, g)


def _class_major(a, d):
    return a.reshape(d, a.shape[0] // d, a.shape[1])


def _class_spec(d, tile, width):
    return pl.BlockSpec((d, tile // d, width), lambda i: (0, i, 0))


LANES = 128


def _token_scratch(tile, width):
    return pltpu.VMEM((width // LANES, tile, LANES), F32)


def _put(scr, val):
    for c in range(scr.shape[0]):
        scr[c] = val[:, c * LANES:(c + 1) * LANES]


def _get(scr):
    return jnp.concatenate([scr[c] for c in range(scr.shape[0])], axis=1)


def _to_classes(ref3, scr, d, dtype):
    n = ref3.shape[1]
    for c in range(scr.shape[0]):
        for r in range(d):
            ref3[r, :, c * LANES:(c + 1) * LANES] = scr.at[c][pl.ds(r, n, stride=d), :].astype(dtype)


def _from_classes(scr, ref3, d):
    n = ref3.shape[1]
    for c in range(scr.shape[0]):
        for r in range(d):
            scr.at[c][pl.ds(r, n, stride=d), :] = ref3[r, :, c * LANES:(c + 1) * LANES].astype(F32)


def _rms_fwd_orders(name, x, g):
    s = x.shape[0]

    def body(x_ref, g_ref, h_ref, h4_ref, h16_ref, scr):
        xf = x_ref[...]
        r = lax.rsqrt(jnp.mean(xf * xf, axis=-1, keepdims=True) + EPS)
        h = xf * r * g_ref[...]
        h_ref[...] = h.astype(BF16)
        _put(scr, h)
        _to_classes(h4_ref, scr, 4, BF16)
        _to_classes(h16_ref, scr, 16, BF16)

    row = pl.BlockSpec((TS, D_MODEL), lambda i: (i, 0))
    h, h4, h16 = pl.pallas_call(
        body, name=name, grid=(s // TS,), in_specs=[row, pl.BlockSpec((1, D_MODEL), lambda i: (0, 0))],
        out_specs=[row, _class_spec(4, TS, D_MODEL), _class_spec(16, TS, D_MODEL)],
        out_shape=[jax.ShapeDtypeStruct((s, D_MODEL), BF16), jax.ShapeDtypeStruct((4, s // 4, D_MODEL), BF16),
                   jax.ShapeDtypeStruct((16, s // 16, D_MODEL), BF16)],
        scratch_shapes=[_token_scratch(TS, D_MODEL)],
        compiler_params=_cp(("parallel",), 32),
    )(x, g)
    return h, h4.reshape(s, D_MODEL), h16.reshape(s, D_MODEL)


def _rms_bwd(name, dh, x, g, dres, dh4=None, dh16=None):
    s = x.shape[0]
    extra = dh4 is not None

    def body(dh_ref, x_ref, g_ref, dres_ref, *rest):
        if extra:
            dh4_ref, dh16_ref, dx_ref, dg_ref, scr = rest
        else:
            dx_ref, dg_ref = rest
        xf = x_ref[...]
        r = lax.rsqrt(jnp.mean(xf * xf, axis=-1, keepdims=True) + EPS)
        xh = xf * r
        dhf = dh_ref[...]
        if extra:
            _from_classes(scr, dh4_ref, 4)
            dhf = dhf + _get(scr)
            _from_classes(scr, dh16_ref, 16)
            dhf = dhf + _get(scr)
        dxh = dhf * g_ref[...]
        dx_ref[...] = dres_ref[...] + r * (dxh - xh * jnp.mean(dxh * xh, axis=-1, keepdims=True))
        part = jnp.sum(dhf * xh, axis=0, keepdims=True)

        @pl.when(pl.program_id(0) == 0)
        def _():
            dg_ref[...] = part

        @pl.when(pl.program_id(0) > 0)
        def _():
            dg_ref[...] += part

    row = pl.BlockSpec((TS, D_MODEL), lambda i: (i, 0))
    vec = pl.BlockSpec((1, D_MODEL), lambda i: (0, 0))
    in_specs, args, scratch = [row, row, vec, row], [dh, x, g, dres], []
    if extra:
        in_specs += [_class_spec(4, TS, D_MODEL), _class_spec(16, TS, D_MODEL)]
        args += [_class_major(dh4, 4), _class_major(dh16, 16)]
        scratch = [_token_scratch(TS, D_MODEL)]
    return pl.pallas_call(
        body, name=name, grid=(s // TS,), in_specs=in_specs, out_specs=[row, vec],
        out_shape=[jax.ShapeDtypeStruct((s, D_MODEL), F32), jax.ShapeDtypeStruct((1, D_MODEL), F32)],
        scratch_shapes=scratch, compiler_params=_cp(("arbitrary",), 40),
    )(*args)


def _final_loss(x, g, target):
    s = x.shape[0]

    def body(x_ref, g_ref, t_ref, dx_ref, loss_ref, dg_ref):
        xf = x_ref[...]
        gg = g_ref[...]
        r = lax.rsqrt(jnp.mean(xf * xf, axis=-1, keepdims=True) + EPS)
        xh = xf * r
        e = xh * gg - t_ref[...]
        dy = e * (1.0 / D_MODEL)
        dxh = dy * gg
        dx_ref[...] = r * (dxh - xh * jnp.mean(dxh * xh, axis=-1, keepdims=True))
        lpart = 0.5 * jnp.sum(jnp.mean(e * e, axis=-1, keepdims=True), axis=0, keepdims=True)
        lpart = jnp.broadcast_to(lpart, (8, 128))
        gpart = jnp.sum(dy * xh, axis=0, keepdims=True)

        @pl.when(pl.program_id(0) == 0)
        def _():
            loss_ref[...] = lpart
            dg_ref[...] = gpart

        @pl.when(pl.program_id(0) > 0)
        def _():
            loss_ref[...] += lpart
            dg_ref[...] += gpart

    row = pl.BlockSpec((TS, D_MODEL), lambda i: (i, 0))
    vec = pl.BlockSpec((1, D_MODEL), lambda i: (0, 0))
    return pl.pallas_call(
        body, name="final_loss", grid=(s // TS,), in_specs=[row, vec, row],
        out_specs=[row, pl.BlockSpec((8, 128), lambda i: (0, 0)), vec],
        out_shape=[jax.ShapeDtypeStruct((s, D_MODEL), F32), jax.ShapeDtypeStruct((8, 128), F32),
                   jax.ShapeDtypeStruct((1, D_MODEL), F32)],
        compiler_params=_cp(("arbitrary",), 40),
    )(x, g, target)


def _zcol(c, tm=TM):
    return pl.BlockSpec((tm, D_MODEL), lambda i, c=c: (i, c))


def _prev_halo(c, tm=TM):
    return pl.BlockSpec((HALO, D_MODEL), lambda i, c=c: (jnp.maximum(i * (tm // HALO) - 1, 0), c))


def _next_halo(c, n_rows, tm=TM):
    last = n_rows // HALO - 1
    return pl.BlockSpec((HALO, D_MODEL), lambda i, c=c: (jnp.minimum((i + 1) * (tm // HALO), last), c))


def _full(shape):
    return pl.BlockSpec(shape, lambda i: (0,) * len(shape))


def _inv_count(first_row, n, w):
    t = first_row + lax.broadcasted_iota(jnp.int32, (n, 1), 0)
    return 1.0 / jnp.minimum(t + 1, w).astype(F32)


def _even_mix_fwd(z, pw, ps, wt, bs):
    s = z.shape[0]

    def body(a_ref, ga_ref, u_ref, v_ref, gb_ref, halo_ref, pw_ref, ps_ref, wt_ref, bs_ref, y_ref):
        i = pl.program_id(0)
        a = a_ref[...].astype(F32)
        halo = jnp.where(i > 0, halo_ref[...].astype(F32), 0.0)
        ext = jnp.concatenate([halo, a], axis=0)
        ga = ga_ref[...].astype(F32)
        sga = ga * _sig(ga)
        for g, w in enumerate(POOL_SIZES):
            cs = slice(g * GROUP_W, (g + 1) * GROUP_W)
            win = _win_sum(ext[:, cs], w, False)[HALO:]
            pooled = win * _inv_count(i * TM, TM, w) - a[:, cs]
            mixed = jnp.dot(pooled.astype(BF16), pw_ref[g], preferred_element_type=F32)
            y_ref[:, cs] = (mixed * ps_ref[:, cs] * sga[:, cs]).astype(BF16)
        gb = gb_ref[...].astype(F32)
        gate = u_ref[...].astype(F32) * (gb * _sig(gb))
        for ch in range(TM // CHUNK):
            rs = slice(ch * CHUNK, (ch + 1) * CHUNK)
            for g in range(4):
                cs = slice(g * GROUP_W, (g + 1) * GROUP_W)
                mixb = jnp.dot(wt_ref[g], v_ref[rs, cs], preferred_element_type=F32) + bs_ref[g]
                y_ref[rs, D_MODEL + g * GROUP_W:D_MODEL + (g + 1) * GROUP_W] = (gate[rs, cs] * mixb).astype(BF16)

    return pl.pallas_call(
        body, name="even_mix_fwd", grid=(s // TM,),
        in_specs=[_zcol(0), _zcol(1), _zcol(2), _zcol(3), _zcol(4), _prev_halo(0),
                  _full((4, GROUP_W, GROUP_W)), _full((1, D_MODEL)), _full((4, CHUNK, CHUNK)), _full((4, CHUNK, 1))],
        out_specs=pl.BlockSpec((TM, 2 * D_MODEL), lambda i: (i, 0)),
        out_shape=jax.ShapeDtypeStruct((s, 2 * D_MODEL), BF16),
        compiler_params=_cp(("parallel",), 48),
    )(z, z, z, z, z, z, pw, ps, wt, bs)


def _even_mix_bwd(dy, z, pw, ps, wt, wtt, bs):
    s = z.shape[0]
    n_tiles = s // TM

    def body(dy_ref, a_ref, ga_ref, u_ref, v_ref, gb_ref, halo_ref, dyn_ref, gan_ref, pw_ref, ps_ref, wt_ref, wtt_ref,
             bs_ref, dz_ref, dpw_ref, dps_ref, dws_ref, dbs_ref):
        i = pl.program_id(0)

        @pl.when(i == 0)
        def _():
            dpw_ref[...] = jnp.zeros_like(dpw_ref)
            dps_ref[...] = jnp.zeros_like(dps_ref)
            dws_ref[...] = jnp.zeros_like(dws_ref)
            dbs_ref[...] = jnp.zeros_like(dbs_ref)

        a = a_ref[...].astype(F32)
        halo = jnp.where(i > 0, halo_ref[...].astype(F32), 0.0)
        ext = jnp.concatenate([halo, a], axis=0)
        ga = ga_ref[...].astype(F32)
        sg = _sig(ga)
        sga = ga * sg
        dsga = sg * (1.0 + ga * (1.0 - sg))
        dya = dy_ref[:, :D_MODEL].astype(F32)
        gan = gan_ref[...].astype(F32)
        dmn_all = jnp.where(i < n_tiles - 1, dyn_ref[...].astype(F32) * ps_ref[...] * (gan * _sig(gan)), 0.0)
        for g, w in enumerate(POOL_SIZES):
            cs = slice(g * GROUP_W, (g + 1) * GROUP_W)
            inv = _inv_count(i * TM, TM, w)
            pooled = _win_sum(ext[:, cs], w, False)[HALO:] * inv - a[:, cs]
            pb = pooled.astype(BF16)
            mixed = jnp.dot(pb, pw_ref[g], preferred_element_type=F32)
            dyg = dya[:, cs]
            psg = ps_ref[:, cs]
            dm = (dyg * psg * sga[:, cs]).astype(BF16)
            dz_ref[:, D_MODEL + g * GROUP_W:D_MODEL + (g + 1) * GROUP_W] = (dyg * mixed * psg * dsga[:, cs]).astype(BF16)
            dps_ref[:, cs] += jnp.sum(dyg * mixed * sga[:, cs], axis=0, keepdims=True)
            dpw_ref[g] += lax.dot_general(pb, dm, (((0,), (0,)), ((), ())), preferred_element_type=F32)
            nt = (((1,), (1,)), ((), ()))
            dpool = lax.dot_general(dm, pw_ref[g], nt, preferred_element_type=F32)
            dpool_n = lax.dot_general(dmn_all[:, cs].astype(BF16), pw_ref[g], nt, preferred_element_type=F32)
            e = jnp.concatenate([dpool * inv, dpool_n * _inv_count((i + 1) * TM, HALO, w)], axis=0)
            dz_ref[:, cs] = (_win_sum(e, w, True)[:TM] - dpool).astype(BF16)

        gb = gb_ref[...].astype(F32)
        sg = _sig(gb)
        sgb = gb * sg
        dsgb = sg * (1.0 + gb * (1.0 - sg))
        u = u_ref[...].astype(F32)
        dyb = dy_ref[:, D_MODEL:].astype(F32)
        tril = lax.broadcasted_iota(jnp.int32, (CHUNK, CHUNK), 0) >= lax.broadcasted_iota(jnp.int32, (CHUNK, CHUNK), 1)
        lane = lax.broadcasted_iota(jnp.int32, (CHUNK, 128), 1)
        for ch in range(TM // CHUNK):
            rs = slice(ch * CHUNK, (ch + 1) * CHUNK)
            for g in range(4):
                cs = slice(g * GROUP_W, (g + 1) * GROUP_W)
                vb = v_ref[rs, cs]
                mixb = jnp.dot(wt_ref[g], vb, preferred_element_type=F32) + bs_ref[g]
                dyu = dyb[rs, cs] * u[rs, cs]
                dmix = dyu * sgb[rs, cs]
                dmb = dmix.astype(BF16)
                o = g * GROUP_W
                dz_ref[rs, 2 * D_MODEL + o:2 * D_MODEL + o + GROUP_W] = (dyb[rs, cs] * mixb * sgb[rs, cs]).astype(BF16)
                dz_ref[rs, 3 * D_MODEL + o:3 * D_MODEL + o + GROUP_W] = jnp.dot(
                    wtt_ref[g], dmb, preferred_element_type=F32).astype(BF16)
                dz_ref[rs, 4 * D_MODEL + o:4 * D_MODEL + o + GROUP_W] = (dyu * mixb * dsgb[rs, cs]).astype(BF16)
                dws = lax.dot_general(dmb, vb, (((1,), (1,)), ((), ())), preferred_element_type=F32)
                dws_ref[g] += jnp.where(tril, dws, 0.0)
                dbs_ref[...] += jnp.where(lane == g, jnp.sum(dmix, axis=1, keepdims=True), 0.0)

    return pl.pallas_call(
        body, name="even_mix_bwd", grid=(n_tiles,),
        in_specs=[pl.BlockSpec((TM, 2 * D_MODEL), lambda i: (i, 0)), _zcol(0), _zcol(1), _zcol(2), _zcol(3), _zcol(4),
                  _prev_halo(0), _next_halo(0, s), _next_halo(1, s),
                  _full((4, GROUP_W, GROUP_W)), _full((1, D_MODEL)), _full((4, CHUNK, CHUNK)), _full((4, CHUNK, CHUNK)),
                  _full((4, CHUNK, 1))],
        out_specs=[pl.BlockSpec((TM, EVEN_IN), lambda i: (i, 0)), _full((4, GROUP_W, GROUP_W)), _full((1, D_MODEL)),
                   _full((4, CHUNK, CHUNK)), _full((CHUNK, 128))],
        out_shape=[jax.ShapeDtypeStruct((s, EVEN_IN), BF16), jax.ShapeDtypeStruct((4, GROUP_W, GROUP_W), F32),
                   jax.ShapeDtypeStruct((1, D_MODEL), F32), jax.ShapeDtypeStruct((4, CHUNK, CHUNK), F32),
                   jax.ShapeDtypeStruct((CHUNK, 128), F32)],
        compiler_params=_cp(("arbitrary",), 56),
    )(dy, z, z, z, z, z, z, dy, z, pw, ps, wt, wtt, bs)


STAT_W = 128
Q_BLOCKS = 2
Q_ROWS = Q_BLOCKS * ATTN_BLOCK


def _band(d):
    row = lax.broadcasted_iota(jnp.int32, (ATTN_BLOCK, 2 * ATTN_BLOCK), 0)
    col = lax.broadcasted_iota(jnp.int32, (ATTN_BLOCK, 2 * ATTN_BLOCK), 1)
    steps = row + ATTN_BLOCK - col
    return (steps >= 0) & (steps <= ATTN_BLOCK), col >= ATTN_BLOCK, -(steps * d).astype(F32)


def _attn_fwd(z, gi):
    s = z.shape[0]
    d = DILATIONS[gi]
    nb = s // d // ATTN_BLOCK
    nq = nb // Q_BLOCKS

    def spec(which, prev=False):
        cb = which * 3 + gi
        if prev:
            return pl.BlockSpec((ATTN_BLOCK, D_MODEL), lambda r, i: (r * nb + jnp.maximum(Q_BLOCKS * i - 1, 0), cb))
        return pl.BlockSpec((Q_ROWS, D_MODEL), lambda r, i: (r * nq + i, cb))

    def body(q_ref, kp_ref, kc_ref, vp_ref, vc_ref, o_ref, lse_ref):
        i = pl.program_id(1)
        inner, own, negdist = _band(d)
        lane = lax.broadcasted_iota(jnp.int32, (ATTN_BLOCK, STAT_W), 1)
        for b in range(Q_BLOCKS):
            rows = slice(b * ATTN_BLOCK, (b + 1) * ATTN_BLOCK)
            valid = (inner & ((i > 0) | own)) if b == 0 else inner
            stat = jnp.zeros((ATTN_BLOCK, STAT_W), F32)
            for h in range(N_SLOTS):
                sl = slice(h * HEAD_DIM, (h + 1) * HEAD_DIM)
                if b == 0:
                    k = jnp.concatenate([kp_ref[:, sl], kc_ref[:ATTN_BLOCK, sl]], axis=0)
                    v = jnp.concatenate([vp_ref[:, sl], vc_ref[:ATTN_BLOCK, sl]], axis=0)
                else:
                    k = kc_ref[(b - 1) * ATTN_BLOCK:(b + 1) * ATTN_BLOCK, sl]
                    v = vc_ref[(b - 1) * ATTN_BLOCK:(b + 1) * ATTN_BLOCK, sl]
                sc = lax.dot_general(q_ref[rows, sl], k, (((1,), (1,)), ((), ())), preferred_element_type=F32) * SCALE
                sc = jnp.where(valid, sc + SLOPES[gi][h] * negdist, NEG)
                m = jnp.max(sc, axis=-1, keepdims=True)
                p = jnp.exp(sc - m)
                l = jnp.sum(p, axis=-1, keepdims=True)
                o = jnp.dot((p / l).astype(BF16), v, preferred_element_type=F32)
                o_ref[rows, sl] = o.astype(BF16)
                stat = jnp.where(lane == h, m + jnp.log(l), stat)
            lse_ref[rows, :] = stat

    return pl.pallas_call(
        body, name=f"attn_fwd_d{d}", grid=(d, nq),
        in_specs=[spec(0), spec(1, True), spec(1), spec(2, True), spec(2)],
        out_specs=[pl.BlockSpec((Q_ROWS, D_MODEL), lambda r, i: (r * nq + i, 0)),
                   pl.BlockSpec((Q_ROWS, STAT_W), lambda r, i: (r * nq + i, 0))],
        out_shape=[jax.ShapeDtypeStruct((s, D_MODEL), BF16), jax.ShapeDtypeStruct((s, STAT_W), F32)],
        compiler_params=_cp(("parallel", "parallel"), 32),
    )(z, z, z, z, z)


def _attn_bwd(z, dyc, ltot, dst, dz, gi):
    s = z.shape[0]
    d = DILATIONS[gi]
    nb = s // d // ATTN_BLOCK
    nq = nb // Q_BLOCKS
    n_steps = d * nq

    def rev(cb, width=D_MODEL, prev=False):
        if prev:
            return pl.BlockSpec((ATTN_BLOCK, width), lambda r, n: (r * nb + jnp.maximum(Q_BLOCKS * (nq - 1 - n) - 1, 0), cb))
        return pl.BlockSpec((Q_ROWS, width), lambda r, n: (r * nq + nq - 1 - n, cb))

    def body(q_ref, kp_ref, kc_ref, vp_ref, vc_ref, dy_ref, l_ref, d_ref, dz_in, dz_out, dq_s, dk_s, dv_s, ck_s, cv_s, sems):
        del dz_in
        r = pl.program_id(0)
        n = pl.program_id(1)
        i = nq - 1 - n
        step = r * nq + n

        def out_copy(src, which):
            rows = pl.ds(pl.multiple_of((r * nq + i) * Q_ROWS, Q_ROWS), Q_ROWS)
            return pltpu.make_async_copy(src, dz_out.at[rows, pl.ds((which * 3 + gi) * D_MODEL, D_MODEL)], sems.at[which])

        copies = [out_copy(dq_s, 0), out_copy(dk_s, 1), out_copy(dv_s, 2)]

        @pl.when(step > 0)
        def _():
            for cp in copies:
                cp.wait()

        @pl.when(n == 0)
        def _():
            ck_s[...] = jnp.zeros_like(ck_s)
            cv_s[...] = jnp.zeros_like(cv_s)

        row = lax.broadcasted_iota(jnp.int32, (Q_ROWS, Q_ROWS + ATTN_BLOCK), 0)
        col = lax.broadcasted_iota(jnp.int32, (Q_ROWS, Q_ROWS + ATTN_BLOCK), 1)
        steps = row + ATTN_BLOCK - col
        valid = (steps >= 0) & (steps <= ATTN_BLOCK) & ((i > 0) | (col >= ATTN_BLOCK))
        negdist = -(steps * d).astype(F32)
        nt = (((1,), (1,)), ((), ()))
        tn = (((0,), (0,)), ((), ()))
        for h in range(N_SLOTS):
            sl = slice(h * HEAD_DIM, (h + 1) * HEAD_DIM)
            q = q_ref[:, sl]
            k = jnp.concatenate([kp_ref[:, sl], kc_ref[:, sl]], axis=0)
            v = jnp.concatenate([vp_ref[:, sl], vc_ref[:, sl]], axis=0)
            dy = dy_ref[:, sl]
            sc = lax.dot_general(q, k, nt, preferred_element_type=F32) * SCALE + SLOPES[gi][h] * negdist
            p = jnp.where(valid, jnp.exp(sc - l_ref[:, h:h + 1]), 0.0)
            dp = lax.dot_general(dy, v, nt, preferred_element_type=F32)
            ds = (p * (dp - d_ref[:, h:h + 1])).astype(BF16)
            dq_s[:, sl] = (jnp.dot(ds, k, preferred_element_type=F32) * SCALE).astype(BF16)
            dk = lax.dot_general(ds, q, tn, preferred_element_type=F32) * SCALE
            dv = lax.dot_general(p.astype(BF16), dy, tn, preferred_element_type=F32)
            dk_s[:Q_ROWS - ATTN_BLOCK, sl] = dk[ATTN_BLOCK:Q_ROWS].astype(BF16)
            dv_s[:Q_ROWS - ATTN_BLOCK, sl] = dv[ATTN_BLOCK:Q_ROWS].astype(BF16)
            dk_s[Q_ROWS - ATTN_BLOCK:, sl] = (ck_s[:, sl] + dk[Q_ROWS:]).astype(BF16)
            dv_s[Q_ROWS - ATTN_BLOCK:, sl] = (cv_s[:, sl] + dv[Q_ROWS:]).astype(BF16)
            ck_s[:, sl] = dk[:ATTN_BLOCK]
            cv_s[:, sl] = dv[:ATTN_BLOCK]

        for cp in copies:
            cp.start()

        @pl.when(step == n_steps - 1)
        def _():
            for cp in copies:
                cp.wait()

    stage = pltpu.VMEM((Q_ROWS, D_MODEL), BF16)
    carry = pltpu.VMEM((ATTN_BLOCK, D_MODEL), F32)
    return pl.pallas_call(
        body, name=f"attn_bwd_d{d}", grid=(d, nq),
        in_specs=[rev(gi), rev(3 + gi, prev=True), rev(3 + gi), rev(6 + gi, prev=True), rev(6 + gi),
                  rev(0), rev(0, STAT_W), rev(0, STAT_W), ANY],
        out_specs=ANY,
        out_shape=jax.ShapeDtypeStruct((s, ODD_IN), BF16),
        scratch_shapes=[stage, stage, stage, carry, carry, pltpu.SemaphoreType.DMA((3,))],
        input_output_aliases={8: 0},
        compiler_params=_cp(("arbitrary", "arbitrary"), 32),
    )(z, z, z, z, z, dyc, ltot, dst, dz)


def _odd_mix_fwd(z, os_, lses, cw):
    s = z.shape[0]

    def body(o0, o1, o2, l0, l1, l2, gc_ref, db_ref, dc_ref, dx_ref, gd_ref, hc_ref, hx_ref, cw_ref, y_ref, yc_ref, lt_ref,
             lt4_ref, lt16_ref, scr_o, scr_l):
        i = pl.program_id(0)
        _from_classes(scr_l, l1, 4)
        lse1 = _get(scr_l)
        _from_classes(scr_l, l2, 16)
        ls = [l0[...], lse1, _get(scr_l)]
        lmax = jnp.maximum(jnp.maximum(ls[0], ls[1]), ls[2])
        es = [jnp.exp(l - lmax) for l in ls]
        den = es[0] + es[1] + es[2]
        alpha = [e / den for e in es]
        ltot = lmax + jnp.log(den)
        lt_ref[...] = ltot
        _put(scr_l, ltot)
        _to_classes(lt4_ref, scr_l, 4, F32)
        _to_classes(lt16_ref, scr_l, 16, F32)
        _from_classes(scr_o, o1, 4)
        og1 = _get(scr_o)
        _from_classes(scr_o, o2, 16)
        og = [o0[...].astype(F32), og1, _get(scr_o)]
        gc = gc_ref[...].astype(F32)
        gate = gc * _sig(gc)
        for h in range(N_SLOTS):
            sl = slice(h * HEAD_DIM, (h + 1) * HEAD_DIM)
            yc = (alpha[0][:, h:h + 1] * og[0][:, sl] + alpha[1][:, h:h + 1] * og[1][:, sl]
                  + alpha[2][:, h:h + 1] * og[2][:, sl])
            yc_ref[:, sl] = yc.astype(BF16)
            y_ref[:, sl] = (yc * gate[:, sl]).astype(BF16)

        zc = dc_ref[...].astype(F32) * dx_ref[...].astype(F32)
        halo = jnp.where(i > 0, hc_ref[...].astype(F32) * hx_ref[...].astype(F32), 0.0)
        ext = jnp.concatenate([halo, zc], axis=0)
        z1 = pltpu.roll(ext, 1, 0)[HALO:]
        z2 = pltpu.roll(ext, 2, 0)[HALO:]
        conv = cw_ref[0:1, :] * z2 + cw_ref[1:2, :] * z1 + cw_ref[2:3, :] * zc
        gd = gd_ref[...].astype(F32)
        y_ref[:, D_MODEL:] = (db_ref[...].astype(F32) * conv * (gd * _sig(gd))).astype(BF16)

    row = pl.BlockSpec((TM, D_MODEL), lambda i: (i, 0))
    stat = pl.BlockSpec((TM, STAT_W), lambda i: (i, 0))
    y, ycr, lt, lt4, lt16 = pl.pallas_call(
        body, name="odd_mix_fwd", grid=(s // TM,),
        in_specs=[row, _class_spec(4, TM, D_MODEL), _class_spec(16, TM, D_MODEL),
                  stat, _class_spec(4, TM, STAT_W), _class_spec(16, TM, STAT_W),
                  _zcol(9), _zcol(10), _zcol(11), _zcol(12), _zcol(13), _prev_halo(11), _prev_halo(12), _full((3, D_MODEL))],
        out_specs=[pl.BlockSpec((TM, 2 * D_MODEL), lambda i: (i, 0)), row, stat, _class_spec(4, TM, STAT_W),
                   _class_spec(16, TM, STAT_W)],
        out_shape=[jax.ShapeDtypeStruct((s, 2 * D_MODEL), BF16), jax.ShapeDtypeStruct((s, D_MODEL), BF16),
                   jax.ShapeDtypeStruct((s, STAT_W), F32), jax.ShapeDtypeStruct((4, s // 4, STAT_W), F32),
                   jax.ShapeDtypeStruct((16, s // 16, STAT_W), F32)],
        scratch_shapes=[_token_scratch(TM, D_MODEL), _token_scratch(TM, STAT_W)],
        compiler_params=_cp(("parallel",), 48),
    )(os_[0], _class_major(os_[1], 4), _class_major(os_[2], 16), lses[0], _class_major(lses[1], 4),
      _class_major(lses[2], 16), z, z, z, z, z, z, z, cw)
    return y, ycr, [lt, lt4.reshape(s, STAT_W), lt16.reshape(s, STAT_W)]


def _odd_mix_bwd(dy, z, ycr, cw):
    s = z.shape[0]
    n_tiles = s // TM
    rest = ODD_IN - QKV_BLOCKS * D_MODEL

    def body(dy_ref, yc_ref, gc_ref, db_ref, dc_ref, dx_ref, gd_ref, hc_ref, hx_ref, dyn_ref, dbn_ref, gdn_ref, cw_ref,
             dz_ref, dyc_ref, dyc4_ref, dyc16_ref, dd_ref, dd4_ref, dd16_ref, dcw_ref, stage, sem, scr_o, scr_l):
        i = pl.program_id(0)
        out = pltpu.make_async_copy(
            stage, dz_ref.at[pl.ds(pl.multiple_of(i * TM, TM), TM), pl.ds(QKV_BLOCKS * D_MODEL, rest)], sem)

        @pl.when(i > 0)
        def _():
            out.wait()

        dyc_in = dy_ref[:, :D_MODEL].astype(F32)
        gc = gc_ref[...].astype(F32)
        sg = _sig(gc)
        yc = yc_ref[...].astype(F32)
        dyc = dyc_in * (gc * sg)
        dyc_ref[...] = dyc.astype(BF16)
        _put(scr_o, dyc)
        _to_classes(dyc4_ref, scr_o, 4, BF16)
        _to_classes(dyc16_ref, scr_o, 16, BF16)
        stage[:, 0:D_MODEL] = (dyc_in * yc * (sg * (1.0 + gc * (1.0 - sg)))).astype(BF16)
        prod = dyc * yc
        lane = lax.broadcasted_iota(jnp.int32, (TM, STAT_W), 1)
        stat = jnp.zeros((TM, STAT_W), F32)
        for h in range(N_SLOTS):
            part = jnp.sum(prod[:, h * HEAD_DIM:(h + 1) * HEAD_DIM], axis=-1, keepdims=True)
            stat = jnp.where(lane == h, part, stat)
        dd_ref[...] = stat
        _put(scr_l, stat)
        _to_classes(dd4_ref, scr_l, 4, F32)
        _to_classes(dd16_ref, scr_l, 16, F32)

        dc = dc_ref[...].astype(F32)
        dx = dx_ref[...].astype(F32)
        zc = dc * dx
        halo = jnp.where(i > 0, hc_ref[...].astype(F32) * hx_ref[...].astype(F32), 0.0)
        ext = jnp.concatenate([halo, zc], axis=0)
        z1 = pltpu.roll(ext, 1, 0)[HALO:]
        z2 = pltpu.roll(ext, 2, 0)[HALO:]
        w0, w1, w2 = cw_ref[0:1, :], cw_ref[1:2, :], cw_ref[2:3, :]
        conv = w0 * z2 + w1 * z1 + w2 * zc
        gd = gd_ref[...].astype(F32)
        sg = _sig(gd)
        sgd = gd * sg
        db = db_ref[...].astype(F32)
        dyd = dy_ref[:, D_MODEL:].astype(F32)
        dconv = dyd * db * sgd
        gdn = gdn_ref[...].astype(F32)
        dconv_n = jnp.where(i < n_tiles - 1, dyn_ref[...].astype(F32) * dbn_ref[...].astype(F32) * (gdn * _sig(gdn)), 0.0)
        extn = jnp.concatenate([dconv, dconv_n], axis=0)
        nrow = TM + HALO
        dzc = w2 * dconv + w1 * pltpu.roll(extn, nrow - 1, 0)[:TM] + w0 * pltpu.roll(extn, nrow - 2, 0)[:TM]
        stage[:, D_MODEL:2 * D_MODEL] = (dyd * conv * sgd).astype(BF16)
        stage[:, 2 * D_MODEL:3 * D_MODEL] = (dzc * dx).astype(BF16)
        stage[:, 3 * D_MODEL:4 * D_MODEL] = (dzc * dc).astype(BF16)
        stage[:, 4 * D_MODEL:5 * D_MODEL] = (dyd * db * conv * (sg * (1.0 + gd * (1.0 - sg)))).astype(BF16)
        @pl.when(i == 0)
        def _():
            dcw_ref[...] = jnp.zeros_like(dcw_ref)

        for tap, shifted in enumerate((z2, z1, zc)):
            dcw_ref[tap:tap + 1, :] += jnp.sum(dconv * shifted, axis=0, keepdims=True)

        out.start()

        @pl.when(i == n_tiles - 1)
        def _():
            out.wait()

    row = pl.BlockSpec((TM, D_MODEL), lambda i: (i, 0))
    stat = pl.BlockSpec((TM, STAT_W), lambda i: (i, 0))
    dz, dyc, dyc4, dyc16, dd, dd4, dd16, g_conv = pl.pallas_call(
        body, name="odd_mix_bwd", grid=(n_tiles,),
        in_specs=[pl.BlockSpec((TM, 2 * D_MODEL), lambda i: (i, 0)), row, _zcol(9), _zcol(10), _zcol(11), _zcol(12), _zcol(13),
                  _prev_halo(11), _prev_halo(12), _next_halo(1, s), _next_halo(10, s), _next_halo(13, s), _full((3, D_MODEL))],
        out_specs=[ANY, row, _class_spec(4, TM, D_MODEL), _class_spec(16, TM, D_MODEL),
                   stat, _class_spec(4, TM, STAT_W), _class_spec(16, TM, STAT_W), _full((3, D_MODEL))],
        out_shape=[jax.ShapeDtypeStruct((s, ODD_IN), BF16), jax.ShapeDtypeStruct((s, D_MODEL), BF16),
                   jax.ShapeDtypeStruct((4, s // 4, D_MODEL), BF16), jax.ShapeDtypeStruct((16, s // 16, D_MODEL), BF16),
                   jax.ShapeDtypeStruct((s, STAT_W), F32), jax.ShapeDtypeStruct((4, s // 4, STAT_W), F32),
                   jax.ShapeDtypeStruct((16, s // 16, STAT_W), F32), jax.ShapeDtypeStruct((3, D_MODEL), F32)],
        scratch_shapes=[pltpu.VMEM((TM, rest), BF16), pltpu.SemaphoreType.DMA(()), _token_scratch(TM, D_MODEL),
                        _token_scratch(TM, STAT_W)],
        compiler_params=_cp(("arbitrary",), 48),
    )(dy, ycr, z, z, z, z, z, z, z, dy, z, z, cw)
    dyc = [dyc, dyc4.reshape(s, D_MODEL), dyc16.reshape(s, D_MODEL)]
    dd = [dd, dd4.reshape(s, STAT_W), dd16.reshape(s, STAT_W)]
    return dz, dyc, dd, g_conv


def _cols_of_order(order):
    if order == 0:
        return (lambda j: jnp.where(j < 3, 3 * j, j + 6)), 8
    return (lambda j: 3 * j + order), 3


def _local_step(x, target, w):
    tril = jnp.tril(jnp.ones((CHUNK, CHUNK), bool))
    wt = jnp.where(tril[None], w["ws"], 0.0).astype(BF16)
    wtt = jnp.swapaxes(wt, 1, 2)
    bs = w["bs"].reshape(4, CHUNK, 1)

    h_e = _rms_fwd("rms_fwd_even", x, w["even_norm"])
    z_e = _mm_nn("even_in_proj", h_e, w["w_in_e"], MM_ROWS, 1280, BF16)
    y_e = _even_mix_fwd(z_e, w["pool_w"], w["pool_scale"], wt, bs)
    x1 = _mm_nn("even_out_proj", y_e, w["w_out_e"], MM_ROWS, 1024, F32, resid=x)
    h_o = _rms_fwd_orders("rms_fwd_odd", x1, w["odd_norm"])
    z_o = None
    for o in range(3):
        cols, n_cols = _cols_of_order(o)
        z_o = _mm_nn(f"odd_in_proj_o{o}", h_o[o], w["w_in_o"], MM_ROWS, D_MODEL, BF16, col_map=cols, n_cols=n_cols,
                     into=z_o)
    att = [_attn_fwd(z_o, gi) for gi in range(3)]
    y_o, ycr, ltot = _odd_mix_fwd(z_o, [a[0] for a in att], [a[1] for a in att], w["conv_w"])
    x2 = _mm_nn("odd_out_proj", y_o, w["w_out_o"], MM_ROWS, 1024, F32, resid=x1)
    dx2, loss8, g_final = _final_loss(x2, w["final_norm"], target)

    g_w_out_o = _mm_tn("odd_out_proj_dw", y_o, dx2, 1024, DW_OUT_TOKENS)
    dy_o = _mm_nt("odd_out_proj_dy", dx2, w["w_out_o"], MM_ROWS, 1024, BF16)
    dz_o, dyc, dst, g_conv = _odd_mix_bwd(dy_o, z_o, ycr, w["conv_w"])
    for gi in range(3):
        dz_o = _attn_bwd(z_o, dyc[gi], ltot[gi], dst[gi], dz_o, gi)
    g_w_in_o, dh_o = None, []
    for o in range(3):
        cols, n_cols = _cols_of_order(o)
        g_w_in_o = _mm_tn(f"odd_in_proj_dw_o{o}", h_o[o], dz_o, D_MODEL, DW_IN_TOKENS, col_map=cols, n_cols=n_cols,
                          into=g_w_in_o)
        dh_o.append(_mm_nt(f"odd_in_proj_dh_o{o}", dz_o, w["w_in_o"], MM_ROWS, D_MODEL, F32, k_map=cols, nk=n_cols))
    dx1, g_odd_norm = _rms_bwd("rms_bwd_odd", dh_o[0], x1, w["odd_norm"], dx2, dh4=dh_o[1], dh16=dh_o[2])
    g_w_out_e = _mm_tn("even_out_proj_dw", y_e, dx1, 1024, DW_OUT_TOKENS)
    dy_e = _mm_nt("even_out_proj_dy", dx1, w["w_out_e"], MM_ROWS, 1024, BF16)
    dz_e, g_pw, g_ps, g_ws, g_bs = _even_mix_bwd(dy_e, z_e, w["pool_w"], w["pool_scale"], wt, wtt, bs)
    g_w_in_e = _mm_tn("even_in_proj_dw", h_e, dz_e, 1280, DW_IN_TOKENS)
    dh_e = _mm_nt("even_in_proj_dh", dz_e, w["w_in_e"], MM_ROWS, 1280, F32)
    dx0, g_even_norm = _rms_bwd("rms_bwd_even", dh_e, x, w["even_norm"], dx1)

    grads = dict(w_in_e=g_w_in_e, pool_w=g_pw, w_out_e=g_w_out_e, w_in_o=g_w_in_o, w_out_o=g_w_out_o,
                 even_norm=g_even_norm, pool_scale=g_ps, ws=g_ws, bs=g_bs[:, :4].T, final_norm=g_final,
                 odd_norm=g_odd_norm, conv_w=g_conv)
    return loss8[0, 0], dx0, grads


class _Big(NamedTuple):
    name: str
    full: tuple
    haxis: int
    kaxis: int
    sub: int


BIGS = (
    _Big("w_in_e", (1024, 5120), 0, 1, 2),
    _Big("pool_w", (4, 256, 256), 0, 1, 1),
    _Big("w_out_e", (2048, 1024), 1, 0, 1),
    _Big("w_in_o", (1024, 14336), 0, 1, 4),
    _Big("w_out_o", (2048, 1024), 1, 0, 1),
)
N_BIG = len(BIGS)


def _shape(b, half=False, shard=False):
    return tuple(n // (2 if (half and ax == b.haxis) else 1) // (4 if (shard and ax == b.kaxis) else 1)
                 for ax, n in enumerate(b.full))


def _at(ref, b, h=None, k=None):
    idx = []
    for ax, n in enumerate(b.full):
        if ax == b.haxis and h is not None:
            idx.append(pl.ds(h * (n // 2), n // 2))
        elif ax == b.kaxis and k is not None:
            idx.append(pl.ds(k * (n // 4), n // 4))
        else:
            idx.append(slice(None))
    return ref.at[tuple(idx)]


def _place():
    x, y, c = lax.axis_index("x"), lax.axis_index("y"), lax.axis_index("c")
    chips = [(1 - x, y), (x, 1 - y), (1 - x, 1 - y)]
    return x, y, c, 2 * x + y, chips, [2 * cx + cy for cx, cy in chips]


def _gather_weights(shards, tiny):
    def body(*refs):
        ins, tiny_in = refs[:N_BIG], refs[N_BIG]
        outs, tiny_out = refs[N_BIG + 1:2 * N_BIG + 1], refs[2 * N_BIG + 1]
        send, recv, loc = refs[2 * N_BIG + 2:]
        x, y, c, k_me, chips, ks = _place()
        sib = (x, y, 1 - c)

        def rc(src, dst, sem, to):
            return pltpu.make_async_remote_copy(src_ref=src, dst_ref=dst, send_sem=send.at[sem], recv_sem=recv.at[sem],
                                                device_id=to, device_id_type=MESH)

        started = []
        for a, b in enumerate(BIGS):
            own = pltpu.make_async_copy(ins[a], _at(outs[a], b, k=k_me), loc.at[a])
            own.start()
            started.append(own)
        own = pltpu.make_async_copy(tiny_in, tiny_out.at[k_me], loc.at[N_BIG])
        own.start()
        started.append(own)
        sends = []
        for j, chip in enumerate(chips):
            for a, b in enumerate(BIGS):
                sends.append(rc(_at(ins[a], b, h=c), _at(outs[a], b, h=c, k=k_me), 6 * a + j, (*chip, c)))
            sends.append(rc(tiny_in, tiny_out.at[k_me], 6 * N_BIG + j, (*chip, c)))
        for cp in sends:
            cp.start()
        for j in range(3):
            for a, b in enumerate(BIGS):
                win = _at(outs[a], b, h=c, k=ks[j])
                rc(win, win, 6 * a + j, sib).wait_recv()
                fwd = rc(win, win, 6 * a + 3 + j, sib)
                fwd.start()
                sends.append(fwd)
            rc(tiny_in, tiny_out.at[ks[j]], 6 * N_BIG + j, sib).wait_recv()
        for j in range(3):
            for a, b in enumerate(BIGS):
                win = _at(outs[a], b, h=1 - c, k=ks[j])
                rc(win, win, 6 * a + 3 + j, sib).wait_recv()
        for cp in sends:
            cp.wait_send()
        for cp in started:
            cp.wait()

    n_sem = 6 * N_BIG + 3
    return pl.pallas_call(
        body, name="gather_weights",
        in_specs=[ANY] * (N_BIG + 1), out_specs=[ANY] * (N_BIG + 1),
        out_shape=[jax.ShapeDtypeStruct(b.full, BF16) for b in BIGS] + [jax.ShapeDtypeStruct((4,) + tiny.shape, F32)],
        scratch_shapes=[pltpu.SemaphoreType.DMA((n_sem,)), pltpu.SemaphoreType.DMA((n_sem,)),
                        pltpu.SemaphoreType.DMA((N_BIG + 1,))],
    )(*shards, tiny)


def _swap_halves(grads):
    def body(*refs):
        ins, outs, send, recv = refs[:N_BIG], refs[N_BIG:2 * N_BIG], refs[2 * N_BIG], refs[2 * N_BIG + 1]
        x, y, c, _, _, _ = _place()
        copies = [pltpu.make_async_remote_copy(src_ref=_at(ins[a], b, h=1 - c), dst_ref=outs[a], send_sem=send.at[a],
                                               recv_sem=recv.at[a], device_id=(x, y, 1 - c), device_id_type=MESH)
                  for a, b in enumerate(BIGS)]
        for cp in copies:
            cp.start()
        for cp in copies:
            cp.wait()

    return pl.pallas_call(
        body, name="rs_swap_halves", in_specs=[ANY] * N_BIG, out_specs=[ANY] * N_BIG,
        out_shape=[jax.ShapeDtypeStruct(_shape(b, half=True), F32) for b in BIGS],
        scratch_shapes=[pltpu.SemaphoreType.DMA((N_BIG,)), pltpu.SemaphoreType.DMA((N_BIG,))],
    )(*grads)


def _blk(b):
    win = _shape(b, half=True, shard=True)
    return (win[0] // b.sub,) + win[1:]


def _bidx(b, h, k, st):
    idx = [0] * len(b.full)
    idx[b.haxis] = h
    idx[b.kaxis] = k
    idx[0] = idx[0] * b.sub + st
    return tuple(idx)


def _chip_sum(b, g, got, c_arr):
    blk = _blk(b)

    def body(c_ref, g_ref, r_ref, o_ref):
        del c_ref
        o_ref[...] = (g_ref[...] + r_ref[...]).astype(BF16)

    half = pl.BlockSpec(blk, lambda k, st, c_ref: _bidx(b, 0, k, st))
    return pl.pallas_call(
        body, name=f"rs_chip_sum_{b.name}",
        grid_spec=pltpu.PrefetchScalarGridSpec(
            num_scalar_prefetch=1, grid=(4, b.sub),
            in_specs=[pl.BlockSpec(blk, lambda k, st, c_ref: _bidx(b, c_ref[0], k, st)), half], out_specs=half),
        out_shape=jax.ShapeDtypeStruct(_shape(b, half=True), BF16),
        compiler_params=_cp(("arbitrary", "arbitrary"), 40),
    )(c_arr, g, got)


def _send_partials(sums):
    def body(*refs):
        ins, outs, send, recv = refs[:N_BIG], refs[N_BIG:2 * N_BIG], refs[2 * N_BIG], refs[2 * N_BIG + 1]
        _, _, c, _, chips, ks = _place()
        copies = [pltpu.make_async_remote_copy(src_ref=_at(ins[a], b, k=ks[j]), dst_ref=outs[a].at[j],
                                               send_sem=send.at[3 * a + j], recv_sem=recv.at[3 * a + j],
                                               device_id=(*chips[j], c), device_id_type=MESH)
                  for j in range(3) for a, b in enumerate(BIGS)]
        for cp in copies:
            cp.start()
        for cp in copies:
            cp.wait()

    return pl.pallas_call(
        body, name="rs_send_partials", in_specs=[ANY] * N_BIG, out_specs=[ANY] * N_BIG,
        out_shape=[jax.ShapeDtypeStruct((3,) + _shape(b, half=True, shard=True), BF16) for b in BIGS],
        scratch_shapes=[pltpu.SemaphoreType.DMA((3 * N_BIG,)), pltpu.SemaphoreType.DMA((3 * N_BIG,))],
    )(*sums)


def _shard_sum(b, mine, got, ck_arr):
    blk = _blk(b)

    def body(ck_ref, m_ref, r0, r1, r2, o_ref):
        del ck_ref
        o_ref[...] = (m_ref[...].astype(F32) + r0[...].astype(F32)) + (r1[...].astype(F32) + r2[...].astype(F32))

    def peer(j):
        return pl.BlockSpec((None,) + blk, lambda st, ck: (j,) + _bidx(b, 0, 0, st))

    return pl.pallas_call(
        body, name=f"rs_shard_sum_{b.name}",
        grid_spec=pltpu.PrefetchScalarGridSpec(
            num_scalar_prefetch=1, grid=(b.sub,),
            in_specs=[pl.BlockSpec(blk, lambda st, ck: _bidx(b, 0, ck[1], st)), peer(0), peer(1), peer(2)],
            out_specs=pl.BlockSpec(blk, lambda st, ck: _bidx(b, ck[0], 0, st))),
        out_shape=jax.ShapeDtypeStruct(_shape(b, shard=True), F32),
        compiler_params=_cp(("arbitrary",), 40),
    )(ck_arr, mine, got, got, got)


def _share_halves(gs):
    def body(*refs):
        ins, outs, send, recv = refs[:N_BIG], refs[N_BIG:2 * N_BIG], refs[2 * N_BIG], refs[2 * N_BIG + 1]
        del ins
        x, y, c, _, _, _ = _place()
        copies = [pltpu.make_async_remote_copy(src_ref=_at(outs[a], b, h=c), dst_ref=_at(outs[a], b, h=c),
                                               send_sem=send.at[a], recv_sem=recv.at[a], device_id=(x, y, 1 - c),
                                               device_id_type=MESH)
                  for a, b in enumerate(BIGS)]
        for cp in copies:
            cp.start()
        for cp in copies:
            cp.wait()

    return pl.pallas_call(
        body, name="rs_share_halves", in_specs=[ANY] * N_BIG, out_specs=[ANY] * N_BIG,
        out_shape=[jax.ShapeDtypeStruct(_shape(b, shard=True), F32) for b in BIGS],
        scratch_shapes=[pltpu.SemaphoreType.DMA((N_BIG,)), pltpu.SemaphoreType.DMA((N_BIG,))],
        input_output_aliases={a: a for a in range(N_BIG)},
    )(*gs)


def _gather_small(block):
    m_per, n = block.shape

    def body(x_ref, out_ref, send_sems, recv_sems, local_sem):
        x, y, c = lax.axis_index("x"), lax.axis_index("y"), lax.axis_index("c")
        me, sibling = (x, y, c), (x, y, 1 - c)
        chips = [(1 - x, y), (x, 1 - y), (1 - x, 1 - y)]

        def rows(px, py, pc):
            return out_ref.at[pl.ds((4 * px + 2 * py + pc) * m_per, m_per), :]

        def copy(k, blk, to, src=None):
            return pltpu.make_async_remote_copy(
                src_ref=rows(*blk) if src is None else src, dst_ref=rows(*blk), send_sem=send_sems.at[k],
                recv_sem=recv_sems.at[k], device_id=to, device_id_type=MESH)

        mine = pltpu.make_async_copy(x_ref, rows(*me), local_sem)
        mine.start()
        first = [copy(0, me, sibling, src=x_ref)]
        first += [copy(1 + j, me, (*chip, c), src=x_ref) for j, chip in enumerate(chips)]
        for cp in first:
            cp.start()
        passed = [copy(4 + j, (*chip, c), sibling) for j, chip in enumerate(chips)]
        for j, chip in enumerate(chips):
            copy(1 + j, (*chip, c), me).wait_recv()
            passed[j].start()
        copy(0, sibling, me).wait_recv()
        for j, chip in enumerate(chips):
            copy(4 + j, (*chip, 1 - c), me).wait_recv()
        for cp in first + passed:
            cp.wait_send()
        mine.wait()

    return pl.pallas_call(
        body, name="gather_small_grads",
        out_shape=jax.ShapeDtypeStruct((8 * m_per, n), block.dtype),
        in_specs=[pl.BlockSpec(memory_space=pltpu.VMEM)], out_specs=pl.BlockSpec(memory_space=pltpu.VMEM),
        scratch_shapes=[pltpu.SemaphoreType.DMA((7,)), pltpu.SemaphoreType.DMA((7,)), pltpu.SemaphoreType.DMA],
    )(block)


def _sum_small(stack):
    _, m_per, n = stack.shape

    def body(x_ref, o_ref):
        acc = x_ref[0]
        for dev in range(1, 8):
            acc = acc + x_ref[dev]
        o_ref[...] = acc

    return pl.pallas_call(body, name="sum_small_grads", out_shape=jax.ShapeDtypeStruct((m_per, n), F32))(stack)


def _adamw(name, w, g, m, v, rows):
    shape = w.shape

    def body(w_ref, g_ref, m_ref, v_ref, d_ref, mo_ref, vo_ref):
        gg = g_ref[...]
        mn = ADAM_B1 * m_ref[...] + (1.0 - ADAM_B1) * gg
        vn = ADAM_B2 * v_ref[...] + (1.0 - ADAM_B2) * (gg * gg)
        m_hat = mn / (1.0 - ADAM_B1 ** ADAM_STEP)
        v_hat = vn / (1.0 - ADAM_B2 ** ADAM_STEP)
        d_ref[...] = -ADAM_LR * (m_hat / (jnp.sqrt(v_hat) + ADAM_EPS) + ADAM_WD * w_ref[...])
        mo_ref[...] = mn
        vo_ref[...] = vn

    spec = pl.BlockSpec((rows,) + shape[1:], lambda i: (i,) + (0,) * (len(shape) - 1))
    return pl.pallas_call(
        body, name=name, grid=(shape[0] // rows,), in_specs=[spec] * 4, out_specs=[spec] * 3,
        out_shape=[jax.ShapeDtypeStruct(shape, F32)] * 3, compiler_params=_cp(("parallel",), 48),
    )(w, g, m, v)


ADAM_ROWS = dict(w_in_e=256, pool_w=4, w_out_e=256, w_in_o=128, w_out_o=256)


def _pack(parts, rows):
    flat = jnp.concatenate([p.reshape(-1).astype(F32) for p in parts])
    return jnp.pad(flat, (0, rows * 128 - flat.shape[0])).reshape(rows, 128)


def _unpack(buf, shapes):
    flat = buf.reshape(-1)
    out, off = [], 0
    for shp in shapes:
        n = 1
        for dim in shp:
            n *= dim
        out.append(flat[off:off + n].reshape(shp))
        off += n
    return out


WEIGHTS = ("even_norm", "even_w_in", "even_pool_w", "even_pool_scale", "even_ws", "even_bs", "even_w_out", "odd_norm",
           "odd_w_in", "odd_conv_w", "odd_w_out", "final_norm")
BIG_OF = dict(w_in_e="even_w_in", pool_w="even_pool_w", w_out_e="even_w_out", w_in_o="odd_w_in", w_out_o="odd_w_out")
SMALL = ("even_norm", "even_pool_scale", "even_ws", "even_bs", "final_norm", "odd_norm", "odd_conv_w")
SMALL_GRAD_ROWS = 576
SMALL_STATE_ROWS = 552


def kernel(x, even_norm, even_w_in, even_pool_w, even_pool_scale, even_ws, even_bs, even_w_out, odd_norm, odd_w_in, odd_conv_w, odd_w_out, final_norm, loss_target, m_even_norm, m_even_w_in, m_even_pool_w, m_even_pool_scale, m_even_ws, m_even_bs, m_even_w_out, m_odd_norm, m_odd_w_in, m_odd_conv_w, m_odd_w_out, m_final_norm, v_even_norm, v_even_w_in, v_even_pool_w, v_even_pool_scale, v_even_ws, v_even_bs, v_even_w_out, v_odd_norm, v_odd_w_in, v_odd_conv_w, v_odd_w_out, v_final_norm):
    wv = dict(zip(WEIGHTS, (even_norm, even_w_in, even_pool_w, even_pool_scale, even_ws, even_bs, even_w_out, odd_norm,
                            odd_w_in, odd_conv_w, odd_w_out, final_norm)))
    mv = dict(zip(WEIGHTS, (m_even_norm, m_even_w_in, m_even_pool_w, m_even_pool_scale, m_even_ws, m_even_bs,
                            m_even_w_out, m_odd_norm, m_odd_w_in, m_odd_conv_w, m_odd_w_out, m_final_norm)))
    vv = dict(zip(WEIGHTS, (v_even_norm, v_even_w_in, v_even_pool_w, v_even_pool_scale, v_even_ws, v_even_bs,
                            v_even_w_out, v_odd_norm, v_odd_w_in, v_odd_conv_w, v_odd_w_out, v_final_norm)))
    c = lax.axis_index("c")
    k_me = 2 * lax.axis_index("x") + lax.axis_index("y")

    shards = [wv[BIG_OF[b.name]][0].astype(BF16) for b in BIGS]
    tiny = jnp.concatenate([odd_conv_w[0], odd_norm], axis=0)
    *full, tiny_all = _gather_weights(shards, tiny)
    tiny_full = jnp.transpose(tiny_all, (1, 0, 2)).reshape(4, D_MODEL)
    w = dict(zip((b.name for b in BIGS), full))
    w.update(even_norm=even_norm, pool_scale=even_pool_scale, ws=even_ws[0], bs=even_bs[0],
             final_norm=final_norm.reshape(1, D_MODEL), conv_w=tiny_full[:3], odd_norm=tiny_full[3:4])

    loss, dx, g = _local_step(x[0], loss_target[0], w)
    loss = lax.psum(loss, ("x", "y", "c"))

    got = _swap_halves([g[b.name] for b in BIGS])
    c_arr = jnp.reshape(c, (1,)).astype(jnp.int32)
    sums = [_chip_sum(b, g[b.name], r, c_arr) for b, r in zip(BIGS, got)]
    parts = _send_partials(sums)
    ck_arr = jnp.stack([c, k_me]).astype(jnp.int32)
    halves = [_shard_sum(b, sm, p, ck_arr) for b, sm, p in zip(BIGS, sums, parts)]
    g_shard = dict(zip((b.name for b in BIGS), _share_halves(halves)))

    small_g = _pack([g["even_norm"], g["pool_scale"], g["ws"], g["bs"], g["final_norm"], g["odd_norm"], g["conv_w"]],
                    SMALL_GRAD_ROWS)
    small_g = _sum_small(_gather_small(small_g).reshape(8, SMALL_GRAD_ROWS, 128))
    g_en, g_ps, g_ws, g_bs, g_fn, g_on, g_cw = _unpack(
        small_g, [(1, D_MODEL), (1, D_MODEL), (1, 4, CHUNK, CHUNK), (1, 4, CHUNK), (D_MODEL,), (1, D_MODEL), (1, 3, D_MODEL)])
    g_on = lax.dynamic_slice(g_on, (0, k_me * 256), (1, 256))
    g_cw = lax.dynamic_slice(g_cw, (0, 0, k_me * 256), (1, 3, 256))
    grad = dict(even_norm=g_en, even_pool_scale=g_ps, even_ws=g_ws, even_bs=g_bs, final_norm=g_fn, odd_norm=g_on,
                odd_conv_w=g_cw)
    for b in BIGS:
        grad[BIG_OF[b.name]] = g_shard[b.name][None]

    delta, new_m, new_v = {}, {}, {}
    for b in BIGS:
        n = BIG_OF[b.name]
        d_, m_, v_ = _adamw(f"adamw_{b.name}", wv[n][0], g_shard[b.name], mv[n][0], vv[n][0], ADAM_ROWS[b.name])
        delta[n], new_m[n], new_v[n] = d_[None], m_[None], v_[None]
    shapes = [wv[n].shape for n in SMALL]
    packed = [_pack([src[n] for n in SMALL], SMALL_STATE_ROWS) for src in (wv, grad, mv, vv)]
    outs = _adamw("adamw_small", *packed, SMALL_STATE_ROWS)
    for dst, buf in zip((delta, new_m, new_v), outs):
        for n, arr in zip(SMALL, _unpack(buf, shapes)):
            dst[n] = arr

    return (loss, dx[None], *[grad[n] for n in WEIGHTS], *[delta[n] for n in WEIGHTS], *[new_m[n] for n in WEIGHTS],
            *[new_v[n] for n in WEIGHTS])
```

```python
from typing import NamedTuple

import jax
import jax.numpy as jnp
from jax import lax
from jax.experimental import pallas as pl
from jax.experimental.pallas import tpu as pltpu

F32, BF16 = jnp.float32, jnp.bfloat16

D_MODEL = 1024
EPS = 1e-6
NEG = -1e30
POOL_SIZES = (2, 4, 8, 16)
GROUP_W = 256
CHUNK = 128
DILATIONS = (1, 4, 16)
N_SLOTS = 8
HEAD_DIM = 128
ATTN_BLOCK = 128
SCALE = HEAD_DIM ** -0.5
EVEN_IN = 5120
ODD_IN = 14336
QKV_BLOCKS = 9
ODD_BLOCKS = ODD_IN // D_MODEL
SLOPES = tuple(tuple(2.0 ** (-8.0 * (g * N_SLOTS + s + 1) / (3 * N_SLOTS)) for s in range(N_SLOTS)) for g in range(3))

ADAM_LR, ADAM_B1, ADAM_B2, ADAM_EPS, ADAM_WD, ADAM_STEP = 0.001, 0.9, 0.999, 1e-08, 0.01, 10

HALO = 16
TS = 512
TM = 256
MM_ROWS = 1024
DW_IN_TOKENS = 2048
DW_OUT_TOKENS = 1024
MIB = 1 << 20
MESH = pl.DeviceIdType.MESH
ANY = pl.BlockSpec(memory_space=pl.ANY)


def _cp(sem, vmem_mib):
    return pltpu.CompilerParams(dimension_semantics=sem, vmem_limit_bytes=vmem_mib * MIB)


def _sig(x):
    return 1.0 / (1.0 + jnp.exp(-x))


def _win_sum(e, w, forward):
    n = e.shape[0]
    k = 1
    while k < w:
        e = e + pltpu.roll(e, (n - k) if forward else k, 0)
        k *= 2
    return e


def _mm_nn(name, a, b, tm, tn, out_dtype, resid=None, col_map=None, n_cols=None, into=None):
    m, k = a.shape
    n = b.shape[1]
    if col_map is None:
        col_map, n_cols = (lambda j: j), n // tn

    def body(*refs):
        a_ref, b_ref = refs[0], refs[1]
        acc = jnp.dot(a_ref[...].astype(BF16), b_ref[...], preferred_element_type=F32)
        if resid is not None:
            acc = acc + refs[2][...]
        o_ref = refs[-1]
        o_ref[...] = acc.astype(out_dtype)

    in_specs = [pl.BlockSpec((tm, k), lambda j, i: (i, 0)), pl.BlockSpec((k, tn), lambda j, i: (0, col_map(j)))]
    args = [a, b]
    if resid is not None:
        in_specs.append(pl.BlockSpec((tm, tn), lambda j, i: (i, col_map(j))))
        args.append(resid)
    aliases = {}
    if into is not None:
        aliases = {len(args): 0}
        in_specs.append(ANY)
        args.append(into)
    return pl.pallas_call(
        body, name=name, grid=(n_cols, m // tm), in_specs=in_specs,
        out_specs=pl.BlockSpec((tm, tn), lambda j, i: (i, col_map(j))),
        out_shape=jax.ShapeDtypeStruct((m, n), out_dtype), input_output_aliases=aliases,
        compiler_params=_cp(("parallel", "parallel"), 48),
    )(*args)


def _mm_nt(name, a, b, tm, tk, out_dtype, k_map=None, nk=None):
    m, k = a.shape
    n = b.shape[0]
    if k_map is None:
        k_map, nk = (lambda kk: kk), k // tk

    def body(a_ref, b_ref, o_ref, acc_ref):
        kk = pl.program_id(1)
        p = lax.dot_general(a_ref[...].astype(BF16), b_ref[...], (((1,), (1,)), ((), ())), preferred_element_type=F32)
        if nk == 1:
            o_ref[...] = p.astype(out_dtype)
        else:
            @pl.when(kk == 0)
            def _():
                acc_ref[...] = p

            @pl.when(kk > 0)
            def _():
                acc_ref[...] += p

            @pl.when(kk == nk - 1)
            def _():
                o_ref[...] = acc_ref[...].astype(out_dtype)

    return pl.pallas_call(
        body, name=name, grid=(m // tm, nk),
        in_specs=[pl.BlockSpec((tm, tk), lambda i, kk: (i, k_map(kk))), pl.BlockSpec((n, tk), lambda i, kk: (0, k_map(kk)))],
        out_specs=pl.BlockSpec((tm, n), lambda i, kk: (i, 0)),
        out_shape=jax.ShapeDtypeStruct((m, n), out_dtype),
        scratch_shapes=[pltpu.VMEM((tm, n) if nk > 1 else (8, 128), F32)],
        compiler_params=_cp(("parallel", "arbitrary"), 56),
    )(a, b)


def _mm_tn(name, a, g, tn, ts, col_map=None, n_cols=None, into=None, after=None):
    s, ka = a.shape
    n = g.shape[1]
    if col_map is None:
        col_map, n_cols = (lambda j: j), n // tn

    def body(a_ref, g_ref, *rest):
        o_ref = rest[-1]
        st = pl.program_id(1)
        p = lax.dot_general(a_ref[...], g_ref[...].astype(BF16), (((0,), (0,)), ((), ())), preferred_element_type=F32)

        @pl.when(st == 0)
        def _():
            o_ref[...] = p

        @pl.when(st > 0)
        def _():
            o_ref[...] += p

    in_specs = [pl.BlockSpec((ts, ka), lambda j, st: (st, 0)), pl.BlockSpec((ts, tn), lambda j, st: (st, col_map(j)))]
    args = [a, g]
    aliases = {}
    if into is not None:
        aliases = {2: 0}
        in_specs.append(ANY)
        args.append(into)
    if after is not None:
        in_specs.append(ANY)
        args.append(after)
    return pl.pallas_call(
        body, name=name, grid=(n_cols, s // ts), in_specs=in_specs,
        out_specs=pl.BlockSpec((ka, tn), lambda j, st: (0, col_map(j))),
        out_shape=jax.ShapeDtypeStruct((ka, n), F32), input_output_aliases=aliases,
        compiler_params=_cp(("parallel", "arbitrary"), 56),
    )(*args)


def _rms_fwd(name, x, g, after=None):
    s = x.shape[0]

    def body(x_ref, g_ref, *rest):
        xf = x_ref[...]
        r = lax.rsqrt(jnp.mean(xf * xf, axis=-1, keepdims=True) + EPS)
        rest[-1][...] = (xf * r * g_ref[...]).astype(BF16)

    row = pl.BlockSpec((TS, D_MODEL), lambda i: (i, 0))
    in_specs, args = [row, pl.BlockSpec((1, D_MODEL), lambda i: (0, 0))], [x, g]
    if after is not None:
        in_specs.append(ANY)
        args.append(after)
    return pl.pallas_call(
        body, name=name, grid=(s // TS,), in_specs=in_specs, out_specs=row,
        out_shape=jax.ShapeDtypeStruct((s, D_MODEL), BF16), compiler_params=_cp(("parallel",), 32),
    )(*args)


def _class_major(a, d):
    return a.reshape(d, a.shape[0] // d, a.shape[1])


def _class_spec(d, tile, width):
    return pl.BlockSpec((d, tile // d, width), lambda i: (0, i, 0))


LANES = 128


def _token_scratch(tile, width):
    return pltpu.VMEM((width // LANES, tile, LANES), F32)


def _put(scr, val):
    for c in range(scr.shape[0]):
        scr[c] = val[:, c * LANES:(c + 1) * LANES]


def _get(scr):
    return jnp.concatenate([scr[c] for c in range(scr.shape[0])], axis=1)


def _to_classes(ref3, scr, d, dtype):
    n = ref3.shape[1]
    for c in range(scr.shape[0]):
        for r in range(d):
            ref3[r, :, c * LANES:(c + 1) * LANES] = scr.at[c][pl.ds(r, n, stride=d), :].astype(dtype)


def _from_classes(scr, ref3, d):
    n = ref3.shape[1]
    for c in range(scr.shape[0]):
        for r in range(d):
            scr.at[c][pl.ds(r, n, stride=d), :] = ref3[r, :, c * LANES:(c + 1) * LANES].astype(F32)


def _rms_fwd_orders(name, x, g):
    s = x.shape[0]

    def body(x_ref, g_ref, h_ref, h4_ref, h16_ref, scr):
        xf = x_ref[...]
        r = lax.rsqrt(jnp.mean(xf * xf, axis=-1, keepdims=True) + EPS)
        h = xf * r * g_ref[...]
        h_ref[...] = h.astype(BF16)
        _put(scr, h)
        _to_classes(h4_ref, scr, 4, BF16)
        _to_classes(h16_ref, scr, 16, BF16)

    row = pl.BlockSpec((TS, D_MODEL), lambda i: (i, 0))
    h, h4, h16 = pl.pallas_call(
        body, name=name, grid=(s // TS,), in_specs=[row, pl.BlockSpec((1, D_MODEL), lambda i: (0, 0))],
        out_specs=[row, _class_spec(4, TS, D_MODEL), _class_spec(16, TS, D_MODEL)],
        out_shape=[jax.ShapeDtypeStruct((s, D_MODEL), BF16), jax.ShapeDtypeStruct((4, s // 4, D_MODEL), BF16),
                   jax.ShapeDtypeStruct((16, s // 16, D_MODEL), BF16)],
        scratch_shapes=[_token_scratch(TS, D_MODEL)],
        compiler_params=_cp(("parallel",), 32),
    )(x, g)
    return h, h4.reshape(s, D_MODEL), h16.reshape(s, D_MODEL)


def _rms_bwd(name, dh, x, g, dres, dh4=None, dh16=None):
    s = x.shape[0]
    extra = dh4 is not None

    def body(dh_ref, x_ref, g_ref, dres_ref, *rest):
        if extra:
            dh4_ref, dh16_ref, dx_ref, dg_ref, scr = rest
        else:
            dx_ref, dg_ref = rest
        xf = x_ref[...]
        r = lax.rsqrt(jnp.mean(xf * xf, axis=-1, keepdims=True) + EPS)
        xh = xf * r
        dhf = dh_ref[...]
        if extra:
            _from_classes(scr, dh4_ref, 4)
            dhf = dhf + _get(scr)
            _from_classes(scr, dh16_ref, 16)
            dhf = dhf + _get(scr)
        dxh = dhf * g_ref[...]
        dx_ref[...] = dres_ref[...] + r * (dxh - xh * jnp.mean(dxh * xh, axis=-1, keepdims=True))
        part = jnp.sum(dhf * xh, axis=0, keepdims=True)

        @pl.when(pl.program_id(0) == 0)
        def _():
            dg_ref[...] = part

        @pl.when(pl.program_id(0) > 0)
        def _():
            dg_ref[...] += part

    row = pl.BlockSpec((TS, D_MODEL), lambda i: (i, 0))
    vec = pl.BlockSpec((1, D_MODEL), lambda i: (0, 0))
    in_specs, args, scratch = [row, row, vec, row], [dh, x, g, dres], []
    if extra:
        in_specs += [_class_spec(4, TS, D_MODEL), _class_spec(16, TS, D_MODEL)]
        args += [_class_major(dh4, 4), _class_major(dh16, 16)]
        scratch = [_token_scratch(TS, D_MODEL)]
    return pl.pallas_call(
        body, name=name, grid=(s // TS,), in_specs=in_specs, out_specs=[row, vec],
        out_shape=[jax.ShapeDtypeStruct((s, D_MODEL), F32), jax.ShapeDtypeStruct((1, D_MODEL), F32)],
        scratch_shapes=scratch, compiler_params=_cp(("arbitrary",), 40),
    )(*args)


def _final_loss(x, g, target):
    s = x.shape[0]

    def body(x_ref, g_ref, t_ref, dx_ref, loss_ref, dg_ref):
        xf = x_ref[...]
        gg = g_ref[...]
        r = lax.rsqrt(jnp.mean(xf * xf, axis=-1, keepdims=True) + EPS)
        xh = xf * r
        e = xh * gg - t_ref[...]
        dy = e * (1.0 / D_MODEL)
        dxh = dy * gg
        dx_ref[...] = r * (dxh - xh * jnp.mean(dxh * xh, axis=-1, keepdims=True))
        lpart = 0.5 * jnp.sum(jnp.mean(e * e, axis=-1, keepdims=True), axis=0, keepdims=True)
        lpart = jnp.broadcast_to(lpart, (8, 128))
        gpart = jnp.sum(dy * xh, axis=0, keepdims=True)

        @pl.when(pl.program_id(0) == 0)
        def _():
            loss_ref[...] = lpart
            dg_ref[...] = gpart

        @pl.when(pl.program_id(0) > 0)
        def _():
            loss_ref[...] += lpart
            dg_ref[...] += gpart

    row = pl.BlockSpec((TS, D_MODEL), lambda i: (i, 0))
    vec = pl.BlockSpec((1, D_MODEL), lambda i: (0, 0))
    return pl.pallas_call(
        body, name="final_loss", grid=(s // TS,), in_specs=[row, vec, row],
        out_specs=[row, pl.BlockSpec((8, 128), lambda i: (0, 0)), vec],
        out_shape=[jax.ShapeDtypeStruct((s, D_MODEL), F32), jax.ShapeDtypeStruct((8, 128), F32),
                   jax.ShapeDtypeStruct((1, D_MODEL), F32)],
        compiler_params=_cp(("arbitrary",), 40),
    )(x, g, target)


def _zcol(c, tm=TM):
    return pl.BlockSpec((tm, D_MODEL), lambda i, c=c: (i, c))


def _prev_halo(c, tm=TM):
    return pl.BlockSpec((HALO, D_MODEL), lambda i, c=c: (jnp.maximum(i * (tm // HALO) - 1, 0), c))


def _next_halo(c, n_rows, tm=TM):
    last = n_rows // HALO - 1
    return pl.BlockSpec((HALO, D_MODEL), lambda i, c=c: (jnp.minimum((i + 1) * (tm // HALO), last), c))


def _full(shape):
    return pl.BlockSpec(shape, lambda i: (0,) * len(shape))


def _inv_count(first_row, n, w):
    t = first_row + lax.broadcasted_iota(jnp.int32, (n, 1), 0)
    return 1.0 / jnp.minimum(t + 1, w).astype(F32)


def _even_mix_fwd(z, pw, ps, wt, bs):
    s = z.shape[0]

    def body(a_ref, ga_ref, u_ref, v_ref, gb_ref, halo_ref, pw_ref, ps_ref, wt_ref, bs_ref, y_ref):
        i = pl.program_id(0)
        a = a_ref[...].astype(F32)
        halo = jnp.where(i > 0, halo_ref[...].astype(F32), 0.0)
        ext = jnp.concatenate([halo, a], axis=0)
        ga = ga_ref[...].astype(F32)
        sga = ga * _sig(ga)
        for g, w in enumerate(POOL_SIZES):
            cs = slice(g * GROUP_W, (g + 1) * GROUP_W)
            win = _win_sum(ext[:, cs], w, False)[HALO:]
            pooled = win * _inv_count(i * TM, TM, w) - a[:, cs]
            mixed = jnp.dot(pooled.astype(BF16), pw_ref[g], preferred_element_type=F32)
            y_ref[:, cs] = (mixed * ps_ref[:, cs] * sga[:, cs]).astype(BF16)
        gb = gb_ref[...].astype(F32)
        gate = u_ref[...].astype(F32) * (gb * _sig(gb))
        for ch in range(TM // CHUNK):
            rs = slice(ch * CHUNK, (ch + 1) * CHUNK)
            for g in range(4):
                cs = slice(g * GROUP_W, (g + 1) * GROUP_W)
                mixb = jnp.dot(wt_ref[g], v_ref[rs, cs], preferred_element_type=F32) + bs_ref[g]
                y_ref[rs, D_MODEL + g * GROUP_W:D_MODEL + (g + 1) * GROUP_W] = (gate[rs, cs] * mixb).astype(BF16)

    return pl.pallas_call(
        body, name="even_mix_fwd", grid=(s // TM,),
        in_specs=[_zcol(0), _zcol(1), _zcol(2), _zcol(3), _zcol(4), _prev_halo(0),
                  _full((4, GROUP_W, GROUP_W)), _full((1, D_MODEL)), _full((4, CHUNK, CHUNK)), _full((4, CHUNK, 1))],
        out_specs=pl.BlockSpec((TM, 2 * D_MODEL), lambda i: (i, 0)),
        out_shape=jax.ShapeDtypeStruct((s, 2 * D_MODEL), BF16),
        compiler_params=_cp(("parallel",), 48),
    )(z, z, z, z, z, z, pw, ps, wt, bs)


def _even_mix_bwd(dy, z, pw, ps, wt, wtt, bs, after=None):
    s = z.shape[0]
    n_tiles = s // TM
    tail_specs, tail_args = ([ANY], [after]) if after is not None else ([], [])

    def body(dy_ref, a_ref, ga_ref, u_ref, v_ref, gb_ref, halo_ref, dyn_ref, gan_ref, pw_ref, ps_ref, wt_ref, wtt_ref,
             bs_ref, *rest):
        dz_ref, dpw_ref, dps_ref, dws_ref, dbs_ref = rest[-5:]
        i = pl.program_id(0)

        @pl.when(i == 0)
        def _():
            dpw_ref[...] = jnp.zeros_like(dpw_ref)
            dps_ref[...] = jnp.zeros_like(dps_ref)
            dws_ref[...] = jnp.zeros_like(dws_ref)
            dbs_ref[...] = jnp.zeros_like(dbs_ref)

        a = a_ref[...].astype(F32)
        halo = jnp.where(i > 0, halo_ref[...].astype(F32), 0.0)
        ext = jnp.concatenate([halo, a], axis=0)
        ga = ga_ref[...].astype(F32)
        sg = _sig(ga)
        sga = ga * sg
        dsga = sg * (1.0 + ga * (1.0 - sg))
        dya = dy_ref[:, :D_MODEL].astype(F32)
        gan = gan_ref[...].astype(F32)
        dmn_all = jnp.where(i < n_tiles - 1, dyn_ref[...].astype(F32) * ps_ref[...] * (gan * _sig(gan)), 0.0)
        for g, w in enumerate(POOL_SIZES):
            cs = slice(g * GROUP_W, (g + 1) * GROUP_W)
            inv = _inv_count(i * TM, TM, w)
            pooled = _win_sum(ext[:, cs], w, False)[HALO:] * inv - a[:, cs]
            pb = pooled.astype(BF16)
            mixed = jnp.dot(pb, pw_ref[g], preferred_element_type=F32)
            dyg = dya[:, cs]
            psg = ps_ref[:, cs]
            dm = (dyg * psg * sga[:, cs]).astype(BF16)
            dz_ref[:, D_MODEL + g * GROUP_W:D_MODEL + (g + 1) * GROUP_W] = (dyg * mixed * psg * dsga[:, cs]).astype(BF16)
            dps_ref[:, cs] += jnp.sum(dyg * mixed * sga[:, cs], axis=0, keepdims=True)
            dpw_ref[g] += lax.dot_general(pb, dm, (((0,), (0,)), ((), ())), preferred_element_type=F32)
            nt = (((1,), (1,)), ((), ()))
            dpool = lax.dot_general(dm, pw_ref[g], nt, preferred_element_type=F32)
            dpool_n = lax.dot_general(dmn_all[:, cs].astype(BF16), pw_ref[g], nt, preferred_element_type=F32)
            e = jnp.concatenate([dpool * inv, dpool_n * _inv_count((i + 1) * TM, HALO, w)], axis=0)
            dz_ref[:, cs] = (_win_sum(e, w, True)[:TM] - dpool).astype(BF16)

        gb = gb_ref[...].astype(F32)
        sg = _sig(gb)
        sgb = gb * sg
        dsgb = sg * (1.0 + gb * (1.0 - sg))
        u = u_ref[...].astype(F32)
        dyb = dy_ref[:, D_MODEL:].astype(F32)
        tril = lax.broadcasted_iota(jnp.int32, (CHUNK, CHUNK), 0) >= lax.broadcasted_iota(jnp.int32, (CHUNK, CHUNK), 1)
        lane = lax.broadcasted_iota(jnp.int32, (CHUNK, 128), 1)
        for ch in range(TM // CHUNK):
            rs = slice(ch * CHUNK, (ch + 1) * CHUNK)
            for g in range(4):
                cs = slice(g * GROUP_W, (g + 1) * GROUP_W)
                vb = v_ref[rs, cs]
                mixb = jnp.dot(wt_ref[g], vb, preferred_element_type=F32) + bs_ref[g]
                dyu = dyb[rs, cs] * u[rs, cs]
                dmix = dyu * sgb[rs, cs]
                dmb = dmix.astype(BF16)
                o = g * GROUP_W
                dz_ref[rs, 2 * D_MODEL + o:2 * D_MODEL + o + GROUP_W] = (dyb[rs, cs] * mixb * sgb[rs, cs]).astype(BF16)
                dz_ref[rs, 3 * D_MODEL + o:3 * D_MODEL + o + GROUP_W] = jnp.dot(
                    wtt_ref[g], dmb, preferred_element_type=F32).astype(BF16)
                dz_ref[rs, 4 * D_MODEL + o:4 * D_MODEL + o + GROUP_W] = (dyu * mixb * dsgb[rs, cs]).astype(BF16)
                dws = lax.dot_general(dmb, vb, (((1,), (1,)), ((), ())), preferred_element_type=F32)
                dws_ref[g] += jnp.where(tril, dws, 0.0)
                dbs_ref[...] += jnp.where(lane == g, jnp.sum(dmix, axis=1, keepdims=True), 0.0)

    return pl.pallas_call(
        body, name="even_mix_bwd", grid=(n_tiles,),
        in_specs=[pl.BlockSpec((TM, 2 * D_MODEL), lambda i: (i, 0)), _zcol(0), _zcol(1), _zcol(2), _zcol(3), _zcol(4),
                  _prev_halo(0), _next_halo(0, s), _next_halo(1, s),
                  _full((4, GROUP_W, GROUP_W)), _full((1, D_MODEL)), _full((4, CHUNK, CHUNK)), _full((4, CHUNK, CHUNK)),
                  _full((4, CHUNK, 1))] + tail_specs,
        out_specs=[pl.BlockSpec((TM, EVEN_IN), lambda i: (i, 0)), _full((4, GROUP_W, GROUP_W)), _full((1, D_MODEL)),
                   _full((4, CHUNK, CHUNK)), _full((CHUNK, 128))],
        out_shape=[jax.ShapeDtypeStruct((s, EVEN_IN), BF16), jax.ShapeDtypeStruct((4, GROUP_W, GROUP_W), F32),
                   jax.ShapeDtypeStruct((1, D_MODEL), F32), jax.ShapeDtypeStruct((4, CHUNK, CHUNK), F32),
                   jax.ShapeDtypeStruct((CHUNK, 128), F32)],
        compiler_params=_cp(("arbitrary",), 56),
    )(dy, z, z, z, z, z, z, dy, z, pw, ps, wt, wtt, bs, *tail_args)


STAT_W = 128
Q_BLOCKS = 2
Q_ROWS = Q_BLOCKS * ATTN_BLOCK


def _band(d):
    row = lax.broadcasted_iota(jnp.int32, (ATTN_BLOCK, 2 * ATTN_BLOCK), 0)
    col = lax.broadcasted_iota(jnp.int32, (ATTN_BLOCK, 2 * ATTN_BLOCK), 1)
    steps = row + ATTN_BLOCK - col
    return (steps >= 0) & (steps <= ATTN_BLOCK), col >= ATTN_BLOCK, -(steps * d).astype(F32)


def _attn_fwd(z, gi):
    s = z.shape[0]
    d = DILATIONS[gi]
    nb = s // d // ATTN_BLOCK
    nq = nb // Q_BLOCKS

    def spec(which, prev=False):
        cb = which * 3 + gi
        if prev:
            return pl.BlockSpec((ATTN_BLOCK, D_MODEL), lambda r, i: (r * nb + jnp.maximum(Q_BLOCKS * i - 1, 0), cb))
        return pl.BlockSpec((Q_ROWS, D_MODEL), lambda r, i: (r * nq + i, cb))

    def body(q_ref, kp_ref, kc_ref, vp_ref, vc_ref, o_ref, lse_ref):
        i = pl.program_id(1)
        inner, own, negdist = _band(d)
        lane = lax.broadcasted_iota(jnp.int32, (ATTN_BLOCK, STAT_W), 1)
        for b in range(Q_BLOCKS):
            rows = slice(b * ATTN_BLOCK, (b + 1) * ATTN_BLOCK)
            valid = (inner & ((i > 0) | own)) if b == 0 else inner
            stat = jnp.zeros((ATTN_BLOCK, STAT_W), F32)
            for h in range(N_SLOTS):
                sl = slice(h * HEAD_DIM, (h + 1) * HEAD_DIM)
                if b == 0:
                    k = jnp.concatenate([kp_ref[:, sl], kc_ref[:ATTN_BLOCK, sl]], axis=0)
                    v = jnp.concatenate([vp_ref[:, sl], vc_ref[:ATTN_BLOCK, sl]], axis=0)
                else:
                    k = kc_ref[(b - 1) * ATTN_BLOCK:(b + 1) * ATTN_BLOCK, sl]
                    v = vc_ref[(b - 1) * ATTN_BLOCK:(b + 1) * ATTN_BLOCK, sl]
                sc = lax.dot_general(q_ref[rows, sl], k, (((1,), (1,)), ((), ())), preferred_element_type=F32) * SCALE
                sc = jnp.where(valid, sc + SLOPES[gi][h] * negdist, NEG)
                m = jnp.max(sc, axis=-1, keepdims=True)
                p = jnp.exp(sc - m)
                l = jnp.sum(p, axis=-1, keepdims=True)
                o = jnp.dot((p / l).astype(BF16), v, preferred_element_type=F32)
                o_ref[rows, sl] = o.astype(BF16)
                stat = jnp.where(lane == h, m + jnp.log(l), stat)
            lse_ref[rows, :] = stat

    return pl.pallas_call(
        body, name=f"attn_fwd_d{d}", grid=(d, nq),
        in_specs=[spec(0), spec(1, True), spec(1), spec(2, True), spec(2)],
        out_specs=[pl.BlockSpec((Q_ROWS, D_MODEL), lambda r, i: (r * nq + i, 0)),
                   pl.BlockSpec((Q_ROWS, STAT_W), lambda r, i: (r * nq + i, 0))],
        out_shape=[jax.ShapeDtypeStruct((s, D_MODEL), BF16), jax.ShapeDtypeStruct((s, STAT_W), F32)],
        compiler_params=_cp(("parallel", "parallel"), 32),
    )(z, z, z, z, z)


def _attn_bwd(z, dyc, ltot, dst, dz, gi):
    s = z.shape[0]
    d = DILATIONS[gi]
    nb = s // d // ATTN_BLOCK
    nq = nb // Q_BLOCKS
    n_steps = d * nq

    def rev(cb, width=D_MODEL, prev=False):
        if prev:
            return pl.BlockSpec((ATTN_BLOCK, width), lambda r, n: (r * nb + jnp.maximum(Q_BLOCKS * (nq - 1 - n) - 1, 0), cb))
        return pl.BlockSpec((Q_ROWS, width), lambda r, n: (r * nq + nq - 1 - n, cb))

    def body(q_ref, kp_ref, kc_ref, vp_ref, vc_ref, dy_ref, l_ref, d_ref, dz_in, dz_out, dq_s, dk_s, dv_s, ck_s, cv_s, sems):
        del dz_in
        r = pl.program_id(0)
        n = pl.program_id(1)
        i = nq - 1 - n
        step = r * nq + n

        def out_copy(src, which):
            rows = pl.ds(pl.multiple_of((r * nq + i) * Q_ROWS, Q_ROWS), Q_ROWS)
            return pltpu.make_async_copy(src, dz_out.at[rows, pl.ds((which * 3 + gi) * D_MODEL, D_MODEL)], sems.at[which])

        copies = [out_copy(dq_s, 0), out_copy(dk_s, 1), out_copy(dv_s, 2)]

        @pl.when(step > 0)
        def _():
            for cp in copies:
                cp.wait()

        @pl.when(n == 0)
        def _():
            ck_s[...] = jnp.zeros_like(ck_s)
            cv_s[...] = jnp.zeros_like(cv_s)

        row = lax.broadcasted_iota(jnp.int32, (Q_ROWS, Q_ROWS + ATTN_BLOCK), 0)
        col = lax.broadcasted_iota(jnp.int32, (Q_ROWS, Q_ROWS + ATTN_BLOCK), 1)
        steps = row + ATTN_BLOCK - col
        valid = (steps >= 0) & (steps <= ATTN_BLOCK) & ((i > 0) | (col >= ATTN_BLOCK))
        negdist = -(steps * d).astype(F32)
        nt = (((1,), (1,)), ((), ()))
        tn = (((0,), (0,)), ((), ()))
        for h in range(N_SLOTS):
            sl = slice(h * HEAD_DIM, (h + 1) * HEAD_DIM)
            q = q_ref[:, sl]
            k = jnp.concatenate([kp_ref[:, sl], kc_ref[:, sl]], axis=0)
            v = jnp.concatenate([vp_ref[:, sl], vc_ref[:, sl]], axis=0)
            dy = dy_ref[:, sl]
            sc = lax.dot_general(q, k, nt, preferred_element_type=F32) * SCALE + SLOPES[gi][h] * negdist
            p = jnp.where(valid, jnp.exp(sc - l_ref[:, h:h + 1]), 0.0)
            dp = lax.dot_general(dy, v, nt, preferred_element_type=F32)
            ds = (p * (dp - d_ref[:, h:h + 1])).astype(BF16)
            dq_s[:, sl] = (jnp.dot(ds, k, preferred_element_type=F32) * SCALE).astype(BF16)
            dk = lax.dot_general(ds, q, tn, preferred_element_type=F32) * SCALE
            dv = lax.dot_general(p.astype(BF16), dy, tn, preferred_element_type=F32)
            dk_s[:Q_ROWS - ATTN_BLOCK, sl] = dk[ATTN_BLOCK:Q_ROWS].astype(BF16)
            dv_s[:Q_ROWS - ATTN_BLOCK, sl] = dv[ATTN_BLOCK:Q_ROWS].astype(BF16)
            dk_s[Q_ROWS - ATTN_BLOCK:, sl] = (ck_s[:, sl] + dk[Q_ROWS:]).astype(BF16)
            dv_s[Q_ROWS - ATTN_BLOCK:, sl] = (cv_s[:, sl] + dv[Q_ROWS:]).astype(BF16)
            ck_s[:, sl] = dk[:ATTN_BLOCK]
            cv_s[:, sl] = dv[:ATTN_BLOCK]

        for cp in copies:
            cp.start()

        @pl.when(step == n_steps - 1)
        def _():
            for cp in copies:
                cp.wait()

    stage = pltpu.VMEM((Q_ROWS, D_MODEL), BF16)
    carry = pltpu.VMEM((ATTN_BLOCK, D_MODEL), F32)
    return pl.pallas_call(
        body, name=f"attn_bwd_d{d}", grid=(d, nq),
        in_specs=[rev(gi), rev(3 + gi, prev=True), rev(3 + gi), rev(6 + gi, prev=True), rev(6 + gi),
                  rev(0), rev(0, STAT_W), rev(0, STAT_W), ANY],
        out_specs=ANY,
        out_shape=jax.ShapeDtypeStruct((s, ODD_IN), BF16),
        scratch_shapes=[stage, stage, stage, carry, carry, pltpu.SemaphoreType.DMA((3,))],
        input_output_aliases={8: 0},
        compiler_params=_cp(("arbitrary", "arbitrary"), 32),
    )(z, z, z, z, z, dyc, ltot, dst, dz)


def _odd_mix_fwd(z, os_, lses, cw):
    s = z.shape[0]

    def body(o0, o1, o2, l0, l1, l2, gc_ref, db_ref, dc_ref, dx_ref, gd_ref, hc_ref, hx_ref, cw_ref, y_ref, yc_ref, lt_ref,
             lt4_ref, lt16_ref, scr_o, scr_l):
        i = pl.program_id(0)
        _from_classes(scr_l, l1, 4)
        lse1 = _get(scr_l)
        _from_classes(scr_l, l2, 16)
        ls = [l0[...], lse1, _get(scr_l)]
        lmax = jnp.maximum(jnp.maximum(ls[0], ls[1]), ls[2])
        es = [jnp.exp(l - lmax) for l in ls]
        den = es[0] + es[1] + es[2]
        alpha = [e / den for e in es]
        ltot = lmax + jnp.log(den)
        lt_ref[...] = ltot
        _put(scr_l, ltot)
        _to_classes(lt4_ref, scr_l, 4, F32)
        _to_classes(lt16_ref, scr_l, 16, F32)
        _from_classes(scr_o, o1, 4)
        og1 = _get(scr_o)
        _from_classes(scr_o, o2, 16)
        og = [o0[...].astype(F32), og1, _get(scr_o)]
        gc = gc_ref[...].astype(F32)
        gate = gc * _sig(gc)
        for h in range(N_SLOTS):
            sl = slice(h * HEAD_DIM, (h + 1) * HEAD_DIM)
            yc = (alpha[0][:, h:h + 1] * og[0][:, sl] + alpha[1][:, h:h + 1] * og[1][:, sl]
                  + alpha[2][:, h:h + 1] * og[2][:, sl])
            yc_ref[:, sl] = yc.astype(BF16)
            y_ref[:, sl] = (yc * gate[:, sl]).astype(BF16)

        zc = dc_ref[...].astype(F32) * dx_ref[...].astype(F32)
        halo = jnp.where(i > 0, hc_ref[...].astype(F32) * hx_ref[...].astype(F32), 0.0)
        ext = jnp.concatenate([halo, zc], axis=0)
        z1 = pltpu.roll(ext, 1, 0)[HALO:]
        z2 = pltpu.roll(ext, 2, 0)[HALO:]
        conv = cw_ref[0:1, :] * z2 + cw_ref[1:2, :] * z1 + cw_ref[2:3, :] * zc
        gd = gd_ref[...].astype(F32)
        y_ref[:, D_MODEL:] = (db_ref[...].astype(F32) * conv * (gd * _sig(gd))).astype(BF16)

    row = pl.BlockSpec((TM, D_MODEL), lambda i: (i, 0))
    stat = pl.BlockSpec((TM, STAT_W), lambda i: (i, 0))
    y, ycr, lt, lt4, lt16 = pl.pallas_call(
        body, name="odd_mix_fwd", grid=(s // TM,),
        in_specs=[row, _class_spec(4, TM, D_MODEL), _class_spec(16, TM, D_MODEL),
                  stat, _class_spec(4, TM, STAT_W), _class_spec(16, TM, STAT_W),
                  _zcol(9), _zcol(10), _zcol(11), _zcol(12), _zcol(13), _prev_halo(11), _prev_halo(12), _full((3, D_MODEL))],
        out_specs=[pl.BlockSpec((TM, 2 * D_MODEL), lambda i: (i, 0)), row, stat, _class_spec(4, TM, STAT_W),
                   _class_spec(16, TM, STAT_W)],
        out_shape=[jax.ShapeDtypeStruct((s, 2 * D_MODEL), BF16), jax.ShapeDtypeStruct((s, D_MODEL), BF16),
                   jax.ShapeDtypeStruct((s, STAT_W), F32), jax.ShapeDtypeStruct((4, s // 4, STAT_W), F32),
                   jax.ShapeDtypeStruct((16, s // 16, STAT_W), F32)],
        scratch_shapes=[_token_scratch(TM, D_MODEL), _token_scratch(TM, STAT_W)],
        compiler_params=_cp(("parallel",), 48),
    )(os_[0], _class_major(os_[1], 4), _class_major(os_[2], 16), lses[0], _class_major(lses[1], 4),
      _class_major(lses[2], 16), z, z, z, z, z, z, z, cw)
    return y, ycr, [lt, lt4.reshape(s, STAT_W), lt16.reshape(s, STAT_W)]


def _odd_mix_bwd(dy, z, ycr, cw):
    s = z.shape[0]
    n_tiles = s // TM
    rest = ODD_IN - QKV_BLOCKS * D_MODEL

    def body(dy_ref, yc_ref, gc_ref, db_ref, dc_ref, dx_ref, gd_ref, hc_ref, hx_ref, dyn_ref, dbn_ref, gdn_ref, cw_ref,
             dz_ref, dyc_ref, dyc4_ref, dyc16_ref, dd_ref, dd4_ref, dd16_ref, dcw_ref, stage, sem, scr_o, scr_l):
        i = pl.program_id(0)
        out = pltpu.make_async_copy(
            stage, dz_ref.at[pl.ds(pl.multiple_of(i * TM, TM), TM), pl.ds(QKV_BLOCKS * D_MODEL, rest)], sem)

        @pl.when(i > 0)
        def _():
            out.wait()

        dyc_in = dy_ref[:, :D_MODEL].astype(F32)
        gc = gc_ref[...].astype(F32)
        sg = _sig(gc)
        yc = yc_ref[...].astype(F32)
        dyc = dyc_in * (gc * sg)
        dyc_ref[...] = dyc.astype(BF16)
        _put(scr_o, dyc)
        _to_classes(dyc4_ref, scr_o, 4, BF16)
        _to_classes(dyc16_ref, scr_o, 16, BF16)
        stage[:, 0:D_MODEL] = (dyc_in * yc * (sg * (1.0 + gc * (1.0 - sg)))).astype(BF16)
        prod = dyc * yc
        lane = lax.broadcasted_iota(jnp.int32, (TM, STAT_W), 1)
        stat = jnp.zeros((TM, STAT_W), F32)
        for h in range(N_SLOTS):
            part = jnp.sum(prod[:, h * HEAD_DIM:(h + 1) * HEAD_DIM], axis=-1, keepdims=True)
            stat = jnp.where(lane == h, part, stat)
        dd_ref[...] = stat
        _put(scr_l, stat)
        _to_classes(dd4_ref, scr_l, 4, F32)
        _to_classes(dd16_ref, scr_l, 16, F32)

        dc = dc_ref[...].astype(F32)
        dx = dx_ref[...].astype(F32)
        zc = dc * dx
        halo = jnp.where(i > 0, hc_ref[...].astype(F32) * hx_ref[...].astype(F32), 0.0)
        ext = jnp.concatenate([halo, zc], axis=0)
        z1 = pltpu.roll(ext, 1, 0)[HALO:]
        z2 = pltpu.roll(ext, 2, 0)[HALO:]
        w0, w1, w2 = cw_ref[0:1, :], cw_ref[1:2, :], cw_ref[2:3, :]
        conv = w0 * z2 + w1 * z1 + w2 * zc
        gd = gd_ref[...].astype(F32)
        sg = _sig(gd)
        sgd = gd * sg
        db = db_ref[...].astype(F32)
        dyd = dy_ref[:, D_MODEL:].astype(F32)
        dconv = dyd * db * sgd
        gdn = gdn_ref[...].astype(F32)
        dconv_n = jnp.where(i < n_tiles - 1, dyn_ref[...].astype(F32) * dbn_ref[...].astype(F32) * (gdn * _sig(gdn)), 0.0)
        extn = jnp.concatenate([dconv, dconv_n], axis=0)
        nrow = TM + HALO
        dzc = w2 * dconv + w1 * pltpu.roll(extn, nrow - 1, 0)[:TM] + w0 * pltpu.roll(extn, nrow - 2, 0)[:TM]
        stage[:, D_MODEL:2 * D_MODEL] = (dyd * conv * sgd).astype(BF16)
        stage[:, 2 * D_MODEL:3 * D_MODEL] = (dzc * dx).astype(BF16)
        stage[:, 3 * D_MODEL:4 * D_MODEL] = (dzc * dc).astype(BF16)
        stage[:, 4 * D_MODEL:5 * D_MODEL] = (dyd * db * conv * (sg * (1.0 + gd * (1.0 - sg)))).astype(BF16)
        @pl.when(i == 0)
        def _():
            dcw_ref[...] = jnp.zeros_like(dcw_ref)

        for tap, shifted in enumerate((z2, z1, zc)):
            dcw_ref[tap:tap + 1, :] += jnp.sum(dconv * shifted, axis=0, keepdims=True)

        out.start()

        @pl.when(i == n_tiles - 1)
        def _():
            out.wait()

    row = pl.BlockSpec((TM, D_MODEL), lambda i: (i, 0))
    stat = pl.BlockSpec((TM, STAT_W), lambda i: (i, 0))
    dz, dyc, dyc4, dyc16, dd, dd4, dd16, g_conv = pl.pallas_call(
        body, name="odd_mix_bwd", grid=(n_tiles,),
        in_specs=[pl.BlockSpec((TM, 2 * D_MODEL), lambda i: (i, 0)), row, _zcol(9), _zcol(10), _zcol(11), _zcol(12), _zcol(13),
                  _prev_halo(11), _prev_halo(12), _next_halo(1, s), _next_halo(10, s), _next_halo(13, s), _full((3, D_MODEL))],
        out_specs=[ANY, row, _class_spec(4, TM, D_MODEL), _class_spec(16, TM, D_MODEL),
                   stat, _class_spec(4, TM, STAT_W), _class_spec(16, TM, STAT_W), _full((3, D_MODEL))],
        out_shape=[jax.ShapeDtypeStruct((s, ODD_IN), BF16), jax.ShapeDtypeStruct((s, D_MODEL), BF16),
                   jax.ShapeDtypeStruct((4, s // 4, D_MODEL), BF16), jax.ShapeDtypeStruct((16, s // 16, D_MODEL), BF16),
                   jax.ShapeDtypeStruct((s, STAT_W), F32), jax.ShapeDtypeStruct((4, s // 4, STAT_W), F32),
                   jax.ShapeDtypeStruct((16, s // 16, STAT_W), F32), jax.ShapeDtypeStruct((3, D_MODEL), F32)],
        scratch_shapes=[pltpu.VMEM((TM, rest), BF16), pltpu.SemaphoreType.DMA(()), _token_scratch(TM, D_MODEL),
                        _token_scratch(TM, STAT_W)],
        compiler_params=_cp(("arbitrary",), 48),
    )(dy, ycr, z, z, z, z, z, z, z, dy, z, z, cw)
    dyc = [dyc, dyc4.reshape(s, D_MODEL), dyc16.reshape(s, D_MODEL)]
    dd = [dd, dd4.reshape(s, STAT_W), dd16.reshape(s, STAT_W)]
    return dz, dyc, dd, g_conv


def _cols_of_order(order):
    if order == 0:
        return (lambda j: jnp.where(j < 3, 3 * j, j + 6)), 8
    return (lambda j: 3 * j + order), 3


class _Hooks:
    def before_even(self):
        return None

    def odd_weights(self, w, x1):
        return w

    def odd_grads_ready(self, g_w_in_o, g_w_out_o):
        return None

    def even_dy_done(self, dy_e):
        return None

    def backward_done(self, dx0):
        return None


def _local_step(x, target, w, hooks=_Hooks()):
    tril = jnp.tril(jnp.ones((CHUNK, CHUNK), bool))
    wt = jnp.where(tril[None], w["ws"], 0.0).astype(BF16)
    wtt = jnp.swapaxes(wt, 1, 2)
    bs = w["bs"].reshape(4, CHUNK, 1)

    h_e = _rms_fwd("rms_fwd_even", x, w["even_norm"], after=hooks.before_even())
    z_e = _mm_nn("even_in_proj", h_e, w["w_in_e"], MM_ROWS, 1280, BF16)
    y_e = _even_mix_fwd(z_e, w["pool_w"], w["pool_scale"], wt, bs)
    x1 = _mm_nn("even_out_proj", y_e, w["w_out_e"], MM_ROWS, 1024, F32, resid=x)
    w = hooks.odd_weights(w, x1)
    h_o = _rms_fwd_orders("rms_fwd_odd", x1, w["odd_norm"])
    z_o = None
    for o in range(3):
        cols, n_cols = _cols_of_order(o)
        z_o = _mm_nn(f"odd_in_proj_o{o}", h_o[o], w["w_in_o"], MM_ROWS, D_MODEL, BF16, col_map=cols, n_cols=n_cols,
                     into=z_o)
    att = [_attn_fwd(z_o, gi) for gi in range(3)]
    y_o, ycr, ltot = _odd_mix_fwd(z_o, [a[0] for a in att], [a[1] for a in att], w["conv_w"])
    x2 = _mm_nn("odd_out_proj", y_o, w["w_out_o"], MM_ROWS, 1024, F32, resid=x1)
    dx2, loss8, g_final = _final_loss(x2, w["final_norm"], target)

    g_w_out_o = _mm_tn("odd_out_proj_dw", y_o, dx2, 1024, DW_OUT_TOKENS)
    dy_o = _mm_nt("odd_out_proj_dy", dx2, w["w_out_o"], MM_ROWS, 1024, BF16)
    dz_o, dyc, dst, g_conv = _odd_mix_bwd(dy_o, z_o, ycr, w["conv_w"])
    for gi in range(3):
        dz_o = _attn_bwd(z_o, dyc[gi], ltot[gi], dst[gi], dz_o, gi)
    g_w_in_o, dh_o = None, []
    for o in range(3):
        cols, n_cols = _cols_of_order(o)
        g_w_in_o = _mm_tn(f"odd_in_proj_dw_o{o}", h_o[o], dz_o, D_MODEL, DW_IN_TOKENS, col_map=cols, n_cols=n_cols,
                          into=g_w_in_o)
        dh_o.append(_mm_nt(f"odd_in_proj_dh_o{o}", dz_o, w["w_in_o"], MM_ROWS, D_MODEL, F32, k_map=cols, nk=n_cols))
    dx1, g_odd_norm = _rms_bwd("rms_bwd_odd", dh_o[0], x1, w["odd_norm"], dx2, dh4=dh_o[1], dh16=dh_o[2])
    after = hooks.odd_grads_ready(g_w_in_o, g_w_out_o)
    g_w_out_e = _mm_tn("even_out_proj_dw", y_e, dx1, 1024, DW_OUT_TOKENS, after=after)
    dy_e = _mm_nt("even_out_proj_dy", dx1, w["w_out_e"], MM_ROWS, 1024, BF16)
    after = hooks.even_dy_done(dy_e)
    dz_e, g_pw, g_ps, g_ws, g_bs = _even_mix_bwd(dy_e, z_e, w["pool_w"], w["pool_scale"], wt, wtt, bs, after=after)
    g_w_in_e = _mm_tn("even_in_proj_dw", h_e, dz_e, 1280, DW_IN_TOKENS)
    dh_e = _mm_nt("even_in_proj_dh", dz_e, w["w_in_e"], MM_ROWS, 1280, F32)
    dx0, g_even_norm = _rms_bwd("rms_bwd_even", dh_e, x, w["even_norm"], dx1)
    hooks.backward_done(dx0)

    grads = dict(w_in_e=g_w_in_e, pool_w=g_pw, w_out_e=g_w_out_e, w_in_o=g_w_in_o, w_out_o=g_w_out_o,
                 even_norm=g_even_norm, pool_scale=g_ps, ws=g_ws, bs=g_bs[:, :4].T, final_norm=g_final,
                 odd_norm=g_odd_norm, conv_w=g_conv)
    return loss8[0, 0], dx0, grads


class _Big(NamedTuple):
    name: str
    full: tuple
    haxis: int
    kaxis: int
    sub: int


BIGS = (
    _Big("w_in_e", (1024, 5120), 0, 1, 2),
    _Big("pool_w", (4, 256, 256), 0, 1, 1),
    _Big("w_out_e", (2048, 1024), 1, 0, 1),
    _Big("w_in_o", (1024, 14336), 0, 1, 4),
    _Big("w_out_o", (2048, 1024), 1, 0, 1),
)
N_BIG = len(BIGS)


def _shape(b, half=False, shard=False):
    return tuple(n // (2 if (half and ax == b.haxis) else 1) // (4 if (shard and ax == b.kaxis) else 1)
                 for ax, n in enumerate(b.full))


def _at(ref, b, h=None, k=None):
    idx = []
    for ax, n in enumerate(b.full):
        if ax == b.haxis and h is not None:
            idx.append(pl.ds(h * (n // 2), n // 2))
        elif ax == b.kaxis and k is not None:
            idx.append(pl.ds(k * (n // 4), n // 4))
        else:
            idx.append(slice(None))
    return ref.at[tuple(idx)]


def _place():
    x, y, c = lax.axis_index("x"), lax.axis_index("y"), lax.axis_index("c")
    chips = [(1 - x, y), (x, 1 - y), (1 - x, 1 - y)]
    return x, y, c, 2 * x + y, chips, [2 * cx + cy for cx, cy in chips]


def _gather_weights(bigs, shards, tiny):
    nb = len(bigs)

    def body(*refs):
        ins, tiny_in = refs[:nb], refs[nb]
        outs, tiny_out = refs[nb + 1:2 * nb + 1], refs[2 * nb + 1]
        send, recv, loc = refs[2 * nb + 2:]
        x, y, c, k_me, chips, ks = _place()
        sib = (x, y, 1 - c)

        def rc(src, dst, sem, to):
            return pltpu.make_async_remote_copy(src_ref=src, dst_ref=dst, send_sem=send.at[sem], recv_sem=recv.at[sem],
                                                device_id=to, device_id_type=MESH)

        started = []
        for a, b in enumerate(bigs):
            own = pltpu.make_async_copy(ins[a], _at(outs[a], b, k=k_me), loc.at[a])
            own.start()
            started.append(own)
        own = pltpu.make_async_copy(tiny_in, tiny_out.at[k_me], loc.at[nb])
        own.start()
        started.append(own)
        sends = []
        for j, chip in enumerate(chips):
            for a, b in enumerate(bigs):
                sends.append(rc(_at(ins[a], b, h=c), _at(outs[a], b, h=c, k=k_me), 6 * a + j, (*chip, c)))
            sends.append(rc(tiny_in, tiny_out.at[k_me], 6 * nb + j, (*chip, c)))
        for cp in sends:
            cp.start()
        for j in range(3):
            for a, b in enumerate(bigs):
                win = _at(outs[a], b, h=c, k=ks[j])
                rc(win, win, 6 * a + j, sib).wait_recv()
                fwd = rc(win, win, 6 * a + 3 + j, sib)
                fwd.start()
                sends.append(fwd)
            rc(tiny_in, tiny_out.at[ks[j]], 6 * nb + j, sib).wait_recv()
        for j in range(3):
            for a, b in enumerate(bigs):
                win = _at(outs[a], b, h=1 - c, k=ks[j])
                rc(win, win, 6 * a + 3 + j, sib).wait_recv()
        for cp in sends:
            cp.wait_send()
        for cp in started:
            cp.wait()

    n_sem = 6 * nb + 3
    return pl.pallas_call(
        body, name="gather_even_weights",
        in_specs=[ANY] * (nb + 1), out_specs=[ANY] * (nb + 1),
        out_shape=[jax.ShapeDtypeStruct(b.full, BF16) for b in bigs] + [jax.ShapeDtypeStruct((4,) + tiny.shape, F32)],
        scratch_shapes=[pltpu.SemaphoreType.DMA((n_sem,)), pltpu.SemaphoreType.DMA((n_sem,)),
                        pltpu.SemaphoreType.DMA((nb + 1,))],
    )(*shards, tiny)


def _copies_to_chips(bigs):
    def copies(srcs, lands, send, recv):
        _, _, c, k_me, chips, _ = _place()
        return [pltpu.make_async_remote_copy(
                    src_ref=_at(srcs[a], b, h=c), dst_ref=_at(lands[a], b, h=c, k=k_me), send_sem=send.at[3 * a + j],
                    recv_sem=recv.at[3 * a + j], device_id=(*chips[j], c), device_id_type=MESH)
                for j in range(3) for a, b in enumerate(bigs)]
    return copies


def _copies_swap_halves(bigs):
    def copies(srcs, lands, send, recv):
        x, y, c, _, _, _ = _place()
        return [pltpu.make_async_remote_copy(
                    src_ref=_at(srcs[a], b, h=1 - c), dst_ref=lands[a], send_sem=send.at[a], recv_sem=recv.at[a],
                    device_id=(x, y, 1 - c), device_id_type=MESH)
                for a, b in enumerate(bigs)]
    return copies


def _copies_partials(bigs):
    def copies(srcs, lands, send, recv):
        _, _, c, _, chips, ks = _place()
        return [pltpu.make_async_remote_copy(
                    src_ref=_at(srcs[a], b, k=ks[j]), dst_ref=lands[a].at[j], send_sem=send.at[3 * a + j],
                    recv_sem=recv.at[3 * a + j], device_id=(*chips[j], c), device_id_type=MESH)
                for j in range(3) for a, b in enumerate(bigs)]
    return copies


def _exchange(name, srcs, land_shapes, copies_of, n_copies):
    ns = len(srcs)

    def body(*refs):
        copies = copies_of(refs[:ns], refs[ns:ns + len(land_shapes)], refs[-2], refs[-1])
        for cp in copies:
            cp.start()
        for cp in copies:
            cp.wait()

    return pl.pallas_call(
        body, name=name, in_specs=[ANY] * ns, out_specs=[ANY] * len(land_shapes), out_shape=land_shapes,
        scratch_shapes=[pltpu.SemaphoreType.DMA((n_copies,)), pltpu.SemaphoreType.DMA((n_copies,))],
    )(*srcs)


HBM = pl.BlockSpec(memory_space=pltpu.HBM)
SEM = pl.BlockSpec(memory_space=pltpu.SEMAPHORE)
SIDE_EFFECT = pltpu.SideEffectType.DATAFLOW_SIDE_EFFECTING


def _in_hbm(a):
    return pltpu.with_memory_space_constraint(a, pltpu.HBM)


def _exchange_start(name, srcs, land_shapes, copies_of, n_copies, after=None):
    ns, nl = len(srcs), len(land_shapes)
    lands = [lax.empty(sh.shape, sh.dtype) for sh in land_shapes]
    tail = [] if after is None else [after]
    n_in = ns + nl + len(tail)

    def body(*refs):
        send, recv, token = refs[n_in], refs[n_in + 1], refs[-1]
        for cp in copies_of(refs[:ns], refs[ns:ns + nl], send, recv):
            cp.start()
        token[...] = jnp.zeros_like(token)

    thru = [pltpu.HBM(a.shape, a.dtype) for a in (*srcs, *lands)]
    send, recv, *bufs, token = pl.pallas_call(
        body, name=name,
        out_shape=(pltpu.SemaphoreType.DMA((n_copies,)), pltpu.SemaphoreType.DMA((n_copies,)), *thru,
                   jax.ShapeDtypeStruct((8, 128), F32)),
        in_specs=[HBM] * (ns + nl) + [ANY] * len(tail),
        out_specs=(SEM, SEM, *([HBM] * (ns + nl)), pl.BlockSpec(memory_space=pltpu.VMEM)),
        input_output_aliases={i: 2 + i for i in range(ns + nl)},
        compiler_params=pltpu.CompilerParams(has_side_effects=SIDE_EFFECT),
    )(*[_in_hbm(a) for a in (*srcs, *lands)], *tail)
    return (send, recv, bufs, ns), token


def _exchange_wait(name, state, copies_of, after):
    send, recv, bufs, ns = state
    n = len(bufs)

    def body(*refs):
        ins = refs[:n]
        for cp in copies_of(ins[:ns], ins[ns:], refs[n], refs[n + 1]):
            cp.wait_send()
            cp.wait_recv()

    out = pl.pallas_call(
        body, name=name, out_shape=tuple(pltpu.HBM(a.shape, a.dtype) for a in bufs),
        in_specs=[HBM] * n + [SEM, SEM, ANY], out_specs=tuple([HBM] * n),
        input_output_aliases={i: i for i in range(n)},
        compiler_params=pltpu.CompilerParams(has_side_effects=SIDE_EFFECT),
    )(*bufs, send, recv, after)
    return list(out[:ns]), list(out[ns:])


def _finish_gather(bigs, shards, fulls):
    nb = len(bigs)

    def body(*refs):
        ins, outs = refs[:nb], refs[2 * nb:3 * nb]
        send, recv, loc = refs[3 * nb:]
        x, y, c, k_me, _, ks = _place()
        own = [pltpu.make_async_copy(ins[a], _at(outs[a], b, k=k_me), loc.at[a]) for a, b in enumerate(bigs)]
        for cp in own:
            cp.start()
        fwd = []
        for j in range(3):
            for a, b in enumerate(bigs):
                fwd.append(pltpu.make_async_remote_copy(
                    src_ref=_at(outs[a], b, h=c, k=ks[j]), dst_ref=_at(outs[a], b, h=c, k=ks[j]), send_sem=send.at[3 * a + j],
                    recv_sem=recv.at[3 * a + j], device_id=(x, y, 1 - c), device_id_type=MESH))
        for cp in fwd:
            cp.start()
        for cp in fwd:
            cp.wait()
        for cp in own:
            cp.wait()

    return pl.pallas_call(
        body, name="gather_odd_finish", in_specs=[ANY] * (2 * nb), out_specs=[ANY] * nb,
        out_shape=[jax.ShapeDtypeStruct(b.full, BF16) for b in bigs],
        scratch_shapes=[pltpu.SemaphoreType.DMA((3 * nb,)), pltpu.SemaphoreType.DMA((3 * nb,)),
                        pltpu.SemaphoreType.DMA((nb,))],
        input_output_aliases={nb + a: a for a in range(nb)},
    )(*shards, *fulls)


def _blk(b):
    win = _shape(b, half=True, shard=True)
    return (win[0] // b.sub,) + win[1:]


def _bidx(b, h, k, st):
    idx = [0] * len(b.full)
    idx[b.haxis] = h
    idx[b.kaxis] = k
    idx[0] = idx[0] * b.sub + st
    return tuple(idx)


def _chip_sum(b, g, got, c_arr):
    blk = _blk(b)

    def body(c_ref, g_ref, r_ref, o_ref):
        del c_ref
        o_ref[...] = (g_ref[...] + r_ref[...]).astype(BF16)

    half = pl.BlockSpec(blk, lambda k, st, c_ref: _bidx(b, 0, k, st))
    return pl.pallas_call(
        body, name=f"rs_chip_sum_{b.name}",
        grid_spec=pltpu.PrefetchScalarGridSpec(
            num_scalar_prefetch=1, grid=(4, b.sub),
            in_specs=[pl.BlockSpec(blk, lambda k, st, c_ref: _bidx(b, c_ref[0], k, st)), half], out_specs=half),
        out_shape=jax.ShapeDtypeStruct(_shape(b, half=True), BF16),
        compiler_params=_cp(("arbitrary", "arbitrary"), 40),
    )(c_arr, g, got)


def _half_shapes(bigs):
    return [jax.ShapeDtypeStruct(_shape(b, half=True), F32) for b in bigs]


def _partial_shapes(bigs):
    return [jax.ShapeDtypeStruct((3,) + _shape(b, half=True, shard=True), BF16) for b in bigs]


def _shard_sum(b, mine, got, ck_arr):
    blk = _blk(b)

    def body(ck_ref, m_ref, r0, r1, r2, o_ref):
        del ck_ref
        o_ref[...] = (m_ref[...].astype(F32) + r0[...].astype(F32)) + (r1[...].astype(F32) + r2[...].astype(F32))

    def peer(j):
        return pl.BlockSpec((None,) + blk, lambda st, ck: (j,) + _bidx(b, 0, 0, st))

    return pl.pallas_call(
        body, name=f"rs_shard_sum_{b.name}",
        grid_spec=pltpu.PrefetchScalarGridSpec(
            num_scalar_prefetch=1, grid=(b.sub,),
            in_specs=[pl.BlockSpec(blk, lambda st, ck: _bidx(b, 0, ck[1], st)), peer(0), peer(1), peer(2)],
            out_specs=pl.BlockSpec(blk, lambda st, ck: _bidx(b, ck[0], 0, st))),
        out_shape=jax.ShapeDtypeStruct(_shape(b, shard=True), F32),
        compiler_params=_cp(("arbitrary",), 40),
    )(ck_arr, mine, got, got, got)


def _share_halves(gs):
    def body(*refs):
        ins, outs, send, recv = refs[:N_BIG], refs[N_BIG:2 * N_BIG], refs[2 * N_BIG], refs[2 * N_BIG + 1]
        del ins
        x, y, c, _, _, _ = _place()
        copies = [pltpu.make_async_remote_copy(src_ref=_at(outs[a], b, h=c), dst_ref=_at(outs[a], b, h=c),
                                               send_sem=send.at[a], recv_sem=recv.at[a], device_id=(x, y, 1 - c),
                                               device_id_type=MESH)
                  for a, b in enumerate(BIGS)]
        for cp in copies:
            cp.start()
        for cp in copies:
            cp.wait()

    return pl.pallas_call(
        body, name="rs_share_halves", in_specs=[ANY] * N_BIG, out_specs=[ANY] * N_BIG,
        out_shape=[jax.ShapeDtypeStruct(_shape(b, shard=True), F32) for b in BIGS],
        scratch_shapes=[pltpu.SemaphoreType.DMA((N_BIG,)), pltpu.SemaphoreType.DMA((N_BIG,))],
        input_output_aliases={a: a for a in range(N_BIG)},
    )(*gs)


def _gather_small(block):
    m_per, n = block.shape

    def body(x_ref, out_ref, send_sems, recv_sems, local_sem):
        x, y, c = lax.axis_index("x"), lax.axis_index("y"), lax.axis_index("c")
        me, sibling = (x, y, c), (x, y, 1 - c)
        chips = [(1 - x, y), (x, 1 - y), (1 - x, 1 - y)]

        def rows(px, py, pc):
            return out_ref.at[pl.ds((4 * px + 2 * py + pc) * m_per, m_per), :]

        def copy(k, blk, to, src=None):
            return pltpu.make_async_remote_copy(
                src_ref=rows(*blk) if src is None else src, dst_ref=rows(*blk), send_sem=send_sems.at[k],
                recv_sem=recv_sems.at[k], device_id=to, device_id_type=MESH)

        mine = pltpu.make_async_copy(x_ref, rows(*me), local_sem)
        mine.start()
        first = [copy(0, me, sibling, src=x_ref)]
        first += [copy(1 + j, me, (*chip, c), src=x_ref) for j, chip in enumerate(chips)]
        for cp in first:
            cp.start()
        passed = [copy(4 + j, (*chip, c), sibling) for j, chip in enumerate(chips)]
        for j, chip in enumerate(chips):
            copy(1 + j, (*chip, c), me).wait_recv()
            passed[j].start()
        copy(0, sibling, me).wait_recv()
        for j, chip in enumerate(chips):
            copy(4 + j, (*chip, 1 - c), me).wait_recv()
        for cp in first + passed:
            cp.wait_send()
        mine.wait()

    return pl.pallas_call(
        body, name="gather_small_grads",
        out_shape=jax.ShapeDtypeStruct((8 * m_per, n), block.dtype),
        in_specs=[pl.BlockSpec(memory_space=pltpu.VMEM)], out_specs=pl.BlockSpec(memory_space=pltpu.VMEM),
        scratch_shapes=[pltpu.SemaphoreType.DMA((7,)), pltpu.SemaphoreType.DMA((7,)), pltpu.SemaphoreType.DMA],
    )(block)


def _sum_small(stack):
    _, m_per, n = stack.shape

    def body(x_ref, o_ref):
        acc = x_ref[0]
        for dev in range(1, 8):
            acc = acc + x_ref[dev]
        o_ref[...] = acc

    return pl.pallas_call(body, name="sum_small_grads", out_shape=jax.ShapeDtypeStruct((m_per, n), F32))(stack)


def _adamw(name, w, g, m, v, rows):
    shape = w.shape

    def body(w_ref, g_ref, m_ref, v_ref, d_ref, mo_ref, vo_ref):
        gg = g_ref[...]
        mn = ADAM_B1 * m_ref[...] + (1.0 - ADAM_B1) * gg
        vn = ADAM_B2 * v_ref[...] + (1.0 - ADAM_B2) * (gg * gg)
        m_hat = mn / (1.0 - ADAM_B1 ** ADAM_STEP)
        v_hat = vn / (1.0 - ADAM_B2 ** ADAM_STEP)
        d_ref[...] = -ADAM_LR * (m_hat / (jnp.sqrt(v_hat) + ADAM_EPS) + ADAM_WD * w_ref[...])
        mo_ref[...] = mn
        vo_ref[...] = vn

    spec = pl.BlockSpec((rows,) + shape[1:], lambda i: (i,) + (0,) * (len(shape) - 1))
    return pl.pallas_call(
        body, name=name, grid=(shape[0] // rows,), in_specs=[spec] * 4, out_specs=[spec] * 3,
        out_shape=[jax.ShapeDtypeStruct(shape, F32)] * 3, compiler_params=_cp(("parallel",), 48),
    )(w, g, m, v)


ADAM_ROWS = dict(w_in_e=256, pool_w=4, w_out_e=256, w_in_o=128, w_out_o=256)


def _pack(parts, rows):
    flat = jnp.concatenate([p.reshape(-1).astype(F32) for p in parts])
    return jnp.pad(flat, (0, rows * 128 - flat.shape[0])).reshape(rows, 128)


def _unpack(buf, shapes):
    flat = buf.reshape(-1)
    out, off = [], 0
    for shp in shapes:
        n = 1
        for dim in shp:
            n *= dim
        out.append(flat[off:off + n].reshape(shp))
        off += n
    return out


WEIGHTS = ("even_norm", "even_w_in", "even_pool_w", "even_pool_scale", "even_ws", "even_bs", "even_w_out", "odd_norm",
           "odd_w_in", "odd_conv_w", "odd_w_out", "final_norm")
BIG_OF = dict(w_in_e="even_w_in", pool_w="even_pool_w", w_out_e="even_w_out", w_in_o="odd_w_in", w_out_o="odd_w_out")
SMALL = ("even_norm", "even_pool_scale", "even_ws", "even_bs", "final_norm", "odd_norm", "odd_conv_w")
SMALL_GRAD_ROWS = 576
SMALL_STATE_ROWS = 552


def kernel(x, even_norm, even_w_in, even_pool_w, even_pool_scale, even_ws, even_bs, even_w_out, odd_norm, odd_w_in, odd_conv_w, odd_w_out, final_norm, loss_target, m_even_norm, m_even_w_in, m_even_pool_w, m_even_pool_scale, m_even_ws, m_even_bs, m_even_w_out, m_odd_norm, m_odd_w_in, m_odd_conv_w, m_odd_w_out, m_final_norm, v_even_norm, v_even_w_in, v_even_pool_w, v_even_pool_scale, v_even_ws, v_even_bs, v_even_w_out, v_odd_norm, v_odd_w_in, v_odd_conv_w, v_odd_w_out, v_final_norm):
    wv = dict(zip(WEIGHTS, (even_norm, even_w_in, even_pool_w, even_pool_scale, even_ws, even_bs, even_w_out, odd_norm,
                            odd_w_in, odd_conv_w, odd_w_out, final_norm)))
    mv = dict(zip(WEIGHTS, (m_even_norm, m_even_w_in, m_even_pool_w, m_even_pool_scale, m_even_ws, m_even_bs,
                            m_even_w_out, m_odd_norm, m_odd_w_in, m_odd_conv_w, m_odd_w_out, m_final_norm)))
    vv = dict(zip(WEIGHTS, (v_even_norm, v_even_w_in, v_even_pool_w, v_even_pool_scale, v_even_ws, v_even_bs,
                            v_even_w_out, v_odd_norm, v_odd_w_in, v_odd_conv_w, v_odd_w_out, v_final_norm)))
    c = lax.axis_index("c")
    k_me = 2 * lax.axis_index("x") + lax.axis_index("y")

    c_arr = jnp.reshape(c, (1,)).astype(jnp.int32)
    ck_arr = jnp.stack([c, k_me]).astype(jnp.int32)
    even_bigs, odd_bigs = BIGS[:3], BIGS[3:]

    shards = {b.name: wv[BIG_OF[b.name]][0].astype(BF16) for b in BIGS}
    tiny = jnp.concatenate([odd_conv_w[0], odd_norm], axis=0)
    *full_even, tiny_all = _gather_weights(even_bigs, [shards[b.name] for b in even_bigs], tiny)
    tiny_full = jnp.transpose(tiny_all, (1, 0, 2)).reshape(4, D_MODEL)
    w = dict(zip((b.name for b in even_bigs), full_even))
    w.update(even_norm=even_norm, pool_scale=even_pool_scale, ws=even_ws[0], bs=even_bs[0],
             final_norm=final_norm.reshape(1, D_MODEL), conv_w=tiny_full[:3], odd_norm=tiny_full[3:4])
    to_chips, swap_odd, partials_odd = _copies_to_chips(odd_bigs), _copies_swap_halves(odd_bigs), _copies_partials(odd_bigs)
    gather_state, gather_token = _exchange_start(
        "gather_odd_start", [shards[b.name] for b in odd_bigs], [jax.ShapeDtypeStruct(b.full, BF16) for b in odd_bigs],
        to_chips, 3 * len(odd_bigs), after=full_even[-1])

    class Hooks(_Hooks):
        def before_even(self):
            return gather_token

        def odd_weights(self, w, x1):
            srcs, lands = _exchange_wait("gather_odd_wait", gather_state, to_chips, after=x1)
            return dict(w, **dict(zip((b.name for b in odd_bigs), _finish_gather(odd_bigs, srcs, lands))))

        def odd_grads_ready(self, g_w_in_o, g_w_out_o):
            self.swap, token = _exchange_start("rs_odd_swap_start", [g_w_in_o, g_w_out_o], _half_shapes(odd_bigs),
                                               swap_odd, len(odd_bigs))
            return token

        def even_dy_done(self, dy_e):
            grads, got = _exchange_wait("rs_odd_swap_wait", self.swap, swap_odd, after=dy_e)
            sums = [_chip_sum(b, g, r, c_arr) for b, g, r in zip(odd_bigs, grads, got)]
            self.partials, token = _exchange_start("rs_odd_partials_start", sums, _partial_shapes(odd_bigs), partials_odd,
                                                   3 * len(odd_bigs))
            return token

        def backward_done(self, dx0):
            self.sums, self.parts = _exchange_wait("rs_odd_partials_wait", self.partials, partials_odd, after=dx0)

    hooks = Hooks()
    loss, dx, g = _local_step(x[0], loss_target[0], w, hooks)
    loss = lax.psum(loss, ("x", "y", "c"))

    got = _exchange("rs_even_swap", [g[b.name] for b in even_bigs], _half_shapes(even_bigs), _copies_swap_halves(even_bigs),
                    len(even_bigs))
    sums = [_chip_sum(b, g[b.name], r, c_arr) for b, r in zip(even_bigs, got)]
    parts = _exchange("rs_even_partials", sums, _partial_shapes(even_bigs), _copies_partials(even_bigs), 3 * len(even_bigs))
    halves = [_shard_sum(b, sm, p, ck_arr) for b, sm, p in zip(BIGS, sums + hooks.sums, list(parts) + hooks.parts)]
    g_shard = dict(zip((b.name for b in BIGS), _share_halves(halves)))

    small_g = _pack([g["even_norm"], g["pool_scale"], g["ws"], g["bs"], g["final_norm"], g["odd_norm"], g["conv_w"]],
                    SMALL_GRAD_ROWS)
    small_g = _sum_small(_gather_small(small_g).reshape(8, SMALL_GRAD_ROWS, 128))
    g_en, g_ps, g_ws, g_bs, g_fn, g_on, g_cw = _unpack(
        small_g, [(1, D_MODEL), (1, D_MODEL), (1, 4, CHUNK, CHUNK), (1, 4, CHUNK), (D_MODEL,), (1, D_MODEL), (1, 3, D_MODEL)])
    g_on = lax.dynamic_slice(g_on, (0, k_me * 256), (1, 256))
    g_cw = lax.dynamic_slice(g_cw, (0, 0, k_me * 256), (1, 3, 256))
    grad = dict(even_norm=g_en, even_pool_scale=g_ps, even_ws=g_ws, even_bs=g_bs, final_norm=g_fn, odd_norm=g_on,
                odd_conv_w=g_cw)
    for b in BIGS:
        grad[BIG_OF[b.name]] = g_shard[b.name][None]

    delta, new_m, new_v = {}, {}, {}
    for b in BIGS:
        n = BIG_OF[b.name]
        d_, m_, v_ = _adamw(f"adamw_{b.name}", wv[n][0], g_shard[b.name], mv[n][0], vv[n][0], ADAM_ROWS[b.name])
        delta[n], new_m[n], new_v[n] = d_[None], m_[None], v_[None]
    shapes = [wv[n].shape for n in SMALL]
    packed = [_pack([src[n] for n in SMALL], SMALL_STATE_ROWS) for src in (wv, grad, mv, vv)]
    outs = _adamw("adamw_small", *packed, SMALL_STATE_ROWS)
    for dst, buf in zip((delta, new_m, new_v), outs):
        for n, arr in zip(SMALL, _unpack(buf, shapes)):
            dst[n] = arr

    return (loss, dx[None], *[grad[n] for n in WEIGHTS], *[delta[n] for n in WEIGHTS], *[new_m[n] for n in WEIGHTS],
            *[new_v[n] for n in WEIGHTS])
```

```python
from typing import NamedTuple

import jax
import jax.numpy as jnp
from jax import lax
from jax.experimental import pallas as pl
from jax.experimental.pallas import tpu as pltpu

F32, BF16 = jnp.float32, jnp.bfloat16

D_MODEL = 1024
EPS = 1e-6
NEG = -1e30
POOL_SIZES = (2, 4, 8, 16)
GROUP_W = 256
CHUNK = 128
DILATIONS = (1, 4, 16)
N_SLOTS = 8
HEAD_DIM = 128
ATTN_BLOCK = 128
SCALE = HEAD_DIM ** -0.5
EVEN_IN = 5120
ODD_IN = 14336
QKV_BLOCKS = 9
ODD_BLOCKS = ODD_IN // D_MODEL
SLOPES = tuple(tuple(2.0 ** (-8.0 * (g * N_SLOTS + s + 1) / (3 * N_SLOTS)) for s in range(N_SLOTS)) for g in range(3))

ADAM_LR, ADAM_B1, ADAM_B2, ADAM_EPS, ADAM_WD, ADAM_STEP = 0.001, 0.9, 0.999, 1e-08, 0.01, 10

HALO = 16
TS = 512
TM = 256
MM_ROWS = 1024
DW_IN_TOKENS = 2048
DW_OUT_TOKENS = 1024
MIB = 1 << 20
MESH = pl.DeviceIdType.MESH
ANY = pl.BlockSpec(memory_space=pl.ANY)


def _cp(sem, vmem_mib):
    return pltpu.CompilerParams(dimension_semantics=sem, vmem_limit_bytes=vmem_mib * MIB)


def _sig(x):
    return 1.0 / (1.0 + jnp.exp(-x))


def _win_sum(e, w, forward):
    n = e.shape[0]
    k = 1
    while k < w:
        e = e + pltpu.roll(e, (n - k) if forward else k, 0)
        k *= 2
    return e


def _mm_nn(name, a, b, tm, tn, out_dtype, resid=None, col_map=None, n_cols=None, into=None):
    m, k = a.shape
    n = b.shape[1]
    if col_map is None:
        col_map, n_cols = (lambda j: j), n // tn

    def body(*refs):
        a_ref, b_ref = refs[0], refs[1]
        acc = jnp.dot(a_ref[...].astype(BF16), b_ref[...], preferred_element_type=F32)
        if resid is not None:
            acc = acc + refs[2][...]
        o_ref = refs[-1]
        o_ref[...] = acc.astype(out_dtype)

    in_specs = [pl.BlockSpec((tm, k), lambda j, i: (i, 0)), pl.BlockSpec((k, tn), lambda j, i: (0, col_map(j)))]
    args = [a, b]
    if resid is not None:
        in_specs.append(pl.BlockSpec((tm, tn), lambda j, i: (i, col_map(j))))
        args.append(resid)
    aliases = {}
    if into is not None:
        aliases = {len(args): 0}
        in_specs.append(ANY)
        args.append(into)
    return pl.pallas_call(
        body, name=name, grid=(n_cols, m // tm), in_specs=in_specs,
        out_specs=pl.BlockSpec((tm, tn), lambda j, i: (i, col_map(j))),
        out_shape=jax.ShapeDtypeStruct((m, n), out_dtype), input_output_aliases=aliases,
        compiler_params=_cp(("parallel", "parallel"), 48),
    )(*args)


def _mm_nt(name, a, b, tm, tk, out_dtype, k_map=None, nk=None):
    m, k = a.shape
    n = b.shape[0]
    if k_map is None:
        k_map, nk = (lambda kk: kk), k // tk

    def body(a_ref, b_ref, o_ref, acc_ref):
        kk = pl.program_id(1)
        p = lax.dot_general(a_ref[...].astype(BF16), b_ref[...], (((1,), (1,)), ((), ())), preferred_element_type=F32)
        if nk == 1:
            o_ref[...] = p.astype(out_dtype)
        else:
            @pl.when(kk == 0)
            def _():
                acc_ref[...] = p

            @pl.when(kk > 0)
            def _():
                acc_ref[...] += p

            @pl.when(kk == nk - 1)
            def _():
                o_ref[...] = acc_ref[...].astype(out_dtype)

    return pl.pallas_call(
        body, name=name, grid=(m // tm, nk),
        in_specs=[pl.BlockSpec((tm, tk), lambda i, kk: (i, k_map(kk))), pl.BlockSpec((n, tk), lambda i, kk: (0, k_map(kk)))],
        out_specs=pl.BlockSpec((tm, n), lambda i, kk: (i, 0)),
        out_shape=jax.ShapeDtypeStruct((m, n), out_dtype),
        scratch_shapes=[pltpu.VMEM((tm, n) if nk > 1 else (8, 128), F32)],
        compiler_params=_cp(("parallel", "arbitrary"), 56),
    )(a, b)


def _mm_tn(name, a, g, tn, ts, col_map=None, n_cols=None, into=None, after=None):
    s, ka = a.shape
    n = g.shape[1]
    if col_map is None:
        col_map, n_cols = (lambda j: j), n // tn

    def body(a_ref, g_ref, *rest):
        o_ref = rest[-1]
        st = pl.program_id(1)
        p = lax.dot_general(a_ref[...], g_ref[...].astype(BF16), (((0,), (0,)), ((), ())), preferred_element_type=F32)

        @pl.when(st == 0)
        def _():
            o_ref[...] = p

        @pl.when(st > 0)
        def _():
            o_ref[...] += p

    in_specs = [pl.BlockSpec((ts, ka), lambda j, st: (st, 0)), pl.BlockSpec((ts, tn), lambda j, st: (st, col_map(j)))]
    args = [a, g]
    aliases = {}
    if into is not None:
        aliases = {2: 0}
        in_specs.append(ANY)
        args.append(into)
    if after is not None:
        in_specs.append(ANY)
        args.append(after)
    return pl.pallas_call(
        body, name=name, grid=(n_cols, s // ts), in_specs=in_specs,
        out_specs=pl.BlockSpec((ka, tn), lambda j, st: (0, col_map(j))),
        out_shape=jax.ShapeDtypeStruct((ka, n), F32), input_output_aliases=aliases,
        compiler_params=_cp(("parallel", "arbitrary"), 56),
    )(*args)


def _rms_fwd(name, x, g, after=None):
    s = x.shape[0]

    def body(x_ref, g_ref, *rest):
        xf = x_ref[...]
        r = lax.rsqrt(jnp.mean(xf * xf, axis=-1, keepdims=True) + EPS)
        rest[-1][...] = (xf * r * g_ref[...]).astype(BF16)

    row = pl.BlockSpec((TS, D_MODEL), lambda i: (i, 0))
    in_specs, args = [row, pl.BlockSpec((1, D_MODEL), lambda i: (0, 0))], [x, g]
    if after is not None:
        in_specs.append(ANY)
        args.append(after)
    return pl.pallas_call(
        body, name=name, grid=(s // TS,), in_specs=in_specs, out_specs=row,
        out_shape=jax.ShapeDtypeStruct((s, D_MODEL), BF16), compiler_params=_cp(("parallel",), 32),
    )(*args)


def _class_major(a, d):
    return a.reshape(d, a.shape[0] // d, a.shape[1])


def _class_spec(d, tile, width):
    return pl.BlockSpec((d, tile // d, width), lambda i: (0, i, 0))


LANES = 128


def _token_scratch(tile, width):
    return pltpu.VMEM((width // LANES, tile, LANES), F32)


def _put(scr, val):
    for c in range(scr.shape[0]):
        scr[c] = val[:, c * LANES:(c + 1) * LANES]


def _get(scr):
    return jnp.concatenate([scr[c] for c in range(scr.shape[0])], axis=1)


def _to_classes(ref3, scr, d, dtype):
    n = ref3.shape[1]
    for c in range(scr.shape[0]):
        for r in range(d):
            ref3[r, :, c * LANES:(c + 1) * LANES] = scr.at[c][pl.ds(r, n, stride=d), :].astype(dtype)


def _from_classes(scr, ref3, d):
    n = ref3.shape[1]
    for c in range(scr.shape[0]):
        for r in range(d):
            scr.at[c][pl.ds(r, n, stride=d), :] = ref3[r, :, c * LANES:(c + 1) * LANES].astype(F32)


def _rms_fwd_orders(name, x, g):
    s = x.shape[0]

    def body(x_ref, g_ref, h_ref, h4_ref, h16_ref, scr):
        xf = x_ref[...]
        r = lax.rsqrt(jnp.mean(xf * xf, axis=-1, keepdims=True) + EPS)
        h = xf * r * g_ref[...]
        h_ref[...] = h.astype(BF16)
        _put(scr, h)
        _to_classes(h4_ref, scr, 4, BF16)
        _to_classes(h16_ref, scr, 16, BF16)

    row = pl.BlockSpec((TS, D_MODEL), lambda i: (i, 0))
    h, h4, h16 = pl.pallas_call(
        body, name=name, grid=(s // TS,), in_specs=[row, pl.BlockSpec((1, D_MODEL), lambda i: (0, 0))],
        out_specs=[row, _class_spec(4, TS, D_MODEL), _class_spec(16, TS, D_MODEL)],
        out_shape=[jax.ShapeDtypeStruct((s, D_MODEL), BF16), jax.ShapeDtypeStruct((4, s // 4, D_MODEL), BF16),
                   jax.ShapeDtypeStruct((16, s // 16, D_MODEL), BF16)],
        scratch_shapes=[_token_scratch(TS, D_MODEL)],
        compiler_params=_cp(("parallel",), 32),
    )(x, g)
    return h, h4.reshape(s, D_MODEL), h16.reshape(s, D_MODEL)


def _rms_bwd(name, dh, x, g, dres, dh4=None, dh16=None):
    s = x.shape[0]
    extra = dh4 is not None

    def body(dh_ref, x_ref, g_ref, dres_ref, *rest):
        if extra:
            dh4_ref, dh16_ref, dx_ref, dg_ref, scr = rest
        else:
            dx_ref, dg_ref = rest
        xf = x_ref[...]
        r = lax.rsqrt(jnp.mean(xf * xf, axis=-1, keepdims=True) + EPS)
        xh = xf * r
        dhf = dh_ref[...]
        if extra:
            _from_classes(scr, dh4_ref, 4)
            dhf = dhf + _get(scr)
            _from_classes(scr, dh16_ref, 16)
            dhf = dhf + _get(scr)
        dxh = dhf * g_ref[...]
        dx_ref[...] = dres_ref[...] + r * (dxh - xh * jnp.mean(dxh * xh, axis=-1, keepdims=True))
        part = jnp.sum(dhf * xh, axis=0, keepdims=True)

        @pl.when(pl.program_id(0) == 0)
        def _():
            dg_ref[...] = part

        @pl.when(pl.program_id(0) > 0)
        def _():
            dg_ref[...] += part

    row = pl.BlockSpec((TS, D_MODEL), lambda i: (i, 0))
    vec = pl.BlockSpec((1, D_MODEL), lambda i: (0, 0))
    in_specs, args, scratch = [row, row, vec, row], [dh, x, g, dres], []
    if extra:
        in_specs += [_class_spec(4, TS, D_MODEL), _class_spec(16, TS, D_MODEL)]
        args += [_class_major(dh4, 4), _class_major(dh16, 16)]
        scratch = [_token_scratch(TS, D_MODEL)]
    return pl.pallas_call(
        body, name=name, grid=(s // TS,), in_specs=in_specs, out_specs=[row, vec],
        out_shape=[jax.ShapeDtypeStruct((s, D_MODEL), F32), jax.ShapeDtypeStruct((1, D_MODEL), F32)],
        scratch_shapes=scratch, compiler_params=_cp(("arbitrary",), 40),
    )(*args)


def _final_loss(x, g, target):
    s = x.shape[0]

    def body(x_ref, g_ref, t_ref, dx_ref, loss_ref, dg_ref):
        xf = x_ref[...]
        gg = g_ref[...]
        r = lax.rsqrt(jnp.mean(xf * xf, axis=-1, keepdims=True) + EPS)
        xh = xf * r
        e = xh * gg - t_ref[...]
        dy = e * (1.0 / D_MODEL)
        dxh = dy * gg
        dx_ref[...] = r * (dxh - xh * jnp.mean(dxh * xh, axis=-1, keepdims=True))
        lpart = 0.5 * jnp.sum(jnp.mean(e * e, axis=-1, keepdims=True), axis=0, keepdims=True)
        lpart = jnp.broadcast_to(lpart, (8, 128))
        gpart = jnp.sum(dy * xh, axis=0, keepdims=True)

        @pl.when(pl.program_id(0) == 0)
        def _():
            loss_ref[...] = lpart
            dg_ref[...] = gpart

        @pl.when(pl.program_id(0) > 0)
        def _():
            loss_ref[...] += lpart
            dg_ref[...] += gpart

    row = pl.BlockSpec((TS, D_MODEL), lambda i: (i, 0))
    vec = pl.BlockSpec((1, D_MODEL), lambda i: (0, 0))
    return pl.pallas_call(
        body, name="final_loss", grid=(s // TS,), in_specs=[row, vec, row],
        out_specs=[row, pl.BlockSpec((8, 128), lambda i: (0, 0)), vec],
        out_shape=[jax.ShapeDtypeStruct((s, D_MODEL), F32), jax.ShapeDtypeStruct((8, 128), F32),
                   jax.ShapeDtypeStruct((1, D_MODEL), F32)],
        compiler_params=_cp(("arbitrary",), 40),
    )(x, g, target)


def _zcol(c, tm=TM):
    return pl.BlockSpec((tm, D_MODEL), lambda i, c=c: (i, c))


def _prev_halo(c, tm=TM):
    return pl.BlockSpec((HALO, D_MODEL), lambda i, c=c: (jnp.maximum(i * (tm // HALO) - 1, 0), c))


def _next_halo(c, n_rows, tm=TM):
    last = n_rows // HALO - 1
    return pl.BlockSpec((HALO, D_MODEL), lambda i, c=c: (jnp.minimum((i + 1) * (tm // HALO), last), c))


def _full(shape):
    return pl.BlockSpec(shape, lambda i: (0,) * len(shape))


def _inv_count(first_row, n, w):
    t = first_row + lax.broadcasted_iota(jnp.int32, (n, 1), 0)
    return 1.0 / jnp.minimum(t + 1, w).astype(F32)


def _even_mix_fwd(z, pw, ps, wt, bs):
    s = z.shape[0]

    def body(a_ref, ga_ref, u_ref, v_ref, gb_ref, halo_ref, pw_ref, ps_ref, wt_ref, bs_ref, y_ref):
        i = pl.program_id(0)
        a = a_ref[...].astype(F32)
        halo = jnp.where(i > 0, halo_ref[...].astype(F32), 0.0)
        ext = jnp.concatenate([halo, a], axis=0)
        ga = ga_ref[...].astype(F32)
        sga = ga * _sig(ga)
        for g, w in enumerate(POOL_SIZES):
            cs = slice(g * GROUP_W, (g + 1) * GROUP_W)
            win = _win_sum(ext[:, cs], w, False)[HALO:]
            pooled = win * _inv_count(i * TM, TM, w) - a[:, cs]
            mixed = jnp.dot(pooled.astype(BF16), pw_ref[g], preferred_element_type=F32)
            y_ref[:, cs] = (mixed * ps_ref[:, cs] * sga[:, cs]).astype(BF16)
        gb = gb_ref[...].astype(F32)
        gate = u_ref[...].astype(F32) * (gb * _sig(gb))
        for ch in range(TM // CHUNK):
            rs = slice(ch * CHUNK, (ch + 1) * CHUNK)
            for g in range(4):
                cs = slice(g * GROUP_W, (g + 1) * GROUP_W)
                mixb = jnp.dot(wt_ref[g], v_ref[rs, cs], preferred_element_type=F32) + bs_ref[g]
                y_ref[rs, D_MODEL + g * GROUP_W:D_MODEL + (g + 1) * GROUP_W] = (gate[rs, cs] * mixb).astype(BF16)

    return pl.pallas_call(
        body, name="even_mix_fwd", grid=(s // TM,),
        in_specs=[_zcol(0), _zcol(1), _zcol(2), _zcol(3), _zcol(4), _prev_halo(0),
                  _full((4, GROUP_W, GROUP_W)), _full((1, D_MODEL)), _full((4, CHUNK, CHUNK)), _full((4, CHUNK, 1))],
        out_specs=pl.BlockSpec((TM, 2 * D_MODEL), lambda i: (i, 0)),
        out_shape=jax.ShapeDtypeStruct((s, 2 * D_MODEL), BF16),
        compiler_params=_cp(("parallel",), 48),
    )(z, z, z, z, z, z, pw, ps, wt, bs)


def _even_mix_bwd(dy, z, pw, ps, wt, wtt, bs, after=None):
    s = z.shape[0]
    n_tiles = s // TM
    tail_specs, tail_args = ([ANY], [after]) if after is not None else ([], [])

    def body(dy_ref, a_ref, ga_ref, u_ref, v_ref, gb_ref, halo_ref, dyn_ref, gan_ref, pw_ref, ps_ref, wt_ref, wtt_ref,
             bs_ref, *rest):
        dz_ref, dpw_ref, dps_ref, dws_ref, dbs_ref = rest[-5:]
        i = pl.program_id(0)

        @pl.when(i == 0)
        def _():
            dpw_ref[...] = jnp.zeros_like(dpw_ref)
            dps_ref[...] = jnp.zeros_like(dps_ref)
            dws_ref[...] = jnp.zeros_like(dws_ref)
            dbs_ref[...] = jnp.zeros_like(dbs_ref)

        a = a_ref[...].astype(F32)
        halo = jnp.where(i > 0, halo_ref[...].astype(F32), 0.0)
        ext = jnp.concatenate([halo, a], axis=0)
        ga = ga_ref[...].astype(F32)
        sg = _sig(ga)
        sga = ga * sg
        dsga = sg * (1.0 + ga * (1.0 - sg))
        dya = dy_ref[:, :D_MODEL].astype(F32)
        gan = gan_ref[...].astype(F32)
        dmn_all = jnp.where(i < n_tiles - 1, dyn_ref[...].astype(F32) * ps_ref[...] * (gan * _sig(gan)), 0.0)
        for g, w in enumerate(POOL_SIZES):
            cs = slice(g * GROUP_W, (g + 1) * GROUP_W)
            inv = _inv_count(i * TM, TM, w)
            pooled = _win_sum(ext[:, cs], w, False)[HALO:] * inv - a[:, cs]
            pb = pooled.astype(BF16)
            mixed = jnp.dot(pb, pw_ref[g], preferred_element_type=F32)
            dyg = dya[:, cs]
            psg = ps_ref[:, cs]
            dm = (dyg * psg * sga[:, cs]).astype(BF16)
            dz_ref[:, D_MODEL + g * GROUP_W:D_MODEL + (g + 1) * GROUP_W] = (dyg * mixed * psg * dsga[:, cs]).astype(BF16)
            dps_ref[:, cs] += jnp.sum(dyg * mixed * sga[:, cs], axis=0, keepdims=True)
            dpw_ref[g] += lax.dot_general(pb, dm, (((0,), (0,)), ((), ())), preferred_element_type=F32)
            nt = (((1,), (1,)), ((), ()))
            dpool = lax.dot_general(dm, pw_ref[g], nt, preferred_element_type=F32)
            dpool_n = lax.dot_general(dmn_all[:, cs].astype(BF16), pw_ref[g], nt, preferred_element_type=F32)
            e = jnp.concatenate([dpool * inv, dpool_n * _inv_count((i + 1) * TM, HALO, w)], axis=0)
            dz_ref[:, cs] = (_win_sum(e, w, True)[:TM] - dpool).astype(BF16)

        gb = gb_ref[...].astype(F32)
        sg = _sig(gb)
        sgb = gb * sg
        dsgb = sg * (1.0 + gb * (1.0 - sg))
        u = u_ref[...].astype(F32)
        dyb = dy_ref[:, D_MODEL:].astype(F32)
        tril = lax.broadcasted_iota(jnp.int32, (CHUNK, CHUNK), 0) >= lax.broadcasted_iota(jnp.int32, (CHUNK, CHUNK), 1)
        lane = lax.broadcasted_iota(jnp.int32, (CHUNK, 128), 1)
        for ch in range(TM // CHUNK):
            rs = slice(ch * CHUNK, (ch + 1) * CHUNK)
            for g in range(4):
                cs = slice(g * GROUP_W, (g + 1) * GROUP_W)
                vb = v_ref[rs, cs]
                mixb = jnp.dot(wt_ref[g], vb, preferred_element_type=F32) + bs_ref[g]
                dyu = dyb[rs, cs] * u[rs, cs]
                dmix = dyu * sgb[rs, cs]
                dmb = dmix.astype(BF16)
                o = g * GROUP_W
                dz_ref[rs, 2 * D_MODEL + o:2 * D_MODEL + o + GROUP_W] = (dyb[rs, cs] * mixb * sgb[rs, cs]).astype(BF16)
                dz_ref[rs, 3 * D_MODEL + o:3 * D_MODEL + o + GROUP_W] = jnp.dot(
                    wtt_ref[g], dmb, preferred_element_type=F32).astype(BF16)
                dz_ref[rs, 4 * D_MODEL + o:4 * D_MODEL + o + GROUP_W] = (dyu * mixb * dsgb[rs, cs]).astype(BF16)
                dws = lax.dot_general(dmb, vb, (((1,), (1,)), ((), ())), preferred_element_type=F32)
                dws_ref[g] += jnp.where(tril, dws, 0.0)
                dbs_ref[...] += jnp.where(lane == g, jnp.sum(dmix, axis=1, keepdims=True), 0.0)

    return pl.pallas_call(
        body, name="even_mix_bwd", grid=(n_tiles,),
        in_specs=[pl.BlockSpec((TM, 2 * D_MODEL), lambda i: (i, 0)), _zcol(0), _zcol(1), _zcol(2), _zcol(3), _zcol(4),
                  _prev_halo(0), _next_halo(0, s), _next_halo(1, s),
                  _full((4, GROUP_W, GROUP_W)), _full((1, D_MODEL)), _full((4, CHUNK, CHUNK)), _full((4, CHUNK, CHUNK)),
                  _full((4, CHUNK, 1))] + tail_specs,
        out_specs=[pl.BlockSpec((TM, EVEN_IN), lambda i: (i, 0)), _full((4, GROUP_W, GROUP_W)), _full((1, D_MODEL)),
                   _full((4, CHUNK, CHUNK)), _full((CHUNK, 128))],
        out_shape=[jax.ShapeDtypeStruct((s, EVEN_IN), BF16), jax.ShapeDtypeStruct((4, GROUP_W, GROUP_W), F32),
                   jax.ShapeDtypeStruct((1, D_MODEL), F32), jax.ShapeDtypeStruct((4, CHUNK, CHUNK), F32),
                   jax.ShapeDtypeStruct((CHUNK, 128), F32)],
        compiler_params=_cp(("arbitrary",), 56),
    )(dy, z, z, z, z, z, z, dy, z, pw, ps, wt, wtt, bs, *tail_args)


STAT_W = 128
Q_BLOCKS = 2
Q_ROWS = Q_BLOCKS * ATTN_BLOCK


def _band(d):
    row = lax.broadcasted_iota(jnp.int32, (ATTN_BLOCK, 2 * ATTN_BLOCK), 0)
    col = lax.broadcasted_iota(jnp.int32, (ATTN_BLOCK, 2 * ATTN_BLOCK), 1)
    steps = row + ATTN_BLOCK - col
    return (steps >= 0) & (steps <= ATTN_BLOCK), col >= ATTN_BLOCK, -(steps * d).astype(F32)


def _attn_fwd(z, gi):
    s = z.shape[0]
    d = DILATIONS[gi]
    nb = s // d // ATTN_BLOCK
    nq = nb // Q_BLOCKS

    def spec(which, prev=False):
        cb = which * 3 + gi
        if prev:
            return pl.BlockSpec((ATTN_BLOCK, D_MODEL), lambda r, i: (r * nb + jnp.maximum(Q_BLOCKS * i - 1, 0), cb))
        return pl.BlockSpec((Q_ROWS, D_MODEL), lambda r, i: (r * nq + i, cb))

    def body(q_ref, kp_ref, kc_ref, vp_ref, vc_ref, o_ref, lse_ref):
        i = pl.program_id(1)
        inner, own, negdist = _band(d)
        lane = lax.broadcasted_iota(jnp.int32, (ATTN_BLOCK, STAT_W), 1)
        for b in range(Q_BLOCKS):
            rows = slice(b * ATTN_BLOCK, (b + 1) * ATTN_BLOCK)
            valid = (inner & ((i > 0) | own)) if b == 0 else inner
            stat = jnp.zeros((ATTN_BLOCK, STAT_W), F32)
            for h in range(N_SLOTS):
                sl = slice(h * HEAD_DIM, (h + 1) * HEAD_DIM)
                if b == 0:
                    k = jnp.concatenate([kp_ref[:, sl], kc_ref[:ATTN_BLOCK, sl]], axis=0)
                    v = jnp.concatenate([vp_ref[:, sl], vc_ref[:ATTN_BLOCK, sl]], axis=0)
                else:
                    k = kc_ref[(b - 1) * ATTN_BLOCK:(b + 1) * ATTN_BLOCK, sl]
                    v = vc_ref[(b - 1) * ATTN_BLOCK:(b + 1) * ATTN_BLOCK, sl]
                sc = lax.dot_general(q_ref[rows, sl], k, (((1,), (1,)), ((), ())), preferred_element_type=F32) * SCALE
                sc = jnp.where(valid, sc + SLOPES[gi][h] * negdist, NEG)
                m = jnp.max(sc, axis=-1, keepdims=True)
                p = jnp.exp(sc - m)
                l = jnp.sum(p, axis=-1, keepdims=True)
                o = jnp.dot((p / l).astype(BF16), v, preferred_element_type=F32)
                o_ref[rows, sl] = o.astype(BF16)
                stat = jnp.where(lane == h, m + jnp.log(l), stat)
            lse_ref[rows, :] = stat

    return pl.pallas_call(
        body, name=f"attn_fwd_d{d}", grid=(d, nq),
        in_specs=[spec(0), spec(1, True), spec(1), spec(2, True), spec(2)],
        out_specs=[pl.BlockSpec((Q_ROWS, D_MODEL), lambda r, i: (r * nq + i, 0)),
                   pl.BlockSpec((Q_ROWS, STAT_W), lambda r, i: (r * nq + i, 0))],
        out_shape=[jax.ShapeDtypeStruct((s, D_MODEL), BF16), jax.ShapeDtypeStruct((s, STAT_W), F32)],
        compiler_params=_cp(("parallel", "parallel"), 32),
    )(z, z, z, z, z)


def _attn_bwd(z, dyc, ltot, dst, dz, gi):
    s = z.shape[0]
    d = DILATIONS[gi]
    nb = s // d // ATTN_BLOCK
    nq = nb // Q_BLOCKS
    n_steps = d * nq

    def rev(cb, width=D_MODEL, prev=False):
        if prev:
            return pl.BlockSpec((ATTN_BLOCK, width), lambda r, n: (r * nb + jnp.maximum(Q_BLOCKS * (nq - 1 - n) - 1, 0), cb))
        return pl.BlockSpec((Q_ROWS, width), lambda r, n: (r * nq + nq - 1 - n, cb))

    def body(q_ref, kp_ref, kc_ref, vp_ref, vc_ref, dy_ref, l_ref, d_ref, dz_in, dz_out, dq_s, dk_s, dv_s, ck_s, cv_s, sems):
        del dz_in
        r = pl.program_id(0)
        n = pl.program_id(1)
        i = nq - 1 - n
        step = r * nq + n

        def out_copy(src, which):
            rows = pl.ds(pl.multiple_of((r * nq + i) * Q_ROWS, Q_ROWS), Q_ROWS)
            return pltpu.make_async_copy(src, dz_out.at[rows, pl.ds((which * 3 + gi) * D_MODEL, D_MODEL)], sems.at[which])

        copies = [out_copy(dq_s, 0), out_copy(dk_s, 1), out_copy(dv_s, 2)]

        @pl.when(step > 0)
        def _():
            for cp in copies:
                cp.wait()

        @pl.when(n == 0)
        def _():
            ck_s[...] = jnp.zeros_like(ck_s)
            cv_s[...] = jnp.zeros_like(cv_s)

        row = lax.broadcasted_iota(jnp.int32, (Q_ROWS, Q_ROWS + ATTN_BLOCK), 0)
        col = lax.broadcasted_iota(jnp.int32, (Q_ROWS, Q_ROWS + ATTN_BLOCK), 1)
        steps = row + ATTN_BLOCK - col
        valid = (steps >= 0) & (steps <= ATTN_BLOCK) & ((i > 0) | (col >= ATTN_BLOCK))
        negdist = -(steps * d).astype(F32)
        nt = (((1,), (1,)), ((), ()))
        tn = (((0,), (0,)), ((), ()))
        for h in range(N_SLOTS):
            sl = slice(h * HEAD_DIM, (h + 1) * HEAD_DIM)
            q = q_ref[:, sl]
            k = jnp.concatenate([kp_ref[:, sl], kc_ref[:, sl]], axis=0)
            v = jnp.concatenate([vp_ref[:, sl], vc_ref[:, sl]], axis=0)
            dy = dy_ref[:, sl]
            sc = lax.dot_general(q, k, nt, preferred_element_type=F32) * SCALE + SLOPES[gi][h] * negdist
            p = jnp.where(valid, jnp.exp(sc - l_ref[:, h:h + 1]), 0.0)
            dp = lax.dot_general(dy, v, nt, preferred_element_type=F32)
            ds = (p * (dp - d_ref[:, h:h + 1])).astype(BF16)
            dq_s[:, sl] = (jnp.dot(ds, k, preferred_element_type=F32) * SCALE).astype(BF16)
            dk = lax.dot_general(ds, q, tn, preferred_element_type=F32) * SCALE
            dv = lax.dot_general(p.astype(BF16), dy, tn, preferred_element_type=F32)
            dk_s[:Q_ROWS - ATTN_BLOCK, sl] = dk[ATTN_BLOCK:Q_ROWS].astype(BF16)
            dv_s[:Q_ROWS - ATTN_BLOCK, sl] = dv[ATTN_BLOCK:Q_ROWS].astype(BF16)
            dk_s[Q_ROWS - ATTN_BLOCK:, sl] = (ck_s[:, sl] + dk[Q_ROWS:]).astype(BF16)
            dv_s[Q_ROWS - ATTN_BLOCK:, sl] = (cv_s[:, sl] + dv[Q_ROWS:]).astype(BF16)
            ck_s[:, sl] = dk[:ATTN_BLOCK]
            cv_s[:, sl] = dv[:ATTN_BLOCK]

        for cp in copies:
            cp.start()

        @pl.when(step == n_steps - 1)
        def _():
            for cp in copies:
                cp.wait()

    stage = pltpu.VMEM((Q_ROWS, D_MODEL), BF16)
    carry = pltpu.VMEM((ATTN_BLOCK, D_MODEL), F32)
    return pl.pallas_call(
        body, name=f"attn_bwd_d{d}", grid=(d, nq),
        in_specs=[rev(gi), rev(3 + gi, prev=True), rev(3 + gi), rev(6 + gi, prev=True), rev(6 + gi),
                  rev(0), rev(0, STAT_W), rev(0, STAT_W), ANY],
        out_specs=ANY,
        out_shape=jax.ShapeDtypeStruct((s, ODD_IN), BF16),
        scratch_shapes=[stage, stage, stage, carry, carry, pltpu.SemaphoreType.DMA((3,))],
        input_output_aliases={8: 0},
        compiler_params=_cp(("arbitrary", "arbitrary"), 32),
    )(z, z, z, z, z, dyc, ltot, dst, dz)


def _odd_mix_fwd(z, os_, lses, cw):
    s = z.shape[0]

    def body(o0, o1, o2, l0, l1, l2, gc_ref, db_ref, dc_ref, dx_ref, gd_ref, hc_ref, hx_ref, cw_ref, y_ref, yc_ref, lt_ref,
             lt4_ref, lt16_ref, scr_o, scr_l):
        i = pl.program_id(0)
        _from_classes(scr_l, l1, 4)
        lse1 = _get(scr_l)
        _from_classes(scr_l, l2, 16)
        ls = [l0[...], lse1, _get(scr_l)]
        lmax = jnp.maximum(jnp.maximum(ls[0], ls[1]), ls[2])
        es = [jnp.exp(l - lmax) for l in ls]
        den = es[0] + es[1] + es[2]
        alpha = [e / den for e in es]
        ltot = lmax + jnp.log(den)
        lt_ref[...] = ltot
        _put(scr_l, ltot)
        _to_classes(lt4_ref, scr_l, 4, F32)
        _to_classes(lt16_ref, scr_l, 16, F32)
        _from_classes(scr_o, o1, 4)
        og1 = _get(scr_o)
        _from_classes(scr_o, o2, 16)
        og = [o0[...].astype(F32), og1, _get(scr_o)]
        gc = gc_ref[...].astype(F32)
        gate = gc * _sig(gc)
        for h in range(N_SLOTS):
            sl = slice(h * HEAD_DIM, (h + 1) * HEAD_DIM)
            yc = (alpha[0][:, h:h + 1] * og[0][:, sl] + alpha[1][:, h:h + 1] * og[1][:, sl]
                  + alpha[2][:, h:h + 1] * og[2][:, sl])
            yc_ref[:, sl] = yc.astype(BF16)
            y_ref[:, sl] = (yc * gate[:, sl]).astype(BF16)

        zc = dc_ref[...].astype(F32) * dx_ref[...].astype(F32)
        halo = jnp.where(i > 0, hc_ref[...].astype(F32) * hx_ref[...].astype(F32), 0.0)
        ext = jnp.concatenate([halo, zc], axis=0)
        z1 = pltpu.roll(ext, 1, 0)[HALO:]
        z2 = pltpu.roll(ext, 2, 0)[HALO:]
        conv = cw_ref[0:1, :] * z2 + cw_ref[1:2, :] * z1 + cw_ref[2:3, :] * zc
        gd = gd_ref[...].astype(F32)
        y_ref[:, D_MODEL:] = (db_ref[...].astype(F32) * conv * (gd * _sig(gd))).astype(BF16)

    row = pl.BlockSpec((TM, D_MODEL), lambda i: (i, 0))
    stat = pl.BlockSpec((TM, STAT_W), lambda i: (i, 0))
    y, ycr, lt, lt4, lt16 = pl.pallas_call(
        body, name="odd_mix_fwd", grid=(s // TM,),
        in_specs=[row, _class_spec(4, TM, D_MODEL), _class_spec(16, TM, D_MODEL),
                  stat, _class_spec(4, TM, STAT_W), _class_spec(16, TM, STAT_W),
                  _zcol(9), _zcol(10), _zcol(11), _zcol(12), _zcol(13), _prev_halo(11), _prev_halo(12), _full((3, D_MODEL))],
        out_specs=[pl.BlockSpec((TM, 2 * D_MODEL), lambda i: (i, 0)), row, stat, _class_spec(4, TM, STAT_W),
                   _class_spec(16, TM, STAT_W)],
        out_shape=[jax.ShapeDtypeStruct((s, 2 * D_MODEL), BF16), jax.ShapeDtypeStruct((s, D_MODEL), BF16),
                   jax.ShapeDtypeStruct((s, STAT_W), F32), jax.ShapeDtypeStruct((4, s // 4, STAT_W), F32),
                   jax.ShapeDtypeStruct((16, s // 16, STAT_W), F32)],
        scratch_shapes=[_token_scratch(TM, D_MODEL), _token_scratch(TM, STAT_W)],
        compiler_params=_cp(("parallel",), 48),
    )(os_[0], _class_major(os_[1], 4), _class_major(os_[2], 16), lses[0], _class_major(lses[1], 4),
      _class_major(lses[2], 16), z, z, z, z, z, z, z, cw)
    return y, ycr, [lt, lt4.reshape(s, STAT_W), lt16.reshape(s, STAT_W)]


def _odd_mix_bwd(dy, z, ycr, cw):
    s = z.shape[0]
    n_tiles = s // TM
    rest = ODD_IN - QKV_BLOCKS * D_MODEL

    def body(dy_ref, yc_ref, gc_ref, db_ref, dc_ref, dx_ref, gd_ref, hc_ref, hx_ref, dyn_ref, dbn_ref, gdn_ref, cw_ref,
             dz_ref, dyc_ref, dyc4_ref, dyc16_ref, dd_ref, dd4_ref, dd16_ref, dcw_ref, stage, sem, scr_o, scr_l):
        i = pl.program_id(0)
        out = pltpu.make_async_copy(
            stage, dz_ref.at[pl.ds(pl.multiple_of(i * TM, TM), TM), pl.ds(QKV_BLOCKS * D_MODEL, rest)], sem)

        @pl.when(i > 0)
        def _():
            out.wait()

        dyc_in = dy_ref[:, :D_MODEL].astype(F32)
        gc = gc_ref[...].astype(F32)
        sg = _sig(gc)
        yc = yc_ref[...].astype(F32)
        dyc = dyc_in * (gc * sg)
        dyc_ref[...] = dyc.astype(BF16)
        _put(scr_o, dyc)
        _to_classes(dyc4_ref, scr_o, 4, BF16)
        _to_classes(dyc16_ref, scr_o, 16, BF16)
        stage[:, 0:D_MODEL] = (dyc_in * yc * (sg * (1.0 + gc * (1.0 - sg)))).astype(BF16)
        prod = dyc * yc
        lane = lax.broadcasted_iota(jnp.int32, (TM, STAT_W), 1)
        stat = jnp.zeros((TM, STAT_W), F32)
        for h in range(N_SLOTS):
            part = jnp.sum(prod[:, h * HEAD_DIM:(h + 1) * HEAD_DIM], axis=-1, keepdims=True)
            stat = jnp.where(lane == h, part, stat)
        dd_ref[...] = stat
        _put(scr_l, stat)
        _to_classes(dd4_ref, scr_l, 4, F32)
        _to_classes(dd16_ref, scr_l, 16, F32)

        dc = dc_ref[...].astype(F32)
        dx = dx_ref[...].astype(F32)
        zc = dc * dx
        halo = jnp.where(i > 0, hc_ref[...].astype(F32) * hx_ref[...].astype(F32), 0.0)
        ext = jnp.concatenate([halo, zc], axis=0)
        z1 = pltpu.roll(ext, 1, 0)[HALO:]
        z2 = pltpu.roll(ext, 2, 0)[HALO:]
        w0, w1, w2 = cw_ref[0:1, :], cw_ref[1:2, :], cw_ref[2:3, :]
        conv = w0 * z2 + w1 * z1 + w2 * zc
        gd = gd_ref[...].astype(F32)
        sg = _sig(gd)
        sgd = gd * sg
        db = db_ref[...].astype(F32)
        dyd = dy_ref[:, D_MODEL:].astype(F32)
        dconv = dyd * db * sgd
        gdn = gdn_ref[...].astype(F32)
        dconv_n = jnp.where(i < n_tiles - 1, dyn_ref[...].astype(F32) * dbn_ref[...].astype(F32) * (gdn * _sig(gdn)), 0.0)
        extn = jnp.concatenate([dconv, dconv_n], axis=0)
        nrow = TM + HALO
        dzc = w2 * dconv + w1 * pltpu.roll(extn, nrow - 1, 0)[:TM] + w0 * pltpu.roll(extn, nrow - 2, 0)[:TM]
        stage[:, D_MODEL:2 * D_MODEL] = (dyd * conv * sgd).astype(BF16)
        stage[:, 2 * D_MODEL:3 * D_MODEL] = (dzc * dx).astype(BF16)
        stage[:, 3 * D_MODEL:4 * D_MODEL] = (dzc * dc).astype(BF16)
        stage[:, 4 * D_MODEL:5 * D_MODEL] = (dyd * db * conv * (sg * (1.0 + gd * (1.0 - sg)))).astype(BF16)
        @pl.when(i == 0)
        def _():
            dcw_ref[...] = jnp.zeros_like(dcw_ref)

        for tap, shifted in enumerate((z2, z1, zc)):
            dcw_ref[tap:tap + 1, :] += jnp.sum(dconv * shifted, axis=0, keepdims=True)

        out.start()

        @pl.when(i == n_tiles - 1)
        def _():
            out.wait()

    row = pl.BlockSpec((TM, D_MODEL), lambda i: (i, 0))
    stat = pl.BlockSpec((TM, STAT_W), lambda i: (i, 0))
    dz, dyc, dyc4, dyc16, dd, dd4, dd16, g_conv = pl.pallas_call(
        body, name="odd_mix_bwd", grid=(n_tiles,),
        in_specs=[pl.BlockSpec((TM, 2 * D_MODEL), lambda i: (i, 0)), row, _zcol(9), _zcol(10), _zcol(11), _zcol(12), _zcol(13),
                  _prev_halo(11), _prev_halo(12), _next_halo(1, s), _next_halo(10, s), _next_halo(13, s), _full((3, D_MODEL))],
        out_specs=[ANY, row, _class_spec(4, TM, D_MODEL), _class_spec(16, TM, D_MODEL),
                   stat, _class_spec(4, TM, STAT_W), _class_spec(16, TM, STAT_W), _full((3, D_MODEL))],
        out_shape=[jax.ShapeDtypeStruct((s, ODD_IN), BF16), jax.ShapeDtypeStruct((s, D_MODEL), BF16),
                   jax.ShapeDtypeStruct((4, s // 4, D_MODEL), BF16), jax.ShapeDtypeStruct((16, s // 16, D_MODEL), BF16),
                   jax.ShapeDtypeStruct((s, STAT_W), F32), jax.ShapeDtypeStruct((4, s // 4, STAT_W), F32),
                   jax.ShapeDtypeStruct((16, s // 16, STAT_W), F32), jax.ShapeDtypeStruct((3, D_MODEL), F32)],
        scratch_shapes=[pltpu.VMEM((TM, rest), BF16), pltpu.SemaphoreType.DMA(()), _token_scratch(TM, D_MODEL),
                        _token_scratch(TM, STAT_W)],
        compiler_params=_cp(("arbitrary",), 48),
    )(dy, ycr, z, z, z, z, z, z, z, dy, z, z, cw)
    dyc = [dyc, dyc4.reshape(s, D_MODEL), dyc16.reshape(s, D_MODEL)]
    dd = [dd, dd4.reshape(s, STAT_W), dd16.reshape(s, STAT_W)]
    return dz, dyc, dd, g_conv


def _cols_of_order(order):
    if order == 0:
        return (lambda j: jnp.where(j < 3, 3 * j, j + 6)), 8
    return (lambda j: 3 * j + order), 3


class _Hooks:
    def before_even(self):
        return None

    def odd_weights(self, w, x1):
        return w

    def odd_grads_ready(self, g_w_in_o, g_w_out_o):
        return None

    def even_mix_done(self, dz_e):
        return None

    def backward_done(self, dx0):
        return None


def _local_step(x, target, w, hooks=_Hooks()):
    tril = jnp.tril(jnp.ones((CHUNK, CHUNK), bool))
    wt = jnp.where(tril[None], w["ws"], 0.0).astype(BF16)
    wtt = jnp.swapaxes(wt, 1, 2)
    bs = w["bs"].reshape(4, CHUNK, 1)

    h_e = _rms_fwd("rms_fwd_even", x, w["even_norm"], after=hooks.before_even())
    z_e = _mm_nn("even_in_proj", h_e, w["w_in_e"], MM_ROWS, 1280, BF16)
    y_e = _even_mix_fwd(z_e, w["pool_w"], w["pool_scale"], wt, bs)
    x1 = _mm_nn("even_out_proj", y_e, w["w_out_e"], MM_ROWS, 1024, F32, resid=x)
    w = hooks.odd_weights(w, x1)
    h_o = _rms_fwd_orders("rms_fwd_odd", x1, w["odd_norm"])
    z_o = None
    for o in range(3):
        cols, n_cols = _cols_of_order(o)
        z_o = _mm_nn(f"odd_in_proj_o{o}", h_o[o], w["w_in_o"], MM_ROWS, D_MODEL, BF16, col_map=cols, n_cols=n_cols,
                     into=z_o)
    att = [_attn_fwd(z_o, gi) for gi in range(3)]
    y_o, ycr, ltot = _odd_mix_fwd(z_o, [a[0] for a in att], [a[1] for a in att], w["conv_w"])
    x2 = _mm_nn("odd_out_proj", y_o, w["w_out_o"], MM_ROWS, 1024, F32, resid=x1)
    dx2, loss8, g_final = _final_loss(x2, w["final_norm"], target)

    g_w_out_o = _mm_tn("odd_out_proj_dw", y_o, dx2, 1024, DW_OUT_TOKENS)
    dy_o = _mm_nt("odd_out_proj_dy", dx2, w["w_out_o"], MM_ROWS, 1024, BF16)
    dz_o, dyc, dst, g_conv = _odd_mix_bwd(dy_o, z_o, ycr, w["conv_w"])
    for gi in range(3):
        dz_o = _attn_bwd(z_o, dyc[gi], ltot[gi], dst[gi], dz_o, gi)
    g_w_in_o, dh_o = None, []
    for o in range(3):
        cols, n_cols = _cols_of_order(o)
        g_w_in_o = _mm_tn(f"odd_in_proj_dw_o{o}", h_o[o], dz_o, D_MODEL, DW_IN_TOKENS, col_map=cols, n_cols=n_cols,
                          into=g_w_in_o)
        dh_o.append(_mm_nt(f"odd_in_proj_dh_o{o}", dz_o, w["w_in_o"], MM_ROWS, D_MODEL, F32, k_map=cols, nk=n_cols))
    dx1, g_odd_norm = _rms_bwd("rms_bwd_odd", dh_o[0], x1, w["odd_norm"], dx2, dh4=dh_o[1], dh16=dh_o[2])
    after = hooks.odd_grads_ready(g_w_in_o, g_w_out_o)
    g_w_out_e = _mm_tn("even_out_proj_dw", y_e, dx1, 1024, DW_OUT_TOKENS, after=after)
    dy_e = _mm_nt("even_out_proj_dy", dx1, w["w_out_e"], MM_ROWS, 1024, BF16)
    dz_e, g_pw, g_ps, g_ws, g_bs = _even_mix_bwd(dy_e, z_e, w["pool_w"], w["pool_scale"], wt, wtt, bs)
    after = hooks.even_mix_done(dz_e)
    g_w_in_e = _mm_tn("even_in_proj_dw", h_e, dz_e, 1280, DW_IN_TOKENS, after=after)
    dh_e = _mm_nt("even_in_proj_dh", dz_e, w["w_in_e"], MM_ROWS, 1280, F32)
    dx0, g_even_norm = _rms_bwd("rms_bwd_even", dh_e, x, w["even_norm"], dx1)
    hooks.backward_done(dx0)

    grads = dict(w_in_e=g_w_in_e, pool_w=g_pw, w_out_e=g_w_out_e, w_in_o=g_w_in_o, w_out_o=g_w_out_o,
                 even_norm=g_even_norm, pool_scale=g_ps, ws=g_ws, bs=g_bs[:, :4].T, final_norm=g_final,
                 odd_norm=g_odd_norm, conv_w=g_conv)
    return loss8[0, 0], dx0, grads


class _Big(NamedTuple):
    name: str
    full: tuple
    haxis: int
    kaxis: int
    sub: int


BIGS = (
    _Big("w_in_e", (1024, 5120), 0, 1, 2),
    _Big("pool_w", (4, 256, 256), 0, 1, 1),
    _Big("w_out_e", (2048, 1024), 1, 0, 1),
    _Big("w_in_o", (1024, 14336), 0, 1, 4),
    _Big("w_out_o", (2048, 1024), 1, 0, 1),
)
N_BIG = len(BIGS)


def _shape(b, half=False, shard=False):
    return tuple(n // (2 if (half and ax == b.haxis) else 1) // (4 if (shard and ax == b.kaxis) else 1)
                 for ax, n in enumerate(b.full))


def _at(ref, b, h=None, k=None):
    idx = []
    for ax, n in enumerate(b.full):
        if ax == b.haxis and h is not None:
            idx.append(pl.ds(h * (n // 2), n // 2))
        elif ax == b.kaxis and k is not None:
            idx.append(pl.ds(k * (n // 4), n // 4))
        else:
            idx.append(slice(None))
    return ref.at[tuple(idx)]


def _place():
    x, y, c = lax.axis_index("x"), lax.axis_index("y"), lax.axis_index("c")
    chips = [(1 - x, y), (x, 1 - y), (1 - x, 1 - y)]
    return x, y, c, 2 * x + y, chips, [2 * cx + cy for cx, cy in chips]


def _gather_weights(bigs, shards, tiny):
    nb = len(bigs)

    def body(*refs):
        ins, tiny_in = refs[:nb], refs[nb]
        outs, tiny_out = refs[nb + 1:2 * nb + 1], refs[2 * nb + 1]
        send, recv, loc = refs[2 * nb + 2:]
        x, y, c, k_me, chips, ks = _place()
        sib = (x, y, 1 - c)

        def rc(src, dst, sem, to):
            return pltpu.make_async_remote_copy(src_ref=src, dst_ref=dst, send_sem=send.at[sem], recv_sem=recv.at[sem],
                                                device_id=to, device_id_type=MESH)

        started = []
        for a, b in enumerate(bigs):
            own = pltpu.make_async_copy(ins[a], _at(outs[a], b, k=k_me), loc.at[a])
            own.start()
            started.append(own)
        own = pltpu.make_async_copy(tiny_in, tiny_out.at[k_me], loc.at[nb])
        own.start()
        started.append(own)
        sends = []
        for j, chip in enumerate(chips):
            for a, b in enumerate(bigs):
                sends.append(rc(_at(ins[a], b, h=c), _at(outs[a], b, h=c, k=k_me), 6 * a + j, (*chip, c)))
            sends.append(rc(tiny_in, tiny_out.at[k_me], 6 * nb + j, (*chip, c)))
        for cp in sends:
            cp.start()
        for j in range(3):
            for a, b in enumerate(bigs):
                win = _at(outs[a], b, h=c, k=ks[j])
                rc(win, win, 6 * a + j, sib).wait_recv()
                fwd = rc(win, win, 6 * a + 3 + j, sib)
                fwd.start()
                sends.append(fwd)
            rc(tiny_in, tiny_out.at[ks[j]], 6 * nb + j, sib).wait_recv()
        for j in range(3):
            for a, b in enumerate(bigs):
                win = _at(outs[a], b, h=1 - c, k=ks[j])
                rc(win, win, 6 * a + 3 + j, sib).wait_recv()
        for cp in sends:
            cp.wait_send()
        for cp in started:
            cp.wait()

    n_sem = 6 * nb + 3
    return pl.pallas_call(
        body, name="gather_even_weights",
        in_specs=[ANY] * (nb + 1), out_specs=[ANY] * (nb + 1),
        out_shape=[jax.ShapeDtypeStruct(b.full, BF16) for b in bigs] + [jax.ShapeDtypeStruct((4,) + tiny.shape, F32)],
        scratch_shapes=[pltpu.SemaphoreType.DMA((n_sem,)), pltpu.SemaphoreType.DMA((n_sem,)),
                        pltpu.SemaphoreType.DMA((nb + 1,))],
    )(*shards, tiny)


def _copies_to_chips(bigs):
    def copies(srcs, lands, send, recv, waiting=False):
        _, _, c, k_me, chips, ks = _place()
        out = []
        for t in range(2):
            for j in range(3):
                for a, b in enumerate(bigs):
                    n = 6 * a + 3 * t + j
                    if waiting:
                        dst, arrival = _at(lands[a], b, h=t, k=ks[j]), n
                    else:
                        dst, arrival = _at(lands[a], b, h=c, k=k_me), 6 * a + 3 * c + j
                    out.append(pltpu.make_async_remote_copy(
                        src_ref=_at(srcs[a], b, h=c), dst_ref=dst, send_sem=send.at[n], recv_sem=recv.at[arrival],
                        device_id=(*chips[j], t), device_id_type=MESH))
        return out
    return copies


def _copies_swap_halves(bigs):
    def copies(srcs, lands, send, recv, waiting=False):
        x, y, c, _, _, _ = _place()
        return [pltpu.make_async_remote_copy(
                    src_ref=_at(srcs[a], b, h=1 - c), dst_ref=lands[a], send_sem=send.at[a], recv_sem=recv.at[a],
                    device_id=(x, y, 1 - c), device_id_type=MESH)
                for a, b in enumerate(bigs)]
    return copies


def _copies_partials(bigs):
    def copies(srcs, lands, send, recv, waiting=False):
        _, _, c, _, chips, ks = _place()
        return [pltpu.make_async_remote_copy(
                    src_ref=_at(srcs[a], b, k=ks[j]), dst_ref=lands[a].at[j], send_sem=send.at[3 * a + j],
                    recv_sem=recv.at[3 * a + j], device_id=(*chips[j], c), device_id_type=MESH)
                for j in range(3) for a, b in enumerate(bigs)]
    return copies


def _exchange(name, srcs, land_shapes, copies_of, n_copies):
    ns = len(srcs)

    def body(*refs):
        copies = copies_of(refs[:ns], refs[ns:ns + len(land_shapes)], refs[-2], refs[-1])
        for cp in copies:
            cp.start()
        for cp in copies:
            cp.wait()

    return pl.pallas_call(
        body, name=name, in_specs=[ANY] * ns, out_specs=[ANY] * len(land_shapes), out_shape=land_shapes,
        scratch_shapes=[pltpu.SemaphoreType.DMA((n_copies,)), pltpu.SemaphoreType.DMA((n_copies,))],
    )(*srcs)


HBM = pl.BlockSpec(memory_space=pltpu.HBM)
SEM = pl.BlockSpec(memory_space=pltpu.SEMAPHORE)
SIDE_EFFECT = pltpu.SideEffectType.DATAFLOW_SIDE_EFFECTING


def _in_hbm(a):
    return pltpu.with_memory_space_constraint(a, pltpu.HBM)


def _exchange_start(name, srcs, land_shapes, copies_of, n_copies, after=None):
    ns, nl = len(srcs), len(land_shapes)
    lands = [lax.empty(sh.shape, sh.dtype) for sh in land_shapes]
    tail = [] if after is None else [after]
    n_in = ns + nl + len(tail)

    def body(*refs):
        send, recv, token = refs[n_in], refs[n_in + 1], refs[-1]
        for cp in copies_of(refs[:ns], refs[ns:ns + nl], send, recv):
            cp.start()
        token[...] = jnp.zeros_like(token)

    thru = [pltpu.HBM(a.shape, a.dtype) for a in (*srcs, *lands)]
    send, recv, *bufs, token = pl.pallas_call(
        body, name=name,
        out_shape=(pltpu.SemaphoreType.DMA((n_copies,)), pltpu.SemaphoreType.DMA((n_copies,)), *thru,
                   jax.ShapeDtypeStruct((8, 128), F32)),
        in_specs=[HBM] * (ns + nl) + [ANY] * len(tail),
        out_specs=(SEM, SEM, *([HBM] * (ns + nl)), pl.BlockSpec(memory_space=pltpu.VMEM)),
        input_output_aliases={i: 2 + i for i in range(ns + nl)},
        compiler_params=pltpu.CompilerParams(has_side_effects=SIDE_EFFECT),
    )(*[_in_hbm(a) for a in (*srcs, *lands)], *tail)
    return (send, recv, bufs, ns), token


def _exchange_wait(name, state, copies_of, after):
    send, recv, bufs, ns = state
    n = len(bufs)

    def body(*refs):
        ins = refs[:n]
        for cp in copies_of(ins[:ns], ins[ns:], refs[n], refs[n + 1], waiting=True):
            cp.wait_send()
            cp.wait_recv()

    out = pl.pallas_call(
        body, name=name, out_shape=tuple(pltpu.HBM(a.shape, a.dtype) for a in bufs),
        in_specs=[HBM] * n + [SEM, SEM, ANY], out_specs=tuple([HBM] * n),
        input_output_aliases={i: i for i in range(n)},
        compiler_params=pltpu.CompilerParams(has_side_effects=SIDE_EFFECT),
    )(*bufs, send, recv, after)
    return list(out[:ns]), list(out[ns:])


def _finish_gather(bigs, shards, fulls, k_arr):
    nb = len(bigs)

    def body(k_ref, *refs):
        ins, outs, loc = refs[:nb], refs[2 * nb:3 * nb], refs[3 * nb]
        own = [pltpu.make_async_copy(ins[a], _at(outs[a], b, k=k_ref[0]), loc.at[a]) for a, b in enumerate(bigs)]
        for cp in own:
            cp.start()
        for cp in own:
            cp.wait()

    return pl.pallas_call(
        body, name="gather_odd_finish",
        grid_spec=pltpu.PrefetchScalarGridSpec(num_scalar_prefetch=1, grid=(1,), in_specs=[ANY] * (2 * nb),
                                               out_specs=[ANY] * nb, scratch_shapes=[pltpu.SemaphoreType.DMA((nb,))]),
        out_shape=[jax.ShapeDtypeStruct(b.full, BF16) for b in bigs],
        input_output_aliases={1 + nb + a: a for a in range(nb)},
    )(k_arr, *shards, *fulls)


def _blk(b):
    win = _shape(b, half=True, shard=True)
    return (win[0] // b.sub,) + win[1:]


def _bidx(b, h, k, st):
    idx = [0] * len(b.full)
    idx[b.haxis] = h
    idx[b.kaxis] = k
    idx[0] = idx[0] * b.sub + st
    return tuple(idx)


def _chip_sum(b, g, got, c_arr):
    blk = _blk(b)

    def body(c_ref, g_ref, r_ref, o_ref):
        del c_ref
        o_ref[...] = (g_ref[...] + r_ref[...]).astype(BF16)

    half = pl.BlockSpec(blk, lambda k, st, c_ref: _bidx(b, 0, k, st))
    return pl.pallas_call(
        body, name=f"rs_chip_sum_{b.name}",
        grid_spec=pltpu.PrefetchScalarGridSpec(
            num_scalar_prefetch=1, grid=(4, b.sub),
            in_specs=[pl.BlockSpec(blk, lambda k, st, c_ref: _bidx(b, c_ref[0], k, st)), half], out_specs=half),
        out_shape=jax.ShapeDtypeStruct(_shape(b, half=True), BF16),
        compiler_params=_cp(("arbitrary", "arbitrary"), 40),
    )(c_arr, g, got)


def _half_shapes(bigs):
    return [jax.ShapeDtypeStruct(_shape(b, half=True), F32) for b in bigs]


def _partial_shapes(bigs):
    return [jax.ShapeDtypeStruct((3,) + _shape(b, half=True, shard=True), BF16) for b in bigs]


def _shard_sum(b, mine, got, ck_arr):
    blk = _blk(b)

    def body(ck_ref, m_ref, r0, r1, r2, o_ref):
        del ck_ref
        o_ref[...] = (m_ref[...].astype(F32) + r0[...].astype(F32)) + (r1[...].astype(F32) + r2[...].astype(F32))

    def peer(j):
        return pl.BlockSpec((None,) + blk, lambda st, ck: (j,) + _bidx(b, 0, 0, st))

    return pl.pallas_call(
        body, name=f"rs_shard_sum_{b.name}",
        grid_spec=pltpu.PrefetchScalarGridSpec(
            num_scalar_prefetch=1, grid=(b.sub,),
            in_specs=[pl.BlockSpec(blk, lambda st, ck: _bidx(b, 0, ck[1], st)), peer(0), peer(1), peer(2)],
            out_specs=pl.BlockSpec(blk, lambda st, ck: _bidx(b, ck[0], 0, st))),
        out_shape=jax.ShapeDtypeStruct(_shape(b, shard=True), F32),
        compiler_params=_cp(("arbitrary",), 40),
    )(ck_arr, mine, got, got, got)


def _share_halves(gs):
    def body(*refs):
        ins, outs, send, recv = refs[:N_BIG], refs[N_BIG:2 * N_BIG], refs[2 * N_BIG], refs[2 * N_BIG + 1]
        del ins
        x, y, c, _, _, _ = _place()
        copies = [pltpu.make_async_remote_copy(src_ref=_at(outs[a], b, h=c), dst_ref=_at(outs[a], b, h=c),
                                               send_sem=send.at[a], recv_sem=recv.at[a], device_id=(x, y, 1 - c),
                                               device_id_type=MESH)
                  for a, b in enumerate(BIGS)]
        for cp in copies:
            cp.start()
        for cp in copies:
            cp.wait()

    return pl.pallas_call(
        body, name="rs_share_halves", in_specs=[ANY] * N_BIG, out_specs=[ANY] * N_BIG,
        out_shape=[jax.ShapeDtypeStruct(_shape(b, shard=True), F32) for b in BIGS],
        scratch_shapes=[pltpu.SemaphoreType.DMA((N_BIG,)), pltpu.SemaphoreType.DMA((N_BIG,))],
        input_output_aliases={a: a for a in range(N_BIG)},
    )(*gs)


def _gather_small(block):
    m_per, n = block.shape

    def body(x_ref, out_ref, send_sems, recv_sems, local_sem):
        x, y, c = lax.axis_index("x"), lax.axis_index("y"), lax.axis_index("c")
        me, sibling = (x, y, c), (x, y, 1 - c)
        chips = [(1 - x, y), (x, 1 - y), (1 - x, 1 - y)]

        def rows(px, py, pc):
            return out_ref.at[pl.ds((4 * px + 2 * py + pc) * m_per, m_per), :]

        def copy(k, blk, to, src=None):
            return pltpu.make_async_remote_copy(
                src_ref=rows(*blk) if src is None else src, dst_ref=rows(*blk), send_sem=send_sems.at[k],
                recv_sem=recv_sems.at[k], device_id=to, device_id_type=MESH)

        mine = pltpu.make_async_copy(x_ref, rows(*me), local_sem)
        mine.start()
        first = [copy(0, me, sibling, src=x_ref)]
        first += [copy(1 + j, me, (*chip, c), src=x_ref) for j, chip in enumerate(chips)]
        for cp in first:
            cp.start()
        passed = [copy(4 + j, (*chip, c), sibling) for j, chip in enumerate(chips)]
        for j, chip in enumerate(chips):
            copy(1 + j, (*chip, c), me).wait_recv()
            passed[j].start()
        copy(0, sibling, me).wait_recv()
        for j, chip in enumerate(chips):
            copy(4 + j, (*chip, 1 - c), me).wait_recv()
        for cp in first + passed:
            cp.wait_send()
        mine.wait()

    return pl.pallas_call(
        body, name="gather_small_grads",
        out_shape=jax.ShapeDtypeStruct((8 * m_per, n), block.dtype),
        in_specs=[pl.BlockSpec(memory_space=pltpu.VMEM)], out_specs=pl.BlockSpec(memory_space=pltpu.VMEM),
        scratch_shapes=[pltpu.SemaphoreType.DMA((7,)), pltpu.SemaphoreType.DMA((7,)), pltpu.SemaphoreType.DMA],
    )(block)


def _sum_small(stack):
    _, m_per, n = stack.shape

    def body(x_ref, o_ref):
        acc = x_ref[0]
        for dev in range(1, 8):
            acc = acc + x_ref[dev]
        o_ref[...] = acc

    return pl.pallas_call(body, name="sum_small_grads", out_shape=jax.ShapeDtypeStruct((m_per, n), F32))(stack)


def _adamw(name, w, g, m, v, rows):
    shape = w.shape

    def body(w_ref, g_ref, m_ref, v_ref, d_ref, mo_ref, vo_ref):
        gg = g_ref[...]
        mn = ADAM_B1 * m_ref[...] + (1.0 - ADAM_B1) * gg
        vn = ADAM_B2 * v_ref[...] + (1.0 - ADAM_B2) * (gg * gg)
        m_hat = mn / (1.0 - ADAM_B1 ** ADAM_STEP)
        v_hat = vn / (1.0 - ADAM_B2 ** ADAM_STEP)
        d_ref[...] = -ADAM_LR * (m_hat / (jnp.sqrt(v_hat) + ADAM_EPS) + ADAM_WD * w_ref[...])
        mo_ref[...] = mn
        vo_ref[...] = vn

    spec = pl.BlockSpec((rows,) + shape[1:], lambda i: (i,) + (0,) * (len(shape) - 1))
    return pl.pallas_call(
        body, name=name, grid=(shape[0] // rows,), in_specs=[spec] * 4, out_specs=[spec] * 3,
        out_shape=[jax.ShapeDtypeStruct(shape, F32)] * 3, compiler_params=_cp(("parallel",), 48),
    )(w, g, m, v)


ADAM_ROWS = dict(w_in_e=256, pool_w=4, w_out_e=256, w_in_o=128, w_out_o=256)


def _pack(parts, rows):
    flat = jnp.concatenate([p.reshape(-1).astype(F32) for p in parts])
    return jnp.pad(flat, (0, rows * 128 - flat.shape[0])).reshape(rows, 128)


def _unpack(buf, shapes):
    flat = buf.reshape(-1)
    out, off = [], 0
    for shp in shapes:
        n = 1
        for dim in shp:
            n *= dim
        out.append(flat[off:off + n].reshape(shp))
        off += n
    return out


WEIGHTS = ("even_norm", "even_w_in", "even_pool_w", "even_pool_scale", "even_ws", "even_bs", "even_w_out", "odd_norm",
           "odd_w_in", "odd_conv_w", "odd_w_out", "final_norm")
BIG_OF = dict(w_in_e="even_w_in", pool_w="even_pool_w", w_out_e="even_w_out", w_in_o="odd_w_in", w_out_o="odd_w_out")
SMALL = ("even_norm", "even_pool_scale", "even_ws", "even_bs", "final_norm", "odd_norm", "odd_conv_w")
SMALL_GRAD_ROWS = 576
SMALL_STATE_ROWS = 552


def kernel(x, even_norm, even_w_in, even_pool_w, even_pool_scale, even_ws, even_bs, even_w_out, odd_norm, odd_w_in, odd_conv_w, odd_w_out, final_norm, loss_target, m_even_norm, m_even_w_in, m_even_pool_w, m_even_pool_scale, m_even_ws, m_even_bs, m_even_w_out, m_odd_norm, m_odd_w_in, m_odd_conv_w, m_odd_w_out, m_final_norm, v_even_norm, v_even_w_in, v_even_pool_w, v_even_pool_scale, v_even_ws, v_even_bs, v_even_w_out, v_odd_norm, v_odd_w_in, v_odd_conv_w, v_odd_w_out, v_final_norm):
    wv = dict(zip(WEIGHTS, (even_norm, even_w_in, even_pool_w, even_pool_scale, even_ws, even_bs, even_w_out, odd_norm,
                            odd_w_in, odd_conv_w, odd_w_out, final_norm)))
    mv = dict(zip(WEIGHTS, (m_even_norm, m_even_w_in, m_even_pool_w, m_even_pool_scale, m_even_ws, m_even_bs,
                            m_even_w_out, m_odd_norm, m_odd_w_in, m_odd_conv_w, m_odd_w_out, m_final_norm)))
    vv = dict(zip(WEIGHTS, (v_even_norm, v_even_w_in, v_even_pool_w, v_even_pool_scale, v_even_ws, v_even_bs,
                            v_even_w_out, v_odd_norm, v_odd_w_in, v_odd_conv_w, v_odd_w_out, v_final_norm)))
    c = lax.axis_index("c")
    k_me = 2 * lax.axis_index("x") + lax.axis_index("y")

    c_arr = jnp.reshape(c, (1,)).astype(jnp.int32)
    ck_arr = jnp.stack([c, k_me]).astype(jnp.int32)
    even_bigs, odd_bigs = BIGS[:3], BIGS[3:]

    shards = {b.name: wv[BIG_OF[b.name]][0].astype(BF16) for b in BIGS}
    tiny = jnp.concatenate([odd_conv_w[0], odd_norm], axis=0)
    *full_even, tiny_all = _gather_weights(even_bigs, [shards[b.name] for b in even_bigs], tiny)
    tiny_full = jnp.transpose(tiny_all, (1, 0, 2)).reshape(4, D_MODEL)
    w = dict(zip((b.name for b in even_bigs), full_even))
    w.update(even_norm=even_norm, pool_scale=even_pool_scale, ws=even_ws[0], bs=even_bs[0],
             final_norm=final_norm.reshape(1, D_MODEL), conv_w=tiny_full[:3], odd_norm=tiny_full[3:4])
    to_chips, swap_odd, partials_odd = _copies_to_chips(odd_bigs), _copies_swap_halves(odd_bigs), _copies_partials(odd_bigs)
    gather_state, gather_token = _exchange_start(
        "gather_odd_start", [shards[b.name] for b in odd_bigs], [jax.ShapeDtypeStruct(b.full, BF16) for b in odd_bigs],
        to_chips, 6 * len(odd_bigs), after=full_even[-1])

    class Hooks(_Hooks):
        def before_even(self):
            return gather_token

        def odd_weights(self, w, x1):
            srcs, lands = _exchange_wait("gather_odd_wait", gather_state, to_chips, after=x1)
            fulls = _finish_gather(odd_bigs, srcs, lands, jnp.reshape(k_me, (1,)).astype(jnp.int32))
            return dict(w, **dict(zip((b.name for b in odd_bigs), fulls)))

        def odd_grads_ready(self, g_w_in_o, g_w_out_o):
            self.swap, token = _exchange_start("rs_odd_swap_start", [g_w_in_o, g_w_out_o], _half_shapes(odd_bigs),
                                               swap_odd, len(odd_bigs))
            return token

        def even_mix_done(self, dz_e):
            grads, got = _exchange_wait("rs_odd_swap_wait", self.swap, swap_odd, after=dz_e)
            sums = [_chip_sum(b, g, r, c_arr) for b, g, r in zip(odd_bigs, grads, got)]
            self.partials, token = _exchange_start("rs_odd_partials_start", sums, _partial_shapes(odd_bigs), partials_odd,
                                                   3 * len(odd_bigs))
            return token

        def backward_done(self, dx0):
            self.sums, self.parts = _exchange_wait("rs_odd_partials_wait", self.partials, partials_odd, after=dx0)

    hooks = Hooks()
    loss, dx, g = _local_step(x[0], loss_target[0], w, hooks)
    loss = lax.psum(loss, ("x", "y", "c"))

    got = _exchange("rs_even_swap", [g[b.name] for b in even_bigs], _half_shapes(even_bigs), _copies_swap_halves(even_bigs),
                    len(even_bigs))
    sums = [_chip_sum(b, g[b.name], r, c_arr) for b, r in zip(even_bigs, got)]
    parts = _exchange("rs_even_partials", sums, _partial_shapes(even_bigs), _copies_partials(even_bigs), 3 * len(even_bigs))
    halves = [_shard_sum(b, sm, p, ck_arr) for b, sm, p in zip(BIGS, sums + hooks.sums, list(parts) + hooks.parts)]
    g_shard = dict(zip((b.name for b in BIGS), _share_halves(halves)))

    small_g = _pack([g["even_norm"], g["pool_scale"], g["ws"], g["bs"], g["final_norm"], g["odd_norm"], g["conv_w"]],
                    SMALL_GRAD_ROWS)
    small_g = _sum_small(_gather_small(small_g).reshape(8, SMALL_GRAD_ROWS, 128))
    g_en, g_ps, g_ws, g_bs, g_fn, g_on, g_cw = _unpack(
        small_g, [(1, D_MODEL), (1, D_MODEL), (1, 4, CHUNK, CHUNK), (1, 4, CHUNK), (D_MODEL,), (1, D_MODEL), (1, 3, D_MODEL)])
    g_on = lax.dynamic_slice(g_on, (0, k_me * 256), (1, 256))
    g_cw = lax.dynamic_slice(g_cw, (0, 0, k_me * 256), (1, 3, 256))
    grad = dict(even_norm=g_en, even_pool_scale=g_ps, even_ws=g_ws, even_bs=g_bs, final_norm=g_fn, odd_norm=g_on,
                odd_conv_w=g_cw)
    for b in BIGS:
        grad[BIG_OF[b.name]] = g_shard[b.name][None]

    delta, new_m, new_v = {}, {}, {}
    for b in BIGS:
        n = BIG_OF[b.name]
        d_, m_, v_ = _adamw(f"adamw_{b.name}", wv[n][0], g_shard[b.name], mv[n][0], vv[n][0], ADAM_ROWS[b.name])
        delta[n], new_m[n], new_v[n] = d_[None], m_[None], v_[None]
    shapes = [wv[n].shape for n in SMALL]
    packed = [_pack([src[n] for n in SMALL], SMALL_STATE_ROWS) for src in (wv, grad, mv, vv)]
    outs = _adamw("adamw_small", *packed, SMALL_STATE_ROWS)
    for dst, buf in zip((delta, new_m, new_v), outs):
        for n, arr in zip(SMALL, _unpack(buf, shapes)):
            dst[n] = arr

    return (loss, dx[None], *[grad[n] for n in WEIGHTS], *[delta[n] for n in WEIGHTS], *[new_m[n] for n in WEIGHTS],
            *[new_v[n] for n in WEIGHTS])
```

```python
from typing import NamedTuple

import jax
import jax.numpy as jnp
from jax import lax
from jax.experimental import pallas as pl
from jax.experimental.pallas import tpu as pltpu

F32, BF16 = jnp.float32, jnp.bfloat16

D_MODEL = 1024
EPS = 1e-6
NEG = -1e30
POOL_SIZES = (2, 4, 8, 16)
GROUP_W = 256
CHUNK = 128
DILATIONS = (1, 4, 16)
N_SLOTS = 8
HEAD_DIM = 128
ATTN_BLOCK = 128
SCALE = HEAD_DIM ** -0.5
EVEN_IN = 5120
ODD_IN = 14336
QKV_BLOCKS = 9
ODD_BLOCKS = ODD_IN // D_MODEL
SLOPES = tuple(tuple(2.0 ** (-8.0 * (g * N_SLOTS + s + 1) / (3 * N_SLOTS)) for s in range(N_SLOTS)) for g in range(3))

ADAM_LR, ADAM_B1, ADAM_B2, ADAM_EPS, ADAM_WD, ADAM_STEP = 0.001, 0.9, 0.999, 1e-08, 0.01, 10

HALO = 16
TS = 512
TM = 256
MM_ROWS = 1024
DW_IN_TOKENS = 2048
DW_OUT_TOKENS = 1024
MIB = 1 << 20
MESH = pl.DeviceIdType.MESH
ANY = pl.BlockSpec(memory_space=pl.ANY)


def _cp(sem, vmem_mib):
    return pltpu.CompilerParams(dimension_semantics=sem, vmem_limit_bytes=vmem_mib * MIB)


def _sig(x):
    return 1.0 / (1.0 + jnp.exp(-x))


def _win_sum(e, w, forward):
    n = e.shape[0]
    k = 1
    while k < w:
        e = e + pltpu.roll(e, (n - k) if forward else k, 0)
        k *= 2
    return e


def _mm_nn(name, a, b, tm, tn, out_dtype, resid=None, col_map=None, n_cols=None, into=None):
    m, k = a.shape
    n = b.shape[1]
    if col_map is None:
        col_map, n_cols = (lambda j: j), n // tn

    def body(*refs):
        a_ref, b_ref = refs[0], refs[1]
        acc = jnp.dot(a_ref[...].astype(BF16), b_ref[...], preferred_element_type=F32)
        if resid is not None:
            acc = acc + refs[2][...]
        o_ref = refs[-1]
        o_ref[...] = acc.astype(out_dtype)

    in_specs = [pl.BlockSpec((tm, k), lambda j, i: (i, 0)), pl.BlockSpec((k, tn), lambda j, i: (0, col_map(j)))]
    args = [a, b]
    if resid is not None:
        in_specs.append(pl.BlockSpec((tm, tn), lambda j, i: (i, col_map(j))))
        args.append(resid)
    aliases = {}
    if into is not None:
        aliases = {len(args): 0}
        in_specs.append(ANY)
        args.append(into)
    return pl.pallas_call(
        body, name=name, grid=(n_cols, m // tm), in_specs=in_specs,
        out_specs=pl.BlockSpec((tm, tn), lambda j, i: (i, col_map(j))),
        out_shape=jax.ShapeDtypeStruct((m, n), out_dtype), input_output_aliases=aliases,
        compiler_params=_cp(("parallel", "parallel"), 48),
    )(*args)


def _mm_nt(name, a, b, tm, tk, out_dtype, k_map=None, nk=None):
    m, k = a.shape
    n = b.shape[0]
    if k_map is None:
        k_map, nk = (lambda kk: kk), k // tk

    def body(a_ref, b_ref, o_ref, acc_ref):
        kk = pl.program_id(1)
        p = lax.dot_general(a_ref[...].astype(BF16), b_ref[...], (((1,), (1,)), ((), ())), preferred_element_type=F32)
        if nk == 1:
            o_ref[...] = p.astype(out_dtype)
        else:
            @pl.when(kk == 0)
            def _():
                acc_ref[...] = p

            @pl.when(kk > 0)
            def _():
                acc_ref[...] += p

            @pl.when(kk == nk - 1)
            def _():
                o_ref[...] = acc_ref[...].astype(out_dtype)

    return pl.pallas_call(
        body, name=name, grid=(m // tm, nk),
        in_specs=[pl.BlockSpec((tm, tk), lambda i, kk: (i, k_map(kk))), pl.BlockSpec((n, tk), lambda i, kk: (0, k_map(kk)))],
        out_specs=pl.BlockSpec((tm, n), lambda i, kk: (i, 0)),
        out_shape=jax.ShapeDtypeStruct((m, n), out_dtype),
        scratch_shapes=[pltpu.VMEM((tm, n) if nk > 1 else (8, 128), F32)],
        compiler_params=_cp(("parallel", "arbitrary"), 56),
    )(a, b)


def _mm_tn(name, a, g, tn, ts, col_map=None, n_cols=None, into=None, after=None):
    s, ka = a.shape
    n = g.shape[1]
    if col_map is None:
        col_map, n_cols = (lambda j: j), n // tn

    def body(a_ref, g_ref, *rest):
        o_ref = rest[-1]
        st = pl.program_id(1)
        p = lax.dot_general(a_ref[...], g_ref[...].astype(BF16), (((0,), (0,)), ((), ())), preferred_element_type=F32)

        @pl.when(st == 0)
        def _():
            o_ref[...] = p

        @pl.when(st > 0)
        def _():
            o_ref[...] += p

    in_specs = [pl.BlockSpec((ts, ka), lambda j, st: (st, 0)), pl.BlockSpec((ts, tn), lambda j, st: (st, col_map(j)))]
    args = [a, g]
    aliases = {}
    if into is not None:
        aliases = {2: 0}
        in_specs.append(ANY)
        args.append(into)
    if after is not None:
        in_specs.append(ANY)
        args.append(after)
    return pl.pallas_call(
        body, name=name, grid=(n_cols, s // ts), in_specs=in_specs,
        out_specs=pl.BlockSpec((ka, tn), lambda j, st: (0, col_map(j))),
        out_shape=jax.ShapeDtypeStruct((ka, n), F32), input_output_aliases=aliases,
        compiler_params=_cp(("parallel", "arbitrary"), 56),
    )(*args)


def _rms_fwd(name, x, g, after=None):
    s = x.shape[0]

    def body(x_ref, g_ref, *rest):
        xf = x_ref[...]
        r = lax.rsqrt(jnp.mean(xf * xf, axis=-1, keepdims=True) + EPS)
        rest[-1][...] = (xf * r * g_ref[...]).astype(BF16)

    row = pl.BlockSpec((TS, D_MODEL), lambda i: (i, 0))
    in_specs, args = [row, pl.BlockSpec((1, D_MODEL), lambda i: (0, 0))], [x, g]
    if after is not None:
        in_specs.append(ANY)
        args.append(after)
    return pl.pallas_call(
        body, name=name, grid=(s // TS,), in_specs=in_specs, out_specs=row,
        out_shape=jax.ShapeDtypeStruct((s, D_MODEL), BF16), compiler_params=_cp(("parallel",), 32),
    )(*args)


def _class_major(a, d):
    return a.reshape(d, a.shape[0] // d, a.shape[1])


def _class_spec(d, tile, width):
    return pl.BlockSpec((d, tile // d, width), lambda i: (0, i, 0))


LANES = 128


def _token_scratch(tile, width):
    return pltpu.VMEM((width // LANES, tile, LANES), F32)


def _put(scr, val):
    for c in range(scr.shape[0]):
        scr[c] = val[:, c * LANES:(c + 1) * LANES]


def _get(scr):
    return jnp.concatenate([scr[c] for c in range(scr.shape[0])], axis=1)


def _to_classes(ref3, scr, d, dtype):
    n = ref3.shape[1]
    for c in range(scr.shape[0]):
        for r in range(d):
            ref3[r, :, c * LANES:(c + 1) * LANES] = scr.at[c][pl.ds(r, n, stride=d), :].astype(dtype)


def _from_classes(scr, ref3, d):
    n = ref3.shape[1]
    for c in range(scr.shape[0]):
        for r in range(d):
            scr.at[c][pl.ds(r, n, stride=d), :] = ref3[r, :, c * LANES:(c + 1) * LANES].astype(F32)


def _rms_fwd_orders(name, x, g):
    s = x.shape[0]

    def body(x_ref, g_ref, h_ref, h4_ref, h16_ref, scr):
        xf = x_ref[...]
        r = lax.rsqrt(jnp.mean(xf * xf, axis=-1, keepdims=True) + EPS)
        h = xf * r * g_ref[...]
        h_ref[...] = h.astype(BF16)
        _put(scr, h)
        _to_classes(h4_ref, scr, 4, BF16)
        _to_classes(h16_ref, scr, 16, BF16)

    row = pl.BlockSpec((TS, D_MODEL), lambda i: (i, 0))
    h, h4, h16 = pl.pallas_call(
        body, name=name, grid=(s // TS,), in_specs=[row, pl.BlockSpec((1, D_MODEL), lambda i: (0, 0))],
        out_specs=[row, _class_spec(4, TS, D_MODEL), _class_spec(16, TS, D_MODEL)],
        out_shape=[jax.ShapeDtypeStruct((s, D_MODEL), BF16), jax.ShapeDtypeStruct((4, s // 4, D_MODEL), BF16),
                   jax.ShapeDtypeStruct((16, s // 16, D_MODEL), BF16)],
        scratch_shapes=[_token_scratch(TS, D_MODEL)],
        compiler_params=_cp(("parallel",), 32),
    )(x, g)
    return h, h4.reshape(s, D_MODEL), h16.reshape(s, D_MODEL)


def _rms_bwd(name, dh, x, g, dres, dh4=None, dh16=None):
    s = x.shape[0]
    extra = dh4 is not None

    def body(dh_ref, x_ref, g_ref, dres_ref, *rest):
        if extra:
            dh4_ref, dh16_ref, dx_ref, dg_ref, scr = rest
        else:
            dx_ref, dg_ref = rest
        xf = x_ref[...]
        r = lax.rsqrt(jnp.mean(xf * xf, axis=-1, keepdims=True) + EPS)
        xh = xf * r
        dhf = dh_ref[...]
        if extra:
            _from_classes(scr, dh4_ref, 4)
            dhf = dhf + _get(scr)
            _from_classes(scr, dh16_ref, 16)
            dhf = dhf + _get(scr)
        dxh = dhf * g_ref[...]
        dx_ref[...] = dres_ref[...] + r * (dxh - xh * jnp.mean(dxh * xh, axis=-1, keepdims=True))
        part = jnp.sum(dhf * xh, axis=0, keepdims=True)

        @pl.when(pl.program_id(0) == 0)
        def _():
            dg_ref[...] = part

        @pl.when(pl.program_id(0) > 0)
        def _():
            dg_ref[...] += part

    row = pl.BlockSpec((TS, D_MODEL), lambda i: (i, 0))
    vec = pl.BlockSpec((1, D_MODEL), lambda i: (0, 0))
    in_specs, args, scratch = [row, row, vec, row], [dh, x, g, dres], []
    if extra:
        in_specs += [_class_spec(4, TS, D_MODEL), _class_spec(16, TS, D_MODEL)]
        args += [_class_major(dh4, 4), _class_major(dh16, 16)]
        scratch = [_token_scratch(TS, D_MODEL)]
    return pl.pallas_call(
        body, name=name, grid=(s // TS,), in_specs=in_specs, out_specs=[row, vec],
        out_shape=[jax.ShapeDtypeStruct((s, D_MODEL), F32), jax.ShapeDtypeStruct((1, D_MODEL), F32)],
        scratch_shapes=scratch, compiler_params=_cp(("arbitrary",), 40),
    )(*args)


def _final_loss(x, g, target):
    s = x.shape[0]

    def body(x_ref, g_ref, t_ref, dx_ref, loss_ref, dg_ref):
        xf = x_ref[...]
        gg = g_ref[...]
        r = lax.rsqrt(jnp.mean(xf * xf, axis=-1, keepdims=True) + EPS)
        xh = xf * r
        e = xh * gg - t_ref[...]
        dy = e * (1.0 / D_MODEL)
        dxh = dy * gg
        dx_ref[...] = r * (dxh - xh * jnp.mean(dxh * xh, axis=-1, keepdims=True))
        lpart = 0.5 * jnp.sum(jnp.mean(e * e, axis=-1, keepdims=True), axis=0, keepdims=True)
        lpart = jnp.broadcast_to(lpart, (8, 128))
        gpart = jnp.sum(dy * xh, axis=0, keepdims=True)

        @pl.when(pl.program_id(0) == 0)
        def _():
            loss_ref[...] = lpart
            dg_ref[...] = gpart

        @pl.when(pl.program_id(0) > 0)
        def _():
            loss_ref[...] += lpart
            dg_ref[...] += gpart

    row = pl.BlockSpec((TS, D_MODEL), lambda i: (i, 0))
    vec = pl.BlockSpec((1, D_MODEL), lambda i: (0, 0))
    return pl.pallas_call(
        body, name="final_loss", grid=(s // TS,), in_specs=[row, vec, row],
        out_specs=[row, pl.BlockSpec((8, 128), lambda i: (0, 0)), vec],
        out_shape=[jax.ShapeDtypeStruct((s, D_MODEL), F32), jax.ShapeDtypeStruct((8, 128), F32),
                   jax.ShapeDtypeStruct((1, D_MODEL), F32)],
        compiler_params=_cp(("arbitrary",), 40),
    )(x, g, target)


def _zcol(c, tm=TM):
    return pl.BlockSpec((tm, D_MODEL), lambda i, c=c: (i, c))


def _prev_halo(c, tm=TM):
    return pl.BlockSpec((HALO, D_MODEL), lambda i, c=c: (jnp.maximum(i * (tm // HALO) - 1, 0), c))


def _next_halo(c, n_rows, tm=TM):
    last = n_rows // HALO - 1
    return pl.BlockSpec((HALO, D_MODEL), lambda i, c=c: (jnp.minimum((i + 1) * (tm // HALO), last), c))


def _full(shape):
    return pl.BlockSpec(shape, lambda i: (0,) * len(shape))


def _inv_count(first_row, n, w):
    t = first_row + lax.broadcasted_iota(jnp.int32, (n, 1), 0)
    return 1.0 / jnp.minimum(t + 1, w).astype(F32)


def _even_mix_fwd(z, pw, ps, wt, bs):
    s = z.shape[0]

    def body(a_ref, ga_ref, u_ref, v_ref, gb_ref, halo_ref, pw_ref, ps_ref, wt_ref, bs_ref, y_ref):
        i = pl.program_id(0)
        a = a_ref[...].astype(F32)
        halo = jnp.where(i > 0, halo_ref[...].astype(F32), 0.0)
        ext = jnp.concatenate([halo, a], axis=0)
        ga = ga_ref[...].astype(F32)
        sga = ga * _sig(ga)
        for g, w in enumerate(POOL_SIZES):
            cs = slice(g * GROUP_W, (g + 1) * GROUP_W)
            win = _win_sum(ext[:, cs], w, False)[HALO:]
            pooled = win * _inv_count(i * TM, TM, w) - a[:, cs]
            mixed = jnp.dot(pooled.astype(BF16), pw_ref[g], preferred_element_type=F32)
            y_ref[:, cs] = (mixed * ps_ref[:, cs] * sga[:, cs]).astype(BF16)
        gb = gb_ref[...].astype(F32)
        gate = u_ref[...].astype(F32) * (gb * _sig(gb))
        for ch in range(TM // CHUNK):
            rs = slice(ch * CHUNK, (ch + 1) * CHUNK)
            for g in range(4):
                cs = slice(g * GROUP_W, (g + 1) * GROUP_W)
                mixb = jnp.dot(wt_ref[g], v_ref[rs, cs], preferred_element_type=F32) + bs_ref[g]
                y_ref[rs, D_MODEL + g * GROUP_W:D_MODEL + (g + 1) * GROUP_W] = (gate[rs, cs] * mixb).astype(BF16)

    return pl.pallas_call(
        body, name="even_mix_fwd", grid=(s // TM,),
        in_specs=[_zcol(0), _zcol(1), _zcol(2), _zcol(3), _zcol(4), _prev_halo(0),
                  _full((4, GROUP_W, GROUP_W)), _full((1, D_MODEL)), _full((4, CHUNK, CHUNK)), _full((4, CHUNK, 1))],
        out_specs=pl.BlockSpec((TM, 2 * D_MODEL), lambda i: (i, 0)),
        out_shape=jax.ShapeDtypeStruct((s, 2 * D_MODEL), BF16),
        compiler_params=_cp(("parallel",), 48),
    )(z, z, z, z, z, z, pw, ps, wt, bs)


def _even_mix_bwd(dy, z, pw, ps, wt, wtt, bs, after=None):
    s = z.shape[0]
    n_tiles = s // TM
    tail_specs, tail_args = ([ANY], [after]) if after is not None else ([], [])

    def body(dy_ref, a_ref, ga_ref, u_ref, v_ref, gb_ref, halo_ref, dyn_ref, gan_ref, pw_ref, ps_ref, wt_ref, wtt_ref,
             bs_ref, *rest):
        dz_ref, dpw_ref, dps_ref, dws_ref, dbs_ref = rest[-5:]
        i = pl.program_id(0)

        @pl.when(i == 0)
        def _():
            dpw_ref[...] = jnp.zeros_like(dpw_ref)
            dps_ref[...] = jnp.zeros_like(dps_ref)
            dws_ref[...] = jnp.zeros_like(dws_ref)
            dbs_ref[...] = jnp.zeros_like(dbs_ref)

        a = a_ref[...].astype(F32)
        halo = jnp.where(i > 0, halo_ref[...].astype(F32), 0.0)
        ext = jnp.concatenate([halo, a], axis=0)
        ga = ga_ref[...].astype(F32)
        sg = _sig(ga)
        sga = ga * sg
        dsga = sg * (1.0 + ga * (1.0 - sg))
        dya = dy_ref[:, :D_MODEL].astype(F32)
        gan = gan_ref[...].astype(F32)
        dmn_all = jnp.where(i < n_tiles - 1, dyn_ref[...].astype(F32) * ps_ref[...] * (gan * _sig(gan)), 0.0)
        for g, w in enumerate(POOL_SIZES):
            cs = slice(g * GROUP_W, (g + 1) * GROUP_W)
            inv = _inv_count(i * TM, TM, w)
            pooled = _win_sum(ext[:, cs], w, False)[HALO:] * inv - a[:, cs]
            pb = pooled.astype(BF16)
            mixed = jnp.dot(pb, pw_ref[g], preferred_element_type=F32)
            dyg = dya[:, cs]
            psg = ps_ref[:, cs]
            dm = (dyg * psg * sga[:, cs]).astype(BF16)
            dz_ref[:, D_MODEL + g * GROUP_W:D_MODEL + (g + 1) * GROUP_W] = (dyg * mixed * psg * dsga[:, cs]).astype(BF16)
            dps_ref[:, cs] += jnp.sum(dyg * mixed * sga[:, cs], axis=0, keepdims=True)
            dpw_ref[g] += lax.dot_general(pb, dm, (((0,), (0,)), ((), ())), preferred_element_type=F32)
            nt = (((1,), (1,)), ((), ()))
            dpool = lax.dot_general(dm, pw_ref[g], nt, preferred_element_type=F32)
            dpool_n = lax.dot_general(dmn_all[:, cs].astype(BF16), pw_ref[g], nt, preferred_element_type=F32)
            e = jnp.concatenate([dpool * inv, dpool_n * _inv_count((i + 1) * TM, HALO, w)], axis=0)
            dz_ref[:, cs] = (_win_sum(e, w, True)[:TM] - dpool).astype(BF16)

        gb = gb_ref[...].astype(F32)
        sg = _sig(gb)
        sgb = gb * sg
        dsgb = sg * (1.0 + gb * (1.0 - sg))
        u = u_ref[...].astype(F32)
        dyb = dy_ref[:, D_MODEL:].astype(F32)
        tril = lax.broadcasted_iota(jnp.int32, (CHUNK, CHUNK), 0) >= lax.broadcasted_iota(jnp.int32, (CHUNK, CHUNK), 1)
        lane = lax.broadcasted_iota(jnp.int32, (CHUNK, 128), 1)
        for ch in range(TM // CHUNK):
            rs = slice(ch * CHUNK, (ch + 1) * CHUNK)
            for g in range(4):
                cs = slice(g * GROUP_W, (g + 1) * GROUP_W)
                vb = v_ref[rs, cs]
                mixb = jnp.dot(wt_ref[g], vb, preferred_element_type=F32) + bs_ref[g]
                dyu = dyb[rs, cs] * u[rs, cs]
                dmix = dyu * sgb[rs, cs]
                dmb = dmix.astype(BF16)
                o = g * GROUP_W
                dz_ref[rs, 2 * D_MODEL + o:2 * D_MODEL + o + GROUP_W] = (dyb[rs, cs] * mixb * sgb[rs, cs]).astype(BF16)
                dz_ref[rs, 3 * D_MODEL + o:3 * D_MODEL + o + GROUP_W] = jnp.dot(
                    wtt_ref[g], dmb, preferred_element_type=F32).astype(BF16)
                dz_ref[rs, 4 * D_MODEL + o:4 * D_MODEL + o + GROUP_W] = (dyu * mixb * dsgb[rs, cs]).astype(BF16)
                dws = lax.dot_general(dmb, vb, (((1,), (1,)), ((), ())), preferred_element_type=F32)
                dws_ref[g] += jnp.where(tril, dws, 0.0)
                dbs_ref[...] += jnp.where(lane == g, jnp.sum(dmix, axis=1, keepdims=True), 0.0)

    return pl.pallas_call(
        body, name="even_mix_bwd", grid=(n_tiles,),
        in_specs=[pl.BlockSpec((TM, 2 * D_MODEL), lambda i: (i, 0)), _zcol(0), _zcol(1), _zcol(2), _zcol(3), _zcol(4),
                  _prev_halo(0), _next_halo(0, s), _next_halo(1, s),
                  _full((4, GROUP_W, GROUP_W)), _full((1, D_MODEL)), _full((4, CHUNK, CHUNK)), _full((4, CHUNK, CHUNK)),
                  _full((4, CHUNK, 1))] + tail_specs,
        out_specs=[pl.BlockSpec((TM, EVEN_IN), lambda i: (i, 0)), _full((4, GROUP_W, GROUP_W)), _full((1, D_MODEL)),
                   _full((4, CHUNK, CHUNK)), _full((CHUNK, 128))],
        out_shape=[jax.ShapeDtypeStruct((s, EVEN_IN), BF16), jax.ShapeDtypeStruct((4, GROUP_W, GROUP_W), F32),
                   jax.ShapeDtypeStruct((1, D_MODEL), F32), jax.ShapeDtypeStruct((4, CHUNK, CHUNK), F32),
                   jax.ShapeDtypeStruct((CHUNK, 128), F32)],
        compiler_params=_cp(("arbitrary",), 56),
    )(dy, z, z, z, z, z, z, dy, z, pw, ps, wt, wtt, bs, *tail_args)


STAT_W = 128
Q_BLOCKS = 2
Q_ROWS = Q_BLOCKS * ATTN_BLOCK


def _band(d):
    row = lax.broadcasted_iota(jnp.int32, (ATTN_BLOCK, 2 * ATTN_BLOCK), 0)
    col = lax.broadcasted_iota(jnp.int32, (ATTN_BLOCK, 2 * ATTN_BLOCK), 1)
    steps = row + ATTN_BLOCK - col
    return (steps >= 0) & (steps <= ATTN_BLOCK), col >= ATTN_BLOCK, -(steps * d).astype(F32)


def _attn_fwd(z, gi):
    s = z.shape[0]
    d = DILATIONS[gi]
    nb = s // d // ATTN_BLOCK
    nq = nb // Q_BLOCKS

    def spec(which, prev=False):
        cb = which * 3 + gi
        if prev:
            return pl.BlockSpec((ATTN_BLOCK, D_MODEL), lambda r, i: (r * nb + jnp.maximum(Q_BLOCKS * i - 1, 0), cb))
        return pl.BlockSpec((Q_ROWS, D_MODEL), lambda r, i: (r * nq + i, cb))

    def body(q_ref, kp_ref, kc_ref, vp_ref, vc_ref, o_ref, lse_ref):
        i = pl.program_id(1)
        inner, own, negdist = _band(d)
        lane = lax.broadcasted_iota(jnp.int32, (ATTN_BLOCK, STAT_W), 1)
        for b in range(Q_BLOCKS):
            rows = slice(b * ATTN_BLOCK, (b + 1) * ATTN_BLOCK)
            valid = (inner & ((i > 0) | own)) if b == 0 else inner
            stat = jnp.zeros((ATTN_BLOCK, STAT_W), F32)
            for h in range(N_SLOTS):
                sl = slice(h * HEAD_DIM, (h + 1) * HEAD_DIM)
                if b == 0:
                    k = jnp.concatenate([kp_ref[:, sl], kc_ref[:ATTN_BLOCK, sl]], axis=0)
                    v = jnp.concatenate([vp_ref[:, sl], vc_ref[:ATTN_BLOCK, sl]], axis=0)
                else:
                    k = kc_ref[(b - 1) * ATTN_BLOCK:(b + 1) * ATTN_BLOCK, sl]
                    v = vc_ref[(b - 1) * ATTN_BLOCK:(b + 1) * ATTN_BLOCK, sl]
                sc = lax.dot_general(q_ref[rows, sl], k, (((1,), (1,)), ((), ())), preferred_element_type=F32) * SCALE
                sc = jnp.where(valid, sc + SLOPES[gi][h] * negdist, NEG)
                m = jnp.max(sc, axis=-1, keepdims=True)
                p = jnp.exp(sc - m)
                l = jnp.sum(p, axis=-1, keepdims=True)
                o = jnp.dot((p / l).astype(BF16), v, preferred_element_type=F32)
                o_ref[rows, sl] = o.astype(BF16)
                stat = jnp.where(lane == h, m + jnp.log(l), stat)
            lse_ref[rows, :] = stat

    return pl.pallas_call(
        body, name=f"attn_fwd_d{d}", grid=(d, nq),
        in_specs=[spec(0), spec(1, True), spec(1), spec(2, True), spec(2)],
        out_specs=[pl.BlockSpec((Q_ROWS, D_MODEL), lambda r, i: (r * nq + i, 0)),
                   pl.BlockSpec((Q_ROWS, STAT_W), lambda r, i: (r * nq + i, 0))],
        out_shape=[jax.ShapeDtypeStruct((s, D_MODEL), BF16), jax.ShapeDtypeStruct((s, STAT_W), F32)],
        compiler_params=_cp(("parallel", "parallel"), 32),
    )(z, z, z, z, z)


def _attn_bwd(z, dyc, ltot, dst, dz, gi):
    s = z.shape[0]
    d = DILATIONS[gi]
    nb = s // d // ATTN_BLOCK
    nq = nb // Q_BLOCKS
    n_steps = d * nq

    def rev(cb, width=D_MODEL, prev=False):
        if prev:
            return pl.BlockSpec((ATTN_BLOCK, width), lambda r, n: (r * nb + jnp.maximum(Q_BLOCKS * (nq - 1 - n) - 1, 0), cb))
        return pl.BlockSpec((Q_ROWS, width), lambda r, n: (r * nq + nq - 1 - n, cb))

    def body(q_ref, kp_ref, kc_ref, vp_ref, vc_ref, dy_ref, l_ref, d_ref, dz_in, dz_out, dq_s, dk_s, dv_s, ck_s, cv_s, sems):
        del dz_in
        r = pl.program_id(0)
        n = pl.program_id(1)
        i = nq - 1 - n
        step = r * nq + n

        def out_copy(src, which):
            rows = pl.ds(pl.multiple_of((r * nq + i) * Q_ROWS, Q_ROWS), Q_ROWS)
            return pltpu.make_async_copy(src, dz_out.at[rows, pl.ds((which * 3 + gi) * D_MODEL, D_MODEL)], sems.at[which])

        copies = [out_copy(dq_s, 0), out_copy(dk_s, 1), out_copy(dv_s, 2)]

        @pl.when(step > 0)
        def _():
            for cp in copies:
                cp.wait()

        @pl.when(n == 0)
        def _():
            ck_s[...] = jnp.zeros_like(ck_s)
            cv_s[...] = jnp.zeros_like(cv_s)

        row = lax.broadcasted_iota(jnp.int32, (Q_ROWS, Q_ROWS + ATTN_BLOCK), 0)
        col = lax.broadcasted_iota(jnp.int32, (Q_ROWS, Q_ROWS + ATTN_BLOCK), 1)
        steps = row + ATTN_BLOCK - col
        valid = (steps >= 0) & (steps <= ATTN_BLOCK) & ((i > 0) | (col >= ATTN_BLOCK))
        negdist = -(steps * d).astype(F32)
        nt = (((1,), (1,)), ((), ()))
        tn = (((0,), (0,)), ((), ()))
        for h in range(N_SLOTS):
            sl = slice(h * HEAD_DIM, (h + 1) * HEAD_DIM)
            q = q_ref[:, sl]
            k = jnp.concatenate([kp_ref[:, sl], kc_ref[:, sl]], axis=0)
            v = jnp.concatenate([vp_ref[:, sl], vc_ref[:, sl]], axis=0)
            dy = dy_ref[:, sl]
            sc = lax.dot_general(q, k, nt, preferred_element_type=F32) * SCALE + SLOPES[gi][h] * negdist
            p = jnp.where(valid, jnp.exp(sc - l_ref[:, h:h + 1]), 0.0)
            dp = lax.dot_general(dy, v, nt, preferred_element_type=F32)
            ds = (p * (dp - d_ref[:, h:h + 1])).astype(BF16)
            dq_s[:, sl] = (jnp.dot(ds, k, preferred_element_type=F32) * SCALE).astype(BF16)
            dk = lax.dot_general(ds, q, tn, preferred_element_type=F32) * SCALE
            dv = lax.dot_general(p.astype(BF16), dy, tn, preferred_element_type=F32)
            dk_s[:Q_ROWS - ATTN_BLOCK, sl] = dk[ATTN_BLOCK:Q_ROWS].astype(BF16)
            dv_s[:Q_ROWS - ATTN_BLOCK, sl] = dv[ATTN_BLOCK:Q_ROWS].astype(BF16)
            dk_s[Q_ROWS - ATTN_BLOCK:, sl] = (ck_s[:, sl] + dk[Q_ROWS:]).astype(BF16)
            dv_s[Q_ROWS - ATTN_BLOCK:, sl] = (cv_s[:, sl] + dv[Q_ROWS:]).astype(BF16)
            ck_s[:, sl] = dk[:ATTN_BLOCK]
            cv_s[:, sl] = dv[:ATTN_BLOCK]

        for cp in copies:
            cp.start()

        @pl.when(step == n_steps - 1)
        def _():
            for cp in copies:
                cp.wait()

    stage = pltpu.VMEM((Q_ROWS, D_MODEL), BF16)
    carry = pltpu.VMEM((ATTN_BLOCK, D_MODEL), F32)
    return pl.pallas_call(
        body, name=f"attn_bwd_d{d}", grid=(d, nq),
        in_specs=[rev(gi), rev(3 + gi, prev=True), rev(3 + gi), rev(6 + gi, prev=True), rev(6 + gi),
                  rev(0), rev(0, STAT_W), rev(0, STAT_W), ANY],
        out_specs=ANY,
        out_shape=jax.ShapeDtypeStruct((s, ODD_IN), BF16),
        scratch_shapes=[stage, stage, stage, carry, carry, pltpu.SemaphoreType.DMA((3,))],
        input_output_aliases={8: 0},
        compiler_params=_cp(("arbitrary", "arbitrary"), 32),
    )(z, z, z, z, z, dyc, ltot, dst, dz)


def _odd_mix_fwd(z, os_, lses, cw):
    s = z.shape[0]

    def body(o0, o1, o2, l0, l1, l2, gc_ref, db_ref, dc_ref, dx_ref, gd_ref, hc_ref, hx_ref, cw_ref, y_ref, yc_ref, lt_ref,
             lt4_ref, lt16_ref, scr_o, scr_l):
        i = pl.program_id(0)
        _from_classes(scr_l, l1, 4)
        lse1 = _get(scr_l)
        _from_classes(scr_l, l2, 16)
        ls = [l0[...], lse1, _get(scr_l)]
        lmax = jnp.maximum(jnp.maximum(ls[0], ls[1]), ls[2])
        es = [jnp.exp(l - lmax) for l in ls]
        den = es[0] + es[1] + es[2]
        alpha = [e / den for e in es]
        ltot = lmax + jnp.log(den)
        lt_ref[...] = ltot
        _put(scr_l, ltot)
        _to_classes(lt4_ref, scr_l, 4, F32)
        _to_classes(lt16_ref, scr_l, 16, F32)
        _from_classes(scr_o, o1, 4)
        og1 = _get(scr_o)
        _from_classes(scr_o, o2, 16)
        og = [o0[...].astype(F32), og1, _get(scr_o)]
        gc = gc_ref[...].astype(F32)
        gate = gc * _sig(gc)
        for h in range(N_SLOTS):
            sl = slice(h * HEAD_DIM, (h + 1) * HEAD_DIM)
            yc = (alpha[0][:, h:h + 1] * og[0][:, sl] + alpha[1][:, h:h + 1] * og[1][:, sl]
                  + alpha[2][:, h:h + 1] * og[2][:, sl])
            yc_ref[:, sl] = yc.astype(BF16)
            y_ref[:, sl] = (yc * gate[:, sl]).astype(BF16)

        zc = dc_ref[...].astype(F32) * dx_ref[...].astype(F32)
        halo = jnp.where(i > 0, hc_ref[...].astype(F32) * hx_ref[...].astype(F32), 0.0)
        ext = jnp.concatenate([halo, zc], axis=0)
        z1 = pltpu.roll(ext, 1, 0)[HALO:]
        z2 = pltpu.roll(ext, 2, 0)[HALO:]
        conv = cw_ref[0:1, :] * z2 + cw_ref[1:2, :] * z1 + cw_ref[2:3, :] * zc
        gd = gd_ref[...].astype(F32)
        y_ref[:, D_MODEL:] = (db_ref[...].astype(F32) * conv * (gd * _sig(gd))).astype(BF16)

    row = pl.BlockSpec((TM, D_MODEL), lambda i: (i, 0))
    stat = pl.BlockSpec((TM, STAT_W), lambda i: (i, 0))
    y, ycr, lt, lt4, lt16 = pl.pallas_call(
        body, name="odd_mix_fwd", grid=(s // TM,),
        in_specs=[row, _class_spec(4, TM, D_MODEL), _class_spec(16, TM, D_MODEL),
                  stat, _class_spec(4, TM, STAT_W), _class_spec(16, TM, STAT_W),
                  _zcol(9), _zcol(10), _zcol(11), _zcol(12), _zcol(13), _prev_halo(11), _prev_halo(12), _full((3, D_MODEL))],
        out_specs=[pl.BlockSpec((TM, 2 * D_MODEL), lambda i: (i, 0)), row, stat, _class_spec(4, TM, STAT_W),
                   _class_spec(16, TM, STAT_W)],
        out_shape=[jax.ShapeDtypeStruct((s, 2 * D_MODEL), BF16), jax.ShapeDtypeStruct((s, D_MODEL), BF16),
                   jax.ShapeDtypeStruct((s, STAT_W), F32), jax.ShapeDtypeStruct((4, s // 4, STAT_W), F32),
                   jax.ShapeDtypeStruct((16, s // 16, STAT_W), F32)],
        scratch_shapes=[_token_scratch(TM, D_MODEL), _token_scratch(TM, STAT_W)],
        compiler_params=_cp(("parallel",), 48),
    )(os_[0], _class_major(os_[1], 4), _class_major(os_[2], 16), lses[0], _class_major(lses[1], 4),
      _class_major(lses[2], 16), z, z, z, z, z, z, z, cw)
    return y, ycr, [lt, lt4.reshape(s, STAT_W), lt16.reshape(s, STAT_W)]


def _odd_mix_bwd(dy, z, ycr, cw):
    s = z.shape[0]
    n_tiles = s // TM
    rest = ODD_IN - QKV_BLOCKS * D_MODEL

    def body(dy_ref, yc_ref, gc_ref, db_ref, dc_ref, dx_ref, gd_ref, hc_ref, hx_ref, dyn_ref, dbn_ref, gdn_ref, cw_ref,
             dz_ref, dyc_ref, dyc4_ref, dyc16_ref, dd_ref, dd4_ref, dd16_ref, dcw_ref, stage, sem, scr_o, scr_l):
        i = pl.program_id(0)
        out = pltpu.make_async_copy(
            stage, dz_ref.at[pl.ds(pl.multiple_of(i * TM, TM), TM), pl.ds(QKV_BLOCKS * D_MODEL, rest)], sem)

        @pl.when(i > 0)
        def _():
            out.wait()

        dyc_in = dy_ref[:, :D_MODEL].astype(F32)
        gc = gc_ref[...].astype(F32)
        sg = _sig(gc)
        yc = yc_ref[...].astype(F32)
        dyc = dyc_in * (gc * sg)
        dyc_ref[...] = dyc.astype(BF16)
        _put(scr_o, dyc)
        _to_classes(dyc4_ref, scr_o, 4, BF16)
        _to_classes(dyc16_ref, scr_o, 16, BF16)
        stage[:, 0:D_MODEL] = (dyc_in * yc * (sg * (1.0 + gc * (1.0 - sg)))).astype(BF16)
        prod = dyc * yc
        lane = lax.broadcasted_iota(jnp.int32, (TM, STAT_W), 1)
        stat = jnp.zeros((TM, STAT_W), F32)
        for h in range(N_SLOTS):
            part = jnp.sum(prod[:, h * HEAD_DIM:(h + 1) * HEAD_DIM], axis=-1, keepdims=True)
            stat = jnp.where(lane == h, part, stat)
        dd_ref[...] = stat
        _put(scr_l, stat)
        _to_classes(dd4_ref, scr_l, 4, F32)
        _to_classes(dd16_ref, scr_l, 16, F32)

        dc = dc_ref[...].astype(F32)
        dx = dx_ref[...].astype(F32)
        zc = dc * dx
        halo = jnp.where(i > 0, hc_ref[...].astype(F32) * hx_ref[...].astype(F32), 0.0)
        ext = jnp.concatenate([halo, zc], axis=0)
        z1 = pltpu.roll(ext, 1, 0)[HALO:]
        z2 = pltpu.roll(ext, 2, 0)[HALO:]
        w0, w1, w2 = cw_ref[0:1, :], cw_ref[1:2, :], cw_ref[2:3, :]
        conv = w0 * z2 + w1 * z1 + w2 * zc
        gd = gd_ref[...].astype(F32)
        sg = _sig(gd)
        sgd = gd * sg
        db = db_ref[...].astype(F32)
        dyd = dy_ref[:, D_MODEL:].astype(F32)
        dconv = dyd * db * sgd
        gdn = gdn_ref[...].astype(F32)
        dconv_n = jnp.where(i < n_tiles - 1, dyn_ref[...].astype(F32) * dbn_ref[...].astype(F32) * (gdn * _sig(gdn)), 0.0)
        extn = jnp.concatenate([dconv, dconv_n], axis=0)
        nrow = TM + HALO
        dzc = w2 * dconv + w1 * pltpu.roll(extn, nrow - 1, 0)[:TM] + w0 * pltpu.roll(extn, nrow - 2, 0)[:TM]
        stage[:, D_MODEL:2 * D_MODEL] = (dyd * conv * sgd).astype(BF16)
        stage[:, 2 * D_MODEL:3 * D_MODEL] = (dzc * dx).astype(BF16)
        stage[:, 3 * D_MODEL:4 * D_MODEL] = (dzc * dc).astype(BF16)
        stage[:, 4 * D_MODEL:5 * D_MODEL] = (dyd * db * conv * (sg * (1.0 + gd * (1.0 - sg)))).astype(BF16)
        @pl.when(i == 0)
        def _():
            dcw_ref[...] = jnp.zeros_like(dcw_ref)

        for tap, shifted in enumerate((z2, z1, zc)):
            dcw_ref[tap:tap + 1, :] += jnp.sum(dconv * shifted, axis=0, keepdims=True)

        out.start()

        @pl.when(i == n_tiles - 1)
        def _():
            out.wait()

    row = pl.BlockSpec((TM, D_MODEL), lambda i: (i, 0))
    stat = pl.BlockSpec((TM, STAT_W), lambda i: (i, 0))
    dz, dyc, dyc4, dyc16, dd, dd4, dd16, g_conv = pl.pallas_call(
        body, name="odd_mix_bwd", grid=(n_tiles,),
        in_specs=[pl.BlockSpec((TM, 2 * D_MODEL), lambda i: (i, 0)), row, _zcol(9), _zcol(10), _zcol(11), _zcol(12), _zcol(13),
                  _prev_halo(11), _prev_halo(12), _next_halo(1, s), _next_halo(10, s), _next_halo(13, s), _full((3, D_MODEL))],
        out_specs=[ANY, row, _class_spec(4, TM, D_MODEL), _class_spec(16, TM, D_MODEL),
                   stat, _class_spec(4, TM, STAT_W), _class_spec(16, TM, STAT_W), _full((3, D_MODEL))],
        out_shape=[jax.ShapeDtypeStruct((s, ODD_IN), BF16), jax.ShapeDtypeStruct((s, D_MODEL), BF16),
                   jax.ShapeDtypeStruct((4, s // 4, D_MODEL), BF16), jax.ShapeDtypeStruct((16, s // 16, D_MODEL), BF16),
                   jax.ShapeDtypeStruct((s, STAT_W), F32), jax.ShapeDtypeStruct((4, s // 4, STAT_W), F32),
                   jax.ShapeDtypeStruct((16, s // 16, STAT_W), F32), jax.ShapeDtypeStruct((3, D_MODEL), F32)],
        scratch_shapes=[pltpu.VMEM((TM, rest), BF16), pltpu.SemaphoreType.DMA(()), _token_scratch(TM, D_MODEL),
                        _token_scratch(TM, STAT_W)],
        compiler_params=_cp(("arbitrary",), 48),
    )(dy, ycr, z, z, z, z, z, z, z, dy, z, z, cw)
    dyc = [dyc, dyc4.reshape(s, D_MODEL), dyc16.reshape(s, D_MODEL)]
    dd = [dd, dd4.reshape(s, STAT_W), dd16.reshape(s, STAT_W)]
    return dz, dyc, dd, g_conv


def _cols_of_order(order):
    if order == 0:
        return (lambda j: jnp.where(j < 3, 3 * j, j + 6)), 8
    return (lambda j: 3 * j + order), 3


class _Hooks:
    def before_even(self):
        return None

    def odd_weights(self, w, x1):
        return w

    def odd_grads_ready(self, g_w_in_o, g_w_out_o):
        return None

    def even_mix_done(self, dz_e):
        return None

    def backward_done(self, dx0):
        return None


def _local_step(x, target, w, hooks=_Hooks()):
    tril = jnp.tril(jnp.ones((CHUNK, CHUNK), bool))
    wt = jnp.where(tril[None], w["ws"], 0.0).astype(BF16)
    wtt = jnp.swapaxes(wt, 1, 2)
    bs = w["bs"].reshape(4, CHUNK, 1)

    h_e = _rms_fwd("rms_fwd_even", x, w["even_norm"], after=hooks.before_even())
    z_e = _mm_nn("even_in_proj", h_e, w["w_in_e"], MM_ROWS, 1280, BF16)
    y_e = _even_mix_fwd(z_e, w["pool_w"], w["pool_scale"], wt, bs)
    x1 = _mm_nn("even_out_proj", y_e, w["w_out_e"], MM_ROWS, 1024, F32, resid=x)
    w = hooks.odd_weights(w, x1)
    h_o = _rms_fwd_orders("rms_fwd_odd", x1, w["odd_norm"])
    z_o = None
    for o in range(3):
        cols, n_cols = _cols_of_order(o)
        z_o = _mm_nn(f"odd_in_proj_o{o}", h_o[o], w["w_in_o"], MM_ROWS, D_MODEL, BF16, col_map=cols, n_cols=n_cols,
                     into=z_o)
    att = [_attn_fwd(z_o, gi) for gi in range(3)]
    y_o, ycr, ltot = _odd_mix_fwd(z_o, [a[0] for a in att], [a[1] for a in att], w["conv_w"])
    x2 = _mm_nn("odd_out_proj", y_o, w["w_out_o"], MM_ROWS, 1024, F32, resid=x1)
    dx2, loss8, g_final = _final_loss(x2, w["final_norm"], target)

    g_w_out_o = _mm_tn("odd_out_proj_dw", y_o, dx2, 1024, DW_OUT_TOKENS)
    dy_o = _mm_nt("odd_out_proj_dy", dx2, w["w_out_o"], MM_ROWS, 1024, BF16)
    dz_o, dyc, dst, g_conv = _odd_mix_bwd(dy_o, z_o, ycr, w["conv_w"])
    for gi in range(3):
        dz_o = _attn_bwd(z_o, dyc[gi], ltot[gi], dst[gi], dz_o, gi)
    g_w_in_o, dh_o = None, []
    for o in range(3):
        cols, n_cols = _cols_of_order(o)
        g_w_in_o = _mm_tn(f"odd_in_proj_dw_o{o}", h_o[o], dz_o, D_MODEL, DW_IN_TOKENS, col_map=cols, n_cols=n_cols,
                          into=g_w_in_o)
        dh_o.append(_mm_nt(f"odd_in_proj_dh_o{o}", dz_o, w["w_in_o"], MM_ROWS, D_MODEL, F32, k_map=cols, nk=n_cols))
    dx1, g_odd_norm = _rms_bwd("rms_bwd_odd", dh_o[0], x1, w["odd_norm"], dx2, dh4=dh_o[1], dh16=dh_o[2])
    after = hooks.odd_grads_ready(g_w_in_o, g_w_out_o)
    g_w_out_e = _mm_tn("even_out_proj_dw", y_e, dx1, 1024, DW_OUT_TOKENS, after=after)
    dy_e = _mm_nt("even_out_proj_dy", dx1, w["w_out_e"], MM_ROWS, 1024, BF16)
    dz_e, g_pw, g_ps, g_ws, g_bs = _even_mix_bwd(dy_e, z_e, w["pool_w"], w["pool_scale"], wt, wtt, bs)
    after = hooks.even_mix_done(dz_e)
    g_w_in_e = _mm_tn("even_in_proj_dw", h_e, dz_e, 1280, DW_IN_TOKENS, after=after)
    dh_e = _mm_nt("even_in_proj_dh", dz_e, w["w_in_e"], MM_ROWS, 1280, F32)
    dx0, g_even_norm = _rms_bwd("rms_bwd_even", dh_e, x, w["even_norm"], dx1)
    hooks.backward_done(dx0)

    grads = dict(w_in_e=g_w_in_e, pool_w=g_pw, w_out_e=g_w_out_e, w_in_o=g_w_in_o, w_out_o=g_w_out_o,
                 even_norm=g_even_norm, pool_scale=g_ps, ws=g_ws, bs=g_bs[:, :4].T, final_norm=g_final,
                 odd_norm=g_odd_norm, conv_w=g_conv)
    return loss8[0, 0], dx0, grads


class _Big(NamedTuple):
    name: str
    full: tuple
    haxis: int
    kaxis: int
    sub: int


BIGS = (
    _Big("w_in_e", (1024, 5120), 0, 1, 2),
    _Big("pool_w", (4, 256, 256), 0, 1, 1),
    _Big("w_out_e", (2048, 1024), 1, 0, 1),
    _Big("w_in_o", (1024, 14336), 0, 1, 4),
    _Big("w_out_o", (2048, 1024), 1, 0, 1),
)
N_BIG = len(BIGS)


def _shape(b, half=False, shard=False):
    return tuple(n // (2 if (half and ax == b.haxis) else 1) // (4 if (shard and ax == b.kaxis) else 1)
                 for ax, n in enumerate(b.full))


def _at(ref, b, h=None, k=None):
    idx = []
    for ax, n in enumerate(b.full):
        if ax == b.haxis and h is not None:
            idx.append(pl.ds(h * (n // 2), n // 2))
        elif ax == b.kaxis and k is not None:
            idx.append(pl.ds(k * (n // 4), n // 4))
        else:
            idx.append(slice(None))
    return ref.at[tuple(idx)]


def _place():
    x, y, c = lax.axis_index("x"), lax.axis_index("y"), lax.axis_index("c")
    chips = [(1 - x, y), (x, 1 - y), (1 - x, 1 - y)]
    return x, y, c, 2 * x + y, chips, [2 * cx + cy for cx, cy in chips]


def _piece_shape(b):
    return (4, 2) + _shape(b, half=True, shard=True)


def _gather_weights(bigs, shards, tiny):
    nb = len(bigs)

    def body(*refs):
        ins, tiny_in = refs[:nb], refs[nb]
        outs, tiny_out = refs[nb + 1:2 * nb + 1], refs[2 * nb + 1]
        send, recv, loc = refs[2 * nb + 2:]
        x, y, c, k_me, chips, ks = _place()
        sib = (x, y, 1 - c)

        def rc(src, dst, sem, to):
            return pltpu.make_async_remote_copy(src_ref=src, dst_ref=dst, send_sem=send.at[sem], recv_sem=recv.at[sem],
                                                device_id=to, device_id_type=MESH)

        started = []
        for a, b in enumerate(bigs):
            for h in range(2):
                own = pltpu.make_async_copy(_at(ins[a], b, h=h), outs[a].at[k_me, h], loc.at[2 * a + h])
                own.start()
                started.append(own)
        own = pltpu.make_async_copy(tiny_in, tiny_out.at[k_me], loc.at[2 * nb])
        own.start()
        started.append(own)
        sends = []
        for j, chip in enumerate(chips):
            for a, b in enumerate(bigs):
                sends.append(rc(_at(ins[a], b, h=c), outs[a].at[k_me, c], 6 * a + j, (*chip, c)))
            sends.append(rc(tiny_in, tiny_out.at[k_me], 6 * nb + j, (*chip, c)))
        for cp in sends:
            cp.start()
        for j in range(3):
            for a in range(nb):
                piece = outs[a].at[ks[j], c]
                rc(piece, piece, 6 * a + j, sib).wait_recv()
                fwd = rc(piece, piece, 6 * a + 3 + j, sib)
                fwd.start()
                sends.append(fwd)
            rc(tiny_in, tiny_out.at[ks[j]], 6 * nb + j, sib).wait_recv()
        for j in range(3):
            for a in range(nb):
                piece = outs[a].at[ks[j], 1 - c]
                rc(piece, piece, 6 * a + 3 + j, sib).wait_recv()
        for cp in sends:
            cp.wait_send()
        for cp in started:
            cp.wait()

    n_sem = 6 * nb + 3
    return pl.pallas_call(
        body, name="gather_even_weights",
        in_specs=[ANY] * (nb + 1), out_specs=[ANY] * (nb + 1),
        out_shape=[jax.ShapeDtypeStruct(_piece_shape(b), BF16) for b in bigs]
        + [jax.ShapeDtypeStruct((4,) + tiny.shape, F32)],
        scratch_shapes=[pltpu.SemaphoreType.DMA((n_sem,)), pltpu.SemaphoreType.DMA((n_sem,)),
                        pltpu.SemaphoreType.DMA((2 * nb + 1,))],
    )(*shards, tiny)


def _assemble(b, pieces):
    blk = _blk(b)

    def body(p_ref, o_ref):
        o_ref[...] = p_ref[...]

    return pl.pallas_call(
        body, name=f"assemble_{b.name}", grid=(4, 2, b.sub),
        in_specs=[pl.BlockSpec((None, None) + blk, lambda k, h, st: (k, h) + _bidx(b, 0, 0, st))],
        out_specs=pl.BlockSpec(blk, lambda k, h, st: _bidx(b, h, k, st)),
        out_shape=jax.ShapeDtypeStruct(b.full, BF16),
        compiler_params=_cp(("parallel", "parallel", "parallel"), 32),
    )(pieces)


def _copies_to_chips(bigs):
    def copies(srcs, lands, send, recv, waiting=False):
        _, _, c, k_me, chips, _ = _place()
        return [pltpu.make_async_remote_copy(
                    src_ref=_at(srcs[a], b, h=c), dst_ref=lands[a].at[k_me, c], send_sem=send.at[3 * a + j],
                    recv_sem=recv.at[3 * a + j], device_id=(*chips[j], c), device_id_type=MESH)
                for j in range(3) for a, b in enumerate(bigs)]
    return copies


def _copies_swap_halves(bigs):
    def copies(srcs, lands, send, recv, waiting=False):
        x, y, c, _, _, _ = _place()
        return [pltpu.make_async_remote_copy(
                    src_ref=_at(srcs[a], b, h=1 - c), dst_ref=lands[a], send_sem=send.at[a], recv_sem=recv.at[a],
                    device_id=(x, y, 1 - c), device_id_type=MESH)
                for a, b in enumerate(bigs)]
    return copies


def _copies_partials(bigs):
    def copies(srcs, lands, send, recv, waiting=False):
        _, _, c, _, chips, ks = _place()
        return [pltpu.make_async_remote_copy(
                    src_ref=_at(srcs[a], b, k=ks[j]), dst_ref=lands[a].at[j], send_sem=send.at[3 * a + j],
                    recv_sem=recv.at[3 * a + j], device_id=(*chips[j], c), device_id_type=MESH)
                for j in range(3) for a, b in enumerate(bigs)]
    return copies


def _exchange(name, srcs, land_shapes, copies_of, n_copies):
    ns = len(srcs)

    def body(*refs):
        copies = copies_of(refs[:ns], refs[ns:ns + len(land_shapes)], refs[-2], refs[-1])
        for cp in copies:
            cp.start()
        for cp in copies:
            cp.wait()

    return pl.pallas_call(
        body, name=name, in_specs=[ANY] * ns, out_specs=[ANY] * len(land_shapes), out_shape=land_shapes,
        scratch_shapes=[pltpu.SemaphoreType.DMA((n_copies,)), pltpu.SemaphoreType.DMA((n_copies,))],
    )(*srcs)


HBM = pl.BlockSpec(memory_space=pltpu.HBM)
SEM = pl.BlockSpec(memory_space=pltpu.SEMAPHORE)
SIDE_EFFECT = pltpu.SideEffectType.DATAFLOW_SIDE_EFFECTING


def _in_hbm(a):
    return pltpu.with_memory_space_constraint(a, pltpu.HBM)


def _exchange_start(name, srcs, land_shapes, copies_of, n_copies, after=None):
    ns, nl = len(srcs), len(land_shapes)
    lands = [lax.empty(sh.shape, sh.dtype) for sh in land_shapes]
    tail = [] if after is None else [after]
    n_in = ns + nl + len(tail)

    def body(*refs):
        send, recv, token = refs[n_in], refs[n_in + 1], refs[-1]
        for cp in copies_of(refs[:ns], refs[ns:ns + nl], send, recv):
            cp.start()
        token[...] = jnp.zeros_like(token)

    thru = [pltpu.HBM(a.shape, a.dtype) for a in (*srcs, *lands)]
    send, recv, *bufs, token = pl.pallas_call(
        body, name=name,
        out_shape=(pltpu.SemaphoreType.DMA((n_copies,)), pltpu.SemaphoreType.DMA((n_copies,)), *thru,
                   jax.ShapeDtypeStruct((8, 128), F32)),
        in_specs=[HBM] * (ns + nl) + [ANY] * len(tail),
        out_specs=(SEM, SEM, *([HBM] * (ns + nl)), pl.BlockSpec(memory_space=pltpu.VMEM)),
        input_output_aliases={i: 2 + i for i in range(ns + nl)},
        compiler_params=pltpu.CompilerParams(has_side_effects=SIDE_EFFECT),
    )(*[_in_hbm(a) for a in (*srcs, *lands)], *tail)
    return (send, recv, bufs, ns), token


def _exchange_wait(name, state, copies_of, after):
    send, recv, bufs, ns = state
    n = len(bufs)

    def body(*refs):
        ins = refs[:n]
        for cp in copies_of(ins[:ns], ins[ns:], refs[n], refs[n + 1], waiting=True):
            cp.wait_send()
            cp.wait_recv()

    out = pl.pallas_call(
        body, name=name, out_shape=tuple(pltpu.HBM(a.shape, a.dtype) for a in bufs),
        in_specs=[HBM] * n + [SEM, SEM, ANY], out_specs=tuple([HBM] * n),
        input_output_aliases={i: i for i in range(n)},
        compiler_params=pltpu.CompilerParams(has_side_effects=SIDE_EFFECT),
    )(*bufs, send, recv, after)
    return list(out[:ns]), list(out[ns:])


def _finish_gather(bigs, shards, pieces):
    nb = len(bigs)

    def body(*refs):
        ins, outs = refs[:nb], refs[2 * nb:3 * nb]
        send, recv, loc = refs[3 * nb:]
        x, y, c, k_me, _, ks = _place()
        own = [pltpu.make_async_copy(_at(ins[a], b, h=h), outs[a].at[k_me, h], loc.at[2 * a + h])
               for a, b in enumerate(bigs) for h in range(2)]
        for cp in own:
            cp.start()
        fwd = [pltpu.make_async_remote_copy(
                   src_ref=outs[a].at[ks[j], c], dst_ref=outs[a].at[ks[j], c], send_sem=send.at[3 * a + j],
                   recv_sem=recv.at[3 * a + j], device_id=(x, y, 1 - c), device_id_type=MESH)
               for j in range(3) for a in range(nb)]
        for cp in fwd:
            cp.start()
        for cp in fwd:
            cp.wait()
        for cp in own:
            cp.wait()

    return pl.pallas_call(
        body, name="gather_odd_finish", in_specs=[ANY] * (2 * nb), out_specs=[ANY] * nb,
        out_shape=[jax.ShapeDtypeStruct(_piece_shape(b), BF16) for b in bigs],
        scratch_shapes=[pltpu.SemaphoreType.DMA((3 * nb,)), pltpu.SemaphoreType.DMA((3 * nb,)),
                        pltpu.SemaphoreType.DMA((2 * nb,))],
        input_output_aliases={nb + a: a for a in range(nb)},
    )(*shards, *pieces)


def _blk(b):
    win = _shape(b, half=True, shard=True)
    return (win[0] // b.sub,) + win[1:]


def _bidx(b, h, k, st):
    idx = [0] * len(b.full)
    idx[b.haxis] = h
    idx[b.kaxis] = k
    idx[0] = idx[0] * b.sub + st
    return tuple(idx)


def _chip_sum(b, g, got, c_arr):
    blk = _blk(b)

    def body(c_ref, g_ref, r_ref, o_ref):
        del c_ref
        o_ref[...] = (g_ref[...] + r_ref[...]).astype(BF16)

    half = pl.BlockSpec(blk, lambda k, st, c_ref: _bidx(b, 0, k, st))
    return pl.pallas_call(
        body, name=f"rs_chip_sum_{b.name}",
        grid_spec=pltpu.PrefetchScalarGridSpec(
            num_scalar_prefetch=1, grid=(4, b.sub),
            in_specs=[pl.BlockSpec(blk, lambda k, st, c_ref: _bidx(b, c_ref[0], k, st)), half], out_specs=half),
        out_shape=jax.ShapeDtypeStruct(_shape(b, half=True), BF16),
        compiler_params=_cp(("arbitrary", "arbitrary"), 40),
    )(c_arr, g, got)


def _half_shapes(bigs):
    return [jax.ShapeDtypeStruct(_shape(b, half=True), F32) for b in bigs]


def _partial_shapes(bigs):
    return [jax.ShapeDtypeStruct((3,) + _shape(b, half=True, shard=True), BF16) for b in bigs]


def _shard_sum(b, mine, got, ck_arr):
    blk = _blk(b)

    def body(ck_ref, m_ref, r0, r1, r2, o_ref):
        del ck_ref
        o_ref[...] = (m_ref[...].astype(F32) + r0[...].astype(F32)) + (r1[...].astype(F32) + r2[...].astype(F32))

    def peer(j):
        return pl.BlockSpec((None,) + blk, lambda st, ck: (j,) + _bidx(b, 0, 0, st))

    return pl.pallas_call(
        body, name=f"rs_shard_sum_{b.name}",
        grid_spec=pltpu.PrefetchScalarGridSpec(
            num_scalar_prefetch=1, grid=(b.sub,),
            in_specs=[pl.BlockSpec(blk, lambda st, ck: _bidx(b, 0, ck[1], st)), peer(0), peer(1), peer(2)],
            out_specs=pl.BlockSpec(blk, lambda st, ck: _bidx(b, ck[0], 0, st))),
        out_shape=jax.ShapeDtypeStruct(_shape(b, shard=True), F32),
        compiler_params=_cp(("arbitrary",), 40),
    )(ck_arr, mine, got, got, got)


def _share_halves(gs):
    def body(*refs):
        ins, outs, send, recv = refs[:N_BIG], refs[N_BIG:2 * N_BIG], refs[2 * N_BIG], refs[2 * N_BIG + 1]
        del ins
        x, y, c, _, _, _ = _place()
        copies = [pltpu.make_async_remote_copy(src_ref=_at(outs[a], b, h=c), dst_ref=_at(outs[a], b, h=c),
                                               send_sem=send.at[a], recv_sem=recv.at[a], device_id=(x, y, 1 - c),
                                               device_id_type=MESH)
                  for a, b in enumerate(BIGS)]
        for cp in copies:
            cp.start()
        for cp in copies:
            cp.wait()

    return pl.pallas_call(
        body, name="rs_share_halves", in_specs=[ANY] * N_BIG, out_specs=[ANY] * N_BIG,
        out_shape=[jax.ShapeDtypeStruct(_shape(b, shard=True), F32) for b in BIGS],
        scratch_shapes=[pltpu.SemaphoreType.DMA((N_BIG,)), pltpu.SemaphoreType.DMA((N_BIG,))],
        input_output_aliases={a: a for a in range(N_BIG)},
    )(*gs)


def _gather_small(block):
    m_per, n = block.shape

    def body(x_ref, out_ref, send_sems, recv_sems, local_sem):
        x, y, c = lax.axis_index("x"), lax.axis_index("y"), lax.axis_index("c")
        me, sibling = (x, y, c), (x, y, 1 - c)
        chips = [(1 - x, y), (x, 1 - y), (1 - x, 1 - y)]

        def rows(px, py, pc):
            return out_ref.at[pl.ds((4 * px + 2 * py + pc) * m_per, m_per), :]

        def copy(k, blk, to, src=None):
            return pltpu.make_async_remote_copy(
                src_ref=rows(*blk) if src is None else src, dst_ref=rows(*blk), send_sem=send_sems.at[k],
                recv_sem=recv_sems.at[k], device_id=to, device_id_type=MESH)

        mine = pltpu.make_async_copy(x_ref, rows(*me), local_sem)
        mine.start()
        first = [copy(0, me, sibling, src=x_ref)]
        first += [copy(1 + j, me, (*chip, c), src=x_ref) for j, chip in enumerate(chips)]
        for cp in first:
            cp.start()
        passed = [copy(4 + j, (*chip, c), sibling) for j, chip in enumerate(chips)]
        for j, chip in enumerate(chips):
            copy(1 + j, (*chip, c), me).wait_recv()
            passed[j].start()
        copy(0, sibling, me).wait_recv()
        for j, chip in enumerate(chips):
            copy(4 + j, (*chip, 1 - c), me).wait_recv()
        for cp in first + passed:
            cp.wait_send()
        mine.wait()

    return pl.pallas_call(
        body, name="gather_small_grads",
        out_shape=jax.ShapeDtypeStruct((8 * m_per, n), block.dtype),
        in_specs=[pl.BlockSpec(memory_space=pltpu.VMEM)], out_specs=pl.BlockSpec(memory_space=pltpu.VMEM),
        scratch_shapes=[pltpu.SemaphoreType.DMA((7,)), pltpu.SemaphoreType.DMA((7,)), pltpu.SemaphoreType.DMA],
    )(block)


def _sum_small(stack):
    _, m_per, n = stack.shape

    def body(x_ref, o_ref):
        acc = x_ref[0]
        for dev in range(1, 8):
            acc = acc + x_ref[dev]
        o_ref[...] = acc

    return pl.pallas_call(body, name="sum_small_grads", out_shape=jax.ShapeDtypeStruct((m_per, n), F32))(stack)


def _adamw(name, w, g, m, v, rows):
    shape = w.shape

    def body(w_ref, g_ref, m_ref, v_ref, d_ref, mo_ref, vo_ref):
        gg = g_ref[...]
        mn = ADAM_B1 * m_ref[...] + (1.0 - ADAM_B1) * gg
        vn = ADAM_B2 * v_ref[...] + (1.0 - ADAM_B2) * (gg * gg)
        m_hat = mn / (1.0 - ADAM_B1 ** ADAM_STEP)
        v_hat = vn / (1.0 - ADAM_B2 ** ADAM_STEP)
        d_ref[...] = -ADAM_LR * (m_hat / (jnp.sqrt(v_hat) + ADAM_EPS) + ADAM_WD * w_ref[...])
        mo_ref[...] = mn
        vo_ref[...] = vn

    spec = pl.BlockSpec((rows,) + shape[1:], lambda i: (i,) + (0,) * (len(shape) - 1))
    return pl.pallas_call(
        body, name=name, grid=(shape[0] // rows,), in_specs=[spec] * 4, out_specs=[spec] * 3,
        out_shape=[jax.ShapeDtypeStruct(shape, F32)] * 3, compiler_params=_cp(("parallel",), 48),
    )(w, g, m, v)


ADAM_ROWS = dict(w_in_e=256, pool_w=4, w_out_e=256, w_in_o=128, w_out_o=256)


def _pack(parts, rows):
    flat = jnp.concatenate([p.reshape(-1).astype(F32) for p in parts])
    return jnp.pad(flat, (0, rows * 128 - flat.shape[0])).reshape(rows, 128)


def _unpack(buf, shapes):
    flat = buf.reshape(-1)
    out, off = [], 0
    for shp in shapes:
        n = 1
        for dim in shp:
            n *= dim
        out.append(flat[off:off + n].reshape(shp))
        off += n
    return out


WEIGHTS = ("even_norm", "even_w_in", "even_pool_w", "even_pool_scale", "even_ws", "even_bs", "even_w_out", "odd_norm",
           "odd_w_in", "odd_conv_w", "odd_w_out", "final_norm")
BIG_OF = dict(w_in_e="even_w_in", pool_w="even_pool_w", w_out_e="even_w_out", w_in_o="odd_w_in", w_out_o="odd_w_out")
SMALL = ("even_norm", "even_pool_scale", "even_ws", "even_bs", "final_norm", "odd_norm", "odd_conv_w")
SMALL_GRAD_ROWS = 576
SMALL_STATE_ROWS = 552


def kernel(x, even_norm, even_w_in, even_pool_w, even_pool_scale, even_ws, even_bs, even_w_out, odd_norm, odd_w_in, odd_conv_w, odd_w_out, final_norm, loss_target, m_even_norm, m_even_w_in, m_even_pool_w, m_even_pool_scale, m_even_ws, m_even_bs, m_even_w_out, m_odd_norm, m_odd_w_in, m_odd_conv_w, m_odd_w_out, m_final_norm, v_even_norm, v_even_w_in, v_even_pool_w, v_even_pool_scale, v_even_ws, v_even_bs, v_even_w_out, v_odd_norm, v_odd_w_in, v_odd_conv_w, v_odd_w_out, v_final_norm):
    wv = dict(zip(WEIGHTS, (even_norm, even_w_in, even_pool_w, even_pool_scale, even_ws, even_bs, even_w_out, odd_norm,
                            odd_w_in, odd_conv_w, odd_w_out, final_norm)))
    mv = dict(zip(WEIGHTS, (m_even_norm, m_even_w_in, m_even_pool_w, m_even_pool_scale, m_even_ws, m_even_bs,
                            m_even_w_out, m_odd_norm, m_odd_w_in, m_odd_conv_w, m_odd_w_out, m_final_norm)))
    vv = dict(zip(WEIGHTS, (v_even_norm, v_even_w_in, v_even_pool_w, v_even_pool_scale, v_even_ws, v_even_bs,
                            v_even_w_out, v_odd_norm, v_odd_w_in, v_odd_conv_w, v_odd_w_out, v_final_norm)))
    c = lax.axis_index("c")
    k_me = 2 * lax.axis_index("x") + lax.axis_index("y")

    c_arr = jnp.reshape(c, (1,)).astype(jnp.int32)
    ck_arr = jnp.stack([c, k_me]).astype(jnp.int32)
    even_bigs, odd_bigs = BIGS[:3], BIGS[3:]

    shards = {b.name: wv[BIG_OF[b.name]][0].astype(BF16) for b in BIGS}
    tiny = jnp.concatenate([odd_conv_w[0], odd_norm], axis=0)
    *pieces_even, tiny_all = _gather_weights(even_bigs, [shards[b.name] for b in even_bigs], tiny)
    tiny_full = jnp.transpose(tiny_all, (1, 0, 2)).reshape(4, D_MODEL)
    to_chips, swap_odd, partials_odd = _copies_to_chips(odd_bigs), _copies_swap_halves(odd_bigs), _copies_partials(odd_bigs)
    gather_state, gather_token = _exchange_start(
        "gather_odd_start", [shards[b.name] for b in odd_bigs],
        [jax.ShapeDtypeStruct(_piece_shape(b), BF16) for b in odd_bigs], to_chips, 3 * len(odd_bigs), after=pieces_even[-1])
    w = {b.name: _assemble(b, p) for b, p in zip(even_bigs, pieces_even)}
    w.update(even_norm=even_norm, pool_scale=even_pool_scale, ws=even_ws[0], bs=even_bs[0],
             final_norm=final_norm.reshape(1, D_MODEL), conv_w=tiny_full[:3], odd_norm=tiny_full[3:4])

    class Hooks(_Hooks):
        def before_even(self):
            return gather_token

        def odd_weights(self, w, x1):
            srcs, lands = _exchange_wait("gather_odd_wait", gather_state, to_chips, after=x1)
            pieces = _finish_gather(odd_bigs, srcs, lands)
            return dict(w, **{b.name: _assemble(b, p) for b, p in zip(odd_bigs, pieces)})

        def odd_grads_ready(self, g_w_in_o, g_w_out_o):
            self.swap, token = _exchange_start("rs_odd_swap_start", [g_w_in_o, g_w_out_o], _half_shapes(odd_bigs),
                                               swap_odd, len(odd_bigs))
            return token

        def even_mix_done(self, dz_e):
            grads, got = _exchange_wait("rs_odd_swap_wait", self.swap, swap_odd, after=dz_e)
            sums = [_chip_sum(b, g, r, c_arr) for b, g, r in zip(odd_bigs, grads, got)]
            self.partials, token = _exchange_start("rs_odd_partials_start", sums, _partial_shapes(odd_bigs), partials_odd,
                                                   3 * len(odd_bigs))
            return token

        def backward_done(self, dx0):
            self.sums, self.parts = _exchange_wait("rs_odd_partials_wait", self.partials, partials_odd, after=dx0)

    hooks = Hooks()
    loss, dx, g = _local_step(x[0], loss_target[0], w, hooks)
    loss = lax.psum(loss, ("x", "y", "c"))

    got = _exchange("rs_even_swap", [g[b.name] for b in even_bigs], _half_shapes(even_bigs), _copies_swap_halves(even_bigs),
                    len(even_bigs))
    sums = [_chip_sum(b, g[b.name], r, c_arr) for b, r in zip(even_bigs, got)]
    parts = _exchange("rs_even_partials", sums, _partial_shapes(even_bigs), _copies_partials(even_bigs), 3 * len(even_bigs))
    halves = [_shard_sum(b, sm, p, ck_arr) for b, sm, p in zip(BIGS, sums + hooks.sums, list(parts) + hooks.parts)]
    g_shard = dict(zip((b.name for b in BIGS), _share_halves(halves)))

    small_g = _pack([g["even_norm"], g["pool_scale"], g["ws"], g["bs"], g["final_norm"], g["odd_norm"], g["conv_w"]],
                    SMALL_GRAD_ROWS)
    small_g = _sum_small(_gather_small(small_g).reshape(8, SMALL_GRAD_ROWS, 128))
    g_en, g_ps, g_ws, g_bs, g_fn, g_on, g_cw = _unpack(
        small_g, [(1, D_MODEL), (1, D_MODEL), (1, 4, CHUNK, CHUNK), (1, 4, CHUNK), (D_MODEL,), (1, D_MODEL), (1, 3, D_MODEL)])
    g_on = lax.dynamic_slice(g_on, (0, k_me * 256), (1, 256))
    g_cw = lax.dynamic_slice(g_cw, (0, 0, k_me * 256), (1, 3, 256))
    grad = dict(even_norm=g_en, even_pool_scale=g_ps, even_ws=g_ws, even_bs=g_bs, final_norm=g_fn, odd_norm=g_on,
                odd_conv_w=g_cw)
    for b in BIGS:
        grad[BIG_OF[b.name]] = g_shard[b.name][None]

    delta, new_m, new_v = {}, {}, {}
    for b in BIGS:
        n = BIG_OF[b.name]
        d_, m_, v_ = _adamw(f"adamw_{b.name}", wv[n][0], g_shard[b.name], mv[n][0], vv[n][0], ADAM_ROWS[b.name])
        delta[n], new_m[n], new_v[n] = d_[None], m_[None], v_[None]
    shapes = [wv[n].shape for n in SMALL]
    packed = [_pack([src[n] for n in SMALL], SMALL_STATE_ROWS) for src in (wv, grad, mv, vv)]
    outs = _adamw("adamw_small", *packed, SMALL_STATE_ROWS)
    for dst, buf in zip((delta, new_m, new_v), outs):
        for n, arr in zip(SMALL, _unpack(buf, shapes)):
            dst[n] = arr

    return (loss, dx[None], *[grad[n] for n in WEIGHTS], *[delta[n] for n in WEIGHTS], *[new_m[n] for n in WEIGHTS],
            *[new_v[n] for n in WEIGHTS])
```

```python
from typing import NamedTuple

import jax
import jax.numpy as jnp
from jax import lax
from jax.experimental import pallas as pl
from jax.experimental.pallas import tpu as pltpu

F32, BF16 = jnp.float32, jnp.bfloat16

D_MODEL = 1024
EPS = 1e-6
NEG = -1e30
POOL_SIZES = (2, 4, 8, 16)
GROUP_W = 256
CHUNK = 128
DILATIONS = (1, 4, 16)
N_SLOTS = 8
HEAD_DIM = 128
ATTN_BLOCK = 128
SCALE = HEAD_DIM ** -0.5
EVEN_IN = 5120
ODD_IN = 14336
QKV_BLOCKS = 9
ODD_BLOCKS = ODD_IN // D_MODEL
SLOPES = tuple(tuple(2.0 ** (-8.0 * (g * N_SLOTS + s + 1) / (3 * N_SLOTS)) for s in range(N_SLOTS)) for g in range(3))

ADAM_LR, ADAM_B1, ADAM_B2, ADAM_EPS, ADAM_WD, ADAM_STEP = 0.001, 0.9, 0.999, 1e-08, 0.01, 10

HALO = 16
TS = 512
TM = 256
MM_ROWS = 1024
DW_IN_TOKENS = 2048
DW_OUT_TOKENS = 1024
MIB = 1 << 20
MESH = pl.DeviceIdType.MESH
ANY = pl.BlockSpec(memory_space=pl.ANY)


def _cp(sem, vmem_mib):
    return pltpu.CompilerParams(dimension_semantics=sem, vmem_limit_bytes=vmem_mib * MIB)


def _sig(x):
    return 1.0 / (1.0 + jnp.exp(-x))


def _win_sum(e, w, forward):
    n = e.shape[0]
    k = 1
    while k < w:
        e = e + pltpu.roll(e, (n - k) if forward else k, 0)
        k *= 2
    return e


def _mm_nn(name, a, b, tm, tn, out_dtype, resid=None, col_map=None, n_cols=None, into=None):
    m, k = a.shape
    n = b.shape[1]
    if col_map is None:
        col_map, n_cols = (lambda j: j), n // tn

    def body(*refs):
        a_ref, b_ref = refs[0], refs[1]
        acc = jnp.dot(a_ref[...].astype(BF16), b_ref[...], preferred_element_type=F32)
        if resid is not None:
            acc = acc + refs[2][...]
        o_ref = refs[-1]
        o_ref[...] = acc.astype(out_dtype)

    in_specs = [pl.BlockSpec((tm, k), lambda j, i: (i, 0)), pl.BlockSpec((k, tn), lambda j, i: (0, col_map(j)))]
    args = [a, b]
    if resid is not None:
        in_specs.append(pl.BlockSpec((tm, tn), lambda j, i: (i, col_map(j))))
        args.append(resid)
    aliases = {}
    if into is not None:
        aliases = {len(args): 0}
        in_specs.append(ANY)
        args.append(into)
    return pl.pallas_call(
        body, name=name, grid=(n_cols, m // tm), in_specs=in_specs,
        out_specs=pl.BlockSpec((tm, tn), lambda j, i: (i, col_map(j))),
        out_shape=jax.ShapeDtypeStruct((m, n), out_dtype), input_output_aliases=aliases,
        compiler_params=_cp(("parallel", "parallel"), 48),
    )(*args)


def _mm_nt(name, a, b, tm, tk, out_dtype, k_map=None, nk=None):
    m, k = a.shape
    n = b.shape[0]
    if k_map is None:
        k_map, nk = (lambda kk: kk), k // tk

    def body(a_ref, b_ref, o_ref, acc_ref):
        kk = pl.program_id(1)
        p = lax.dot_general(a_ref[...].astype(BF16), b_ref[...], (((1,), (1,)), ((), ())), preferred_element_type=F32)
        if nk == 1:
            o_ref[...] = p.astype(out_dtype)
        else:
            @pl.when(kk == 0)
            def _():
                acc_ref[...] = p

            @pl.when(kk > 0)
            def _():
                acc_ref[...] += p

            @pl.when(kk == nk - 1)
            def _():
                o_ref[...] = acc_ref[...].astype(out_dtype)

    return pl.pallas_call(
        body, name=name, grid=(m // tm, nk),
        in_specs=[pl.BlockSpec((tm, tk), lambda i, kk: (i, k_map(kk))), pl.BlockSpec((n, tk), lambda i, kk: (0, k_map(kk)))],
        out_specs=pl.BlockSpec((tm, n), lambda i, kk: (i, 0)),
        out_shape=jax.ShapeDtypeStruct((m, n), out_dtype),
        scratch_shapes=[pltpu.VMEM((tm, n) if nk > 1 else (8, 128), F32)],
        compiler_params=_cp(("parallel", "arbitrary"), 56),
    )(a, b)


def _mm_tn(name, a, g, tn, ts, col_map=None, n_cols=None, into=None, after=None):
    s, ka = a.shape
    n = g.shape[1]
    if col_map is None:
        col_map, n_cols = (lambda j: j), n // tn

    def body(a_ref, g_ref, *rest):
        o_ref = rest[-1]
        st = pl.program_id(1)
        p = lax.dot_general(a_ref[...], g_ref[...].astype(BF16), (((0,), (0,)), ((), ())), preferred_element_type=F32)

        @pl.when(st == 0)
        def _():
            o_ref[...] = p

        @pl.when(st > 0)
        def _():
            o_ref[...] += p

    in_specs = [pl.BlockSpec((ts, ka), lambda j, st: (st, 0)), pl.BlockSpec((ts, tn), lambda j, st: (st, col_map(j)))]
    args = [a, g]
    aliases = {}
    if into is not None:
        aliases = {2: 0}
        in_specs.append(ANY)
        args.append(into)
    if after is not None:
        in_specs.append(ANY)
        args.append(after)
    return pl.pallas_call(
        body, name=name, grid=(n_cols, s // ts), in_specs=in_specs,
        out_specs=pl.BlockSpec((ka, tn), lambda j, st: (0, col_map(j))),
        out_shape=jax.ShapeDtypeStruct((ka, n), F32), input_output_aliases=aliases,
        compiler_params=_cp(("parallel", "arbitrary"), 56),
    )(*args)


def _rms_fwd(name, x, g, after=None):
    s = x.shape[0]

    def body(x_ref, g_ref, *rest):
        xf = x_ref[...]
        r = lax.rsqrt(jnp.mean(xf * xf, axis=-1, keepdims=True) + EPS)
        rest[-1][...] = (xf * r * g_ref[...]).astype(BF16)

    row = pl.BlockSpec((TS, D_MODEL), lambda i: (i, 0))
    in_specs, args = [row, pl.BlockSpec((1, D_MODEL), lambda i: (0, 0))], [x, g]
    if after is not None:
        in_specs.append(ANY)
        args.append(after)
    return pl.pallas_call(
        body, name=name, grid=(s // TS,), in_specs=in_specs, out_specs=row,
        out_shape=jax.ShapeDtypeStruct((s, D_MODEL), BF16), compiler_params=_cp(("parallel",), 32),
    )(*args)


def _class_major(a, d):
    return a.reshape(d, a.shape[0] // d, a.shape[1])


def _class_spec(d, tile, width):
    return pl.BlockSpec((d, tile // d, width), lambda i: (0, i, 0))


LANES = 128


def _token_scratch(tile, width):
    return pltpu.VMEM((width // LANES, tile, LANES), F32)


def _put(scr, val):
    for c in range(scr.shape[0]):
        scr[c] = val[:, c * LANES:(c + 1) * LANES]


def _get(scr):
    return jnp.concatenate([scr[c] for c in range(scr.shape[0])], axis=1)


def _to_classes(ref3, scr, d, dtype):
    n = ref3.shape[1]
    for c in range(scr.shape[0]):
        for r in range(d):
            ref3[r, :, c * LANES:(c + 1) * LANES] = scr.at[c][pl.ds(r, n, stride=d), :].astype(dtype)


def _from_classes(scr, ref3, d):
    n = ref3.shape[1]
    for c in range(scr.shape[0]):
        for r in range(d):
            scr.at[c][pl.ds(r, n, stride=d), :] = ref3[r, :, c * LANES:(c + 1) * LANES].astype(F32)


def _rms_fwd_orders(name, x, g):
    s = x.shape[0]

    def body(x_ref, g_ref, h_ref, h4_ref, h16_ref, scr):
        xf = x_ref[...]
        r = lax.rsqrt(jnp.mean(xf * xf, axis=-1, keepdims=True) + EPS)
        h = xf * r * g_ref[...]
        h_ref[...] = h.astype(BF16)
        _put(scr, h)
        _to_classes(h4_ref, scr, 4, BF16)
        _to_classes(h16_ref, scr, 16, BF16)

    row = pl.BlockSpec((TS, D_MODEL), lambda i: (i, 0))
    h, h4, h16 = pl.pallas_call(
        body, name=name, grid=(s // TS,), in_specs=[row, pl.BlockSpec((1, D_MODEL), lambda i: (0, 0))],
        out_specs=[row, _class_spec(4, TS, D_MODEL), _class_spec(16, TS, D_MODEL)],
        out_shape=[jax.ShapeDtypeStruct((s, D_MODEL), BF16), jax.ShapeDtypeStruct((4, s // 4, D_MODEL), BF16),
                   jax.ShapeDtypeStruct((16, s // 16, D_MODEL), BF16)],
        scratch_shapes=[_token_scratch(TS, D_MODEL)],
        compiler_params=_cp(("parallel",), 32),
    )(x, g)
    return h, h4.reshape(s, D_MODEL), h16.reshape(s, D_MODEL)


def _rms_bwd(name, dh, x, g, dres, dh4=None, dh16=None):
    s = x.shape[0]
    extra = dh4 is not None

    def body(dh_ref, x_ref, g_ref, dres_ref, *rest):
        if extra:
            dh4_ref, dh16_ref, dx_ref, dg_ref, scr = rest
        else:
            dx_ref, dg_ref = rest
        xf = x_ref[...]
        r = lax.rsqrt(jnp.mean(xf * xf, axis=-1, keepdims=True) + EPS)
        xh = xf * r
        dhf = dh_ref[...]
        if extra:
            _from_classes(scr, dh4_ref, 4)
            dhf = dhf + _get(scr)
            _from_classes(scr, dh16_ref, 16)
            dhf = dhf + _get(scr)
        dxh = dhf * g_ref[...]
        dx_ref[...] = dres_ref[...] + r * (dxh - xh * jnp.mean(dxh * xh, axis=-1, keepdims=True))
        part = jnp.sum(dhf * xh, axis=0, keepdims=True)

        @pl.when(pl.program_id(0) == 0)
        def _():
            dg_ref[...] = part

        @pl.when(pl.program_id(0) > 0)
        def _():
            dg_ref[...] += part

    row = pl.BlockSpec((TS, D_MODEL), lambda i: (i, 0))
    vec = pl.BlockSpec((1, D_MODEL), lambda i: (0, 0))
    in_specs, args, scratch = [row, row, vec, row], [dh, x, g, dres], []
    if extra:
        in_specs += [_class_spec(4, TS, D_MODEL), _class_spec(16, TS, D_MODEL)]
        args += [_class_major(dh4, 4), _class_major(dh16, 16)]
        scratch = [_token_scratch(TS, D_MODEL)]
    return pl.pallas_call(
        body, name=name, grid=(s // TS,), in_specs=in_specs, out_specs=[row, vec],
        out_shape=[jax.ShapeDtypeStruct((s, D_MODEL), F32), jax.ShapeDtypeStruct((1, D_MODEL), F32)],
        scratch_shapes=scratch, compiler_params=_cp(("arbitrary",), 40),
    )(*args)


def _final_loss(x, g, target):
    s = x.shape[0]

    def body(x_ref, g_ref, t_ref, dx_ref, loss_ref, dg_ref):
        xf = x_ref[...]
        gg = g_ref[...]
        r = lax.rsqrt(jnp.mean(xf * xf, axis=-1, keepdims=True) + EPS)
        xh = xf * r
        e = xh * gg - t_ref[...]
        dy = e * (1.0 / D_MODEL)
        dxh = dy * gg
        dx_ref[...] = r * (dxh - xh * jnp.mean(dxh * xh, axis=-1, keepdims=True))
        lpart = 0.5 * jnp.sum(jnp.mean(e * e, axis=-1, keepdims=True), axis=0, keepdims=True)
        lpart = jnp.broadcast_to(lpart, (8, 128))
        gpart = jnp.sum(dy * xh, axis=0, keepdims=True)

        @pl.when(pl.program_id(0) == 0)
        def _():
            loss_ref[...] = lpart
            dg_ref[...] = gpart

        @pl.when(pl.program_id(0) > 0)
        def _():
            loss_ref[...] += lpart
            dg_ref[...] += gpart

    row = pl.BlockSpec((TS, D_MODEL), lambda i: (i, 0))
    vec = pl.BlockSpec((1, D_MODEL), lambda i: (0, 0))
    return pl.pallas_call(
        body, name="final_loss", grid=(s // TS,), in_specs=[row, vec, row],
        out_specs=[row, pl.BlockSpec((8, 128), lambda i: (0, 0)), vec],
        out_shape=[jax.ShapeDtypeStruct((s, D_MODEL), F32), jax.ShapeDtypeStruct((8, 128), F32),
                   jax.ShapeDtypeStruct((1, D_MODEL), F32)],
        compiler_params=_cp(("arbitrary",), 40),
    )(x, g, target)


def _zcol(c, tm=TM):
    return pl.BlockSpec((tm, D_MODEL), lambda i, c=c: (i, c))


def _prev_halo(c, tm=TM):
    return pl.BlockSpec((HALO, D_MODEL), lambda i, c=c: (jnp.maximum(i * (tm // HALO) - 1, 0), c))


def _next_halo(c, n_rows, tm=TM):
    last = n_rows // HALO - 1
    return pl.BlockSpec((HALO, D_MODEL), lambda i, c=c: (jnp.minimum((i + 1) * (tm // HALO), last), c))


def _full(shape):
    return pl.BlockSpec(shape, lambda i: (0,) * len(shape))


def _inv_count(first_row, n, w):
    t = first_row + lax.broadcasted_iota(jnp.int32, (n, 1), 0)
    return 1.0 / jnp.minimum(t + 1, w).astype(F32)


def _even_mix_fwd(z, pw, ps, wt, bs):
    s = z.shape[0]

    def body(a_ref, ga_ref, u_ref, v_ref, gb_ref, halo_ref, pw_ref, ps_ref, wt_ref, bs_ref, y_ref):
        i = pl.program_id(0)
        a = a_ref[...].astype(F32)
        halo = jnp.where(i > 0, halo_ref[...].astype(F32), 0.0)
        ext = jnp.concatenate([halo, a], axis=0)
        ga = ga_ref[...].astype(F32)
        sga = ga * _sig(ga)
        for g, w in enumerate(POOL_SIZES):
            cs = slice(g * GROUP_W, (g + 1) * GROUP_W)
            win = _win_sum(ext[:, cs], w, False)[HALO:]
            pooled = win * _inv_count(i * TM, TM, w) - a[:, cs]
            mixed = jnp.dot(pooled.astype(BF16), pw_ref[g], preferred_element_type=F32)
            y_ref[:, cs] = (mixed * ps_ref[:, cs] * sga[:, cs]).astype(BF16)
        gb = gb_ref[...].astype(F32)
        gate = u_ref[...].astype(F32) * (gb * _sig(gb))
        for ch in range(TM // CHUNK):
            rs = slice(ch * CHUNK, (ch + 1) * CHUNK)
            for g in range(4):
                cs = slice(g * GROUP_W, (g + 1) * GROUP_W)
                mixb = jnp.dot(wt_ref[g], v_ref[rs, cs], preferred_element_type=F32) + bs_ref[g]
                y_ref[rs, D_MODEL + g * GROUP_W:D_MODEL + (g + 1) * GROUP_W] = (gate[rs, cs] * mixb).astype(BF16)

    return pl.pallas_call(
        body, name="even_mix_fwd", grid=(s // TM,),
        in_specs=[_zcol(0), _zcol(1), _zcol(2), _zcol(3), _zcol(4), _prev_halo(0),
                  _full((4, GROUP_W, GROUP_W)), _full((1, D_MODEL)), _full((4, CHUNK, CHUNK)), _full((4, CHUNK, 1))],
        out_specs=pl.BlockSpec((TM, 2 * D_MODEL), lambda i: (i, 0)),
        out_shape=jax.ShapeDtypeStruct((s, 2 * D_MODEL), BF16),
        compiler_params=_cp(("parallel",), 48),
    )(z, z, z, z, z, z, pw, ps, wt, bs)


def _even_mix_bwd(dy, z, pw, ps, wt, wtt, bs, after=None):
    s = z.shape[0]
    n_tiles = s // TM
    tail_specs, tail_args = ([ANY], [after]) if after is not None else ([], [])

    def body(dy_ref, a_ref, ga_ref, u_ref, v_ref, gb_ref, halo_ref, dyn_ref, gan_ref, pw_ref, ps_ref, wt_ref, wtt_ref,
             bs_ref, *rest):
        dz_ref, dpw_ref, dps_ref, dws_ref, dbs_ref = rest[-5:]
        i = pl.program_id(0)

        @pl.when(i == 0)
        def _():
            dpw_ref[...] = jnp.zeros_like(dpw_ref)
            dps_ref[...] = jnp.zeros_like(dps_ref)
            dws_ref[...] = jnp.zeros_like(dws_ref)
            dbs_ref[...] = jnp.zeros_like(dbs_ref)

        a = a_ref[...].astype(F32)
        halo = jnp.where(i > 0, halo_ref[...].astype(F32), 0.0)
        ext = jnp.concatenate([halo, a], axis=0)
        ga = ga_ref[...].astype(F32)
        sg = _sig(ga)
        sga = ga * sg
        dsga = sg * (1.0 + ga * (1.0 - sg))
        dya = dy_ref[:, :D_MODEL].astype(F32)
        gan = gan_ref[...].astype(F32)
        dmn_all = jnp.where(i < n_tiles - 1, dyn_ref[...].astype(F32) * ps_ref[...] * (gan * _sig(gan)), 0.0)
        for g, w in enumerate(POOL_SIZES):
            cs = slice(g * GROUP_W, (g + 1) * GROUP_W)
            inv = _inv_count(i * TM, TM, w)
            pooled = _win_sum(ext[:, cs], w, False)[HALO:] * inv - a[:, cs]
            pb = pooled.astype(BF16)
            mixed = jnp.dot(pb, pw_ref[g], preferred_element_type=F32)
            dyg = dya[:, cs]
            psg = ps_ref[:, cs]
            dm = (dyg * psg * sga[:, cs]).astype(BF16)
            dz_ref[:, D_MODEL + g * GROUP_W:D_MODEL + (g + 1) * GROUP_W] = (dyg * mixed * psg * dsga[:, cs]).astype(BF16)
            dps_ref[:, cs] += jnp.sum(dyg * mixed * sga[:, cs], axis=0, keepdims=True)
            dpw_ref[g] += lax.dot_general(pb, dm, (((0,), (0,)), ((), ())), preferred_element_type=F32)
            nt = (((1,), (1,)), ((), ()))
            dpool = lax.dot_general(dm, pw_ref[g], nt, preferred_element_type=F32)
            dpool_n = lax.dot_general(dmn_all[:, cs].astype(BF16), pw_ref[g], nt, preferred_element_type=F32)
            e = jnp.concatenate([dpool * inv, dpool_n * _inv_count((i + 1) * TM, HALO, w)], axis=0)
            dz_ref[:, cs] = (_win_sum(e, w, True)[:TM] - dpool).astype(BF16)

        gb = gb_ref[...].astype(F32)
        sg = _sig(gb)
        sgb = gb * sg
        dsgb = sg * (1.0 + gb * (1.0 - sg))
        u = u_ref[...].astype(F32)
        dyb = dy_ref[:, D_MODEL:].astype(F32)
        tril = lax.broadcasted_iota(jnp.int32, (CHUNK, CHUNK), 0) >= lax.broadcasted_iota(jnp.int32, (CHUNK, CHUNK), 1)
        lane = lax.broadcasted_iota(jnp.int32, (CHUNK, 128), 1)
        for ch in range(TM // CHUNK):
            rs = slice(ch * CHUNK, (ch + 1) * CHUNK)
            for g in range(4):
                cs = slice(g * GROUP_W, (g + 1) * GROUP_W)
                vb = v_ref[rs, cs]
                mixb = jnp.dot(wt_ref[g], vb, preferred_element_type=F32) + bs_ref[g]
                dyu = dyb[rs, cs] * u[rs, cs]
                dmix = dyu * sgb[rs, cs]
                dmb = dmix.astype(BF16)
                o = g * GROUP_W
                dz_ref[rs, 2 * D_MODEL + o:2 * D_MODEL + o + GROUP_W] = (dyb[rs, cs] * mixb * sgb[rs, cs]).astype(BF16)
                dz_ref[rs, 3 * D_MODEL + o:3 * D_MODEL + o + GROUP_W] = jnp.dot(
                    wtt_ref[g], dmb, preferred_element_type=F32).astype(BF16)
                dz_ref[rs, 4 * D_MODEL + o:4 * D_MODEL + o + GROUP_W] = (dyu * mixb * dsgb[rs, cs]).astype(BF16)
                dws = lax.dot_general(dmb, vb, (((1,), (1,)), ((), ())), preferred_element_type=F32)
                dws_ref[g] += jnp.where(tril, dws, 0.0)
                dbs_ref[...] += jnp.where(lane == g, jnp.sum(dmix, axis=1, keepdims=True), 0.0)

    return pl.pallas_call(
        body, name="even_mix_bwd", grid=(n_tiles,),
        in_specs=[pl.BlockSpec((TM, 2 * D_MODEL), lambda i: (i, 0)), _zcol(0), _zcol(1), _zcol(2), _zcol(3), _zcol(4),
                  _prev_halo(0), _next_halo(0, s), _next_halo(1, s),
                  _full((4, GROUP_W, GROUP_W)), _full((1, D_MODEL)), _full((4, CHUNK, CHUNK)), _full((4, CHUNK, CHUNK)),
                  _full((4, CHUNK, 1))] + tail_specs,
        out_specs=[pl.BlockSpec((TM, EVEN_IN), lambda i: (i, 0)), _full((4, GROUP_W, GROUP_W)), _full((1, D_MODEL)),
                   _full((4, CHUNK, CHUNK)), _full((CHUNK, 128))],
        out_shape=[jax.ShapeDtypeStruct((s, EVEN_IN), BF16), jax.ShapeDtypeStruct((4, GROUP_W, GROUP_W), F32),
                   jax.ShapeDtypeStruct((1, D_MODEL), F32), jax.ShapeDtypeStruct((4, CHUNK, CHUNK), F32),
                   jax.ShapeDtypeStruct((CHUNK, 128), F32)],
        compiler_params=_cp(("arbitrary",), 56),
    )(dy, z, z, z, z, z, z, dy, z, pw, ps, wt, wtt, bs, *tail_args)


STAT_W = 128
Q_BLOCKS = 2
Q_ROWS = Q_BLOCKS * ATTN_BLOCK


def _band(d):
    row = lax.broadcasted_iota(jnp.int32, (ATTN_BLOCK, 2 * ATTN_BLOCK), 0)
    col = lax.broadcasted_iota(jnp.int32, (ATTN_BLOCK, 2 * ATTN_BLOCK), 1)
    steps = row + ATTN_BLOCK - col
    return (steps >= 0) & (steps <= ATTN_BLOCK), col >= ATTN_BLOCK, -(steps * d).astype(F32)


def _attn_fwd(z, gi):
    s = z.shape[0]
    d = DILATIONS[gi]
    nb = s // d // ATTN_BLOCK
    nq = nb // Q_BLOCKS

    def spec(which, prev=False):
        cb = which * 3 + gi
        if prev:
            return pl.BlockSpec((ATTN_BLOCK, D_MODEL), lambda r, i: (r * nb + jnp.maximum(Q_BLOCKS * i - 1, 0), cb))
        return pl.BlockSpec((Q_ROWS, D_MODEL), lambda r, i: (r * nq + i, cb))

    def body(q_ref, kp_ref, kc_ref, vp_ref, vc_ref, o_ref, lse_ref):
        i = pl.program_id(1)
        inner, own, negdist = _band(d)
        lane = lax.broadcasted_iota(jnp.int32, (ATTN_BLOCK, STAT_W), 1)
        for b in range(Q_BLOCKS):
            rows = slice(b * ATTN_BLOCK, (b + 1) * ATTN_BLOCK)
            valid = (inner & ((i > 0) | own)) if b == 0 else inner
            stat = jnp.zeros((ATTN_BLOCK, STAT_W), F32)
            for h in range(N_SLOTS):
                sl = slice(h * HEAD_DIM, (h + 1) * HEAD_DIM)
                if b == 0:
                    k = jnp.concatenate([kp_ref[:, sl], kc_ref[:ATTN_BLOCK, sl]], axis=0)
                    v = jnp.concatenate([vp_ref[:, sl], vc_ref[:ATTN_BLOCK, sl]], axis=0)
                else:
                    k = kc_ref[(b - 1) * ATTN_BLOCK:(b + 1) * ATTN_BLOCK, sl]
                    v = vc_ref[(b - 1) * ATTN_BLOCK:(b + 1) * ATTN_BLOCK, sl]
                sc = lax.dot_general(q_ref[rows, sl], k, (((1,), (1,)), ((), ())), preferred_element_type=F32) * SCALE
                sc = jnp.where(valid, sc + SLOPES[gi][h] * negdist, NEG)
                m = jnp.max(sc, axis=-1, keepdims=True)
                p = jnp.exp(sc - m)
                l = jnp.sum(p, axis=-1, keepdims=True)
                o = jnp.dot((p / l).astype(BF16), v, preferred_element_type=F32)
                o_ref[rows, sl] = o.astype(BF16)
                stat = jnp.where(lane == h, m + jnp.log(l), stat)
            lse_ref[rows, :] = stat

    return pl.pallas_call(
        body, name=f"attn_fwd_d{d}", grid=(d, nq),
        in_specs=[spec(0), spec(1, True), spec(1), spec(2, True), spec(2)],
        out_specs=[pl.BlockSpec((Q_ROWS, D_MODEL), lambda r, i: (r * nq + i, 0)),
                   pl.BlockSpec((Q_ROWS, STAT_W), lambda r, i: (r * nq + i, 0))],
        out_shape=[jax.ShapeDtypeStruct((s, D_MODEL), BF16), jax.ShapeDtypeStruct((s, STAT_W), F32)],
        compiler_params=_cp(("parallel", "parallel"), 32),
    )(z, z, z, z, z)


def _attn_bwd(z, dyc, ltot, dst, dz, gi):
    s = z.shape[0]
    d = DILATIONS[gi]
    nb = s // d // ATTN_BLOCK
    nq = nb // Q_BLOCKS
    n_steps = d * nq

    def rev(cb, width=D_MODEL, prev=False):
        if prev:
            return pl.BlockSpec((ATTN_BLOCK, width), lambda r, n: (r * nb + jnp.maximum(Q_BLOCKS * (nq - 1 - n) - 1, 0), cb))
        return pl.BlockSpec((Q_ROWS, width), lambda r, n: (r * nq + nq - 1 - n, cb))

    def body(q_ref, kp_ref, kc_ref, vp_ref, vc_ref, dy_ref, l_ref, d_ref, dz_in, dz_out, dq_s, dk_s, dv_s, ck_s, cv_s, sems):
        del dz_in
        r = pl.program_id(0)
        n = pl.program_id(1)
        i = nq - 1 - n
        step = r * nq + n

        def out_copy(src, which):
            rows = pl.ds(pl.multiple_of((r * nq + i) * Q_ROWS, Q_ROWS), Q_ROWS)
            return pltpu.make_async_copy(src, dz_out.at[rows, pl.ds((which * 3 + gi) * D_MODEL, D_MODEL)], sems.at[which])

        copies = [out_copy(dq_s, 0), out_copy(dk_s, 1), out_copy(dv_s, 2)]

        @pl.when(step > 0)
        def _():
            for cp in copies:
                cp.wait()

        @pl.when(n == 0)
        def _():
            ck_s[...] = jnp.zeros_like(ck_s)
            cv_s[...] = jnp.zeros_like(cv_s)

        row = lax.broadcasted_iota(jnp.int32, (Q_ROWS, Q_ROWS + ATTN_BLOCK), 0)
        col = lax.broadcasted_iota(jnp.int32, (Q_ROWS, Q_ROWS + ATTN_BLOCK), 1)
        steps = row + ATTN_BLOCK - col
        valid = (steps >= 0) & (steps <= ATTN_BLOCK) & ((i > 0) | (col >= ATTN_BLOCK))
        negdist = -(steps * d).astype(F32)
        nt = (((1,), (1,)), ((), ()))
        tn = (((0,), (0,)), ((), ()))
        for h in range(N_SLOTS):
            sl = slice(h * HEAD_DIM, (h + 1) * HEAD_DIM)
            q = q_ref[:, sl]
            k = jnp.concatenate([kp_ref[:, sl], kc_ref[:, sl]], axis=0)
            v = jnp.concatenate([vp_ref[:, sl], vc_ref[:, sl]], axis=0)
            dy = dy_ref[:, sl]
            sc = lax.dot_general(q, k, nt, preferred_element_type=F32) * SCALE + SLOPES[gi][h] * negdist
            p = jnp.where(valid, jnp.exp(sc - l_ref[:, h:h + 1]), 0.0)
            dp = lax.dot_general(dy, v, nt, preferred_element_type=F32)
            ds = (p * (dp - d_ref[:, h:h + 1])).astype(BF16)
            dq_s[:, sl] = (jnp.dot(ds, k, preferred_element_type=F32) * SCALE).astype(BF16)
            dk = lax.dot_general(ds, q, tn, preferred_element_type=F32) * SCALE
            dv = lax.dot_general(p.astype(BF16), dy, tn, preferred_element_type=F32)
            dk_s[:Q_ROWS - ATTN_BLOCK, sl] = dk[ATTN_BLOCK:Q_ROWS].astype(BF16)
            dv_s[:Q_ROWS - ATTN_BLOCK, sl] = dv[ATTN_BLOCK:Q_ROWS].astype(BF16)
            dk_s[Q_ROWS - ATTN_BLOCK:, sl] = (ck_s[:, sl] + dk[Q_ROWS:]).astype(BF16)
            dv_s[Q_ROWS - ATTN_BLOCK:, sl] = (cv_s[:, sl] + dv[Q_ROWS:]).astype(BF16)
            ck_s[:, sl] = dk[:ATTN_BLOCK]
            cv_s[:, sl] = dv[:ATTN_BLOCK]

        for cp in copies:
            cp.start()

        @pl.when(step == n_steps - 1)
        def _():
            for cp in copies:
                cp.wait()

    stage = pltpu.VMEM((Q_ROWS, D_MODEL), BF16)
    carry = pltpu.VMEM((ATTN_BLOCK, D_MODEL), F32)
    return pl.pallas_call(
        body, name=f"attn_bwd_d{d}", grid=(d, nq),
        in_specs=[rev(gi), rev(3 + gi, prev=True), rev(3 + gi), rev(6 + gi, prev=True), rev(6 + gi),
                  rev(0), rev(0, STAT_W), rev(0, STAT_W), ANY],
        out_specs=ANY,
        out_shape=jax.ShapeDtypeStruct((s, ODD_IN), BF16),
        scratch_shapes=[stage, stage, stage, carry, carry, pltpu.SemaphoreType.DMA((3,))],
        input_output_aliases={8: 0},
        compiler_params=_cp(("arbitrary", "arbitrary"), 32),
    )(z, z, z, z, z, dyc, ltot, dst, dz)


def _odd_mix_fwd(z, os_, lses, cw):
    s = z.shape[0]

    def body(o0, o1, o2, l0, l1, l2, gc_ref, db_ref, dc_ref, dx_ref, gd_ref, hc_ref, hx_ref, cw_ref, y_ref, yc_ref, lt_ref,
             lt4_ref, lt16_ref, scr_o, scr_l):
        i = pl.program_id(0)
        _from_classes(scr_l, l1, 4)
        lse1 = _get(scr_l)
        _from_classes(scr_l, l2, 16)
        ls = [l0[...], lse1, _get(scr_l)]
        lmax = jnp.maximum(jnp.maximum(ls[0], ls[1]), ls[2])
        es = [jnp.exp(l - lmax) for l in ls]
        den = es[0] + es[1] + es[2]
        alpha = [e / den for e in es]
        ltot = lmax + jnp.log(den)
        lt_ref[...] = ltot
        _put(scr_l, ltot)
        _to_classes(lt4_ref, scr_l, 4, F32)
        _to_classes(lt16_ref, scr_l, 16, F32)
        _from_classes(scr_o, o1, 4)
        og1 = _get(scr_o)
        _from_classes(scr_o, o2, 16)
        og = [o0[...].astype(F32), og1, _get(scr_o)]
        gc = gc_ref[...].astype(F32)
        gate = gc * _sig(gc)
        for h in range(N_SLOTS):
            sl = slice(h * HEAD_DIM, (h + 1) * HEAD_DIM)
            yc = (alpha[0][:, h:h + 1] * og[0][:, sl] + alpha[1][:, h:h + 1] * og[1][:, sl]
                  + alpha[2][:, h:h + 1] * og[2][:, sl])
            yc_ref[:, sl] = yc.astype(BF16)
            y_ref[:, sl] = (yc * gate[:, sl]).astype(BF16)

        zc = dc_ref[...].astype(F32) * dx_ref[...].astype(F32)
        halo = jnp.where(i > 0, hc_ref[...].astype(F32) * hx_ref[...].astype(F32), 0.0)
        ext = jnp.concatenate([halo, zc], axis=0)
        z1 = pltpu.roll(ext, 1, 0)[HALO:]
        z2 = pltpu.roll(ext, 2, 0)[HALO:]
        conv = cw_ref[0:1, :] * z2 + cw_ref[1:2, :] * z1 + cw_ref[2:3, :] * zc
        gd = gd_ref[...].astype(F32)
        y_ref[:, D_MODEL:] = (db_ref[...].astype(F32) * conv * (gd * _sig(gd))).astype(BF16)

    row = pl.BlockSpec((TM, D_MODEL), lambda i: (i, 0))
    stat = pl.BlockSpec((TM, STAT_W), lambda i: (i, 0))
    y, ycr, lt, lt4, lt16 = pl.pallas_call(
        body, name="odd_mix_fwd", grid=(s // TM,),
        in_specs=[row, _class_spec(4, TM, D_MODEL), _class_spec(16, TM, D_MODEL),
                  stat, _class_spec(4, TM, STAT_W), _class_spec(16, TM, STAT_W),
                  _zcol(9), _zcol(10), _zcol(11), _zcol(12), _zcol(13), _prev_halo(11), _prev_halo(12), _full((3, D_MODEL))],
        out_specs=[pl.BlockSpec((TM, 2 * D_MODEL), lambda i: (i, 0)), row, stat, _class_spec(4, TM, STAT_W),
                   _class_spec(16, TM, STAT_W)],
        out_shape=[jax.ShapeDtypeStruct((s, 2 * D_MODEL), BF16), jax.ShapeDtypeStruct((s, D_MODEL), BF16),
                   jax.ShapeDtypeStruct((s, STAT_W), F32), jax.ShapeDtypeStruct((4, s // 4, STAT_W), F32),
                   jax.ShapeDtypeStruct((16, s // 16, STAT_W), F32)],
        scratch_shapes=[_token_scratch(TM, D_MODEL), _token_scratch(TM, STAT_W)],
        compiler_params=_cp(("parallel",), 48),
    )(os_[0], _class_major(os_[1], 4), _class_major(os_[2], 16), lses[0], _class_major(lses[1], 4),
      _class_major(lses[2], 16), z, z, z, z, z, z, z, cw)
    return y, ycr, [lt, lt4.reshape(s, STAT_W), lt16.reshape(s, STAT_W)]


def _odd_mix_bwd(dy, z, ycr, cw):
    s = z.shape[0]
    n_tiles = s // TM
    rest = ODD_IN - QKV_BLOCKS * D_MODEL

    def body(dy_ref, yc_ref, gc_ref, db_ref, dc_ref, dx_ref, gd_ref, hc_ref, hx_ref, dyn_ref, dbn_ref, gdn_ref, cw_ref,
             dz_ref, dyc_ref, dyc4_ref, dyc16_ref, dd_ref, dd4_ref, dd16_ref, dcw_ref, stage, sem, scr_o, scr_l):
        i = pl.program_id(0)
        out = pltpu.make_async_copy(
            stage, dz_ref.at[pl.ds(pl.multiple_of(i * TM, TM), TM), pl.ds(QKV_BLOCKS * D_MODEL, rest)], sem)

        @pl.when(i > 0)
        def _():
            out.wait()

        dyc_in = dy_ref[:, :D_MODEL].astype(F32)
        gc = gc_ref[...].astype(F32)
        sg = _sig(gc)
        yc = yc_ref[...].astype(F32)
        dyc = dyc_in * (gc * sg)
        dyc_ref[...] = dyc.astype(BF16)
        _put(scr_o, dyc)
        _to_classes(dyc4_ref, scr_o, 4, BF16)
        _to_classes(dyc16_ref, scr_o, 16, BF16)
        stage[:, 0:D_MODEL] = (dyc_in * yc * (sg * (1.0 + gc * (1.0 - sg)))).astype(BF16)
        prod = dyc * yc
        lane = lax.broadcasted_iota(jnp.int32, (TM, STAT_W), 1)
        stat = jnp.zeros((TM, STAT_W), F32)
        for h in range(N_SLOTS):
            part = jnp.sum(prod[:, h * HEAD_DIM:(h + 1) * HEAD_DIM], axis=-1, keepdims=True)
            stat = jnp.where(lane == h, part, stat)
        dd_ref[...] = stat
        _put(scr_l, stat)
        _to_classes(dd4_ref, scr_l, 4, F32)
        _to_classes(dd16_ref, scr_l, 16, F32)

        dc = dc_ref[...].astype(F32)
        dx = dx_ref[...].astype(F32)
        zc = dc * dx
        halo = jnp.where(i > 0, hc_ref[...].astype(F32) * hx_ref[...].astype(F32), 0.0)
        ext = jnp.concatenate([halo, zc], axis=0)
        z1 = pltpu.roll(ext, 1, 0)[HALO:]
        z2 = pltpu.roll(ext, 2, 0)[HALO:]
        w0, w1, w2 = cw_ref[0:1, :], cw_ref[1:2, :], cw_ref[2:3, :]
        conv = w0 * z2 + w1 * z1 + w2 * zc
        gd = gd_ref[...].astype(F32)
        sg = _sig(gd)
        sgd = gd * sg
        db = db_ref[...].astype(F32)
        dyd = dy_ref[:, D_MODEL:].astype(F32)
        dconv = dyd * db * sgd
        gdn = gdn_ref[...].astype(F32)
        dconv_n = jnp.where(i < n_tiles - 1, dyn_ref[...].astype(F32) * dbn_ref[...].astype(F32) * (gdn * _sig(gdn)), 0.0)
        extn = jnp.concatenate([dconv, dconv_n], axis=0)
        nrow = TM + HALO
        dzc = w2 * dconv + w1 * pltpu.roll(extn, nrow - 1, 0)[:TM] + w0 * pltpu.roll(extn, nrow - 2, 0)[:TM]
        stage[:, D_MODEL:2 * D_MODEL] = (dyd * conv * sgd).astype(BF16)
        stage[:, 2 * D_MODEL:3 * D_MODEL] = (dzc * dx).astype(BF16)
        stage[:, 3 * D_MODEL:4 * D_MODEL] = (dzc * dc).astype(BF16)
        stage[:, 4 * D_MODEL:5 * D_MODEL] = (dyd * db * conv * (sg * (1.0 + gd * (1.0 - sg)))).astype(BF16)
        @pl.when(i == 0)
        def _():
            dcw_ref[...] = jnp.zeros_like(dcw_ref)

        for tap, shifted in enumerate((z2, z1, zc)):
            dcw_ref[tap:tap + 1, :] += jnp.sum(dconv * shifted, axis=0, keepdims=True)

        out.start()

        @pl.when(i == n_tiles - 1)
        def _():
            out.wait()

    row = pl.BlockSpec((TM, D_MODEL), lambda i: (i, 0))
    stat = pl.BlockSpec((TM, STAT_W), lambda i: (i, 0))
    dz, dyc, dyc4, dyc16, dd, dd4, dd16, g_conv = pl.pallas_call(
        body, name="odd_mix_bwd", grid=(n_tiles,),
        in_specs=[pl.BlockSpec((TM, 2 * D_MODEL), lambda i: (i, 0)), row, _zcol(9), _zcol(10), _zcol(11), _zcol(12), _zcol(13),
                  _prev_halo(11), _prev_halo(12), _next_halo(1, s), _next_halo(10, s), _next_halo(13, s), _full((3, D_MODEL))],
        out_specs=[ANY, row, _class_spec(4, TM, D_MODEL), _class_spec(16, TM, D_MODEL),
                   stat, _class_spec(4, TM, STAT_W), _class_spec(16, TM, STAT_W), _full((3, D_MODEL))],
        out_shape=[jax.ShapeDtypeStruct((s, ODD_IN), BF16), jax.ShapeDtypeStruct((s, D_MODEL), BF16),
                   jax.ShapeDtypeStruct((4, s // 4, D_MODEL), BF16), jax.ShapeDtypeStruct((16, s // 16, D_MODEL), BF16),
                   jax.ShapeDtypeStruct((s, STAT_W), F32), jax.ShapeDtypeStruct((4, s // 4, STAT_W), F32),
                   jax.ShapeDtypeStruct((16, s // 16, STAT_W), F32), jax.ShapeDtypeStruct((3, D_MODEL), F32)],
        scratch_shapes=[pltpu.VMEM((TM, rest), BF16), pltpu.SemaphoreType.DMA(()), _token_scratch(TM, D_MODEL),
                        _token_scratch(TM, STAT_W)],
        compiler_params=_cp(("arbitrary",), 48),
    )(dy, ycr, z, z, z, z, z, z, z, dy, z, z, cw)
    dyc = [dyc, dyc4.reshape(s, D_MODEL), dyc16.reshape(s, D_MODEL)]
    dd = [dd, dd4.reshape(s, STAT_W), dd16.reshape(s, STAT_W)]
    return dz, dyc, dd, g_conv


def _cols_of_order(order):
    if order == 0:
        return (lambda j: jnp.where(j < 3, 3 * j, j + 6)), 8
    return (lambda j: 3 * j + order), 3


class _Hooks:
    def before_even(self):
        return None

    def odd_weights(self, w, x1):
        return w

    def odd_grads_ready(self, g_w_in_o, g_w_out_o):
        return None

    def even_mix_done(self, dz_e):
        return None

    def backward_done(self, dx0):
        return None


def _local_step(x, target, w, hooks=_Hooks()):
    tril = jnp.tril(jnp.ones((CHUNK, CHUNK), bool))
    wt = jnp.where(tril[None], w["ws"], 0.0).astype(BF16)
    wtt = jnp.swapaxes(wt, 1, 2)
    bs = w["bs"].reshape(4, CHUNK, 1)

    h_e = _rms_fwd("rms_fwd_even", x, w["even_norm"], after=hooks.before_even())
    z_e = _mm_nn("even_in_proj", h_e, w["w_in_e"], MM_ROWS, 1280, BF16)
    y_e = _even_mix_fwd(z_e, w["pool_w"], w["pool_scale"], wt, bs)
    x1 = _mm_nn("even_out_proj", y_e, w["w_out_e"], MM_ROWS, 1024, F32, resid=x)
    w = hooks.odd_weights(w, x1)
    h_o = _rms_fwd_orders("rms_fwd_odd", x1, w["odd_norm"])
    z_o = None
    for o in range(3):
        cols, n_cols = _cols_of_order(o)
        z_o = _mm_nn(f"odd_in_proj_o{o}", h_o[o], w["w_in_o"], MM_ROWS, D_MODEL, BF16, col_map=cols, n_cols=n_cols,
                     into=z_o)
    att = [_attn_fwd(z_o, gi) for gi in range(3)]
    y_o, ycr, ltot = _odd_mix_fwd(z_o, [a[0] for a in att], [a[1] for a in att], w["conv_w"])
    x2 = _mm_nn("odd_out_proj", y_o, w["w_out_o"], MM_ROWS, 1024, F32, resid=x1)
    dx2, loss8, g_final = _final_loss(x2, w["final_norm"], target)

    g_w_out_o = _mm_tn("odd_out_proj_dw", y_o, dx2, 1024, DW_OUT_TOKENS)
    dy_o = _mm_nt("odd_out_proj_dy", dx2, w["w_out_o"], MM_ROWS, 1024, BF16)
    dz_o, dyc, dst, g_conv = _odd_mix_bwd(dy_o, z_o, ycr, w["conv_w"])
    for gi in range(3):
        dz_o = _attn_bwd(z_o, dyc[gi], ltot[gi], dst[gi], dz_o, gi)
    g_w_in_o, dh_o = None, []
    for o in range(3):
        cols, n_cols = _cols_of_order(o)
        g_w_in_o = _mm_tn(f"odd_in_proj_dw_o{o}", h_o[o], dz_o, D_MODEL, DW_IN_TOKENS, col_map=cols, n_cols=n_cols,
                          into=g_w_in_o)
        dh_o.append(_mm_nt(f"odd_in_proj_dh_o{o}", dz_o, w["w_in_o"], MM_ROWS, D_MODEL, F32, k_map=cols, nk=n_cols))
    dx1, g_odd_norm = _rms_bwd("rms_bwd_odd", dh_o[0], x1, w["odd_norm"], dx2, dh4=dh_o[1], dh16=dh_o[2])
    after = hooks.odd_grads_ready(g_w_in_o, g_w_out_o)
    g_w_out_e = _mm_tn("even_out_proj_dw", y_e, dx1, 1024, DW_OUT_TOKENS, after=after)
    dy_e = _mm_nt("even_out_proj_dy", dx1, w["w_out_e"], MM_ROWS, 1024, BF16)
    dz_e, g_pw, g_ps, g_ws, g_bs = _even_mix_bwd(dy_e, z_e, w["pool_w"], w["pool_scale"], wt, wtt, bs)
    after = hooks.even_mix_done(dz_e)
    g_w_in_e = _mm_tn("even_in_proj_dw", h_e, dz_e, 1280, DW_IN_TOKENS, after=after)
    dh_e = _mm_nt("even_in_proj_dh", dz_e, w["w_in_e"], MM_ROWS, 1280, F32)
    dx0, g_even_norm = _rms_bwd("rms_bwd_even", dh_e, x, w["even_norm"], dx1)
    hooks.backward_done(dx0)

    grads = dict(w_in_e=g_w_in_e, pool_w=g_pw, w_out_e=g_w_out_e, w_in_o=g_w_in_o, w_out_o=g_w_out_o,
                 even_norm=g_even_norm, pool_scale=g_ps, ws=g_ws, bs=g_bs[:, :4].T, final_norm=g_final,
                 odd_norm=g_odd_norm, conv_w=g_conv)
    return loss8[0, 0], dx0, grads


class _Big(NamedTuple):
    name: str
    full: tuple
    haxis: int
    kaxis: int
    sub: int


BIGS = (
    _Big("w_in_e", (1024, 5120), 0, 1, 2),
    _Big("pool_w", (4, 256, 256), 0, 1, 1),
    _Big("w_out_e", (2048, 1024), 1, 0, 1),
    _Big("w_in_o", (1024, 14336), 0, 1, 4),
    _Big("w_out_o", (2048, 1024), 1, 0, 1),
)
N_BIG = len(BIGS)


def _shape(b, half=False, shard=False):
    return tuple(n // (2 if (half and ax == b.haxis) else 1) // (4 if (shard and ax == b.kaxis) else 1)
                 for ax, n in enumerate(b.full))


def _at(ref, b, h=None, k=None):
    idx = []
    for ax, n in enumerate(b.full):
        if ax == b.haxis and h is not None:
            idx.append(pl.ds(h * (n // 2), n // 2))
        elif ax == b.kaxis and k is not None:
            idx.append(pl.ds(k * (n // 4), n // 4))
        else:
            idx.append(slice(None))
    return ref.at[tuple(idx)]


def _place():
    x, y, c = lax.axis_index("x"), lax.axis_index("y"), lax.axis_index("c")
    chips = [(1 - x, y), (x, 1 - y), (1 - x, 1 - y)]
    return x, y, c, 2 * x + y, chips, [2 * cx + cy for cx, cy in chips]


def _piece_shape(b):
    return (4, 2) + _shape(b, half=True, shard=True)


def _gather_weights(bigs, shards, tiny):
    nb = len(bigs)

    def body(*refs):
        ins, tiny_in = refs[:nb], refs[nb]
        outs, tiny_out = refs[nb + 1:2 * nb + 1], refs[2 * nb + 1]
        send, recv, loc = refs[2 * nb + 2:]
        x, y, c, k_me, chips, ks = _place()
        sib = (x, y, 1 - c)

        def rc(src, dst, sem, to):
            return pltpu.make_async_remote_copy(src_ref=src, dst_ref=dst, send_sem=send.at[sem], recv_sem=recv.at[sem],
                                                device_id=to, device_id_type=MESH)

        own = pltpu.make_async_copy(tiny_in, tiny_out.at[k_me], loc)
        own.start()
        sends = []
        for j, chip in enumerate(chips):
            for a, b in enumerate(bigs):
                sends.append(rc(_at(ins[a], b, h=c), outs[a].at[k_me, c], 6 * a + j, (*chip, c)))
            sends.append(rc(tiny_in, tiny_out.at[k_me], 6 * nb + j, (*chip, c)))
        for cp in sends:
            cp.start()
        for j in range(3):
            for a in range(nb):
                piece = outs[a].at[ks[j], c]
                rc(piece, piece, 6 * a + j, sib).wait_recv()
                fwd = rc(piece, piece, 6 * a + 3 + j, sib)
                fwd.start()
                sends.append(fwd)
            rc(tiny_in, tiny_out.at[ks[j]], 6 * nb + j, sib).wait_recv()
        for j in range(3):
            for a in range(nb):
                piece = outs[a].at[ks[j], 1 - c]
                rc(piece, piece, 6 * a + 3 + j, sib).wait_recv()
        for cp in sends:
            cp.wait_send()
        own.wait()

    n_sem = 6 * nb + 3
    return pl.pallas_call(
        body, name="gather_even_weights",
        in_specs=[ANY] * (nb + 1), out_specs=[ANY] * (nb + 1),
        out_shape=[jax.ShapeDtypeStruct(_piece_shape(b), BF16) for b in bigs]
        + [jax.ShapeDtypeStruct((4,) + tiny.shape, F32)],
        scratch_shapes=[pltpu.SemaphoreType.DMA((n_sem,)), pltpu.SemaphoreType.DMA((n_sem,)), pltpu.SemaphoreType.DMA(())],
    )(*shards, tiny)


def _assemble(b, pieces, shard, k_arr):
    blk = _blk(b)

    def body(k_ref, p_ref, s_ref, o_ref):
        mine = pl.program_id(0) == k_ref[0]

        @pl.when(mine)
        def _():
            o_ref[...] = s_ref[...]

        @pl.when(jnp.logical_not(mine))
        def _():
            o_ref[...] = p_ref[...]

    return pl.pallas_call(
        body, name=f"assemble_{b.name}",
        grid_spec=pltpu.PrefetchScalarGridSpec(
            num_scalar_prefetch=1, grid=(4, 2, b.sub),
            in_specs=[pl.BlockSpec((None, None) + blk, lambda k, h, st, k_ref: (k, h) + _bidx(b, 0, 0, st)),
                      pl.BlockSpec(blk, lambda k, h, st, k_ref: _bidx(b, h, 0, st))],
            out_specs=pl.BlockSpec(blk, lambda k, h, st, k_ref: _bidx(b, h, k, st))),
        out_shape=jax.ShapeDtypeStruct(b.full, BF16),
        compiler_params=_cp(("arbitrary", "arbitrary", "arbitrary"), 32),
    )(k_arr, pieces, shard)


def _copies_to_chips(bigs):
    def copies(srcs, lands, send, recv, waiting=False):
        _, _, c, k_me, chips, _ = _place()
        return [pltpu.make_async_remote_copy(
                    src_ref=_at(srcs[a], b, h=c), dst_ref=lands[a].at[k_me, c], send_sem=send.at[3 * a + j],
                    recv_sem=recv.at[3 * a + j], device_id=(*chips[j], c), device_id_type=MESH)
                for j in range(3) for a, b in enumerate(bigs)]
    return copies


def _copies_swap_halves(bigs):
    def copies(srcs, lands, send, recv, waiting=False):
        x, y, c, _, _, _ = _place()
        return [pltpu.make_async_remote_copy(
                    src_ref=_at(srcs[a], b, h=1 - c), dst_ref=lands[a], send_sem=send.at[a], recv_sem=recv.at[a],
                    device_id=(x, y, 1 - c), device_id_type=MESH)
                for a, b in enumerate(bigs)]
    return copies


def _copies_partials(bigs):
    def copies(srcs, lands, send, recv, waiting=False):
        _, _, c, _, chips, ks = _place()
        return [pltpu.make_async_remote_copy(
                    src_ref=_at(srcs[a], b, k=ks[j]), dst_ref=lands[a].at[j], send_sem=send.at[3 * a + j],
                    recv_sem=recv.at[3 * a + j], device_id=(*chips[j], c), device_id_type=MESH)
                for j in range(3) for a, b in enumerate(bigs)]
    return copies


def _exchange(name, srcs, land_shapes, copies_of, n_copies):
    ns = len(srcs)

    def body(*refs):
        copies = copies_of(refs[:ns], refs[ns:ns + len(land_shapes)], refs[-2], refs[-1])
        for cp in copies:
            cp.start()
        for cp in copies:
            cp.wait()

    return pl.pallas_call(
        body, name=name, in_specs=[ANY] * ns, out_specs=[ANY] * len(land_shapes), out_shape=land_shapes,
        scratch_shapes=[pltpu.SemaphoreType.DMA((n_copies,)), pltpu.SemaphoreType.DMA((n_copies,))],
    )(*srcs)


HBM = pl.BlockSpec(memory_space=pltpu.HBM)
SEM = pl.BlockSpec(memory_space=pltpu.SEMAPHORE)
SIDE_EFFECT = pltpu.SideEffectType.DATAFLOW_SIDE_EFFECTING


def _in_hbm(a):
    return pltpu.with_memory_space_constraint(a, pltpu.HBM)


def _exchange_start(name, srcs, land_shapes, copies_of, n_copies, after=None):
    ns, nl = len(srcs), len(land_shapes)
    lands = [lax.empty(sh.shape, sh.dtype) for sh in land_shapes]
    tail = [] if after is None else [after]
    n_in = ns + nl + len(tail)

    def body(*refs):
        send, recv, token = refs[n_in], refs[n_in + 1], refs[-1]
        for cp in copies_of(refs[:ns], refs[ns:ns + nl], send, recv):
            cp.start()
        token[...] = jnp.zeros_like(token)

    thru = [pltpu.HBM(a.shape, a.dtype) for a in (*srcs, *lands)]
    send, recv, *bufs, token = pl.pallas_call(
        body, name=name,
        out_shape=(pltpu.SemaphoreType.DMA((n_copies,)), pltpu.SemaphoreType.DMA((n_copies,)), *thru,
                   jax.ShapeDtypeStruct((8, 128), F32)),
        in_specs=[HBM] * (ns + nl) + [ANY] * len(tail),
        out_specs=(SEM, SEM, *([HBM] * (ns + nl)), pl.BlockSpec(memory_space=pltpu.VMEM)),
        input_output_aliases={i: 2 + i for i in range(ns + nl)},
        compiler_params=pltpu.CompilerParams(has_side_effects=SIDE_EFFECT),
    )(*[_in_hbm(a) for a in (*srcs, *lands)], *tail)
    return (send, recv, bufs, ns), token


def _exchange_wait(name, state, copies_of, after):
    send, recv, bufs, ns = state
    n = len(bufs)

    def body(*refs):
        ins = refs[:n]
        for cp in copies_of(ins[:ns], ins[ns:], refs[n], refs[n + 1], waiting=True):
            cp.wait_send()
            cp.wait_recv()

    out = pl.pallas_call(
        body, name=name, out_shape=tuple(pltpu.HBM(a.shape, a.dtype) for a in bufs),
        in_specs=[HBM] * n + [SEM, SEM, ANY], out_specs=tuple([HBM] * n),
        input_output_aliases={i: i for i in range(n)},
        compiler_params=pltpu.CompilerParams(has_side_effects=SIDE_EFFECT),
    )(*bufs, send, recv, after)
    return list(out[:ns]), list(out[ns:])


def _finish_gather(bigs, pieces):
    nb = len(bigs)

    def body(*refs):
        outs, send, recv = refs[nb:2 * nb], refs[2 * nb], refs[2 * nb + 1]
        x, y, c, _, _, ks = _place()
        fwd = [pltpu.make_async_remote_copy(
                   src_ref=outs[a].at[ks[j], c], dst_ref=outs[a].at[ks[j], c], send_sem=send.at[3 * a + j],
                   recv_sem=recv.at[3 * a + j], device_id=(x, y, 1 - c), device_id_type=MESH)
               for j in range(3) for a in range(nb)]
        for cp in fwd:
            cp.start()
        for cp in fwd:
            cp.wait()

    return pl.pallas_call(
        body, name="gather_odd_finish", in_specs=[ANY] * nb, out_specs=[ANY] * nb,
        out_shape=[jax.ShapeDtypeStruct(_piece_shape(b), BF16) for b in bigs],
        scratch_shapes=[pltpu.SemaphoreType.DMA((3 * nb,)), pltpu.SemaphoreType.DMA((3 * nb,))],
        input_output_aliases={a: a for a in range(nb)},
    )(*pieces)


def _blk(b):
    win = _shape(b, half=True, shard=True)
    return (win[0] // b.sub,) + win[1:]


def _bidx(b, h, k, st):
    idx = [0] * len(b.full)
    idx[b.haxis] = h
    idx[b.kaxis] = k
    idx[0] = idx[0] * b.sub + st
    return tuple(idx)


def _chip_sum(b, g, got, c_arr):
    blk = _blk(b)

    def body(c_ref, g_ref, r_ref, o_ref):
        del c_ref
        o_ref[...] = (g_ref[...] + r_ref[...]).astype(BF16)

    half = pl.BlockSpec(blk, lambda k, st, c_ref: _bidx(b, 0, k, st))
    return pl.pallas_call(
        body, name=f"rs_chip_sum_{b.name}",
        grid_spec=pltpu.PrefetchScalarGridSpec(
            num_scalar_prefetch=1, grid=(4, b.sub),
            in_specs=[pl.BlockSpec(blk, lambda k, st, c_ref: _bidx(b, c_ref[0], k, st)), half], out_specs=half),
        out_shape=jax.ShapeDtypeStruct(_shape(b, half=True), BF16),
        compiler_params=_cp(("arbitrary", "arbitrary"), 40),
    )(c_arr, g, got)


def _half_shapes(bigs):
    return [jax.ShapeDtypeStruct(_shape(b, half=True), F32) for b in bigs]


def _partial_shapes(bigs):
    return [jax.ShapeDtypeStruct((3,) + _shape(b, half=True, shard=True), BF16) for b in bigs]


def _shard_sum(b, mine, got, ck_arr):
    blk = _blk(b)

    def body(ck_ref, m_ref, r0, r1, r2, o_ref):
        del ck_ref
        o_ref[...] = (m_ref[...].astype(F32) + r0[...].astype(F32)) + (r1[...].astype(F32) + r2[...].astype(F32))

    def peer(j):
        return pl.BlockSpec((None,) + blk, lambda st, ck: (j,) + _bidx(b, 0, 0, st))

    return pl.pallas_call(
        body, name=f"rs_shard_sum_{b.name}",
        grid_spec=pltpu.PrefetchScalarGridSpec(
            num_scalar_prefetch=1, grid=(b.sub,),
            in_specs=[pl.BlockSpec(blk, lambda st, ck: _bidx(b, 0, ck[1], st)), peer(0), peer(1), peer(2)],
            out_specs=pl.BlockSpec(blk, lambda st, ck: _bidx(b, ck[0], 0, st))),
        out_shape=jax.ShapeDtypeStruct(_shape(b, shard=True), F32),
        compiler_params=_cp(("arbitrary",), 40),
    )(ck_arr, mine, got, got, got)


def _share_halves(gs):
    def body(*refs):
        ins, outs, send, recv = refs[:N_BIG], refs[N_BIG:2 * N_BIG], refs[2 * N_BIG], refs[2 * N_BIG + 1]
        del ins
        x, y, c, _, _, _ = _place()
        copies = [pltpu.make_async_remote_copy(src_ref=_at(outs[a], b, h=c), dst_ref=_at(outs[a], b, h=c),
                                               send_sem=send.at[a], recv_sem=recv.at[a], device_id=(x, y, 1 - c),
                                               device_id_type=MESH)
                  for a, b in enumerate(BIGS)]
        for cp in copies:
            cp.start()
        for cp in copies:
            cp.wait()

    return pl.pallas_call(
        body, name="rs_share_halves", in_specs=[ANY] * N_BIG, out_specs=[ANY] * N_BIG,
        out_shape=[jax.ShapeDtypeStruct(_shape(b, shard=True), F32) for b in BIGS],
        scratch_shapes=[pltpu.SemaphoreType.DMA((N_BIG,)), pltpu.SemaphoreType.DMA((N_BIG,))],
        input_output_aliases={a: a for a in range(N_BIG)},
    )(*gs)


def _gather_small(block):
    m_per, n = block.shape

    def body(x_ref, out_ref, send_sems, recv_sems, local_sem):
        x, y, c = lax.axis_index("x"), lax.axis_index("y"), lax.axis_index("c")
        me, sibling = (x, y, c), (x, y, 1 - c)
        chips = [(1 - x, y), (x, 1 - y), (1 - x, 1 - y)]

        def rows(px, py, pc):
            return out_ref.at[pl.ds((4 * px + 2 * py + pc) * m_per, m_per), :]

        def copy(k, blk, to, src=None):
            return pltpu.make_async_remote_copy(
                src_ref=rows(*blk) if src is None else src, dst_ref=rows(*blk), send_sem=send_sems.at[k],
                recv_sem=recv_sems.at[k], device_id=to, device_id_type=MESH)

        mine = pltpu.make_async_copy(x_ref, rows(*me), local_sem)
        mine.start()
        first = [copy(0, me, sibling, src=x_ref)]
        first += [copy(1 + j, me, (*chip, c), src=x_ref) for j, chip in enumerate(chips)]
        for cp in first:
            cp.start()
        passed = [copy(4 + j, (*chip, c), sibling) for j, chip in enumerate(chips)]
        for j, chip in enumerate(chips):
            copy(1 + j, (*chip, c), me).wait_recv()
            passed[j].start()
        copy(0, sibling, me).wait_recv()
        for j, chip in enumerate(chips):
            copy(4 + j, (*chip, 1 - c), me).wait_recv()
        for cp in first + passed:
            cp.wait_send()
        mine.wait()

    return pl.pallas_call(
        body, name="gather_small_grads",
        out_shape=jax.ShapeDtypeStruct((8 * m_per, n), block.dtype),
        in_specs=[pl.BlockSpec(memory_space=pltpu.VMEM)], out_specs=pl.BlockSpec(memory_space=pltpu.VMEM),
        scratch_shapes=[pltpu.SemaphoreType.DMA((7,)), pltpu.SemaphoreType.DMA((7,)), pltpu.SemaphoreType.DMA],
    )(block)


def _sum_small(stack):
    _, m_per, n = stack.shape

    def body(x_ref, o_ref):
        acc = x_ref[0]
        for dev in range(1, 8):
            acc = acc + x_ref[dev]
        o_ref[...] = acc

    return pl.pallas_call(body, name="sum_small_grads", out_shape=jax.ShapeDtypeStruct((m_per, n), F32))(stack)


def _adamw(name, w, g, m, v, rows):
    shape = w.shape

    def body(w_ref, g_ref, m_ref, v_ref, d_ref, mo_ref, vo_ref):
        gg = g_ref[...]
        mn = ADAM_B1 * m_ref[...] + (1.0 - ADAM_B1) * gg
        vn = ADAM_B2 * v_ref[...] + (1.0 - ADAM_B2) * (gg * gg)
        m_hat = mn / (1.0 - ADAM_B1 ** ADAM_STEP)
        v_hat = vn / (1.0 - ADAM_B2 ** ADAM_STEP)
        d_ref[...] = -ADAM_LR * (m_hat / (jnp.sqrt(v_hat) + ADAM_EPS) + ADAM_WD * w_ref[...])
        mo_ref[...] = mn
        vo_ref[...] = vn

    spec = pl.BlockSpec((rows,) + shape[1:], lambda i: (i,) + (0,) * (len(shape) - 1))
    return pl.pallas_call(
        body, name=name, grid=(shape[0] // rows,), in_specs=[spec] * 4, out_specs=[spec] * 3,
        out_shape=[jax.ShapeDtypeStruct(shape, F32)] * 3, compiler_params=_cp(("parallel",), 48),
    )(w, g, m, v)


ADAM_ROWS = dict(w_in_e=256, pool_w=4, w_out_e=256, w_in_o=128, w_out_o=256)


def _pack(parts, rows):
    flat = jnp.concatenate([p.reshape(-1).astype(F32) for p in parts])
    return jnp.pad(flat, (0, rows * 128 - flat.shape[0])).reshape(rows, 128)


def _unpack(buf, shapes):
    flat = buf.reshape(-1)
    out, off = [], 0
    for shp in shapes:
        n = 1
        for dim in shp:
            n *= dim
        out.append(flat[off:off + n].reshape(shp))
        off += n
    return out


WEIGHTS = ("even_norm", "even_w_in", "even_pool_w", "even_pool_scale", "even_ws", "even_bs", "even_w_out", "odd_norm",
           "odd_w_in", "odd_conv_w", "odd_w_out", "final_norm")
BIG_OF = dict(w_in_e="even_w_in", pool_w="even_pool_w", w_out_e="even_w_out", w_in_o="odd_w_in", w_out_o="odd_w_out")
SMALL = ("even_norm", "even_pool_scale", "even_ws", "even_bs", "final_norm", "odd_norm", "odd_conv_w")
SMALL_GRAD_ROWS = 576
SMALL_STATE_ROWS = 552


def kernel(x, even_norm, even_w_in, even_pool_w, even_pool_scale, even_ws, even_bs, even_w_out, odd_norm, odd_w_in, odd_conv_w, odd_w_out, final_norm, loss_target, m_even_norm, m_even_w_in, m_even_pool_w, m_even_pool_scale, m_even_ws, m_even_bs, m_even_w_out, m_odd_norm, m_odd_w_in, m_odd_conv_w, m_odd_w_out, m_final_norm, v_even_norm, v_even_w_in, v_even_pool_w, v_even_pool_scale, v_even_ws, v_even_bs, v_even_w_out, v_odd_norm, v_odd_w_in, v_odd_conv_w, v_odd_w_out, v_final_norm):
    wv = dict(zip(WEIGHTS, (even_norm, even_w_in, even_pool_w, even_pool_scale, even_ws, even_bs, even_w_out, odd_norm,
                            odd_w_in, odd_conv_w, odd_w_out, final_norm)))
    mv = dict(zip(WEIGHTS, (m_even_norm, m_even_w_in, m_even_pool_w, m_even_pool_scale, m_even_ws, m_even_bs,
                            m_even_w_out, m_odd_norm, m_odd_w_in, m_odd_conv_w, m_odd_w_out, m_final_norm)))
    vv = dict(zip(WEIGHTS, (v_even_norm, v_even_w_in, v_even_pool_w, v_even_pool_scale, v_even_ws, v_even_bs,
                            v_even_w_out, v_odd_norm, v_odd_w_in, v_odd_conv_w, v_odd_w_out, v_final_norm)))
    c = lax.axis_index("c")
    k_me = 2 * lax.axis_index("x") + lax.axis_index("y")

    c_arr = jnp.reshape(c, (1,)).astype(jnp.int32)
    ck_arr = jnp.stack([c, k_me]).astype(jnp.int32)
    even_bigs, odd_bigs = BIGS[:3], BIGS[3:]

    shards = {b.name: wv[BIG_OF[b.name]][0].astype(BF16) for b in BIGS}
    tiny = jnp.concatenate([odd_conv_w[0], odd_norm], axis=0)
    *pieces_even, tiny_all = _gather_weights(even_bigs, [shards[b.name] for b in even_bigs], tiny)
    tiny_full = jnp.transpose(tiny_all, (1, 0, 2)).reshape(4, D_MODEL)
    to_chips, swap_odd, partials_odd = _copies_to_chips(odd_bigs), _copies_swap_halves(odd_bigs), _copies_partials(odd_bigs)
    gather_state, gather_token = _exchange_start(
        "gather_odd_start", [shards[b.name] for b in odd_bigs],
        [jax.ShapeDtypeStruct(_piece_shape(b), BF16) for b in odd_bigs], to_chips, 3 * len(odd_bigs), after=pieces_even[-1])
    k_arr = jnp.reshape(k_me, (1,)).astype(jnp.int32)
    w = {b.name: _assemble(b, p, shards[b.name], k_arr) for b, p in zip(even_bigs, pieces_even)}
    w.update(even_norm=even_norm, pool_scale=even_pool_scale, ws=even_ws[0], bs=even_bs[0],
             final_norm=final_norm.reshape(1, D_MODEL), conv_w=tiny_full[:3], odd_norm=tiny_full[3:4])

    class Hooks(_Hooks):
        def before_even(self):
            return gather_token

        def odd_weights(self, w, x1):
            srcs, lands = _exchange_wait("gather_odd_wait", gather_state, to_chips, after=x1)
            pieces = _finish_gather(odd_bigs, lands)
            return dict(w, **{b.name: _assemble(b, p, s, k_arr) for b, p, s in zip(odd_bigs, pieces, srcs)})

        def odd_grads_ready(self, g_w_in_o, g_w_out_o):
            self.swap, token = _exchange_start("rs_odd_swap_start", [g_w_in_o, g_w_out_o], _half_shapes(odd_bigs),
                                               swap_odd, len(odd_bigs))
            return token

        def even_mix_done(self, dz_e):
            grads, got = _exchange_wait("rs_odd_swap_wait", self.swap, swap_odd, after=dz_e)
            sums = [_chip_sum(b, g, r, c_arr) for b, g, r in zip(odd_bigs, grads, got)]
            self.partials, token = _exchange_start("rs_odd_partials_start", sums, _partial_shapes(odd_bigs), partials_odd,
                                                   3 * len(odd_bigs))
            return token

        def backward_done(self, dx0):
            self.sums, self.parts = _exchange_wait("rs_odd_partials_wait", self.partials, partials_odd, after=dx0)

    hooks = Hooks()
    loss, dx, g = _local_step(x[0], loss_target[0], w, hooks)
    loss = lax.psum(loss, ("x", "y", "c"))

    got = _exchange("rs_even_swap", [g[b.name] for b in even_bigs], _half_shapes(even_bigs), _copies_swap_halves(even_bigs),
                    len(even_bigs))
    sums = [_chip_sum(b, g[b.name], r, c_arr) for b, r in zip(even_bigs, got)]
    parts = _exchange("rs_even_partials", sums, _partial_shapes(even_bigs), _copies_partials(even_bigs), 3 * len(even_bigs))
    halves = [_shard_sum(b, sm, p, ck_arr) for b, sm, p in zip(BIGS, sums + hooks.sums, list(parts) + hooks.parts)]
    g_shard = dict(zip((b.name for b in BIGS), _share_halves(halves)))

    small_g = _pack([g["even_norm"], g["pool_scale"], g["ws"], g["bs"], g["final_norm"], g["odd_norm"], g["conv_w"]],
                    SMALL_GRAD_ROWS)
    small_g = _sum_small(_gather_small(small_g).reshape(8, SMALL_GRAD_ROWS, 128))
    g_en, g_ps, g_ws, g_bs, g_fn, g_on, g_cw = _unpack(
        small_g, [(1, D_MODEL), (1, D_MODEL), (1, 4, CHUNK, CHUNK), (1, 4, CHUNK), (D_MODEL,), (1, D_MODEL), (1, 3, D_MODEL)])
    g_on = lax.dynamic_slice(g_on, (0, k_me * 256), (1, 256))
    g_cw = lax.dynamic_slice(g_cw, (0, 0, k_me * 256), (1, 3, 256))
    grad = dict(even_norm=g_en, even_pool_scale=g_ps, even_ws=g_ws, even_bs=g_bs, final_norm=g_fn, odd_norm=g_on,
                odd_conv_w=g_cw)
    for b in BIGS:
        grad[BIG_OF[b.name]] = g_shard[b.name][None]

    delta, new_m, new_v = {}, {}, {}
    for b in BIGS:
        n = BIG_OF[b.name]
        d_, m_, v_ = _adamw(f"adamw_{b.name}", wv[n][0], g_shard[b.name], mv[n][0], vv[n][0], ADAM_ROWS[b.name])
        delta[n], new_m[n], new_v[n] = d_[None], m_[None], v_[None]
    shapes = [wv[n].shape for n in SMALL]
    packed = [_pack([src[n] for n in SMALL], SMALL_STATE_ROWS) for src in (wv, grad, mv, vv)]
    outs = _adamw("adamw_small", *packed, SMALL_STATE_ROWS)
    for dst, buf in zip((delta, new_m, new_v), outs):
        for n, arr in zip(SMALL, _unpack(buf, shapes)):
            dst[n] = arr

    return (loss, dx[None], *[grad[n] for n in WEIGHTS], *[delta[n] for n in WEIGHTS], *[new_m[n] for n in WEIGHTS],
            *[new_v[n] for n in WEIGHTS])
```

```python
from typing import NamedTuple

import jax
import jax.numpy as jnp
from jax import lax
from jax.experimental import pallas as pl
from jax.experimental.pallas import tpu as pltpu

F32, BF16 = jnp.float32, jnp.bfloat16

D_MODEL = 1024
EPS = 1e-6
NEG = -1e30
POOL_SIZES = (2, 4, 8, 16)
GROUP_W = 256
CHUNK = 128
DILATIONS = (1, 4, 16)
N_SLOTS = 8
HEAD_DIM = 128
ATTN_BLOCK = 128
SCALE = HEAD_DIM ** -0.5
EVEN_IN = 5120
ODD_IN = 14336
QKV_BLOCKS = 9
ODD_BLOCKS = ODD_IN // D_MODEL
SLOPES = tuple(tuple(2.0 ** (-8.0 * (g * N_SLOTS + s + 1) / (3 * N_SLOTS)) for s in range(N_SLOTS)) for g in range(3))

ADAM_LR, ADAM_B1, ADAM_B2, ADAM_EPS, ADAM_WD, ADAM_STEP = 0.001, 0.9, 0.999, 1e-08, 0.01, 10

HALO = 16
TS = 512
TM = 256
MM_ROWS = 1024
DW_IN_TOKENS = 2048
DW_OUT_TOKENS = 1024
MIB = 1 << 20
MESH = pl.DeviceIdType.MESH
ANY = pl.BlockSpec(memory_space=pl.ANY)


def _cp(sem, vmem_mib):
    return pltpu.CompilerParams(dimension_semantics=sem, vmem_limit_bytes=vmem_mib * MIB)


def _sig(x):
    return 0.5 * jnp.tanh(0.5 * x) + 0.5


def _win_sum(e, w, forward):
    n = e.shape[0]
    k = 1
    while k < w:
        e = e + pltpu.roll(e, (n - k) if forward else k, 0)
        k *= 2
    return e


def _mm_nn(name, a, b, tm, tn, out_dtype, resid=None, col_map=None, n_cols=None, into=None):
    m, k = a.shape
    n = b.shape[1]
    if col_map is None:
        col_map, n_cols = (lambda j: j), n // tn

    def body(*refs):
        a_ref, b_ref = refs[0], refs[1]
        acc = jnp.dot(a_ref[...].astype(BF16), b_ref[...], preferred_element_type=F32)
        if resid is not None:
            acc = acc + refs[2][...]
        o_ref = refs[-1]
        o_ref[...] = acc.astype(out_dtype)

    in_specs = [pl.BlockSpec((tm, k), lambda j, i: (i, 0)), pl.BlockSpec((k, tn), lambda j, i: (0, col_map(j)))]
    args = [a, b]
    if resid is not None:
        in_specs.append(pl.BlockSpec((tm, tn), lambda j, i: (i, col_map(j))))
        args.append(resid)
    aliases = {}
    if into is not None:
        aliases = {len(args): 0}
        in_specs.append(ANY)
        args.append(into)
    return pl.pallas_call(
        body, name=name, grid=(n_cols, m // tm), in_specs=in_specs,
        out_specs=pl.BlockSpec((tm, tn), lambda j, i: (i, col_map(j))),
        out_shape=jax.ShapeDtypeStruct((m, n), out_dtype), input_output_aliases=aliases,
        compiler_params=_cp(("parallel", "parallel"), 48),
    )(*args)


def _mm_nt(name, a, b, tm, tk, out_dtype, k_map=None, nk=None):
    m, k = a.shape
    n = b.shape[0]
    if k_map is None:
        k_map, nk = (lambda kk: kk), k // tk

    def body(a_ref, b_ref, o_ref, acc_ref):
        kk = pl.program_id(1)
        p = lax.dot_general(a_ref[...].astype(BF16), b_ref[...], (((1,), (1,)), ((), ())), preferred_element_type=F32)
        if nk == 1:
            o_ref[...] = p.astype(out_dtype)
        else:
            @pl.when(kk == 0)
            def _():
                acc_ref[...] = p

            @pl.when(kk > 0)
            def _():
                acc_ref[...] += p

            @pl.when(kk == nk - 1)
            def _():
                o_ref[...] = acc_ref[...].astype(out_dtype)

    return pl.pallas_call(
        body, name=name, grid=(m // tm, nk),
        in_specs=[pl.BlockSpec((tm, tk), lambda i, kk: (i, k_map(kk))), pl.BlockSpec((n, tk), lambda i, kk: (0, k_map(kk)))],
        out_specs=pl.BlockSpec((tm, n), lambda i, kk: (i, 0)),
        out_shape=jax.ShapeDtypeStruct((m, n), out_dtype),
        scratch_shapes=[pltpu.VMEM((tm, n) if nk > 1 else (8, 128), F32)],
        compiler_params=_cp(("parallel", "arbitrary"), 56),
    )(a, b)


def _mm_tn(name, a, g, tn, ts, col_map=None, n_cols=None, into=None, after=None):
    s, ka = a.shape
    n = g.shape[1]
    if col_map is None:
        col_map, n_cols = (lambda j: j), n // tn

    def body(a_ref, g_ref, *rest):
        o_ref = rest[-1]
        st = pl.program_id(1)
        p = lax.dot_general(a_ref[...], g_ref[...].astype(BF16), (((0,), (0,)), ((), ())), preferred_element_type=F32)

        @pl.when(st == 0)
        def _():
            o_ref[...] = p

        @pl.when(st > 0)
        def _():
            o_ref[...] += p

    in_specs = [pl.BlockSpec((ts, ka), lambda j, st: (st, 0)), pl.BlockSpec((ts, tn), lambda j, st: (st, col_map(j)))]
    args = [a, g]
    aliases = {}
    if into is not None:
        aliases = {2: 0}
        in_specs.append(ANY)
        args.append(into)
    if after is not None:
        in_specs.append(ANY)
        args.append(after)
    return pl.pallas_call(
        body, name=name, grid=(n_cols, s // ts), in_specs=in_specs,
        out_specs=pl.BlockSpec((ka, tn), lambda j, st: (0, col_map(j))),
        out_shape=jax.ShapeDtypeStruct((ka, n), F32), input_output_aliases=aliases,
        compiler_params=_cp(("parallel", "arbitrary"), 56),
    )(*args)


def _rms_fwd(name, x, g, after=None):
    s = x.shape[0]

    def body(x_ref, g_ref, *rest):
        xf = x_ref[...]
        r = lax.rsqrt(jnp.mean(xf * xf, axis=-1, keepdims=True) + EPS)
        rest[-1][...] = (xf * r * g_ref[...]).astype(BF16)

    row = pl.BlockSpec((TS, D_MODEL), lambda i: (i, 0))
    in_specs, args = [row, pl.BlockSpec((1, D_MODEL), lambda i: (0, 0))], [x, g]
    if after is not None:
        in_specs.append(ANY)
        args.append(after)
    return pl.pallas_call(
        body, name=name, grid=(s // TS,), in_specs=in_specs, out_specs=row,
        out_shape=jax.ShapeDtypeStruct((s, D_MODEL), BF16), compiler_params=_cp(("parallel",), 32),
    )(*args)


def _class_major(a, d):
    return a.reshape(d, a.shape[0] // d, a.shape[1])


def _class_spec(d, tile, width):
    return pl.BlockSpec((d, tile // d, width), lambda i: (0, i, 0))


LANES = 128


def _token_scratch(tile, width):
    return pltpu.VMEM((width // LANES, tile, LANES), F32)


def _put(scr, val):
    for c in range(scr.shape[0]):
        scr[c] = val[:, c * LANES:(c + 1) * LANES]


def _get(scr):
    return jnp.concatenate([scr[c] for c in range(scr.shape[0])], axis=1)


def _to_classes(ref3, scr, d, dtype):
    n = ref3.shape[1]
    for c in range(scr.shape[0]):
        for r in range(d):
            ref3[r, :, c * LANES:(c + 1) * LANES] = scr.at[c][pl.ds(r, n, stride=d), :].astype(dtype)


def _from_classes(scr, ref3, d):
    n = ref3.shape[1]
    for c in range(scr.shape[0]):
        for r in range(d):
            scr.at[c][pl.ds(r, n, stride=d), :] = ref3[r, :, c * LANES:(c + 1) * LANES].astype(F32)


def _rms_fwd_orders(name, x, g):
    s = x.shape[0]

    def body(x_ref, g_ref, h_ref, h4_ref, h16_ref, scr):
        xf = x_ref[...]
        r = lax.rsqrt(jnp.mean(xf * xf, axis=-1, keepdims=True) + EPS)
        h = xf * r * g_ref[...]
        h_ref[...] = h.astype(BF16)
        _put(scr, h)
        _to_classes(h4_ref, scr, 4, BF16)
        _to_classes(h16_ref, scr, 16, BF16)

    row = pl.BlockSpec((TS, D_MODEL), lambda i: (i, 0))
    h, h4, h16 = pl.pallas_call(
        body, name=name, grid=(s // TS,), in_specs=[row, pl.BlockSpec((1, D_MODEL), lambda i: (0, 0))],
        out_specs=[row, _class_spec(4, TS, D_MODEL), _class_spec(16, TS, D_MODEL)],
        out_shape=[jax.ShapeDtypeStruct((s, D_MODEL), BF16), jax.ShapeDtypeStruct((4, s // 4, D_MODEL), BF16),
                   jax.ShapeDtypeStruct((16, s // 16, D_MODEL), BF16)],
        scratch_shapes=[_token_scratch(TS, D_MODEL)],
        compiler_params=_cp(("parallel",), 32),
    )(x, g)
    return h, h4.reshape(s, D_MODEL), h16.reshape(s, D_MODEL)


def _rms_bwd(name, dh, x, g, dres, dh4=None, dh16=None):
    s = x.shape[0]
    extra = dh4 is not None

    def body(dh_ref, x_ref, g_ref, dres_ref, *rest):
        if extra:
            dh4_ref, dh16_ref, dx_ref, dg_ref, scr = rest
        else:
            dx_ref, dg_ref = rest
        xf = x_ref[...]
        r = lax.rsqrt(jnp.mean(xf * xf, axis=-1, keepdims=True) + EPS)
        xh = xf * r
        dhf = dh_ref[...]
        if extra:
            _from_classes(scr, dh4_ref, 4)
            dhf = dhf + _get(scr)
            _from_classes(scr, dh16_ref, 16)
            dhf = dhf + _get(scr)
        dxh = dhf * g_ref[...]
        dx_ref[...] = dres_ref[...] + r * (dxh - xh * jnp.mean(dxh * xh, axis=-1, keepdims=True))
        part = jnp.sum(dhf * xh, axis=0, keepdims=True)

        @pl.when(pl.program_id(0) == 0)
        def _():
            dg_ref[...] = part

        @pl.when(pl.program_id(0) > 0)
        def _():
            dg_ref[...] += part

    row = pl.BlockSpec((TS, D_MODEL), lambda i: (i, 0))
    vec = pl.BlockSpec((1, D_MODEL), lambda i: (0, 0))
    in_specs, args, scratch = [row, row, vec, row], [dh, x, g, dres], []
    if extra:
        in_specs += [_class_spec(4, TS, D_MODEL), _class_spec(16, TS, D_MODEL)]
        args += [_class_major(dh4, 4), _class_major(dh16, 16)]
        scratch = [_token_scratch(TS, D_MODEL)]
    return pl.pallas_call(
        body, name=name, grid=(s // TS,), in_specs=in_specs, out_specs=[row, vec],
        out_shape=[jax.ShapeDtypeStruct((s, D_MODEL), F32), jax.ShapeDtypeStruct((1, D_MODEL), F32)],
        scratch_shapes=scratch, compiler_params=_cp(("arbitrary",), 40),
    )(*args)


def _final_loss(x, g, target):
    s = x.shape[0]

    def body(x_ref, g_ref, t_ref, dx_ref, loss_ref, dg_ref):
        xf = x_ref[...]
        gg = g_ref[...]
        r = lax.rsqrt(jnp.mean(xf * xf, axis=-1, keepdims=True) + EPS)
        xh = xf * r
        e = xh * gg - t_ref[...]
        dy = e * (1.0 / D_MODEL)
        dxh = dy * gg
        dx_ref[...] = r * (dxh - xh * jnp.mean(dxh * xh, axis=-1, keepdims=True))
        lpart = 0.5 * jnp.sum(jnp.mean(e * e, axis=-1, keepdims=True), axis=0, keepdims=True)
        lpart = jnp.broadcast_to(lpart, (8, 128))
        gpart = jnp.sum(dy * xh, axis=0, keepdims=True)

        @pl.when(pl.program_id(0) == 0)
        def _():
            loss_ref[...] = lpart
            dg_ref[...] = gpart

        @pl.when(pl.program_id(0) > 0)
        def _():
            loss_ref[...] += lpart
            dg_ref[...] += gpart

    row = pl.BlockSpec((TS, D_MODEL), lambda i: (i, 0))
    vec = pl.BlockSpec((1, D_MODEL), lambda i: (0, 0))
    return pl.pallas_call(
        body, name="final_loss", grid=(s // TS,), in_specs=[row, vec, row],
        out_specs=[row, pl.BlockSpec((8, 128), lambda i: (0, 0)), vec],
        out_shape=[jax.ShapeDtypeStruct((s, D_MODEL), F32), jax.ShapeDtypeStruct((8, 128), F32),
                   jax.ShapeDtypeStruct((1, D_MODEL), F32)],
        compiler_params=_cp(("arbitrary",), 40),
    )(x, g, target)


def _zcol(c, tm=TM):
    return pl.BlockSpec((tm, D_MODEL), lambda i, c=c: (i, c))


def _prev_halo(c, tm=TM):
    return pl.BlockSpec((HALO, D_MODEL), lambda i, c=c: (jnp.maximum(i * (tm // HALO) - 1, 0), c))


def _next_halo(c, n_rows, tm=TM):
    last = n_rows // HALO - 1
    return pl.BlockSpec((HALO, D_MODEL), lambda i, c=c: (jnp.minimum((i + 1) * (tm // HALO), last), c))


def _full(shape):
    return pl.BlockSpec(shape, lambda i: (0,) * len(shape))


def _inv_count(first_row, n, w):
    t = first_row + lax.broadcasted_iota(jnp.int32, (n, 1), 0)
    return 1.0 / jnp.minimum(t + 1, w).astype(F32)


def _even_mix_fwd(z, pw, ps, wt, bs):
    s = z.shape[0]

    def body(a_ref, ga_ref, u_ref, v_ref, gb_ref, halo_ref, pw_ref, ps_ref, wt_ref, bs_ref, y_ref):
        i = pl.program_id(0)
        a = a_ref[...].astype(F32)
        halo = jnp.where(i > 0, halo_ref[...].astype(F32), 0.0)
        ext = jnp.concatenate([halo, a], axis=0)
        ga = ga_ref[...].astype(F32)
        sga = ga * _sig(ga)
        for g, w in enumerate(POOL_SIZES):
            cs = slice(g * GROUP_W, (g + 1) * GROUP_W)
            win = _win_sum(ext[:, cs], w, False)[HALO:]
            pooled = win * _inv_count(i * TM, TM, w) - a[:, cs]
            mixed = jnp.dot(pooled.astype(BF16), pw_ref[g], preferred_element_type=F32)
            y_ref[:, cs] = (mixed * ps_ref[:, cs] * sga[:, cs]).astype(BF16)
        gb = gb_ref[...].astype(F32)
        gate = u_ref[...].astype(F32) * (gb * _sig(gb))
        for ch in range(TM // CHUNK):
            rs = slice(ch * CHUNK, (ch + 1) * CHUNK)
            for g in range(4):
                cs = slice(g * GROUP_W, (g + 1) * GROUP_W)
                mixb = jnp.dot(wt_ref[g], v_ref[rs, cs], preferred_element_type=F32) + bs_ref[g]
                y_ref[rs, D_MODEL + g * GROUP_W:D_MODEL + (g + 1) * GROUP_W] = (gate[rs, cs] * mixb).astype(BF16)

    return pl.pallas_call(
        body, name="even_mix_fwd", grid=(s // TM,),
        in_specs=[_zcol(0), _zcol(1), _zcol(2), _zcol(3), _zcol(4), _prev_halo(0),
                  _full((4, GROUP_W, GROUP_W)), _full((1, D_MODEL)), _full((4, CHUNK, CHUNK)), _full((4, CHUNK, 1))],
        out_specs=pl.BlockSpec((TM, 2 * D_MODEL), lambda i: (i, 0)),
        out_shape=jax.ShapeDtypeStruct((s, 2 * D_MODEL), BF16),
        compiler_params=_cp(("parallel",), 48),
    )(z, z, z, z, z, z, pw, ps, wt, bs)


def _even_mix_bwd(dy, z, pw, ps, wt, wtt, bs, after=None):
    s = z.shape[0]
    n_tiles = s // TM
    tail_specs, tail_args = ([ANY], [after]) if after is not None else ([], [])

    def body(dy_ref, a_ref, ga_ref, u_ref, v_ref, gb_ref, halo_ref, dyn_ref, gan_ref, pw_ref, ps_ref, wt_ref, wtt_ref,
             bs_ref, *rest):
        dz_ref, dpw_ref, dps_ref, dws_ref, dbs_ref = rest[-5:]
        i = pl.program_id(0)

        @pl.when(i == 0)
        def _():
            dpw_ref[...] = jnp.zeros_like(dpw_ref)
            dps_ref[...] = jnp.zeros_like(dps_ref)
            dws_ref[...] = jnp.zeros_like(dws_ref)
            dbs_ref[...] = jnp.zeros_like(dbs_ref)

        a = a_ref[...].astype(F32)
        halo = jnp.where(i > 0, halo_ref[...].astype(F32), 0.0)
        ext = jnp.concatenate([halo, a], axis=0)
        ga = ga_ref[...].astype(F32)
        sg = _sig(ga)
        sga = ga * sg
        dsga = sg * (1.0 + ga * (1.0 - sg))
        dya = dy_ref[:, :D_MODEL].astype(F32)
        gan = gan_ref[...].astype(F32)
        dmn_all = jnp.where(i < n_tiles - 1, dyn_ref[...].astype(F32) * ps_ref[...] * (gan * _sig(gan)), 0.0)
        for g, w in enumerate(POOL_SIZES):
            cs = slice(g * GROUP_W, (g + 1) * GROUP_W)
            inv = _inv_count(i * TM, TM, w)
            pooled = _win_sum(ext[:, cs], w, False)[HALO:] * inv - a[:, cs]
            pb = pooled.astype(BF16)
            mixed = jnp.dot(pb, pw_ref[g], preferred_element_type=F32)
            dyg = dya[:, cs]
            psg = ps_ref[:, cs]
            dm = (dyg * psg * sga[:, cs]).astype(BF16)
            dz_ref[:, D_MODEL + g * GROUP_W:D_MODEL + (g + 1) * GROUP_W] = (dyg * mixed * psg * dsga[:, cs]).astype(BF16)
            dps_ref[:, cs] += jnp.sum(dyg * mixed * sga[:, cs], axis=0, keepdims=True)
            dpw_ref[g] += lax.dot_general(pb, dm, (((0,), (0,)), ((), ())), preferred_element_type=F32)
            nt = (((1,), (1,)), ((), ()))
            dpool = lax.dot_general(dm, pw_ref[g], nt, preferred_element_type=F32)
            dpool_n = lax.dot_general(dmn_all[:, cs].astype(BF16), pw_ref[g], nt, preferred_element_type=F32)
            e = jnp.concatenate([dpool * inv, dpool_n * _inv_count((i + 1) * TM, HALO, w)], axis=0)
            dz_ref[:, cs] = (_win_sum(e, w, True)[:TM] - dpool).astype(BF16)

        gb = gb_ref[...].astype(F32)
        sg = _sig(gb)
        sgb = gb * sg
        dsgb = sg * (1.0 + gb * (1.0 - sg))
        u = u_ref[...].astype(F32)
        dyb = dy_ref[:, D_MODEL:].astype(F32)
        tril = lax.broadcasted_iota(jnp.int32, (CHUNK, CHUNK), 0) >= lax.broadcasted_iota(jnp.int32, (CHUNK, CHUNK), 1)
        lane = lax.broadcasted_iota(jnp.int32, (CHUNK, 128), 1)
        for ch in range(TM // CHUNK):
            rs = slice(ch * CHUNK, (ch + 1) * CHUNK)
            for g in range(4):
                cs = slice(g * GROUP_W, (g + 1) * GROUP_W)
                vb = v_ref[rs, cs]
                mixb = jnp.dot(wt_ref[g], vb, preferred_element_type=F32) + bs_ref[g]
                dyu = dyb[rs, cs] * u[rs, cs]
                dmix = dyu * sgb[rs, cs]
                dmb = dmix.astype(BF16)
                o = g * GROUP_W
                dz_ref[rs, 2 * D_MODEL + o:2 * D_MODEL + o + GROUP_W] = (dyb[rs, cs] * mixb * sgb[rs, cs]).astype(BF16)
                dz_ref[rs, 3 * D_MODEL + o:3 * D_MODEL + o + GROUP_W] = jnp.dot(
                    wtt_ref[g], dmb, preferred_element_type=F32).astype(BF16)
                dz_ref[rs, 4 * D_MODEL + o:4 * D_MODEL + o + GROUP_W] = (dyu * mixb * dsgb[rs, cs]).astype(BF16)
                dws = lax.dot_general(dmb, vb, (((1,), (1,)), ((), ())), preferred_element_type=F32)
                dws_ref[g] += jnp.where(tril, dws, 0.0)
                dbs_ref[...] += jnp.where(lane == g, jnp.sum(dmix, axis=1, keepdims=True), 0.0)

    return pl.pallas_call(
        body, name="even_mix_bwd", grid=(n_tiles,),
        in_specs=[pl.BlockSpec((TM, 2 * D_MODEL), lambda i: (i, 0)), _zcol(0), _zcol(1), _zcol(2), _zcol(3), _zcol(4),
                  _prev_halo(0), _next_halo(0, s), _next_halo(1, s),
                  _full((4, GROUP_W, GROUP_W)), _full((1, D_MODEL)), _full((4, CHUNK, CHUNK)), _full((4, CHUNK, CHUNK)),
                  _full((4, CHUNK, 1))] + tail_specs,
        out_specs=[pl.BlockSpec((TM, EVEN_IN), lambda i: (i, 0)), _full((4, GROUP_W, GROUP_W)), _full((1, D_MODEL)),
                   _full((4, CHUNK, CHUNK)), _full((CHUNK, 128))],
        out_shape=[jax.ShapeDtypeStruct((s, EVEN_IN), BF16), jax.ShapeDtypeStruct((4, GROUP_W, GROUP_W), F32),
                   jax.ShapeDtypeStruct((1, D_MODEL), F32), jax.ShapeDtypeStruct((4, CHUNK, CHUNK), F32),
                   jax.ShapeDtypeStruct((CHUNK, 128), F32)],
        compiler_params=_cp(("arbitrary",), 56),
    )(dy, z, z, z, z, z, z, dy, z, pw, ps, wt, wtt, bs, *tail_args)


STAT_W = 128
Q_BLOCKS = 2
Q_ROWS = Q_BLOCKS * ATTN_BLOCK


def _band(d):
    row = lax.broadcasted_iota(jnp.int32, (ATTN_BLOCK, 2 * ATTN_BLOCK), 0)
    col = lax.broadcasted_iota(jnp.int32, (ATTN_BLOCK, 2 * ATTN_BLOCK), 1)
    steps = row + ATTN_BLOCK - col
    return (steps >= 0) & (steps <= ATTN_BLOCK), col >= ATTN_BLOCK, -(steps * d).astype(F32)


def _attn_fwd(z, gi):
    s = z.shape[0]
    d = DILATIONS[gi]
    nb = s // d // ATTN_BLOCK
    nq = nb // Q_BLOCKS

    def spec(which, prev=False):
        cb = which * 3 + gi
        if prev:
            return pl.BlockSpec((ATTN_BLOCK, D_MODEL), lambda r, i: (r * nb + jnp.maximum(Q_BLOCKS * i - 1, 0), cb))
        return pl.BlockSpec((Q_ROWS, D_MODEL), lambda r, i: (r * nq + i, cb))

    def body(q_ref, kp_ref, kc_ref, vp_ref, vc_ref, o_ref, lse_ref):
        i = pl.program_id(1)
        inner, own, negdist = _band(d)
        lane = lax.broadcasted_iota(jnp.int32, (ATTN_BLOCK, STAT_W), 1)
        for b in range(Q_BLOCKS):
            rows = slice(b * ATTN_BLOCK, (b + 1) * ATTN_BLOCK)
            valid = (inner & ((i > 0) | own)) if b == 0 else inner
            stat = jnp.zeros((ATTN_BLOCK, STAT_W), F32)
            for h in range(N_SLOTS):
                sl = slice(h * HEAD_DIM, (h + 1) * HEAD_DIM)
                if b == 0:
                    k = jnp.concatenate([kp_ref[:, sl], kc_ref[:ATTN_BLOCK, sl]], axis=0)
                    v = jnp.concatenate([vp_ref[:, sl], vc_ref[:ATTN_BLOCK, sl]], axis=0)
                else:
                    k = kc_ref[(b - 1) * ATTN_BLOCK:(b + 1) * ATTN_BLOCK, sl]
                    v = vc_ref[(b - 1) * ATTN_BLOCK:(b + 1) * ATTN_BLOCK, sl]
                sc = lax.dot_general(q_ref[rows, sl], k, (((1,), (1,)), ((), ())), preferred_element_type=F32) * SCALE
                sc = jnp.where(valid, sc + SLOPES[gi][h] * negdist, NEG)
                m = jnp.max(sc, axis=-1, keepdims=True)
                p = jnp.exp(sc - m)
                l = jnp.sum(p, axis=-1, keepdims=True)
                o = jnp.dot((p * (1.0 / l)).astype(BF16), v, preferred_element_type=F32)
                o_ref[rows, sl] = o.astype(BF16)
                stat = jnp.where(lane == h, m + jnp.log(l), stat)
            lse_ref[rows, :] = stat

    return pl.pallas_call(
        body, name=f"attn_fwd_d{d}", grid=(d, nq),
        in_specs=[spec(0), spec(1, True), spec(1), spec(2, True), spec(2)],
        out_specs=[pl.BlockSpec((Q_ROWS, D_MODEL), lambda r, i: (r * nq + i, 0)),
                   pl.BlockSpec((Q_ROWS, STAT_W), lambda r, i: (r * nq + i, 0))],
        out_shape=[jax.ShapeDtypeStruct((s, D_MODEL), BF16), jax.ShapeDtypeStruct((s, STAT_W), F32)],
        compiler_params=_cp(("parallel", "parallel"), 32),
    )(z, z, z, z, z)


def _attn_bwd(z, dyc, ltot, dst, dz, gi):
    s = z.shape[0]
    d = DILATIONS[gi]
    nb = s // d // ATTN_BLOCK
    nq = nb // Q_BLOCKS
    n_steps = d * nq

    def rev(cb, width=D_MODEL, prev=False):
        if prev:
            return pl.BlockSpec((ATTN_BLOCK, width), lambda r, n: (r * nb + jnp.maximum(Q_BLOCKS * (nq - 1 - n) - 1, 0), cb))
        return pl.BlockSpec((Q_ROWS, width), lambda r, n: (r * nq + nq - 1 - n, cb))

    def body(q_ref, kp_ref, kc_ref, vp_ref, vc_ref, dy_ref, l_ref, d_ref, dz_in, dz_out, dq_s, dk_s, dv_s, ck_s, cv_s, sems):
        del dz_in
        r = pl.program_id(0)
        n = pl.program_id(1)
        i = nq - 1 - n
        step = r * nq + n

        def out_copy(src, which):
            rows = pl.ds(pl.multiple_of((r * nq + i) * Q_ROWS, Q_ROWS), Q_ROWS)
            return pltpu.make_async_copy(src, dz_out.at[rows, pl.ds((which * 3 + gi) * D_MODEL, D_MODEL)], sems.at[which])

        copies = [out_copy(dq_s, 0), out_copy(dk_s, 1), out_copy(dv_s, 2)]

        @pl.when(step > 0)
        def _():
            for cp in copies:
                cp.wait()

        @pl.when(n == 0)
        def _():
            ck_s[...] = jnp.zeros_like(ck_s)
            cv_s[...] = jnp.zeros_like(cv_s)

        row = lax.broadcasted_iota(jnp.int32, (Q_ROWS, Q_ROWS + ATTN_BLOCK), 0)
        col = lax.broadcasted_iota(jnp.int32, (Q_ROWS, Q_ROWS + ATTN_BLOCK), 1)
        steps = row + ATTN_BLOCK - col
        valid = (steps >= 0) & (steps <= ATTN_BLOCK) & ((i > 0) | (col >= ATTN_BLOCK))
        negdist = -(steps * d).astype(F32)
        nt = (((1,), (1,)), ((), ()))
        tn = (((0,), (0,)), ((), ()))
        for h in range(N_SLOTS):
            sl = slice(h * HEAD_DIM, (h + 1) * HEAD_DIM)
            q = q_ref[:, sl]
            k = jnp.concatenate([kp_ref[:, sl], kc_ref[:, sl]], axis=0)
            v = jnp.concatenate([vp_ref[:, sl], vc_ref[:, sl]], axis=0)
            dy = dy_ref[:, sl]
            sc = lax.dot_general(q, k, nt, preferred_element_type=F32) * SCALE + SLOPES[gi][h] * negdist
            p = jnp.where(valid, jnp.exp(sc - l_ref[:, h:h + 1]), 0.0)
            dp = lax.dot_general(dy, v, nt, preferred_element_type=F32)
            ds = (p * (dp - d_ref[:, h:h + 1])).astype(BF16)
            dq_s[:, sl] = (jnp.dot(ds, k, preferred_element_type=F32) * SCALE).astype(BF16)
            dk = lax.dot_general(ds, q, tn, preferred_element_type=F32) * SCALE
            dv = lax.dot_general(p.astype(BF16), dy, tn, preferred_element_type=F32)
            dk_s[:Q_ROWS - ATTN_BLOCK, sl] = dk[ATTN_BLOCK:Q_ROWS].astype(BF16)
            dv_s[:Q_ROWS - ATTN_BLOCK, sl] = dv[ATTN_BLOCK:Q_ROWS].astype(BF16)
            dk_s[Q_ROWS - ATTN_BLOCK:, sl] = (ck_s[:, sl] + dk[Q_ROWS:]).astype(BF16)
            dv_s[Q_ROWS - ATTN_BLOCK:, sl] = (cv_s[:, sl] + dv[Q_ROWS:]).astype(BF16)
            ck_s[:, sl] = dk[:ATTN_BLOCK]
            cv_s[:, sl] = dv[:ATTN_BLOCK]

        for cp in copies:
            cp.start()

        @pl.when(step == n_steps - 1)
        def _():
            for cp in copies:
                cp.wait()

    stage = pltpu.VMEM((Q_ROWS, D_MODEL), BF16)
    carry = pltpu.VMEM((ATTN_BLOCK, D_MODEL), F32)
    return pl.pallas_call(
        body, name=f"attn_bwd_d{d}", grid=(d, nq),
        in_specs=[rev(gi), rev(3 + gi, prev=True), rev(3 + gi), rev(6 + gi, prev=True), rev(6 + gi),
                  rev(0), rev(0, STAT_W), rev(0, STAT_W), ANY],
        out_specs=ANY,
        out_shape=jax.ShapeDtypeStruct((s, ODD_IN), BF16),
        scratch_shapes=[stage, stage, stage, carry, carry, pltpu.SemaphoreType.DMA((3,))],
        input_output_aliases={8: 0},
        compiler_params=_cp(("arbitrary", "arbitrary"), 32),
    )(z, z, z, z, z, dyc, ltot, dst, dz)


def _odd_mix_fwd(z, os_, lses, cw):
    s = z.shape[0]

    def body(o0, o1, o2, l0, l1, l2, gc_ref, db_ref, dc_ref, dx_ref, gd_ref, hc_ref, hx_ref, cw_ref, y_ref, yc_ref, lt_ref,
             lt4_ref, lt16_ref, scr_o, scr_o2, scr_l):
        i = pl.program_id(0)
        _from_classes(scr_l, l1, 4)
        lse1 = _get(scr_l)
        _from_classes(scr_l, l2, 16)
        ls = [l0[...], lse1, _get(scr_l)]
        lmax = jnp.maximum(jnp.maximum(ls[0], ls[1]), ls[2])
        es = [jnp.exp(l - lmax) for l in ls]
        den = es[0] + es[1] + es[2]
        alpha = [e / den for e in es]
        ltot = lmax + jnp.log(den)
        lt_ref[...] = ltot
        _put(scr_l, ltot)
        _to_classes(lt4_ref, scr_l, 4, F32)
        _to_classes(lt16_ref, scr_l, 16, F32)
        _from_classes(scr_o, o1, 4)
        _from_classes(scr_o2, o2, 16)
        for h in range(N_SLOTS):
            sl = slice(h * HEAD_DIM, (h + 1) * HEAD_DIM)
            yc = (alpha[0][:, h:h + 1] * o0[:, sl].astype(F32) + alpha[1][:, h:h + 1] * scr_o[h]
                  + alpha[2][:, h:h + 1] * scr_o2[h])
            yc_ref[:, sl] = yc.astype(BF16)
            gc = gc_ref[:, sl].astype(F32)
            y_ref[:, sl] = (yc * (gc * _sig(gc))).astype(BF16)
            zc = dc_ref[:, sl].astype(F32) * dx_ref[:, sl].astype(F32)
            halo = jnp.where(i > 0, hc_ref[:, sl].astype(F32) * hx_ref[:, sl].astype(F32), 0.0)
            ext = jnp.concatenate([halo, zc], axis=0)
            z1 = pltpu.roll(ext, 1, 0)[HALO:]
            z2 = pltpu.roll(ext, 2, 0)[HALO:]
            conv = cw_ref[0:1, sl] * z2 + cw_ref[1:2, sl] * z1 + cw_ref[2:3, sl] * zc
            gd = gd_ref[:, sl].astype(F32)
            y_ref[:, D_MODEL + h * HEAD_DIM:D_MODEL + (h + 1) * HEAD_DIM] = (
                db_ref[:, sl].astype(F32) * conv * (gd * _sig(gd))).astype(BF16)

    row = pl.BlockSpec((TM, D_MODEL), lambda i: (i, 0))
    stat = pl.BlockSpec((TM, STAT_W), lambda i: (i, 0))
    y, ycr, lt, lt4, lt16 = pl.pallas_call(
        body, name="odd_mix_fwd", grid=(s // TM,),
        in_specs=[row, _class_spec(4, TM, D_MODEL), _class_spec(16, TM, D_MODEL),
                  stat, _class_spec(4, TM, STAT_W), _class_spec(16, TM, STAT_W),
                  _zcol(9), _zcol(10), _zcol(11), _zcol(12), _zcol(13), _prev_halo(11), _prev_halo(12), _full((3, D_MODEL))],
        out_specs=[pl.BlockSpec((TM, 2 * D_MODEL), lambda i: (i, 0)), row, stat, _class_spec(4, TM, STAT_W),
                   _class_spec(16, TM, STAT_W)],
        out_shape=[jax.ShapeDtypeStruct((s, 2 * D_MODEL), BF16), jax.ShapeDtypeStruct((s, D_MODEL), BF16),
                   jax.ShapeDtypeStruct((s, STAT_W), F32), jax.ShapeDtypeStruct((4, s // 4, STAT_W), F32),
                   jax.ShapeDtypeStruct((16, s // 16, STAT_W), F32)],
        scratch_shapes=[_token_scratch(TM, D_MODEL), _token_scratch(TM, D_MODEL), _token_scratch(TM, STAT_W)],
        compiler_params=_cp(("parallel",), 48),
    )(os_[0], _class_major(os_[1], 4), _class_major(os_[2], 16), lses[0], _class_major(lses[1], 4),
      _class_major(lses[2], 16), z, z, z, z, z, z, z, cw)
    return y, ycr, [lt, lt4.reshape(s, STAT_W), lt16.reshape(s, STAT_W)]


def _odd_mix_bwd(dy, z, ycr, cw):
    s = z.shape[0]
    n_tiles = s // TM
    rest = ODD_IN - QKV_BLOCKS * D_MODEL

    def body(dy_ref, yc_ref, gc_ref, db_ref, dc_ref, dx_ref, gd_ref, hc_ref, hx_ref, dyn_ref, dbn_ref, gdn_ref, cw_ref,
             dz_ref, dyc_ref, dyc4_ref, dyc16_ref, dd_ref, dd4_ref, dd16_ref, dcw_ref, stage, sem, scr_o, scr_l):
        i = pl.program_id(0)
        out = pltpu.make_async_copy(
            stage, dz_ref.at[pl.ds(pl.multiple_of(i * TM, TM), TM), pl.ds(QKV_BLOCKS * D_MODEL, rest)], sem)

        @pl.when(i > 0)
        def _():
            out.wait()

        @pl.when(i == 0)
        def _():
            dcw_ref[...] = jnp.zeros_like(dcw_ref)

        lane = lax.broadcasted_iota(jnp.int32, (TM, STAT_W), 1)
        stat = jnp.zeros((TM, STAT_W), F32)
        nrow = TM + HALO
        for h in range(N_SLOTS):
            sl = slice(h * HEAD_DIM, (h + 1) * HEAD_DIM)
            sd = slice(D_MODEL + h * HEAD_DIM, D_MODEL + (h + 1) * HEAD_DIM)
            dyc_in = dy_ref[:, sl].astype(F32)
            gc = gc_ref[:, sl].astype(F32)
            sg = _sig(gc)
            yc = yc_ref[:, sl].astype(F32)
            dyc = dyc_in * (gc * sg)
            dyc_ref[:, sl] = dyc.astype(BF16)
            scr_o[h] = dyc
            stage[:, sl] = (dyc_in * yc * (sg * (1.0 + gc * (1.0 - sg)))).astype(BF16)
            stat = jnp.where(lane == h, jnp.sum(dyc * yc, axis=-1, keepdims=True), stat)
            dc = dc_ref[:, sl].astype(F32)
            dx = dx_ref[:, sl].astype(F32)
            zc = dc * dx
            halo = jnp.where(i > 0, hc_ref[:, sl].astype(F32) * hx_ref[:, sl].astype(F32), 0.0)
            ext = jnp.concatenate([halo, zc], axis=0)
            z1 = pltpu.roll(ext, 1, 0)[HALO:]
            z2 = pltpu.roll(ext, 2, 0)[HALO:]
            w0, w1, w2 = cw_ref[0:1, sl], cw_ref[1:2, sl], cw_ref[2:3, sl]
            conv = w0 * z2 + w1 * z1 + w2 * zc
            gd = gd_ref[:, sl].astype(F32)
            sg = _sig(gd)
            sgd = gd * sg
            db = db_ref[:, sl].astype(F32)
            dyd = dy_ref[:, sd].astype(F32)
            dconv = dyd * db * sgd
            gdn = gdn_ref[:, sl].astype(F32)
            dconv_n = jnp.where(i < n_tiles - 1,
                                dyn_ref[:, sl].astype(F32) * dbn_ref[:, sl].astype(F32) * (gdn * _sig(gdn)), 0.0)
            extn = jnp.concatenate([dconv, dconv_n], axis=0)
            dzc = w2 * dconv + w1 * pltpu.roll(extn, nrow - 1, 0)[:TM] + w0 * pltpu.roll(extn, nrow - 2, 0)[:TM]
            stage[:, sd] = (dyd * conv * sgd).astype(BF16)
            stage[:, 2 * D_MODEL + h * HEAD_DIM:2 * D_MODEL + (h + 1) * HEAD_DIM] = (dzc * dx).astype(BF16)
            stage[:, 3 * D_MODEL + h * HEAD_DIM:3 * D_MODEL + (h + 1) * HEAD_DIM] = (dzc * dc).astype(BF16)
            stage[:, 4 * D_MODEL + h * HEAD_DIM:4 * D_MODEL + (h + 1) * HEAD_DIM] = (
                dyd * db * conv * (sg * (1.0 + gd * (1.0 - sg)))).astype(BF16)
            for tap, shifted in enumerate((z2, z1, zc)):
                dcw_ref[tap:tap + 1, sl] += jnp.sum(dconv * shifted, axis=0, keepdims=True)
        _to_classes(dyc4_ref, scr_o, 4, BF16)
        _to_classes(dyc16_ref, scr_o, 16, BF16)
        dd_ref[...] = stat
        _put(scr_l, stat)
        _to_classes(dd4_ref, scr_l, 4, F32)
        _to_classes(dd16_ref, scr_l, 16, F32)

        out.start()

        @pl.when(i == n_tiles - 1)
        def _():
            out.wait()

    row = pl.BlockSpec((TM, D_MODEL), lambda i: (i, 0))
    stat = pl.BlockSpec((TM, STAT_W), lambda i: (i, 0))
    dz, dyc, dyc4, dyc16, dd, dd4, dd16, g_conv = pl.pallas_call(
        body, name="odd_mix_bwd", grid=(n_tiles,),
        in_specs=[pl.BlockSpec((TM, 2 * D_MODEL), lambda i: (i, 0)), row, _zcol(9), _zcol(10), _zcol(11), _zcol(12), _zcol(13),
                  _prev_halo(11), _prev_halo(12), _next_halo(1, s), _next_halo(10, s), _next_halo(13, s), _full((3, D_MODEL))],
        out_specs=[ANY, row, _class_spec(4, TM, D_MODEL), _class_spec(16, TM, D_MODEL),
                   stat, _class_spec(4, TM, STAT_W), _class_spec(16, TM, STAT_W), _full((3, D_MODEL))],
        out_shape=[jax.ShapeDtypeStruct((s, ODD_IN), BF16), jax.ShapeDtypeStruct((s, D_MODEL), BF16),
                   jax.ShapeDtypeStruct((4, s // 4, D_MODEL), BF16), jax.ShapeDtypeStruct((16, s // 16, D_MODEL), BF16),
                   jax.ShapeDtypeStruct((s, STAT_W), F32), jax.ShapeDtypeStruct((4, s // 4, STAT_W), F32),
                   jax.ShapeDtypeStruct((16, s // 16, STAT_W), F32), jax.ShapeDtypeStruct((3, D_MODEL), F32)],
        scratch_shapes=[pltpu.VMEM((TM, rest), BF16), pltpu.SemaphoreType.DMA(()), _token_scratch(TM, D_MODEL),
                        _token_scratch(TM, STAT_W)],
        compiler_params=_cp(("arbitrary",), 48),
    )(dy, ycr, z, z, z, z, z, z, z, dy, z, z, cw)
    dyc = [dyc, dyc4.reshape(s, D_MODEL), dyc16.reshape(s, D_MODEL)]
    dd = [dd, dd4.reshape(s, STAT_W), dd16.reshape(s, STAT_W)]
    return dz, dyc, dd, g_conv


def _cols_of_order(order):
    if order == 0:
        return (lambda j: jnp.where(j < 3, 3 * j, j + 6)), 8
    return (lambda j: 3 * j + order), 3


class _Hooks:
    def before_even(self):
        return None

    def odd_weights(self, w, x1):
        return w

    def odd_grads_ready(self, g_w_in_o, g_w_out_o):
        return None

    def even_mix_done(self, dz_e):
        return None

    def backward_done(self, dx0):
        return None


def _local_step(x, target, w, hooks=_Hooks()):
    tril = jnp.tril(jnp.ones((CHUNK, CHUNK), bool))
    wt = jnp.where(tril[None], w["ws"], 0.0).astype(BF16)
    wtt = jnp.swapaxes(wt, 1, 2)
    bs = w["bs"].reshape(4, CHUNK, 1)

    h_e = _rms_fwd("rms_fwd_even", x, w["even_norm"], after=hooks.before_even())
    z_e = _mm_nn("even_in_proj", h_e, w["w_in_e"], MM_ROWS, 1280, BF16)
    y_e = _even_mix_fwd(z_e, w["pool_w"], w["pool_scale"], wt, bs)
    x1 = _mm_nn("even_out_proj", y_e, w["w_out_e"], MM_ROWS, 1024, F32, resid=x)
    w = hooks.odd_weights(w, x1)
    h_o = _rms_fwd_orders("rms_fwd_odd", x1, w["odd_norm"])
    z_o = None
    for o in range(3):
        cols, n_cols = _cols_of_order(o)
        z_o = _mm_nn(f"odd_in_proj_o{o}", h_o[o], w["w_in_o"], MM_ROWS, D_MODEL, BF16, col_map=cols, n_cols=n_cols,
                     into=z_o)
    att = [_attn_fwd(z_o, gi) for gi in range(3)]
    y_o, ycr, ltot = _odd_mix_fwd(z_o, [a[0] for a in att], [a[1] for a in att], w["conv_w"])
    x2 = _mm_nn("odd_out_proj", y_o, w["w_out_o"], MM_ROWS, 1024, F32, resid=x1)
    dx2, loss8, g_final = _final_loss(x2, w["final_norm"], target)

    g_w_out_o = _mm_tn("odd_out_proj_dw", y_o, dx2, 1024, DW_OUT_TOKENS)
    dy_o = _mm_nt("odd_out_proj_dy", dx2, w["w_out_o"], MM_ROWS, 1024, BF16)
    dz_o, dyc, dst, g_conv = _odd_mix_bwd(dy_o, z_o, ycr, w["conv_w"])
    for gi in range(3):
        dz_o = _attn_bwd(z_o, dyc[gi], ltot[gi], dst[gi], dz_o, gi)
    g_w_in_o, dh_o = None, []
    for o in range(3):
        cols, n_cols = _cols_of_order(o)
        g_w_in_o = _mm_tn(f"odd_in_proj_dw_o{o}", h_o[o], dz_o, D_MODEL, DW_IN_TOKENS, col_map=cols, n_cols=n_cols,
                          into=g_w_in_o)
        dh_o.append(_mm_nt(f"odd_in_proj_dh_o{o}", dz_o, w["w_in_o"], MM_ROWS, D_MODEL, F32, k_map=cols, nk=n_cols))
    dx1, g_odd_norm = _rms_bwd("rms_bwd_odd", dh_o[0], x1, w["odd_norm"], dx2, dh4=dh_o[1], dh16=dh_o[2])
    after = hooks.odd_grads_ready(g_w_in_o, g_w_out_o)
    g_w_out_e = _mm_tn("even_out_proj_dw", y_e, dx1, 1024, DW_OUT_TOKENS, after=after)
    dy_e = _mm_nt("even_out_proj_dy", dx1, w["w_out_e"], MM_ROWS, 1024, BF16)
    dz_e, g_pw, g_ps, g_ws, g_bs = _even_mix_bwd(dy_e, z_e, w["pool_w"], w["pool_scale"], wt, wtt, bs)
    after = hooks.even_mix_done(dz_e)
    g_w_in_e = _mm_tn("even_in_proj_dw", h_e, dz_e, 1280, DW_IN_TOKENS, after=after)
    dh_e = _mm_nt("even_in_proj_dh", dz_e, w["w_in_e"], MM_ROWS, 2560, F32)
    dx0, g_even_norm = _rms_bwd("rms_bwd_even", dh_e, x, w["even_norm"], dx1)
    hooks.backward_done(dx0)

    grads = dict(w_in_e=g_w_in_e, pool_w=g_pw, w_out_e=g_w_out_e, w_in_o=g_w_in_o, w_out_o=g_w_out_o,
                 even_norm=g_even_norm, pool_scale=g_ps, ws=g_ws, bs=g_bs[:, :4].T, final_norm=g_final,
                 odd_norm=g_odd_norm, conv_w=g_conv)
    return loss8[0, 0], dx0, grads


class _Big(NamedTuple):
    name: str
    full: tuple
    haxis: int
    kaxis: int
    sub: int


BIGS = (
    _Big("w_in_e", (1024, 5120), 0, 1, 2),
    _Big("pool_w", (4, 256, 256), 0, 1, 1),
    _Big("w_out_e", (2048, 1024), 1, 0, 1),
    _Big("w_in_o", (1024, 14336), 0, 1, 4),
    _Big("w_out_o", (2048, 1024), 1, 0, 1),
)
N_BIG = len(BIGS)


def _shape(b, half=False, shard=False):
    return tuple(n // (2 if (half and ax == b.haxis) else 1) // (4 if (shard and ax == b.kaxis) else 1)
                 for ax, n in enumerate(b.full))


def _at(ref, b, h=None, k=None):
    idx = []
    for ax, n in enumerate(b.full):
        if ax == b.haxis and h is not None:
            idx.append(pl.ds(h * (n // 2), n // 2))
        elif ax == b.kaxis and k is not None:
            idx.append(pl.ds(k * (n // 4), n // 4))
        else:
            idx.append(slice(None))
    return ref.at[tuple(idx)]


def _place():
    x, y, c = lax.axis_index("x"), lax.axis_index("y"), lax.axis_index("c")
    chips = [(1 - x, y), (x, 1 - y), (1 - x, 1 - y)]
    return x, y, c, 2 * x + y, chips, [2 * cx + cy for cx, cy in chips]


def _piece_shape(b):
    return (4, 2) + _shape(b, half=True, shard=True)


def _gather_weights(bigs, shards, tiny):
    nb = len(bigs)

    def body(*refs):
        ins, tiny_in = refs[:nb], refs[nb]
        outs, tiny_out = refs[nb + 1:2 * nb + 1], refs[2 * nb + 1]
        send, recv, loc = refs[2 * nb + 2:]
        x, y, c, k_me, chips, ks = _place()
        sib = (x, y, 1 - c)

        def rc(src, dst, sem, to):
            return pltpu.make_async_remote_copy(src_ref=src, dst_ref=dst, send_sem=send.at[sem], recv_sem=recv.at[sem],
                                                device_id=to, device_id_type=MESH)

        own = pltpu.make_async_copy(tiny_in, tiny_out.at[k_me], loc)
        own.start()
        sends = []
        for j, chip in enumerate(chips):
            for a, b in enumerate(bigs):
                sends.append(rc(_at(ins[a], b, h=c), outs[a].at[k_me, c], 6 * a + j, (*chip, c)))
            sends.append(rc(tiny_in, tiny_out.at[k_me], 6 * nb + j, (*chip, c)))
        for cp in sends:
            cp.start()
        for j in range(3):
            for a in range(nb):
                piece = outs[a].at[ks[j], c]
                rc(piece, piece, 6 * a + j, sib).wait_recv()
                fwd = rc(piece, piece, 6 * a + 3 + j, sib)
                fwd.start()
                sends.append(fwd)
            rc(tiny_in, tiny_out.at[ks[j]], 6 * nb + j, sib).wait_recv()
        for j in range(3):
            for a in range(nb):
                piece = outs[a].at[ks[j], 1 - c]
                rc(piece, piece, 6 * a + 3 + j, sib).wait_recv()
        for cp in sends:
            cp.wait_send()
        own.wait()

    n_sem = 6 * nb + 3
    return pl.pallas_call(
        body, name="gather_even_weights",
        in_specs=[ANY] * (nb + 1), out_specs=[ANY] * (nb + 1),
        out_shape=[jax.ShapeDtypeStruct(_piece_shape(b), BF16) for b in bigs]
        + [jax.ShapeDtypeStruct((4,) + tiny.shape, F32)],
        scratch_shapes=[pltpu.SemaphoreType.DMA((n_sem,)), pltpu.SemaphoreType.DMA((n_sem,)), pltpu.SemaphoreType.DMA(())],
    )(*shards, tiny)


def _assemble(b, pieces, shard, k_arr):
    blk = _blk(b)

    def body(k_ref, p_ref, s_ref, o_ref):
        mine = pl.program_id(0) == k_ref[0]

        @pl.when(mine)
        def _():
            o_ref[...] = s_ref[...]

        @pl.when(jnp.logical_not(mine))
        def _():
            o_ref[...] = p_ref[...]

    return pl.pallas_call(
        body, name=f"assemble_{b.name}",
        grid_spec=pltpu.PrefetchScalarGridSpec(
            num_scalar_prefetch=1, grid=(4, 2, b.sub),
            in_specs=[pl.BlockSpec((None, None) + blk, lambda k, h, st, k_ref: (k, h) + _bidx(b, 0, 0, st)),
                      pl.BlockSpec(blk, lambda k, h, st, k_ref: _bidx(b, h, 0, st))],
            out_specs=pl.BlockSpec(blk, lambda k, h, st, k_ref: _bidx(b, h, k, st))),
        out_shape=jax.ShapeDtypeStruct(b.full, BF16),
        compiler_params=_cp(("arbitrary", "arbitrary", "arbitrary"), 32),
    )(k_arr, pieces, shard)


def _copies_to_chips(bigs):
    def copies(srcs, lands, send, recv, waiting=False):
        _, _, c, k_me, chips, _ = _place()
        return [pltpu.make_async_remote_copy(
                    src_ref=_at(srcs[a], b, h=c), dst_ref=lands[a].at[k_me, c], send_sem=send.at[3 * a + j],
                    recv_sem=recv.at[3 * a + j], device_id=(*chips[j], c), device_id_type=MESH)
                for j in range(3) for a, b in enumerate(bigs)]
    return copies


def _copies_swap_halves(bigs):
    def copies(srcs, lands, send, recv, waiting=False):
        x, y, c, _, _, _ = _place()
        return [pltpu.make_async_remote_copy(
                    src_ref=_at(srcs[a], b, h=1 - c), dst_ref=lands[a], send_sem=send.at[a], recv_sem=recv.at[a],
                    device_id=(x, y, 1 - c), device_id_type=MESH)
                for a, b in enumerate(bigs)]
    return copies


def _copies_partials(bigs):
    def copies(srcs, lands, send, recv, waiting=False):
        _, _, c, _, chips, ks = _place()
        return [pltpu.make_async_remote_copy(
                    src_ref=_at(srcs[a], b, k=ks[j]), dst_ref=lands[a].at[j], send_sem=send.at[3 * a + j],
                    recv_sem=recv.at[3 * a + j], device_id=(*chips[j], c), device_id_type=MESH)
                for j in range(3) for a, b in enumerate(bigs)]
    return copies


def _exchange(name, srcs, land_shapes, copies_of, n_copies):
    ns = len(srcs)

    def body(*refs):
        copies = copies_of(refs[:ns], refs[ns:ns + len(land_shapes)], refs[-2], refs[-1])
        for cp in copies:
            cp.start()
        for cp in copies:
            cp.wait()

    return pl.pallas_call(
        body, name=name, in_specs=[ANY] * ns, out_specs=[ANY] * len(land_shapes), out_shape=land_shapes,
        scratch_shapes=[pltpu.SemaphoreType.DMA((n_copies,)), pltpu.SemaphoreType.DMA((n_copies,))],
    )(*srcs)


HBM = pl.BlockSpec(memory_space=pltpu.HBM)
SEM = pl.BlockSpec(memory_space=pltpu.SEMAPHORE)
SIDE_EFFECT = pltpu.SideEffectType.DATAFLOW_SIDE_EFFECTING


def _in_hbm(a):
    return pltpu.with_memory_space_constraint(a, pltpu.HBM)


def _exchange_start(name, srcs, land_shapes, copies_of, n_copies, after=None):
    ns, nl = len(srcs), len(land_shapes)
    lands = [lax.empty(sh.shape, sh.dtype) for sh in land_shapes]
    tail = [] if after is None else [after]
    n_in = ns + nl + len(tail)

    def body(*refs):
        send, recv, token = refs[n_in], refs[n_in + 1], refs[-1]
        for cp in copies_of(refs[:ns], refs[ns:ns + nl], send, recv):
            cp.start()
        token[...] = jnp.zeros_like(token)

    thru = [pltpu.HBM(a.shape, a.dtype) for a in (*srcs, *lands)]
    send, recv, *bufs, token = pl.pallas_call(
        body, name=name,
        out_shape=(pltpu.SemaphoreType.DMA((n_copies,)), pltpu.SemaphoreType.DMA((n_copies,)), *thru,
                   jax.ShapeDtypeStruct((8, 128), F32)),
        in_specs=[HBM] * (ns + nl) + [ANY] * len(tail),
        out_specs=(SEM, SEM, *([HBM] * (ns + nl)), pl.BlockSpec(memory_space=pltpu.VMEM)),
        input_output_aliases={i: 2 + i for i in range(ns + nl)},
        compiler_params=pltpu.CompilerParams(has_side_effects=SIDE_EFFECT),
    )(*[_in_hbm(a) for a in (*srcs, *lands)], *tail)
    return (send, recv, bufs, ns), token


def _exchange_wait(name, state, copies_of, after):
    send, recv, bufs, ns = state
    n = len(bufs)

    def body(*refs):
        ins = refs[:n]
        for cp in copies_of(ins[:ns], ins[ns:], refs[n], refs[n + 1], waiting=True):
            cp.wait_send()
            cp.wait_recv()

    out = pl.pallas_call(
        body, name=name, out_shape=tuple(pltpu.HBM(a.shape, a.dtype) for a in bufs),
        in_specs=[HBM] * n + [SEM, SEM, ANY], out_specs=tuple([HBM] * n),
        input_output_aliases={i: i for i in range(n)},
        compiler_params=pltpu.CompilerParams(has_side_effects=SIDE_EFFECT),
    )(*bufs, send, recv, after)
    return list(out[:ns]), list(out[ns:])


def _finish_gather(bigs, pieces):
    nb = len(bigs)

    def body(*refs):
        outs, send, recv = refs[nb:2 * nb], refs[2 * nb], refs[2 * nb + 1]
        x, y, c, _, _, ks = _place()
        fwd = [pltpu.make_async_remote_copy(
                   src_ref=outs[a].at[ks[j], c], dst_ref=outs[a].at[ks[j], c], send_sem=send.at[3 * a + j],
                   recv_sem=recv.at[3 * a + j], device_id=(x, y, 1 - c), device_id_type=MESH)
               for j in range(3) for a in range(nb)]
        for cp in fwd:
            cp.start()
        for cp in fwd:
            cp.wait()

    return pl.pallas_call(
        body, name="gather_odd_finish", in_specs=[ANY] * nb, out_specs=[ANY] * nb,
        out_shape=[jax.ShapeDtypeStruct(_piece_shape(b), BF16) for b in bigs],
        scratch_shapes=[pltpu.SemaphoreType.DMA((3 * nb,)), pltpu.SemaphoreType.DMA((3 * nb,))],
        input_output_aliases={a: a for a in range(nb)},
    )(*pieces)


def _blk(b):
    win = _shape(b, half=True, shard=True)
    return (win[0] // b.sub,) + win[1:]


def _bidx(b, h, k, st):
    idx = [0] * len(b.full)
    idx[b.haxis] = h
    idx[b.kaxis] = k
    idx[0] = idx[0] * b.sub + st
    return tuple(idx)


def _chip_sum(b, g, got, c_arr):
    blk = _blk(b)

    def body(c_ref, g_ref, r_ref, o_ref):
        del c_ref
        o_ref[...] = (g_ref[...] + r_ref[...]).astype(BF16)

    half = pl.BlockSpec(blk, lambda k, st, c_ref: _bidx(b, 0, k, st))
    return pl.pallas_call(
        body, name=f"rs_chip_sum_{b.name}",
        grid_spec=pltpu.PrefetchScalarGridSpec(
            num_scalar_prefetch=1, grid=(4, b.sub),
            in_specs=[pl.BlockSpec(blk, lambda k, st, c_ref: _bidx(b, c_ref[0], k, st)), half], out_specs=half),
        out_shape=jax.ShapeDtypeStruct(_shape(b, half=True), BF16),
        compiler_params=_cp(("arbitrary", "arbitrary"), 40),
    )(c_arr, g, got)


def _half_shapes(bigs):
    return [jax.ShapeDtypeStruct(_shape(b, half=True), F32) for b in bigs]


def _partial_shapes(bigs):
    return [jax.ShapeDtypeStruct((3,) + _shape(b, half=True, shard=True), BF16) for b in bigs]


def _shard_sum(b, mine, got, ck_arr):
    blk = _blk(b)

    def body(ck_ref, m_ref, r0, r1, r2, o_ref):
        del ck_ref
        o_ref[...] = (m_ref[...].astype(F32) + r0[...].astype(F32)) + (r1[...].astype(F32) + r2[...].astype(F32))

    def peer(j):
        return pl.BlockSpec((None,) + blk, lambda st, ck: (j,) + _bidx(b, 0, 0, st))

    return pl.pallas_call(
        body, name=f"rs_shard_sum_{b.name}",
        grid_spec=pltpu.PrefetchScalarGridSpec(
            num_scalar_prefetch=1, grid=(b.sub,),
            in_specs=[pl.BlockSpec(blk, lambda st, ck: _bidx(b, 0, ck[1], st)), peer(0), peer(1), peer(2)],
            out_specs=pl.BlockSpec(blk, lambda st, ck: _bidx(b, ck[0], 0, st))),
        out_shape=jax.ShapeDtypeStruct(_shape(b, shard=True), F32),
        compiler_params=_cp(("arbitrary",), 40),
    )(ck_arr, mine, got, got, got)


def _share_halves(gs):
    def body(*refs):
        ins, outs, send, recv = refs[:N_BIG], refs[N_BIG:2 * N_BIG], refs[2 * N_BIG], refs[2 * N_BIG + 1]
        del ins
        x, y, c, _, _, _ = _place()
        copies = [pltpu.make_async_remote_copy(src_ref=_at(outs[a], b, h=c), dst_ref=_at(outs[a], b, h=c),
                                               send_sem=send.at[a], recv_sem=recv.at[a], device_id=(x, y, 1 - c),
                                               device_id_type=MESH)
                  for a, b in enumerate(BIGS)]
        for cp in copies:
            cp.start()
        for cp in copies:
            cp.wait()

    return pl.pallas_call(
        body, name="rs_share_halves", in_specs=[ANY] * N_BIG, out_specs=[ANY] * N_BIG,
        out_shape=[jax.ShapeDtypeStruct(_shape(b, shard=True), F32) for b in BIGS],
        scratch_shapes=[pltpu.SemaphoreType.DMA((N_BIG,)), pltpu.SemaphoreType.DMA((N_BIG,))],
        input_output_aliases={a: a for a in range(N_BIG)},
    )(*gs)


def _gather_small(block):
    m_per, n = block.shape

    def body(x_ref, out_ref, send_sems, recv_sems, local_sem):
        x, y, c = lax.axis_index("x"), lax.axis_index("y"), lax.axis_index("c")
        me, sibling = (x, y, c), (x, y, 1 - c)
        chips = [(1 - x, y), (x, 1 - y), (1 - x, 1 - y)]

        def rows(px, py, pc):
            return out_ref.at[pl.ds((4 * px + 2 * py + pc) * m_per, m_per), :]

        def copy(k, blk, to, src=None):
            return pltpu.make_async_remote_copy(
                src_ref=rows(*blk) if src is None else src, dst_ref=rows(*blk), send_sem=send_sems.at[k],
                recv_sem=recv_sems.at[k], device_id=to, device_id_type=MESH)

        mine = pltpu.make_async_copy(x_ref, rows(*me), local_sem)
        mine.start()
        first = [copy(0, me, sibling, src=x_ref)]
        first += [copy(1 + j, me, (*chip, c), src=x_ref) for j, chip in enumerate(chips)]
        for cp in first:
            cp.start()
        passed = [copy(4 + j, (*chip, c), sibling) for j, chip in enumerate(chips)]
        for j, chip in enumerate(chips):
            copy(1 + j, (*chip, c), me).wait_recv()
            passed[j].start()
        copy(0, sibling, me).wait_recv()
        for j, chip in enumerate(chips):
            copy(4 + j, (*chip, 1 - c), me).wait_recv()
        for cp in first + passed:
            cp.wait_send()
        mine.wait()

    return pl.pallas_call(
        body, name="gather_small_grads",
        out_shape=jax.ShapeDtypeStruct((8 * m_per, n), block.dtype),
        in_specs=[pl.BlockSpec(memory_space=pltpu.VMEM)], out_specs=pl.BlockSpec(memory_space=pltpu.VMEM),
        scratch_shapes=[pltpu.SemaphoreType.DMA((7,)), pltpu.SemaphoreType.DMA((7,)), pltpu.SemaphoreType.DMA],
    )(block)


def _sum_small(stack):
    _, m_per, n = stack.shape

    def body(x_ref, o_ref):
        acc = x_ref[0]
        for dev in range(1, 8):
            acc = acc + x_ref[dev]
        o_ref[...] = acc

    return pl.pallas_call(body, name="sum_small_grads", out_shape=jax.ShapeDtypeStruct((m_per, n), F32))(stack)


def _adamw(name, w, g, m, v, rows):
    shape = w.shape

    def body(w_ref, g_ref, m_ref, v_ref, d_ref, mo_ref, vo_ref):
        gg = g_ref[...]
        mn = ADAM_B1 * m_ref[...] + (1.0 - ADAM_B1) * gg
        vn = ADAM_B2 * v_ref[...] + (1.0 - ADAM_B2) * (gg * gg)
        m_hat = mn / (1.0 - ADAM_B1 ** ADAM_STEP)
        v_hat = vn / (1.0 - ADAM_B2 ** ADAM_STEP)
        d_ref[...] = -ADAM_LR * (m_hat / (jnp.sqrt(v_hat) + ADAM_EPS) + ADAM_WD * w_ref[...])
        mo_ref[...] = mn
        vo_ref[...] = vn

    spec = pl.BlockSpec((rows,) + shape[1:], lambda i: (i,) + (0,) * (len(shape) - 1))
    return pl.pallas_call(
        body, name=name, grid=(shape[0] // rows,), in_specs=[spec] * 4, out_specs=[spec] * 3,
        out_shape=[jax.ShapeDtypeStruct(shape, F32)] * 3, compiler_params=_cp(("parallel",), 48),
    )(w, g, m, v)


ADAM_ROWS = dict(w_in_e=256, pool_w=4, w_out_e=256, w_in_o=128, w_out_o=256)


def _pack(parts, rows):
    flat = jnp.concatenate([p.reshape(-1).astype(F32) for p in parts])
    return jnp.pad(flat, (0, rows * 128 - flat.shape[0])).reshape(rows, 128)


def _unpack(buf, shapes):
    flat = buf.reshape(-1)
    out, off = [], 0
    for shp in shapes:
        n = 1
        for dim in shp:
            n *= dim
        out.append(flat[off:off + n].reshape(shp))
        off += n
    return out


WEIGHTS = ("even_norm", "even_w_in", "even_pool_w", "even_pool_scale", "even_ws", "even_bs", "even_w_out", "odd_norm",
           "odd_w_in", "odd_conv_w", "odd_w_out", "final_norm")
BIG_OF = dict(w_in_e="even_w_in", pool_w="even_pool_w", w_out_e="even_w_out", w_in_o="odd_w_in", w_out_o="odd_w_out")
SMALL = ("even_norm", "even_pool_scale", "even_ws", "even_bs", "final_norm", "odd_norm", "odd_conv_w")
SMALL_GRAD_ROWS = 576
SMALL_STATE_ROWS = 552


def kernel(x, even_norm, even_w_in, even_pool_w, even_pool_scale, even_ws, even_bs, even_w_out, odd_norm, odd_w_in, odd_conv_w, odd_w_out, final_norm, loss_target, m_even_norm, m_even_w_in, m_even_pool_w, m_even_pool_scale, m_even_ws, m_even_bs, m_even_w_out, m_odd_norm, m_odd_w_in, m_odd_conv_w, m_odd_w_out, m_final_norm, v_even_norm, v_even_w_in, v_even_pool_w, v_even_pool_scale, v_even_ws, v_even_bs, v_even_w_out, v_odd_norm, v_odd_w_in, v_odd_conv_w, v_odd_w_out, v_final_norm):
    wv = dict(zip(WEIGHTS, (even_norm, even_w_in, even_pool_w, even_pool_scale, even_ws, even_bs, even_w_out, odd_norm,
                            odd_w_in, odd_conv_w, odd_w_out, final_norm)))
    mv = dict(zip(WEIGHTS, (m_even_norm, m_even_w_in, m_even_pool_w, m_even_pool_scale, m_even_ws, m_even_bs,
                            m_even_w_out, m_odd_norm, m_odd_w_in, m_odd_conv_w, m_odd_w_out, m_final_norm)))
    vv = dict(zip(WEIGHTS, (v_even_norm, v_even_w_in, v_even_pool_w, v_even_pool_scale, v_even_ws, v_even_bs,
                            v_even_w_out, v_odd_norm, v_odd_w_in, v_odd_conv_w, v_odd_w_out, v_final_norm)))
    c = lax.axis_index("c")
    k_me = 2 * lax.axis_index("x") + lax.axis_index("y")

    c_arr = jnp.reshape(c, (1,)).astype(jnp.int32)
    ck_arr = jnp.stack([c, k_me]).astype(jnp.int32)
    even_bigs, odd_bigs = BIGS[:3], BIGS[3:]

    shards = {b.name: wv[BIG_OF[b.name]][0].astype(BF16) for b in BIGS}
    tiny = jnp.concatenate([odd_conv_w[0], odd_norm], axis=0)
    *pieces_even, tiny_all = _gather_weights(even_bigs, [shards[b.name] for b in even_bigs], tiny)
    tiny_full = jnp.transpose(tiny_all, (1, 0, 2)).reshape(4, D_MODEL)
    to_chips, swap_odd, partials_odd = _copies_to_chips(odd_bigs), _copies_swap_halves(odd_bigs), _copies_partials(odd_bigs)
    gather_state, gather_token = _exchange_start(
        "gather_odd_start", [shards[b.name] for b in odd_bigs],
        [jax.ShapeDtypeStruct(_piece_shape(b), BF16) for b in odd_bigs], to_chips, 3 * len(odd_bigs), after=pieces_even[-1])
    k_arr = jnp.reshape(k_me, (1,)).astype(jnp.int32)
    w = {b.name: _assemble(b, p, shards[b.name], k_arr) for b, p in zip(even_bigs, pieces_even)}
    w.update(even_norm=even_norm, pool_scale=even_pool_scale, ws=even_ws[0], bs=even_bs[0],
             final_norm=final_norm.reshape(1, D_MODEL), conv_w=tiny_full[:3], odd_norm=tiny_full[3:4])

    class Hooks(_Hooks):
        def before_even(self):
            return gather_token

        def odd_weights(self, w, x1):
            srcs, lands = _exchange_wait("gather_odd_wait", gather_state, to_chips, after=x1)
            pieces = _finish_gather(odd_bigs, lands)
            return dict(w, **{b.name: _assemble(b, p, s, k_arr) for b, p, s in zip(odd_bigs, pieces, srcs)})

        def odd_grads_ready(self, g_w_in_o, g_w_out_o):
            self.swap, token = _exchange_start("rs_odd_swap_start", [g_w_in_o, g_w_out_o], _half_shapes(odd_bigs),
                                               swap_odd, len(odd_bigs))
            return token

        def even_mix_done(self, dz_e):
            grads, got = _exchange_wait("rs_odd_swap_wait", self.swap, swap_odd, after=dz_e)
            sums = [_chip_sum(b, g, r, c_arr) for b, g, r in zip(odd_bigs, grads, got)]
            self.partials, token = _exchange_start("rs_odd_partials_start", sums, _partial_shapes(odd_bigs), partials_odd,
                                                   3 * len(odd_bigs))
            return token

        def backward_done(self, dx0):
            self.sums, self.parts = _exchange_wait("rs_odd_partials_wait", self.partials, partials_odd, after=dx0)

    hooks = Hooks()
    loss, dx, g = _local_step(x[0], loss_target[0], w, hooks)
    loss = lax.psum(loss, ("x", "y", "c"))

    got = _exchange("rs_even_swap", [g[b.name] for b in even_bigs], _half_shapes(even_bigs), _copies_swap_halves(even_bigs),
                    len(even_bigs))
    sums = [_chip_sum(b, g[b.name], r, c_arr) for b, r in zip(even_bigs, got)]
    parts = _exchange("rs_even_partials", sums, _partial_shapes(even_bigs), _copies_partials(even_bigs), 3 * len(even_bigs))
    halves = [_shard_sum(b, sm, p, ck_arr) for b, sm, p in zip(BIGS, sums + hooks.sums, list(parts) + hooks.parts)]
    g_shard = dict(zip((b.name for b in BIGS), _share_halves(halves)))

    small_g = _pack([g["even_norm"], g["pool_scale"], g["ws"], g["bs"], g["final_norm"], g["odd_norm"], g["conv_w"]],
                    SMALL_GRAD_ROWS)
    small_g = _sum_small(_gather_small(small_g).reshape(8, SMALL_GRAD_ROWS, 128))
    g_en, g_ps, g_ws, g_bs, g_fn, g_on, g_cw = _unpack(
        small_g, [(1, D_MODEL), (1, D_MODEL), (1, 4, CHUNK, CHUNK), (1, 4, CHUNK), (D_MODEL,), (1, D_MODEL), (1, 3, D_MODEL)])
    g_on = lax.dynamic_slice(g_on, (0, k_me * 256), (1, 256))
    g_cw = lax.dynamic_slice(g_cw, (0, 0, k_me * 256), (1, 3, 256))
    grad = dict(even_norm=g_en, even_pool_scale=g_ps, even_ws=g_ws, even_bs=g_bs, final_norm=g_fn, odd_norm=g_on,
                odd_conv_w=g_cw)
    for b in BIGS:
        grad[BIG_OF[b.name]] = g_shard[b.name][None]

    delta, new_m, new_v = {}, {}, {}
    for b in BIGS:
        n = BIG_OF[b.name]
        d_, m_, v_ = _adamw(f"adamw_{b.name}", wv[n][0], g_shard[b.name], mv[n][0], vv[n][0], ADAM_ROWS[b.name])
        delta[n], new_m[n], new_v[n] = d_[None], m_[None], v_[None]
    shapes = [wv[n].shape for n in SMALL]
    packed = [_pack([src[n] for n in SMALL], SMALL_STATE_ROWS) for src in (wv, grad, mv, vv)]
    outs = _adamw("adamw_small", *packed, SMALL_STATE_ROWS)
    for dst, buf in zip((delta, new_m, new_v), outs):
        for n, arr in zip(SMALL, _unpack(buf, shapes)):
            dst[n] = arr

    return (loss, dx[None], *[grad[n] for n in WEIGHTS], *[delta[n] for n in WEIGHTS], *[new_m[n] for n in WEIGHTS],
            *[new_v[n] for n in WEIGHTS])
```

```python
from typing import NamedTuple

import jax
import jax.numpy as jnp
from jax import lax
from jax.experimental import pallas as pl
from jax.experimental.pallas import tpu as pltpu

F32, BF16 = jnp.float32, jnp.bfloat16

D_MODEL = 1024
EPS = 1e-6
NEG = -1e30
POOL_SIZES = (2, 4, 8, 16)
GROUP_W = 256
CHUNK = 128
DILATIONS = (1, 4, 16)
N_SLOTS = 8
HEAD_DIM = 128
ATTN_BLOCK = 128
SCALE = HEAD_DIM ** -0.5
EVEN_IN = 5120
ODD_IN = 14336
QKV_BLOCKS = 9
ODD_BLOCKS = ODD_IN // D_MODEL
SLOPES = tuple(tuple(2.0 ** (-8.0 * (g * N_SLOTS + s + 1) / (3 * N_SLOTS)) for s in range(N_SLOTS)) for g in range(3))

ADAM_LR, ADAM_B1, ADAM_B2, ADAM_EPS, ADAM_WD, ADAM_STEP = 0.001, 0.9, 0.999, 1e-08, 0.01, 10

HALO = 16
TS = 512
TM = 256
MM_ROWS = 1024
DW_IN_TOKENS = 2048
DW_OUT_TOKENS = 1024
MIB = 1 << 20
MESH = pl.DeviceIdType.MESH
ANY = pl.BlockSpec(memory_space=pl.ANY)


def _cp(sem, vmem_mib):
    return pltpu.CompilerParams(dimension_semantics=sem, vmem_limit_bytes=vmem_mib * MIB)


def _sig(x):
    return 0.5 * jnp.tanh(0.5 * x) + 0.5


def _win_sum(e, w, forward):
    n = e.shape[0]
    k = 1
    while k < w:
        e = e + pltpu.roll(e, (n - k) if forward else k, 0)
        k *= 2
    return e


def _mm_nn(name, a, b, tm, tn, out_dtype, resid=None, col_map=None, n_cols=None, into=None):
    m, k = a.shape
    n = b.shape[1]
    if col_map is None:
        col_map, n_cols = (lambda j: j), n // tn

    def body(*refs):
        a_ref, b_ref = refs[0], refs[1]
        acc = jnp.dot(a_ref[...].astype(BF16), b_ref[...], preferred_element_type=F32)
        if resid is not None:
            acc = acc + refs[2][...]
        o_ref = refs[-1]
        o_ref[...] = acc.astype(out_dtype)

    in_specs = [pl.BlockSpec((tm, k), lambda j, i: (i, 0)), pl.BlockSpec((k, tn), lambda j, i: (0, col_map(j)))]
    args = [a, b]
    if resid is not None:
        in_specs.append(pl.BlockSpec((tm, tn), lambda j, i: (i, col_map(j))))
        args.append(resid)
    aliases = {}
    if into is not None:
        aliases = {len(args): 0}
        in_specs.append(ANY)
        args.append(into)
    return pl.pallas_call(
        body, name=name, grid=(n_cols, m // tm), in_specs=in_specs,
        out_specs=pl.BlockSpec((tm, tn), lambda j, i: (i, col_map(j))),
        out_shape=jax.ShapeDtypeStruct((m, n), out_dtype), input_output_aliases=aliases,
        compiler_params=_cp(("parallel", "parallel"), 48),
    )(*args)


def _mm_nt(name, a, b, tm, tk, out_dtype, k_map=None, nk=None):
    m, k = a.shape
    n = b.shape[0]
    if k_map is None:
        k_map, nk = (lambda kk: kk), k // tk

    def body(a_ref, b_ref, o_ref, acc_ref):
        kk = pl.program_id(1)
        p = lax.dot_general(a_ref[...].astype(BF16), b_ref[...], (((1,), (1,)), ((), ())), preferred_element_type=F32)
        if nk == 1:
            o_ref[...] = p.astype(out_dtype)
        else:
            @pl.when(kk == 0)
            def _():
                acc_ref[...] = p

            @pl.when(kk > 0)
            def _():
                acc_ref[...] += p

            @pl.when(kk == nk - 1)
            def _():
                o_ref[...] = acc_ref[...].astype(out_dtype)

    return pl.pallas_call(
        body, name=name, grid=(m // tm, nk),
        in_specs=[pl.BlockSpec((tm, tk), lambda i, kk: (i, k_map(kk))), pl.BlockSpec((n, tk), lambda i, kk: (0, k_map(kk)))],
        out_specs=pl.BlockSpec((tm, n), lambda i, kk: (i, 0)),
        out_shape=jax.ShapeDtypeStruct((m, n), out_dtype),
        scratch_shapes=[pltpu.VMEM((tm, n) if nk > 1 else (8, 128), F32)],
        compiler_params=_cp(("parallel", "arbitrary"), 56),
    )(a, b)


def _mm_tn(name, a, g, tn, ts, col_map=None, n_cols=None, into=None, after=None):
    s, ka = a.shape
    n = g.shape[1]
    if col_map is None:
        col_map, n_cols = (lambda j: j), n // tn

    def body(a_ref, g_ref, *rest):
        o_ref = rest[-1]
        st = pl.program_id(1)
        p = lax.dot_general(a_ref[...], g_ref[...].astype(BF16), (((0,), (0,)), ((), ())), preferred_element_type=F32)

        @pl.when(st == 0)
        def _():
            o_ref[...] = p

        @pl.when(st > 0)
        def _():
            o_ref[...] += p

    in_specs = [pl.BlockSpec((ts, ka), lambda j, st: (st, 0)), pl.BlockSpec((ts, tn), lambda j, st: (st, col_map(j)))]
    args = [a, g]
    aliases = {}
    if into is not None:
        aliases = {2: 0}
        in_specs.append(ANY)
        args.append(into)
    if after is not None:
        in_specs.append(ANY)
        args.append(after)
    return pl.pallas_call(
        body, name=name, grid=(n_cols, s // ts), in_specs=in_specs,
        out_specs=pl.BlockSpec((ka, tn), lambda j, st: (0, col_map(j))),
        out_shape=jax.ShapeDtypeStruct((ka, n), F32), input_output_aliases=aliases,
        compiler_params=_cp(("parallel", "arbitrary"), 56),
    )(*args)


def _rms_fwd(name, x, g, after=None):
    s = x.shape[0]

    def body(x_ref, g_ref, *rest):
        xf = x_ref[...]
        r = lax.rsqrt(jnp.mean(xf * xf, axis=-1, keepdims=True) + EPS)
        rest[-1][...] = (xf * r * g_ref[...]).astype(BF16)

    row = pl.BlockSpec((TS, D_MODEL), lambda i: (i, 0))
    in_specs, args = [row, pl.BlockSpec((1, D_MODEL), lambda i: (0, 0))], [x, g]
    if after is not None:
        in_specs.append(ANY)
        args.append(after)
    return pl.pallas_call(
        body, name=name, grid=(s // TS,), in_specs=in_specs, out_specs=row,
        out_shape=jax.ShapeDtypeStruct((s, D_MODEL), BF16), compiler_params=_cp(("parallel",), 32),
    )(*args)


def _class_major(a, d):
    return a.reshape(d, a.shape[0] // d, a.shape[1])


def _class_spec(d, tile, width):
    return pl.BlockSpec((d, tile // d, width), lambda i: (0, i, 0))


LANES = 128


def _token_scratch(tile, width):
    return pltpu.VMEM((width // LANES, tile, LANES), F32)


def _put(scr, val):
    for c in range(scr.shape[0]):
        scr[c] = val[:, c * LANES:(c + 1) * LANES]


def _get(scr):
    return jnp.concatenate([scr[c] for c in range(scr.shape[0])], axis=1)


def _to_classes(ref3, scr, d, dtype):
    n = ref3.shape[1]
    for c in range(scr.shape[0]):
        for r in range(d):
            ref3[r, :, c * LANES:(c + 1) * LANES] = scr.at[c][pl.ds(r, n, stride=d), :].astype(dtype)


def _from_classes(scr, ref3, d):
    n = ref3.shape[1]
    for c in range(scr.shape[0]):
        for r in range(d):
            scr.at[c][pl.ds(r, n, stride=d), :] = ref3[r, :, c * LANES:(c + 1) * LANES].astype(F32)


def _rms_fwd_orders(name, x, g):
    s = x.shape[0]

    def body(x_ref, g_ref, h_ref, h4_ref, h16_ref, scr):
        xf = x_ref[...]
        r = lax.rsqrt(jnp.mean(xf * xf, axis=-1, keepdims=True) + EPS)
        h = xf * r * g_ref[...]
        h_ref[...] = h.astype(BF16)
        _put(scr, h)
        _to_classes(h4_ref, scr, 4, BF16)
        _to_classes(h16_ref, scr, 16, BF16)

    row = pl.BlockSpec((TS, D_MODEL), lambda i: (i, 0))
    h, h4, h16 = pl.pallas_call(
        body, name=name, grid=(s // TS,), in_specs=[row, pl.BlockSpec((1, D_MODEL), lambda i: (0, 0))],
        out_specs=[row, _class_spec(4, TS, D_MODEL), _class_spec(16, TS, D_MODEL)],
        out_shape=[jax.ShapeDtypeStruct((s, D_MODEL), BF16), jax.ShapeDtypeStruct((4, s // 4, D_MODEL), BF16),
                   jax.ShapeDtypeStruct((16, s // 16, D_MODEL), BF16)],
        scratch_shapes=[_token_scratch(TS, D_MODEL)],
        compiler_params=_cp(("parallel",), 32),
    )(x, g)
    return h, h4.reshape(s, D_MODEL), h16.reshape(s, D_MODEL)


def _rms_bwd(name, dh, x, g, dres, dh4=None, dh16=None):
    s = x.shape[0]
    extra = dh4 is not None

    def body(dh_ref, x_ref, g_ref, dres_ref, *rest):
        if extra:
            dh4_ref, dh16_ref, dx_ref, dg_ref, scr = rest
        else:
            dx_ref, dg_ref = rest
        xf = x_ref[...]
        r = lax.rsqrt(jnp.mean(xf * xf, axis=-1, keepdims=True) + EPS)
        xh = xf * r
        dhf = dh_ref[...].astype(F32)
        if extra:
            _from_classes(scr, dh4_ref, 4)
            dhf = dhf + _get(scr)
            _from_classes(scr, dh16_ref, 16)
            dhf = dhf + _get(scr)
        dxh = dhf * g_ref[...]
        dx_ref[...] = dres_ref[...] + r * (dxh - xh * jnp.mean(dxh * xh, axis=-1, keepdims=True))
        part = jnp.sum(dhf * xh, axis=0, keepdims=True)

        @pl.when(pl.program_id(0) == 0)
        def _():
            dg_ref[...] = part

        @pl.when(pl.program_id(0) > 0)
        def _():
            dg_ref[...] += part

    row = pl.BlockSpec((TS, D_MODEL), lambda i: (i, 0))
    vec = pl.BlockSpec((1, D_MODEL), lambda i: (0, 0))
    in_specs, args, scratch = [row, row, vec, row], [dh, x, g, dres], []
    if extra:
        in_specs += [_class_spec(4, TS, D_MODEL), _class_spec(16, TS, D_MODEL)]
        args += [_class_major(dh4, 4), _class_major(dh16, 16)]
        scratch = [_token_scratch(TS, D_MODEL)]
    return pl.pallas_call(
        body, name=name, grid=(s // TS,), in_specs=in_specs, out_specs=[row, vec],
        out_shape=[jax.ShapeDtypeStruct((s, D_MODEL), F32), jax.ShapeDtypeStruct((1, D_MODEL), F32)],
        scratch_shapes=scratch, compiler_params=_cp(("arbitrary",), 40),
    )(*args)


def _out_proj_loss(y, w_out, resid, g, target):
    s, k = y.shape

    def body(y_ref, w_ref, r_ref, g_ref, t_ref, dx_ref, loss_ref, dg_ref):
        xf = jnp.dot(y_ref[...], w_ref[...], preferred_element_type=F32) + r_ref[...]
        gg = g_ref[...]
        r = lax.rsqrt(jnp.mean(xf * xf, axis=-1, keepdims=True) + EPS)
        xh = xf * r
        e = xh * gg - t_ref[...]
        dy = e * (1.0 / D_MODEL)
        dxh = dy * gg
        dx_ref[...] = r * (dxh - xh * jnp.mean(dxh * xh, axis=-1, keepdims=True))
        lpart = 0.5 * jnp.sum(jnp.mean(e * e, axis=-1, keepdims=True), axis=0, keepdims=True)
        lpart = jnp.broadcast_to(lpart, (8, 128))
        gpart = jnp.sum(dy * xh, axis=0, keepdims=True)

        @pl.when(pl.program_id(0) == 0)
        def _():
            loss_ref[...] = lpart
            dg_ref[...] = gpart

        @pl.when(pl.program_id(0) > 0)
        def _():
            loss_ref[...] += lpart
            dg_ref[...] += gpart

    row = pl.BlockSpec((TS, D_MODEL), lambda i: (i, 0))
    vec = pl.BlockSpec((1, D_MODEL), lambda i: (0, 0))
    return pl.pallas_call(
        body, name="odd_out_proj_loss", grid=(s // TS,),
        in_specs=[pl.BlockSpec((TS, k), lambda i: (i, 0)), pl.BlockSpec((k, D_MODEL), lambda i: (0, 0)), row, vec, row],
        out_specs=[row, pl.BlockSpec((8, 128), lambda i: (0, 0)), vec],
        out_shape=[jax.ShapeDtypeStruct((s, D_MODEL), F32), jax.ShapeDtypeStruct((8, 128), F32),
                   jax.ShapeDtypeStruct((1, D_MODEL), F32)],
        compiler_params=_cp(("arbitrary",), 48),
    )(y, w_out, resid, g, target)


def _zcol(c, tm=TM):
    return pl.BlockSpec((tm, D_MODEL), lambda i, c=c: (i, c))


def _prev_halo(c, tm=TM):
    return pl.BlockSpec((HALO, D_MODEL), lambda i, c=c: (jnp.maximum(i * (tm // HALO) - 1, 0), c))


def _next_halo(c, n_rows, tm=TM):
    last = n_rows // HALO - 1
    return pl.BlockSpec((HALO, D_MODEL), lambda i, c=c: (jnp.minimum((i + 1) * (tm // HALO), last), c))


def _full(shape):
    return pl.BlockSpec(shape, lambda i: (0,) * len(shape))


def _inv_count(first_row, n, w):
    t = first_row + lax.broadcasted_iota(jnp.int32, (n, 1), 0)
    return 1.0 / jnp.minimum(t + 1, w).astype(F32)


def _even_mix_fwd(z, pw, ps, wt, bs):
    s = z.shape[0]

    def body(a_ref, ga_ref, u_ref, v_ref, gb_ref, halo_ref, pw_ref, ps_ref, wt_ref, bs_ref, y_ref):
        i = pl.program_id(0)
        a = a_ref[...].astype(F32)
        halo = jnp.where(i > 0, halo_ref[...].astype(F32), 0.0)
        ext = jnp.concatenate([halo, a], axis=0)
        ga = ga_ref[...].astype(F32)
        sga = ga * _sig(ga)
        for g, w in enumerate(POOL_SIZES):
            cs = slice(g * GROUP_W, (g + 1) * GROUP_W)
            win = _win_sum(ext[:, cs], w, False)[HALO:]
            pooled = win * _inv_count(i * TM, TM, w) - a[:, cs]
            mixed = jnp.dot(pooled.astype(BF16), pw_ref[g], preferred_element_type=F32)
            y_ref[:, cs] = (mixed * ps_ref[:, cs] * sga[:, cs]).astype(BF16)
        gb = gb_ref[...].astype(F32)
        gate = u_ref[...].astype(F32) * (gb * _sig(gb))
        for ch in range(TM // CHUNK):
            rs = slice(ch * CHUNK, (ch + 1) * CHUNK)
            for g in range(4):
                cs = slice(g * GROUP_W, (g + 1) * GROUP_W)
                mixb = jnp.dot(wt_ref[g], v_ref[rs, cs], preferred_element_type=F32) + bs_ref[g]
                y_ref[rs, D_MODEL + g * GROUP_W:D_MODEL + (g + 1) * GROUP_W] = (gate[rs, cs] * mixb).astype(BF16)

    return pl.pallas_call(
        body, name="even_mix_fwd", grid=(s // TM,),
        in_specs=[_zcol(0), _zcol(1), _zcol(2), _zcol(3), _zcol(4), _prev_halo(0),
                  _full((4, GROUP_W, GROUP_W)), _full((1, D_MODEL)), _full((4, CHUNK, CHUNK)), _full((4, CHUNK, 1))],
        out_specs=pl.BlockSpec((TM, 2 * D_MODEL), lambda i: (i, 0)),
        out_shape=jax.ShapeDtypeStruct((s, 2 * D_MODEL), BF16),
        compiler_params=_cp(("parallel",), 48),
    )(z, z, z, z, z, z, pw, ps, wt, bs)


def _even_mix_bwd(dy, z, pw, ps, wt, wtt, bs, after=None):
    s = z.shape[0]
    n_tiles = s // TM
    tail_specs, tail_args = ([ANY], [after]) if after is not None else ([], [])

    def body(dy_ref, a_ref, ga_ref, u_ref, v_ref, gb_ref, halo_ref, dyn_ref, gan_ref, pw_ref, ps_ref, wt_ref, wtt_ref,
             bs_ref, *rest):
        dz_ref, dpw_ref, dps_ref, dws_ref, dbs_ref = rest[-5:]
        i = pl.program_id(0)

        @pl.when(i == 0)
        def _():
            dpw_ref[...] = jnp.zeros_like(dpw_ref)
            dps_ref[...] = jnp.zeros_like(dps_ref)
            dws_ref[...] = jnp.zeros_like(dws_ref)
            dbs_ref[...] = jnp.zeros_like(dbs_ref)

        a = a_ref[...].astype(F32)
        halo = jnp.where(i > 0, halo_ref[...].astype(F32), 0.0)
        ext = jnp.concatenate([halo, a], axis=0)
        ga = ga_ref[...].astype(F32)
        sg = _sig(ga)
        sga = ga * sg
        dsga = sg * (1.0 + ga * (1.0 - sg))
        dya = dy_ref[:, :D_MODEL].astype(F32)
        gan = gan_ref[...].astype(F32)
        dmn_all = jnp.where(i < n_tiles - 1, dyn_ref[...].astype(F32) * ps_ref[...] * (gan * _sig(gan)), 0.0)
        for g, w in enumerate(POOL_SIZES):
            cs = slice(g * GROUP_W, (g + 1) * GROUP_W)
            inv = _inv_count(i * TM, TM, w)
            pooled = _win_sum(ext[:, cs], w, False)[HALO:] * inv - a[:, cs]
            pb = pooled.astype(BF16)
            mixed = jnp.dot(pb, pw_ref[g], preferred_element_type=F32)
            dyg = dya[:, cs]
            psg = ps_ref[:, cs]
            dm = (dyg * psg * sga[:, cs]).astype(BF16)
            dz_ref[:, D_MODEL + g * GROUP_W:D_MODEL + (g + 1) * GROUP_W] = (dyg * mixed * psg * dsga[:, cs]).astype(BF16)
            dps_ref[:, cs] += jnp.sum(dyg * mixed * sga[:, cs], axis=0, keepdims=True)
            dpw_ref[g] += lax.dot_general(pb, dm, (((0,), (0,)), ((), ())), preferred_element_type=F32)
            nt = (((1,), (1,)), ((), ()))
            dpool = lax.dot_general(dm, pw_ref[g], nt, preferred_element_type=F32)
            dpool_n = lax.dot_general(dmn_all[:, cs].astype(BF16), pw_ref[g], nt, preferred_element_type=F32)
            e = jnp.concatenate([dpool * inv, dpool_n * _inv_count((i + 1) * TM, HALO, w)], axis=0)
            dz_ref[:, cs] = (_win_sum(e, w, True)[:TM] - dpool).astype(BF16)

        gb = gb_ref[...].astype(F32)
        sg = _sig(gb)
        sgb = gb * sg
        dsgb = sg * (1.0 + gb * (1.0 - sg))
        u = u_ref[...].astype(F32)
        dyb = dy_ref[:, D_MODEL:].astype(F32)
        tril = lax.broadcasted_iota(jnp.int32, (CHUNK, CHUNK), 0) >= lax.broadcasted_iota(jnp.int32, (CHUNK, CHUNK), 1)
        lane = lax.broadcasted_iota(jnp.int32, (CHUNK, 128), 1)
        for ch in range(TM // CHUNK):
            rs = slice(ch * CHUNK, (ch + 1) * CHUNK)
            for g in range(4):
                cs = slice(g * GROUP_W, (g + 1) * GROUP_W)
                vb = v_ref[rs, cs]
                mixb = jnp.dot(wt_ref[g], vb, preferred_element_type=F32) + bs_ref[g]
                dyu = dyb[rs, cs] * u[rs, cs]
                dmix = dyu * sgb[rs, cs]
                dmb = dmix.astype(BF16)
                o = g * GROUP_W
                dz_ref[rs, 2 * D_MODEL + o:2 * D_MODEL + o + GROUP_W] = (dyb[rs, cs] * mixb * sgb[rs, cs]).astype(BF16)
                dz_ref[rs, 3 * D_MODEL + o:3 * D_MODEL + o + GROUP_W] = jnp.dot(
                    wtt_ref[g], dmb, preferred_element_type=F32).astype(BF16)
                dz_ref[rs, 4 * D_MODEL + o:4 * D_MODEL + o + GROUP_W] = (dyu * mixb * dsgb[rs, cs]).astype(BF16)
                dws = lax.dot_general(dmb, vb, (((1,), (1,)), ((), ())), preferred_element_type=F32)
                dws_ref[g] += jnp.where(tril, dws, 0.0)
                dbs_ref[...] += jnp.where(lane == g, jnp.sum(dmix, axis=1, keepdims=True), 0.0)

    return pl.pallas_call(
        body, name="even_mix_bwd", grid=(n_tiles,),
        in_specs=[pl.BlockSpec((TM, 2 * D_MODEL), lambda i: (i, 0)), _zcol(0), _zcol(1), _zcol(2), _zcol(3), _zcol(4),
                  _prev_halo(0), _next_halo(0, s), _next_halo(1, s),
                  _full((4, GROUP_W, GROUP_W)), _full((1, D_MODEL)), _full((4, CHUNK, CHUNK)), _full((4, CHUNK, CHUNK)),
                  _full((4, CHUNK, 1))] + tail_specs,
        out_specs=[pl.BlockSpec((TM, EVEN_IN), lambda i: (i, 0)), _full((4, GROUP_W, GROUP_W)), _full((1, D_MODEL)),
                   _full((4, CHUNK, CHUNK)), _full((CHUNK, 128))],
        out_shape=[jax.ShapeDtypeStruct((s, EVEN_IN), BF16), jax.ShapeDtypeStruct((4, GROUP_W, GROUP_W), F32),
                   jax.ShapeDtypeStruct((1, D_MODEL), F32), jax.ShapeDtypeStruct((4, CHUNK, CHUNK), F32),
                   jax.ShapeDtypeStruct((CHUNK, 128), F32)],
        compiler_params=_cp(("arbitrary",), 56),
    )(dy, z, z, z, z, z, z, dy, z, pw, ps, wt, wtt, bs, *tail_args)


STAT_W = 128
Q_BLOCKS = 2
Q_ROWS = Q_BLOCKS * ATTN_BLOCK


def _band(d):
    row = lax.broadcasted_iota(jnp.int32, (ATTN_BLOCK, 2 * ATTN_BLOCK), 0)
    col = lax.broadcasted_iota(jnp.int32, (ATTN_BLOCK, 2 * ATTN_BLOCK), 1)
    steps = row + ATTN_BLOCK - col
    return (steps >= 0) & (steps <= ATTN_BLOCK), col >= ATTN_BLOCK, -(steps * d).astype(F32)


def _attn_fwd(z, gi):
    s = z.shape[0]
    d = DILATIONS[gi]
    nb = s // d // ATTN_BLOCK
    nq = nb // Q_BLOCKS

    def spec(which, prev=False):
        cb = which * 3 + gi
        if prev:
            return pl.BlockSpec((ATTN_BLOCK, D_MODEL), lambda r, i: (r * nb + jnp.maximum(Q_BLOCKS * i - 1, 0), cb))
        return pl.BlockSpec((Q_ROWS, D_MODEL), lambda r, i: (r * nq + i, cb))

    def body(q_ref, kp_ref, kc_ref, vp_ref, vc_ref, o_ref, lse_ref):
        i = pl.program_id(1)
        inner, own, negdist = _band(d)
        lane = lax.broadcasted_iota(jnp.int32, (ATTN_BLOCK, STAT_W), 1)
        for b in range(Q_BLOCKS):
            rows = slice(b * ATTN_BLOCK, (b + 1) * ATTN_BLOCK)
            valid = (inner & ((i > 0) | own)) if b == 0 else inner
            stat = jnp.zeros((ATTN_BLOCK, STAT_W), F32)
            for h in range(N_SLOTS):
                sl = slice(h * HEAD_DIM, (h + 1) * HEAD_DIM)
                if b == 0:
                    k = jnp.concatenate([kp_ref[:, sl], kc_ref[:ATTN_BLOCK, sl]], axis=0)
                    v = jnp.concatenate([vp_ref[:, sl], vc_ref[:ATTN_BLOCK, sl]], axis=0)
                else:
                    k = kc_ref[(b - 1) * ATTN_BLOCK:(b + 1) * ATTN_BLOCK, sl]
                    v = vc_ref[(b - 1) * ATTN_BLOCK:(b + 1) * ATTN_BLOCK, sl]
                sc = lax.dot_general(q_ref[rows, sl], k, (((1,), (1,)), ((), ())), preferred_element_type=F32) * SCALE
                sc = jnp.where(valid, sc + SLOPES[gi][h] * negdist, NEG)
                m = jnp.max(sc, axis=-1, keepdims=True)
                p = jnp.exp(sc - m)
                l = jnp.sum(p, axis=-1, keepdims=True)
                o = jnp.dot((p * (1.0 / l)).astype(BF16), v, preferred_element_type=F32)
                o_ref[rows, sl] = o.astype(BF16)
                stat = jnp.where(lane == h, m + jnp.log(l), stat)
            lse_ref[rows, :] = stat

    return pl.pallas_call(
        body, name=f"attn_fwd_d{d}", grid=(d, nq),
        in_specs=[spec(0), spec(1, True), spec(1), spec(2, True), spec(2)],
        out_specs=[pl.BlockSpec((Q_ROWS, D_MODEL), lambda r, i: (r * nq + i, 0)),
                   pl.BlockSpec((Q_ROWS, STAT_W), lambda r, i: (r * nq + i, 0))],
        out_shape=[jax.ShapeDtypeStruct((s, D_MODEL), BF16), jax.ShapeDtypeStruct((s, STAT_W), F32)],
        compiler_params=_cp(("parallel", "parallel"), 32),
    )(z, z, z, z, z)


def _attn_bwd(z, dyc, ltot, dst, dz, gi):
    s = z.shape[0]
    d = DILATIONS[gi]
    nb = s // d // ATTN_BLOCK
    nq = nb // Q_BLOCKS
    n_steps = d * nq

    def rev(cb, width=D_MODEL, prev=False):
        if prev:
            return pl.BlockSpec((ATTN_BLOCK, width), lambda r, n: (r * nb + jnp.maximum(Q_BLOCKS * (nq - 1 - n) - 1, 0), cb))
        return pl.BlockSpec((Q_ROWS, width), lambda r, n: (r * nq + nq - 1 - n, cb))

    def body(q_ref, kp_ref, kc_ref, vp_ref, vc_ref, dy_ref, l_ref, d_ref, dz_in, dz_out, dq_s, dk_s, dv_s, ck_s, cv_s, sems):
        del dz_in
        r = pl.program_id(0)
        n = pl.program_id(1)
        i = nq - 1 - n
        step = r * nq + n

        def out_copy(src, which):
            rows = pl.ds(pl.multiple_of((r * nq + i) * Q_ROWS, Q_ROWS), Q_ROWS)
            return pltpu.make_async_copy(src, dz_out.at[rows, pl.ds((which * 3 + gi) * D_MODEL, D_MODEL)], sems.at[which])

        copies = [out_copy(dq_s, 0), out_copy(dk_s, 1), out_copy(dv_s, 2)]

        @pl.when(step > 0)
        def _():
            for cp in copies:
                cp.wait()

        @pl.when(n == 0)
        def _():
            ck_s[...] = jnp.zeros_like(ck_s)
            cv_s[...] = jnp.zeros_like(cv_s)

        row = lax.broadcasted_iota(jnp.int32, (Q_ROWS, Q_ROWS + ATTN_BLOCK), 0)
        col = lax.broadcasted_iota(jnp.int32, (Q_ROWS, Q_ROWS + ATTN_BLOCK), 1)
        steps = row + ATTN_BLOCK - col
        valid = (steps >= 0) & (steps <= ATTN_BLOCK) & ((i > 0) | (col >= ATTN_BLOCK))
        negdist = -(steps * d).astype(F32)
        nt = (((1,), (1,)), ((), ()))
        tn = (((0,), (0,)), ((), ()))
        for h in range(N_SLOTS):
            sl = slice(h * HEAD_DIM, (h + 1) * HEAD_DIM)
            q = q_ref[:, sl]
            k = jnp.concatenate([kp_ref[:, sl], kc_ref[:, sl]], axis=0)
            v = jnp.concatenate([vp_ref[:, sl], vc_ref[:, sl]], axis=0)
            dy = dy_ref[:, sl]
            sc = lax.dot_general(q, k, nt, preferred_element_type=F32) * SCALE + SLOPES[gi][h] * negdist
            p = jnp.where(valid, jnp.exp(sc - l_ref[:, h:h + 1]), 0.0)
            dp = lax.dot_general(dy, v, nt, preferred_element_type=F32)
            ds = (p * (dp - d_ref[:, h:h + 1])).astype(BF16)
            dq_s[:, sl] = (jnp.dot(ds, k, preferred_element_type=F32) * SCALE).astype(BF16)
            dk = lax.dot_general(ds, q, tn, preferred_element_type=F32) * SCALE
            dv = lax.dot_general(p.astype(BF16), dy, tn, preferred_element_type=F32)
            dk_s[:Q_ROWS - ATTN_BLOCK, sl] = dk[ATTN_BLOCK:Q_ROWS].astype(BF16)
            dv_s[:Q_ROWS - ATTN_BLOCK, sl] = dv[ATTN_BLOCK:Q_ROWS].astype(BF16)
            dk_s[Q_ROWS - ATTN_BLOCK:, sl] = (ck_s[:, sl] + dk[Q_ROWS:]).astype(BF16)
            dv_s[Q_ROWS - ATTN_BLOCK:, sl] = (cv_s[:, sl] + dv[Q_ROWS:]).astype(BF16)
            ck_s[:, sl] = dk[:ATTN_BLOCK]
            cv_s[:, sl] = dv[:ATTN_BLOCK]

        for cp in copies:
            cp.start()

        @pl.when(step == n_steps - 1)
        def _():
            for cp in copies:
                cp.wait()

    stage = pltpu.VMEM((Q_ROWS, D_MODEL), BF16)
    carry = pltpu.VMEM((ATTN_BLOCK, D_MODEL), F32)
    return pl.pallas_call(
        body, name=f"attn_bwd_d{d}", grid=(d, nq),
        in_specs=[rev(gi), rev(3 + gi, prev=True), rev(3 + gi), rev(6 + gi, prev=True), rev(6 + gi),
                  rev(0), rev(0, STAT_W), rev(0, STAT_W), ANY],
        out_specs=ANY,
        out_shape=jax.ShapeDtypeStruct((s, ODD_IN), BF16),
        scratch_shapes=[stage, stage, stage, carry, carry, pltpu.SemaphoreType.DMA((3,))],
        input_output_aliases={8: 0},
        compiler_params=_cp(("arbitrary", "arbitrary"), 32),
    )(z, z, z, z, z, dyc, ltot, dst, dz)


def _odd_mix_fwd(z, os_, lses, cw):
    s = z.shape[0]

    def body(o0, o1, o2, l0, l1, l2, gc_ref, db_ref, dc_ref, dx_ref, gd_ref, hc_ref, hx_ref, cw_ref, y_ref, yc_ref, lt_ref,
             lt4_ref, lt16_ref, scr_o, scr_o2, scr_l):
        i = pl.program_id(0)
        _from_classes(scr_l, l1, 4)
        lse1 = _get(scr_l)
        _from_classes(scr_l, l2, 16)
        ls = [l0[...], lse1, _get(scr_l)]
        lmax = jnp.maximum(jnp.maximum(ls[0], ls[1]), ls[2])
        es = [jnp.exp(l - lmax) for l in ls]
        den = es[0] + es[1] + es[2]
        alpha = [e / den for e in es]
        ltot = lmax + jnp.log(den)
        lt_ref[...] = ltot
        _put(scr_l, ltot)
        _to_classes(lt4_ref, scr_l, 4, F32)
        _to_classes(lt16_ref, scr_l, 16, F32)
        _from_classes(scr_o, o1, 4)
        _from_classes(scr_o2, o2, 16)
        for h in range(N_SLOTS):
            sl = slice(h * HEAD_DIM, (h + 1) * HEAD_DIM)
            yc = (alpha[0][:, h:h + 1] * o0[:, sl].astype(F32) + alpha[1][:, h:h + 1] * scr_o[h]
                  + alpha[2][:, h:h + 1] * scr_o2[h])
            yc_ref[:, sl] = yc.astype(BF16)
            gc = gc_ref[:, sl].astype(F32)
            y_ref[:, sl] = (yc * (gc * _sig(gc))).astype(BF16)
            zc = dc_ref[:, sl].astype(F32) * dx_ref[:, sl].astype(F32)
            halo = jnp.where(i > 0, hc_ref[:, sl].astype(F32) * hx_ref[:, sl].astype(F32), 0.0)
            ext = jnp.concatenate([halo, zc], axis=0)
            z1 = pltpu.roll(ext, 1, 0)[HALO:]
            z2 = pltpu.roll(ext, 2, 0)[HALO:]
            conv = cw_ref[0:1, sl] * z2 + cw_ref[1:2, sl] * z1 + cw_ref[2:3, sl] * zc
            gd = gd_ref[:, sl].astype(F32)
            y_ref[:, D_MODEL + h * HEAD_DIM:D_MODEL + (h + 1) * HEAD_DIM] = (
                db_ref[:, sl].astype(F32) * conv * (gd * _sig(gd))).astype(BF16)

    row = pl.BlockSpec((TM, D_MODEL), lambda i: (i, 0))
    stat = pl.BlockSpec((TM, STAT_W), lambda i: (i, 0))
    y, ycr, lt, lt4, lt16 = pl.pallas_call(
        body, name="odd_mix_fwd", grid=(s // TM,),
        in_specs=[row, _class_spec(4, TM, D_MODEL), _class_spec(16, TM, D_MODEL),
                  stat, _class_spec(4, TM, STAT_W), _class_spec(16, TM, STAT_W),
                  _zcol(9), _zcol(10), _zcol(11), _zcol(12), _zcol(13), _prev_halo(11), _prev_halo(12), _full((3, D_MODEL))],
        out_specs=[pl.BlockSpec((TM, 2 * D_MODEL), lambda i: (i, 0)), row, stat, _class_spec(4, TM, STAT_W),
                   _class_spec(16, TM, STAT_W)],
        out_shape=[jax.ShapeDtypeStruct((s, 2 * D_MODEL), BF16), jax.ShapeDtypeStruct((s, D_MODEL), BF16),
                   jax.ShapeDtypeStruct((s, STAT_W), F32), jax.ShapeDtypeStruct((4, s // 4, STAT_W), F32),
                   jax.ShapeDtypeStruct((16, s // 16, STAT_W), F32)],
        scratch_shapes=[_token_scratch(TM, D_MODEL), _token_scratch(TM, D_MODEL), _token_scratch(TM, STAT_W)],
        compiler_params=_cp(("parallel",), 48),
    )(os_[0], _class_major(os_[1], 4), _class_major(os_[2], 16), lses[0], _class_major(lses[1], 4),
      _class_major(lses[2], 16), z, z, z, z, z, z, z, cw)
    return y, ycr, [lt, lt4.reshape(s, STAT_W), lt16.reshape(s, STAT_W)]


def _odd_mix_bwd(dy, z, ycr, cw):
    s = z.shape[0]
    n_tiles = s // TM
    rest = ODD_IN - QKV_BLOCKS * D_MODEL

    def body(dy_ref, yc_ref, gc_ref, db_ref, dc_ref, dx_ref, gd_ref, hc_ref, hx_ref, dyn_ref, dbn_ref, gdn_ref, cw_ref,
             dz_ref, dyc_ref, dyc4_ref, dyc16_ref, dd_ref, dd4_ref, dd16_ref, dcw_ref, stage, sem, scr_o, scr_l):
        i = pl.program_id(0)
        out = pltpu.make_async_copy(
            stage, dz_ref.at[pl.ds(pl.multiple_of(i * TM, TM), TM), pl.ds(QKV_BLOCKS * D_MODEL, rest)], sem)

        @pl.when(i > 0)
        def _():
            out.wait()

        @pl.when(i == 0)
        def _():
            dcw_ref[...] = jnp.zeros_like(dcw_ref)

        lane = lax.broadcasted_iota(jnp.int32, (TM, STAT_W), 1)
        stat = jnp.zeros((TM, STAT_W), F32)
        nrow = TM + HALO
        for h in range(N_SLOTS):
            sl = slice(h * HEAD_DIM, (h + 1) * HEAD_DIM)
            sd = slice(D_MODEL + h * HEAD_DIM, D_MODEL + (h + 1) * HEAD_DIM)
            dyc_in = dy_ref[:, sl].astype(F32)
            gc = gc_ref[:, sl].astype(F32)
            sg = _sig(gc)
            yc = yc_ref[:, sl].astype(F32)
            dyc = dyc_in * (gc * sg)
            dyc_ref[:, sl] = dyc.astype(BF16)
            scr_o[h] = dyc
            stage[:, sl] = (dyc_in * yc * (sg * (1.0 + gc * (1.0 - sg)))).astype(BF16)
            stat = jnp.where(lane == h, jnp.sum(dyc * yc, axis=-1, keepdims=True), stat)
            dc = dc_ref[:, sl].astype(F32)
            dx = dx_ref[:, sl].astype(F32)
            zc = dc * dx
            halo = jnp.where(i > 0, hc_ref[:, sl].astype(F32) * hx_ref[:, sl].astype(F32), 0.0)
            ext = jnp.concatenate([halo, zc], axis=0)
            z1 = pltpu.roll(ext, 1, 0)[HALO:]
            z2 = pltpu.roll(ext, 2, 0)[HALO:]
            w0, w1, w2 = cw_ref[0:1, sl], cw_ref[1:2, sl], cw_ref[2:3, sl]
            conv = w0 * z2 + w1 * z1 + w2 * zc
            gd = gd_ref[:, sl].astype(F32)
            sg = _sig(gd)
            sgd = gd * sg
            db = db_ref[:, sl].astype(F32)
            dyd = dy_ref[:, sd].astype(F32)
            dconv = dyd * db * sgd
            gdn = gdn_ref[:, sl].astype(F32)
            dconv_n = jnp.where(i < n_tiles - 1,
                                dyn_ref[:, sl].astype(F32) * dbn_ref[:, sl].astype(F32) * (gdn * _sig(gdn)), 0.0)
            extn = jnp.concatenate([dconv, dconv_n], axis=0)
            dzc = w2 * dconv + w1 * pltpu.roll(extn, nrow - 1, 0)[:TM] + w0 * pltpu.roll(extn, nrow - 2, 0)[:TM]
            stage[:, sd] = (dyd * conv * sgd).astype(BF16)
            stage[:, 2 * D_MODEL + h * HEAD_DIM:2 * D_MODEL + (h + 1) * HEAD_DIM] = (dzc * dx).astype(BF16)
            stage[:, 3 * D_MODEL + h * HEAD_DIM:3 * D_MODEL + (h + 1) * HEAD_DIM] = (dzc * dc).astype(BF16)
            stage[:, 4 * D_MODEL + h * HEAD_DIM:4 * D_MODEL + (h + 1) * HEAD_DIM] = (
                dyd * db * conv * (sg * (1.0 + gd * (1.0 - sg)))).astype(BF16)
            for tap, shifted in enumerate((z2, z1, zc)):
                dcw_ref[tap:tap + 1, sl] += jnp.sum(dconv * shifted, axis=0, keepdims=True)
        _to_classes(dyc4_ref, scr_o, 4, BF16)
        _to_classes(dyc16_ref, scr_o, 16, BF16)
        dd_ref[...] = stat
        _put(scr_l, stat)
        _to_classes(dd4_ref, scr_l, 4, F32)
        _to_classes(dd16_ref, scr_l, 16, F32)

        out.start()

        @pl.when(i == n_tiles - 1)
        def _():
            out.wait()

    row = pl.BlockSpec((TM, D_MODEL), lambda i: (i, 0))
    stat = pl.BlockSpec((TM, STAT_W), lambda i: (i, 0))
    dz, dyc, dyc4, dyc16, dd, dd4, dd16, g_conv = pl.pallas_call(
        body, name="odd_mix_bwd", grid=(n_tiles,),
        in_specs=[pl.BlockSpec((TM, 2 * D_MODEL), lambda i: (i, 0)), row, _zcol(9), _zcol(10), _zcol(11), _zcol(12), _zcol(13),
                  _prev_halo(11), _prev_halo(12), _next_halo(1, s), _next_halo(10, s), _next_halo(13, s), _full((3, D_MODEL))],
        out_specs=[ANY, row, _class_spec(4, TM, D_MODEL), _class_spec(16, TM, D_MODEL),
                   stat, _class_spec(4, TM, STAT_W), _class_spec(16, TM, STAT_W), _full((3, D_MODEL))],
        out_shape=[jax.ShapeDtypeStruct((s, ODD_IN), BF16), jax.ShapeDtypeStruct((s, D_MODEL), BF16),
                   jax.ShapeDtypeStruct((4, s // 4, D_MODEL), BF16), jax.ShapeDtypeStruct((16, s // 16, D_MODEL), BF16),
                   jax.ShapeDtypeStruct((s, STAT_W), F32), jax.ShapeDtypeStruct((4, s // 4, STAT_W), F32),
                   jax.ShapeDtypeStruct((16, s // 16, STAT_W), F32), jax.ShapeDtypeStruct((3, D_MODEL), F32)],
        scratch_shapes=[pltpu.VMEM((TM, rest), BF16), pltpu.SemaphoreType.DMA(()), _token_scratch(TM, D_MODEL),
                        _token_scratch(TM, STAT_W)],
        compiler_params=_cp(("arbitrary",), 48),
    )(dy, ycr, z, z, z, z, z, z, z, dy, z, z, cw)
    dyc = [dyc, dyc4.reshape(s, D_MODEL), dyc16.reshape(s, D_MODEL)]
    dd = [dd, dd4.reshape(s, STAT_W), dd16.reshape(s, STAT_W)]
    return dz, dyc, dd, g_conv


def _cols_of_order(order):
    if order == 0:
        return (lambda j: jnp.where(j < 3, 3 * j, j + 6)), 8
    return (lambda j: 3 * j + order), 3


class _Hooks:
    def before_even(self):
        return None

    def odd_weights(self, w, x1):
        return w

    def odd_grads_ready(self, g_w_in_o, g_w_out_o):
        return None

    def even_mix_done(self, dz_e):
        return None

    def backward_done(self, dx0):
        return None


def _local_step(x, target, w, hooks=_Hooks()):
    tril = jnp.tril(jnp.ones((CHUNK, CHUNK), bool))
    wt = jnp.where(tril[None], w["ws"], 0.0).astype(BF16)
    wtt = jnp.swapaxes(wt, 1, 2)
    bs = w["bs"].reshape(4, CHUNK, 1)

    h_e = _rms_fwd("rms_fwd_even", x, w["even_norm"], after=hooks.before_even())
    z_e = _mm_nn("even_in_proj", h_e, w["w_in_e"], MM_ROWS, 1280, BF16)
    y_e = _even_mix_fwd(z_e, w["pool_w"], w["pool_scale"], wt, bs)
    x1 = _mm_nn("even_out_proj", y_e, w["w_out_e"], MM_ROWS, 1024, F32, resid=x)
    w = hooks.odd_weights(w, x1)
    h_o = _rms_fwd_orders("rms_fwd_odd", x1, w["odd_norm"])
    z_o = None
    for o in range(3):
        cols, n_cols = _cols_of_order(o)
        z_o = _mm_nn(f"odd_in_proj_o{o}", h_o[o], w["w_in_o"], MM_ROWS, D_MODEL, BF16, col_map=cols, n_cols=n_cols,
                     into=z_o)
    att = [_attn_fwd(z_o, gi) for gi in range(3)]
    y_o, ycr, ltot = _odd_mix_fwd(z_o, [a[0] for a in att], [a[1] for a in att], w["conv_w"])
    dx2, loss8, g_final = _out_proj_loss(y_o, w["w_out_o"], x1, w["final_norm"], target)

    g_w_out_o = _mm_tn("odd_out_proj_dw", y_o, dx2, 1024, DW_OUT_TOKENS)
    dy_o = _mm_nt("odd_out_proj_dy", dx2, w["w_out_o"], MM_ROWS, 1024, BF16)
    dz_o, dyc, dst, g_conv = _odd_mix_bwd(dy_o, z_o, ycr, w["conv_w"])
    for gi in range(3):
        dz_o = _attn_bwd(z_o, dyc[gi], ltot[gi], dst[gi], dz_o, gi)
    g_w_in_o, dh_o = None, []
    for o in range(3):
        cols, n_cols = _cols_of_order(o)
        g_w_in_o = _mm_tn(f"odd_in_proj_dw_o{o}", h_o[o], dz_o, D_MODEL, DW_IN_TOKENS, col_map=cols, n_cols=n_cols,
                          into=g_w_in_o)
        dh_o.append(_mm_nt(f"odd_in_proj_dh_o{o}", dz_o, w["w_in_o"], MM_ROWS, D_MODEL, BF16, k_map=cols, nk=n_cols))
    dx1, g_odd_norm = _rms_bwd("rms_bwd_odd", dh_o[0], x1, w["odd_norm"], dx2, dh4=dh_o[1], dh16=dh_o[2])
    after = hooks.odd_grads_ready(g_w_in_o, g_w_out_o)
    g_w_out_e = _mm_tn("even_out_proj_dw", y_e, dx1, 1024, DW_OUT_TOKENS, after=after)
    dy_e = _mm_nt("even_out_proj_dy", dx1, w["w_out_e"], MM_ROWS, 1024, BF16)
    dz_e, g_pw, g_ps, g_ws, g_bs = _even_mix_bwd(dy_e, z_e, w["pool_w"], w["pool_scale"], wt, wtt, bs)
    after = hooks.even_mix_done(dz_e)
    g_w_in_e = _mm_tn("even_in_proj_dw", h_e, dz_e, 1280, DW_IN_TOKENS, after=after)
    dh_e = _mm_nt("even_in_proj_dh", dz_e, w["w_in_e"], MM_ROWS, 2560, F32)
    dx0, g_even_norm = _rms_bwd("rms_bwd_even", dh_e, x, w["even_norm"], dx1)
    hooks.backward_done(dx0)

    grads = dict(w_in_e=g_w_in_e, pool_w=g_pw, w_out_e=g_w_out_e, w_in_o=g_w_in_o, w_out_o=g_w_out_o,
                 even_norm=g_even_norm, pool_scale=g_ps, ws=g_ws, bs=g_bs[:, :4].T, final_norm=g_final,
                 odd_norm=g_odd_norm, conv_w=g_conv)
    return loss8[0, 0], dx0, grads


class _Big(NamedTuple):
    name: str
    full: tuple
    haxis: int
    kaxis: int
    sub: int


BIGS = (
    _Big("w_in_e", (1024, 5120), 0, 1, 2),
    _Big("pool_w", (4, 256, 256), 0, 1, 1),
    _Big("w_out_e", (2048, 1024), 1, 0, 1),
    _Big("w_in_o", (1024, 14336), 0, 1, 4),
    _Big("w_out_o", (2048, 1024), 1, 0, 1),
)
N_BIG = len(BIGS)


def _shape(b, half=False, shard=False):
    return tuple(n // (2 if (half and ax == b.haxis) else 1) // (4 if (shard and ax == b.kaxis) else 1)
                 for ax, n in enumerate(b.full))


def _at(ref, b, h=None, k=None):
    idx = []
    for ax, n in enumerate(b.full):
        if ax == b.haxis and h is not None:
            idx.append(pl.ds(h * (n // 2), n // 2))
        elif ax == b.kaxis and k is not None:
            idx.append(pl.ds(k * (n // 4), n // 4))
        else:
            idx.append(slice(None))
    return ref.at[tuple(idx)]


def _place():
    x, y, c = lax.axis_index("x"), lax.axis_index("y"), lax.axis_index("c")
    chips = [(1 - x, y), (x, 1 - y), (1 - x, 1 - y)]
    return x, y, c, 2 * x + y, chips, [2 * cx + cy for cx, cy in chips]


def _piece_shape(b):
    return (4, 2) + _shape(b, half=True, shard=True)


def _gather_weights(bigs, shards, tiny):
    nb = len(bigs)

    def body(*refs):
        ins, tiny_in = refs[:nb], refs[nb]
        outs, tiny_out = refs[nb + 1:2 * nb + 1], refs[2 * nb + 1]
        send, recv, loc = refs[2 * nb + 2:]
        x, y, c, k_me, chips, ks = _place()
        sib = (x, y, 1 - c)

        def rc(src, dst, sem, to):
            return pltpu.make_async_remote_copy(src_ref=src, dst_ref=dst, send_sem=send.at[sem], recv_sem=recv.at[sem],
                                                device_id=to, device_id_type=MESH)

        own = pltpu.make_async_copy(tiny_in, tiny_out.at[k_me], loc)
        own.start()
        sends = []
        for j, chip in enumerate(chips):
            for a, b in enumerate(bigs):
                sends.append(rc(_at(ins[a], b, h=c), outs[a].at[k_me, c], 6 * a + j, (*chip, c)))
            sends.append(rc(tiny_in, tiny_out.at[k_me], 6 * nb + j, (*chip, c)))
        for cp in sends:
            cp.start()
        for j in range(3):
            for a in range(nb):
                piece = outs[a].at[ks[j], c]
                rc(piece, piece, 6 * a + j, sib).wait_recv()
                fwd = rc(piece, piece, 6 * a + 3 + j, sib)
                fwd.start()
                sends.append(fwd)
            rc(tiny_in, tiny_out.at[ks[j]], 6 * nb + j, sib).wait_recv()
        for j in range(3):
            for a in range(nb):
                piece = outs[a].at[ks[j], 1 - c]
                rc(piece, piece, 6 * a + 3 + j, sib).wait_recv()
        for cp in sends:
            cp.wait_send()
        own.wait()

    n_sem = 6 * nb + 3
    return pl.pallas_call(
        body, name="gather_even_weights",
        in_specs=[ANY] * (nb + 1), out_specs=[ANY] * (nb + 1),
        out_shape=[jax.ShapeDtypeStruct(_piece_shape(b), BF16) for b in bigs]
        + [jax.ShapeDtypeStruct((4,) + tiny.shape, F32)],
        scratch_shapes=[pltpu.SemaphoreType.DMA((n_sem,)), pltpu.SemaphoreType.DMA((n_sem,)), pltpu.SemaphoreType.DMA(())],
    )(*shards, tiny)


def _assemble(b, pieces, shard, k_arr):
    blk = _blk(b)

    def body(k_ref, p_ref, s_ref, o_ref):
        mine = pl.program_id(0) == k_ref[0]

        @pl.when(mine)
        def _():
            o_ref[...] = s_ref[...]

        @pl.when(jnp.logical_not(mine))
        def _():
            o_ref[...] = p_ref[...]

    return pl.pallas_call(
        body, name=f"assemble_{b.name}",
        grid_spec=pltpu.PrefetchScalarGridSpec(
            num_scalar_prefetch=1, grid=(4, 2, b.sub),
            in_specs=[pl.BlockSpec((None, None) + blk, lambda k, h, st, k_ref: (k, h) + _bidx(b, 0, 0, st)),
                      pl.BlockSpec(blk, lambda k, h, st, k_ref: _bidx(b, h, 0, st))],
            out_specs=pl.BlockSpec(blk, lambda k, h, st, k_ref: _bidx(b, h, k, st))),
        out_shape=jax.ShapeDtypeStruct(b.full, BF16),
        compiler_params=_cp(("arbitrary", "arbitrary", "arbitrary"), 32),
    )(k_arr, pieces, shard)


def _copies_to_chips(bigs):
    def copies(srcs, lands, send, recv, waiting=False):
        _, _, c, k_me, chips, _ = _place()
        return [pltpu.make_async_remote_copy(
                    src_ref=_at(srcs[a], b, h=c), dst_ref=lands[a].at[k_me, c], send_sem=send.at[3 * a + j],
                    recv_sem=recv.at[3 * a + j], device_id=(*chips[j], c), device_id_type=MESH)
                for j in range(3) for a, b in enumerate(bigs)]
    return copies


def _copies_swap_halves(bigs):
    def copies(srcs, lands, send, recv, waiting=False):
        x, y, c, _, _, _ = _place()
        return [pltpu.make_async_remote_copy(
                    src_ref=_at(srcs[a], b, h=1 - c), dst_ref=lands[a], send_sem=send.at[a], recv_sem=recv.at[a],
                    device_id=(x, y, 1 - c), device_id_type=MESH)
                for a, b in enumerate(bigs)]
    return copies


def _copies_partials(bigs):
    def copies(srcs, lands, send, recv, waiting=False):
        _, _, c, _, chips, ks = _place()
        return [pltpu.make_async_remote_copy(
                    src_ref=_at(srcs[a], b, k=ks[j]), dst_ref=lands[a].at[j], send_sem=send.at[3 * a + j],
                    recv_sem=recv.at[3 * a + j], device_id=(*chips[j], c), device_id_type=MESH)
                for j in range(3) for a, b in enumerate(bigs)]
    return copies


def _exchange(name, srcs, land_shapes, copies_of, n_copies):
    ns = len(srcs)

    def body(*refs):
        copies = copies_of(refs[:ns], refs[ns:ns + len(land_shapes)], refs[-2], refs[-1])
        for cp in copies:
            cp.start()
        for cp in copies:
            cp.wait()

    return pl.pallas_call(
        body, name=name, in_specs=[ANY] * ns, out_specs=[ANY] * len(land_shapes), out_shape=land_shapes,
        scratch_shapes=[pltpu.SemaphoreType.DMA((n_copies,)), pltpu.SemaphoreType.DMA((n_copies,))],
    )(*srcs)


HBM = pl.BlockSpec(memory_space=pltpu.HBM)
SEM = pl.BlockSpec(memory_space=pltpu.SEMAPHORE)
SIDE_EFFECT = pltpu.SideEffectType.DATAFLOW_SIDE_EFFECTING


def _in_hbm(a):
    return pltpu.with_memory_space_constraint(a, pltpu.HBM)


def _exchange_start(name, srcs, land_shapes, copies_of, n_copies, after=None):
    ns, nl = len(srcs), len(land_shapes)
    lands = [lax.empty(sh.shape, sh.dtype) for sh in land_shapes]
    tail = [] if after is None else [after]
    n_in = ns + nl + len(tail)

    def body(*refs):
        send, recv, token = refs[n_in], refs[n_in + 1], refs[-1]
        for cp in copies_of(refs[:ns], refs[ns:ns + nl], send, recv):
            cp.start()
        token[...] = jnp.zeros_like(token)

    thru = [pltpu.HBM(a.shape, a.dtype) for a in (*srcs, *lands)]
    send, recv, *bufs, token = pl.pallas_call(
        body, name=name,
        out_shape=(pltpu.SemaphoreType.DMA((n_copies,)), pltpu.SemaphoreType.DMA((n_copies,)), *thru,
                   jax.ShapeDtypeStruct((8, 128), F32)),
        in_specs=[HBM] * (ns + nl) + [ANY] * len(tail),
        out_specs=(SEM, SEM, *([HBM] * (ns + nl)), pl.BlockSpec(memory_space=pltpu.VMEM)),
        input_output_aliases={i: 2 + i for i in range(ns + nl)},
        compiler_params=pltpu.CompilerParams(has_side_effects=SIDE_EFFECT),
    )(*[_in_hbm(a) for a in (*srcs, *lands)], *tail)
    return (send, recv, bufs, ns), token


def _exchange_wait(name, state, copies_of, after):
    send, recv, bufs, ns = state
    n = len(bufs)

    def body(*refs):
        ins = refs[:n]
        for cp in copies_of(ins[:ns], ins[ns:], refs[n], refs[n + 1], waiting=True):
            cp.wait_send()
            cp.wait_recv()

    out = pl.pallas_call(
        body, name=name, out_shape=tuple(pltpu.HBM(a.shape, a.dtype) for a in bufs),
        in_specs=[HBM] * n + [SEM, SEM, ANY], out_specs=tuple([HBM] * n),
        input_output_aliases={i: i for i in range(n)},
        compiler_params=pltpu.CompilerParams(has_side_effects=SIDE_EFFECT),
    )(*bufs, send, recv, after)
    return list(out[:ns]), list(out[ns:])


def _finish_gather(bigs, pieces):
    nb = len(bigs)

    def body(*refs):
        outs, send, recv = refs[nb:2 * nb], refs[2 * nb], refs[2 * nb + 1]
        x, y, c, _, _, ks = _place()
        fwd = [pltpu.make_async_remote_copy(
                   src_ref=outs[a].at[ks[j], c], dst_ref=outs[a].at[ks[j], c], send_sem=send.at[3 * a + j],
                   recv_sem=recv.at[3 * a + j], device_id=(x, y, 1 - c), device_id_type=MESH)
               for j in range(3) for a in range(nb)]
        for cp in fwd:
            cp.start()
        for cp in fwd:
            cp.wait()

    return pl.pallas_call(
        body, name="gather_odd_finish", in_specs=[ANY] * nb, out_specs=[ANY] * nb,
        out_shape=[jax.ShapeDtypeStruct(_piece_shape(b), BF16) for b in bigs],
        scratch_shapes=[pltpu.SemaphoreType.DMA((3 * nb,)), pltpu.SemaphoreType.DMA((3 * nb,))],
        input_output_aliases={a: a for a in range(nb)},
    )(*pieces)


def _blk(b):
    win = _shape(b, half=True, shard=True)
    return (win[0] // b.sub,) + win[1:]


def _bidx(b, h, k, st):
    idx = [0] * len(b.full)
    idx[b.haxis] = h
    idx[b.kaxis] = k
    idx[0] = idx[0] * b.sub + st
    return tuple(idx)


def _chip_sum(b, g, got, c_arr):
    blk = _blk(b)

    def body(c_ref, g_ref, r_ref, o_ref):
        del c_ref
        o_ref[...] = (g_ref[...] + r_ref[...]).astype(BF16)

    half = pl.BlockSpec(blk, lambda k, st, c_ref: _bidx(b, 0, k, st))
    return pl.pallas_call(
        body, name=f"rs_chip_sum_{b.name}",
        grid_spec=pltpu.PrefetchScalarGridSpec(
            num_scalar_prefetch=1, grid=(4, b.sub),
            in_specs=[pl.BlockSpec(blk, lambda k, st, c_ref: _bidx(b, c_ref[0], k, st)), half], out_specs=half),
        out_shape=jax.ShapeDtypeStruct(_shape(b, half=True), BF16),
        compiler_params=_cp(("arbitrary", "arbitrary"), 40),
    )(c_arr, g, got)


def _half_shapes(bigs):
    return [jax.ShapeDtypeStruct(_shape(b, half=True), F32) for b in bigs]


def _partial_shapes(bigs):
    return [jax.ShapeDtypeStruct((3,) + _shape(b, half=True, shard=True), BF16) for b in bigs]


def _shard_sum(b, mine, got, ck_arr):
    blk = _blk(b)

    def body(ck_ref, m_ref, r0, r1, r2, o_ref):
        del ck_ref
        o_ref[...] = (m_ref[...].astype(F32) + r0[...].astype(F32)) + (r1[...].astype(F32) + r2[...].astype(F32))

    def peer(j):
        return pl.BlockSpec((None,) + blk, lambda st, ck: (j,) + _bidx(b, 0, 0, st))

    return pl.pallas_call(
        body, name=f"rs_shard_sum_{b.name}",
        grid_spec=pltpu.PrefetchScalarGridSpec(
            num_scalar_prefetch=1, grid=(b.sub,),
            in_specs=[pl.BlockSpec(blk, lambda st, ck: _bidx(b, 0, ck[1], st)), peer(0), peer(1), peer(2)],
            out_specs=pl.BlockSpec(blk, lambda st, ck: _bidx(b, ck[0], 0, st))),
        out_shape=jax.ShapeDtypeStruct(_shape(b, shard=True), F32),
        compiler_params=_cp(("arbitrary",), 40),
    )(ck_arr, mine, got, got, got)


def _share_halves(gs):
    def body(*refs):
        ins, outs, send, recv = refs[:N_BIG], refs[N_BIG:2 * N_BIG], refs[2 * N_BIG], refs[2 * N_BIG + 1]
        del ins
        x, y, c, _, _, _ = _place()
        copies = [pltpu.make_async_remote_copy(src_ref=_at(outs[a], b, h=c), dst_ref=_at(outs[a], b, h=c),
                                               send_sem=send.at[a], recv_sem=recv.at[a], device_id=(x, y, 1 - c),
                                               device_id_type=MESH)
                  for a, b in enumerate(BIGS)]
        for cp in copies:
            cp.start()
        for cp in copies:
            cp.wait()

    return pl.pallas_call(
        body, name="rs_share_halves", in_specs=[ANY] * N_BIG, out_specs=[ANY] * N_BIG,
        out_shape=[jax.ShapeDtypeStruct(_shape(b, shard=True), F32) for b in BIGS],
        scratch_shapes=[pltpu.SemaphoreType.DMA((N_BIG,)), pltpu.SemaphoreType.DMA((N_BIG,))],
        input_output_aliases={a: a for a in range(N_BIG)},
    )(*gs)


def _gather_small(block):
    m_per, n = block.shape

    def body(x_ref, out_ref, send_sems, recv_sems, local_sem):
        x, y, c = lax.axis_index("x"), lax.axis_index("y"), lax.axis_index("c")
        me, sibling = (x, y, c), (x, y, 1 - c)
        chips = [(1 - x, y), (x, 1 - y), (1 - x, 1 - y)]

        def rows(px, py, pc):
            return out_ref.at[pl.ds((4 * px + 2 * py + pc) * m_per, m_per), :]

        def copy(k, blk, to, src=None):
            return pltpu.make_async_remote_copy(
                src_ref=rows(*blk) if src is None else src, dst_ref=rows(*blk), send_sem=send_sems.at[k],
                recv_sem=recv_sems.at[k], device_id=to, device_id_type=MESH)

        mine = pltpu.make_async_copy(x_ref, rows(*me), local_sem)
        mine.start()
        first = [copy(0, me, sibling, src=x_ref)]
        first += [copy(1 + j, me, (*chip, c), src=x_ref) for j, chip in enumerate(chips)]
        for cp in first:
            cp.start()
        passed = [copy(4 + j, (*chip, c), sibling) for j, chip in enumerate(chips)]
        for j, chip in enumerate(chips):
            copy(1 + j, (*chip, c), me).wait_recv()
            passed[j].start()
        copy(0, sibling, me).wait_recv()
        for j, chip in enumerate(chips):
            copy(4 + j, (*chip, 1 - c), me).wait_recv()
        for cp in first + passed:
            cp.wait_send()
        mine.wait()

    return pl.pallas_call(
        body, name="gather_small_grads",
        out_shape=jax.ShapeDtypeStruct((8 * m_per, n), block.dtype),
        in_specs=[pl.BlockSpec(memory_space=pltpu.VMEM)], out_specs=pl.BlockSpec(memory_space=pltpu.VMEM),
        scratch_shapes=[pltpu.SemaphoreType.DMA((7,)), pltpu.SemaphoreType.DMA((7,)), pltpu.SemaphoreType.DMA],
    )(block)


def _sum_small(stack):
    _, m_per, n = stack.shape

    def body(x_ref, o_ref):
        acc = x_ref[0]
        for dev in range(1, 8):
            acc = acc + x_ref[dev]
        o_ref[...] = acc

    return pl.pallas_call(body, name="sum_small_grads", out_shape=jax.ShapeDtypeStruct((m_per, n), F32))(stack)


def _adam_update(w_ref, g_ref, m_ref, v_ref, d_ref, mo_ref, vo_ref):
    gg = g_ref[...]
    mn = ADAM_B1 * m_ref[...] + (1.0 - ADAM_B1) * gg
    vn = ADAM_B2 * v_ref[...] + (1.0 - ADAM_B2) * (gg * gg)
    m_hat = mn / (1.0 - ADAM_B1 ** ADAM_STEP)
    v_hat = vn / (1.0 - ADAM_B2 ** ADAM_STEP)
    d_ref[...] = -ADAM_LR * (m_hat / (jnp.sqrt(v_hat) + ADAM_EPS) + ADAM_WD * w_ref[...])
    mo_ref[...] = mn
    vo_ref[...] = vn


def _adamw(name, w, g, m, v, rows):
    shape = w.shape

    def body(*refs):
        _adam_update(*refs)

    spec = pl.BlockSpec((rows,) + shape[1:], lambda i: (i,) + (0,) * (len(shape) - 1))
    return pl.pallas_call(
        body, name=name, grid=(shape[0] // rows,), in_specs=[spec] * 4, out_specs=[spec] * 3,
        out_shape=[jax.ShapeDtypeStruct(shape, F32)] * 3, compiler_params=_cp(("parallel",), 48),
    )(w, g, m, v)


def _adamw_small(ws, gs, ms, vs):
    n = len(ws)

    def body(*refs):
        for a in range(n):
            _adam_update(*[refs[q * n + a] for q in range(7)])

    outs = pl.pallas_call(
        body, name="adamw_small", out_shape=[jax.ShapeDtypeStruct(w.shape, F32) for w in ws] * 3,
    )(*ws, *gs, *ms, *vs)
    return outs[:n], outs[n:2 * n], outs[2 * n:]


ADAM_ROWS = dict(w_in_e=256, pool_w=4, w_out_e=256, w_in_o=128, w_out_o=256)


def _pack(parts, rows):
    flat = jnp.concatenate([p.reshape(-1).astype(F32) for p in parts])
    return jnp.pad(flat, (0, rows * 128 - flat.shape[0])).reshape(rows, 128)


def _unpack(buf, shapes):
    flat = buf.reshape(-1)
    out, off = [], 0
    for shp in shapes:
        n = 1
        for dim in shp:
            n *= dim
        out.append(flat[off:off + n].reshape(shp))
        off += n
    return out


WEIGHTS = ("even_norm", "even_w_in", "even_pool_w", "even_pool_scale", "even_ws", "even_bs", "even_w_out", "odd_norm",
           "odd_w_in", "odd_conv_w", "odd_w_out", "final_norm")
BIG_OF = dict(w_in_e="even_w_in", pool_w="even_pool_w", w_out_e="even_w_out", w_in_o="odd_w_in", w_out_o="odd_w_out")
SMALL = ("even_norm", "even_pool_scale", "even_ws", "even_bs", "final_norm", "odd_norm", "odd_conv_w")
SMALL_GRAD_ROWS = 576


def kernel(x, even_norm, even_w_in, even_pool_w, even_pool_scale, even_ws, even_bs, even_w_out, odd_norm, odd_w_in, odd_conv_w, odd_w_out, final_norm, loss_target, m_even_norm, m_even_w_in, m_even_pool_w, m_even_pool_scale, m_even_ws, m_even_bs, m_even_w_out, m_odd_norm, m_odd_w_in, m_odd_conv_w, m_odd_w_out, m_final_norm, v_even_norm, v_even_w_in, v_even_pool_w, v_even_pool_scale, v_even_ws, v_even_bs, v_even_w_out, v_odd_norm, v_odd_w_in, v_odd_conv_w, v_odd_w_out, v_final_norm):
    wv = dict(zip(WEIGHTS, (even_norm, even_w_in, even_pool_w, even_pool_scale, even_ws, even_bs, even_w_out, odd_norm,
                            odd_w_in, odd_conv_w, odd_w_out, final_norm)))
    mv = dict(zip(WEIGHTS, (m_even_norm, m_even_w_in, m_even_pool_w, m_even_pool_scale, m_even_ws, m_even_bs,
                            m_even_w_out, m_odd_norm, m_odd_w_in, m_odd_conv_w, m_odd_w_out, m_final_norm)))
    vv = dict(zip(WEIGHTS, (v_even_norm, v_even_w_in, v_even_pool_w, v_even_pool_scale, v_even_ws, v_even_bs,
                            v_even_w_out, v_odd_norm, v_odd_w_in, v_odd_conv_w, v_odd_w_out, v_final_norm)))
    c = lax.axis_index("c")
    k_me = 2 * lax.axis_index("x") + lax.axis_index("y")

    c_arr = jnp.reshape(c, (1,)).astype(jnp.int32)
    ck_arr = jnp.stack([c, k_me]).astype(jnp.int32)
    even_bigs, odd_bigs = BIGS[:3], BIGS[3:]

    shards = {b.name: wv[BIG_OF[b.name]][0].astype(BF16) for b in BIGS}
    tiny = jnp.concatenate([odd_conv_w[0], odd_norm], axis=0)
    *pieces_even, tiny_all = _gather_weights(even_bigs, [shards[b.name] for b in even_bigs], tiny)
    tiny_full = jnp.transpose(tiny_all, (1, 0, 2)).reshape(4, D_MODEL)
    to_chips, swap_odd, partials_odd = _copies_to_chips(odd_bigs), _copies_swap_halves(odd_bigs), _copies_partials(odd_bigs)
    gather_state, gather_token = _exchange_start(
        "gather_odd_start", [shards[b.name] for b in odd_bigs],
        [jax.ShapeDtypeStruct(_piece_shape(b), BF16) for b in odd_bigs], to_chips, 3 * len(odd_bigs), after=pieces_even[-1])
    k_arr = jnp.reshape(k_me, (1,)).astype(jnp.int32)
    w = {b.name: _assemble(b, p, shards[b.name], k_arr) for b, p in zip(even_bigs, pieces_even)}
    w.update(even_norm=even_norm, pool_scale=even_pool_scale, ws=even_ws[0], bs=even_bs[0],
             final_norm=final_norm.reshape(1, D_MODEL), conv_w=tiny_full[:3], odd_norm=tiny_full[3:4])

    class Hooks(_Hooks):
        def before_even(self):
            return gather_token

        def odd_weights(self, w, x1):
            srcs, lands = _exchange_wait("gather_odd_wait", gather_state, to_chips, after=x1)
            pieces = _finish_gather(odd_bigs, lands)
            return dict(w, **{b.name: _assemble(b, p, s, k_arr) for b, p, s in zip(odd_bigs, pieces, srcs)})

        def odd_grads_ready(self, g_w_in_o, g_w_out_o):
            self.swap, token = _exchange_start("rs_odd_swap_start", [g_w_in_o, g_w_out_o], _half_shapes(odd_bigs),
                                               swap_odd, len(odd_bigs))
            return token

        def even_mix_done(self, dz_e):
            grads, got = _exchange_wait("rs_odd_swap_wait", self.swap, swap_odd, after=dz_e)
            sums = [_chip_sum(b, g, r, c_arr) for b, g, r in zip(odd_bigs, grads, got)]
            self.partials, token = _exchange_start("rs_odd_partials_start", sums, _partial_shapes(odd_bigs), partials_odd,
                                                   3 * len(odd_bigs))
            return token

        def backward_done(self, dx0):
            self.sums, self.parts = _exchange_wait("rs_odd_partials_wait", self.partials, partials_odd, after=dx0)

    hooks = Hooks()
    loss, dx, g = _local_step(x[0], loss_target[0], w, hooks)
    loss = lax.psum(loss, ("x", "y", "c"))

    got = _exchange("rs_even_swap", [g[b.name] for b in even_bigs], _half_shapes(even_bigs), _copies_swap_halves(even_bigs),
                    len(even_bigs))
    sums = [_chip_sum(b, g[b.name], r, c_arr) for b, r in zip(even_bigs, got)]
    parts = _exchange("rs_even_partials", sums, _partial_shapes(even_bigs), _copies_partials(even_bigs), 3 * len(even_bigs))
    halves = [_shard_sum(b, sm, p, ck_arr) for b, sm, p in zip(BIGS, sums + hooks.sums, list(parts) + hooks.parts)]
    g_shard = dict(zip((b.name for b in BIGS), _share_halves(halves)))

    small_g = _pack([g["even_norm"], g["pool_scale"], g["ws"], g["bs"], g["final_norm"], g["odd_norm"], g["conv_w"]],
                    SMALL_GRAD_ROWS)
    small_g = _sum_small(_gather_small(small_g).reshape(8, SMALL_GRAD_ROWS, 128))
    g_en, g_ps, g_ws, g_bs, g_fn, g_on, g_cw = _unpack(
        small_g, [(1, D_MODEL), (1, D_MODEL), (1, 4, CHUNK, CHUNK), (1, 4, CHUNK), (D_MODEL,), (1, D_MODEL), (1, 3, D_MODEL)])
    g_on = lax.dynamic_slice(g_on, (0, k_me * 256), (1, 256))
    g_cw = lax.dynamic_slice(g_cw, (0, 0, k_me * 256), (1, 3, 256))
    grad = dict(even_norm=g_en, even_pool_scale=g_ps, even_ws=g_ws, even_bs=g_bs, final_norm=g_fn, odd_norm=g_on,
                odd_conv_w=g_cw)
    for b in BIGS:
        grad[BIG_OF[b.name]] = g_shard[b.name][None]

    delta, new_m, new_v = {}, {}, {}
    for b in BIGS:
        n = BIG_OF[b.name]
        d_, m_, v_ = _adamw(f"adamw_{b.name}", wv[n][0], g_shard[b.name], mv[n][0], vv[n][0], ADAM_ROWS[b.name])
        delta[n], new_m[n], new_v[n] = d_[None], m_[None], v_[None]
    flat = [(wv[n].size // wv[n].shape[-1], wv[n].shape[-1]) for n in SMALL]
    outs = _adamw_small(*[[src[n].reshape(shp) for n, shp in zip(SMALL, flat)] for src in (wv, grad, mv, vv)])
    for dst, arrs in zip((delta, new_m, new_v), outs):
        for n, arr in zip(SMALL, arrs):
            dst[n] = arr.reshape(wv[n].shape)

    return (loss, dx[None], *[grad[n] for n in WEIGHTS], *[delta[n] for n in WEIGHTS], *[new_m[n] for n in WEIGHTS],
            *[new_v[n] for n in WEIGHTS])
```

```python
from typing import NamedTuple

import jax
import jax.numpy as jnp
from jax import lax
from jax.experimental import pallas as pl
from jax.experimental.pallas import tpu as pltpu

F32, BF16 = jnp.float32, jnp.bfloat16

D_MODEL = 1024
EPS = 1e-6
NEG = -1e30
POOL_SIZES = (2, 4, 8, 16)
GROUP_W = 256
CHUNK = 128
DILATIONS = (1, 4, 16)
N_SLOTS = 8
HEAD_DIM = 128
ATTN_BLOCK = 128
SCALE = HEAD_DIM ** -0.5
EVEN_IN = 5120
ODD_IN = 14336
QKV_BLOCKS = 9
ODD_BLOCKS = ODD_IN // D_MODEL
SLOPES = tuple(tuple(2.0 ** (-8.0 * (g * N_SLOTS + s + 1) / (3 * N_SLOTS)) for s in range(N_SLOTS)) for g in range(3))

ADAM_LR, ADAM_B1, ADAM_B2, ADAM_EPS, ADAM_WD, ADAM_STEP = 0.001, 0.9, 0.999, 1e-08, 0.01, 10

HALO = 16
TS = 512
TM = 256
MM_ROWS = 1024
IN_ROWS = 2048
DW_IN_TOKENS = 2048
DW_OUT_TOKENS = 1024
MIB = 1 << 20
MESH = pl.DeviceIdType.MESH
ANY = pl.BlockSpec(memory_space=pl.ANY)


def _cp(sem, vmem_mib):
    return pltpu.CompilerParams(dimension_semantics=sem, vmem_limit_bytes=vmem_mib * MIB)


def _sig(x):
    return 0.5 * jnp.tanh(0.5 * x) + 0.5


def _win_sum(e, w, forward):
    n = e.shape[0]
    k = 1
    while k < w:
        e = e + pltpu.roll(e, (n - k) if forward else k, 0)
        k *= 2
    return e


def _mm_nn(name, a, b, tm, tn, out_dtype, resid=None, col_map=None, n_cols=None, into=None):
    m, k = a.shape
    n = b.shape[1]
    if col_map is None:
        col_map, n_cols = (lambda j: j), n // tn

    def body(*refs):
        a_ref, b_ref = refs[0], refs[1]
        acc = jnp.dot(a_ref[...].astype(BF16), b_ref[...], preferred_element_type=F32)
        if resid is not None:
            acc = acc + refs[2][...]
        o_ref = refs[-1]
        o_ref[...] = acc.astype(out_dtype)

    in_specs = [pl.BlockSpec((tm, k), lambda j, i: (i, 0)), pl.BlockSpec((k, tn), lambda j, i: (0, col_map(j)))]
    args = [a, b]
    if resid is not None:
        in_specs.append(pl.BlockSpec((tm, tn), lambda j, i: (i, col_map(j))))
        args.append(resid)
    aliases = {}
    if into is not None:
        aliases = {len(args): 0}
        in_specs.append(ANY)
        args.append(into)
    return pl.pallas_call(
        body, name=name, grid=(n_cols, m // tm), in_specs=in_specs,
        out_specs=pl.BlockSpec((tm, tn), lambda j, i: (i, col_map(j))),
        out_shape=jax.ShapeDtypeStruct((m, n), out_dtype), input_output_aliases=aliases,
        compiler_params=_cp(("parallel", "parallel"), 48),
    )(*args)


def _mm_nt(name, a, b, tm, tk, out_dtype, k_map=None, nk=None):
    m, k = a.shape
    n = b.shape[0]
    if k_map is None:
        k_map, nk = (lambda kk: kk), k // tk

    def body(a_ref, b_ref, o_ref, acc_ref):
        kk = pl.program_id(1)
        p = lax.dot_general(a_ref[...].astype(BF16), b_ref[...], (((1,), (1,)), ((), ())), preferred_element_type=F32)
        if nk == 1:
            o_ref[...] = p.astype(out_dtype)
        else:
            @pl.when(kk == 0)
            def _():
                acc_ref[...] = p

            @pl.when(kk > 0)
            def _():
                acc_ref[...] += p

            @pl.when(kk == nk - 1)
            def _():
                o_ref[...] = acc_ref[...].astype(out_dtype)

    return pl.pallas_call(
        body, name=name, grid=(m // tm, nk),
        in_specs=[pl.BlockSpec((tm, tk), lambda i, kk: (i, k_map(kk))), pl.BlockSpec((n, tk), lambda i, kk: (0, k_map(kk)))],
        out_specs=pl.BlockSpec((tm, n), lambda i, kk: (i, 0)),
        out_shape=jax.ShapeDtypeStruct((m, n), out_dtype),
        scratch_shapes=[pltpu.VMEM((tm, n) if nk > 1 else (8, 128), F32)],
        compiler_params=_cp(("parallel", "arbitrary"), 56),
    )(a, b)


def _mm_tn(name, a, g, tn, ts, col_map=None, n_cols=None, into=None, after=None):
    s, ka = a.shape
    n = g.shape[1]
    if col_map is None:
        col_map, n_cols = (lambda j: j), n // tn

    def body(a_ref, g_ref, *rest):
        o_ref = rest[-1]
        st = pl.program_id(1)
        p = lax.dot_general(a_ref[...], g_ref[...].astype(BF16), (((0,), (0,)), ((), ())), preferred_element_type=F32)

        @pl.when(st == 0)
        def _():
            o_ref[...] = p

        @pl.when(st > 0)
        def _():
            o_ref[...] += p

    in_specs = [pl.BlockSpec((ts, ka), lambda j, st: (st, 0)), pl.BlockSpec((ts, tn), lambda j, st: (st, col_map(j)))]
    args = [a, g]
    aliases = {}
    if into is not None:
        aliases = {2: 0}
        in_specs.append(ANY)
        args.append(into)
    if after is not None:
        in_specs.append(ANY)
        args.append(after)
    return pl.pallas_call(
        body, name=name, grid=(n_cols, s // ts), in_specs=in_specs,
        out_specs=pl.BlockSpec((ka, tn), lambda j, st: (0, col_map(j))),
        out_shape=jax.ShapeDtypeStruct((ka, n), F32), input_output_aliases=aliases,
        compiler_params=_cp(("parallel", "arbitrary"), 56),
    )(*args)


def _rms_fwd(name, x, g, after=None):
    s = x.shape[0]

    def body(x_ref, g_ref, *rest):
        xf = x_ref[...]
        r = lax.rsqrt(jnp.mean(xf * xf, axis=-1, keepdims=True) + EPS)
        rest[-1][...] = (xf * r * g_ref[...]).astype(BF16)

    row = pl.BlockSpec((TS, D_MODEL), lambda i: (i, 0))
    in_specs, args = [row, pl.BlockSpec((1, D_MODEL), lambda i: (0, 0))], [x, g]
    if after is not None:
        in_specs.append(ANY)
        args.append(after)
    return pl.pallas_call(
        body, name=name, grid=(s // TS,), in_specs=in_specs, out_specs=row,
        out_shape=jax.ShapeDtypeStruct((s, D_MODEL), BF16), compiler_params=_cp(("parallel",), 32),
    )(*args)


def _class_major(a, d):
    return a.reshape(d, a.shape[0] // d, a.shape[1])


def _class_spec(d, tile, width):
    return pl.BlockSpec((d, tile // d, width), lambda i: (0, i, 0))


LANES = 128


def _token_scratch(tile, width):
    return pltpu.VMEM((width // LANES, tile, LANES), F32)


def _put(scr, val):
    for c in range(scr.shape[0]):
        scr[c] = val[:, c * LANES:(c + 1) * LANES]


def _get(scr):
    return jnp.concatenate([scr[c] for c in range(scr.shape[0])], axis=1)


def _to_classes(ref3, scr, d, dtype):
    n = ref3.shape[1]
    for c in range(scr.shape[0]):
        for r in range(d):
            ref3[r, :, c * LANES:(c + 1) * LANES] = scr.at[c][pl.ds(r, n, stride=d), :].astype(dtype)


def _from_classes(scr, ref3, d):
    n = ref3.shape[1]
    for c in range(scr.shape[0]):
        for r in range(d):
            scr.at[c][pl.ds(r, n, stride=d), :] = ref3[r, :, c * LANES:(c + 1) * LANES].astype(F32)


def _rms_fwd_orders(name, x, g):
    s = x.shape[0]

    def body(x_ref, g_ref, h_ref, h4_ref, h16_ref, scr):
        xf = x_ref[...]
        r = lax.rsqrt(jnp.mean(xf * xf, axis=-1, keepdims=True) + EPS)
        h = xf * r * g_ref[...]
        h_ref[...] = h.astype(BF16)
        _put(scr, h)
        _to_classes(h4_ref, scr, 4, BF16)
        _to_classes(h16_ref, scr, 16, BF16)

    row = pl.BlockSpec((TS, D_MODEL), lambda i: (i, 0))
    h, h4, h16 = pl.pallas_call(
        body, name=name, grid=(s // TS,), in_specs=[row, pl.BlockSpec((1, D_MODEL), lambda i: (0, 0))],
        out_specs=[row, _class_spec(4, TS, D_MODEL), _class_spec(16, TS, D_MODEL)],
        out_shape=[jax.ShapeDtypeStruct((s, D_MODEL), BF16), jax.ShapeDtypeStruct((4, s // 4, D_MODEL), BF16),
                   jax.ShapeDtypeStruct((16, s // 16, D_MODEL), BF16)],
        scratch_shapes=[_token_scratch(TS, D_MODEL)],
        compiler_params=_cp(("parallel",), 32),
    )(x, g)
    return h, h4.reshape(s, D_MODEL), h16.reshape(s, D_MODEL)


def _rms_bwd(name, dh, x, g, dres, dh4=None, dh16=None):
    s = x.shape[0]
    extra = dh4 is not None

    def body(dh_ref, x_ref, g_ref, dres_ref, *rest):
        if extra:
            dh4_ref, dh16_ref, dx_ref, dg_ref, scr = rest
        else:
            dx_ref, dg_ref = rest
        xf = x_ref[...]
        r = lax.rsqrt(jnp.mean(xf * xf, axis=-1, keepdims=True) + EPS)
        xh = xf * r
        dhf = dh_ref[...].astype(F32)
        if extra:
            _from_classes(scr, dh4_ref, 4)
            dhf = dhf + _get(scr)
            _from_classes(scr, dh16_ref, 16)
            dhf = dhf + _get(scr)
        dxh = dhf * g_ref[...]
        dx_ref[...] = dres_ref[...] + r * (dxh - xh * jnp.mean(dxh * xh, axis=-1, keepdims=True))
        part = jnp.sum(dhf * xh, axis=0, keepdims=True)

        @pl.when(pl.program_id(0) == 0)
        def _():
            dg_ref[...] = part

        @pl.when(pl.program_id(0) > 0)
        def _():
            dg_ref[...] += part

    row = pl.BlockSpec((TS, D_MODEL), lambda i: (i, 0))
    vec = pl.BlockSpec((1, D_MODEL), lambda i: (0, 0))
    in_specs, args, scratch = [row, row, vec, row], [dh, x, g, dres], []
    if extra:
        in_specs += [_class_spec(4, TS, D_MODEL), _class_spec(16, TS, D_MODEL)]
        args += [_class_major(dh4, 4), _class_major(dh16, 16)]
        scratch = [_token_scratch(TS, D_MODEL)]
    return pl.pallas_call(
        body, name=name, grid=(s // TS,), in_specs=in_specs, out_specs=[row, vec],
        out_shape=[jax.ShapeDtypeStruct((s, D_MODEL), F32), jax.ShapeDtypeStruct((1, D_MODEL), F32)],
        scratch_shapes=scratch, compiler_params=_cp(("arbitrary",), 40),
    )(*args)


def _out_proj_loss(y, w_out, resid, g, target):
    s, k = y.shape

    def body(y_ref, w_ref, r_ref, g_ref, t_ref, dx_ref, loss_ref, dg_ref):
        xf = jnp.dot(y_ref[...], w_ref[...], preferred_element_type=F32) + r_ref[...]
        gg = g_ref[...]
        r = lax.rsqrt(jnp.mean(xf * xf, axis=-1, keepdims=True) + EPS)
        xh = xf * r
        e = xh * gg - t_ref[...]
        dy = e * (1.0 / D_MODEL)
        dxh = dy * gg
        dx_ref[...] = r * (dxh - xh * jnp.mean(dxh * xh, axis=-1, keepdims=True))
        lpart = 0.5 * jnp.sum(jnp.mean(e * e, axis=-1, keepdims=True), axis=0, keepdims=True)
        lpart = jnp.broadcast_to(lpart, (8, 128))
        gpart = jnp.sum(dy * xh, axis=0, keepdims=True)

        @pl.when(pl.program_id(0) == 0)
        def _():
            loss_ref[...] = lpart
            dg_ref[...] = gpart

        @pl.when(pl.program_id(0) > 0)
        def _():
            loss_ref[...] += lpart
            dg_ref[...] += gpart

    row = pl.BlockSpec((TS, D_MODEL), lambda i: (i, 0))
    vec = pl.BlockSpec((1, D_MODEL), lambda i: (0, 0))
    return pl.pallas_call(
        body, name="odd_out_proj_loss", grid=(s // TS,),
        in_specs=[pl.BlockSpec((TS, k), lambda i: (i, 0)), pl.BlockSpec((k, D_MODEL), lambda i: (0, 0)), row, vec, row],
        out_specs=[row, pl.BlockSpec((8, 128), lambda i: (0, 0)), vec],
        out_shape=[jax.ShapeDtypeStruct((s, D_MODEL), F32), jax.ShapeDtypeStruct((8, 128), F32),
                   jax.ShapeDtypeStruct((1, D_MODEL), F32)],
        compiler_params=_cp(("arbitrary",), 48),
    )(y, w_out, resid, g, target)


def _zcol(c, tm=TM):
    return pl.BlockSpec((tm, D_MODEL), lambda i, c=c: (i, c))


def _prev_halo(c, tm=TM):
    return pl.BlockSpec((HALO, D_MODEL), lambda i, c=c: (jnp.maximum(i * (tm // HALO) - 1, 0), c))


def _next_halo(c, n_rows, tm=TM):
    last = n_rows // HALO - 1
    return pl.BlockSpec((HALO, D_MODEL), lambda i, c=c: (jnp.minimum((i + 1) * (tm // HALO), last), c))


def _full(shape):
    return pl.BlockSpec(shape, lambda i: (0,) * len(shape))


def _inv_count(first_row, n, w):
    t = first_row + lax.broadcasted_iota(jnp.int32, (n, 1), 0)
    return 1.0 / jnp.minimum(t + 1, w).astype(F32)


def _even_mix_fwd(z, pw, ps, wt, bs):
    s = z.shape[0]

    def body(a_ref, ga_ref, u_ref, v_ref, gb_ref, halo_ref, pw_ref, ps_ref, wt_ref, bs_ref, y_ref):
        i = pl.program_id(0)
        a = a_ref[...].astype(F32)
        halo = jnp.where(i > 0, halo_ref[...].astype(F32), 0.0)
        ext = jnp.concatenate([halo, a], axis=0)
        ga = ga_ref[...].astype(F32)
        sga = ga * _sig(ga)
        for g, w in enumerate(POOL_SIZES):
            cs = slice(g * GROUP_W, (g + 1) * GROUP_W)
            win = _win_sum(ext[:, cs], w, False)[HALO:]
            pooled = win * _inv_count(i * TM, TM, w) - a[:, cs]
            mixed = jnp.dot(pooled.astype(BF16), pw_ref[g], preferred_element_type=F32)
            y_ref[:, cs] = (mixed * ps_ref[:, cs] * sga[:, cs]).astype(BF16)
        gb = gb_ref[...].astype(F32)
        gate = u_ref[...].astype(F32) * (gb * _sig(gb))
        for ch in range(TM // CHUNK):
            rs = slice(ch * CHUNK, (ch + 1) * CHUNK)
            for g in range(4):
                cs = slice(g * GROUP_W, (g + 1) * GROUP_W)
                mixb = jnp.dot(wt_ref[g], v_ref[rs, cs], preferred_element_type=F32) + bs_ref[g]
                y_ref[rs, D_MODEL + g * GROUP_W:D_MODEL + (g + 1) * GROUP_W] = (gate[rs, cs] * mixb).astype(BF16)

    return pl.pallas_call(
        body, name="even_mix_fwd", grid=(s // TM,),
        in_specs=[_zcol(0), _zcol(1), _zcol(2), _zcol(3), _zcol(4), _prev_halo(0),
                  _full((4, GROUP_W, GROUP_W)), _full((1, D_MODEL)), _full((4, CHUNK, CHUNK)), _full((4, CHUNK, 1))],
        out_specs=pl.BlockSpec((TM, 2 * D_MODEL), lambda i: (i, 0)),
        out_shape=jax.ShapeDtypeStruct((s, 2 * D_MODEL), BF16),
        compiler_params=_cp(("parallel",), 48),
    )(z, z, z, z, z, z, pw, ps, wt, bs)


def _even_mix_bwd(dy, z, pw, ps, wt, wtt, bs, after=None):
    s = z.shape[0]
    n_tiles = s // TM
    tail_specs, tail_args = ([ANY], [after]) if after is not None else ([], [])

    def body(dy_ref, a_ref, ga_ref, u_ref, v_ref, gb_ref, halo_ref, dyn_ref, gan_ref, pw_ref, ps_ref, wt_ref, wtt_ref,
             bs_ref, *rest):
        dz_ref, dpw_ref, dps_ref, dws_ref, dbs_ref = rest[-5:]
        i = pl.program_id(0)

        @pl.when(i == 0)
        def _():
            dpw_ref[...] = jnp.zeros_like(dpw_ref)
            dps_ref[...] = jnp.zeros_like(dps_ref)
            dws_ref[...] = jnp.zeros_like(dws_ref)
            dbs_ref[...] = jnp.zeros_like(dbs_ref)

        a = a_ref[...].astype(F32)
        halo = jnp.where(i > 0, halo_ref[...].astype(F32), 0.0)
        ext = jnp.concatenate([halo, a], axis=0)
        ga = ga_ref[...].astype(F32)
        sg = _sig(ga)
        sga = ga * sg
        dsga = sg * (1.0 + ga * (1.0 - sg))
        dya = dy_ref[:, :D_MODEL].astype(F32)
        gan = gan_ref[...].astype(F32)
        dmn_all = jnp.where(i < n_tiles - 1, dyn_ref[...].astype(F32) * ps_ref[...] * (gan * _sig(gan)), 0.0)
        for g, w in enumerate(POOL_SIZES):
            cs = slice(g * GROUP_W, (g + 1) * GROUP_W)
            inv = _inv_count(i * TM, TM, w)
            pooled = _win_sum(ext[:, cs], w, False)[HALO:] * inv - a[:, cs]
            pb = pooled.astype(BF16)
            mixed = jnp.dot(pb, pw_ref[g], preferred_element_type=F32)
            dyg = dya[:, cs]
            psg = ps_ref[:, cs]
            dm = (dyg * psg * sga[:, cs]).astype(BF16)
            dz_ref[:, D_MODEL + g * GROUP_W:D_MODEL + (g + 1) * GROUP_W] = (dyg * mixed * psg * dsga[:, cs]).astype(BF16)
            dps_ref[:, cs] += jnp.sum(dyg * mixed * sga[:, cs], axis=0, keepdims=True)
            dpw_ref[g] += lax.dot_general(pb, dm, (((0,), (0,)), ((), ())), preferred_element_type=F32)
            nt = (((1,), (1,)), ((), ()))
            dpool = lax.dot_general(dm, pw_ref[g], nt, preferred_element_type=F32)
            dpool_n = lax.dot_general(dmn_all[:, cs].astype(BF16), pw_ref[g], nt, preferred_element_type=F32)
            e = jnp.concatenate([dpool * inv, dpool_n * _inv_count((i + 1) * TM, HALO, w)], axis=0)
            dz_ref[:, cs] = (_win_sum(e, w, True)[:TM] - dpool).astype(BF16)

        gb = gb_ref[...].astype(F32)
        sg = _sig(gb)
        sgb = gb * sg
        dsgb = sg * (1.0 + gb * (1.0 - sg))
        u = u_ref[...].astype(F32)
        dyb = dy_ref[:, D_MODEL:].astype(F32)
        tril = lax.broadcasted_iota(jnp.int32, (CHUNK, CHUNK), 0) >= lax.broadcasted_iota(jnp.int32, (CHUNK, CHUNK), 1)
        lane = lax.broadcasted_iota(jnp.int32, (CHUNK, 128), 1)
        for ch in range(TM // CHUNK):
            rs = slice(ch * CHUNK, (ch + 1) * CHUNK)
            for g in range(4):
                cs = slice(g * GROUP_W, (g + 1) * GROUP_W)
                vb = v_ref[rs, cs]
                mixb = jnp.dot(wt_ref[g], vb, preferred_element_type=F32) + bs_ref[g]
                dyu = dyb[rs, cs] * u[rs, cs]
                dmix = dyu * sgb[rs, cs]
                dmb = dmix.astype(BF16)
                o = g * GROUP_W
                dz_ref[rs, 2 * D_MODEL + o:2 * D_MODEL + o + GROUP_W] = (dyb[rs, cs] * mixb * sgb[rs, cs]).astype(BF16)
                dz_ref[rs, 3 * D_MODEL + o:3 * D_MODEL + o + GROUP_W] = jnp.dot(
                    wtt_ref[g], dmb, preferred_element_type=F32).astype(BF16)
                dz_ref[rs, 4 * D_MODEL + o:4 * D_MODEL + o + GROUP_W] = (dyu * mixb * dsgb[rs, cs]).astype(BF16)
                dws = lax.dot_general(dmb, vb, (((1,), (1,)), ((), ())), preferred_element_type=F32)
                dws_ref[g] += jnp.where(tril, dws, 0.0)
                dbs_ref[...] += jnp.where(lane == g, jnp.sum(dmix, axis=1, keepdims=True), 0.0)

    return pl.pallas_call(
        body, name="even_mix_bwd", grid=(n_tiles,),
        in_specs=[pl.BlockSpec((TM, 2 * D_MODEL), lambda i: (i, 0)), _zcol(0), _zcol(1), _zcol(2), _zcol(3), _zcol(4),
                  _prev_halo(0), _next_halo(0, s), _next_halo(1, s),
                  _full((4, GROUP_W, GROUP_W)), _full((1, D_MODEL)), _full((4, CHUNK, CHUNK)), _full((4, CHUNK, CHUNK)),
                  _full((4, CHUNK, 1))] + tail_specs,
        out_specs=[pl.BlockSpec((TM, EVEN_IN), lambda i: (i, 0)), _full((4, GROUP_W, GROUP_W)), _full((1, D_MODEL)),
                   _full((4, CHUNK, CHUNK)), _full((CHUNK, 128))],
        out_shape=[jax.ShapeDtypeStruct((s, EVEN_IN), BF16), jax.ShapeDtypeStruct((4, GROUP_W, GROUP_W), F32),
                   jax.ShapeDtypeStruct((1, D_MODEL), F32), jax.ShapeDtypeStruct((4, CHUNK, CHUNK), F32),
                   jax.ShapeDtypeStruct((CHUNK, 128), F32)],
        compiler_params=_cp(("arbitrary",), 56),
    )(dy, z, z, z, z, z, z, dy, z, pw, ps, wt, wtt, bs, *tail_args)


STAT_W = 128
Q_BLOCKS = 2
Q_ROWS = Q_BLOCKS * ATTN_BLOCK


def _band(d):
    row = lax.broadcasted_iota(jnp.int32, (ATTN_BLOCK, 2 * ATTN_BLOCK), 0)
    col = lax.broadcasted_iota(jnp.int32, (ATTN_BLOCK, 2 * ATTN_BLOCK), 1)
    steps = row + ATTN_BLOCK - col
    return (steps >= 0) & (steps <= ATTN_BLOCK), col >= ATTN_BLOCK, -(steps * d).astype(F32)


def _attn_fwd(z, gi):
    s = z.shape[0]
    d = DILATIONS[gi]
    nb = s // d // ATTN_BLOCK
    nq = nb // Q_BLOCKS

    def spec(which, prev=False):
        cb = which * 3 + gi
        if prev:
            return pl.BlockSpec((ATTN_BLOCK, D_MODEL), lambda r, i: (r * nb + jnp.maximum(Q_BLOCKS * i - 1, 0), cb))
        return pl.BlockSpec((Q_ROWS, D_MODEL), lambda r, i: (r * nq + i, cb))

    def body(q_ref, kp_ref, kc_ref, vp_ref, vc_ref, o_ref, lse_ref):
        i = pl.program_id(1)
        inner, own, negdist = _band(d)
        lane = lax.broadcasted_iota(jnp.int32, (ATTN_BLOCK, STAT_W), 1)
        for b in range(Q_BLOCKS):
            rows = slice(b * ATTN_BLOCK, (b + 1) * ATTN_BLOCK)
            valid = (inner & ((i > 0) | own)) if b == 0 else inner
            stat = jnp.zeros((ATTN_BLOCK, STAT_W), F32)
            for h in range(N_SLOTS):
                sl = slice(h * HEAD_DIM, (h + 1) * HEAD_DIM)
                if b == 0:
                    k = jnp.concatenate([kp_ref[:, sl], kc_ref[:ATTN_BLOCK, sl]], axis=0)
                    v = jnp.concatenate([vp_ref[:, sl], vc_ref[:ATTN_BLOCK, sl]], axis=0)
                else:
                    k = kc_ref[(b - 1) * ATTN_BLOCK:(b + 1) * ATTN_BLOCK, sl]
                    v = vc_ref[(b - 1) * ATTN_BLOCK:(b + 1) * ATTN_BLOCK, sl]
                sc = lax.dot_general(q_ref[rows, sl], k, (((1,), (1,)), ((), ())), preferred_element_type=F32) * SCALE
                sc = jnp.where(valid, sc + SLOPES[gi][h] * negdist, NEG)
                m = jnp.max(sc, axis=-1, keepdims=True)
                p = jnp.exp(sc - m)
                l = jnp.sum(p, axis=-1, keepdims=True)
                o = jnp.dot((p * (1.0 / l)).astype(BF16), v, preferred_element_type=F32)
                o_ref[rows, sl] = o.astype(BF16)
                stat = jnp.where(lane == h, m + jnp.log(l), stat)
            lse_ref[rows, :] = stat

    return pl.pallas_call(
        body, name=f"attn_fwd_d{d}", grid=(d, nq),
        in_specs=[spec(0), spec(1, True), spec(1), spec(2, True), spec(2)],
        out_specs=[pl.BlockSpec((Q_ROWS, D_MODEL), lambda r, i: (r * nq + i, 0)),
                   pl.BlockSpec((Q_ROWS, STAT_W), lambda r, i: (r * nq + i, 0))],
        out_shape=[jax.ShapeDtypeStruct((s, D_MODEL), BF16), jax.ShapeDtypeStruct((s, STAT_W), F32)],
        compiler_params=_cp(("parallel", "parallel"), 32),
    )(z, z, z, z, z)


def _attn_bwd(z, dyc, ltot, dst, dz, gi):
    s = z.shape[0]
    d = DILATIONS[gi]
    nb = s // d // ATTN_BLOCK
    nq = nb // Q_BLOCKS
    n_steps = d * nq

    def rev(cb, width=D_MODEL, prev=False):
        if prev:
            return pl.BlockSpec((ATTN_BLOCK, width), lambda r, n: (r * nb + jnp.maximum(Q_BLOCKS * (nq - 1 - n) - 1, 0), cb))
        return pl.BlockSpec((Q_ROWS, width), lambda r, n: (r * nq + nq - 1 - n, cb))

    def body(q_ref, kp_ref, kc_ref, vp_ref, vc_ref, dy_ref, l_ref, d_ref, dz_in, dz_out, dq_s, dk_s, dv_s, ck_s, cv_s, sems):
        del dz_in
        r = pl.program_id(0)
        n = pl.program_id(1)
        i = nq - 1 - n
        step = r * nq + n

        def out_copy(src, which):
            rows = pl.ds(pl.multiple_of((r * nq + i) * Q_ROWS, Q_ROWS), Q_ROWS)
            return pltpu.make_async_copy(src, dz_out.at[rows, pl.ds((which * 3 + gi) * D_MODEL, D_MODEL)], sems.at[which])

        copies = [out_copy(dq_s, 0), out_copy(dk_s, 1), out_copy(dv_s, 2)]

        @pl.when(step > 0)
        def _():
            for cp in copies:
                cp.wait()

        @pl.when(n == 0)
        def _():
            ck_s[...] = jnp.zeros_like(ck_s)
            cv_s[...] = jnp.zeros_like(cv_s)

        row = lax.broadcasted_iota(jnp.int32, (Q_ROWS, Q_ROWS + ATTN_BLOCK), 0)
        col = lax.broadcasted_iota(jnp.int32, (Q_ROWS, Q_ROWS + ATTN_BLOCK), 1)
        steps = row + ATTN_BLOCK - col
        valid = (steps >= 0) & (steps <= ATTN_BLOCK) & ((i > 0) | (col >= ATTN_BLOCK))
        negdist = -(steps * d).astype(F32)
        nt = (((1,), (1,)), ((), ()))
        tn = (((0,), (0,)), ((), ()))
        for h in range(N_SLOTS):
            sl = slice(h * HEAD_DIM, (h + 1) * HEAD_DIM)
            q = q_ref[:, sl]
            k = jnp.concatenate([kp_ref[:, sl], kc_ref[:, sl]], axis=0)
            v = jnp.concatenate([vp_ref[:, sl], vc_ref[:, sl]], axis=0)
            dy = dy_ref[:, sl]
            sc = lax.dot_general(q, k, nt, preferred_element_type=F32) * SCALE + SLOPES[gi][h] * negdist
            p = jnp.where(valid, jnp.exp(sc - l_ref[:, h:h + 1]), 0.0)
            dp = lax.dot_general(dy, v, nt, preferred_element_type=F32)
            ds = (p * (dp - d_ref[:, h:h + 1])).astype(BF16)
            dq_s[:, sl] = (jnp.dot(ds, k, preferred_element_type=F32) * SCALE).astype(BF16)
            dk = lax.dot_general(ds, q, tn, preferred_element_type=F32) * SCALE
            dv = lax.dot_general(p.astype(BF16), dy, tn, preferred_element_type=F32)
            dk_s[:Q_ROWS - ATTN_BLOCK, sl] = dk[ATTN_BLOCK:Q_ROWS].astype(BF16)
            dv_s[:Q_ROWS - ATTN_BLOCK, sl] = dv[ATTN_BLOCK:Q_ROWS].astype(BF16)
            dk_s[Q_ROWS - ATTN_BLOCK:, sl] = (ck_s[:, sl] + dk[Q_ROWS:]).astype(BF16)
            dv_s[Q_ROWS - ATTN_BLOCK:, sl] = (cv_s[:, sl] + dv[Q_ROWS:]).astype(BF16)
            ck_s[:, sl] = dk[:ATTN_BLOCK]
            cv_s[:, sl] = dv[:ATTN_BLOCK]

        for cp in copies:
            cp.start()

        @pl.when(step == n_steps - 1)
        def _():
            for cp in copies:
                cp.wait()

    stage = pltpu.VMEM((Q_ROWS, D_MODEL), BF16)
    carry = pltpu.VMEM((ATTN_BLOCK, D_MODEL), F32)
    return pl.pallas_call(
        body, name=f"attn_bwd_d{d}", grid=(d, nq),
        in_specs=[rev(gi), rev(3 + gi, prev=True), rev(3 + gi), rev(6 + gi, prev=True), rev(6 + gi),
                  rev(0), rev(0, STAT_W), rev(0, STAT_W), ANY],
        out_specs=ANY,
        out_shape=jax.ShapeDtypeStruct((s, ODD_IN), BF16),
        scratch_shapes=[stage, stage, stage, carry, carry, pltpu.SemaphoreType.DMA((3,))],
        input_output_aliases={8: 0},
        compiler_params=_cp(("arbitrary", "arbitrary"), 32),
    )(z, z, z, z, z, dyc, ltot, dst, dz)


def _odd_mix_fwd(z, os_, lses, cw):
    s = z.shape[0]

    def body(o0, o1, o2, l0, l1, l2, gc_ref, db_ref, dc_ref, dx_ref, gd_ref, hc_ref, hx_ref, cw_ref, y_ref, yc_ref, lt_ref,
             lt4_ref, lt16_ref, scr_o, scr_o2, scr_l):
        i = pl.program_id(0)
        _from_classes(scr_l, l1, 4)
        lse1 = _get(scr_l)
        _from_classes(scr_l, l2, 16)
        ls = [l0[...], lse1, _get(scr_l)]
        lmax = jnp.maximum(jnp.maximum(ls[0], ls[1]), ls[2])
        es = [jnp.exp(l - lmax) for l in ls]
        den = es[0] + es[1] + es[2]
        alpha = [e / den for e in es]
        ltot = lmax + jnp.log(den)
        lt_ref[...] = ltot
        _put(scr_l, ltot)
        _to_classes(lt4_ref, scr_l, 4, F32)
        _to_classes(lt16_ref, scr_l, 16, F32)
        _from_classes(scr_o, o1, 4)
        _from_classes(scr_o2, o2, 16)
        for h in range(N_SLOTS):
            sl = slice(h * HEAD_DIM, (h + 1) * HEAD_DIM)
            yc = (alpha[0][:, h:h + 1] * o0[:, sl].astype(F32) + alpha[1][:, h:h + 1] * scr_o[h]
                  + alpha[2][:, h:h + 1] * scr_o2[h])
            yc_ref[:, sl] = yc.astype(BF16)
            gc = gc_ref[:, sl].astype(F32)
            y_ref[:, sl] = (yc * (gc * _sig(gc))).astype(BF16)
            zc = dc_ref[:, sl].astype(F32) * dx_ref[:, sl].astype(F32)
            halo = jnp.where(i > 0, hc_ref[:, sl].astype(F32) * hx_ref[:, sl].astype(F32), 0.0)
            ext = jnp.concatenate([halo, zc], axis=0)
            z1 = pltpu.roll(ext, 1, 0)[HALO:]
            z2 = pltpu.roll(ext, 2, 0)[HALO:]
            conv = cw_ref[0:1, sl] * z2 + cw_ref[1:2, sl] * z1 + cw_ref[2:3, sl] * zc
            gd = gd_ref[:, sl].astype(F32)
            y_ref[:, D_MODEL + h * HEAD_DIM:D_MODEL + (h + 1) * HEAD_DIM] = (
                db_ref[:, sl].astype(F32) * conv * (gd * _sig(gd))).astype(BF16)

    row = pl.BlockSpec((TM, D_MODEL), lambda i: (i, 0))
    stat = pl.BlockSpec((TM, STAT_W), lambda i: (i, 0))
    y, ycr, lt, lt4, lt16 = pl.pallas_call(
        body, name="odd_mix_fwd", grid=(s // TM,),
        in_specs=[row, _class_spec(4, TM, D_MODEL), _class_spec(16, TM, D_MODEL),
                  stat, _class_spec(4, TM, STAT_W), _class_spec(16, TM, STAT_W),
                  _zcol(9), _zcol(10), _zcol(11), _zcol(12), _zcol(13), _prev_halo(11), _prev_halo(12), _full((3, D_MODEL))],
        out_specs=[pl.BlockSpec((TM, 2 * D_MODEL), lambda i: (i, 0)), row, stat, _class_spec(4, TM, STAT_W),
                   _class_spec(16, TM, STAT_W)],
        out_shape=[jax.ShapeDtypeStruct((s, 2 * D_MODEL), BF16), jax.ShapeDtypeStruct((s, D_MODEL), BF16),
                   jax.ShapeDtypeStruct((s, STAT_W), F32), jax.ShapeDtypeStruct((4, s // 4, STAT_W), F32),
                   jax.ShapeDtypeStruct((16, s // 16, STAT_W), F32)],
        scratch_shapes=[_token_scratch(TM, D_MODEL), _token_scratch(TM, D_MODEL), _token_scratch(TM, STAT_W)],
        compiler_params=_cp(("parallel",), 48),
    )(os_[0], _class_major(os_[1], 4), _class_major(os_[2], 16), lses[0], _class_major(lses[1], 4),
      _class_major(lses[2], 16), z, z, z, z, z, z, z, cw)
    return y, ycr, [lt, lt4.reshape(s, STAT_W), lt16.reshape(s, STAT_W)]


def _odd_mix_bwd(dy, z, ycr, cw):
    s = z.shape[0]
    n_tiles = s // TM
    rest = ODD_IN - QKV_BLOCKS * D_MODEL

    def body(dy_ref, yc_ref, gc_ref, db_ref, dc_ref, dx_ref, gd_ref, hc_ref, hx_ref, dyn_ref, dbn_ref, gdn_ref, cw_ref,
             dz_ref, dyc_ref, dyc4_ref, dyc16_ref, dd_ref, dd4_ref, dd16_ref, dcw_ref, stage, sem, scr_o, scr_l):
        i = pl.program_id(0)
        out = pltpu.make_async_copy(
            stage, dz_ref.at[pl.ds(pl.multiple_of(i * TM, TM), TM), pl.ds(QKV_BLOCKS * D_MODEL, rest)], sem)

        @pl.when(i > 0)
        def _():
            out.wait()

        @pl.when(i == 0)
        def _():
            dcw_ref[...] = jnp.zeros_like(dcw_ref)

        lane = lax.broadcasted_iota(jnp.int32, (TM, STAT_W), 1)
        stat = jnp.zeros((TM, STAT_W), F32)
        nrow = TM + HALO
        for h in range(N_SLOTS):
            sl = slice(h * HEAD_DIM, (h + 1) * HEAD_DIM)
            sd = slice(D_MODEL + h * HEAD_DIM, D_MODEL + (h + 1) * HEAD_DIM)
            dyc_in = dy_ref[:, sl].astype(F32)
            gc = gc_ref[:, sl].astype(F32)
            sg = _sig(gc)
            yc = yc_ref[:, sl].astype(F32)
            dyc = dyc_in * (gc * sg)
            dyc_ref[:, sl] = dyc.astype(BF16)
            scr_o[h] = dyc
            stage[:, sl] = (dyc_in * yc * (sg * (1.0 + gc * (1.0 - sg)))).astype(BF16)
            stat = jnp.where(lane == h, jnp.sum(dyc * yc, axis=-1, keepdims=True), stat)
            dc = dc_ref[:, sl].astype(F32)
            dx = dx_ref[:, sl].astype(F32)
            zc = dc * dx
            halo = jnp.where(i > 0, hc_ref[:, sl].astype(F32) * hx_ref[:, sl].astype(F32), 0.0)
            ext = jnp.concatenate([halo, zc], axis=0)
            z1 = pltpu.roll(ext, 1, 0)[HALO:]
            z2 = pltpu.roll(ext, 2, 0)[HALO:]
            w0, w1, w2 = cw_ref[0:1, sl], cw_ref[1:2, sl], cw_ref[2:3, sl]
            conv = w0 * z2 + w1 * z1 + w2 * zc
            gd = gd_ref[:, sl].astype(F32)
            sg = _sig(gd)
            sgd = gd * sg
            db = db_ref[:, sl].astype(F32)
            dyd = dy_ref[:, sd].astype(F32)
            dconv = dyd * db * sgd
            gdn = gdn_ref[:, sl].astype(F32)
            dconv_n = jnp.where(i < n_tiles - 1,
                                dyn_ref[:, sl].astype(F32) * dbn_ref[:, sl].astype(F32) * (gdn * _sig(gdn)), 0.0)
            extn = jnp.concatenate([dconv, dconv_n], axis=0)
            dzc = w2 * dconv + w1 * pltpu.roll(extn, nrow - 1, 0)[:TM] + w0 * pltpu.roll(extn, nrow - 2, 0)[:TM]
            stage[:, sd] = (dyd * conv * sgd).astype(BF16)
            stage[:, 2 * D_MODEL + h * HEAD_DIM:2 * D_MODEL + (h + 1) * HEAD_DIM] = (dzc * dx).astype(BF16)
            stage[:, 3 * D_MODEL + h * HEAD_DIM:3 * D_MODEL + (h + 1) * HEAD_DIM] = (dzc * dc).astype(BF16)
            stage[:, 4 * D_MODEL + h * HEAD_DIM:4 * D_MODEL + (h + 1) * HEAD_DIM] = (
                dyd * db * conv * (sg * (1.0 + gd * (1.0 - sg)))).astype(BF16)
            for tap, shifted in enumerate((z2, z1, zc)):
                dcw_ref[tap:tap + 1, sl] += jnp.sum(dconv * shifted, axis=0, keepdims=True)
        _to_classes(dyc4_ref, scr_o, 4, BF16)
        _to_classes(dyc16_ref, scr_o, 16, BF16)
        dd_ref[...] = stat
        _put(scr_l, stat)
        _to_classes(dd4_ref, scr_l, 4, F32)
        _to_classes(dd16_ref, scr_l, 16, F32)

        out.start()

        @pl.when(i == n_tiles - 1)
        def _():
            out.wait()

    row = pl.BlockSpec((TM, D_MODEL), lambda i: (i, 0))
    stat = pl.BlockSpec((TM, STAT_W), lambda i: (i, 0))
    dz, dyc, dyc4, dyc16, dd, dd4, dd16, g_conv = pl.pallas_call(
        body, name="odd_mix_bwd", grid=(n_tiles,),
        in_specs=[pl.BlockSpec((TM, 2 * D_MODEL), lambda i: (i, 0)), row, _zcol(9), _zcol(10), _zcol(11), _zcol(12), _zcol(13),
                  _prev_halo(11), _prev_halo(12), _next_halo(1, s), _next_halo(10, s), _next_halo(13, s), _full((3, D_MODEL))],
        out_specs=[ANY, row, _class_spec(4, TM, D_MODEL), _class_spec(16, TM, D_MODEL),
                   stat, _class_spec(4, TM, STAT_W), _class_spec(16, TM, STAT_W), _full((3, D_MODEL))],
        out_shape=[jax.ShapeDtypeStruct((s, ODD_IN), BF16), jax.ShapeDtypeStruct((s, D_MODEL), BF16),
                   jax.ShapeDtypeStruct((4, s // 4, D_MODEL), BF16), jax.ShapeDtypeStruct((16, s // 16, D_MODEL), BF16),
                   jax.ShapeDtypeStruct((s, STAT_W), F32), jax.ShapeDtypeStruct((4, s // 4, STAT_W), F32),
                   jax.ShapeDtypeStruct((16, s // 16, STAT_W), F32), jax.ShapeDtypeStruct((3, D_MODEL), F32)],
        scratch_shapes=[pltpu.VMEM((TM, rest), BF16), pltpu.SemaphoreType.DMA(()), _token_scratch(TM, D_MODEL),
                        _token_scratch(TM, STAT_W)],
        compiler_params=_cp(("arbitrary",), 48),
    )(dy, ycr, z, z, z, z, z, z, z, dy, z, z, cw)
    dyc = [dyc, dyc4.reshape(s, D_MODEL), dyc16.reshape(s, D_MODEL)]
    dd = [dd, dd4.reshape(s, STAT_W), dd16.reshape(s, STAT_W)]
    return dz, dyc, dd, g_conv


def _cols_of_order(order):
    if order == 0:
        return (lambda j: jnp.where(j < 3, 3 * j, j + 6)), 8
    return (lambda j: 3 * j + order), 3


class _Hooks:
    def before_even(self):
        return None

    def odd_weights(self, w, x1):
        return w

    def odd_grads_ready(self, g_w_in_o, g_w_out_o):
        return None

    def even_mix_done(self, dz_e):
        return None

    def backward_done(self, dx0):
        return None


def _local_step(x, target, w, hooks=_Hooks()):
    tril = jnp.tril(jnp.ones((CHUNK, CHUNK), bool))
    wt = jnp.where(tril[None], w["ws"], 0.0).astype(BF16)
    wtt = jnp.swapaxes(wt, 1, 2)
    bs = w["bs"].reshape(4, CHUNK, 1)

    h_e = _rms_fwd("rms_fwd_even", x, w["even_norm"], after=hooks.before_even())
    z_e = _mm_nn("even_in_proj", h_e, w["w_in_e"], IN_ROWS, 1280, BF16)
    y_e = _even_mix_fwd(z_e, w["pool_w"], w["pool_scale"], wt, bs)
    x1 = _mm_nn("even_out_proj", y_e, w["w_out_e"], MM_ROWS, 1024, F32, resid=x)
    w = hooks.odd_weights(w, x1)
    h_o = _rms_fwd_orders("rms_fwd_odd", x1, w["odd_norm"])
    z_o = None
    for o in range(3):
        cols, n_cols = _cols_of_order(o)
        z_o = _mm_nn(f"odd_in_proj_o{o}", h_o[o], w["w_in_o"], IN_ROWS, D_MODEL, BF16, col_map=cols, n_cols=n_cols,
                     into=z_o)
    att = [_attn_fwd(z_o, gi) for gi in range(3)]
    y_o, ycr, ltot = _odd_mix_fwd(z_o, [a[0] for a in att], [a[1] for a in att], w["conv_w"])
    dx2, loss8, g_final = _out_proj_loss(y_o, w["w_out_o"], x1, w["final_norm"], target)

    g_w_out_o = _mm_tn("odd_out_proj_dw", y_o, dx2, 1024, DW_OUT_TOKENS)
    dy_o = _mm_nt("odd_out_proj_dy", dx2, w["w_out_o"], MM_ROWS, 1024, BF16)
    dz_o, dyc, dst, g_conv = _odd_mix_bwd(dy_o, z_o, ycr, w["conv_w"])
    for gi in range(3):
        dz_o = _attn_bwd(z_o, dyc[gi], ltot[gi], dst[gi], dz_o, gi)
    g_w_in_o, dh_o = None, []
    for o in range(3):
        cols, n_cols = _cols_of_order(o)
        g_w_in_o = _mm_tn(f"odd_in_proj_dw_o{o}", h_o[o], dz_o, D_MODEL, DW_IN_TOKENS, col_map=cols, n_cols=n_cols,
                          into=g_w_in_o)
        dh_o.append(_mm_nt(f"odd_in_proj_dh_o{o}", dz_o, w["w_in_o"], IN_ROWS, D_MODEL, BF16, k_map=cols, nk=n_cols))
    dx1, g_odd_norm = _rms_bwd("rms_bwd_odd", dh_o[0], x1, w["odd_norm"], dx2, dh4=dh_o[1], dh16=dh_o[2])
    after = hooks.odd_grads_ready(g_w_in_o, g_w_out_o)
    g_w_out_e = _mm_tn("even_out_proj_dw", y_e, dx1, 1024, DW_OUT_TOKENS, after=after)
    dy_e = _mm_nt("even_out_proj_dy", dx1, w["w_out_e"], MM_ROWS, 1024, BF16)
    dz_e, g_pw, g_ps, g_ws, g_bs = _even_mix_bwd(dy_e, z_e, w["pool_w"], w["pool_scale"], wt, wtt, bs)
    after = hooks.even_mix_done(dz_e)
    g_w_in_e = _mm_tn("even_in_proj_dw", h_e, dz_e, 1280, DW_IN_TOKENS, after=after)
    dh_e = _mm_nt("even_in_proj_dh", dz_e, w["w_in_e"], MM_ROWS, 2560, F32)
    dx0, g_even_norm = _rms_bwd("rms_bwd_even", dh_e, x, w["even_norm"], dx1)
    hooks.backward_done(dx0)

    grads = dict(w_in_e=g_w_in_e, pool_w=g_pw, w_out_e=g_w_out_e, w_in_o=g_w_in_o, w_out_o=g_w_out_o,
                 even_norm=g_even_norm, pool_scale=g_ps, ws=g_ws, bs=g_bs[:, :4].T, final_norm=g_final,
                 odd_norm=g_odd_norm, conv_w=g_conv)
    return loss8[0, 0], dx0, grads


class _Big(NamedTuple):
    name: str
    full: tuple
    haxis: int
    kaxis: int
    sub: int


BIGS = (
    _Big("w_in_e", (1024, 5120), 0, 1, 2),
    _Big("pool_w", (4, 256, 256), 0, 1, 1),
    _Big("w_out_e", (2048, 1024), 1, 0, 1),
    _Big("w_in_o", (1024, 14336), 0, 1, 4),
    _Big("w_out_o", (2048, 1024), 1, 0, 1),
)
N_BIG = len(BIGS)


def _shape(b, half=False, shard=False):
    return tuple(n // (2 if (half and ax == b.haxis) else 1) // (4 if (shard and ax == b.kaxis) else 1)
                 for ax, n in enumerate(b.full))


def _at(ref, b, h=None, k=None):
    idx = []
    for ax, n in enumerate(b.full):
        if ax == b.haxis and h is not None:
            idx.append(pl.ds(h * (n // 2), n // 2))
        elif ax == b.kaxis and k is not None:
            idx.append(pl.ds(k * (n // 4), n // 4))
        else:
            idx.append(slice(None))
    return ref.at[tuple(idx)]


def _place():
    x, y, c = lax.axis_index("x"), lax.axis_index("y"), lax.axis_index("c")
    chips = [(1 - x, y), (x, 1 - y), (1 - x, 1 - y)]
    return x, y, c, 2 * x + y, chips, [2 * cx + cy for cx, cy in chips]


def _piece_shape(b):
    return (4, 2) + _shape(b, half=True, shard=True)


def _gather_weights(bigs, shards, tiny):
    nb = len(bigs)

    def body(*refs):
        ins, tiny_in = refs[:nb], refs[nb]
        outs, tiny_out = refs[nb + 1:2 * nb + 1], refs[2 * nb + 1]
        send, recv, loc = refs[2 * nb + 2:]
        x, y, c, k_me, chips, ks = _place()
        sib = (x, y, 1 - c)

        def rc(src, dst, sem, to):
            return pltpu.make_async_remote_copy(src_ref=src, dst_ref=dst, send_sem=send.at[sem], recv_sem=recv.at[sem],
                                                device_id=to, device_id_type=MESH)

        own = pltpu.make_async_copy(tiny_in, tiny_out.at[k_me], loc)
        own.start()
        sends = []
        for j, chip in enumerate(chips):
            for a, b in enumerate(bigs):
                sends.append(rc(_at(ins[a], b, h=c), outs[a].at[k_me, c], 6 * a + j, (*chip, c)))
            sends.append(rc(tiny_in, tiny_out.at[k_me], 6 * nb + j, (*chip, c)))
        for cp in sends:
            cp.start()
        for j in range(3):
            for a in range(nb):
                piece = outs[a].at[ks[j], c]
                rc(piece, piece, 6 * a + j, sib).wait_recv()
                fwd = rc(piece, piece, 6 * a + 3 + j, sib)
                fwd.start()
                sends.append(fwd)
            rc(tiny_in, tiny_out.at[ks[j]], 6 * nb + j, sib).wait_recv()
        for j in range(3):
            for a in range(nb):
                piece = outs[a].at[ks[j], 1 - c]
                rc(piece, piece, 6 * a + 3 + j, sib).wait_recv()
        for cp in sends:
            cp.wait_send()
        own.wait()

    n_sem = 6 * nb + 3
    return pl.pallas_call(
        body, name="gather_even_weights",
        in_specs=[ANY] * (nb + 1), out_specs=[ANY] * (nb + 1),
        out_shape=[jax.ShapeDtypeStruct(_piece_shape(b), BF16) for b in bigs]
        + [jax.ShapeDtypeStruct((4,) + tiny.shape, F32)],
        scratch_shapes=[pltpu.SemaphoreType.DMA((n_sem,)), pltpu.SemaphoreType.DMA((n_sem,)), pltpu.SemaphoreType.DMA(())],
    )(*shards, tiny)


def _assemble(b, pieces, shard, k_arr):
    blk = _blk(b)

    def body(k_ref, p_ref, s_ref, o_ref):
        mine = pl.program_id(0) == k_ref[0]

        @pl.when(mine)
        def _():
            o_ref[...] = s_ref[...]

        @pl.when(jnp.logical_not(mine))
        def _():
            o_ref[...] = p_ref[...]

    return pl.pallas_call(
        body, name=f"assemble_{b.name}",
        grid_spec=pltpu.PrefetchScalarGridSpec(
            num_scalar_prefetch=1, grid=(4, 2, b.sub),
            in_specs=[pl.BlockSpec((None, None) + blk, lambda k, h, st, k_ref: (k, h) + _bidx(b, 0, 0, st)),
                      pl.BlockSpec(blk, lambda k, h, st, k_ref: _bidx(b, h, 0, st))],
            out_specs=pl.BlockSpec(blk, lambda k, h, st, k_ref: _bidx(b, h, k, st))),
        out_shape=jax.ShapeDtypeStruct(b.full, BF16),
        compiler_params=_cp(("arbitrary", "arbitrary", "arbitrary"), 32),
    )(k_arr, pieces, shard)


def _copies_to_chips(bigs):
    def copies(srcs, lands, send, recv, waiting=False):
        _, _, c, k_me, chips, _ = _place()
        return [pltpu.make_async_remote_copy(
                    src_ref=_at(srcs[a], b, h=c), dst_ref=lands[a].at[k_me, c], send_sem=send.at[3 * a + j],
                    recv_sem=recv.at[3 * a + j], device_id=(*chips[j], c), device_id_type=MESH)
                for j in range(3) for a, b in enumerate(bigs)]
    return copies


def _copies_swap_halves(bigs):
    def copies(srcs, lands, send, recv, waiting=False):
        x, y, c, _, _, _ = _place()
        return [pltpu.make_async_remote_copy(
                    src_ref=_at(srcs[a], b, h=1 - c), dst_ref=lands[a], send_sem=send.at[a], recv_sem=recv.at[a],
                    device_id=(x, y, 1 - c), device_id_type=MESH)
                for a, b in enumerate(bigs)]
    return copies


def _copies_partials(bigs):
    def copies(srcs, lands, send, recv, waiting=False):
        _, _, c, _, chips, ks = _place()
        return [pltpu.make_async_remote_copy(
                    src_ref=_at(srcs[a], b, k=ks[j]), dst_ref=lands[a].at[j], send_sem=send.at[3 * a + j],
                    recv_sem=recv.at[3 * a + j], device_id=(*chips[j], c), device_id_type=MESH)
                for j in range(3) for a, b in enumerate(bigs)]
    return copies


def _exchange(name, srcs, land_shapes, copies_of, n_copies):
    ns = len(srcs)

    def body(*refs):
        copies = copies_of(refs[:ns], refs[ns:ns + len(land_shapes)], refs[-2], refs[-1])
        for cp in copies:
            cp.start()
        for cp in copies:
            cp.wait()

    return pl.pallas_call(
        body, name=name, in_specs=[ANY] * ns, out_specs=[ANY] * len(land_shapes), out_shape=land_shapes,
        scratch_shapes=[pltpu.SemaphoreType.DMA((n_copies,)), pltpu.SemaphoreType.DMA((n_copies,))],
    )(*srcs)


HBM = pl.BlockSpec(memory_space=pltpu.HBM)
SEM = pl.BlockSpec(memory_space=pltpu.SEMAPHORE)
SIDE_EFFECT = pltpu.SideEffectType.DATAFLOW_SIDE_EFFECTING


def _in_hbm(a):
    return pltpu.with_memory_space_constraint(a, pltpu.HBM)


def _exchange_start(name, srcs, land_shapes, copies_of, n_copies, after=None):
    ns, nl = len(srcs), len(land_shapes)
    lands = [lax.empty(sh.shape, sh.dtype) for sh in land_shapes]
    tail = [] if after is None else [after]
    n_in = ns + nl + len(tail)

    def body(*refs):
        send, recv, token = refs[n_in], refs[n_in + 1], refs[-1]
        for cp in copies_of(refs[:ns], refs[ns:ns + nl], send, recv):
            cp.start()
        token[...] = jnp.zeros_like(token)

    thru = [pltpu.HBM(a.shape, a.dtype) for a in (*srcs, *lands)]
    send, recv, *bufs, token = pl.pallas_call(
        body, name=name,
        out_shape=(pltpu.SemaphoreType.DMA((n_copies,)), pltpu.SemaphoreType.DMA((n_copies,)), *thru,
                   jax.ShapeDtypeStruct((8, 128), F32)),
        in_specs=[HBM] * (ns + nl) + [ANY] * len(tail),
        out_specs=(SEM, SEM, *([HBM] * (ns + nl)), pl.BlockSpec(memory_space=pltpu.VMEM)),
        input_output_aliases={i: 2 + i for i in range(ns + nl)},
        compiler_params=pltpu.CompilerParams(has_side_effects=SIDE_EFFECT),
    )(*[_in_hbm(a) for a in (*srcs, *lands)], *tail)
    return (send, recv, bufs, ns), token


def _exchange_wait(name, state, copies_of, after):
    send, recv, bufs, ns = state
    n = len(bufs)

    def body(*refs):
        ins = refs[:n]
        for cp in copies_of(ins[:ns], ins[ns:], refs[n], refs[n + 1], waiting=True):
            cp.wait_send()
            cp.wait_recv()

    out = pl.pallas_call(
        body, name=name, out_shape=tuple(pltpu.HBM(a.shape, a.dtype) for a in bufs),
        in_specs=[HBM] * n + [SEM, SEM, ANY], out_specs=tuple([HBM] * n),
        input_output_aliases={i: i for i in range(n)},
        compiler_params=pltpu.CompilerParams(has_side_effects=SIDE_EFFECT),
    )(*bufs, send, recv, after)
    return list(out[:ns]), list(out[ns:])


def _finish_gather(bigs, pieces):
    nb = len(bigs)

    def body(*refs):
        outs, send, recv = refs[nb:2 * nb], refs[2 * nb], refs[2 * nb + 1]
        x, y, c, _, _, ks = _place()
        fwd = [pltpu.make_async_remote_copy(
                   src_ref=outs[a].at[ks[j], c], dst_ref=outs[a].at[ks[j], c], send_sem=send.at[3 * a + j],
                   recv_sem=recv.at[3 * a + j], device_id=(x, y, 1 - c), device_id_type=MESH)
               for j in range(3) for a in range(nb)]
        for cp in fwd:
            cp.start()
        for cp in fwd:
            cp.wait()

    return pl.pallas_call(
        body, name="gather_odd_finish", in_specs=[ANY] * nb, out_specs=[ANY] * nb,
        out_shape=[jax.ShapeDtypeStruct(_piece_shape(b), BF16) for b in bigs],
        scratch_shapes=[pltpu.SemaphoreType.DMA((3 * nb,)), pltpu.SemaphoreType.DMA((3 * nb,))],
        input_output_aliases={a: a for a in range(nb)},
    )(*pieces)


def _blk(b):
    win = _shape(b, half=True, shard=True)
    return (win[0] // b.sub,) + win[1:]


def _bidx(b, h, k, st):
    idx = [0] * len(b.full)
    idx[b.haxis] = h
    idx[b.kaxis] = k
    idx[0] = idx[0] * b.sub + st
    return tuple(idx)


def _chip_sum(b, g, got, c_arr):
    blk = _blk(b)

    def body(c_ref, g_ref, r_ref, o_ref):
        del c_ref
        o_ref[...] = (g_ref[...] + r_ref[...]).astype(BF16)

    half = pl.BlockSpec(blk, lambda k, st, c_ref: _bidx(b, 0, k, st))
    return pl.pallas_call(
        body, name=f"rs_chip_sum_{b.name}",
        grid_spec=pltpu.PrefetchScalarGridSpec(
            num_scalar_prefetch=1, grid=(4, b.sub),
            in_specs=[pl.BlockSpec(blk, lambda k, st, c_ref: _bidx(b, c_ref[0], k, st)), half], out_specs=half),
        out_shape=jax.ShapeDtypeStruct(_shape(b, half=True), BF16),
        compiler_params=_cp(("arbitrary", "arbitrary"), 40),
    )(c_arr, g, got)


def _half_shapes(bigs):
    return [jax.ShapeDtypeStruct(_shape(b, half=True), F32) for b in bigs]


def _partial_shapes(bigs):
    return [jax.ShapeDtypeStruct((3,) + _shape(b, half=True, shard=True), BF16) for b in bigs]


def _shard_sum(b, mine, got, ck_arr):
    blk = _blk(b)

    def body(ck_ref, m_ref, r0, r1, r2, o_ref):
        del ck_ref
        o_ref[...] = (m_ref[...].astype(F32) + r0[...].astype(F32)) + (r1[...].astype(F32) + r2[...].astype(F32))

    def peer(j):
        return pl.BlockSpec((None,) + blk, lambda st, ck: (j,) + _bidx(b, 0, 0, st))

    return pl.pallas_call(
        body, name=f"rs_shard_sum_{b.name}",
        grid_spec=pltpu.PrefetchScalarGridSpec(
            num_scalar_prefetch=1, grid=(b.sub,),
            in_specs=[pl.BlockSpec(blk, lambda st, ck: _bidx(b, 0, ck[1], st)), peer(0), peer(1), peer(2)],
            out_specs=pl.BlockSpec(blk, lambda st, ck: _bidx(b, ck[0], 0, st))),
        out_shape=jax.ShapeDtypeStruct(_shape(b, shard=True), F32),
        compiler_params=_cp(("arbitrary",), 40),
    )(ck_arr, mine, got, got, got)


def _share_halves(gs):
    def body(*refs):
        ins, outs, send, recv = refs[:N_BIG], refs[N_BIG:2 * N_BIG], refs[2 * N_BIG], refs[2 * N_BIG + 1]
        del ins
        x, y, c, _, _, _ = _place()
        copies = [pltpu.make_async_remote_copy(src_ref=_at(outs[a], b, h=c), dst_ref=_at(outs[a], b, h=c),
                                               send_sem=send.at[a], recv_sem=recv.at[a], device_id=(x, y, 1 - c),
                                               device_id_type=MESH)
                  for a, b in enumerate(BIGS)]
        for cp in copies:
            cp.start()
        for cp in copies:
            cp.wait()

    return pl.pallas_call(
        body, name="rs_share_halves", in_specs=[ANY] * N_BIG, out_specs=[ANY] * N_BIG,
        out_shape=[jax.ShapeDtypeStruct(_shape(b, shard=True), F32) for b in BIGS],
        scratch_shapes=[pltpu.SemaphoreType.DMA((N_BIG,)), pltpu.SemaphoreType.DMA((N_BIG,))],
        input_output_aliases={a: a for a in range(N_BIG)},
    )(*gs)


def _gather_small(block):
    m_per, n = block.shape

    def body(x_ref, out_ref, send_sems, recv_sems, local_sem):
        x, y, c = lax.axis_index("x"), lax.axis_index("y"), lax.axis_index("c")
        me, sibling = (x, y, c), (x, y, 1 - c)
        chips = [(1 - x, y), (x, 1 - y), (1 - x, 1 - y)]

        def rows(px, py, pc):
            return out_ref.at[pl.ds((4 * px + 2 * py + pc) * m_per, m_per), :]

        def copy(k, blk, to, src=None):
            return pltpu.make_async_remote_copy(
                src_ref=rows(*blk) if src is None else src, dst_ref=rows(*blk), send_sem=send_sems.at[k],
                recv_sem=recv_sems.at[k], device_id=to, device_id_type=MESH)

        mine = pltpu.make_async_copy(x_ref, rows(*me), local_sem)
        mine.start()
        first = [copy(0, me, sibling, src=x_ref)]
        first += [copy(1 + j, me, (*chip, c), src=x_ref) for j, chip in enumerate(chips)]
        for cp in first:
            cp.start()
        passed = [copy(4 + j, (*chip, c), sibling) for j, chip in enumerate(chips)]
        for j, chip in enumerate(chips):
            copy(1 + j, (*chip, c), me).wait_recv()
            passed[j].start()
        copy(0, sibling, me).wait_recv()
        for j, chip in enumerate(chips):
            copy(4 + j, (*chip, 1 - c), me).wait_recv()
        for cp in first + passed:
            cp.wait_send()
        mine.wait()

    return pl.pallas_call(
        body, name="gather_small_grads",
        out_shape=jax.ShapeDtypeStruct((8 * m_per, n), block.dtype),
        in_specs=[pl.BlockSpec(memory_space=pltpu.VMEM)], out_specs=pl.BlockSpec(memory_space=pltpu.VMEM),
        scratch_shapes=[pltpu.SemaphoreType.DMA((7,)), pltpu.SemaphoreType.DMA((7,)), pltpu.SemaphoreType.DMA],
    )(block)


def _sum_small(stack):
    _, m_per, n = stack.shape

    def body(x_ref, o_ref):
        acc = x_ref[0]
        for dev in range(1, 8):
            acc = acc + x_ref[dev]
        o_ref[...] = acc

    return pl.pallas_call(body, name="sum_small_grads", out_shape=jax.ShapeDtypeStruct((m_per, n), F32))(stack)


def _adam_update(w_ref, g_ref, m_ref, v_ref, d_ref, mo_ref, vo_ref):
    gg = g_ref[...]
    mn = ADAM_B1 * m_ref[...] + (1.0 - ADAM_B1) * gg
    vn = ADAM_B2 * v_ref[...] + (1.0 - ADAM_B2) * (gg * gg)
    m_hat = mn / (1.0 - ADAM_B1 ** ADAM_STEP)
    v_hat = vn / (1.0 - ADAM_B2 ** ADAM_STEP)
    d_ref[...] = -ADAM_LR * (m_hat / (jnp.sqrt(v_hat) + ADAM_EPS) + ADAM_WD * w_ref[...])
    mo_ref[...] = mn
    vo_ref[...] = vn


def _adamw(name, w, g, m, v, rows):
    shape = w.shape

    def body(*refs):
        _adam_update(*refs)

    spec = pl.BlockSpec((rows,) + shape[1:], lambda i: (i,) + (0,) * (len(shape) - 1))
    return pl.pallas_call(
        body, name=name, grid=(shape[0] // rows,), in_specs=[spec] * 4, out_specs=[spec] * 3,
        out_shape=[jax.ShapeDtypeStruct(shape, F32)] * 3, compiler_params=_cp(("parallel",), 48),
    )(w, g, m, v)


def _adamw_small(ws, gs, ms, vs):
    n = len(ws)

    def body(*refs):
        for a in range(n):
            _adam_update(*[refs[q * n + a] for q in range(7)])

    outs = pl.pallas_call(
        body, name="adamw_small", out_shape=[jax.ShapeDtypeStruct(w.shape, F32) for w in ws] * 3,
    )(*ws, *gs, *ms, *vs)
    return outs[:n], outs[n:2 * n], outs[2 * n:]


ADAM_ROWS = dict(w_in_e=256, pool_w=4, w_out_e=256, w_in_o=128, w_out_o=256)


def _pack(parts, rows):
    flat = jnp.concatenate([p.reshape(-1).astype(F32) for p in parts])
    return jnp.pad(flat, (0, rows * 128 - flat.shape[0])).reshape(rows, 128)


def _unpack(buf, shapes):
    flat = buf.reshape(-1)
    out, off = [], 0
    for shp in shapes:
        n = 1
        for dim in shp:
            n *= dim
        out.append(flat[off:off + n].reshape(shp))
        off += n
    return out


WEIGHTS = ("even_norm", "even_w_in", "even_pool_w", "even_pool_scale", "even_ws", "even_bs", "even_w_out", "odd_norm",
           "odd_w_in", "odd_conv_w", "odd_w_out", "final_norm")
BIG_OF = dict(w_in_e="even_w_in", pool_w="even_pool_w", w_out_e="even_w_out", w_in_o="odd_w_in", w_out_o="odd_w_out")
SMALL = ("even_norm", "even_pool_scale", "even_ws", "even_bs", "final_norm", "odd_norm", "odd_conv_w")
SMALL_GRAD_ROWS = 576


def kernel(x, even_norm, even_w_in, even_pool_w, even_pool_scale, even_ws, even_bs, even_w_out, odd_norm, odd_w_in, odd_conv_w, odd_w_out, final_norm, loss_target, m_even_norm, m_even_w_in, m_even_pool_w, m_even_pool_scale, m_even_ws, m_even_bs, m_even_w_out, m_odd_norm, m_odd_w_in, m_odd_conv_w, m_odd_w_out, m_final_norm, v_even_norm, v_even_w_in, v_even_pool_w, v_even_pool_scale, v_even_ws, v_even_bs, v_even_w_out, v_odd_norm, v_odd_w_in, v_odd_conv_w, v_odd_w_out, v_final_norm):
    wv = dict(zip(WEIGHTS, (even_norm, even_w_in, even_pool_w, even_pool_scale, even_ws, even_bs, even_w_out, odd_norm,
                            odd_w_in, odd_conv_w, odd_w_out, final_norm)))
    mv = dict(zip(WEIGHTS, (m_even_norm, m_even_w_in, m_even_pool_w, m_even_pool_scale, m_even_ws, m_even_bs,
                            m_even_w_out, m_odd_norm, m_odd_w_in, m_odd_conv_w, m_odd_w_out, m_final_norm)))
    vv = dict(zip(WEIGHTS, (v_even_norm, v_even_w_in, v_even_pool_w, v_even_pool_scale, v_even_ws, v_even_bs,
                            v_even_w_out, v_odd_norm, v_odd_w_in, v_odd_conv_w, v_odd_w_out, v_final_norm)))
    c = lax.axis_index("c")
    k_me = 2 * lax.axis_index("x") + lax.axis_index("y")

    c_arr = jnp.reshape(c, (1,)).astype(jnp.int32)
    ck_arr = jnp.stack([c, k_me]).astype(jnp.int32)
    even_bigs, odd_bigs = BIGS[:3], BIGS[3:]

    shards = {b.name: wv[BIG_OF[b.name]][0].astype(BF16) for b in BIGS}
    tiny = jnp.concatenate([odd_conv_w[0], odd_norm], axis=0)
    *pieces_even, tiny_all = _gather_weights(even_bigs, [shards[b.name] for b in even_bigs], tiny)
    tiny_full = jnp.transpose(tiny_all, (1, 0, 2)).reshape(4, D_MODEL)
    to_chips, swap_odd, partials_odd = _copies_to_chips(odd_bigs), _copies_swap_halves(odd_bigs), _copies_partials(odd_bigs)
    gather_state, gather_token = _exchange_start(
        "gather_odd_start", [shards[b.name] for b in odd_bigs],
        [jax.ShapeDtypeStruct(_piece_shape(b), BF16) for b in odd_bigs], to_chips, 3 * len(odd_bigs), after=pieces_even[-1])
    k_arr = jnp.reshape(k_me, (1,)).astype(jnp.int32)
    w = {b.name: _assemble(b, p, shards[b.name], k_arr) for b, p in zip(even_bigs, pieces_even)}
    w.update(even_norm=even_norm, pool_scale=even_pool_scale, ws=even_ws[0], bs=even_bs[0],
             final_norm=final_norm.reshape(1, D_MODEL), conv_w=tiny_full[:3], odd_norm=tiny_full[3:4])

    class Hooks(_Hooks):
        def before_even(self):
            return gather_token

        def odd_weights(self, w, x1):
            srcs, lands = _exchange_wait("gather_odd_wait", gather_state, to_chips, after=x1)
            pieces = _finish_gather(odd_bigs, lands)
            return dict(w, **{b.name: _assemble(b, p, s, k_arr) for b, p, s in zip(odd_bigs, pieces, srcs)})

        def odd_grads_ready(self, g_w_in_o, g_w_out_o):
            self.swap, token = _exchange_start("rs_odd_swap_start", [g_w_in_o, g_w_out_o], _half_shapes(odd_bigs),
                                               swap_odd, len(odd_bigs))
            return token

        def even_mix_done(self, dz_e):
            grads, got = _exchange_wait("rs_odd_swap_wait", self.swap, swap_odd, after=dz_e)
            sums = [_chip_sum(b, g, r, c_arr) for b, g, r in zip(odd_bigs, grads, got)]
            self.partials, token = _exchange_start("rs_odd_partials_start", sums, _partial_shapes(odd_bigs), partials_odd,
                                                   3 * len(odd_bigs))
            return token

        def backward_done(self, dx0):
            self.sums, self.parts = _exchange_wait("rs_odd_partials_wait", self.partials, partials_odd, after=dx0)

    hooks = Hooks()
    loss, dx, g = _local_step(x[0], loss_target[0], w, hooks)
    loss = lax.psum(loss, ("x", "y", "c"))

    got = _exchange("rs_even_swap", [g[b.name] for b in even_bigs], _half_shapes(even_bigs), _copies_swap_halves(even_bigs),
                    len(even_bigs))
    sums = [_chip_sum(b, g[b.name], r, c_arr) for b, r in zip(even_bigs, got)]
    parts = _exchange("rs_even_partials", sums, _partial_shapes(even_bigs), _copies_partials(even_bigs), 3 * len(even_bigs))
    halves = [_shard_sum(b, sm, p, ck_arr) for b, sm, p in zip(BIGS, sums + hooks.sums, list(parts) + hooks.parts)]
    g_shard = dict(zip((b.name for b in BIGS), _share_halves(halves)))

    small_g = _pack([g["even_norm"], g["pool_scale"], g["ws"], g["bs"], g["final_norm"], g["odd_norm"], g["conv_w"]],
                    SMALL_GRAD_ROWS)
    small_g = _sum_small(_gather_small(small_g).reshape(8, SMALL_GRAD_ROWS, 128))
    g_en, g_ps, g_ws, g_bs, g_fn, g_on, g_cw = _unpack(
        small_g, [(1, D_MODEL), (1, D_MODEL), (1, 4, CHUNK, CHUNK), (1, 4, CHUNK), (D_MODEL,), (1, D_MODEL), (1, 3, D_MODEL)])
    g_on = lax.dynamic_slice(g_on, (0, k_me * 256), (1, 256))
    g_cw = lax.dynamic_slice(g_cw, (0, 0, k_me * 256), (1, 3, 256))
    grad = dict(even_norm=g_en, even_pool_scale=g_ps, even_ws=g_ws, even_bs=g_bs, final_norm=g_fn, odd_norm=g_on,
                odd_conv_w=g_cw)
    for b in BIGS:
        grad[BIG_OF[b.name]] = g_shard[b.name][None]

    delta, new_m, new_v = {}, {}, {}
    for b in BIGS:
        n = BIG_OF[b.name]
        d_, m_, v_ = _adamw(f"adamw_{b.name}", wv[n][0], g_shard[b.name], mv[n][0], vv[n][0], ADAM_ROWS[b.name])
        delta[n], new_m[n], new_v[n] = d_[None], m_[None], v_[None]
    flat = [(wv[n].size // wv[n].shape[-1], wv[n].shape[-1]) for n in SMALL]
    outs = _adamw_small(*[[src[n].reshape(shp) for n, shp in zip(SMALL, flat)] for src in (wv, grad, mv, vv)])
    for dst, arrs in zip((delta, new_m, new_v), outs):
        for n, arr in zip(SMALL, arrs):
            dst[n] = arr.reshape(wv[n].shape)

    return (loss, dx[None], *[grad[n] for n in WEIGHTS], *[delta[n] for n in WEIGHTS], *[new_m[n] for n in WEIGHTS],
            *[new_v[n] for n in WEIGHTS])
```

```python
from typing import NamedTuple

import jax
import jax.numpy as jnp
from jax import lax
from jax.experimental import pallas as pl
from jax.experimental.pallas import tpu as pltpu

F32, BF16 = jnp.float32, jnp.bfloat16

D_MODEL = 1024
EPS = 1e-6
NEG = -1e30
POOL_SIZES = (2, 4, 8, 16)
GROUP_W = 256
CHUNK = 128
DILATIONS = (1, 4, 16)
N_SLOTS = 8
HEAD_DIM = 128
ATTN_BLOCK = 128
SCALE = HEAD_DIM ** -0.5
EVEN_IN = 5120
ODD_IN = 14336
QKV_BLOCKS = 9
ODD_BLOCKS = ODD_IN // D_MODEL
SLOPES = tuple(tuple(2.0 ** (-8.0 * (g * N_SLOTS + s + 1) / (3 * N_SLOTS)) for s in range(N_SLOTS)) for g in range(3))

ADAM_LR, ADAM_B1, ADAM_B2, ADAM_EPS, ADAM_WD, ADAM_STEP = 0.001, 0.9, 0.999, 1e-08, 0.01, 10

HALO = 16
TS = 512
TM = 256
TMO = 256
MM_ROWS = 1024
IN_ROWS = 2048
DW_IN_TOKENS = 2048
DW_OUT_TOKENS = 1024
MIB = 1 << 20
MESH = pl.DeviceIdType.MESH
ANY = pl.BlockSpec(memory_space=pl.ANY)


def _cp(sem, vmem_mib):
    return pltpu.CompilerParams(dimension_semantics=sem, vmem_limit_bytes=vmem_mib * MIB)


def _sig(x):
    return 0.5 * jnp.tanh(0.5 * x) + 0.5


def _win_sum(e, w, forward):
    n = e.shape[0]
    k = 1
    while k < w:
        e = e + pltpu.roll(e, (n - k) if forward else k, 0)
        k *= 2
    return e


def _mm_nn(name, a, b, tm, tn, out_dtype, resid=None, col_map=None, n_cols=None, into=None):
    m, k = a.shape
    n = b.shape[1]
    if col_map is None:
        col_map, n_cols = (lambda j: j), n // tn

    def body(*refs):
        a_ref, b_ref = refs[0], refs[1]
        acc = jnp.dot(a_ref[...].astype(BF16), b_ref[...], preferred_element_type=F32)
        if resid is not None:
            acc = acc + refs[2][...]
        o_ref = refs[-1]
        o_ref[...] = acc.astype(out_dtype)

    in_specs = [pl.BlockSpec((tm, k), lambda j, i: (i, 0)), pl.BlockSpec((k, tn), lambda j, i: (0, col_map(j)))]
    args = [a, b]
    if resid is not None:
        in_specs.append(pl.BlockSpec((tm, tn), lambda j, i: (i, col_map(j))))
        args.append(resid)
    aliases = {}
    if into is not None:
        aliases = {len(args): 0}
        in_specs.append(ANY)
        args.append(into)
    return pl.pallas_call(
        body, name=name, grid=(n_cols, m // tm), in_specs=in_specs,
        out_specs=pl.BlockSpec((tm, tn), lambda j, i: (i, col_map(j))),
        out_shape=jax.ShapeDtypeStruct((m, n), out_dtype), input_output_aliases=aliases,
        compiler_params=_cp(("parallel", "parallel"), 48),
    )(*args)


def _mm_nt(name, a, b, tm, tk, out_dtype, k_map=None, nk=None):
    m, k = a.shape
    n = b.shape[0]
    if k_map is None:
        k_map, nk = (lambda kk: kk), k // tk

    def body(a_ref, b_ref, o_ref, acc_ref):
        kk = pl.program_id(1)
        p = lax.dot_general(a_ref[...].astype(BF16), b_ref[...], (((1,), (1,)), ((), ())), preferred_element_type=F32)
        if nk == 1:
            o_ref[...] = p.astype(out_dtype)
        else:
            @pl.when(kk == 0)
            def _():
                acc_ref[...] = p

            @pl.when(kk > 0)
            def _():
                acc_ref[...] += p

            @pl.when(kk == nk - 1)
            def _():
                o_ref[...] = acc_ref[...].astype(out_dtype)

    return pl.pallas_call(
        body, name=name, grid=(m // tm, nk),
        in_specs=[pl.BlockSpec((tm, tk), lambda i, kk: (i, k_map(kk))), pl.BlockSpec((n, tk), lambda i, kk: (0, k_map(kk)))],
        out_specs=pl.BlockSpec((tm, n), lambda i, kk: (i, 0)),
        out_shape=jax.ShapeDtypeStruct((m, n), out_dtype),
        scratch_shapes=[pltpu.VMEM((tm, n) if nk > 1 else (8, 128), F32)],
        compiler_params=_cp(("parallel", "arbitrary"), 56),
    )(a, b)


def _mm_tn(name, a, g, tn, ts, col_map=None, n_cols=None, into=None, after=None):
    s, ka = a.shape
    n = g.shape[1]
    if col_map is None:
        col_map, n_cols = (lambda j: j), n // tn

    def body(a_ref, g_ref, *rest):
        o_ref = rest[-1]
        st = pl.program_id(1)
        p = lax.dot_general(a_ref[...], g_ref[...].astype(BF16), (((0,), (0,)), ((), ())), preferred_element_type=F32)

        @pl.when(st == 0)
        def _():
            o_ref[...] = p

        @pl.when(st > 0)
        def _():
            o_ref[...] += p

    in_specs = [pl.BlockSpec((ts, ka), lambda j, st: (st, 0)), pl.BlockSpec((ts, tn), lambda j, st: (st, col_map(j)))]
    args = [a, g]
    aliases = {}
    if into is not None:
        aliases = {2: 0}
        in_specs.append(ANY)
        args.append(into)
    if after is not None:
        in_specs.append(ANY)
        args.append(after)
    return pl.pallas_call(
        body, name=name, grid=(n_cols, s // ts), in_specs=in_specs,
        out_specs=pl.BlockSpec((ka, tn), lambda j, st: (0, col_map(j))),
        out_shape=jax.ShapeDtypeStruct((ka, n), F32), input_output_aliases=aliases,
        compiler_params=_cp(("parallel", "arbitrary"), 56),
    )(*args)


def _rms_fwd(name, x, g, after=None):
    s = x.shape[0]

    def body(x_ref, g_ref, *rest):
        xf = x_ref[...]
        r = lax.rsqrt(jnp.mean(xf * xf, axis=-1, keepdims=True) + EPS)
        rest[-1][...] = (xf * r * g_ref[...]).astype(BF16)

    row = pl.BlockSpec((TS, D_MODEL), lambda i: (i, 0))
    in_specs, args = [row, pl.BlockSpec((1, D_MODEL), lambda i: (0, 0))], [x, g]
    if after is not None:
        in_specs.append(ANY)
        args.append(after)
    return pl.pallas_call(
        body, name=name, grid=(s // TS,), in_specs=in_specs, out_specs=row,
        out_shape=jax.ShapeDtypeStruct((s, D_MODEL), BF16), compiler_params=_cp(("parallel",), 32),
    )(*args)


def _class_major(a, d):
    return a.reshape(d, a.shape[0] // d, a.shape[1])


def _class_spec(d, tile, width):
    return pl.BlockSpec((d, tile // d, width), lambda i: (0, i, 0))


LANES = 128


def _token_scratch(tile, width):
    return pltpu.VMEM((width // LANES, tile, LANES), F32)


def _put(scr, val):
    for c in range(scr.shape[0]):
        scr[c] = val[:, c * LANES:(c + 1) * LANES]


def _get(scr):
    return jnp.concatenate([scr[c] for c in range(scr.shape[0])], axis=1)


def _to_classes(ref3, scr, d, dtype):
    n = ref3.shape[1]
    for c in range(scr.shape[0]):
        for r in range(d):
            ref3[r, :, c * LANES:(c + 1) * LANES] = scr.at[c][pl.ds(r, n, stride=d), :].astype(dtype)


def _from_classes(scr, ref3, d):
    n = ref3.shape[1]
    for c in range(scr.shape[0]):
        for r in range(d):
            scr.at[c][pl.ds(r, n, stride=d), :] = ref3[r, :, c * LANES:(c + 1) * LANES].astype(F32)


def _rms_fwd_orders(name, x, g):
    s = x.shape[0]

    def body(x_ref, g_ref, h_ref, h4_ref, h16_ref, scr):
        xf = x_ref[...]
        r = lax.rsqrt(jnp.mean(xf * xf, axis=-1, keepdims=True) + EPS)
        h = xf * r * g_ref[...]
        h_ref[...] = h.astype(BF16)
        _put(scr, h)
        _to_classes(h4_ref, scr, 4, BF16)
        _to_classes(h16_ref, scr, 16, BF16)

    row = pl.BlockSpec((TS, D_MODEL), lambda i: (i, 0))
    h, h4, h16 = pl.pallas_call(
        body, name=name, grid=(s // TS,), in_specs=[row, pl.BlockSpec((1, D_MODEL), lambda i: (0, 0))],
        out_specs=[row, _class_spec(4, TS, D_MODEL), _class_spec(16, TS, D_MODEL)],
        out_shape=[jax.ShapeDtypeStruct((s, D_MODEL), BF16), jax.ShapeDtypeStruct((4, s // 4, D_MODEL), BF16),
                   jax.ShapeDtypeStruct((16, s // 16, D_MODEL), BF16)],
        scratch_shapes=[_token_scratch(TS, D_MODEL)],
        compiler_params=_cp(("parallel",), 32),
    )(x, g)
    return h, h4.reshape(s, D_MODEL), h16.reshape(s, D_MODEL)


def _rms_bwd(name, dh, x, g, dres, dh4=None, dh16=None):
    s = x.shape[0]
    extra = dh4 is not None

    def body(dh_ref, x_ref, g_ref, dres_ref, *rest):
        if extra:
            dh4_ref, dh16_ref, dx_ref, dg_ref, scr = rest
        else:
            dx_ref, dg_ref = rest
        xf = x_ref[...]
        r = lax.rsqrt(jnp.mean(xf * xf, axis=-1, keepdims=True) + EPS)
        xh = xf * r
        dhf = dh_ref[...].astype(F32)
        if extra:
            _from_classes(scr, dh4_ref, 4)
            dhf = dhf + _get(scr)
            _from_classes(scr, dh16_ref, 16)
            dhf = dhf + _get(scr)
        dxh = dhf * g_ref[...]
        dx_ref[...] = dres_ref[...] + r * (dxh - xh * jnp.mean(dxh * xh, axis=-1, keepdims=True))
        part = jnp.sum(dhf * xh, axis=0, keepdims=True)

        @pl.when(pl.program_id(0) == 0)
        def _():
            dg_ref[...] = part

        @pl.when(pl.program_id(0) > 0)
        def _():
            dg_ref[...] += part

    row = pl.BlockSpec((TS, D_MODEL), lambda i: (i, 0))
    vec = pl.BlockSpec((1, D_MODEL), lambda i: (0, 0))
    in_specs, args, scratch = [row, row, vec, row], [dh, x, g, dres], []
    if extra:
        in_specs += [_class_spec(4, TS, D_MODEL), _class_spec(16, TS, D_MODEL)]
        args += [_class_major(dh4, 4), _class_major(dh16, 16)]
        scratch = [_token_scratch(TS, D_MODEL)]
    return pl.pallas_call(
        body, name=name, grid=(s // TS,), in_specs=in_specs, out_specs=[row, vec],
        out_shape=[jax.ShapeDtypeStruct((s, D_MODEL), F32), jax.ShapeDtypeStruct((1, D_MODEL), F32)],
        scratch_shapes=scratch, compiler_params=_cp(("arbitrary",), 40),
    )(*args)


def _out_proj_loss(y, w_out, resid, g, target):
    s, k = y.shape

    def body(y_ref, w_ref, r_ref, g_ref, t_ref, dx_ref, loss_ref, dg_ref):
        xf = jnp.dot(y_ref[...], w_ref[...], preferred_element_type=F32) + r_ref[...]
        gg = g_ref[...]
        r = lax.rsqrt(jnp.mean(xf * xf, axis=-1, keepdims=True) + EPS)
        xh = xf * r
        e = xh * gg - t_ref[...]
        dy = e * (1.0 / D_MODEL)
        dxh = dy * gg
        dx_ref[...] = r * (dxh - xh * jnp.mean(dxh * xh, axis=-1, keepdims=True))
        lpart = 0.5 * jnp.sum(jnp.mean(e * e, axis=-1, keepdims=True), axis=0, keepdims=True)
        lpart = jnp.broadcast_to(lpart, (8, 128))
        gpart = jnp.sum(dy * xh, axis=0, keepdims=True)

        @pl.when(pl.program_id(0) == 0)
        def _():
            loss_ref[...] = lpart
            dg_ref[...] = gpart

        @pl.when(pl.program_id(0) > 0)
        def _():
            loss_ref[...] += lpart
            dg_ref[...] += gpart

    row = pl.BlockSpec((TS, D_MODEL), lambda i: (i, 0))
    vec = pl.BlockSpec((1, D_MODEL), lambda i: (0, 0))
    return pl.pallas_call(
        body, name="odd_out_proj_loss", grid=(s // TS,),
        in_specs=[pl.BlockSpec((TS, k), lambda i: (i, 0)), pl.BlockSpec((k, D_MODEL), lambda i: (0, 0)), row, vec, row],
        out_specs=[row, pl.BlockSpec((8, 128), lambda i: (0, 0)), vec],
        out_shape=[jax.ShapeDtypeStruct((s, D_MODEL), F32), jax.ShapeDtypeStruct((8, 128), F32),
                   jax.ShapeDtypeStruct((1, D_MODEL), F32)],
        compiler_params=_cp(("arbitrary",), 48),
    )(y, w_out, resid, g, target)


def _zcol(c, tm=TM):
    return pl.BlockSpec((tm, D_MODEL), lambda i, c=c: (i, c))


def _prev_halo(c, tm=TM):
    return pl.BlockSpec((HALO, D_MODEL), lambda i, c=c: (jnp.maximum(i * (tm // HALO) - 1, 0), c))


def _next_halo(c, n_rows, tm=TM):
    last = n_rows // HALO - 1
    return pl.BlockSpec((HALO, D_MODEL), lambda i, c=c: (jnp.minimum((i + 1) * (tm // HALO), last), c))


def _full(shape):
    return pl.BlockSpec(shape, lambda i: (0,) * len(shape))


def _inv_count(first_row, n, w):
    t = first_row + lax.broadcasted_iota(jnp.int32, (n, 1), 0)
    return 1.0 / jnp.minimum(t + 1, w).astype(F32)


def _even_mix_fwd(z, pw, ps, wt, bs):
    s = z.shape[0]

    def body(a_ref, ga_ref, u_ref, v_ref, gb_ref, halo_ref, pw_ref, ps_ref, wt_ref, bs_ref, y_ref):
        i = pl.program_id(0)
        a = a_ref[...].astype(F32)
        halo = jnp.where(i > 0, halo_ref[...].astype(F32), 0.0)
        ext = jnp.concatenate([halo, a], axis=0)
        ga = ga_ref[...].astype(F32)
        sga = ga * _sig(ga)
        for g, w in enumerate(POOL_SIZES):
            cs = slice(g * GROUP_W, (g + 1) * GROUP_W)
            win = _win_sum(ext[:, cs], w, False)[HALO:]
            pooled = win * _inv_count(i * TM, TM, w) - a[:, cs]
            mixed = jnp.dot(pooled.astype(BF16), pw_ref[g], preferred_element_type=F32)
            y_ref[:, cs] = (mixed * ps_ref[:, cs] * sga[:, cs]).astype(BF16)
        gb = gb_ref[...].astype(F32)
        gate = u_ref[...].astype(F32) * (gb * _sig(gb))
        for ch in range(TM // CHUNK):
            rs = slice(ch * CHUNK, (ch + 1) * CHUNK)
            for g in range(4):
                cs = slice(g * GROUP_W, (g + 1) * GROUP_W)
                mixb = jnp.dot(wt_ref[g], v_ref[rs, cs], preferred_element_type=F32) + bs_ref[g]
                y_ref[rs, D_MODEL + g * GROUP_W:D_MODEL + (g + 1) * GROUP_W] = (gate[rs, cs] * mixb).astype(BF16)

    return pl.pallas_call(
        body, name="even_mix_fwd", grid=(s // TM,),
        in_specs=[_zcol(0), _zcol(1), _zcol(2), _zcol(3), _zcol(4), _prev_halo(0),
                  _full((4, GROUP_W, GROUP_W)), _full((1, D_MODEL)), _full((4, CHUNK, CHUNK)), _full((4, CHUNK, 1))],
        out_specs=pl.BlockSpec((TM, 2 * D_MODEL), lambda i: (i, 0)),
        out_shape=jax.ShapeDtypeStruct((s, 2 * D_MODEL), BF16),
        compiler_params=_cp(("parallel",), 48),
    )(z, z, z, z, z, z, pw, ps, wt, bs)


def _even_mix_bwd(dy, z, pw, ps, wt, wtt, bs, after=None):
    s = z.shape[0]
    n_tiles = s // TM
    tail_specs, tail_args = ([ANY], [after]) if after is not None else ([], [])

    def body(dy_ref, a_ref, ga_ref, u_ref, v_ref, gb_ref, halo_ref, dyn_ref, gan_ref, pw_ref, ps_ref, wt_ref, wtt_ref,
             bs_ref, *rest):
        dz_ref, dpw_ref, dps_ref, dws_ref, dbs_ref = rest[-5:]
        i = pl.program_id(0)

        @pl.when(i == 0)
        def _():
            dpw_ref[...] = jnp.zeros_like(dpw_ref)
            dps_ref[...] = jnp.zeros_like(dps_ref)
            dws_ref[...] = jnp.zeros_like(dws_ref)
            dbs_ref[...] = jnp.zeros_like(dbs_ref)

        a = a_ref[...].astype(F32)
        halo = jnp.where(i > 0, halo_ref[...].astype(F32), 0.0)
        ext = jnp.concatenate([halo, a], axis=0)
        ga = ga_ref[...].astype(F32)
        sg = _sig(ga)
        sga = ga * sg
        dsga = sg * (1.0 + ga * (1.0 - sg))
        dya = dy_ref[:, :D_MODEL].astype(F32)
        gan = gan_ref[...].astype(F32)
        dmn_all = jnp.where(i < n_tiles - 1, dyn_ref[...].astype(F32) * ps_ref[...] * (gan * _sig(gan)), 0.0)
        for g, w in enumerate(POOL_SIZES):
            cs = slice(g * GROUP_W, (g + 1) * GROUP_W)
            inv = _inv_count(i * TM, TM, w)
            pooled = _win_sum(ext[:, cs], w, False)[HALO:] * inv - a[:, cs]
            pb = pooled.astype(BF16)
            mixed = jnp.dot(pb, pw_ref[g], preferred_element_type=F32)
            dyg = dya[:, cs]
            psg = ps_ref[:, cs]
            dm = (dyg * psg * sga[:, cs]).astype(BF16)
            dz_ref[:, D_MODEL + g * GROUP_W:D_MODEL + (g + 1) * GROUP_W] = (dyg * mixed * psg * dsga[:, cs]).astype(BF16)
            dps_ref[:, cs] += jnp.sum(dyg * mixed * sga[:, cs], axis=0, keepdims=True)
            dpw_ref[g] += lax.dot_general(pb, dm, (((0,), (0,)), ((), ())), preferred_element_type=F32)
            nt = (((1,), (1,)), ((), ()))
            dpool = lax.dot_general(dm, pw_ref[g], nt, preferred_element_type=F32)
            dpool_n = lax.dot_general(dmn_all[:, cs].astype(BF16), pw_ref[g], nt, preferred_element_type=F32)
            e = jnp.concatenate([dpool * inv, dpool_n * _inv_count((i + 1) * TM, HALO, w)], axis=0)
            dz_ref[:, cs] = (_win_sum(e, w, True)[:TM] - dpool).astype(BF16)

        gb = gb_ref[...].astype(F32)
        sg = _sig(gb)
        sgb = gb * sg
        dsgb = sg * (1.0 + gb * (1.0 - sg))
        u = u_ref[...].astype(F32)
        dyb = dy_ref[:, D_MODEL:].astype(F32)
        tril = lax.broadcasted_iota(jnp.int32, (CHUNK, CHUNK), 0) >= lax.broadcasted_iota(jnp.int32, (CHUNK, CHUNK), 1)
        lane = lax.broadcasted_iota(jnp.int32, (CHUNK, 128), 1)
        for ch in range(TM // CHUNK):
            rs = slice(ch * CHUNK, (ch + 1) * CHUNK)
            for g in range(4):
                cs = slice(g * GROUP_W, (g + 1) * GROUP_W)
                vb = v_ref[rs, cs]
                mixb = jnp.dot(wt_ref[g], vb, preferred_element_type=F32) + bs_ref[g]
                dyu = dyb[rs, cs] * u[rs, cs]
                dmix = dyu * sgb[rs, cs]
                dmb = dmix.astype(BF16)
                o = g * GROUP_W
                dz_ref[rs, 2 * D_MODEL + o:2 * D_MODEL + o + GROUP_W] = (dyb[rs, cs] * mixb * sgb[rs, cs]).astype(BF16)
                dz_ref[rs, 3 * D_MODEL + o:3 * D_MODEL + o + GROUP_W] = jnp.dot(
                    wtt_ref[g], dmb, preferred_element_type=F32).astype(BF16)
                dz_ref[rs, 4 * D_MODEL + o:4 * D_MODEL + o + GROUP_W] = (dyu * mixb * dsgb[rs, cs]).astype(BF16)
                dws = lax.dot_general(dmb, vb, (((1,), (1,)), ((), ())), preferred_element_type=F32)
                dws_ref[g] += jnp.where(tril, dws, 0.0)
                dbs_ref[...] += jnp.where(lane == g, jnp.sum(dmix, axis=1, keepdims=True), 0.0)

    return pl.pallas_call(
        body, name="even_mix_bwd", grid=(n_tiles,),
        in_specs=[pl.BlockSpec((TM, 2 * D_MODEL), lambda i: (i, 0)), _zcol(0), _zcol(1), _zcol(2), _zcol(3), _zcol(4),
                  _prev_halo(0), _next_halo(0, s), _next_halo(1, s),
                  _full((4, GROUP_W, GROUP_W)), _full((1, D_MODEL)), _full((4, CHUNK, CHUNK)), _full((4, CHUNK, CHUNK)),
                  _full((4, CHUNK, 1))] + tail_specs,
        out_specs=[pl.BlockSpec((TM, EVEN_IN), lambda i: (i, 0)), _full((4, GROUP_W, GROUP_W)), _full((1, D_MODEL)),
                   _full((4, CHUNK, CHUNK)), _full((CHUNK, 128))],
        out_shape=[jax.ShapeDtypeStruct((s, EVEN_IN), BF16), jax.ShapeDtypeStruct((4, GROUP_W, GROUP_W), F32),
                   jax.ShapeDtypeStruct((1, D_MODEL), F32), jax.ShapeDtypeStruct((4, CHUNK, CHUNK), F32),
                   jax.ShapeDtypeStruct((CHUNK, 128), F32)],
        compiler_params=_cp(("arbitrary",), 56),
    )(dy, z, z, z, z, z, z, dy, z, pw, ps, wt, wtt, bs, *tail_args)


STAT_W = 128
Q_BLOCKS = 2
Q_ROWS = Q_BLOCKS * ATTN_BLOCK


def _band(d):
    row = lax.broadcasted_iota(jnp.int32, (ATTN_BLOCK, 2 * ATTN_BLOCK), 0)
    col = lax.broadcasted_iota(jnp.int32, (ATTN_BLOCK, 2 * ATTN_BLOCK), 1)
    steps = row + ATTN_BLOCK - col
    return (steps >= 0) & (steps <= ATTN_BLOCK), col >= ATTN_BLOCK, -(steps * d).astype(F32)


def _attn_fwd(z, gi):
    s = z.shape[0]
    d = DILATIONS[gi]
    nb = s // d // ATTN_BLOCK
    nq = nb // Q_BLOCKS

    def spec(which, prev=False):
        cb = which * 3 + gi
        if prev:
            return pl.BlockSpec((ATTN_BLOCK, D_MODEL), lambda r, i: (r * nb + jnp.maximum(Q_BLOCKS * i - 1, 0), cb))
        return pl.BlockSpec((Q_ROWS, D_MODEL), lambda r, i: (r * nq + i, cb))

    def body(q_ref, kp_ref, kc_ref, vp_ref, vc_ref, o_ref, lse_ref):
        i = pl.program_id(1)
        inner, own, negdist = _band(d)
        lane = lax.broadcasted_iota(jnp.int32, (ATTN_BLOCK, STAT_W), 1)
        for b in range(Q_BLOCKS):
            rows = slice(b * ATTN_BLOCK, (b + 1) * ATTN_BLOCK)
            valid = (inner & ((i > 0) | own)) if b == 0 else inner
            stat = jnp.zeros((ATTN_BLOCK, STAT_W), F32)
            for h in range(N_SLOTS):
                sl = slice(h * HEAD_DIM, (h + 1) * HEAD_DIM)
                if b == 0:
                    k = jnp.concatenate([kp_ref[:, sl], kc_ref[:ATTN_BLOCK, sl]], axis=0)
                    v = jnp.concatenate([vp_ref[:, sl], vc_ref[:ATTN_BLOCK, sl]], axis=0)
                else:
                    k = kc_ref[(b - 1) * ATTN_BLOCK:(b + 1) * ATTN_BLOCK, sl]
                    v = vc_ref[(b - 1) * ATTN_BLOCK:(b + 1) * ATTN_BLOCK, sl]
                sc = lax.dot_general(q_ref[rows, sl], k, (((1,), (1,)), ((), ())), preferred_element_type=F32) * SCALE
                sc = jnp.where(valid, sc + SLOPES[gi][h] * negdist, NEG)
                m = jnp.max(sc, axis=-1, keepdims=True)
                p = jnp.exp(sc - m)
                l = jnp.sum(p, axis=-1, keepdims=True)
                o = jnp.dot((p * (1.0 / l)).astype(BF16), v, preferred_element_type=F32)
                o_ref[rows, sl] = o.astype(BF16)
                stat = jnp.where(lane == h, m + jnp.log(l), stat)
            lse_ref[rows, :] = stat

    return pl.pallas_call(
        body, name=f"attn_fwd_d{d}", grid=(d, nq),
        in_specs=[spec(0), spec(1, True), spec(1), spec(2, True), spec(2)],
        out_specs=[pl.BlockSpec((Q_ROWS, D_MODEL), lambda r, i: (r * nq + i, 0)),
                   pl.BlockSpec((Q_ROWS, STAT_W), lambda r, i: (r * nq + i, 0))],
        out_shape=[jax.ShapeDtypeStruct((s, D_MODEL), BF16), jax.ShapeDtypeStruct((s, STAT_W), F32)],
        compiler_params=_cp(("parallel", "parallel"), 32),
    )(z, z, z, z, z)


def _attn_bwd(z, dyc, ltot, dst, dz, gi):
    s = z.shape[0]
    d = DILATIONS[gi]
    nb = s // d // ATTN_BLOCK
    nq = nb // Q_BLOCKS
    n_steps = d * nq

    def rev(cb, width=D_MODEL, prev=False):
        if prev:
            return pl.BlockSpec((ATTN_BLOCK, width), lambda r, n: (r * nb + jnp.maximum(Q_BLOCKS * (nq - 1 - n) - 1, 0), cb))
        return pl.BlockSpec((Q_ROWS, width), lambda r, n: (r * nq + nq - 1 - n, cb))

    def body(q_ref, kp_ref, kc_ref, vp_ref, vc_ref, dy_ref, l_ref, d_ref, dz_in, dz_out, dq_s, dk_s, dv_s, ck_s, cv_s, sems):
        del dz_in
        r = pl.program_id(0)
        n = pl.program_id(1)
        i = nq - 1 - n
        step = r * nq + n

        def out_copy(src, which):
            rows = pl.ds(pl.multiple_of((r * nq + i) * Q_ROWS, Q_ROWS), Q_ROWS)
            return pltpu.make_async_copy(src, dz_out.at[rows, pl.ds((which * 3 + gi) * D_MODEL, D_MODEL)], sems.at[which])

        copies = [out_copy(dq_s, 0), out_copy(dk_s, 1), out_copy(dv_s, 2)]

        @pl.when(step > 0)
        def _():
            for cp in copies:
                cp.wait()

        @pl.when(n == 0)
        def _():
            ck_s[...] = jnp.zeros_like(ck_s)
            cv_s[...] = jnp.zeros_like(cv_s)

        row = lax.broadcasted_iota(jnp.int32, (Q_ROWS, Q_ROWS + ATTN_BLOCK), 0)
        col = lax.broadcasted_iota(jnp.int32, (Q_ROWS, Q_ROWS + ATTN_BLOCK), 1)
        steps = row + ATTN_BLOCK - col
        valid = (steps >= 0) & (steps <= ATTN_BLOCK) & ((i > 0) | (col >= ATTN_BLOCK))
        negdist = -(steps * d).astype(F32)
        nt = (((1,), (1,)), ((), ()))
        tn = (((0,), (0,)), ((), ()))
        for h in range(N_SLOTS):
            sl = slice(h * HEAD_DIM, (h + 1) * HEAD_DIM)
            q = q_ref[:, sl]
            k = jnp.concatenate([kp_ref[:, sl], kc_ref[:, sl]], axis=0)
            v = jnp.concatenate([vp_ref[:, sl], vc_ref[:, sl]], axis=0)
            dy = dy_ref[:, sl]
            sc = lax.dot_general(q, k, nt, preferred_element_type=F32) * SCALE + SLOPES[gi][h] * negdist
            p = jnp.where(valid, jnp.exp(sc - l_ref[:, h:h + 1]), 0.0)
            dp = lax.dot_general(dy, v, nt, preferred_element_type=F32)
            ds = (p * (dp - d_ref[:, h:h + 1])).astype(BF16)
            dq_s[:, sl] = (jnp.dot(ds, k, preferred_element_type=F32) * SCALE).astype(BF16)
            dk = lax.dot_general(ds, q, tn, preferred_element_type=F32) * SCALE
            dv = lax.dot_general(p.astype(BF16), dy, tn, preferred_element_type=F32)
            dk_s[:Q_ROWS - ATTN_BLOCK, sl] = dk[ATTN_BLOCK:Q_ROWS].astype(BF16)
            dv_s[:Q_ROWS - ATTN_BLOCK, sl] = dv[ATTN_BLOCK:Q_ROWS].astype(BF16)
            dk_s[Q_ROWS - ATTN_BLOCK:, sl] = (ck_s[:, sl] + dk[Q_ROWS:]).astype(BF16)
            dv_s[Q_ROWS - ATTN_BLOCK:, sl] = (cv_s[:, sl] + dv[Q_ROWS:]).astype(BF16)
            ck_s[:, sl] = dk[:ATTN_BLOCK]
            cv_s[:, sl] = dv[:ATTN_BLOCK]

        for cp in copies:
            cp.start()

        @pl.when(step == n_steps - 1)
        def _():
            for cp in copies:
                cp.wait()

    stage = pltpu.VMEM((Q_ROWS, D_MODEL), BF16)
    carry = pltpu.VMEM((ATTN_BLOCK, D_MODEL), F32)
    return pl.pallas_call(
        body, name=f"attn_bwd_d{d}", grid=(d, nq),
        in_specs=[rev(gi), rev(3 + gi, prev=True), rev(3 + gi), rev(6 + gi, prev=True), rev(6 + gi),
                  rev(0), rev(0, STAT_W), rev(0, STAT_W), ANY],
        out_specs=ANY,
        out_shape=jax.ShapeDtypeStruct((s, ODD_IN), BF16),
        scratch_shapes=[stage, stage, stage, carry, carry, pltpu.SemaphoreType.DMA((3,))],
        input_output_aliases={8: 0},
        compiler_params=_cp(("arbitrary", "arbitrary"), 32),
    )(z, z, z, z, z, dyc, ltot, dst, dz)


def _odd_mix_fwd(z, os_, lses, cw):
    s = z.shape[0]

    def body(o0, o1, o2, l0, l1, l2, gc_ref, db_ref, dc_ref, dx_ref, gd_ref, hc_ref, hx_ref, cw_ref, y_ref, yc_ref, lt_ref,
             lt4_ref, lt16_ref, scr_o, scr_o2, scr_l):
        i = pl.program_id(0)
        _from_classes(scr_l, l1, 4)
        lse1 = _get(scr_l)
        _from_classes(scr_l, l2, 16)
        ls = [l0[...], lse1, _get(scr_l)]
        lmax = jnp.maximum(jnp.maximum(ls[0], ls[1]), ls[2])
        es = [jnp.exp(l - lmax) for l in ls]
        den = es[0] + es[1] + es[2]
        alpha = [e / den for e in es]
        ltot = lmax + jnp.log(den)
        lt_ref[...] = ltot
        _put(scr_l, ltot)
        _to_classes(lt4_ref, scr_l, 4, F32)
        _to_classes(lt16_ref, scr_l, 16, F32)
        _from_classes(scr_o, o1, 4)
        _from_classes(scr_o2, o2, 16)
        for h in range(N_SLOTS):
            sl = slice(h * HEAD_DIM, (h + 1) * HEAD_DIM)
            yc = (alpha[0][:, h:h + 1] * o0[:, sl].astype(F32) + alpha[1][:, h:h + 1] * scr_o[h]
                  + alpha[2][:, h:h + 1] * scr_o2[h])
            yc_ref[:, sl] = yc.astype(BF16)
            gc = gc_ref[:, sl].astype(F32)
            y_ref[:, sl] = (yc * (gc * _sig(gc))).astype(BF16)
            zc = dc_ref[:, sl].astype(F32) * dx_ref[:, sl].astype(F32)
            halo = jnp.where(i > 0, hc_ref[:, sl].astype(F32) * hx_ref[:, sl].astype(F32), 0.0)
            ext = jnp.concatenate([halo, zc], axis=0)
            z1 = pltpu.roll(ext, 1, 0)[HALO:]
            z2 = pltpu.roll(ext, 2, 0)[HALO:]
            conv = cw_ref[0:1, sl] * z2 + cw_ref[1:2, sl] * z1 + cw_ref[2:3, sl] * zc
            gd = gd_ref[:, sl].astype(F32)
            y_ref[:, D_MODEL + h * HEAD_DIM:D_MODEL + (h + 1) * HEAD_DIM] = (
                db_ref[:, sl].astype(F32) * conv * (gd * _sig(gd))).astype(BF16)

    row = pl.BlockSpec((TMO, D_MODEL), lambda i: (i, 0))
    stat = pl.BlockSpec((TMO, STAT_W), lambda i: (i, 0))
    y, ycr, lt, lt4, lt16 = pl.pallas_call(
        body, name="odd_mix_fwd", grid=(s // TMO,),
        in_specs=[row, _class_spec(4, TMO, D_MODEL), _class_spec(16, TMO, D_MODEL),
                  stat, _class_spec(4, TMO, STAT_W), _class_spec(16, TMO, STAT_W),
                  _zcol(9, TMO), _zcol(10, TMO), _zcol(11, TMO), _zcol(12, TMO), _zcol(13, TMO), _prev_halo(11, TMO),
                  _prev_halo(12, TMO), _full((3, D_MODEL))],
        out_specs=[pl.BlockSpec((TMO, 2 * D_MODEL), lambda i: (i, 0)), row, stat, _class_spec(4, TMO, STAT_W),
                   _class_spec(16, TMO, STAT_W)],
        out_shape=[jax.ShapeDtypeStruct((s, 2 * D_MODEL), BF16), jax.ShapeDtypeStruct((s, D_MODEL), BF16),
                   jax.ShapeDtypeStruct((s, STAT_W), F32), jax.ShapeDtypeStruct((4, s // 4, STAT_W), F32),
                   jax.ShapeDtypeStruct((16, s // 16, STAT_W), F32)],
        scratch_shapes=[_token_scratch(TMO, D_MODEL), _token_scratch(TMO, D_MODEL), _token_scratch(TMO, STAT_W)],
        compiler_params=_cp(("parallel",), 48),
    )(os_[0], _class_major(os_[1], 4), _class_major(os_[2], 16), lses[0], _class_major(lses[1], 4),
      _class_major(lses[2], 16), z, z, z, z, z, z, z, cw)
    return y, ycr, [lt, lt4.reshape(s, STAT_W), lt16.reshape(s, STAT_W)]


def _odd_mix_bwd(dy, z, ycr, cw):
    s = z.shape[0]
    n_tiles = s // TMO
    rest = ODD_IN - QKV_BLOCKS * D_MODEL

    def body(dy_ref, yc_ref, gc_ref, db_ref, dc_ref, dx_ref, gd_ref, hc_ref, hx_ref, dyn_ref, dbn_ref, gdn_ref, cw_ref,
             dz_ref, dyc_ref, dyc4_ref, dyc16_ref, dd_ref, dd4_ref, dd16_ref, dcw_ref, stage, sem, scr_o, scr_l):
        i = pl.program_id(0)
        out = pltpu.make_async_copy(
            stage, dz_ref.at[pl.ds(pl.multiple_of(i * TMO, TMO), TMO), pl.ds(QKV_BLOCKS * D_MODEL, rest)], sem)

        @pl.when(i > 0)
        def _():
            out.wait()

        @pl.when(i == 0)
        def _():
            dcw_ref[...] = jnp.zeros_like(dcw_ref)

        lane = lax.broadcasted_iota(jnp.int32, (TMO, STAT_W), 1)
        stat = jnp.zeros((TMO, STAT_W), F32)
        nrow = TMO + HALO
        for h in range(N_SLOTS):
            sl = slice(h * HEAD_DIM, (h + 1) * HEAD_DIM)
            sd = slice(D_MODEL + h * HEAD_DIM, D_MODEL + (h + 1) * HEAD_DIM)
            dyc_in = dy_ref[:, sl].astype(F32)
            gc = gc_ref[:, sl].astype(F32)
            sg = _sig(gc)
            yc = yc_ref[:, sl].astype(F32)
            dyc = dyc_in * (gc * sg)
            dyc_ref[:, sl] = dyc.astype(BF16)
            scr_o[h] = dyc
            stage[:, sl] = (dyc_in * yc * (sg * (1.0 + gc * (1.0 - sg)))).astype(BF16)
            stat = jnp.where(lane == h, jnp.sum(dyc * yc, axis=-1, keepdims=True), stat)
            dc = dc_ref[:, sl].astype(F32)
            dx = dx_ref[:, sl].astype(F32)
            zc = dc * dx
            halo = jnp.where(i > 0, hc_ref[:, sl].astype(F32) * hx_ref[:, sl].astype(F32), 0.0)
            ext = jnp.concatenate([halo, zc], axis=0)
            z1 = pltpu.roll(ext, 1, 0)[HALO:]
            z2 = pltpu.roll(ext, 2, 0)[HALO:]
            w0, w1, w2 = cw_ref[0:1, sl], cw_ref[1:2, sl], cw_ref[2:3, sl]
            conv = w0 * z2 + w1 * z1 + w2 * zc
            gd = gd_ref[:, sl].astype(F32)
            sg = _sig(gd)
            sgd = gd * sg
            db = db_ref[:, sl].astype(F32)
            dyd = dy_ref[:, sd].astype(F32)
            dconv = dyd * db * sgd
            gdn = gdn_ref[:, sl].astype(F32)
            dconv_n = jnp.where(i < n_tiles - 1,
                                dyn_ref[:, sl].astype(F32) * dbn_ref[:, sl].astype(F32) * (gdn * _sig(gdn)), 0.0)
            extn = jnp.concatenate([dconv, dconv_n], axis=0)
            dzc = w2 * dconv + w1 * pltpu.roll(extn, nrow - 1, 0)[:TMO] + w0 * pltpu.roll(extn, nrow - 2, 0)[:TMO]
            stage[:, sd] = (dyd * conv * sgd).astype(BF16)
            stage[:, 2 * D_MODEL + h * HEAD_DIM:2 * D_MODEL + (h + 1) * HEAD_DIM] = (dzc * dx).astype(BF16)
            stage[:, 3 * D_MODEL + h * HEAD_DIM:3 * D_MODEL + (h + 1) * HEAD_DIM] = (dzc * dc).astype(BF16)
            stage[:, 4 * D_MODEL + h * HEAD_DIM:4 * D_MODEL + (h + 1) * HEAD_DIM] = (
                dyd * db * conv * (sg * (1.0 + gd * (1.0 - sg)))).astype(BF16)
            for tap, shifted in enumerate((z2, z1, zc)):
                dcw_ref[tap:tap + 1, sl] += jnp.sum(dconv * shifted, axis=0, keepdims=True)
        _to_classes(dyc4_ref, scr_o, 4, BF16)
        _to_classes(dyc16_ref, scr_o, 16, BF16)
        dd_ref[...] = stat
        _put(scr_l, stat)
        _to_classes(dd4_ref, scr_l, 4, F32)
        _to_classes(dd16_ref, scr_l, 16, F32)

        out.start()

        @pl.when(i == n_tiles - 1)
        def _():
            out.wait()

    row = pl.BlockSpec((TMO, D_MODEL), lambda i: (i, 0))
    stat = pl.BlockSpec((TMO, STAT_W), lambda i: (i, 0))
    dz, dyc, dyc4, dyc16, dd, dd4, dd16, g_conv = pl.pallas_call(
        body, name="odd_mix_bwd", grid=(n_tiles,),
        in_specs=[pl.BlockSpec((TMO, 2 * D_MODEL), lambda i: (i, 0)), row, _zcol(9, TMO), _zcol(10, TMO), _zcol(11, TMO),
                  _zcol(12, TMO), _zcol(13, TMO), _prev_halo(11, TMO), _prev_halo(12, TMO), _next_halo(1, s, TMO),
                  _next_halo(10, s, TMO), _next_halo(13, s, TMO), _full((3, D_MODEL))],
        out_specs=[ANY, row, _class_spec(4, TMO, D_MODEL), _class_spec(16, TMO, D_MODEL),
                   stat, _class_spec(4, TMO, STAT_W), _class_spec(16, TMO, STAT_W), _full((3, D_MODEL))],
        out_shape=[jax.ShapeDtypeStruct((s, ODD_IN), BF16), jax.ShapeDtypeStruct((s, D_MODEL), BF16),
                   jax.ShapeDtypeStruct((4, s // 4, D_MODEL), BF16), jax.ShapeDtypeStruct((16, s // 16, D_MODEL), BF16),
                   jax.ShapeDtypeStruct((s, STAT_W), F32), jax.ShapeDtypeStruct((4, s // 4, STAT_W), F32),
                   jax.ShapeDtypeStruct((16, s // 16, STAT_W), F32), jax.ShapeDtypeStruct((3, D_MODEL), F32)],
        scratch_shapes=[pltpu.VMEM((TMO, rest), BF16), pltpu.SemaphoreType.DMA(()), _token_scratch(TMO, D_MODEL),
                        _token_scratch(TMO, STAT_W)],
        compiler_params=_cp(("arbitrary",), 48),
    )(dy, ycr, z, z, z, z, z, z, z, dy, z, z, cw)
    dyc = [dyc, dyc4.reshape(s, D_MODEL), dyc16.reshape(s, D_MODEL)]
    dd = [dd, dd4.reshape(s, STAT_W), dd16.reshape(s, STAT_W)]
    return dz, dyc, dd, g_conv


def _cols_of_order(order):
    if order == 0:
        return (lambda j: jnp.where(j < 3, 3 * j, j + 6)), 8
    return (lambda j: 3 * j + order), 3


class _Hooks:
    def before_even(self):
        return None

    def odd_weights(self, w, x1):
        return w

    def odd_grads_ready(self, g_w_in_o, g_w_out_o):
        return None

    def even_mix_done(self, dz_e):
        return None

    def backward_done(self, dx0):
        return None


def _local_step(x, target, w, hooks=_Hooks()):
    tril = jnp.tril(jnp.ones((CHUNK, CHUNK), bool))
    wt = jnp.where(tril[None], w["ws"], 0.0).astype(BF16)
    wtt = jnp.swapaxes(wt, 1, 2)
    bs = w["bs"].reshape(4, CHUNK, 1)

    h_e = _rms_fwd("rms_fwd_even", x, w["even_norm"], after=hooks.before_even())
    z_e = _mm_nn("even_in_proj", h_e, w["w_in_e"], IN_ROWS, 1280, BF16)
    y_e = _even_mix_fwd(z_e, w["pool_w"], w["pool_scale"], wt, bs)
    x1 = _mm_nn("even_out_proj", y_e, w["w_out_e"], MM_ROWS, 1024, F32, resid=x)
    w = hooks.odd_weights(w, x1)
    h_o = _rms_fwd_orders("rms_fwd_odd", x1, w["odd_norm"])
    z_o = None
    for o in range(3):
        cols, n_cols = _cols_of_order(o)
        z_o = _mm_nn(f"odd_in_proj_o{o}", h_o[o], w["w_in_o"], IN_ROWS, D_MODEL, BF16, col_map=cols, n_cols=n_cols,
                     into=z_o)
    att = [_attn_fwd(z_o, gi) for gi in range(3)]
    y_o, ycr, ltot = _odd_mix_fwd(z_o, [a[0] for a in att], [a[1] for a in att], w["conv_w"])
    dx2, loss8, g_final = _out_proj_loss(y_o, w["w_out_o"], x1, w["final_norm"], target)

    g_w_out_o = _mm_tn("odd_out_proj_dw", y_o, dx2, 1024, DW_OUT_TOKENS)
    dy_o = _mm_nt("odd_out_proj_dy", dx2, w["w_out_o"], MM_ROWS, 1024, BF16)
    dz_o, dyc, dst, g_conv = _odd_mix_bwd(dy_o, z_o, ycr, w["conv_w"])
    for gi in range(3):
        dz_o = _attn_bwd(z_o, dyc[gi], ltot[gi], dst[gi], dz_o, gi)
    g_w_in_o, dh_o = None, []
    for o in range(3):
        cols, n_cols = _cols_of_order(o)
        g_w_in_o = _mm_tn(f"odd_in_proj_dw_o{o}", h_o[o], dz_o, D_MODEL, DW_IN_TOKENS, col_map=cols, n_cols=n_cols,
                          into=g_w_in_o)
        dh_o.append(_mm_nt(f"odd_in_proj_dh_o{o}", dz_o, w["w_in_o"], IN_ROWS, D_MODEL, BF16, k_map=cols, nk=n_cols))
    dx1, g_odd_norm = _rms_bwd("rms_bwd_odd", dh_o[0], x1, w["odd_norm"], dx2, dh4=dh_o[1], dh16=dh_o[2])
    after = hooks.odd_grads_ready(g_w_in_o, g_w_out_o)
    g_w_out_e = _mm_tn("even_out_proj_dw", y_e, dx1, 1024, DW_OUT_TOKENS, after=after)
    dy_e = _mm_nt("even_out_proj_dy", dx1, w["w_out_e"], MM_ROWS, 1024, BF16)
    dz_e, g_pw, g_ps, g_ws, g_bs = _even_mix_bwd(dy_e, z_e, w["pool_w"], w["pool_scale"], wt, wtt, bs)
    after = hooks.even_mix_done(dz_e)
    g_w_in_e = _mm_tn("even_in_proj_dw", h_e, dz_e, 1280, DW_IN_TOKENS, after=after)
    dh_e = _mm_nt("even_in_proj_dh", dz_e, w["w_in_e"], MM_ROWS, 2560, F32)
    dx0, g_even_norm = _rms_bwd("rms_bwd_even", dh_e, x, w["even_norm"], dx1)
    hooks.backward_done(dx0)

    grads = dict(w_in_e=g_w_in_e, pool_w=g_pw, w_out_e=g_w_out_e, w_in_o=g_w_in_o, w_out_o=g_w_out_o,
                 even_norm=g_even_norm, pool_scale=g_ps, ws=g_ws, bs=g_bs[:, :4].T, final_norm=g_final,
                 odd_norm=g_odd_norm, conv_w=g_conv)
    return loss8[0, 0], dx0, grads


class _Big(NamedTuple):
    name: str
    full: tuple
    haxis: int
    kaxis: int
    sub: int


BIGS = (
    _Big("w_in_e", (1024, 5120), 0, 1, 2),
    _Big("pool_w", (4, 256, 256), 0, 1, 1),
    _Big("w_out_e", (2048, 1024), 1, 0, 1),
    _Big("w_in_o", (1024, 14336), 0, 1, 4),
    _Big("w_out_o", (2048, 1024), 1, 0, 1),
)
N_BIG = len(BIGS)


def _shape(b, half=False, shard=False):
    return tuple(n // (2 if (half and ax == b.haxis) else 1) // (4 if (shard and ax == b.kaxis) else 1)
                 for ax, n in enumerate(b.full))


def _at(ref, b, h=None, k=None):
    idx = []
    for ax, n in enumerate(b.full):
        if ax == b.haxis and h is not None:
            idx.append(pl.ds(h * (n // 2), n // 2))
        elif ax == b.kaxis and k is not None:
            idx.append(pl.ds(k * (n // 4), n // 4))
        else:
            idx.append(slice(None))
    return ref.at[tuple(idx)]


def _place():
    x, y, c = lax.axis_index("x"), lax.axis_index("y"), lax.axis_index("c")
    chips = [(1 - x, y), (x, 1 - y), (1 - x, 1 - y)]
    return x, y, c, 2 * x + y, chips, [2 * cx + cy for cx, cy in chips]


def _piece_shape(b):
    return (4, 2) + _shape(b, half=True, shard=True)


def _gather_weights(bigs, shards, tiny):
    nb = len(bigs)

    def body(*refs):
        ins, tiny_in = refs[:nb], refs[nb]
        outs, tiny_out = refs[nb + 1:2 * nb + 1], refs[2 * nb + 1]
        send, recv, loc = refs[2 * nb + 2:]
        x, y, c, k_me, chips, ks = _place()
        sib = (x, y, 1 - c)

        def rc(src, dst, sem, to):
            return pltpu.make_async_remote_copy(src_ref=src, dst_ref=dst, send_sem=send.at[sem], recv_sem=recv.at[sem],
                                                device_id=to, device_id_type=MESH)

        own = pltpu.make_async_copy(tiny_in, tiny_out.at[k_me], loc)
        own.start()
        sends = []
        for j, chip in enumerate(chips):
            for a, b in enumerate(bigs):
                sends.append(rc(_at(ins[a], b, h=c), outs[a].at[k_me, c], 6 * a + j, (*chip, c)))
            sends.append(rc(tiny_in, tiny_out.at[k_me], 6 * nb + j, (*chip, c)))
        for cp in sends:
            cp.start()
        for j in range(3):
            for a in range(nb):
                piece = outs[a].at[ks[j], c]
                rc(piece, piece, 6 * a + j, sib).wait_recv()
                fwd = rc(piece, piece, 6 * a + 3 + j, sib)
                fwd.start()
                sends.append(fwd)
            rc(tiny_in, tiny_out.at[ks[j]], 6 * nb + j, sib).wait_recv()
        for j in range(3):
            for a in range(nb):
                piece = outs[a].at[ks[j], 1 - c]
                rc(piece, piece, 6 * a + 3 + j, sib).wait_recv()
        for cp in sends:
            cp.wait_send()
        own.wait()

    n_sem = 6 * nb + 3
    return pl.pallas_call(
        body, name="gather_even_weights",
        in_specs=[ANY] * (nb + 1), out_specs=[ANY] * (nb + 1),
        out_shape=[jax.ShapeDtypeStruct(_piece_shape(b), BF16) for b in bigs]
        + [jax.ShapeDtypeStruct((4,) + tiny.shape, F32)],
        scratch_shapes=[pltpu.SemaphoreType.DMA((n_sem,)), pltpu.SemaphoreType.DMA((n_sem,)), pltpu.SemaphoreType.DMA(())],
    )(*shards, tiny)


def _assemble(b, pieces, shard, k_arr):
    blk = _blk(b)

    def body(k_ref, p_ref, s_ref, o_ref):
        mine = pl.program_id(0) == k_ref[0]

        @pl.when(mine)
        def _():
            o_ref[...] = s_ref[...]

        @pl.when(jnp.logical_not(mine))
        def _():
            o_ref[...] = p_ref[...]

    return pl.pallas_call(
        body, name=f"assemble_{b.name}",
        grid_spec=pltpu.PrefetchScalarGridSpec(
            num_scalar_prefetch=1, grid=(4, 2, b.sub),
            in_specs=[pl.BlockSpec((None, None) + blk, lambda k, h, st, k_ref: (k, h) + _bidx(b, 0, 0, st)),
                      pl.BlockSpec(blk, lambda k, h, st, k_ref: _bidx(b, h, 0, st))],
            out_specs=pl.BlockSpec(blk, lambda k, h, st, k_ref: _bidx(b, h, k, st))),
        out_shape=jax.ShapeDtypeStruct(b.full, BF16),
        compiler_params=_cp(("arbitrary", "arbitrary", "arbitrary"), 32),
    )(k_arr, pieces, shard)


def _copies_to_chips(bigs):
    def copies(srcs, lands, send, recv, waiting=False):
        _, _, c, k_me, chips, _ = _place()
        return [pltpu.make_async_remote_copy(
                    src_ref=_at(srcs[a], b, h=c), dst_ref=lands[a].at[k_me, c], send_sem=send.at[3 * a + j],
                    recv_sem=recv.at[3 * a + j], device_id=(*chips[j], c), device_id_type=MESH)
                for j in range(3) for a, b in enumerate(bigs)]
    return copies


def _copies_swap_halves(bigs):
    def copies(srcs, lands, send, recv, waiting=False):
        x, y, c, _, _, _ = _place()
        return [pltpu.make_async_remote_copy(
                    src_ref=_at(srcs[a], b, h=1 - c), dst_ref=lands[a], send_sem=send.at[a], recv_sem=recv.at[a],
                    device_id=(x, y, 1 - c), device_id_type=MESH)
                for a, b in enumerate(bigs)]
    return copies


def _copies_partials(bigs):
    def copies(srcs, lands, send, recv, waiting=False):
        _, _, c, _, chips, ks = _place()
        return [pltpu.make_async_remote_copy(
                    src_ref=_at(srcs[a], b, k=ks[j]), dst_ref=lands[a].at[j], send_sem=send.at[3 * a + j],
                    recv_sem=recv.at[3 * a + j], device_id=(*chips[j], c), device_id_type=MESH)
                for j in range(3) for a, b in enumerate(bigs)]
    return copies


def _exchange(name, srcs, land_shapes, copies_of, n_copies):
    ns = len(srcs)

    def body(*refs):
        copies = copies_of(refs[:ns], refs[ns:ns + len(land_shapes)], refs[-2], refs[-1])
        for cp in copies:
            cp.start()
        for cp in copies:
            cp.wait()

    return pl.pallas_call(
        body, name=name, in_specs=[ANY] * ns, out_specs=[ANY] * len(land_shapes), out_shape=land_shapes,
        scratch_shapes=[pltpu.SemaphoreType.DMA((n_copies,)), pltpu.SemaphoreType.DMA((n_copies,))],
    )(*srcs)


HBM = pl.BlockSpec(memory_space=pltpu.HBM)
SEM = pl.BlockSpec(memory_space=pltpu.SEMAPHORE)
SIDE_EFFECT = pltpu.SideEffectType.DATAFLOW_SIDE_EFFECTING


def _in_hbm(a):
    return pltpu.with_memory_space_constraint(a, pltpu.HBM)


def _exchange_start(name, srcs, land_shapes, copies_of, n_copies, after=None):
    ns, nl = len(srcs), len(land_shapes)
    lands = [lax.empty(sh.shape, sh.dtype) for sh in land_shapes]
    tail = [] if after is None else [after]
    n_in = ns + nl + len(tail)

    def body(*refs):
        send, recv, token = refs[n_in], refs[n_in + 1], refs[-1]
        for cp in copies_of(refs[:ns], refs[ns:ns + nl], send, recv):
            cp.start()
        token[...] = jnp.zeros_like(token)

    thru = [pltpu.HBM(a.shape, a.dtype) for a in (*srcs, *lands)]
    send, recv, *bufs, token = pl.pallas_call(
        body, name=name,
        out_shape=(pltpu.SemaphoreType.DMA((n_copies,)), pltpu.SemaphoreType.DMA((n_copies,)), *thru,
                   jax.ShapeDtypeStruct((8, 128), F32)),
        in_specs=[HBM] * (ns + nl) + [ANY] * len(tail),
        out_specs=(SEM, SEM, *([HBM] * (ns + nl)), pl.BlockSpec(memory_space=pltpu.VMEM)),
        input_output_aliases={i: 2 + i for i in range(ns + nl)},
        compiler_params=pltpu.CompilerParams(has_side_effects=SIDE_EFFECT),
    )(*[_in_hbm(a) for a in (*srcs, *lands)], *tail)
    return (send, recv, bufs, ns), token


def _exchange_wait(name, state, copies_of, after):
    send, recv, bufs, ns = state
    n = len(bufs)

    def body(*refs):
        ins = refs[:n]
        for cp in copies_of(ins[:ns], ins[ns:], refs[n], refs[n + 1], waiting=True):
            cp.wait_send()
            cp.wait_recv()

    out = pl.pallas_call(
        body, name=name, out_shape=tuple(pltpu.HBM(a.shape, a.dtype) for a in bufs),
        in_specs=[HBM] * n + [SEM, SEM, ANY], out_specs=tuple([HBM] * n),
        input_output_aliases={i: i for i in range(n)},
        compiler_params=pltpu.CompilerParams(has_side_effects=SIDE_EFFECT),
    )(*bufs, send, recv, after)
    return list(out[:ns]), list(out[ns:])


def _finish_gather(bigs, pieces):
    nb = len(bigs)

    def body(*refs):
        outs, send, recv = refs[nb:2 * nb], refs[2 * nb], refs[2 * nb + 1]
        x, y, c, _, _, ks = _place()
        fwd = [pltpu.make_async_remote_copy(
                   src_ref=outs[a].at[ks[j], c], dst_ref=outs[a].at[ks[j], c], send_sem=send.at[3 * a + j],
                   recv_sem=recv.at[3 * a + j], device_id=(x, y, 1 - c), device_id_type=MESH)
               for j in range(3) for a in range(nb)]
        for cp in fwd:
            cp.start()
        for cp in fwd:
            cp.wait()

    return pl.pallas_call(
        body, name="gather_odd_finish", in_specs=[ANY] * nb, out_specs=[ANY] * nb,
        out_shape=[jax.ShapeDtypeStruct(_piece_shape(b), BF16) for b in bigs],
        scratch_shapes=[pltpu.SemaphoreType.DMA((3 * nb,)), pltpu.SemaphoreType.DMA((3 * nb,))],
        input_output_aliases={a: a for a in range(nb)},
    )(*pieces)


def _blk(b):
    win = _shape(b, half=True, shard=True)
    return (win[0] // b.sub,) + win[1:]


def _bidx(b, h, k, st):
    idx = [0] * len(b.full)
    idx[b.haxis] = h
    idx[b.kaxis] = k
    idx[0] = idx[0] * b.sub + st
    return tuple(idx)


def _chip_sum(b, g, got, c_arr):
    blk = _blk(b)

    def body(c_ref, g_ref, r_ref, o_ref):
        del c_ref
        o_ref[...] = (g_ref[...] + r_ref[...]).astype(BF16)

    half = pl.BlockSpec(blk, lambda k, st, c_ref: _bidx(b, 0, k, st))
    return pl.pallas_call(
        body, name=f"rs_chip_sum_{b.name}",
        grid_spec=pltpu.PrefetchScalarGridSpec(
            num_scalar_prefetch=1, grid=(4, b.sub),
            in_specs=[pl.BlockSpec(blk, lambda k, st, c_ref: _bidx(b, c_ref[0], k, st)), half], out_specs=half),
        out_shape=jax.ShapeDtypeStruct(_shape(b, half=True), BF16),
        compiler_params=_cp(("arbitrary", "arbitrary"), 40),
    )(c_arr, g, got)


def _half_shapes(bigs):
    return [jax.ShapeDtypeStruct(_shape(b, half=True), F32) for b in bigs]


def _partial_shapes(bigs):
    return [jax.ShapeDtypeStruct((3,) + _shape(b, half=True, shard=True), BF16) for b in bigs]


def _shard_sum(b, mine, got, ck_arr):
    blk = _blk(b)

    def body(ck_ref, m_ref, r0, r1, r2, o_ref):
        del ck_ref
        o_ref[...] = (m_ref[...].astype(F32) + r0[...].astype(F32)) + (r1[...].astype(F32) + r2[...].astype(F32))

    def peer(j):
        return pl.BlockSpec((None,) + blk, lambda st, ck: (j,) + _bidx(b, 0, 0, st))

    return pl.pallas_call(
        body, name=f"rs_shard_sum_{b.name}",
        grid_spec=pltpu.PrefetchScalarGridSpec(
            num_scalar_prefetch=1, grid=(b.sub,),
            in_specs=[pl.BlockSpec(blk, lambda st, ck: _bidx(b, 0, ck[1], st)), peer(0), peer(1), peer(2)],
            out_specs=pl.BlockSpec(blk, lambda st, ck: _bidx(b, ck[0], 0, st))),
        out_shape=jax.ShapeDtypeStruct(_shape(b, shard=True), F32),
        compiler_params=_cp(("arbitrary",), 40),
    )(ck_arr, mine, got, got, got)


def _share_halves(gs):
    def body(*refs):
        ins, outs, send, recv = refs[:N_BIG], refs[N_BIG:2 * N_BIG], refs[2 * N_BIG], refs[2 * N_BIG + 1]
        del ins
        x, y, c, _, _, _ = _place()
        copies = [pltpu.make_async_remote_copy(src_ref=_at(outs[a], b, h=c), dst_ref=_at(outs[a], b, h=c),
                                               send_sem=send.at[a], recv_sem=recv.at[a], device_id=(x, y, 1 - c),
                                               device_id_type=MESH)
                  for a, b in enumerate(BIGS)]
        for cp in copies:
            cp.start()
        for cp in copies:
            cp.wait()

    return pl.pallas_call(
        body, name="rs_share_halves", in_specs=[ANY] * N_BIG, out_specs=[ANY] * N_BIG,
        out_shape=[jax.ShapeDtypeStruct(_shape(b, shard=True), F32) for b in BIGS],
        scratch_shapes=[pltpu.SemaphoreType.DMA((N_BIG,)), pltpu.SemaphoreType.DMA((N_BIG,))],
        input_output_aliases={a: a for a in range(N_BIG)},
    )(*gs)


def _gather_small(block):
    m_per, n = block.shape

    def body(x_ref, out_ref, send_sems, recv_sems, local_sem):
        x, y, c = lax.axis_index("x"), lax.axis_index("y"), lax.axis_index("c")
        me, sibling = (x, y, c), (x, y, 1 - c)
        chips = [(1 - x, y), (x, 1 - y), (1 - x, 1 - y)]

        def rows(px, py, pc):
            return out_ref.at[pl.ds((4 * px + 2 * py + pc) * m_per, m_per), :]

        def copy(k, blk, to, src=None):
            return pltpu.make_async_remote_copy(
                src_ref=rows(*blk) if src is None else src, dst_ref=rows(*blk), send_sem=send_sems.at[k],
                recv_sem=recv_sems.at[k], device_id=to, device_id_type=MESH)

        mine = pltpu.make_async_copy(x_ref, rows(*me), local_sem)
        mine.start()
        first = [copy(0, me, sibling, src=x_ref)]
        first += [copy(1 + j, me, (*chip, c), src=x_ref) for j, chip in enumerate(chips)]
        for cp in first:
            cp.start()
        passed = [copy(4 + j, (*chip, c), sibling) for j, chip in enumerate(chips)]
        for j, chip in enumerate(chips):
            copy(1 + j, (*chip, c), me).wait_recv()
            passed[j].start()
        copy(0, sibling, me).wait_recv()
        for j, chip in enumerate(chips):
            copy(4 + j, (*chip, 1 - c), me).wait_recv()
        for cp in first + passed:
            cp.wait_send()
        mine.wait()

    return pl.pallas_call(
        body, name="gather_small_grads",
        out_shape=jax.ShapeDtypeStruct((8 * m_per, n), block.dtype),
        in_specs=[pl.BlockSpec(memory_space=pltpu.VMEM)], out_specs=pl.BlockSpec(memory_space=pltpu.VMEM),
        scratch_shapes=[pltpu.SemaphoreType.DMA((7,)), pltpu.SemaphoreType.DMA((7,)), pltpu.SemaphoreType.DMA],
    )(block)


def _sum_small(stack):
    _, m_per, n = stack.shape

    def body(x_ref, o_ref):
        acc = x_ref[0]
        for dev in range(1, 8):
            acc = acc + x_ref[dev]
        o_ref[...] = acc

    return pl.pallas_call(body, name="sum_small_grads", out_shape=jax.ShapeDtypeStruct((m_per, n), F32))(stack)


def _adam_update(w_ref, g_ref, m_ref, v_ref, d_ref, mo_ref, vo_ref):
    gg = g_ref[...]
    mn = ADAM_B1 * m_ref[...] + (1.0 - ADAM_B1) * gg
    vn = ADAM_B2 * v_ref[...] + (1.0 - ADAM_B2) * (gg * gg)
    m_hat = mn / (1.0 - ADAM_B1 ** ADAM_STEP)
    v_hat = vn / (1.0 - ADAM_B2 ** ADAM_STEP)
    d_ref[...] = -ADAM_LR * (m_hat / (jnp.sqrt(v_hat) + ADAM_EPS) + ADAM_WD * w_ref[...])
    mo_ref[...] = mn
    vo_ref[...] = vn


def _adamw(name, w, g, m, v, rows):
    shape = w.shape

    def body(*refs):
        _adam_update(*refs)

    spec = pl.BlockSpec((rows,) + shape[1:], lambda i: (i,) + (0,) * (len(shape) - 1))
    return pl.pallas_call(
        body, name=name, grid=(shape[0] // rows,), in_specs=[spec] * 4, out_specs=[spec] * 3,
        out_shape=[jax.ShapeDtypeStruct(shape, F32)] * 3, compiler_params=_cp(("parallel",), 48),
    )(w, g, m, v)


def _adamw_small(ws, gs, ms, vs):
    n = len(ws)

    def body(*refs):
        for a in range(n):
            _adam_update(*[refs[q * n + a] for q in range(7)])

    outs = pl.pallas_call(
        body, name="adamw_small", out_shape=[jax.ShapeDtypeStruct(w.shape, F32) for w in ws] * 3,
    )(*ws, *gs, *ms, *vs)
    return outs[:n], outs[n:2 * n], outs[2 * n:]


ADAM_ROWS = dict(w_in_e=256, pool_w=4, w_out_e=256, w_in_o=128, w_out_o=256)


def _pack(parts, rows):
    flat = jnp.concatenate([p.reshape(-1).astype(F32) for p in parts])
    return jnp.pad(flat, (0, rows * 128 - flat.shape[0])).reshape(rows, 128)


def _unpack(buf, shapes):
    flat = buf.reshape(-1)
    out, off = [], 0
    for shp in shapes:
        n = 1
        for dim in shp:
            n *= dim
        out.append(flat[off:off + n].reshape(shp))
        off += n
    return out


WEIGHTS = ("even_norm", "even_w_in", "even_pool_w", "even_pool_scale", "even_ws", "even_bs", "even_w_out", "odd_norm",
           "odd_w_in", "odd_conv_w", "odd_w_out", "final_norm")
BIG_OF = dict(w_in_e="even_w_in", pool_w="even_pool_w", w_out_e="even_w_out", w_in_o="odd_w_in", w_out_o="odd_w_out")
SMALL = ("even_norm", "even_pool_scale", "even_ws", "even_bs", "final_norm", "odd_norm", "odd_conv_w")
SMALL_GRAD_ROWS = 576


def kernel(x, even_norm, even_w_in, even_pool_w, even_pool_scale, even_ws, even_bs, even_w_out, odd_norm, odd_w_in, odd_conv_w, odd_w_out, final_norm, loss_target, m_even_norm, m_even_w_in, m_even_pool_w, m_even_pool_scale, m_even_ws, m_even_bs, m_even_w_out, m_odd_norm, m_odd_w_in, m_odd_conv_w, m_odd_w_out, m_final_norm, v_even_norm, v_even_w_in, v_even_pool_w, v_even_pool_scale, v_even_ws, v_even_bs, v_even_w_out, v_odd_norm, v_odd_w_in, v_odd_conv_w, v_odd_w_out, v_final_norm):
    wv = dict(zip(WEIGHTS, (even_norm, even_w_in, even_pool_w, even_pool_scale, even_ws, even_bs, even_w_out, odd_norm,
                            odd_w_in, odd_conv_w, odd_w_out, final_norm)))
    mv = dict(zip(WEIGHTS, (m_even_norm, m_even_w_in, m_even_pool_w, m_even_pool_scale, m_even_ws, m_even_bs,
                            m_even_w_out, m_odd_norm, m_odd_w_in, m_odd_conv_w, m_odd_w_out, m_final_norm)))
    vv = dict(zip(WEIGHTS, (v_even_norm, v_even_w_in, v_even_pool_w, v_even_pool_scale, v_even_ws, v_even_bs,
                            v_even_w_out, v_odd_norm, v_odd_w_in, v_odd_conv_w, v_odd_w_out, v_final_norm)))
    c = lax.axis_index("c")
    k_me = 2 * lax.axis_index("x") + lax.axis_index("y")

    c_arr = jnp.reshape(c, (1,)).astype(jnp.int32)
    ck_arr = jnp.stack([c, k_me]).astype(jnp.int32)
    even_bigs, odd_bigs = BIGS[:3], BIGS[3:]

    shards = {b.name: wv[BIG_OF[b.name]][0].astype(BF16) for b in BIGS}
    tiny = jnp.concatenate([odd_conv_w[0], odd_norm], axis=0)
    *pieces_even, tiny_all = _gather_weights(even_bigs, [shards[b.name] for b in even_bigs], tiny)
    tiny_full = jnp.transpose(tiny_all, (1, 0, 2)).reshape(4, D_MODEL)
    to_chips, swap_odd, partials_odd = _copies_to_chips(odd_bigs), _copies_swap_halves(odd_bigs), _copies_partials(odd_bigs)
    gather_state, gather_token = _exchange_start(
        "gather_odd_start", [shards[b.name] for b in odd_bigs],
        [jax.ShapeDtypeStruct(_piece_shape(b), BF16) for b in odd_bigs], to_chips, 3 * len(odd_bigs), after=pieces_even[-1])
    k_arr = jnp.reshape(k_me, (1,)).astype(jnp.int32)
    w = {b.name: _assemble(b, p, shards[b.name], k_arr) for b, p in zip(even_bigs, pieces_even)}
    w.update(even_norm=even_norm, pool_scale=even_pool_scale, ws=even_ws[0], bs=even_bs[0],
             final_norm=final_norm.reshape(1, D_MODEL), conv_w=tiny_full[:3], odd_norm=tiny_full[3:4])

    class Hooks(_Hooks):
        def before_even(self):
            return gather_token

        def odd_weights(self, w, x1):
            srcs, lands = _exchange_wait("gather_odd_wait", gather_state, to_chips, after=x1)
            pieces = _finish_gather(odd_bigs, lands)
            return dict(w, **{b.name: _assemble(b, p, s, k_arr) for b, p, s in zip(odd_bigs, pieces, srcs)})

        def odd_grads_ready(self, g_w_in_o, g_w_out_o):
            self.swap, token = _exchange_start("rs_odd_swap_start", [g_w_in_o, g_w_out_o], _half_shapes(odd_bigs),
                                               swap_odd, len(odd_bigs))
            return token

        def even_mix_done(self, dz_e):
            grads, got = _exchange_wait("rs_odd_swap_wait", self.swap, swap_odd, after=dz_e)
            sums = [_chip_sum(b, g, r, c_arr) for b, g, r in zip(odd_bigs, grads, got)]
            self.partials, token = _exchange_start("rs_odd_partials_start", sums, _partial_shapes(odd_bigs), partials_odd,
                                                   3 * len(odd_bigs))
            return token

        def backward_done(self, dx0):
            self.sums, self.parts = _exchange_wait("rs_odd_partials_wait", self.partials, partials_odd, after=dx0)

    hooks = Hooks()
    loss, dx, g = _local_step(x[0], loss_target[0], w, hooks)

    got = _exchange("rs_even_swap", [g[b.name] for b in even_bigs], _half_shapes(even_bigs), _copies_swap_halves(even_bigs),
                    len(even_bigs))
    sums = [_chip_sum(b, g[b.name], r, c_arr) for b, r in zip(even_bigs, got)]
    parts = _exchange("rs_even_partials", sums, _partial_shapes(even_bigs), _copies_partials(even_bigs), 3 * len(even_bigs))
    halves = [_shard_sum(b, sm, p, ck_arr) for b, sm, p in zip(BIGS, sums + hooks.sums, list(parts) + hooks.parts)]
    g_shard = dict(zip((b.name for b in BIGS), _share_halves(halves)))

    small_g = _pack([g["even_norm"], g["pool_scale"], g["ws"], g["bs"], g["final_norm"], g["odd_norm"], g["conv_w"], loss],
                    SMALL_GRAD_ROWS)
    small_g = _sum_small(_gather_small(small_g).reshape(8, SMALL_GRAD_ROWS, 128))
    g_en, g_ps, g_ws, g_bs, g_fn, g_on, g_cw, loss = _unpack(
        small_g, [(1, D_MODEL), (1, D_MODEL), (1, 4, CHUNK, CHUNK), (1, 4, CHUNK), (D_MODEL,), (1, D_MODEL), (1, 3, D_MODEL), ()])
    g_on = lax.dynamic_slice(g_on, (0, k_me * 256), (1, 256))
    g_cw = lax.dynamic_slice(g_cw, (0, 0, k_me * 256), (1, 3, 256))
    grad = dict(even_norm=g_en, even_pool_scale=g_ps, even_ws=g_ws, even_bs=g_bs, final_norm=g_fn, odd_norm=g_on,
                odd_conv_w=g_cw)
    for b in BIGS:
        grad[BIG_OF[b.name]] = g_shard[b.name][None]

    delta, new_m, new_v = {}, {}, {}
    for b in BIGS:
        n = BIG_OF[b.name]
        d_, m_, v_ = _adamw(f"adamw_{b.name}", wv[n][0], g_shard[b.name], mv[n][0], vv[n][0], ADAM_ROWS[b.name])
        delta[n], new_m[n], new_v[n] = d_[None], m_[None], v_[None]
    flat = [(wv[n].size // wv[n].shape[-1], wv[n].shape[-1]) for n in SMALL]
    outs = _adamw_small(*[[src[n].reshape(shp) for n, shp in zip(SMALL, flat)] for src in (wv, grad, mv, vv)])
    for dst, arrs in zip((delta, new_m, new_v), outs):
        for n, arr in zip(SMALL, arrs):
            dst[n] = arr.reshape(wv[n].shape)

    return (loss, dx[None], *[grad[n] for n in WEIGHTS], *[delta[n] for n in WEIGHTS], *[new_m[n] for n in WEIGHTS],
            *[new_v[n] for n in WEIGHTS])
```

```python
from typing import NamedTuple

import jax
import jax.numpy as jnp
from jax import lax
from jax.experimental import pallas as pl
from jax.experimental.pallas import tpu as pltpu

F32, BF16 = jnp.float32, jnp.bfloat16

D_MODEL = 1024
EPS = 1e-6
NEG = -1e30
POOL_SIZES = (2, 4, 8, 16)
GROUP_W = 256
CHUNK = 128
DILATIONS = (1, 4, 16)
N_SLOTS = 8
HEAD_DIM = 128
ATTN_BLOCK = 128
SCALE = HEAD_DIM ** -0.5
EVEN_IN = 5120
ODD_IN = 14336
QKV_BLOCKS = 9
ODD_BLOCKS = ODD_IN // D_MODEL
SLOPES = tuple(tuple(2.0 ** (-8.0 * (g * N_SLOTS + s + 1) / (3 * N_SLOTS)) for s in range(N_SLOTS)) for g in range(3))

ADAM_LR, ADAM_B1, ADAM_B2, ADAM_EPS, ADAM_WD, ADAM_STEP = 0.001, 0.9, 0.999, 1e-08, 0.01, 10

HALO = 16
TS = 512
TM = 256
TMO = 256
MM_ROWS = 1024
IN_ROWS = 2048
DW_IN_TOKENS = 2048
DW_OUT_TOKENS = 1024
MIB = 1 << 20
MESH = pl.DeviceIdType.MESH
ANY = pl.BlockSpec(memory_space=pl.ANY)


def _cp(sem, vmem_mib):
    return pltpu.CompilerParams(dimension_semantics=sem, vmem_limit_bytes=vmem_mib * MIB)


def _sig(x):
    return 0.5 * jnp.tanh(0.5 * x) + 0.5


def _win_sum(e, w, forward):
    n = e.shape[0]
    k = 1
    while k < w:
        e = e + pltpu.roll(e, (n - k) if forward else k, 0)
        k *= 2
    return e


def _mm_nn(name, a, b, tm, tn, out_dtype, resid=None, col_map=None, n_cols=None, into=None):
    m, k = a.shape
    n = b.shape[1]
    if col_map is None:
        col_map, n_cols = (lambda j: j), n // tn

    def body(*refs):
        a_ref, b_ref = refs[0], refs[1]
        acc = jnp.dot(a_ref[...].astype(BF16), b_ref[...], preferred_element_type=F32)
        if resid is not None:
            acc = acc + refs[2][...]
        o_ref = refs[-1]
        o_ref[...] = acc.astype(out_dtype)

    in_specs = [pl.BlockSpec((tm, k), lambda j, i: (i, 0)), pl.BlockSpec((k, tn), lambda j, i: (0, col_map(j)))]
    args = [a, b]
    if resid is not None:
        in_specs.append(pl.BlockSpec((tm, tn), lambda j, i: (i, col_map(j))))
        args.append(resid)
    aliases = {}
    if into is not None:
        aliases = {len(args): 0}
        in_specs.append(ANY)
        args.append(into)
    return pl.pallas_call(
        body, name=name, grid=(n_cols, m // tm), in_specs=in_specs,
        out_specs=pl.BlockSpec((tm, tn), lambda j, i: (i, col_map(j))),
        out_shape=jax.ShapeDtypeStruct((m, n), out_dtype), input_output_aliases=aliases,
        compiler_params=_cp(("parallel", "parallel"), 48),
    )(*args)


def _mm_nt(name, a, b, tm, tk, out_dtype, k_map=None, nk=None):
    m, k = a.shape
    n = b.shape[0]
    if k_map is None:
        k_map, nk = (lambda kk: kk), k // tk

    def body(a_ref, b_ref, o_ref, acc_ref):
        kk = pl.program_id(1)
        p = lax.dot_general(a_ref[...].astype(BF16), b_ref[...], (((1,), (1,)), ((), ())), preferred_element_type=F32)
        if nk == 1:
            o_ref[...] = p.astype(out_dtype)
        else:
            @pl.when(kk == 0)
            def _():
                acc_ref[...] = p

            @pl.when(kk > 0)
            def _():
                acc_ref[...] += p

            @pl.when(kk == nk - 1)
            def _():
                o_ref[...] = acc_ref[...].astype(out_dtype)

    return pl.pallas_call(
        body, name=name, grid=(m // tm, nk),
        in_specs=[pl.BlockSpec((tm, tk), lambda i, kk: (i, k_map(kk))), pl.BlockSpec((n, tk), lambda i, kk: (0, k_map(kk)))],
        out_specs=pl.BlockSpec((tm, n), lambda i, kk: (i, 0)),
        out_shape=jax.ShapeDtypeStruct((m, n), out_dtype),
        scratch_shapes=[pltpu.VMEM((tm, n) if nk > 1 else (8, 128), F32)],
        compiler_params=_cp(("parallel", "arbitrary"), 56),
    )(a, b)


def _mm_tn(name, a, g, tn, ts, col_map=None, n_cols=None, into=None, after=None):
    s, ka = a.shape
    n = g.shape[1]
    if col_map is None:
        col_map, n_cols = (lambda j: j), n // tn

    def body(a_ref, g_ref, *rest):
        o_ref = rest[-1]
        st = pl.program_id(1)
        p = lax.dot_general(a_ref[...], g_ref[...].astype(BF16), (((0,), (0,)), ((), ())), preferred_element_type=F32)

        @pl.when(st == 0)
        def _():
            o_ref[...] = p

        @pl.when(st > 0)
        def _():
            o_ref[...] += p

    in_specs = [pl.BlockSpec((ts, ka), lambda j, st: (st, 0)), pl.BlockSpec((ts, tn), lambda j, st: (st, col_map(j)))]
    args = [a, g]
    aliases = {}
    if into is not None:
        aliases = {2: 0}
        in_specs.append(ANY)
        args.append(into)
    if after is not None:
        in_specs.append(ANY)
        args.append(after)
    return pl.pallas_call(
        body, name=name, grid=(n_cols, s // ts), in_specs=in_specs,
        out_specs=pl.BlockSpec((ka, tn), lambda j, st: (0, col_map(j))),
        out_shape=jax.ShapeDtypeStruct((ka, n), F32), input_output_aliases=aliases,
        compiler_params=_cp(("parallel", "arbitrary"), 56),
    )(*args)


def _rms_fwd(name, x, g, after=None):
    s = x.shape[0]

    def body(x_ref, g_ref, *rest):
        xf = x_ref[...]
        r = lax.rsqrt(jnp.mean(xf * xf, axis=-1, keepdims=True) + EPS)
        rest[-1][...] = (xf * r * g_ref[...]).astype(BF16)

    row = pl.BlockSpec((TS, D_MODEL), lambda i: (i, 0))
    in_specs, args = [row, pl.BlockSpec((1, D_MODEL), lambda i: (0, 0))], [x, g]
    if after is not None:
        in_specs.append(ANY)
        args.append(after)
    return pl.pallas_call(
        body, name=name, grid=(s // TS,), in_specs=in_specs, out_specs=row,
        out_shape=jax.ShapeDtypeStruct((s, D_MODEL), BF16), compiler_params=_cp(("parallel",), 32),
    )(*args)


def _class_major(a, d):
    return a.reshape(d, a.shape[0] // d, a.shape[1])


def _class_spec(d, tile, width):
    return pl.BlockSpec((d, tile // d, width), lambda i: (0, i, 0))


LANES = 128


def _token_scratch(tile, width):
    return pltpu.VMEM((width // LANES, tile, LANES), F32)


def _put(scr, val):
    for c in range(scr.shape[0]):
        scr[c] = val[:, c * LANES:(c + 1) * LANES]


def _get(scr):
    return jnp.concatenate([scr[c] for c in range(scr.shape[0])], axis=1)


def _to_classes(ref3, scr, d, dtype):
    n = ref3.shape[1]
    for c in range(scr.shape[0]):
        for r in range(d):
            ref3[r, :, c * LANES:(c + 1) * LANES] = scr.at[c][pl.ds(r, n, stride=d), :].astype(dtype)


def _from_classes(scr, ref3, d):
    n = ref3.shape[1]
    for c in range(scr.shape[0]):
        for r in range(d):
            scr.at[c][pl.ds(r, n, stride=d), :] = ref3[r, :, c * LANES:(c + 1) * LANES].astype(F32)


def _rms_fwd_orders(name, x, g):
    s = x.shape[0]

    def body(x_ref, g_ref, h_ref, h4_ref, h16_ref, scr):
        xf = x_ref[...]
        r = lax.rsqrt(jnp.mean(xf * xf, axis=-1, keepdims=True) + EPS)
        h = xf * r * g_ref[...]
        h_ref[...] = h.astype(BF16)
        _put(scr, h)
        _to_classes(h4_ref, scr, 4, BF16)
        _to_classes(h16_ref, scr, 16, BF16)

    row = pl.BlockSpec((TS, D_MODEL), lambda i: (i, 0))
    h, h4, h16 = pl.pallas_call(
        body, name=name, grid=(s // TS,), in_specs=[row, pl.BlockSpec((1, D_MODEL), lambda i: (0, 0))],
        out_specs=[row, _class_spec(4, TS, D_MODEL), _class_spec(16, TS, D_MODEL)],
        out_shape=[jax.ShapeDtypeStruct((s, D_MODEL), BF16), jax.ShapeDtypeStruct((4, s // 4, D_MODEL), BF16),
                   jax.ShapeDtypeStruct((16, s // 16, D_MODEL), BF16)],
        scratch_shapes=[_token_scratch(TS, D_MODEL)],
        compiler_params=_cp(("parallel",), 32),
    )(x, g)
    return h, h4.reshape(s, D_MODEL), h16.reshape(s, D_MODEL)


def _rms_bwd(name, dh, x, g, dres, dh4=None, dh16=None):
    s = x.shape[0]
    extra = dh4 is not None

    def body(dh_ref, x_ref, g_ref, dres_ref, *rest):
        if extra:
            dh4_ref, dh16_ref, dx_ref, dg_ref, scr = rest
        else:
            dx_ref, dg_ref = rest
        xf = x_ref[...]
        r = lax.rsqrt(jnp.mean(xf * xf, axis=-1, keepdims=True) + EPS)
        xh = xf * r
        dhf = dh_ref[...].astype(F32)
        if extra:
            _from_classes(scr, dh4_ref, 4)
            dhf = dhf + _get(scr)
            _from_classes(scr, dh16_ref, 16)
            dhf = dhf + _get(scr)
        dxh = dhf * g_ref[...]
        dx_ref[...] = dres_ref[...] + r * (dxh - xh * jnp.mean(dxh * xh, axis=-1, keepdims=True))
        part = jnp.sum(dhf * xh, axis=0, keepdims=True)

        @pl.when(pl.program_id(0) == 0)
        def _():
            dg_ref[...] = part

        @pl.when(pl.program_id(0) > 0)
        def _():
            dg_ref[...] += part

    row = pl.BlockSpec((TS, D_MODEL), lambda i: (i, 0))
    vec = pl.BlockSpec((1, D_MODEL), lambda i: (0, 0))
    in_specs, args, scratch = [row, row, vec, row], [dh, x, g, dres], []
    if extra:
        in_specs += [_class_spec(4, TS, D_MODEL), _class_spec(16, TS, D_MODEL)]
        args += [_class_major(dh4, 4), _class_major(dh16, 16)]
        scratch = [_token_scratch(TS, D_MODEL)]
    return pl.pallas_call(
        body, name=name, grid=(s // TS,), in_specs=in_specs, out_specs=[row, vec],
        out_shape=[jax.ShapeDtypeStruct((s, D_MODEL), F32), jax.ShapeDtypeStruct((1, D_MODEL), F32)],
        scratch_shapes=scratch, compiler_params=_cp(("arbitrary",), 40),
    )(*args)


def _out_proj_loss(y, w_out, resid, g, target):
    s, k = y.shape

    def body(y_ref, w_ref, r_ref, g_ref, t_ref, dx_ref, loss_ref, dg_ref):
        xf = jnp.dot(y_ref[...], w_ref[...], preferred_element_type=F32) + r_ref[...]
        gg = g_ref[...]
        r = lax.rsqrt(jnp.mean(xf * xf, axis=-1, keepdims=True) + EPS)
        xh = xf * r
        e = xh * gg - t_ref[...]
        dy = e * (1.0 / D_MODEL)
        dxh = dy * gg
        dx_ref[...] = r * (dxh - xh * jnp.mean(dxh * xh, axis=-1, keepdims=True))
        lpart = 0.5 * jnp.sum(jnp.mean(e * e, axis=-1, keepdims=True), axis=0, keepdims=True)
        lpart = jnp.broadcast_to(lpart, (8, 128))
        gpart = jnp.sum(dy * xh, axis=0, keepdims=True)

        @pl.when(pl.program_id(0) == 0)
        def _():
            loss_ref[...] = lpart
            dg_ref[...] = gpart

        @pl.when(pl.program_id(0) > 0)
        def _():
            loss_ref[...] += lpart
            dg_ref[...] += gpart

    row = pl.BlockSpec((TS, D_MODEL), lambda i: (i, 0))
    vec = pl.BlockSpec((1, D_MODEL), lambda i: (0, 0))
    return pl.pallas_call(
        body, name="odd_out_proj_loss", grid=(s // TS,),
        in_specs=[pl.BlockSpec((TS, k), lambda i: (i, 0)), pl.BlockSpec((k, D_MODEL), lambda i: (0, 0)), row, vec, row],
        out_specs=[row, pl.BlockSpec((8, 128), lambda i: (0, 0)), vec],
        out_shape=[jax.ShapeDtypeStruct((s, D_MODEL), F32), jax.ShapeDtypeStruct((8, 128), F32),
                   jax.ShapeDtypeStruct((1, D_MODEL), F32)],
        compiler_params=_cp(("arbitrary",), 48),
    )(y, w_out, resid, g, target)


def _zcol(c, tm=TM):
    return pl.BlockSpec((tm, D_MODEL), lambda i, c=c: (i, c))


def _prev_halo(c, tm=TM):
    return pl.BlockSpec((HALO, D_MODEL), lambda i, c=c: (jnp.maximum(i * (tm // HALO) - 1, 0), c))


def _next_halo(c, n_rows, tm=TM):
    last = n_rows // HALO - 1
    return pl.BlockSpec((HALO, D_MODEL), lambda i, c=c: (jnp.minimum((i + 1) * (tm // HALO), last), c))


def _full(shape):
    return pl.BlockSpec(shape, lambda i: (0,) * len(shape))


def _inv_count(first_row, n, w):
    t = first_row + lax.broadcasted_iota(jnp.int32, (n, 1), 0)
    return 1.0 / jnp.minimum(t + 1, w).astype(F32)


def _even_mix_fwd(z, pw, ps, wt, bs):
    s = z.shape[0]

    def body(a_ref, ga_ref, u_ref, v_ref, gb_ref, halo_ref, pw_ref, ps_ref, wt_ref, bs_ref, y_ref):
        i = pl.program_id(0)
        a = a_ref[...].astype(F32)
        halo = jnp.where(i > 0, halo_ref[...].astype(F32), 0.0)
        ext = jnp.concatenate([halo, a], axis=0)
        ga = ga_ref[...].astype(F32)
        sga = ga * _sig(ga)
        for g, w in enumerate(POOL_SIZES):
            cs = slice(g * GROUP_W, (g + 1) * GROUP_W)
            win = _win_sum(ext[:, cs], w, False)[HALO:]
            pooled = win * _inv_count(i * TM, TM, w) - a[:, cs]
            mixed = jnp.dot(pooled.astype(BF16), pw_ref[g], preferred_element_type=F32)
            y_ref[:, cs] = (mixed * ps_ref[:, cs] * sga[:, cs]).astype(BF16)
        gb = gb_ref[...].astype(F32)
        gate = u_ref[...].astype(F32) * (gb * _sig(gb))
        for ch in range(TM // CHUNK):
            rs = slice(ch * CHUNK, (ch + 1) * CHUNK)
            for g in range(4):
                cs = slice(g * GROUP_W, (g + 1) * GROUP_W)
                mixb = jnp.dot(wt_ref[g], v_ref[rs, cs], preferred_element_type=F32) + bs_ref[g]
                y_ref[rs, D_MODEL + g * GROUP_W:D_MODEL + (g + 1) * GROUP_W] = (gate[rs, cs] * mixb).astype(BF16)

    return pl.pallas_call(
        body, name="even_mix_fwd", grid=(s // TM,),
        in_specs=[_zcol(0), _zcol(1), _zcol(2), _zcol(3), _zcol(4), _prev_halo(0),
                  _full((4, GROUP_W, GROUP_W)), _full((1, D_MODEL)), _full((4, CHUNK, CHUNK)), _full((4, CHUNK, 1))],
        out_specs=pl.BlockSpec((TM, 2 * D_MODEL), lambda i: (i, 0)),
        out_shape=jax.ShapeDtypeStruct((s, 2 * D_MODEL), BF16),
        compiler_params=_cp(("parallel",), 48),
    )(z, z, z, z, z, z, pw, ps, wt, bs)


def _even_mix_bwd(dy, z, pw, ps, wt, wtt, bs, after=None):
    s = z.shape[0]
    n_tiles = s // TM
    tail_specs, tail_args = ([ANY], [after]) if after is not None else ([], [])

    def body(dy_ref, a_ref, ga_ref, u_ref, v_ref, gb_ref, halo_ref, dyn_ref, gan_ref, pw_ref, ps_ref, wt_ref, wtt_ref,
             bs_ref, *rest):
        dz_ref, dpw_ref, dps_ref, dws_ref, dbs_ref = rest[-5:]
        i = pl.program_id(0)

        @pl.when(i == 0)
        def _():
            dpw_ref[...] = jnp.zeros_like(dpw_ref)
            dps_ref[...] = jnp.zeros_like(dps_ref)
            dws_ref[...] = jnp.zeros_like(dws_ref)
            dbs_ref[...] = jnp.zeros_like(dbs_ref)

        a = a_ref[...].astype(F32)
        halo = jnp.where(i > 0, halo_ref[...].astype(F32), 0.0)
        ext = jnp.concatenate([halo, a], axis=0)
        ga = ga_ref[...].astype(F32)
        sg = _sig(ga)
        sga = ga * sg
        dsga = sg * (1.0 + ga * (1.0 - sg))
        dya = dy_ref[:, :D_MODEL].astype(F32)
        gan = gan_ref[...].astype(F32)
        dmn_all = jnp.where(i < n_tiles - 1, dyn_ref[...].astype(F32) * ps_ref[...] * (gan * _sig(gan)), 0.0)
        for g, w in enumerate(POOL_SIZES):
            cs = slice(g * GROUP_W, (g + 1) * GROUP_W)
            inv = _inv_count(i * TM, TM, w)
            pooled = _win_sum(ext[:, cs], w, False)[HALO:] * inv - a[:, cs]
            pb = pooled.astype(BF16)
            mixed = jnp.dot(pb, pw_ref[g], preferred_element_type=F32)
            dyg = dya[:, cs]
            psg = ps_ref[:, cs]
            dm = (dyg * psg * sga[:, cs]).astype(BF16)
            dz_ref[:, D_MODEL + g * GROUP_W:D_MODEL + (g + 1) * GROUP_W] = (dyg * mixed * psg * dsga[:, cs]).astype(BF16)
            dps_ref[:, cs] += jnp.sum(dyg * mixed * sga[:, cs], axis=0, keepdims=True)
            dpw_ref[g] += lax.dot_general(pb, dm, (((0,), (0,)), ((), ())), preferred_element_type=F32)
            nt = (((1,), (1,)), ((), ()))
            dpool = lax.dot_general(dm, pw_ref[g], nt, preferred_element_type=F32)
            dpool_n = lax.dot_general(dmn_all[:, cs].astype(BF16), pw_ref[g], nt, preferred_element_type=F32)
            e = jnp.concatenate([dpool * inv, dpool_n * _inv_count((i + 1) * TM, HALO, w)], axis=0)
            dz_ref[:, cs] = (_win_sum(e, w, True)[:TM] - dpool).astype(BF16)

        gb = gb_ref[...].astype(F32)
        sg = _sig(gb)
        sgb = gb * sg
        dsgb = sg * (1.0 + gb * (1.0 - sg))
        u = u_ref[...].astype(F32)
        dyb = dy_ref[:, D_MODEL:].astype(F32)
        tril = lax.broadcasted_iota(jnp.int32, (CHUNK, CHUNK), 0) >= lax.broadcasted_iota(jnp.int32, (CHUNK, CHUNK), 1)
        lane = lax.broadcasted_iota(jnp.int32, (CHUNK, 128), 1)
        for ch in range(TM // CHUNK):
            rs = slice(ch * CHUNK, (ch + 1) * CHUNK)
            for g in range(4):
                cs = slice(g * GROUP_W, (g + 1) * GROUP_W)
                vb = v_ref[rs, cs]
                mixb = jnp.dot(wt_ref[g], vb, preferred_element_type=F32) + bs_ref[g]
                dyu = dyb[rs, cs] * u[rs, cs]
                dmix = dyu * sgb[rs, cs]
                dmb = dmix.astype(BF16)
                o = g * GROUP_W
                dz_ref[rs, 2 * D_MODEL + o:2 * D_MODEL + o + GROUP_W] = (dyb[rs, cs] * mixb * sgb[rs, cs]).astype(BF16)
                dz_ref[rs, 3 * D_MODEL + o:3 * D_MODEL + o + GROUP_W] = jnp.dot(
                    wtt_ref[g], dmb, preferred_element_type=F32).astype(BF16)
                dz_ref[rs, 4 * D_MODEL + o:4 * D_MODEL + o + GROUP_W] = (dyu * mixb * dsgb[rs, cs]).astype(BF16)
                dws = lax.dot_general(dmb, vb, (((1,), (1,)), ((), ())), preferred_element_type=F32)
                dws_ref[g] += jnp.where(tril, dws, 0.0)
                dbs_ref[...] += jnp.where(lane == g, jnp.sum(dmix, axis=1, keepdims=True), 0.0)

    return pl.pallas_call(
        body, name="even_mix_bwd", grid=(n_tiles,),
        in_specs=[pl.BlockSpec((TM, 2 * D_MODEL), lambda i: (i, 0)), _zcol(0), _zcol(1), _zcol(2), _zcol(3), _zcol(4),
                  _prev_halo(0), _next_halo(0, s), _next_halo(1, s),
                  _full((4, GROUP_W, GROUP_W)), _full((1, D_MODEL)), _full((4, CHUNK, CHUNK)), _full((4, CHUNK, CHUNK)),
                  _full((4, CHUNK, 1))] + tail_specs,
        out_specs=[pl.BlockSpec((TM, EVEN_IN), lambda i: (i, 0)), _full((4, GROUP_W, GROUP_W)), _full((1, D_MODEL)),
                   _full((4, CHUNK, CHUNK)), _full((CHUNK, 128))],
        out_shape=[jax.ShapeDtypeStruct((s, EVEN_IN), BF16), jax.ShapeDtypeStruct((4, GROUP_W, GROUP_W), F32),
                   jax.ShapeDtypeStruct((1, D_MODEL), F32), jax.ShapeDtypeStruct((4, CHUNK, CHUNK), F32),
                   jax.ShapeDtypeStruct((CHUNK, 128), F32)],
        compiler_params=_cp(("arbitrary",), 56),
    )(dy, z, z, z, z, z, z, dy, z, pw, ps, wt, wtt, bs, *tail_args)


STAT_W = 128
Q_BLOCKS = 2
Q_ROWS = Q_BLOCKS * ATTN_BLOCK


def _band(d):
    row = lax.broadcasted_iota(jnp.int32, (ATTN_BLOCK, 2 * ATTN_BLOCK), 0)
    col = lax.broadcasted_iota(jnp.int32, (ATTN_BLOCK, 2 * ATTN_BLOCK), 1)
    steps = row + ATTN_BLOCK - col
    return (steps >= 0) & (steps <= ATTN_BLOCK), col >= ATTN_BLOCK, -(steps * d).astype(F32)


def _attn_fwd(z, gi):
    s = z.shape[0]
    d = DILATIONS[gi]
    nb = s // d // ATTN_BLOCK
    nq = nb // Q_BLOCKS

    def spec(which, prev=False):
        cb = which * 3 + gi
        if prev:
            return pl.BlockSpec((ATTN_BLOCK, D_MODEL), lambda r, i: (r * nb + jnp.maximum(Q_BLOCKS * i - 1, 0), cb))
        return pl.BlockSpec((Q_ROWS, D_MODEL), lambda r, i: (r * nq + i, cb))

    def body(q_ref, kp_ref, kc_ref, vp_ref, vc_ref, o_ref, lse_ref):
        i = pl.program_id(1)
        inner, own, negdist = _band(d)
        lane = lax.broadcasted_iota(jnp.int32, (ATTN_BLOCK, STAT_W), 1)
        for b in range(Q_BLOCKS):
            rows = slice(b * ATTN_BLOCK, (b + 1) * ATTN_BLOCK)
            valid = (inner & ((i > 0) | own)) if b == 0 else inner
            stat = jnp.zeros((ATTN_BLOCK, STAT_W), F32)
            for h in range(N_SLOTS):
                sl = slice(h * HEAD_DIM, (h + 1) * HEAD_DIM)
                if b == 0:
                    k = jnp.concatenate([kp_ref[:, sl], kc_ref[:ATTN_BLOCK, sl]], axis=0)
                    v = jnp.concatenate([vp_ref[:, sl], vc_ref[:ATTN_BLOCK, sl]], axis=0)
                else:
                    k = kc_ref[(b - 1) * ATTN_BLOCK:(b + 1) * ATTN_BLOCK, sl]
                    v = vc_ref[(b - 1) * ATTN_BLOCK:(b + 1) * ATTN_BLOCK, sl]
                sc = lax.dot_general(q_ref[rows, sl], k, (((1,), (1,)), ((), ())), preferred_element_type=F32) * SCALE
                sc = jnp.where(valid, sc + SLOPES[gi][h] * negdist, NEG)
                m = jnp.max(sc, axis=-1, keepdims=True)
                p = jnp.exp(sc - m)
                l = jnp.sum(p, axis=-1, keepdims=True)
                o = jnp.dot((p * (1.0 / l)).astype(BF16), v, preferred_element_type=F32)
                o_ref[rows, sl] = o.astype(BF16)
                stat = jnp.where(lane == h, m + jnp.log(l), stat)
            lse_ref[rows, :] = stat

    return pl.pallas_call(
        body, name=f"attn_fwd_d{d}", grid=(d, nq),
        in_specs=[spec(0), spec(1, True), spec(1), spec(2, True), spec(2)],
        out_specs=[pl.BlockSpec((Q_ROWS, D_MODEL), lambda r, i: (r * nq + i, 0)),
                   pl.BlockSpec((Q_ROWS, STAT_W), lambda r, i: (r * nq + i, 0))],
        out_shape=[jax.ShapeDtypeStruct((s, D_MODEL), BF16), jax.ShapeDtypeStruct((s, STAT_W), F32)],
        compiler_params=_cp(("parallel", "parallel"), 32),
    )(z, z, z, z, z)


def _attn_bwd(z, dyc, ltot, dst, dz, gi):
    s = z.shape[0]
    d = DILATIONS[gi]
    nb = s // d // ATTN_BLOCK
    nq = nb // Q_BLOCKS
    n_steps = d * nq

    def rev(cb, width=D_MODEL, prev=False):
        if prev:
            return pl.BlockSpec((ATTN_BLOCK, width), lambda r, n: (r * nb + jnp.maximum(Q_BLOCKS * (nq - 1 - n) - 1, 0), cb))
        return pl.BlockSpec((Q_ROWS, width), lambda r, n: (r * nq + nq - 1 - n, cb))

    def body(q_ref, kp_ref, kc_ref, vp_ref, vc_ref, dy_ref, l_ref, d_ref, dz_in, dz_out, dq_s, dk_s, dv_s, ck_s, cv_s, sems):
        del dz_in
        r = pl.program_id(0)
        n = pl.program_id(1)
        i = nq - 1 - n
        step = r * nq + n

        slot = lax.rem(step, 2)

        def out_copies(sl_):
            rows = pl.ds(pl.multiple_of((r * nq + i) * Q_ROWS, Q_ROWS), Q_ROWS)
            return [pltpu.make_async_copy(stage_ref.at[sl_], dz_out.at[rows, pl.ds((which * 3 + gi) * D_MODEL, D_MODEL)],
                                          sems.at[sl_, which])
                    for which, stage_ref in enumerate((dq_s, dk_s, dv_s))]

        @pl.when(step >= 2)
        def _():
            for cp in out_copies(slot):
                cp.wait()

        @pl.when(n == 0)
        def _():
            ck_s[...] = jnp.zeros_like(ck_s)
            cv_s[...] = jnp.zeros_like(cv_s)

        row = lax.broadcasted_iota(jnp.int32, (Q_ROWS, Q_ROWS + ATTN_BLOCK), 0)
        col = lax.broadcasted_iota(jnp.int32, (Q_ROWS, Q_ROWS + ATTN_BLOCK), 1)
        steps = row + ATTN_BLOCK - col
        valid = (steps >= 0) & (steps <= ATTN_BLOCK) & ((i > 0) | (col >= ATTN_BLOCK))
        negdist = -(steps * d).astype(F32)
        nt = (((1,), (1,)), ((), ()))
        tn = (((0,), (0,)), ((), ()))
        for h in range(N_SLOTS):
            sl = slice(h * HEAD_DIM, (h + 1) * HEAD_DIM)
            q = q_ref[:, sl]
            k = jnp.concatenate([kp_ref[:, sl], kc_ref[:, sl]], axis=0)
            v = jnp.concatenate([vp_ref[:, sl], vc_ref[:, sl]], axis=0)
            dy = dy_ref[:, sl]
            sc = lax.dot_general(q, k, nt, preferred_element_type=F32) * SCALE + SLOPES[gi][h] * negdist
            p = jnp.where(valid, jnp.exp(sc - l_ref[:, h:h + 1]), 0.0)
            dp = lax.dot_general(dy, v, nt, preferred_element_type=F32)
            ds = (p * (dp - d_ref[:, h:h + 1])).astype(BF16)
            dq_s[slot, :, sl] = (jnp.dot(ds, k, preferred_element_type=F32) * SCALE).astype(BF16)
            dk = lax.dot_general(ds, q, tn, preferred_element_type=F32) * SCALE
            dv = lax.dot_general(p.astype(BF16), dy, tn, preferred_element_type=F32)
            dk_s[slot, :Q_ROWS - ATTN_BLOCK, sl] = dk[ATTN_BLOCK:Q_ROWS].astype(BF16)
            dv_s[slot, :Q_ROWS - ATTN_BLOCK, sl] = dv[ATTN_BLOCK:Q_ROWS].astype(BF16)
            dk_s[slot, Q_ROWS - ATTN_BLOCK:, sl] = (ck_s[:, sl] + dk[Q_ROWS:]).astype(BF16)
            dv_s[slot, Q_ROWS - ATTN_BLOCK:, sl] = (cv_s[:, sl] + dv[Q_ROWS:]).astype(BF16)
            ck_s[:, sl] = dk[:ATTN_BLOCK]
            cv_s[:, sl] = dv[:ATTN_BLOCK]

        for cp in out_copies(slot):
            cp.start()

        @pl.when(step == n_steps - 1)
        def _():
            for cp in out_copies(1 - slot) + out_copies(slot):
                cp.wait()

    assert n_steps >= 2
    stage = pltpu.VMEM((2, Q_ROWS, D_MODEL), BF16)
    carry = pltpu.VMEM((ATTN_BLOCK, D_MODEL), F32)
    return pl.pallas_call(
        body, name=f"attn_bwd_d{d}", grid=(d, nq),
        in_specs=[rev(gi), rev(3 + gi, prev=True), rev(3 + gi), rev(6 + gi, prev=True), rev(6 + gi),
                  rev(0), rev(0, STAT_W), rev(0, STAT_W), ANY],
        out_specs=ANY,
        out_shape=jax.ShapeDtypeStruct((s, ODD_IN), BF16),
        scratch_shapes=[stage, stage, stage, carry, carry, pltpu.SemaphoreType.DMA((2, 3))],
        input_output_aliases={8: 0},
        compiler_params=_cp(("arbitrary", "arbitrary"), 32),
    )(z, z, z, z, z, dyc, ltot, dst, dz)


def _odd_mix_fwd(z, os_, lses, cw):
    s = z.shape[0]

    def body(o0, o1, o2, l0, l1, l2, gc_ref, db_ref, dc_ref, dx_ref, gd_ref, hc_ref, hx_ref, cw_ref, y_ref, yc_ref, lt_ref,
             lt4_ref, lt16_ref, scr_o, scr_o2, scr_l):
        i = pl.program_id(0)
        _from_classes(scr_l, l1, 4)
        lse1 = _get(scr_l)
        _from_classes(scr_l, l2, 16)
        ls = [l0[...], lse1, _get(scr_l)]
        lmax = jnp.maximum(jnp.maximum(ls[0], ls[1]), ls[2])
        es = [jnp.exp(l - lmax) for l in ls]
        den = es[0] + es[1] + es[2]
        alpha = [e / den for e in es]
        ltot = lmax + jnp.log(den)
        lt_ref[...] = ltot
        _put(scr_l, ltot)
        _to_classes(lt4_ref, scr_l, 4, F32)
        _to_classes(lt16_ref, scr_l, 16, F32)
        _from_classes(scr_o, o1, 4)
        _from_classes(scr_o2, o2, 16)
        for h in range(N_SLOTS):
            sl = slice(h * HEAD_DIM, (h + 1) * HEAD_DIM)
            yc = (alpha[0][:, h:h + 1] * o0[:, sl].astype(F32) + alpha[1][:, h:h + 1] * scr_o[h]
                  + alpha[2][:, h:h + 1] * scr_o2[h])
            yc_ref[:, sl] = yc.astype(BF16)
            gc = gc_ref[:, sl].astype(F32)
            y_ref[:, sl] = (yc * (gc * _sig(gc))).astype(BF16)
            zc = dc_ref[:, sl].astype(F32) * dx_ref[:, sl].astype(F32)
            halo = jnp.where(i > 0, hc_ref[:, sl].astype(F32) * hx_ref[:, sl].astype(F32), 0.0)
            ext = jnp.concatenate([halo, zc], axis=0)
            z1 = pltpu.roll(ext, 1, 0)[HALO:]
            z2 = pltpu.roll(ext, 2, 0)[HALO:]
            conv = cw_ref[0:1, sl] * z2 + cw_ref[1:2, sl] * z1 + cw_ref[2:3, sl] * zc
            gd = gd_ref[:, sl].astype(F32)
            y_ref[:, D_MODEL + h * HEAD_DIM:D_MODEL + (h + 1) * HEAD_DIM] = (
                db_ref[:, sl].astype(F32) * conv * (gd * _sig(gd))).astype(BF16)

    row = pl.BlockSpec((TMO, D_MODEL), lambda i: (i, 0))
    stat = pl.BlockSpec((TMO, STAT_W), lambda i: (i, 0))
    y, ycr, lt, lt4, lt16 = pl.pallas_call(
        body, name="odd_mix_fwd", grid=(s // TMO,),
        in_specs=[row, _class_spec(4, TMO, D_MODEL), _class_spec(16, TMO, D_MODEL),
                  stat, _class_spec(4, TMO, STAT_W), _class_spec(16, TMO, STAT_W),
                  _zcol(9, TMO), _zcol(10, TMO), _zcol(11, TMO), _zcol(12, TMO), _zcol(13, TMO), _prev_halo(11, TMO),
                  _prev_halo(12, TMO), _full((3, D_MODEL))],
        out_specs=[pl.BlockSpec((TMO, 2 * D_MODEL), lambda i: (i, 0)), row, stat, _class_spec(4, TMO, STAT_W),
                   _class_spec(16, TMO, STAT_W)],
        out_shape=[jax.ShapeDtypeStruct((s, 2 * D_MODEL), BF16), jax.ShapeDtypeStruct((s, D_MODEL), BF16),
                   jax.ShapeDtypeStruct((s, STAT_W), F32), jax.ShapeDtypeStruct((4, s // 4, STAT_W), F32),
                   jax.ShapeDtypeStruct((16, s // 16, STAT_W), F32)],
        scratch_shapes=[_token_scratch(TMO, D_MODEL), _token_scratch(TMO, D_MODEL), _token_scratch(TMO, STAT_W)],
        compiler_params=_cp(("parallel",), 48),
    )(os_[0], _class_major(os_[1], 4), _class_major(os_[2], 16), lses[0], _class_major(lses[1], 4),
      _class_major(lses[2], 16), z, z, z, z, z, z, z, cw)
    return y, ycr, [lt, lt4.reshape(s, STAT_W), lt16.reshape(s, STAT_W)]


def _odd_mix_bwd(dy, z, ycr, cw):
    s = z.shape[0]
    n_tiles = s // TMO
    rest = ODD_IN - QKV_BLOCKS * D_MODEL

    def body(dy_ref, yc_ref, gc_ref, db_ref, dc_ref, dx_ref, gd_ref, hc_ref, hx_ref, dyn_ref, dbn_ref, gdn_ref, cw_ref,
             dz_ref, dyc_ref, dyc4_ref, dyc16_ref, dd_ref, dd4_ref, dd16_ref, dcw_ref, stages, sems, scr_o, scr_l):
        i = pl.program_id(0)
        slot = lax.rem(i, 2)

        def out_copy(sl_):
            return pltpu.make_async_copy(
                stages.at[sl_], dz_ref.at[pl.ds(pl.multiple_of(i * TMO, TMO), TMO), pl.ds(QKV_BLOCKS * D_MODEL, rest)],
                sems.at[sl_])

        @pl.when(i >= 2)
        def _():
            out_copy(slot).wait()

        stage = stages.at[slot]

        @pl.when(i == 0)
        def _():
            dcw_ref[...] = jnp.zeros_like(dcw_ref)

        lane = lax.broadcasted_iota(jnp.int32, (TMO, STAT_W), 1)
        stat = jnp.zeros((TMO, STAT_W), F32)
        nrow = TMO + HALO
        for h in range(N_SLOTS):
            sl = slice(h * HEAD_DIM, (h + 1) * HEAD_DIM)
            sd = slice(D_MODEL + h * HEAD_DIM, D_MODEL + (h + 1) * HEAD_DIM)
            dyc_in = dy_ref[:, sl].astype(F32)
            gc = gc_ref[:, sl].astype(F32)
            sg = _sig(gc)
            yc = yc_ref[:, sl].astype(F32)
            dyc = dyc_in * (gc * sg)
            dyc_ref[:, sl] = dyc.astype(BF16)
            scr_o[h] = dyc
            stage[:, sl] = (dyc_in * yc * (sg * (1.0 + gc * (1.0 - sg)))).astype(BF16)
            stat = jnp.where(lane == h, jnp.sum(dyc * yc, axis=-1, keepdims=True), stat)
            dc = dc_ref[:, sl].astype(F32)
            dx = dx_ref[:, sl].astype(F32)
            zc = dc * dx
            halo = jnp.where(i > 0, hc_ref[:, sl].astype(F32) * hx_ref[:, sl].astype(F32), 0.0)
            ext = jnp.concatenate([halo, zc], axis=0)
            z1 = pltpu.roll(ext, 1, 0)[HALO:]
            z2 = pltpu.roll(ext, 2, 0)[HALO:]
            w0, w1, w2 = cw_ref[0:1, sl], cw_ref[1:2, sl], cw_ref[2:3, sl]
            conv = w0 * z2 + w1 * z1 + w2 * zc
            gd = gd_ref[:, sl].astype(F32)
            sg = _sig(gd)
            sgd = gd * sg
            db = db_ref[:, sl].astype(F32)
            dyd = dy_ref[:, sd].astype(F32)
            dconv = dyd * db * sgd
            gdn = gdn_ref[:, sl].astype(F32)
            dconv_n = jnp.where(i < n_tiles - 1,
                                dyn_ref[:, sl].astype(F32) * dbn_ref[:, sl].astype(F32) * (gdn * _sig(gdn)), 0.0)
            extn = jnp.concatenate([dconv, dconv_n], axis=0)
            dzc = w2 * dconv + w1 * pltpu.roll(extn, nrow - 1, 0)[:TMO] + w0 * pltpu.roll(extn, nrow - 2, 0)[:TMO]
            stage[:, sd] = (dyd * conv * sgd).astype(BF16)
            stage[:, 2 * D_MODEL + h * HEAD_DIM:2 * D_MODEL + (h + 1) * HEAD_DIM] = (dzc * dx).astype(BF16)
            stage[:, 3 * D_MODEL + h * HEAD_DIM:3 * D_MODEL + (h + 1) * HEAD_DIM] = (dzc * dc).astype(BF16)
            stage[:, 4 * D_MODEL + h * HEAD_DIM:4 * D_MODEL + (h + 1) * HEAD_DIM] = (
                dyd * db * conv * (sg * (1.0 + gd * (1.0 - sg)))).astype(BF16)
            for tap, shifted in enumerate((z2, z1, zc)):
                dcw_ref[tap:tap + 1, sl] += jnp.sum(dconv * shifted, axis=0, keepdims=True)
        _to_classes(dyc4_ref, scr_o, 4, BF16)
        _to_classes(dyc16_ref, scr_o, 16, BF16)
        dd_ref[...] = stat
        _put(scr_l, stat)
        _to_classes(dd4_ref, scr_l, 4, F32)
        _to_classes(dd16_ref, scr_l, 16, F32)

        out_copy(slot).start()

        @pl.when(i == n_tiles - 1)
        def _():
            out_copy(1 - slot).wait()
            out_copy(slot).wait()

    assert n_tiles >= 2
    row = pl.BlockSpec((TMO, D_MODEL), lambda i: (i, 0))
    stat = pl.BlockSpec((TMO, STAT_W), lambda i: (i, 0))
    dz, dyc, dyc4, dyc16, dd, dd4, dd16, g_conv = pl.pallas_call(
        body, name="odd_mix_bwd", grid=(n_tiles,),
        in_specs=[pl.BlockSpec((TMO, 2 * D_MODEL), lambda i: (i, 0)), row, _zcol(9, TMO), _zcol(10, TMO), _zcol(11, TMO),
                  _zcol(12, TMO), _zcol(13, TMO), _prev_halo(11, TMO), _prev_halo(12, TMO), _next_halo(1, s, TMO),
                  _next_halo(10, s, TMO), _next_halo(13, s, TMO), _full((3, D_MODEL))],
        out_specs=[ANY, row, _class_spec(4, TMO, D_MODEL), _class_spec(16, TMO, D_MODEL),
                   stat, _class_spec(4, TMO, STAT_W), _class_spec(16, TMO, STAT_W), _full((3, D_MODEL))],
        out_shape=[jax.ShapeDtypeStruct((s, ODD_IN), BF16), jax.ShapeDtypeStruct((s, D_MODEL), BF16),
                   jax.ShapeDtypeStruct((4, s // 4, D_MODEL), BF16), jax.ShapeDtypeStruct((16, s // 16, D_MODEL), BF16),
                   jax.ShapeDtypeStruct((s, STAT_W), F32), jax.ShapeDtypeStruct((4, s // 4, STAT_W), F32),
                   jax.ShapeDtypeStruct((16, s // 16, STAT_W), F32), jax.ShapeDtypeStruct((3, D_MODEL), F32)],
        scratch_shapes=[pltpu.VMEM((2, TMO, rest), BF16), pltpu.SemaphoreType.DMA((2,)), _token_scratch(TMO, D_MODEL),
                        _token_scratch(TMO, STAT_W)],
        compiler_params=_cp(("arbitrary",), 48),
    )(dy, ycr, z, z, z, z, z, z, z, dy, z, z, cw)
    dyc = [dyc, dyc4.reshape(s, D_MODEL), dyc16.reshape(s, D_MODEL)]
    dd = [dd, dd4.reshape(s, STAT_W), dd16.reshape(s, STAT_W)]
    return dz, dyc, dd, g_conv


def _cols_of_order(order):
    if order == 0:
        return (lambda j: jnp.where(j < 3, 3 * j, j + 6)), 8
    return (lambda j: 3 * j + order), 3


class _Hooks:
    def before_even(self):
        return None

    def odd_weights(self, w, x1):
        return w

    def odd_grads_ready(self, g_w_in_o, g_w_out_o):
        return None

    def even_mix_done(self, dz_e):
        return None

    def backward_done(self, dx0):
        return None


def _local_step(x, target, w, hooks=_Hooks()):
    tril = jnp.tril(jnp.ones((CHUNK, CHUNK), bool))
    wt = jnp.where(tril[None], w["ws"], 0.0).astype(BF16)
    wtt = jnp.swapaxes(wt, 1, 2)
    bs = w["bs"].reshape(4, CHUNK, 1)

    h_e = _rms_fwd("rms_fwd_even", x, w["even_norm"], after=hooks.before_even())
    z_e = _mm_nn("even_in_proj", h_e, w["w_in_e"], IN_ROWS, 1280, BF16)
    y_e = _even_mix_fwd(z_e, w["pool_w"], w["pool_scale"], wt, bs)
    x1 = _mm_nn("even_out_proj", y_e, w["w_out_e"], MM_ROWS, 1024, F32, resid=x)
    w = hooks.odd_weights(w, x1)
    h_o = _rms_fwd_orders("rms_fwd_odd", x1, w["odd_norm"])
    z_o = None
    for o in range(3):
        cols, n_cols = _cols_of_order(o)
        z_o = _mm_nn(f"odd_in_proj_o{o}", h_o[o], w["w_in_o"], IN_ROWS, D_MODEL, BF16, col_map=cols, n_cols=n_cols,
                     into=z_o)
    att = [_attn_fwd(z_o, gi) for gi in range(3)]
    y_o, ycr, ltot = _odd_mix_fwd(z_o, [a[0] for a in att], [a[1] for a in att], w["conv_w"])
    dx2, loss8, g_final = _out_proj_loss(y_o, w["w_out_o"], x1, w["final_norm"], target)

    g_w_out_o = _mm_tn("odd_out_proj_dw", y_o, dx2, 1024, DW_OUT_TOKENS)
    dy_o = _mm_nt("odd_out_proj_dy", dx2, w["w_out_o"], MM_ROWS, 1024, BF16)
    dz_o, dyc, dst, g_conv = _odd_mix_bwd(dy_o, z_o, ycr, w["conv_w"])
    for gi in range(3):
        dz_o = _attn_bwd(z_o, dyc[gi], ltot[gi], dst[gi], dz_o, gi)
    g_w_in_o, dh_o = None, []
    for o in range(3):
        cols, n_cols = _cols_of_order(o)
        g_w_in_o = _mm_tn(f"odd_in_proj_dw_o{o}", h_o[o], dz_o, D_MODEL, DW_IN_TOKENS, col_map=cols, n_cols=n_cols,
                          into=g_w_in_o)
        dh_o.append(_mm_nt(f"odd_in_proj_dh_o{o}", dz_o, w["w_in_o"], IN_ROWS, D_MODEL, BF16, k_map=cols, nk=n_cols))
    dx1, g_odd_norm = _rms_bwd("rms_bwd_odd", dh_o[0], x1, w["odd_norm"], dx2, dh4=dh_o[1], dh16=dh_o[2])
    after = hooks.odd_grads_ready(g_w_in_o, g_w_out_o)
    g_w_out_e = _mm_tn("even_out_proj_dw", y_e, dx1, 1024, DW_OUT_TOKENS, after=after)
    dy_e = _mm_nt("even_out_proj_dy", dx1, w["w_out_e"], MM_ROWS, 1024, BF16)
    dz_e, g_pw, g_ps, g_ws, g_bs = _even_mix_bwd(dy_e, z_e, w["pool_w"], w["pool_scale"], wt, wtt, bs)
    after = hooks.even_mix_done(dz_e)
    g_w_in_e = _mm_tn("even_in_proj_dw", h_e, dz_e, 1280, DW_IN_TOKENS, after=after)
    dh_e = _mm_nt("even_in_proj_dh", dz_e, w["w_in_e"], MM_ROWS, 2560, F32)
    dx0, g_even_norm = _rms_bwd("rms_bwd_even", dh_e, x, w["even_norm"], dx1)
    hooks.backward_done(dx0)

    grads = dict(w_in_e=g_w_in_e, pool_w=g_pw, w_out_e=g_w_out_e, w_in_o=g_w_in_o, w_out_o=g_w_out_o,
                 even_norm=g_even_norm, pool_scale=g_ps, ws=g_ws, bs=g_bs[:, :4].T, final_norm=g_final,
                 odd_norm=g_odd_norm, conv_w=g_conv)
    return loss8[0, 0], dx0, grads


class _Big(NamedTuple):
    name: str
    full: tuple
    haxis: int
    kaxis: int
    sub: int


BIGS = (
    _Big("w_in_e", (1024, 5120), 0, 1, 2),
    _Big("pool_w", (4, 256, 256), 0, 1, 1),
    _Big("w_out_e", (2048, 1024), 1, 0, 1),
    _Big("w_in_o", (1024, 14336), 0, 1, 4),
    _Big("w_out_o", (2048, 1024), 1, 0, 1),
)
N_BIG = len(BIGS)


def _shape(b, half=False, shard=False):
    return tuple(n // (2 if (half and ax == b.haxis) else 1) // (4 if (shard and ax == b.kaxis) else 1)
                 for ax, n in enumerate(b.full))


def _at(ref, b, h=None, k=None):
    idx = []
    for ax, n in enumerate(b.full):
        if ax == b.haxis and h is not None:
            idx.append(pl.ds(h * (n // 2), n // 2))
        elif ax == b.kaxis and k is not None:
            idx.append(pl.ds(k * (n // 4), n // 4))
        else:
            idx.append(slice(None))
    return ref.at[tuple(idx)]


def _place():
    x, y, c = lax.axis_index("x"), lax.axis_index("y"), lax.axis_index("c")
    chips = [(1 - x, y), (x, 1 - y), (1 - x, 1 - y)]
    return x, y, c, 2 * x + y, chips, [2 * cx + cy for cx, cy in chips]


def _piece_shape(b):
    return (4, 2) + _shape(b, half=True, shard=True)


def _gather_weights(bigs, shards, tiny):
    nb = len(bigs)

    def body(*refs):
        ins, tiny_in = refs[:nb], refs[nb]
        outs, tiny_out = refs[nb + 1:2 * nb + 1], refs[2 * nb + 1]
        send, recv, loc = refs[2 * nb + 2:]
        x, y, c, k_me, chips, ks = _place()
        sib = (x, y, 1 - c)

        def rc(src, dst, sem, to):
            return pltpu.make_async_remote_copy(src_ref=src, dst_ref=dst, send_sem=send.at[sem], recv_sem=recv.at[sem],
                                                device_id=to, device_id_type=MESH)

        own = pltpu.make_async_copy(tiny_in, tiny_out.at[k_me], loc)
        own.start()
        sends = []
        for j, chip in enumerate(chips):
            for a, b in enumerate(bigs):
                sends.append(rc(_at(ins[a], b, h=c), outs[a].at[k_me, c], 6 * a + j, (*chip, c)))
            sends.append(rc(tiny_in, tiny_out.at[k_me], 6 * nb + j, (*chip, c)))
        for cp in sends:
            cp.start()
        for j in range(3):
            for a in range(nb):
                piece = outs[a].at[ks[j], c]
                rc(piece, piece, 6 * a + j, sib).wait_recv()
                fwd = rc(piece, piece, 6 * a + 3 + j, sib)
                fwd.start()
                sends.append(fwd)
            rc(tiny_in, tiny_out.at[ks[j]], 6 * nb + j, sib).wait_recv()
        for j in range(3):
            for a in range(nb):
                piece = outs[a].at[ks[j], 1 - c]
                rc(piece, piece, 6 * a + 3 + j, sib).wait_recv()
        for cp in sends:
            cp.wait_send()
        own.wait()

    n_sem = 6 * nb + 3
    return pl.pallas_call(
        body, name="gather_even_weights",
        in_specs=[ANY] * (nb + 1), out_specs=[ANY] * (nb + 1),
        out_shape=[jax.ShapeDtypeStruct(_piece_shape(b), BF16) for b in bigs]
        + [jax.ShapeDtypeStruct((4,) + tiny.shape, F32)],
        scratch_shapes=[pltpu.SemaphoreType.DMA((n_sem,)), pltpu.SemaphoreType.DMA((n_sem,)), pltpu.SemaphoreType.DMA(())],
    )(*shards, tiny)


def _assemble(b, pieces, shard, k_arr):
    blk = _blk(b)

    def body(k_ref, p_ref, s_ref, o_ref):
        mine = pl.program_id(0) == k_ref[0]

        @pl.when(mine)
        def _():
            o_ref[...] = s_ref[...]

        @pl.when(jnp.logical_not(mine))
        def _():
            o_ref[...] = p_ref[...]

    return pl.pallas_call(
        body, name=f"assemble_{b.name}",
        grid_spec=pltpu.PrefetchScalarGridSpec(
            num_scalar_prefetch=1, grid=(4, 2, b.sub),
            in_specs=[pl.BlockSpec((None, None) + blk, lambda k, h, st, k_ref: (k, h) + _bidx(b, 0, 0, st)),
                      pl.BlockSpec(blk, lambda k, h, st, k_ref: _bidx(b, h, 0, st))],
            out_specs=pl.BlockSpec(blk, lambda k, h, st, k_ref: _bidx(b, h, k, st))),
        out_shape=jax.ShapeDtypeStruct(b.full, BF16),
        compiler_params=_cp(("arbitrary", "arbitrary", "arbitrary"), 32),
    )(k_arr, pieces, shard)


def _copies_to_chips(bigs):
    def copies(srcs, lands, send, recv, waiting=False):
        _, _, c, k_me, chips, _ = _place()
        return [pltpu.make_async_remote_copy(
                    src_ref=_at(srcs[a], b, h=c), dst_ref=lands[a].at[k_me, c], send_sem=send.at[3 * a + j],
                    recv_sem=recv.at[3 * a + j], device_id=(*chips[j], c), device_id_type=MESH)
                for j in range(3) for a, b in enumerate(bigs)]
    return copies


def _copies_swap_halves(bigs):
    def copies(srcs, lands, send, recv, waiting=False):
        x, y, c, _, _, _ = _place()
        return [pltpu.make_async_remote_copy(
                    src_ref=_at(srcs[a], b, h=1 - c), dst_ref=lands[a], send_sem=send.at[a], recv_sem=recv.at[a],
                    device_id=(x, y, 1 - c), device_id_type=MESH)
                for a, b in enumerate(bigs)]
    return copies


def _copies_partials(bigs):
    def copies(srcs, lands, send, recv, waiting=False):
        _, _, c, _, chips, ks = _place()
        return [pltpu.make_async_remote_copy(
                    src_ref=_at(srcs[a], b, k=ks[j]), dst_ref=lands[a].at[j], send_sem=send.at[3 * a + j],
                    recv_sem=recv.at[3 * a + j], device_id=(*chips[j], c), device_id_type=MESH)
                for j in range(3) for a, b in enumerate(bigs)]
    return copies


def _exchange(name, srcs, land_shapes, copies_of, n_copies):
    ns = len(srcs)

    def body(*refs):
        copies = copies_of(refs[:ns], refs[ns:ns + len(land_shapes)], refs[-2], refs[-1])
        for cp in copies:
            cp.start()
        for cp in copies:
            cp.wait()

    return pl.pallas_call(
        body, name=name, in_specs=[ANY] * ns, out_specs=[ANY] * len(land_shapes), out_shape=land_shapes,
        scratch_shapes=[pltpu.SemaphoreType.DMA((n_copies,)), pltpu.SemaphoreType.DMA((n_copies,))],
    )(*srcs)


HBM = pl.BlockSpec(memory_space=pltpu.HBM)
SEM = pl.BlockSpec(memory_space=pltpu.SEMAPHORE)
SIDE_EFFECT = pltpu.SideEffectType.DATAFLOW_SIDE_EFFECTING


def _in_hbm(a):
    return pltpu.with_memory_space_constraint(a, pltpu.HBM)


def _exchange_start(name, srcs, land_shapes, copies_of, n_copies, after=None):
    ns, nl = len(srcs), len(land_shapes)
    lands = [lax.empty(sh.shape, sh.dtype) for sh in land_shapes]
    tail = [] if after is None else [after]
    n_in = ns + nl + len(tail)

    def body(*refs):
        send, recv, token = refs[n_in], refs[n_in + 1], refs[-1]
        for cp in copies_of(refs[:ns], refs[ns:ns + nl], send, recv):
            cp.start()
        token[...] = jnp.zeros_like(token)

    thru = [pltpu.HBM(a.shape, a.dtype) for a in (*srcs, *lands)]
    send, recv, *bufs, token = pl.pallas_call(
        body, name=name,
        out_shape=(pltpu.SemaphoreType.DMA((n_copies,)), pltpu.SemaphoreType.DMA((n_copies,)), *thru,
                   jax.ShapeDtypeStruct((8, 128), F32)),
        in_specs=[HBM] * (ns + nl) + [ANY] * len(tail),
        out_specs=(SEM, SEM, *([HBM] * (ns + nl)), pl.BlockSpec(memory_space=pltpu.VMEM)),
        input_output_aliases={i: 2 + i for i in range(ns + nl)},
        compiler_params=pltpu.CompilerParams(has_side_effects=SIDE_EFFECT),
    )(*[_in_hbm(a) for a in (*srcs, *lands)], *tail)
    return (send, recv, bufs, ns), token


def _exchange_wait(name, state, copies_of, after):
    send, recv, bufs, ns = state
    n = len(bufs)

    def body(*refs):
        ins = refs[:n]
        for cp in copies_of(ins[:ns], ins[ns:], refs[n], refs[n + 1], waiting=True):
            cp.wait_send()
            cp.wait_recv()

    out = pl.pallas_call(
        body, name=name, out_shape=tuple(pltpu.HBM(a.shape, a.dtype) for a in bufs),
        in_specs=[HBM] * n + [SEM, SEM, ANY], out_specs=tuple([HBM] * n),
        input_output_aliases={i: i for i in range(n)},
        compiler_params=pltpu.CompilerParams(has_side_effects=SIDE_EFFECT),
    )(*bufs, send, recv, after)
    return list(out[:ns]), list(out[ns:])


def _finish_gather(bigs, pieces):
    nb = len(bigs)

    def body(*refs):
        outs, send, recv = refs[nb:2 * nb], refs[2 * nb], refs[2 * nb + 1]
        x, y, c, _, _, ks = _place()
        fwd = [pltpu.make_async_remote_copy(
                   src_ref=outs[a].at[ks[j], c], dst_ref=outs[a].at[ks[j], c], send_sem=send.at[3 * a + j],
                   recv_sem=recv.at[3 * a + j], device_id=(x, y, 1 - c), device_id_type=MESH)
               for j in range(3) for a in range(nb)]
        for cp in fwd:
            cp.start()
        for cp in fwd:
            cp.wait()

    return pl.pallas_call(
        body, name="gather_odd_finish", in_specs=[ANY] * nb, out_specs=[ANY] * nb,
        out_shape=[jax.ShapeDtypeStruct(_piece_shape(b), BF16) for b in bigs],
        scratch_shapes=[pltpu.SemaphoreType.DMA((3 * nb,)), pltpu.SemaphoreType.DMA((3 * nb,))],
        input_output_aliases={a: a for a in range(nb)},
    )(*pieces)


def _blk(b):
    win = _shape(b, half=True, shard=True)
    return (win[0] // b.sub,) + win[1:]


def _bidx(b, h, k, st):
    idx = [0] * len(b.full)
    idx[b.haxis] = h
    idx[b.kaxis] = k
    idx[0] = idx[0] * b.sub + st
    return tuple(idx)


def _chip_sum(b, g, got, c_arr):
    blk = _blk(b)

    def body(c_ref, g_ref, r_ref, o_ref):
        del c_ref
        o_ref[...] = (g_ref[...] + r_ref[...]).astype(BF16)

    half = pl.BlockSpec(blk, lambda k, st, c_ref: _bidx(b, 0, k, st))
    return pl.pallas_call(
        body, name=f"rs_chip_sum_{b.name}",
        grid_spec=pltpu.PrefetchScalarGridSpec(
            num_scalar_prefetch=1, grid=(4, b.sub),
            in_specs=[pl.BlockSpec(blk, lambda k, st, c_ref: _bidx(b, c_ref[0], k, st)), half], out_specs=half),
        out_shape=jax.ShapeDtypeStruct(_shape(b, half=True), BF16),
        compiler_params=_cp(("arbitrary", "arbitrary"), 40),
    )(c_arr, g, got)


def _half_shapes(bigs):
    return [jax.ShapeDtypeStruct(_shape(b, half=True), F32) for b in bigs]


def _partial_shapes(bigs):
    return [jax.ShapeDtypeStruct((3,) + _shape(b, half=True, shard=True), BF16) for b in bigs]


def _shard_sum(b, mine, got, ck_arr):
    blk = _blk(b)

    def body(ck_ref, m_ref, r0, r1, r2, o_ref):
        del ck_ref
        o_ref[...] = (m_ref[...].astype(F32) + r0[...].astype(F32)) + (r1[...].astype(F32) + r2[...].astype(F32))

    def peer(j):
        return pl.BlockSpec((None,) + blk, lambda st, ck: (j,) + _bidx(b, 0, 0, st))

    return pl.pallas_call(
        body, name=f"rs_shard_sum_{b.name}",
        grid_spec=pltpu.PrefetchScalarGridSpec(
            num_scalar_prefetch=1, grid=(b.sub,),
            in_specs=[pl.BlockSpec(blk, lambda st, ck: _bidx(b, 0, ck[1], st)), peer(0), peer(1), peer(2)],
            out_specs=pl.BlockSpec(blk, lambda st, ck: _bidx(b, ck[0], 0, st))),
        out_shape=jax.ShapeDtypeStruct(_shape(b, shard=True), F32),
        compiler_params=_cp(("arbitrary",), 40),
    )(ck_arr, mine, got, got, got)


def _share_halves(gs):
    def body(*refs):
        ins, outs, send, recv = refs[:N_BIG], refs[N_BIG:2 * N_BIG], refs[2 * N_BIG], refs[2 * N_BIG + 1]
        del ins
        x, y, c, _, _, _ = _place()
        copies = [pltpu.make_async_remote_copy(src_ref=_at(outs[a], b, h=c), dst_ref=_at(outs[a], b, h=c),
                                               send_sem=send.at[a], recv_sem=recv.at[a], device_id=(x, y, 1 - c),
                                               device_id_type=MESH)
                  for a, b in enumerate(BIGS)]
        for cp in copies:
            cp.start()
        for cp in copies:
            cp.wait()

    return pl.pallas_call(
        body, name="rs_share_halves", in_specs=[ANY] * N_BIG, out_specs=[ANY] * N_BIG,
        out_shape=[jax.ShapeDtypeStruct(_shape(b, shard=True), F32) for b in BIGS],
        scratch_shapes=[pltpu.SemaphoreType.DMA((N_BIG,)), pltpu.SemaphoreType.DMA((N_BIG,))],
        input_output_aliases={a: a for a in range(N_BIG)},
    )(*gs)


def _gather_small(block):
    m_per, n = block.shape

    def body(x_ref, out_ref, send_sems, recv_sems, local_sem):
        x, y, c = lax.axis_index("x"), lax.axis_index("y"), lax.axis_index("c")
        me, sibling = (x, y, c), (x, y, 1 - c)
        chips = [(1 - x, y), (x, 1 - y), (1 - x, 1 - y)]

        def rows(px, py, pc):
            return out_ref.at[pl.ds((4 * px + 2 * py + pc) * m_per, m_per), :]

        def copy(k, blk, to, src=None):
            return pltpu.make_async_remote_copy(
                src_ref=rows(*blk) if src is None else src, dst_ref=rows(*blk), send_sem=send_sems.at[k],
                recv_sem=recv_sems.at[k], device_id=to, device_id_type=MESH)

        mine = pltpu.make_async_copy(x_ref, rows(*me), local_sem)
        mine.start()
        first = [copy(0, me, sibling, src=x_ref)]
        first += [copy(1 + j, me, (*chip, c), src=x_ref) for j, chip in enumerate(chips)]
        for cp in first:
            cp.start()
        passed = [copy(4 + j, (*chip, c), sibling) for j, chip in enumerate(chips)]
        for j, chip in enumerate(chips):
            copy(1 + j, (*chip, c), me).wait_recv()
            passed[j].start()
        copy(0, sibling, me).wait_recv()
        for j, chip in enumerate(chips):
            copy(4 + j, (*chip, 1 - c), me).wait_recv()
        for cp in first + passed:
            cp.wait_send()
        mine.wait()

    return pl.pallas_call(
        body, name="gather_small_grads",
        out_shape=jax.ShapeDtypeStruct((8 * m_per, n), block.dtype),
        in_specs=[pl.BlockSpec(memory_space=pltpu.VMEM)], out_specs=pl.BlockSpec(memory_space=pltpu.VMEM),
        scratch_shapes=[pltpu.SemaphoreType.DMA((7,)), pltpu.SemaphoreType.DMA((7,)), pltpu.SemaphoreType.DMA],
    )(block)


def _sum_small(stack):
    _, m_per, n = stack.shape

    def body(x_ref, o_ref):
        acc = x_ref[0]
        for dev in range(1, 8):
            acc = acc + x_ref[dev]
        o_ref[...] = acc

    return pl.pallas_call(body, name="sum_small_grads", out_shape=jax.ShapeDtypeStruct((m_per, n), F32))(stack)


def _adam_update(w_ref, g_ref, m_ref, v_ref, d_ref, mo_ref, vo_ref):
    gg = g_ref[...]
    mn = ADAM_B1 * m_ref[...] + (1.0 - ADAM_B1) * gg
    vn = ADAM_B2 * v_ref[...] + (1.0 - ADAM_B2) * (gg * gg)
    m_hat = mn / (1.0 - ADAM_B1 ** ADAM_STEP)
    v_hat = vn / (1.0 - ADAM_B2 ** ADAM_STEP)
    d_ref[...] = -ADAM_LR * (m_hat / (jnp.sqrt(v_hat) + ADAM_EPS) + ADAM_WD * w_ref[...])
    mo_ref[...] = mn
    vo_ref[...] = vn


def _adamw(name, w, g, m, v, rows):
    shape = w.shape

    def body(*refs):
        _adam_update(*refs)

    spec = pl.BlockSpec((rows,) + shape[1:], lambda i: (i,) + (0,) * (len(shape) - 1))
    return pl.pallas_call(
        body, name=name, grid=(shape[0] // rows,), in_specs=[spec] * 4, out_specs=[spec] * 3,
        out_shape=[jax.ShapeDtypeStruct(shape, F32)] * 3, compiler_params=_cp(("parallel",), 48),
    )(w, g, m, v)


def _adamw_small(ws, gs, ms, vs):
    n = len(ws)

    def body(*refs):
        for a in range(n):
            _adam_update(*[refs[q * n + a] for q in range(7)])

    outs = pl.pallas_call(
        body, name="adamw_small", out_shape=[jax.ShapeDtypeStruct(w.shape, F32) for w in ws] * 3,
    )(*ws, *gs, *ms, *vs)
    return outs[:n], outs[n:2 * n], outs[2 * n:]


ADAM_ROWS = dict(w_in_e=256, pool_w=4, w_out_e=256, w_in_o=128, w_out_o=256)


def _pack(parts, rows):
    flat = jnp.concatenate([p.reshape(-1).astype(F32) for p in parts])
    return jnp.pad(flat, (0, rows * 128 - flat.shape[0])).reshape(rows, 128)


def _unpack(buf, shapes):
    flat = buf.reshape(-1)
    out, off = [], 0
    for shp in shapes:
        n = 1
        for dim in shp:
            n *= dim
        out.append(flat[off:off + n].reshape(shp))
        off += n
    return out


WEIGHTS = ("even_norm", "even_w_in", "even_pool_w", "even_pool_scale", "even_ws", "even_bs", "even_w_out", "odd_norm",
           "odd_w_in", "odd_conv_w", "odd_w_out", "final_norm")
BIG_OF = dict(w_in_e="even_w_in", pool_w="even_pool_w", w_out_e="even_w_out", w_in_o="odd_w_in", w_out_o="odd_w_out")
SMALL = ("even_norm", "even_pool_scale", "even_ws", "even_bs", "final_norm", "odd_norm", "odd_conv_w")
SMALL_GRAD_ROWS = 576


def kernel(x, even_norm, even_w_in, even_pool_w, even_pool_scale, even_ws, even_bs, even_w_out, odd_norm, odd_w_in, odd_conv_w, odd_w_out, final_norm, loss_target, m_even_norm, m_even_w_in, m_even_pool_w, m_even_pool_scale, m_even_ws, m_even_bs, m_even_w_out, m_odd_norm, m_odd_w_in, m_odd_conv_w, m_odd_w_out, m_final_norm, v_even_norm, v_even_w_in, v_even_pool_w, v_even_pool_scale, v_even_ws, v_even_bs, v_even_w_out, v_odd_norm, v_odd_w_in, v_odd_conv_w, v_odd_w_out, v_final_norm):
    wv = dict(zip(WEIGHTS, (even_norm, even_w_in, even_pool_w, even_pool_scale, even_ws, even_bs, even_w_out, odd_norm,
                            odd_w_in, odd_conv_w, odd_w_out, final_norm)))
    mv = dict(zip(WEIGHTS, (m_even_norm, m_even_w_in, m_even_pool_w, m_even_pool_scale, m_even_ws, m_even_bs,
                            m_even_w_out, m_odd_norm, m_odd_w_in, m_odd_conv_w, m_odd_w_out, m_final_norm)))
    vv = dict(zip(WEIGHTS, (v_even_norm, v_even_w_in, v_even_pool_w, v_even_pool_scale, v_even_ws, v_even_bs,
                            v_even_w_out, v_odd_norm, v_odd_w_in, v_odd_conv_w, v_odd_w_out, v_final_norm)))
    c = lax.axis_index("c")
    k_me = 2 * lax.axis_index("x") + lax.axis_index("y")

    c_arr = jnp.reshape(c, (1,)).astype(jnp.int32)
    ck_arr = jnp.stack([c, k_me]).astype(jnp.int32)
    even_bigs, odd_bigs = BIGS[:3], BIGS[3:]

    shards = {b.name: wv[BIG_OF[b.name]][0].astype(BF16) for b in BIGS}
    tiny = jnp.concatenate([odd_conv_w[0], odd_norm], axis=0)
    *pieces_even, tiny_all = _gather_weights(even_bigs, [shards[b.name] for b in even_bigs], tiny)
    tiny_full = jnp.transpose(tiny_all, (1, 0, 2)).reshape(4, D_MODEL)
    to_chips, swap_odd, partials_odd = _copies_to_chips(odd_bigs), _copies_swap_halves(odd_bigs), _copies_partials(odd_bigs)
    gather_state, gather_token = _exchange_start(
        "gather_odd_start", [shards[b.name] for b in odd_bigs],
        [jax.ShapeDtypeStruct(_piece_shape(b), BF16) for b in odd_bigs], to_chips, 3 * len(odd_bigs), after=pieces_even[-1])
    k_arr = jnp.reshape(k_me, (1,)).astype(jnp.int32)
    w = {b.name: _assemble(b, p, shards[b.name], k_arr) for b, p in zip(even_bigs, pieces_even)}
    w.update(even_norm=even_norm, pool_scale=even_pool_scale, ws=even_ws[0], bs=even_bs[0],
             final_norm=final_norm.reshape(1, D_MODEL), conv_w=tiny_full[:3], odd_norm=tiny_full[3:4])

    class Hooks(_Hooks):
        def before_even(self):
            return gather_token

        def odd_weights(self, w, x1):
            srcs, lands = _exchange_wait("gather_odd_wait", gather_state, to_chips, after=x1)
            pieces = _finish_gather(odd_bigs, lands)
            return dict(w, **{b.name: _assemble(b, p, s, k_arr) for b, p, s in zip(odd_bigs, pieces, srcs)})

        def odd_grads_ready(self, g_w_in_o, g_w_out_o):
            self.swap, token = _exchange_start("rs_odd_swap_start", [g_w_in_o, g_w_out_o], _half_shapes(odd_bigs),
                                               swap_odd, len(odd_bigs))
            return token

        def even_mix_done(self, dz_e):
            grads, got = _exchange_wait("rs_odd_swap_wait", self.swap, swap_odd, after=dz_e)
            sums = [_chip_sum(b, g, r, c_arr) for b, g, r in zip(odd_bigs, grads, got)]
            self.partials, token = _exchange_start("rs_odd_partials_start", sums, _partial_shapes(odd_bigs), partials_odd,
                                                   3 * len(odd_bigs))
            return token

        def backward_done(self, dx0):
            self.sums, self.parts = _exchange_wait("rs_odd_partials_wait", self.partials, partials_odd, after=dx0)

    hooks = Hooks()
    loss, dx, g = _local_step(x[0], loss_target[0], w, hooks)

    got = _exchange("rs_even_swap", [g[b.name] for b in even_bigs], _half_shapes(even_bigs), _copies_swap_halves(even_bigs),
                    len(even_bigs))
    sums = [_chip_sum(b, g[b.name], r, c_arr) for b, r in zip(even_bigs, got)]
    parts = _exchange("rs_even_partials", sums, _partial_shapes(even_bigs), _copies_partials(even_bigs), 3 * len(even_bigs))
    halves = [_shard_sum(b, sm, p, ck_arr) for b, sm, p in zip(BIGS, sums + hooks.sums, list(parts) + hooks.parts)]
    g_shard = dict(zip((b.name for b in BIGS), _share_halves(halves)))

    small_g = _pack([g["even_norm"], g["pool_scale"], g["ws"], g["bs"], g["final_norm"], g["odd_norm"], g["conv_w"], loss],
                    SMALL_GRAD_ROWS)
    small_g = _sum_small(_gather_small(small_g).reshape(8, SMALL_GRAD_ROWS, 128))
    g_en, g_ps, g_ws, g_bs, g_fn, g_on, g_cw, loss = _unpack(
        small_g, [(1, D_MODEL), (1, D_MODEL), (1, 4, CHUNK, CHUNK), (1, 4, CHUNK), (D_MODEL,), (1, D_MODEL), (1, 3, D_MODEL), ()])
    g_on = lax.dynamic_slice(g_on, (0, k_me * 256), (1, 256))
    g_cw = lax.dynamic_slice(g_cw, (0, 0, k_me * 256), (1, 3, 256))
    grad = dict(even_norm=g_en, even_pool_scale=g_ps, even_ws=g_ws, even_bs=g_bs, final_norm=g_fn, odd_norm=g_on,
                odd_conv_w=g_cw)
    for b in BIGS:
        grad[BIG_OF[b.name]] = g_shard[b.name][None]

    delta, new_m, new_v = {}, {}, {}
    for b in BIGS:
        n = BIG_OF[b.name]
        d_, m_, v_ = _adamw(f"adamw_{b.name}", wv[n][0], g_shard[b.name], mv[n][0], vv[n][0], ADAM_ROWS[b.name])
        delta[n], new_m[n], new_v[n] = d_[None], m_[None], v_[None]
    flat = [(wv[n].size // wv[n].shape[-1], wv[n].shape[-1]) for n in SMALL]
    outs = _adamw_small(*[[src[n].reshape(shp) for n, shp in zip(SMALL, flat)] for src in (wv, grad, mv, vv)])
    for dst, arrs in zip((delta, new_m, new_v), outs):
        for n, arr in zip(SMALL, arrs):
            dst[n] = arr.reshape(wv[n].shape)

    return (loss, dx[None], *[grad[n] for n in WEIGHTS], *[delta[n] for n in WEIGHTS], *[new_m[n] for n in WEIGHTS],
            *[new_v[n] for n in WEIGHTS])
```

```python
from typing import NamedTuple

import jax
import jax.numpy as jnp
from jax import lax
from jax.experimental import pallas as pl
from jax.experimental.pallas import tpu as pltpu

F32, BF16 = jnp.float32, jnp.bfloat16

D_MODEL = 1024
EPS = 1e-6
NEG = -1e30
POOL_SIZES = (2, 4, 8, 16)
GROUP_W = 256
CHUNK = 128
DILATIONS = (1, 4, 16)
N_SLOTS = 8
HEAD_DIM = 128
ATTN_BLOCK = 128
SCALE = HEAD_DIM ** -0.5
EVEN_IN = 5120
ODD_IN = 14336
QKV_BLOCKS = 9
ODD_BLOCKS = ODD_IN // D_MODEL
SLOPES = tuple(tuple(2.0 ** (-8.0 * (g * N_SLOTS + s + 1) / (3 * N_SLOTS)) for s in range(N_SLOTS)) for g in range(3))

ADAM_LR, ADAM_B1, ADAM_B2, ADAM_EPS, ADAM_WD, ADAM_STEP = 0.001, 0.9, 0.999, 1e-08, 0.01, 10

HALO = 16
TS = 512
TM = 256
TMO = 256
MM_ROWS = 1024
IN_ROWS = 2048
DW_IN_TOKENS = 2048
DW_OUT_TOKENS = 1024
MIB = 1 << 20
MESH = pl.DeviceIdType.MESH
ANY = pl.BlockSpec(memory_space=pl.ANY)


def _cp(sem, vmem_mib):
    return pltpu.CompilerParams(dimension_semantics=sem, vmem_limit_bytes=vmem_mib * MIB)


def _sig(x):
    return 0.5 * jnp.tanh(0.5 * x) + 0.5


def _win_sum(e, w, forward):
    n = e.shape[0]
    k = 1
    while k < w:
        e = e + pltpu.roll(e, (n - k) if forward else k, 0)
        k *= 2
    return e


def _mm_nn(name, a, b, tm, tn, out_dtype, resid=None, col_map=None, n_cols=None, into=None):
    m, k = a.shape
    n = b.shape[1]
    if col_map is None:
        col_map, n_cols = (lambda j: j), n // tn

    def body(*refs):
        a_ref, b_ref = refs[0], refs[1]
        acc = jnp.dot(a_ref[...].astype(BF16), b_ref[...], preferred_element_type=F32)
        if resid is not None:
            acc = acc + refs[2][...]
        o_ref = refs[-1]
        o_ref[...] = acc.astype(out_dtype)

    in_specs = [pl.BlockSpec((tm, k), lambda j, i: (i, 0)), pl.BlockSpec((k, tn), lambda j, i: (0, col_map(j)))]
    args = [a, b]
    if resid is not None:
        in_specs.append(pl.BlockSpec((tm, tn), lambda j, i: (i, col_map(j))))
        args.append(resid)
    aliases = {}
    if into is not None:
        aliases = {len(args): 0}
        in_specs.append(ANY)
        args.append(into)
    return pl.pallas_call(
        body, name=name, grid=(n_cols, m // tm), in_specs=in_specs,
        out_specs=pl.BlockSpec((tm, tn), lambda j, i: (i, col_map(j))),
        out_shape=jax.ShapeDtypeStruct((m, n), out_dtype), input_output_aliases=aliases,
        compiler_params=_cp(("parallel", "parallel"), 48),
    )(*args)


def _mm_nt(name, a, b, tm, tk, out_dtype, k_map=None, nk=None):
    m, k = a.shape
    n = b.shape[0]
    if k_map is None:
        k_map, nk = (lambda kk: kk), k // tk

    def body(a_ref, b_ref, o_ref, acc_ref):
        kk = pl.program_id(1)
        p = lax.dot_general(a_ref[...].astype(BF16), b_ref[...], (((1,), (1,)), ((), ())), preferred_element_type=F32)
        if nk == 1:
            o_ref[...] = p.astype(out_dtype)
        else:
            @pl.when(kk == 0)
            def _():
                acc_ref[...] = p

            @pl.when(kk > 0)
            def _():
                acc_ref[...] += p

            @pl.when(kk == nk - 1)
            def _():
                o_ref[...] = acc_ref[...].astype(out_dtype)

    return pl.pallas_call(
        body, name=name, grid=(m // tm, nk),
        in_specs=[pl.BlockSpec((tm, tk), lambda i, kk: (i, k_map(kk))), pl.BlockSpec((n, tk), lambda i, kk: (0, k_map(kk)))],
        out_specs=pl.BlockSpec((tm, n), lambda i, kk: (i, 0)),
        out_shape=jax.ShapeDtypeStruct((m, n), out_dtype),
        scratch_shapes=[pltpu.VMEM((tm, n) if nk > 1 else (8, 128), F32)],
        compiler_params=_cp(("parallel", "arbitrary"), 56),
    )(a, b)


def _mm_tn(name, a, g, tn, ts, col_map=None, n_cols=None, into=None, after=None):
    s, ka = a.shape
    n = g.shape[1]
    if col_map is None:
        col_map, n_cols = (lambda j: j), n // tn

    def body(a_ref, g_ref, *rest):
        o_ref = rest[-1]
        st = pl.program_id(1)
        p = lax.dot_general(a_ref[...], g_ref[...].astype(BF16), (((0,), (0,)), ((), ())), preferred_element_type=F32)

        @pl.when(st == 0)
        def _():
            o_ref[...] = p

        @pl.when(st > 0)
        def _():
            o_ref[...] += p

    in_specs = [pl.BlockSpec((ts, ka), lambda j, st: (st, 0)), pl.BlockSpec((ts, tn), lambda j, st: (st, col_map(j)))]
    args = [a, g]
    aliases = {}
    if into is not None:
        aliases = {2: 0}
        in_specs.append(ANY)
        args.append(into)
    if after is not None:
        in_specs.append(ANY)
        args.append(after)
    return pl.pallas_call(
        body, name=name, grid=(n_cols, s // ts), in_specs=in_specs,
        out_specs=pl.BlockSpec((ka, tn), lambda j, st: (0, col_map(j))),
        out_shape=jax.ShapeDtypeStruct((ka, n), F32), input_output_aliases=aliases,
        compiler_params=_cp(("parallel", "arbitrary"), 56),
    )(*args)


def _rms_fwd(name, x, g, after=None):
    s = x.shape[0]

    def body(x_ref, g_ref, *rest):
        xf = x_ref[...]
        r = lax.rsqrt(jnp.mean(xf * xf, axis=-1, keepdims=True) + EPS)
        rest[-1][...] = (xf * r * g_ref[...]).astype(BF16)

    row = pl.BlockSpec((TS, D_MODEL), lambda i: (i, 0))
    in_specs, args = [row, pl.BlockSpec((1, D_MODEL), lambda i: (0, 0))], [x, g]
    if after is not None:
        in_specs.append(ANY)
        args.append(after)
    return pl.pallas_call(
        body, name=name, grid=(s // TS,), in_specs=in_specs, out_specs=row,
        out_shape=jax.ShapeDtypeStruct((s, D_MODEL), BF16), compiler_params=_cp(("parallel",), 32),
    )(*args)


def _class_major(a, d):
    return a.reshape(d, a.shape[0] // d, a.shape[1])


def _class_spec(d, tile, width):
    return pl.BlockSpec((d, tile // d, width), lambda i: (0, i, 0))


LANES = 128


def _token_scratch(tile, width):
    return pltpu.VMEM((width // LANES, tile, LANES), F32)


def _put(scr, val):
    for c in range(scr.shape[0]):
        scr[c] = val[:, c * LANES:(c + 1) * LANES]


def _get(scr):
    return jnp.concatenate([scr[c] for c in range(scr.shape[0])], axis=1)


def _to_classes(ref3, scr, d, dtype):
    n = ref3.shape[1]
    for c in range(scr.shape[0]):
        for r in range(d):
            ref3[r, :, c * LANES:(c + 1) * LANES] = scr.at[c][pl.ds(r, n, stride=d), :].astype(dtype)


def _from_classes(scr, ref3, d):
    n = ref3.shape[1]
    for c in range(scr.shape[0]):
        for r in range(d):
            scr.at[c][pl.ds(r, n, stride=d), :] = ref3[r, :, c * LANES:(c + 1) * LANES].astype(F32)


def _rms_fwd_orders(name, x, g):
    s = x.shape[0]

    def body(x_ref, g_ref, h_ref, h4_ref, h16_ref, scr):
        xf = x_ref[...]
        r = lax.rsqrt(jnp.mean(xf * xf, axis=-1, keepdims=True) + EPS)
        h = xf * r * g_ref[...]
        h_ref[...] = h.astype(BF16)
        _put(scr, h)
        _to_classes(h4_ref, scr, 4, BF16)
        _to_classes(h16_ref, scr, 16, BF16)

    row = pl.BlockSpec((TS, D_MODEL), lambda i: (i, 0))
    h, h4, h16 = pl.pallas_call(
        body, name=name, grid=(s // TS,), in_specs=[row, pl.BlockSpec((1, D_MODEL), lambda i: (0, 0))],
        out_specs=[row, _class_spec(4, TS, D_MODEL), _class_spec(16, TS, D_MODEL)],
        out_shape=[jax.ShapeDtypeStruct((s, D_MODEL), BF16), jax.ShapeDtypeStruct((4, s // 4, D_MODEL), BF16),
                   jax.ShapeDtypeStruct((16, s // 16, D_MODEL), BF16)],
        scratch_shapes=[_token_scratch(TS, D_MODEL)],
        compiler_params=_cp(("parallel",), 32),
    )(x, g)
    return h, h4.reshape(s, D_MODEL), h16.reshape(s, D_MODEL)


def _rms_bwd(name, dh, x, g, dres, dh4=None, dh16=None):
    s = x.shape[0]
    extra = dh4 is not None

    def body(dh_ref, x_ref, g_ref, dres_ref, *rest):
        if extra:
            dh4_ref, dh16_ref, dx_ref, dg_ref, scr = rest
        else:
            dx_ref, dg_ref = rest
        xf = x_ref[...]
        r = lax.rsqrt(jnp.mean(xf * xf, axis=-1, keepdims=True) + EPS)
        xh = xf * r
        dhf = dh_ref[...].astype(F32)
        if extra:
            _from_classes(scr, dh4_ref, 4)
            dhf = dhf + _get(scr)
            _from_classes(scr, dh16_ref, 16)
            dhf = dhf + _get(scr)
        dxh = dhf * g_ref[...]
        dx_ref[...] = dres_ref[...] + r * (dxh - xh * jnp.mean(dxh * xh, axis=-1, keepdims=True))
        part = jnp.sum(dhf * xh, axis=0, keepdims=True)

        @pl.when(pl.program_id(0) == 0)
        def _():
            dg_ref[...] = part

        @pl.when(pl.program_id(0) > 0)
        def _():
            dg_ref[...] += part

    row = pl.BlockSpec((TS, D_MODEL), lambda i: (i, 0))
    vec = pl.BlockSpec((1, D_MODEL), lambda i: (0, 0))
    in_specs, args, scratch = [row, row, vec, row], [dh, x, g, dres], []
    if extra:
        in_specs += [_class_spec(4, TS, D_MODEL), _class_spec(16, TS, D_MODEL)]
        args += [_class_major(dh4, 4), _class_major(dh16, 16)]
        scratch = [_token_scratch(TS, D_MODEL)]
    return pl.pallas_call(
        body, name=name, grid=(s // TS,), in_specs=in_specs, out_specs=[row, vec],
        out_shape=[jax.ShapeDtypeStruct((s, D_MODEL), F32), jax.ShapeDtypeStruct((1, D_MODEL), F32)],
        scratch_shapes=scratch, compiler_params=_cp(("arbitrary",), 40),
    )(*args)


def _out_proj_loss(y, w_out, resid, g, target):
    s, k = y.shape

    def body(y_ref, w_ref, r_ref, g_ref, t_ref, dx_ref, loss_ref, dg_ref):
        xf = jnp.dot(y_ref[...], w_ref[...], preferred_element_type=F32) + r_ref[...]
        gg = g_ref[...]
        r = lax.rsqrt(jnp.mean(xf * xf, axis=-1, keepdims=True) + EPS)
        xh = xf * r
        e = xh * gg - t_ref[...]
        dy = e * (1.0 / D_MODEL)
        dxh = dy * gg
        dx_ref[...] = r * (dxh - xh * jnp.mean(dxh * xh, axis=-1, keepdims=True))
        lpart = 0.5 * jnp.sum(jnp.mean(e * e, axis=-1, keepdims=True), axis=0, keepdims=True)
        lpart = jnp.broadcast_to(lpart, (8, 128))
        gpart = jnp.sum(dy * xh, axis=0, keepdims=True)

        @pl.when(pl.program_id(0) == 0)
        def _():
            loss_ref[...] = lpart
            dg_ref[...] = gpart

        @pl.when(pl.program_id(0) > 0)
        def _():
            loss_ref[...] += lpart
            dg_ref[...] += gpart

    row = pl.BlockSpec((TS, D_MODEL), lambda i: (i, 0))
    vec = pl.BlockSpec((1, D_MODEL), lambda i: (0, 0))
    return pl.pallas_call(
        body, name="odd_out_proj_loss", grid=(s // TS,),
        in_specs=[pl.BlockSpec((TS, k), lambda i: (i, 0)), pl.BlockSpec((k, D_MODEL), lambda i: (0, 0)), row, vec, row],
        out_specs=[row, pl.BlockSpec((8, 128), lambda i: (0, 0)), vec],
        out_shape=[jax.ShapeDtypeStruct((s, D_MODEL), F32), jax.ShapeDtypeStruct((8, 128), F32),
                   jax.ShapeDtypeStruct((1, D_MODEL), F32)],
        compiler_params=_cp(("arbitrary",), 48),
    )(y, w_out, resid, g, target)


def _zcol(c, tm=TM):
    return pl.BlockSpec((tm, D_MODEL), lambda i, c=c: (i, c))


def _prev_halo(c, tm=TM):
    return pl.BlockSpec((HALO, D_MODEL), lambda i, c=c: (jnp.maximum(i * (tm // HALO) - 1, 0), c))


def _next_halo(c, n_rows, tm=TM):
    last = n_rows // HALO - 1
    return pl.BlockSpec((HALO, D_MODEL), lambda i, c=c: (jnp.minimum((i + 1) * (tm // HALO), last), c))


def _full(shape):
    return pl.BlockSpec(shape, lambda i: (0,) * len(shape))


def _inv_count(first_row, n, w):
    t = first_row + lax.broadcasted_iota(jnp.int32, (n, 1), 0)
    return 1.0 / jnp.minimum(t + 1, w).astype(F32)


def _even_mix_fwd(z, pw, ps, wt, bs):
    s = z.shape[0]

    def body(a_ref, ga_ref, u_ref, v_ref, gb_ref, halo_ref, pw_ref, ps_ref, wt_ref, bs_ref, y_ref):
        i = pl.program_id(0)
        a = a_ref[...].astype(F32)
        halo = jnp.where(i > 0, halo_ref[...].astype(F32), 0.0)
        ext = jnp.concatenate([halo, a], axis=0)
        ga = ga_ref[...].astype(F32)
        sga = ga * _sig(ga)
        for g, w in enumerate(POOL_SIZES):
            cs = slice(g * GROUP_W, (g + 1) * GROUP_W)
            win = _win_sum(ext[:, cs], w, False)[HALO:]
            pooled = win * _inv_count(i * TM, TM, w) - a[:, cs]
            mixed = jnp.dot(pooled.astype(BF16), pw_ref[g], preferred_element_type=F32)
            y_ref[:, cs] = (mixed * ps_ref[:, cs] * sga[:, cs]).astype(BF16)
        gb = gb_ref[...].astype(F32)
        gate = u_ref[...].astype(F32) * (gb * _sig(gb))
        for ch in range(TM // CHUNK):
            rs = slice(ch * CHUNK, (ch + 1) * CHUNK)
            for g in range(4):
                cs = slice(g * GROUP_W, (g + 1) * GROUP_W)
                mixb = jnp.dot(wt_ref[g], v_ref[rs, cs], preferred_element_type=F32) + bs_ref[g]
                y_ref[rs, D_MODEL + g * GROUP_W:D_MODEL + (g + 1) * GROUP_W] = (gate[rs, cs] * mixb).astype(BF16)

    return pl.pallas_call(
        body, name="even_mix_fwd", grid=(s // TM,),
        in_specs=[_zcol(0), _zcol(1), _zcol(2), _zcol(3), _zcol(4), _prev_halo(0),
                  _full((4, GROUP_W, GROUP_W)), _full((1, D_MODEL)), _full((4, CHUNK, CHUNK)), _full((4, CHUNK, 1))],
        out_specs=pl.BlockSpec((TM, 2 * D_MODEL), lambda i: (i, 0)),
        out_shape=jax.ShapeDtypeStruct((s, 2 * D_MODEL), BF16),
        compiler_params=_cp(("parallel",), 48),
    )(z, z, z, z, z, z, pw, ps, wt, bs)


def _even_mix_bwd(dy, z, pw, ps, wt, wtt, bs, after=None):
    s = z.shape[0]
    n_tiles = s // TM
    tail_specs, tail_args = ([ANY], [after]) if after is not None else ([], [])

    def body(dy_ref, a_ref, ga_ref, u_ref, v_ref, gb_ref, halo_ref, dyn_ref, gan_ref, pw_ref, ps_ref, wt_ref, wtt_ref,
             bs_ref, *rest):
        dz_ref, dpw_ref, dps_ref, dws_ref, dbs_ref = rest[-5:]
        i = pl.program_id(0)

        @pl.when(i == 0)
        def _():
            dpw_ref[...] = jnp.zeros_like(dpw_ref)
            dps_ref[...] = jnp.zeros_like(dps_ref)
            dws_ref[...] = jnp.zeros_like(dws_ref)
            dbs_ref[...] = jnp.zeros_like(dbs_ref)

        a = a_ref[...].astype(F32)
        halo = jnp.where(i > 0, halo_ref[...].astype(F32), 0.0)
        ext = jnp.concatenate([halo, a], axis=0)
        ga = ga_ref[...].astype(F32)
        sg = _sig(ga)
        sga = ga * sg
        dsga = sg * (1.0 + ga * (1.0 - sg))
        dya = dy_ref[:, :D_MODEL].astype(F32)
        gan = gan_ref[...].astype(F32)
        dmn_all = jnp.where(i < n_tiles - 1, dyn_ref[...].astype(F32) * ps_ref[...] * (gan * _sig(gan)), 0.0)
        for g, w in enumerate(POOL_SIZES):
            cs = slice(g * GROUP_W, (g + 1) * GROUP_W)
            inv = _inv_count(i * TM, TM, w)
            pooled = _win_sum(ext[:, cs], w, False)[HALO:] * inv - a[:, cs]
            pb = pooled.astype(BF16)
            mixed = jnp.dot(pb, pw_ref[g], preferred_element_type=F32)
            dyg = dya[:, cs]
            psg = ps_ref[:, cs]
            dm = (dyg * psg * sga[:, cs]).astype(BF16)
            dz_ref[:, D_MODEL + g * GROUP_W:D_MODEL + (g + 1) * GROUP_W] = (dyg * mixed * psg * dsga[:, cs]).astype(BF16)
            dps_ref[:, cs] += jnp.sum(dyg * mixed * sga[:, cs], axis=0, keepdims=True)
            dpw_ref[g] += lax.dot_general(pb, dm, (((0,), (0,)), ((), ())), preferred_element_type=F32)
            nt = (((1,), (1,)), ((), ()))
            dpool = lax.dot_general(dm, pw_ref[g], nt, preferred_element_type=F32)
            dpool_n = lax.dot_general(dmn_all[:, cs].astype(BF16), pw_ref[g], nt, preferred_element_type=F32)
            e = jnp.concatenate([dpool * inv, dpool_n * _inv_count((i + 1) * TM, HALO, w)], axis=0)
            dz_ref[:, cs] = (_win_sum(e, w, True)[:TM] - dpool).astype(BF16)

        gb = gb_ref[...].astype(F32)
        sg = _sig(gb)
        sgb = gb * sg
        dsgb = sg * (1.0 + gb * (1.0 - sg))
        u = u_ref[...].astype(F32)
        dyb = dy_ref[:, D_MODEL:].astype(F32)
        tril = lax.broadcasted_iota(jnp.int32, (CHUNK, CHUNK), 0) >= lax.broadcasted_iota(jnp.int32, (CHUNK, CHUNK), 1)
        lane = lax.broadcasted_iota(jnp.int32, (CHUNK, 128), 1)
        for ch in range(TM // CHUNK):
            rs = slice(ch * CHUNK, (ch + 1) * CHUNK)
            for g in range(4):
                cs = slice(g * GROUP_W, (g + 1) * GROUP_W)
                vb = v_ref[rs, cs]
                mixb = jnp.dot(wt_ref[g], vb, preferred_element_type=F32) + bs_ref[g]
                dyu = dyb[rs, cs] * u[rs, cs]
                dmix = dyu * sgb[rs, cs]
                dmb = dmix.astype(BF16)
                o = g * GROUP_W
                dz_ref[rs, 2 * D_MODEL + o:2 * D_MODEL + o + GROUP_W] = (dyb[rs, cs] * mixb * sgb[rs, cs]).astype(BF16)
                dz_ref[rs, 3 * D_MODEL + o:3 * D_MODEL + o + GROUP_W] = jnp.dot(
                    wtt_ref[g], dmb, preferred_element_type=F32).astype(BF16)
                dz_ref[rs, 4 * D_MODEL + o:4 * D_MODEL + o + GROUP_W] = (dyu * mixb * dsgb[rs, cs]).astype(BF16)
                dws = lax.dot_general(dmb, vb, (((1,), (1,)), ((), ())), preferred_element_type=F32)
                dws_ref[g] += jnp.where(tril, dws, 0.0)
                dbs_ref[...] += jnp.where(lane == g, jnp.sum(dmix, axis=1, keepdims=True), 0.0)

    return pl.pallas_call(
        body, name="even_mix_bwd", grid=(n_tiles,),
        in_specs=[pl.BlockSpec((TM, 2 * D_MODEL), lambda i: (i, 0)), _zcol(0), _zcol(1), _zcol(2), _zcol(3), _zcol(4),
                  _prev_halo(0), _next_halo(0, s), _next_halo(1, s),
                  _full((4, GROUP_W, GROUP_W)), _full((1, D_MODEL)), _full((4, CHUNK, CHUNK)), _full((4, CHUNK, CHUNK)),
                  _full((4, CHUNK, 1))] + tail_specs,
        out_specs=[pl.BlockSpec((TM, EVEN_IN), lambda i: (i, 0)), _full((4, GROUP_W, GROUP_W)), _full((1, D_MODEL)),
                   _full((4, CHUNK, CHUNK)), _full((CHUNK, 128))],
        out_shape=[jax.ShapeDtypeStruct((s, EVEN_IN), BF16), jax.ShapeDtypeStruct((4, GROUP_W, GROUP_W), F32),
                   jax.ShapeDtypeStruct((1, D_MODEL), F32), jax.ShapeDtypeStruct((4, CHUNK, CHUNK), F32),
                   jax.ShapeDtypeStruct((CHUNK, 128), F32)],
        compiler_params=_cp(("arbitrary",), 56),
    )(dy, z, z, z, z, z, z, dy, z, pw, ps, wt, wtt, bs, *tail_args)


STAT_W = 128
Q_BLOCKS = 2
Q_ROWS = Q_BLOCKS * ATTN_BLOCK


def _band(d):
    row = lax.broadcasted_iota(jnp.int32, (ATTN_BLOCK, 2 * ATTN_BLOCK), 0)
    col = lax.broadcasted_iota(jnp.int32, (ATTN_BLOCK, 2 * ATTN_BLOCK), 1)
    steps = row + ATTN_BLOCK - col
    return (steps >= 0) & (steps <= ATTN_BLOCK), col >= ATTN_BLOCK, -(steps * d).astype(F32)


def _attn_fwd(z, gi):
    s = z.shape[0]
    d = DILATIONS[gi]
    nb = s // d // ATTN_BLOCK
    nq = nb // Q_BLOCKS

    def spec(which, prev=False):
        cb = which * 3 + gi
        if prev:
            return pl.BlockSpec((ATTN_BLOCK, D_MODEL), lambda r, i: (r * nb + jnp.maximum(Q_BLOCKS * i - 1, 0), cb))
        return pl.BlockSpec((Q_ROWS, D_MODEL), lambda r, i: (r * nq + i, cb))

    def body(q_ref, kp_ref, kc_ref, vp_ref, vc_ref, o_ref, lse_ref):
        i = pl.program_id(1)
        inner, own, negdist = _band(d)
        lane = lax.broadcasted_iota(jnp.int32, (ATTN_BLOCK, STAT_W), 1)
        for b in range(Q_BLOCKS):
            rows = slice(b * ATTN_BLOCK, (b + 1) * ATTN_BLOCK)
            valid = (inner & ((i > 0) | own)) if b == 0 else inner
            stat = jnp.zeros((ATTN_BLOCK, STAT_W), F32)
            for h in range(N_SLOTS):
                sl = slice(h * HEAD_DIM, (h + 1) * HEAD_DIM)
                if b == 0:
                    k = jnp.concatenate([kp_ref[:, sl], kc_ref[:ATTN_BLOCK, sl]], axis=0)
                    v = jnp.concatenate([vp_ref[:, sl], vc_ref[:ATTN_BLOCK, sl]], axis=0)
                else:
                    k = kc_ref[(b - 1) * ATTN_BLOCK:(b + 1) * ATTN_BLOCK, sl]
                    v = vc_ref[(b - 1) * ATTN_BLOCK:(b + 1) * ATTN_BLOCK, sl]
                sc = lax.dot_general(q_ref[rows, sl], k, (((1,), (1,)), ((), ())), preferred_element_type=F32) * SCALE
                sc = jnp.where(valid, sc + SLOPES[gi][h] * negdist, NEG)
                m = jnp.max(sc, axis=-1, keepdims=True)
                p = jnp.exp(sc - m)
                l = jnp.sum(p, axis=-1, keepdims=True)
                o = jnp.dot((p * (1.0 / l)).astype(BF16), v, preferred_element_type=F32)
                o_ref[rows, sl] = o.astype(BF16)
                stat = jnp.where(lane == h, m + jnp.log(l), stat)
            lse_ref[rows, :] = stat

    return pl.pallas_call(
        body, name=f"attn_fwd_d{d}", grid=(d, nq),
        in_specs=[spec(0), spec(1, True), spec(1), spec(2, True), spec(2)],
        out_specs=[pl.BlockSpec((Q_ROWS, D_MODEL), lambda r, i: (r * nq + i, 0)),
                   pl.BlockSpec((Q_ROWS, STAT_W), lambda r, i: (r * nq + i, 0))],
        out_shape=[jax.ShapeDtypeStruct((s, D_MODEL), BF16), jax.ShapeDtypeStruct((s, STAT_W), F32)],
        compiler_params=_cp(("parallel", "parallel"), 32),
    )(z, z, z, z, z)


def _attn_bwd(z, dyc, ltot, dst, dz, gi):
    s = z.shape[0]
    d = DILATIONS[gi]
    nb = s // d // ATTN_BLOCK
    nq = nb // Q_BLOCKS
    n_steps = d * nq

    def rev(cb, width=D_MODEL, prev=False):
        if prev:
            return pl.BlockSpec((ATTN_BLOCK, width), lambda r, n: (r * nb + jnp.maximum(Q_BLOCKS * (nq - 1 - n) - 1, 0), cb))
        return pl.BlockSpec((Q_ROWS, width), lambda r, n: (r * nq + nq - 1 - n, cb))

    def body(q_ref, kp_ref, kc_ref, vp_ref, vc_ref, dy_ref, l_ref, d_ref, dz_in, dz_out, dq_s, dk_s, dv_s, ck_s, cv_s, sems):
        del dz_in
        r = pl.program_id(0)
        n = pl.program_id(1)
        i = nq - 1 - n
        step = r * nq + n

        slot = lax.rem(step, 2)

        def out_copies(sl_):
            rows = pl.ds(pl.multiple_of((r * nq + i) * Q_ROWS, Q_ROWS), Q_ROWS)
            return [pltpu.make_async_copy(stage_ref.at[sl_], dz_out.at[rows, pl.ds((which * 3 + gi) * D_MODEL, D_MODEL)],
                                          sems.at[sl_, which])
                    for which, stage_ref in enumerate((dq_s, dk_s, dv_s))]

        @pl.when(step >= 2)
        def _():
            for cp in out_copies(slot):
                cp.wait()

        @pl.when(n == 0)
        def _():
            ck_s[...] = jnp.zeros_like(ck_s)
            cv_s[...] = jnp.zeros_like(cv_s)

        row = lax.broadcasted_iota(jnp.int32, (Q_ROWS, Q_ROWS + ATTN_BLOCK), 0)
        col = lax.broadcasted_iota(jnp.int32, (Q_ROWS, Q_ROWS + ATTN_BLOCK), 1)
        steps = row + ATTN_BLOCK - col
        valid = (steps >= 0) & (steps <= ATTN_BLOCK) & ((i > 0) | (col >= ATTN_BLOCK))
        negdist = -(steps * d).astype(F32)
        nt = (((1,), (1,)), ((), ()))
        tn = (((0,), (0,)), ((), ()))
        for h in range(N_SLOTS):
            sl = slice(h * HEAD_DIM, (h + 1) * HEAD_DIM)
            q = q_ref[:, sl]
            k = jnp.concatenate([kp_ref[:, sl], kc_ref[:, sl]], axis=0)
            v = jnp.concatenate([vp_ref[:, sl], vc_ref[:, sl]], axis=0)
            dy = dy_ref[:, sl]
            sc = lax.dot_general(q, k, nt, preferred_element_type=F32) * SCALE + SLOPES[gi][h] * negdist
            p = jnp.where(valid, jnp.exp(sc - l_ref[:, h:h + 1]), 0.0)
            dp = lax.dot_general(dy, v, nt, preferred_element_type=F32)
            ds = (p * (dp - d_ref[:, h:h + 1])).astype(BF16)
            dq_s[slot, :, sl] = (jnp.dot(ds, k, preferred_element_type=F32) * SCALE).astype(BF16)
            dk = lax.dot_general(ds, q, tn, preferred_element_type=F32) * SCALE
            dv = lax.dot_general(p.astype(BF16), dy, tn, preferred_element_type=F32)
            dk_s[slot, :Q_ROWS - ATTN_BLOCK, sl] = dk[ATTN_BLOCK:Q_ROWS].astype(BF16)
            dv_s[slot, :Q_ROWS - ATTN_BLOCK, sl] = dv[ATTN_BLOCK:Q_ROWS].astype(BF16)
            dk_s[slot, Q_ROWS - ATTN_BLOCK:, sl] = (ck_s[:, sl] + dk[Q_ROWS:]).astype(BF16)
            dv_s[slot, Q_ROWS - ATTN_BLOCK:, sl] = (cv_s[:, sl] + dv[Q_ROWS:]).astype(BF16)
            ck_s[:, sl] = dk[:ATTN_BLOCK]
            cv_s[:, sl] = dv[:ATTN_BLOCK]

        for cp in out_copies(slot):
            cp.start()

        @pl.when(step == n_steps - 1)
        def _():
            for cp in out_copies(1 - slot) + out_copies(slot):
                cp.wait()

    assert n_steps >= 2
    stage = pltpu.VMEM((2, Q_ROWS, D_MODEL), BF16)
    carry = pltpu.VMEM((ATTN_BLOCK, D_MODEL), F32)
    return pl.pallas_call(
        body, name=f"attn_bwd_d{d}", grid=(d, nq),
        in_specs=[rev(gi), rev(3 + gi, prev=True), rev(3 + gi), rev(6 + gi, prev=True), rev(6 + gi),
                  rev(0), rev(0, STAT_W), rev(0, STAT_W), ANY],
        out_specs=ANY,
        out_shape=jax.ShapeDtypeStruct((s, ODD_IN), BF16),
        scratch_shapes=[stage, stage, stage, carry, carry, pltpu.SemaphoreType.DMA((2, 3))],
        input_output_aliases={8: 0},
        compiler_params=_cp(("arbitrary", "arbitrary"), 32),
    )(z, z, z, z, z, dyc, ltot, dst, dz)


def _odd_mix_fwd(z, os_, lses, cw):
    s = z.shape[0]

    def body(o0, o1, o2, l0, l1, l2, gc_ref, db_ref, dc_ref, dx_ref, gd_ref, hc_ref, hx_ref, cw_ref, y_ref, yc_ref, lt_ref,
             lt4_ref, lt16_ref, scr_o, scr_o2, scr_l):
        i = pl.program_id(0)
        _from_classes(scr_l, l1, 4)
        lse1 = _get(scr_l)
        _from_classes(scr_l, l2, 16)
        ls = [l0[...], lse1, _get(scr_l)]
        lmax = jnp.maximum(jnp.maximum(ls[0], ls[1]), ls[2])
        es = [jnp.exp(l - lmax) for l in ls]
        den = es[0] + es[1] + es[2]
        alpha = [e / den for e in es]
        ltot = lmax + jnp.log(den)
        lt_ref[...] = ltot
        _put(scr_l, ltot)
        _to_classes(lt4_ref, scr_l, 4, F32)
        _to_classes(lt16_ref, scr_l, 16, F32)
        _from_classes(scr_o, o1, 4)
        _from_classes(scr_o2, o2, 16)
        for h in range(N_SLOTS):
            sl = slice(h * HEAD_DIM, (h + 1) * HEAD_DIM)
            yc = (alpha[0][:, h:h + 1] * o0[:, sl].astype(F32) + alpha[1][:, h:h + 1] * scr_o[h]
                  + alpha[2][:, h:h + 1] * scr_o2[h])
            yc_ref[:, sl] = yc.astype(BF16)
            gc = gc_ref[:, sl].astype(F32)
            y_ref[:, sl] = (yc * (gc * _sig(gc))).astype(BF16)
            zc = dc_ref[:, sl].astype(F32) * dx_ref[:, sl].astype(F32)
            halo = jnp.where(i > 0, hc_ref[:, sl].astype(F32) * hx_ref[:, sl].astype(F32), 0.0)
            ext = jnp.concatenate([halo, zc], axis=0)
            z1 = pltpu.roll(ext, 1, 0)[HALO:]
            z2 = pltpu.roll(ext, 2, 0)[HALO:]
            conv = cw_ref[0:1, sl] * z2 + cw_ref[1:2, sl] * z1 + cw_ref[2:3, sl] * zc
            gd = gd_ref[:, sl].astype(F32)
            y_ref[:, D_MODEL + h * HEAD_DIM:D_MODEL + (h + 1) * HEAD_DIM] = (
                db_ref[:, sl].astype(F32) * conv * (gd * _sig(gd))).astype(BF16)

    row = pl.BlockSpec((TMO, D_MODEL), lambda i: (i, 0))
    stat = pl.BlockSpec((TMO, STAT_W), lambda i: (i, 0))
    y, ycr, lt, lt4, lt16 = pl.pallas_call(
        body, name="odd_mix_fwd", grid=(s // TMO,),
        in_specs=[row, _class_spec(4, TMO, D_MODEL), _class_spec(16, TMO, D_MODEL),
                  stat, _class_spec(4, TMO, STAT_W), _class_spec(16, TMO, STAT_W),
                  _zcol(9, TMO), _zcol(10, TMO), _zcol(11, TMO), _zcol(12, TMO), _zcol(13, TMO), _prev_halo(11, TMO),
                  _prev_halo(12, TMO), _full((3, D_MODEL))],
        out_specs=[pl.BlockSpec((TMO, 2 * D_MODEL), lambda i: (i, 0)), row, stat, _class_spec(4, TMO, STAT_W),
                   _class_spec(16, TMO, STAT_W)],
        out_shape=[jax.ShapeDtypeStruct((s, 2 * D_MODEL), BF16), jax.ShapeDtypeStruct((s, D_MODEL), BF16),
                   jax.ShapeDtypeStruct((s, STAT_W), F32), jax.ShapeDtypeStruct((4, s // 4, STAT_W), F32),
                   jax.ShapeDtypeStruct((16, s // 16, STAT_W), F32)],
        scratch_shapes=[_token_scratch(TMO, D_MODEL), _token_scratch(TMO, D_MODEL), _token_scratch(TMO, STAT_W)],
        compiler_params=_cp(("parallel",), 48),
    )(os_[0], _class_major(os_[1], 4), _class_major(os_[2], 16), lses[0], _class_major(lses[1], 4),
      _class_major(lses[2], 16), z, z, z, z, z, z, z, cw)
    return y, ycr, [lt, lt4.reshape(s, STAT_W), lt16.reshape(s, STAT_W)]


def _odd_mix_bwd(dy, z, ycr, cw):
    s = z.shape[0]
    n_tiles = s // TMO
    rest = ODD_IN - QKV_BLOCKS * D_MODEL

    def body(dy_ref, yc_ref, gc_ref, db_ref, dc_ref, dx_ref, gd_ref, hc_ref, hx_ref, dyn_ref, dbn_ref, gdn_ref, cw_ref,
             dz_ref, dyc_ref, dyc4_ref, dyc16_ref, dd_ref, dd4_ref, dd16_ref, dcw_ref, stages, sems, scr_o, scr_l):
        i = pl.program_id(0)
        slot = lax.rem(i, 2)

        def out_copy(sl_):
            return pltpu.make_async_copy(
                stages.at[sl_], dz_ref.at[pl.ds(pl.multiple_of(i * TMO, TMO), TMO), pl.ds(QKV_BLOCKS * D_MODEL, rest)],
                sems.at[sl_])

        @pl.when(i >= 2)
        def _():
            out_copy(slot).wait()

        stage = stages.at[slot]

        @pl.when(i == 0)
        def _():
            dcw_ref[...] = jnp.zeros_like(dcw_ref)

        lane = lax.broadcasted_iota(jnp.int32, (TMO, STAT_W), 1)
        stat = jnp.zeros((TMO, STAT_W), F32)
        nrow = TMO + HALO
        for h in range(N_SLOTS):
            sl = slice(h * HEAD_DIM, (h + 1) * HEAD_DIM)
            sd = slice(D_MODEL + h * HEAD_DIM, D_MODEL + (h + 1) * HEAD_DIM)
            dyc_in = dy_ref[:, sl].astype(F32)
            gc = gc_ref[:, sl].astype(F32)
            sg = _sig(gc)
            yc = yc_ref[:, sl].astype(F32)
            dyc = dyc_in * (gc * sg)
            dyc_ref[:, sl] = dyc.astype(BF16)
            scr_o[h] = dyc
            stage[:, sl] = (dyc_in * yc * (sg * (1.0 + gc * (1.0 - sg)))).astype(BF16)
            stat = jnp.where(lane == h, jnp.sum(dyc * yc, axis=-1, keepdims=True), stat)
            dc = dc_ref[:, sl].astype(F32)
            dx = dx_ref[:, sl].astype(F32)
            zc = dc * dx
            halo = jnp.where(i > 0, hc_ref[:, sl].astype(F32) * hx_ref[:, sl].astype(F32), 0.0)
            ext = jnp.concatenate([halo, zc], axis=0)
            z1 = pltpu.roll(ext, 1, 0)[HALO:]
            z2 = pltpu.roll(ext, 2, 0)[HALO:]
            w0, w1, w2 = cw_ref[0:1, sl], cw_ref[1:2, sl], cw_ref[2:3, sl]
            conv = w0 * z2 + w1 * z1 + w2 * zc
            gd = gd_ref[:, sl].astype(F32)
            sg = _sig(gd)
            sgd = gd * sg
            db = db_ref[:, sl].astype(F32)
            dyd = dy_ref[:, sd].astype(F32)
            dconv = dyd * db * sgd
            gdn = gdn_ref[:, sl].astype(F32)
            dconv_n = jnp.where(i < n_tiles - 1,
                                dyn_ref[:, sl].astype(F32) * dbn_ref[:, sl].astype(F32) * (gdn * _sig(gdn)), 0.0)
            extn = jnp.concatenate([dconv, dconv_n], axis=0)
            dzc = w2 * dconv + w1 * pltpu.roll(extn, nrow - 1, 0)[:TMO] + w0 * pltpu.roll(extn, nrow - 2, 0)[:TMO]
            stage[:, sd] = (dyd * conv * sgd).astype(BF16)
            stage[:, 2 * D_MODEL + h * HEAD_DIM:2 * D_MODEL + (h + 1) * HEAD_DIM] = (dzc * dx).astype(BF16)
            stage[:, 3 * D_MODEL + h * HEAD_DIM:3 * D_MODEL + (h + 1) * HEAD_DIM] = (dzc * dc).astype(BF16)
            stage[:, 4 * D_MODEL + h * HEAD_DIM:4 * D_MODEL + (h + 1) * HEAD_DIM] = (
                dyd * db * conv * (sg * (1.0 + gd * (1.0 - sg)))).astype(BF16)
            for tap, shifted in enumerate((z2, z1, zc)):
                dcw_ref[tap:tap + 1, sl] += jnp.sum(dconv * shifted, axis=0, keepdims=True)
        _to_classes(dyc4_ref, scr_o, 4, BF16)
        _to_classes(dyc16_ref, scr_o, 16, BF16)
        dd_ref[...] = stat
        _put(scr_l, stat)
        _to_classes(dd4_ref, scr_l, 4, F32)
        _to_classes(dd16_ref, scr_l, 16, F32)

        out_copy(slot).start()

        @pl.when(i == n_tiles - 1)
        def _():
            out_copy(1 - slot).wait()
            out_copy(slot).wait()

    assert n_tiles >= 2
    row = pl.BlockSpec((TMO, D_MODEL), lambda i: (i, 0))
    stat = pl.BlockSpec((TMO, STAT_W), lambda i: (i, 0))
    dz, dyc, dyc4, dyc16, dd, dd4, dd16, g_conv = pl.pallas_call(
        body, name="odd_mix_bwd", grid=(n_tiles,),
        in_specs=[pl.BlockSpec((TMO, 2 * D_MODEL), lambda i: (i, 0)), row, _zcol(9, TMO), _zcol(10, TMO), _zcol(11, TMO),
                  _zcol(12, TMO), _zcol(13, TMO), _prev_halo(11, TMO), _prev_halo(12, TMO), _next_halo(1, s, TMO),
                  _next_halo(10, s, TMO), _next_halo(13, s, TMO), _full((3, D_MODEL))],
        out_specs=[ANY, row, _class_spec(4, TMO, D_MODEL), _class_spec(16, TMO, D_MODEL),
                   stat, _class_spec(4, TMO, STAT_W), _class_spec(16, TMO, STAT_W), _full((3, D_MODEL))],
        out_shape=[jax.ShapeDtypeStruct((s, ODD_IN), BF16), jax.ShapeDtypeStruct((s, D_MODEL), BF16),
                   jax.ShapeDtypeStruct((4, s // 4, D_MODEL), BF16), jax.ShapeDtypeStruct((16, s // 16, D_MODEL), BF16),
                   jax.ShapeDtypeStruct((s, STAT_W), F32), jax.ShapeDtypeStruct((4, s // 4, STAT_W), F32),
                   jax.ShapeDtypeStruct((16, s // 16, STAT_W), F32), jax.ShapeDtypeStruct((3, D_MODEL), F32)],
        scratch_shapes=[pltpu.VMEM((2, TMO, rest), BF16), pltpu.SemaphoreType.DMA((2,)), _token_scratch(TMO, D_MODEL),
                        _token_scratch(TMO, STAT_W)],
        compiler_params=_cp(("arbitrary",), 48),
    )(dy, ycr, z, z, z, z, z, z, z, dy, z, z, cw)
    dyc = [dyc, dyc4.reshape(s, D_MODEL), dyc16.reshape(s, D_MODEL)]
    dd = [dd, dd4.reshape(s, STAT_W), dd16.reshape(s, STAT_W)]
    return dz, dyc, dd, g_conv


def _cols_of_order(order):
    if order == 0:
        return (lambda j: jnp.where(j < 3, 3 * j, j + 6)), 8
    return (lambda j: 3 * j + order), 3


class _Hooks:
    def before_even(self):
        return None

    def odd_weights(self, w, x1):
        return w

    def odd_grads_ready(self, g_w_in_o, g_w_out_o):
        return None

    def even_mix_done(self, dz_e):
        return None

    def backward_done(self, dx0):
        return None


def _local_step(x, target, w, hooks=_Hooks()):
    tril = jnp.tril(jnp.ones((CHUNK, CHUNK), bool))
    wt = jnp.where(tril[None], w["ws"], 0.0).astype(BF16)
    wtt = jnp.swapaxes(wt, 1, 2)
    bs = w["bs"].reshape(4, CHUNK, 1)

    h_e = _rms_fwd("rms_fwd_even", x, w["even_norm"], after=hooks.before_even())
    z_e = _mm_nn("even_in_proj", h_e, w["w_in_e"], IN_ROWS, 1280, BF16)
    y_e = _even_mix_fwd(z_e, w["pool_w"], w["pool_scale"], wt, bs)
    x1 = _mm_nn("even_out_proj", y_e, w["w_out_e"], MM_ROWS, 1024, F32, resid=x)
    w = hooks.odd_weights(w, x1)
    h_o = _rms_fwd_orders("rms_fwd_odd", x1, w["odd_norm"])
    z_o = None
    for o in range(3):
        cols, n_cols = _cols_of_order(o)
        z_o = _mm_nn(f"odd_in_proj_o{o}", h_o[o], w["w_in_o"], IN_ROWS, D_MODEL, BF16, col_map=cols, n_cols=n_cols,
                     into=z_o)
    att = [_attn_fwd(z_o, gi) for gi in range(3)]
    y_o, ycr, ltot = _odd_mix_fwd(z_o, [a[0] for a in att], [a[1] for a in att], w["conv_w"])
    dx2, loss8, g_final = _out_proj_loss(y_o, w["w_out_o"], x1, w["final_norm"], target)

    g_w_out_o = _mm_tn("odd_out_proj_dw", y_o, dx2, 1024, DW_OUT_TOKENS)
    dy_o = _mm_nt("odd_out_proj_dy", dx2, w["w_out_o"], MM_ROWS, 1024, BF16)
    dz_o, dyc, dst, g_conv = _odd_mix_bwd(dy_o, z_o, ycr, w["conv_w"])
    for gi in range(3):
        dz_o = _attn_bwd(z_o, dyc[gi], ltot[gi], dst[gi], dz_o, gi)
    g_w_in_o, dh_o = None, []
    for o in range(3):
        cols, n_cols = _cols_of_order(o)
        g_w_in_o = _mm_tn(f"odd_in_proj_dw_o{o}", h_o[o], dz_o, D_MODEL, DW_IN_TOKENS, col_map=cols, n_cols=n_cols,
                          into=g_w_in_o)
        dh_o.append(_mm_nt(f"odd_in_proj_dh_o{o}", dz_o, w["w_in_o"], IN_ROWS, D_MODEL, BF16, k_map=cols, nk=n_cols))
    dx1, g_odd_norm = _rms_bwd("rms_bwd_odd", dh_o[0], x1, w["odd_norm"], dx2, dh4=dh_o[1], dh16=dh_o[2])
    after = hooks.odd_grads_ready(g_w_in_o, g_w_out_o)
    g_w_out_e = _mm_tn("even_out_proj_dw", y_e, dx1, 1024, DW_OUT_TOKENS, after=after)
    dy_e = _mm_nt("even_out_proj_dy", dx1, w["w_out_e"], MM_ROWS, 1024, BF16)
    dz_e, g_pw, g_ps, g_ws, g_bs = _even_mix_bwd(dy_e, z_e, w["pool_w"], w["pool_scale"], wt, wtt, bs)
    after = hooks.even_mix_done(dz_e)
    g_w_in_e = _mm_tn("even_in_proj_dw", h_e, dz_e, 1280, DW_IN_TOKENS, after=after)
    dh_e = _mm_nt("even_in_proj_dh", dz_e, w["w_in_e"], MM_ROWS, 2560, F32)
    dx0, g_even_norm = _rms_bwd("rms_bwd_even", dh_e, x, w["even_norm"], dx1)
    hooks.backward_done(dx0)

    grads = dict(w_in_e=g_w_in_e, pool_w=g_pw, w_out_e=g_w_out_e, w_in_o=g_w_in_o, w_out_o=g_w_out_o,
                 even_norm=g_even_norm, pool_scale=g_ps, ws=g_ws, bs=g_bs[:, :4].T, final_norm=g_final,
                 odd_norm=g_odd_norm, conv_w=g_conv)
    return loss8[0, 0], dx0, grads


class _Big(NamedTuple):
    name: str
    full: tuple
    haxis: int
    kaxis: int
    sub: int


BIGS = (
    _Big("w_in_e", (1024, 5120), 0, 1, 2),
    _Big("pool_w", (4, 256, 256), 0, 1, 1),
    _Big("w_out_e", (2048, 1024), 1, 0, 1),
    _Big("w_in_o", (1024, 14336), 0, 1, 4),
    _Big("w_out_o", (2048, 1024), 1, 0, 1),
)
N_BIG = len(BIGS)


def _shape(b, half=False, shard=False):
    return tuple(n // (2 if (half and ax == b.haxis) else 1) // (4 if (shard and ax == b.kaxis) else 1)
                 for ax, n in enumerate(b.full))


def _at(ref, b, h=None, k=None):
    idx = []
    for ax, n in enumerate(b.full):
        if ax == b.haxis and h is not None:
            idx.append(pl.ds(h * (n // 2), n // 2))
        elif ax == b.kaxis and k is not None:
            idx.append(pl.ds(k * (n // 4), n // 4))
        else:
            idx.append(slice(None))
    return ref.at[tuple(idx)]


def _place():
    x, y, c = lax.axis_index("x"), lax.axis_index("y"), lax.axis_index("c")
    chips = [(1 - x, y), (x, 1 - y), (1 - x, 1 - y)]
    return x, y, c, 2 * x + y, chips, [2 * cx + cy for cx, cy in chips]


def _piece_shape(b):
    return (4, 2) + _shape(b, half=True, shard=True)


def _gather_weights(bigs, shards, tiny):
    nb = len(bigs)

    def body(*refs):
        ins, tiny_in = refs[:nb], refs[nb]
        outs, tiny_out = refs[nb + 1:2 * nb + 1], refs[2 * nb + 1]
        send, recv, loc = refs[2 * nb + 2:]
        x, y, c, k_me, chips, ks = _place()
        sib = (x, y, 1 - c)

        def rc(src, dst, sem, to):
            return pltpu.make_async_remote_copy(src_ref=src, dst_ref=dst, send_sem=send.at[sem], recv_sem=recv.at[sem],
                                                device_id=to, device_id_type=MESH)

        own = pltpu.make_async_copy(tiny_in, tiny_out.at[k_me], loc)
        own.start()
        sends = []
        for j, chip in enumerate(chips):
            for a, b in enumerate(bigs):
                sends.append(rc(_at(ins[a], b, h=c), outs[a].at[k_me, c], 6 * a + j, (*chip, c)))
            sends.append(rc(tiny_in, tiny_out.at[k_me], 6 * nb + j, (*chip, c)))
        for cp in sends:
            cp.start()
        for j in range(3):
            for a in range(nb):
                piece = outs[a].at[ks[j], c]
                rc(piece, piece, 6 * a + j, sib).wait_recv()
                fwd = rc(piece, piece, 6 * a + 3 + j, sib)
                fwd.start()
                sends.append(fwd)
            rc(tiny_in, tiny_out.at[ks[j]], 6 * nb + j, sib).wait_recv()
        for j in range(3):
            for a in range(nb):
                piece = outs[a].at[ks[j], 1 - c]
                rc(piece, piece, 6 * a + 3 + j, sib).wait_recv()
        for cp in sends:
            cp.wait_send()
        own.wait()

    n_sem = 6 * nb + 3
    return pl.pallas_call(
        body, name="gather_even_weights",
        in_specs=[ANY] * (nb + 1), out_specs=[ANY] * (nb + 1),
        out_shape=[jax.ShapeDtypeStruct(_piece_shape(b), BF16) for b in bigs]
        + [jax.ShapeDtypeStruct((4,) + tiny.shape, F32)],
        scratch_shapes=[pltpu.SemaphoreType.DMA((n_sem,)), pltpu.SemaphoreType.DMA((n_sem,)), pltpu.SemaphoreType.DMA(())],
    )(*shards, tiny)


def _assemble(b, pieces, shard, k_arr):
    blk = _blk(b)

    def body(k_ref, p_ref, s_ref, o_ref):
        mine = pl.program_id(0) == k_ref[0]

        @pl.when(mine)
        def _():
            o_ref[...] = s_ref[...]

        @pl.when(jnp.logical_not(mine))
        def _():
            o_ref[...] = p_ref[...]

    return pl.pallas_call(
        body, name=f"assemble_{b.name}",
        grid_spec=pltpu.PrefetchScalarGridSpec(
            num_scalar_prefetch=1, grid=(4, 2, b.sub),
            in_specs=[pl.BlockSpec((None, None) + blk, lambda k, h, st, k_ref: (k, h) + _bidx(b, 0, 0, st)),
                      pl.BlockSpec(blk, lambda k, h, st, k_ref: _bidx(b, h, 0, st))],
            out_specs=pl.BlockSpec(blk, lambda k, h, st, k_ref: _bidx(b, h, k, st))),
        out_shape=jax.ShapeDtypeStruct(b.full, BF16),
        compiler_params=_cp(("arbitrary", "arbitrary", "arbitrary"), 32),
    )(k_arr, pieces, shard)


def _copies_to_chips(bigs):
    def copies(srcs, lands, send, recv, waiting=False):
        _, _, c, k_me, chips, _ = _place()
        return [pltpu.make_async_remote_copy(
                    src_ref=_at(srcs[a], b, h=c), dst_ref=lands[a].at[k_me, c], send_sem=send.at[3 * a + j],
                    recv_sem=recv.at[3 * a + j], device_id=(*chips[j], c), device_id_type=MESH)
                for j in range(3) for a, b in enumerate(bigs)]
    return copies


def _copies_swap_halves(bigs):
    def copies(srcs, lands, send, recv, waiting=False):
        x, y, c, _, _, _ = _place()
        return [pltpu.make_async_remote_copy(
                    src_ref=_at(srcs[a], b, h=1 - c), dst_ref=lands[a], send_sem=send.at[a], recv_sem=recv.at[a],
                    device_id=(x, y, 1 - c), device_id_type=MESH)
                for a, b in enumerate(bigs)]
    return copies


def _copies_partials(bigs):
    def copies(srcs, lands, send, recv, waiting=False):
        _, _, c, _, chips, ks = _place()
        return [pltpu.make_async_remote_copy(
                    src_ref=_at(srcs[a], b, k=ks[j]), dst_ref=lands[a].at[j], send_sem=send.at[3 * a + j],
                    recv_sem=recv.at[3 * a + j], device_id=(*chips[j], c), device_id_type=MESH)
                for j in range(3) for a, b in enumerate(bigs)]
    return copies


def _exchange(name, srcs, land_shapes, copies_of, n_copies):
    ns = len(srcs)

    def body(*refs):
        copies = copies_of(refs[:ns], refs[ns:ns + len(land_shapes)], refs[-2], refs[-1])
        for cp in copies:
            cp.start()
        for cp in copies:
            cp.wait()

    return pl.pallas_call(
        body, name=name, in_specs=[ANY] * ns, out_specs=[ANY] * len(land_shapes), out_shape=land_shapes,
        scratch_shapes=[pltpu.SemaphoreType.DMA((n_copies,)), pltpu.SemaphoreType.DMA((n_copies,))],
    )(*srcs)


HBM = pl.BlockSpec(memory_space=pltpu.HBM)
SEM = pl.BlockSpec(memory_space=pltpu.SEMAPHORE)
SIDE_EFFECT = pltpu.SideEffectType.DATAFLOW_SIDE_EFFECTING


def _in_hbm(a):
    return pltpu.with_memory_space_constraint(a, pltpu.HBM)


def _exchange_start(name, srcs, land_shapes, copies_of, n_copies, after=None):
    ns, nl = len(srcs), len(land_shapes)
    lands = [lax.empty(sh.shape, sh.dtype) for sh in land_shapes]
    tail = [] if after is None else [after]
    n_in = ns + nl + len(tail)

    def body(*refs):
        send, recv, token = refs[n_in], refs[n_in + 1], refs[-1]
        for cp in copies_of(refs[:ns], refs[ns:ns + nl], send, recv):
            cp.start()
        token[...] = jnp.zeros_like(token)

    thru = [pltpu.HBM(a.shape, a.dtype) for a in (*srcs, *lands)]
    send, recv, *bufs, token = pl.pallas_call(
        body, name=name,
        out_shape=(pltpu.SemaphoreType.DMA((n_copies,)), pltpu.SemaphoreType.DMA((n_copies,)), *thru,
                   jax.ShapeDtypeStruct((8, 128), F32)),
        in_specs=[HBM] * (ns + nl) + [ANY] * len(tail),
        out_specs=(SEM, SEM, *([HBM] * (ns + nl)), pl.BlockSpec(memory_space=pltpu.VMEM)),
        input_output_aliases={i: 2 + i for i in range(ns + nl)},
        compiler_params=pltpu.CompilerParams(has_side_effects=SIDE_EFFECT),
    )(*[_in_hbm(a) for a in (*srcs, *lands)], *tail)
    return (send, recv, bufs, ns), token


def _exchange_wait(name, state, copies_of, after):
    send, recv, bufs, ns = state
    n = len(bufs)

    def body(*refs):
        ins = refs[:n]
        for cp in copies_of(ins[:ns], ins[ns:], refs[n], refs[n + 1], waiting=True):
            cp.wait_send()
            cp.wait_recv()

    out = pl.pallas_call(
        body, name=name, out_shape=tuple(pltpu.HBM(a.shape, a.dtype) for a in bufs),
        in_specs=[HBM] * n + [SEM, SEM, ANY], out_specs=tuple([HBM] * n),
        input_output_aliases={i: i for i in range(n)},
        compiler_params=pltpu.CompilerParams(has_side_effects=SIDE_EFFECT),
    )(*bufs, send, recv, after)
    return list(out[:ns]), list(out[ns:])


def _finish_gather(bigs, pieces):
    nb = len(bigs)

    def body(*refs):
        outs, send, recv = refs[nb:2 * nb], refs[2 * nb], refs[2 * nb + 1]
        x, y, c, _, _, ks = _place()
        fwd = [pltpu.make_async_remote_copy(
                   src_ref=outs[a].at[ks[j], c], dst_ref=outs[a].at[ks[j], c], send_sem=send.at[3 * a + j],
                   recv_sem=recv.at[3 * a + j], device_id=(x, y, 1 - c), device_id_type=MESH)
               for j in range(3) for a in range(nb)]
        for cp in fwd:
            cp.start()
        for cp in fwd:
            cp.wait()

    return pl.pallas_call(
        body, name="gather_odd_finish", in_specs=[ANY] * nb, out_specs=[ANY] * nb,
        out_shape=[jax.ShapeDtypeStruct(_piece_shape(b), BF16) for b in bigs],
        scratch_shapes=[pltpu.SemaphoreType.DMA((3 * nb,)), pltpu.SemaphoreType.DMA((3 * nb,))],
        input_output_aliases={a: a for a in range(nb)},
    )(*pieces)


def _blk(b):
    win = _shape(b, half=True, shard=True)
    return (win[0] // b.sub,) + win[1:]


def _bidx(b, h, k, st):
    idx = [0] * len(b.full)
    idx[b.haxis] = h
    idx[b.kaxis] = k
    idx[0] = idx[0] * b.sub + st
    return tuple(idx)


def _chip_sum(b, g, got, c_arr):
    blk = _blk(b)

    def body(c_ref, g_ref, r_ref, o_ref):
        del c_ref
        o_ref[...] = (g_ref[...] + r_ref[...]).astype(BF16)

    half = pl.BlockSpec(blk, lambda k, st, c_ref: _bidx(b, 0, k, st))
    return pl.pallas_call(
        body, name=f"rs_chip_sum_{b.name}",
        grid_spec=pltpu.PrefetchScalarGridSpec(
            num_scalar_prefetch=1, grid=(4, b.sub),
            in_specs=[pl.BlockSpec(blk, lambda k, st, c_ref: _bidx(b, c_ref[0], k, st)), half], out_specs=half),
        out_shape=jax.ShapeDtypeStruct(_shape(b, half=True), BF16),
        compiler_params=_cp(("arbitrary", "arbitrary"), 40),
    )(c_arr, g, got)


def _half_shapes(bigs):
    return [jax.ShapeDtypeStruct(_shape(b, half=True), F32) for b in bigs]


def _partial_shapes(bigs):
    return [jax.ShapeDtypeStruct((3,) + _shape(b, half=True, shard=True), BF16) for b in bigs]


def _shard_sum(b, mine, got, ck_arr):
    blk = _blk(b)

    def body(ck_ref, m_ref, r0, r1, r2, o_ref):
        del ck_ref
        o_ref[...] = (m_ref[...].astype(F32) + r0[...].astype(F32)) + (r1[...].astype(F32) + r2[...].astype(F32))

    def peer(j):
        return pl.BlockSpec((None,) + blk, lambda st, ck: (j,) + _bidx(b, 0, 0, st))

    return pl.pallas_call(
        body, name=f"rs_shard_sum_{b.name}",
        grid_spec=pltpu.PrefetchScalarGridSpec(
            num_scalar_prefetch=1, grid=(b.sub,),
            in_specs=[pl.BlockSpec(blk, lambda st, ck: _bidx(b, 0, ck[1], st)), peer(0), peer(1), peer(2)],
            out_specs=pl.BlockSpec(blk, lambda st, ck: _bidx(b, ck[0], 0, st))),
        out_shape=jax.ShapeDtypeStruct(_shape(b, shard=True), F32),
        compiler_params=_cp(("arbitrary",), 40),
    )(ck_arr, mine, got, got, got)


def _share_halves(name, bigs, gs):
    nb = len(bigs)

    def body(*refs):
        outs, send, recv = refs[nb:2 * nb], refs[2 * nb], refs[2 * nb + 1]
        x, y, c, _, _, _ = _place()
        copies = [pltpu.make_async_remote_copy(src_ref=_at(outs[a], b, h=c), dst_ref=_at(outs[a], b, h=c),
                                               send_sem=send.at[a], recv_sem=recv.at[a], device_id=(x, y, 1 - c),
                                               device_id_type=MESH)
                  for a, b in enumerate(bigs)]
        for cp in copies:
            cp.start()
        for cp in copies:
            cp.wait()

    return pl.pallas_call(
        body, name=name, in_specs=[ANY] * nb, out_specs=[ANY] * nb,
        out_shape=[jax.ShapeDtypeStruct(_shape(b, shard=True), F32) for b in bigs],
        scratch_shapes=[pltpu.SemaphoreType.DMA((nb,)), pltpu.SemaphoreType.DMA((nb,))],
        input_output_aliases={a: a for a in range(nb)},
    )(*gs)


def _gather_small(block):
    m_per, n = block.shape

    def body(x_ref, out_ref, send_sems, recv_sems, local_sem):
        x, y, c = lax.axis_index("x"), lax.axis_index("y"), lax.axis_index("c")
        me, sibling = (x, y, c), (x, y, 1 - c)
        chips = [(1 - x, y), (x, 1 - y), (1 - x, 1 - y)]

        def rows(px, py, pc):
            return out_ref.at[pl.ds((4 * px + 2 * py + pc) * m_per, m_per), :]

        def copy(k, blk, to, src=None):
            return pltpu.make_async_remote_copy(
                src_ref=rows(*blk) if src is None else src, dst_ref=rows(*blk), send_sem=send_sems.at[k],
                recv_sem=recv_sems.at[k], device_id=to, device_id_type=MESH)

        mine = pltpu.make_async_copy(x_ref, rows(*me), local_sem)
        mine.start()
        first = [copy(0, me, sibling, src=x_ref)]
        first += [copy(1 + j, me, (*chip, c), src=x_ref) for j, chip in enumerate(chips)]
        for cp in first:
            cp.start()
        passed = [copy(4 + j, (*chip, c), sibling) for j, chip in enumerate(chips)]
        for j, chip in enumerate(chips):
            copy(1 + j, (*chip, c), me).wait_recv()
            passed[j].start()
        copy(0, sibling, me).wait_recv()
        for j, chip in enumerate(chips):
            copy(4 + j, (*chip, 1 - c), me).wait_recv()
        for cp in first + passed:
            cp.wait_send()
        mine.wait()

    return pl.pallas_call(
        body, name="gather_small_grads",
        out_shape=jax.ShapeDtypeStruct((8 * m_per, n), block.dtype),
        in_specs=[pl.BlockSpec(memory_space=pltpu.VMEM)], out_specs=pl.BlockSpec(memory_space=pltpu.VMEM),
        scratch_shapes=[pltpu.SemaphoreType.DMA((7,)), pltpu.SemaphoreType.DMA((7,)), pltpu.SemaphoreType.DMA],
    )(block)


def _sum_small(stack):
    _, m_per, n = stack.shape

    def body(x_ref, o_ref):
        acc = x_ref[0]
        for dev in range(1, 8):
            acc = acc + x_ref[dev]
        o_ref[...] = acc

    return pl.pallas_call(body, name="sum_small_grads", out_shape=jax.ShapeDtypeStruct((m_per, n), F32))(stack)


def _adam_update(w_ref, g_ref, m_ref, v_ref, d_ref, mo_ref, vo_ref):
    gg = g_ref[...]
    mn = ADAM_B1 * m_ref[...] + (1.0 - ADAM_B1) * gg
    vn = ADAM_B2 * v_ref[...] + (1.0 - ADAM_B2) * (gg * gg)
    m_hat = mn / (1.0 - ADAM_B1 ** ADAM_STEP)
    v_hat = vn / (1.0 - ADAM_B2 ** ADAM_STEP)
    d_ref[...] = -ADAM_LR * (m_hat / (jnp.sqrt(v_hat) + ADAM_EPS) + ADAM_WD * w_ref[...])
    mo_ref[...] = mn
    vo_ref[...] = vn


def _adamw(name, w, g, m, v, rows):
    shape = w.shape

    def body(*refs):
        _adam_update(*refs)

    spec = pl.BlockSpec((rows,) + shape[1:], lambda i: (i,) + (0,) * (len(shape) - 1))
    return pl.pallas_call(
        body, name=name, grid=(shape[0] // rows,), in_specs=[spec] * 4, out_specs=[spec] * 3,
        out_shape=[jax.ShapeDtypeStruct(shape, F32)] * 3, compiler_params=_cp(("parallel",), 48),
    )(w, g, m, v)


def _adamw_small(ws, gs, ms, vs):
    n = len(ws)

    def body(*refs):
        for a in range(n):
            _adam_update(*[refs[q * n + a] for q in range(7)])

    outs = pl.pallas_call(
        body, name="adamw_small", out_shape=[jax.ShapeDtypeStruct(w.shape, F32) for w in ws] * 3,
    )(*ws, *gs, *ms, *vs)
    return outs[:n], outs[n:2 * n], outs[2 * n:]


ADAM_ROWS = dict(w_in_e=256, pool_w=4, w_out_e=256, w_in_o=128, w_out_o=256)


def _pack(parts, rows):
    flat = jnp.concatenate([p.reshape(-1).astype(F32) for p in parts])
    return jnp.pad(flat, (0, rows * 128 - flat.shape[0])).reshape(rows, 128)


def _unpack(buf, shapes):
    flat = buf.reshape(-1)
    out, off = [], 0
    for shp in shapes:
        n = 1
        for dim in shp:
            n *= dim
        out.append(flat[off:off + n].reshape(shp))
        off += n
    return out


WEIGHTS = ("even_norm", "even_w_in", "even_pool_w", "even_pool_scale", "even_ws", "even_bs", "even_w_out", "odd_norm",
           "odd_w_in", "odd_conv_w", "odd_w_out", "final_norm")
BIG_OF = dict(w_in_e="even_w_in", pool_w="even_pool_w", w_out_e="even_w_out", w_in_o="odd_w_in", w_out_o="odd_w_out")
SMALL = ("even_norm", "even_pool_scale", "even_ws", "even_bs", "final_norm", "odd_norm", "odd_conv_w")
SMALL_GRAD_ROWS = 576


def kernel(x, even_norm, even_w_in, even_pool_w, even_pool_scale, even_ws, even_bs, even_w_out, odd_norm, odd_w_in, odd_conv_w, odd_w_out, final_norm, loss_target, m_even_norm, m_even_w_in, m_even_pool_w, m_even_pool_scale, m_even_ws, m_even_bs, m_even_w_out, m_odd_norm, m_odd_w_in, m_odd_conv_w, m_odd_w_out, m_final_norm, v_even_norm, v_even_w_in, v_even_pool_w, v_even_pool_scale, v_even_ws, v_even_bs, v_even_w_out, v_odd_norm, v_odd_w_in, v_odd_conv_w, v_odd_w_out, v_final_norm):
    wv = dict(zip(WEIGHTS, (even_norm, even_w_in, even_pool_w, even_pool_scale, even_ws, even_bs, even_w_out, odd_norm,
                            odd_w_in, odd_conv_w, odd_w_out, final_norm)))
    mv = dict(zip(WEIGHTS, (m_even_norm, m_even_w_in, m_even_pool_w, m_even_pool_scale, m_even_ws, m_even_bs,
                            m_even_w_out, m_odd_norm, m_odd_w_in, m_odd_conv_w, m_odd_w_out, m_final_norm)))
    vv = dict(zip(WEIGHTS, (v_even_norm, v_even_w_in, v_even_pool_w, v_even_pool_scale, v_even_ws, v_even_bs,
                            v_even_w_out, v_odd_norm, v_odd_w_in, v_odd_conv_w, v_odd_w_out, v_final_norm)))
    c = lax.axis_index("c")
    k_me = 2 * lax.axis_index("x") + lax.axis_index("y")

    c_arr = jnp.reshape(c, (1,)).astype(jnp.int32)
    ck_arr = jnp.stack([c, k_me]).astype(jnp.int32)
    even_bigs, odd_bigs = BIGS[:3], BIGS[3:]

    shards = {b.name: wv[BIG_OF[b.name]][0].astype(BF16) for b in BIGS}
    tiny = jnp.concatenate([odd_conv_w[0], odd_norm], axis=0)
    *pieces_even, tiny_all = _gather_weights(even_bigs, [shards[b.name] for b in even_bigs], tiny)
    tiny_full = jnp.transpose(tiny_all, (1, 0, 2)).reshape(4, D_MODEL)
    to_chips, swap_odd, partials_odd = _copies_to_chips(odd_bigs), _copies_swap_halves(odd_bigs), _copies_partials(odd_bigs)
    gather_state, gather_token = _exchange_start(
        "gather_odd_start", [shards[b.name] for b in odd_bigs],
        [jax.ShapeDtypeStruct(_piece_shape(b), BF16) for b in odd_bigs], to_chips, 3 * len(odd_bigs), after=pieces_even[-1])
    k_arr = jnp.reshape(k_me, (1,)).astype(jnp.int32)
    w = {b.name: _assemble(b, p, shards[b.name], k_arr) for b, p in zip(even_bigs, pieces_even)}
    w.update(even_norm=even_norm, pool_scale=even_pool_scale, ws=even_ws[0], bs=even_bs[0],
             final_norm=final_norm.reshape(1, D_MODEL), conv_w=tiny_full[:3], odd_norm=tiny_full[3:4])

    class Hooks(_Hooks):
        def before_even(self):
            return gather_token

        def odd_weights(self, w, x1):
            srcs, lands = _exchange_wait("gather_odd_wait", gather_state, to_chips, after=x1)
            pieces = _finish_gather(odd_bigs, lands)
            return dict(w, **{b.name: _assemble(b, p, s, k_arr) for b, p, s in zip(odd_bigs, pieces, srcs)})

        def odd_grads_ready(self, g_w_in_o, g_w_out_o):
            self.swap, token = _exchange_start("rs_odd_swap_start", [g_w_in_o, g_w_out_o], _half_shapes(odd_bigs),
                                               swap_odd, len(odd_bigs))
            return token

        def even_mix_done(self, dz_e):
            grads, got = _exchange_wait("rs_odd_swap_wait", self.swap, swap_odd, after=dz_e)
            sums = [_chip_sum(b, g, r, c_arr) for b, g, r in zip(odd_bigs, grads, got)]
            self.partials, token = _exchange_start("rs_odd_partials_start", sums, _partial_shapes(odd_bigs), partials_odd,
                                                   3 * len(odd_bigs))
            return token

        def backward_done(self, dx0):
            self.sums, self.parts = _exchange_wait("rs_odd_partials_wait", self.partials, partials_odd, after=dx0)

    hooks = Hooks()
    loss, dx, g = _local_step(x[0], loss_target[0], w, hooks)

    got = _exchange("rs_even_swap", [g[b.name] for b in even_bigs], _half_shapes(even_bigs), _copies_swap_halves(even_bigs),
                    len(even_bigs))
    sums = [_chip_sum(b, g[b.name], r, c_arr) for b, r in zip(even_bigs, got)]
    partials_even = _copies_partials(even_bigs)
    even_state, _ = _exchange_start("rs_even_partials_start", sums, _partial_shapes(even_bigs), partials_even,
                                    3 * len(even_bigs))

    delta, new_m, new_v, g_shard = {}, {}, {}, {}

    def finish(bigs, sums_, parts_, name):
        halves = [_shard_sum(b, sm, p, ck_arr) for b, sm, p in zip(bigs, sums_, parts_)]
        for b, gs in zip(bigs, _share_halves(name, bigs, halves)):
            n = BIG_OF[b.name]
            g_shard[b.name] = gs
            d_, m_, v_ = _adamw(f"adamw_{b.name}", wv[n][0], gs, mv[n][0], vv[n][0], ADAM_ROWS[b.name])
            delta[n], new_m[n], new_v[n] = d_[None], m_[None], v_[None]

    finish(odd_bigs, hooks.sums, hooks.parts, "rs_share_halves_odd")
    sums, parts = _exchange_wait("rs_even_partials_wait", even_state, partials_even, after=new_v["odd_w_in"])
    finish(even_bigs, sums, parts, "rs_share_halves_even")

    small_g = _pack([g["even_norm"], g["pool_scale"], g["ws"], g["bs"], g["final_norm"], g["odd_norm"], g["conv_w"], loss],
                    SMALL_GRAD_ROWS)
    small_g = _sum_small(_gather_small(small_g).reshape(8, SMALL_GRAD_ROWS, 128))
    g_en, g_ps, g_ws, g_bs, g_fn, g_on, g_cw, loss = _unpack(
        small_g, [(1, D_MODEL), (1, D_MODEL), (1, 4, CHUNK, CHUNK), (1, 4, CHUNK), (D_MODEL,), (1, D_MODEL), (1, 3, D_MODEL), ()])
    g_on = lax.dynamic_slice(g_on, (0, k_me * 256), (1, 256))
    g_cw = lax.dynamic_slice(g_cw, (0, 0, k_me * 256), (1, 3, 256))
    grad = dict(even_norm=g_en, even_pool_scale=g_ps, even_ws=g_ws, even_bs=g_bs, final_norm=g_fn, odd_norm=g_on,
                odd_conv_w=g_cw)
    for b in BIGS:
        grad[BIG_OF[b.name]] = g_shard[b.name][None]

    flat = [(wv[n].size // wv[n].shape[-1], wv[n].shape[-1]) for n in SMALL]
    outs = _adamw_small(*[[src[n].reshape(shp) for n, shp in zip(SMALL, flat)] for src in (wv, grad, mv, vv)])
    for dst, arrs in zip((delta, new_m, new_v), outs):
        for n, arr in zip(SMALL, arrs):
            dst[n] = arr.reshape(wv[n].shape)

    return (loss, dx[None], *[grad[n] for n in WEIGHTS], *[delta[n] for n in WEIGHTS], *[new_m[n] for n in WEIGHTS],
            *[new_v[n] for n in WEIGHTS])
```

```python
from typing import NamedTuple

import jax
import jax.numpy as jnp
from jax import lax
from jax.experimental import pallas as pl
from jax.experimental.pallas import tpu as pltpu

F32, BF16 = jnp.float32, jnp.bfloat16

D_MODEL = 1024
EPS = 1e-6
NEG = -1e30
POOL_SIZES = (2, 4, 8, 16)
GROUP_W = 256
CHUNK = 128
DILATIONS = (1, 4, 16)
N_SLOTS = 8
HEAD_DIM = 128
ATTN_BLOCK = 128
SCALE = HEAD_DIM ** -0.5
EVEN_IN = 5120
ODD_IN = 14336
QKV_BLOCKS = 9
ODD_BLOCKS = ODD_IN // D_MODEL
SLOPES = tuple(tuple(2.0 ** (-8.0 * (g * N_SLOTS + s + 1) / (3 * N_SLOTS)) for s in range(N_SLOTS)) for g in range(3))

ADAM_LR, ADAM_B1, ADAM_B2, ADAM_EPS, ADAM_WD, ADAM_STEP = 0.001, 0.9, 0.999, 1e-08, 0.01, 10

HALO = 16
TS = 512
TM = 256
TMO = 256
MM_ROWS = 1024
IN_ROWS = 2048
DW_IN_TOKENS = 2048
DW_OUT_TOKENS = 1024
MIB = 1 << 20
MESH = pl.DeviceIdType.MESH
ANY = pl.BlockSpec(memory_space=pl.ANY)


def _cp(sem, vmem_mib):
    return pltpu.CompilerParams(dimension_semantics=sem, vmem_limit_bytes=vmem_mib * MIB)


def _sig(x):
    return 0.5 * jnp.tanh(0.5 * x) + 0.5


def _win_sum(e, w, forward):
    n = e.shape[0]
    k = 1
    while k < w:
        e = e + pltpu.roll(e, (n - k) if forward else k, 0)
        k *= 2
    return e


def _mm_nn(name, a, b, tm, tn, out_dtype, resid=None, col_map=None, n_cols=None, into=None):
    m, k = a.shape
    n = b.shape[1]
    if col_map is None:
        col_map, n_cols = (lambda j: j), n // tn

    def body(*refs):
        a_ref, b_ref = refs[0], refs[1]
        acc = jnp.dot(a_ref[...].astype(BF16), b_ref[...], preferred_element_type=F32)
        if resid is not None:
            acc = acc + refs[2][...]
        o_ref = refs[-1]
        o_ref[...] = acc.astype(out_dtype)

    in_specs = [pl.BlockSpec((tm, k), lambda j, i: (i, 0)), pl.BlockSpec((k, tn), lambda j, i: (0, col_map(j)))]
    args = [a, b]
    if resid is not None:
        in_specs.append(pl.BlockSpec((tm, tn), lambda j, i: (i, col_map(j))))
        args.append(resid)
    aliases = {}
    if into is not None:
        aliases = {len(args): 0}
        in_specs.append(ANY)
        args.append(into)
    return pl.pallas_call(
        body, name=name, grid=(n_cols, m // tm), in_specs=in_specs,
        out_specs=pl.BlockSpec((tm, tn), lambda j, i: (i, col_map(j))),
        out_shape=jax.ShapeDtypeStruct((m, n), out_dtype), input_output_aliases=aliases,
        compiler_params=_cp(("parallel", "parallel"), 48),
    )(*args)


def _mm_nt(name, a, b, tm, tk, out_dtype, k_map=None, nk=None):
    m, k = a.shape
    n = b.shape[0]
    if k_map is None:
        k_map, nk = (lambda kk: kk), k // tk

    def body(a_ref, b_ref, o_ref, acc_ref):
        kk = pl.program_id(1)
        p = lax.dot_general(a_ref[...].astype(BF16), b_ref[...], (((1,), (1,)), ((), ())), preferred_element_type=F32)
        if nk == 1:
            o_ref[...] = p.astype(out_dtype)
        else:
            @pl.when(kk == 0)
            def _():
                acc_ref[...] = p

            @pl.when(kk > 0)
            def _():
                acc_ref[...] += p

            @pl.when(kk == nk - 1)
            def _():
                o_ref[...] = acc_ref[...].astype(out_dtype)

    return pl.pallas_call(
        body, name=name, grid=(m // tm, nk),
        in_specs=[pl.BlockSpec((tm, tk), lambda i, kk: (i, k_map(kk))), pl.BlockSpec((n, tk), lambda i, kk: (0, k_map(kk)))],
        out_specs=pl.BlockSpec((tm, n), lambda i, kk: (i, 0)),
        out_shape=jax.ShapeDtypeStruct((m, n), out_dtype),
        scratch_shapes=[pltpu.VMEM((tm, n) if nk > 1 else (8, 128), F32)],
        compiler_params=_cp(("parallel", "arbitrary"), 56),
    )(a, b)


def _mm_tn(name, a, g, tn, ts, col_map=None, n_cols=None, into=None, after=None):
    s, ka = a.shape
    n = g.shape[1]
    if col_map is None:
        col_map, n_cols = (lambda j: j), n // tn

    def body(a_ref, g_ref, *rest):
        o_ref = rest[-1]
        st = pl.program_id(1)
        p = lax.dot_general(a_ref[...], g_ref[...].astype(BF16), (((0,), (0,)), ((), ())), preferred_element_type=F32)

        @pl.when(st == 0)
        def _():
            o_ref[...] = p

        @pl.when(st > 0)
        def _():
            o_ref[...] += p

    in_specs = [pl.BlockSpec((ts, ka), lambda j, st: (st, 0)), pl.BlockSpec((ts, tn), lambda j, st: (st, col_map(j)))]
    args = [a, g]
    aliases = {}
    if into is not None:
        aliases = {2: 0}
        in_specs.append(ANY)
        args.append(into)
    if after is not None:
        in_specs.append(ANY)
        args.append(after)
    return pl.pallas_call(
        body, name=name, grid=(n_cols, s // ts), in_specs=in_specs,
        out_specs=pl.BlockSpec((ka, tn), lambda j, st: (0, col_map(j))),
        out_shape=jax.ShapeDtypeStruct((ka, n), F32), input_output_aliases=aliases,
        compiler_params=_cp(("parallel", "arbitrary"), 56),
    )(*args)


def _rms_fwd(name, x, g, after=None):
    s = x.shape[0]

    def body(x_ref, g_ref, *rest):
        xf = x_ref[...]
        r = lax.rsqrt(jnp.mean(xf * xf, axis=-1, keepdims=True) + EPS)
        rest[-1][...] = (xf * r * g_ref[...]).astype(BF16)

    row = pl.BlockSpec((TS, D_MODEL), lambda i: (i, 0))
    in_specs, args = [row, pl.BlockSpec((1, D_MODEL), lambda i: (0, 0))], [x, g]
    if after is not None:
        in_specs.append(ANY)
        args.append(after)
    return pl.pallas_call(
        body, name=name, grid=(s // TS,), in_specs=in_specs, out_specs=row,
        out_shape=jax.ShapeDtypeStruct((s, D_MODEL), BF16), compiler_params=_cp(("parallel",), 32),
    )(*args)


def _class_major(a, d):
    return a.reshape(d, a.shape[0] // d, a.shape[1])


def _class_spec(d, tile, width):
    return pl.BlockSpec((d, tile // d, width), lambda i: (0, i, 0))


LANES = 128


def _token_scratch(tile, width):
    return pltpu.VMEM((width // LANES, tile, LANES), F32)


def _put(scr, val):
    for c in range(scr.shape[0]):
        scr[c] = val[:, c * LANES:(c + 1) * LANES]


def _get(scr):
    return jnp.concatenate([scr[c] for c in range(scr.shape[0])], axis=1)


def _to_classes(ref3, scr, d, dtype):
    n = ref3.shape[1]
    for c in range(scr.shape[0]):
        for r in range(d):
            ref3[r, :, c * LANES:(c + 1) * LANES] = scr.at[c][pl.ds(r, n, stride=d), :].astype(dtype)


def _from_classes(scr, ref3, d):
    n = ref3.shape[1]
    for c in range(scr.shape[0]):
        for r in range(d):
            scr.at[c][pl.ds(r, n, stride=d), :] = ref3[r, :, c * LANES:(c + 1) * LANES].astype(F32)


def _rms_fwd_orders(name, x, g):
    s = x.shape[0]

    def body(x_ref, g_ref, h_ref, h4_ref, h16_ref, scr):
        xf = x_ref[...]
        r = lax.rsqrt(jnp.mean(xf * xf, axis=-1, keepdims=True) + EPS)
        h = xf * r * g_ref[...]
        h_ref[...] = h.astype(BF16)
        _put(scr, h)
        _to_classes(h4_ref, scr, 4, BF16)
        _to_classes(h16_ref, scr, 16, BF16)

    row = pl.BlockSpec((TS, D_MODEL), lambda i: (i, 0))
    h, h4, h16 = pl.pallas_call(
        body, name=name, grid=(s // TS,), in_specs=[row, pl.BlockSpec((1, D_MODEL), lambda i: (0, 0))],
        out_specs=[row, _class_spec(4, TS, D_MODEL), _class_spec(16, TS, D_MODEL)],
        out_shape=[jax.ShapeDtypeStruct((s, D_MODEL), BF16), jax.ShapeDtypeStruct((4, s // 4, D_MODEL), BF16),
                   jax.ShapeDtypeStruct((16, s // 16, D_MODEL), BF16)],
        scratch_shapes=[_token_scratch(TS, D_MODEL)],
        compiler_params=_cp(("parallel",), 32),
    )(x, g)
    return h, h4.reshape(s, D_MODEL), h16.reshape(s, D_MODEL)


def _rms_bwd(name, dh, x, g, dres, dh4=None, dh16=None):
    s = x.shape[0]
    extra = dh4 is not None

    def body(dh_ref, x_ref, g_ref, dres_ref, *rest):
        if extra:
            dh4_ref, dh16_ref, dx_ref, dg_ref, scr = rest
        else:
            dx_ref, dg_ref = rest
        xf = x_ref[...]
        r = lax.rsqrt(jnp.mean(xf * xf, axis=-1, keepdims=True) + EPS)
        xh = xf * r
        dhf = dh_ref[...].astype(F32)
        if extra:
            _from_classes(scr, dh4_ref, 4)
            dhf = dhf + _get(scr)
            _from_classes(scr, dh16_ref, 16)
            dhf = dhf + _get(scr)
        dxh = dhf * g_ref[...]
        dx_ref[...] = dres_ref[...] + r * (dxh - xh * jnp.mean(dxh * xh, axis=-1, keepdims=True))
        part = jnp.sum(dhf * xh, axis=0, keepdims=True)

        @pl.when(pl.program_id(0) == 0)
        def _():
            dg_ref[...] = part

        @pl.when(pl.program_id(0) > 0)
        def _():
            dg_ref[...] += part

    row = pl.BlockSpec((TS, D_MODEL), lambda i: (i, 0))
    vec = pl.BlockSpec((1, D_MODEL), lambda i: (0, 0))
    in_specs, args, scratch = [row, row, vec, row], [dh, x, g, dres], []
    if extra:
        in_specs += [_class_spec(4, TS, D_MODEL), _class_spec(16, TS, D_MODEL)]
        args += [_class_major(dh4, 4), _class_major(dh16, 16)]
        scratch = [_token_scratch(TS, D_MODEL)]
    return pl.pallas_call(
        body, name=name, grid=(s // TS,), in_specs=in_specs, out_specs=[row, vec],
        out_shape=[jax.ShapeDtypeStruct((s, D_MODEL), F32), jax.ShapeDtypeStruct((1, D_MODEL), F32)],
        scratch_shapes=scratch, compiler_params=_cp(("arbitrary",), 40),
    )(*args)


def _out_proj_loss(y, w_out, resid, g, target):
    s, k = y.shape

    def body(y_ref, w_ref, r_ref, g_ref, t_ref, dx_ref, loss_ref, dg_ref):
        xf = jnp.dot(y_ref[...], w_ref[...], preferred_element_type=F32) + r_ref[...]
        gg = g_ref[...]
        r = lax.rsqrt(jnp.mean(xf * xf, axis=-1, keepdims=True) + EPS)
        xh = xf * r
        e = xh * gg - t_ref[...]
        dy = e * (1.0 / D_MODEL)
        dxh = dy * gg
        dx_ref[...] = r * (dxh - xh * jnp.mean(dxh * xh, axis=-1, keepdims=True))
        lpart = 0.5 * jnp.sum(jnp.mean(e * e, axis=-1, keepdims=True), axis=0, keepdims=True)
        lpart = jnp.broadcast_to(lpart, (8, 128))
        gpart = jnp.sum(dy * xh, axis=0, keepdims=True)

        @pl.when(pl.program_id(0) == 0)
        def _():
            loss_ref[...] = lpart
            dg_ref[...] = gpart

        @pl.when(pl.program_id(0) > 0)
        def _():
            loss_ref[...] += lpart
            dg_ref[...] += gpart

    row = pl.BlockSpec((TS, D_MODEL), lambda i: (i, 0))
    vec = pl.BlockSpec((1, D_MODEL), lambda i: (0, 0))
    return pl.pallas_call(
        body, name="odd_out_proj_loss", grid=(s // TS,),
        in_specs=[pl.BlockSpec((TS, k), lambda i: (i, 0)), pl.BlockSpec((k, D_MODEL), lambda i: (0, 0)), row, vec, row],
        out_specs=[row, pl.BlockSpec((8, 128), lambda i: (0, 0)), vec],
        out_shape=[jax.ShapeDtypeStruct((s, D_MODEL), F32), jax.ShapeDtypeStruct((8, 128), F32),
                   jax.ShapeDtypeStruct((1, D_MODEL), F32)],
        compiler_params=_cp(("arbitrary",), 48),
    )(y, w_out, resid, g, target)


def _zcol(c, tm=TM):
    return pl.BlockSpec((tm, D_MODEL), lambda i, c=c: (i, c))


def _prev_halo(c, tm=TM):
    return pl.BlockSpec((HALO, D_MODEL), lambda i, c=c: (jnp.maximum(i * (tm // HALO) - 1, 0), c))


def _next_halo(c, n_rows, tm=TM):
    last = n_rows // HALO - 1
    return pl.BlockSpec((HALO, D_MODEL), lambda i, c=c: (jnp.minimum((i + 1) * (tm // HALO), last), c))


def _full(shape):
    return pl.BlockSpec(shape, lambda i: (0,) * len(shape))


def _inv_count(first_row, n, w):
    t = first_row + lax.broadcasted_iota(jnp.int32, (n, 1), 0)
    return 1.0 / jnp.minimum(t + 1, w).astype(F32)


def _even_mix_fwd(z, pw, ps, wt, bs):
    s = z.shape[0]

    def body(a_ref, ga_ref, u_ref, v_ref, gb_ref, halo_ref, pw_ref, ps_ref, wt_ref, bs_ref, y_ref):
        i = pl.program_id(0)
        a = a_ref[...].astype(F32)
        halo = jnp.where(i > 0, halo_ref[...].astype(F32), 0.0)
        ext = jnp.concatenate([halo, a], axis=0)
        ga = ga_ref[...].astype(F32)
        sga = ga * _sig(ga)
        for g, w in enumerate(POOL_SIZES):
            cs = slice(g * GROUP_W, (g + 1) * GROUP_W)
            win = _win_sum(ext[:, cs], w, False)[HALO:]
            pooled = win * _inv_count(i * TM, TM, w) - a[:, cs]
            mixed = jnp.dot(pooled.astype(BF16), pw_ref[g], preferred_element_type=F32)
            y_ref[:, cs] = (mixed * ps_ref[:, cs] * sga[:, cs]).astype(BF16)
        gb = gb_ref[...].astype(F32)
        gate = u_ref[...].astype(F32) * (gb * _sig(gb))
        for ch in range(TM // CHUNK):
            rs = slice(ch * CHUNK, (ch + 1) * CHUNK)
            for g in range(4):
                cs = slice(g * GROUP_W, (g + 1) * GROUP_W)
                mixb = jnp.dot(wt_ref[g], v_ref[rs, cs], preferred_element_type=F32) + bs_ref[g]
                y_ref[rs, D_MODEL + g * GROUP_W:D_MODEL + (g + 1) * GROUP_W] = (gate[rs, cs] * mixb).astype(BF16)

    return pl.pallas_call(
        body, name="even_mix_fwd", grid=(s // TM,),
        in_specs=[_zcol(0), _zcol(1), _zcol(2), _zcol(3), _zcol(4), _prev_halo(0),
                  _full((4, GROUP_W, GROUP_W)), _full((1, D_MODEL)), _full((4, CHUNK, CHUNK)), _full((4, CHUNK, 1))],
        out_specs=pl.BlockSpec((TM, 2 * D_MODEL), lambda i: (i, 0)),
        out_shape=jax.ShapeDtypeStruct((s, 2 * D_MODEL), BF16),
        compiler_params=_cp(("parallel",), 48),
    )(z, z, z, z, z, z, pw, ps, wt, bs)


def _even_mix_bwd(dy, z, pw, ps, wt, wtt, bs, after=None):
    s = z.shape[0]
    n_tiles = s // TM
    tail_specs, tail_args = ([ANY], [after]) if after is not None else ([], [])

    def body(dy_ref, a_ref, ga_ref, u_ref, v_ref, gb_ref, halo_ref, dyn_ref, gan_ref, pw_ref, ps_ref, wt_ref, wtt_ref,
             bs_ref, *rest):
        dz_ref, dpw_ref, dps_ref, dws_ref, dbs_ref = rest[-5:]
        i = pl.program_id(0)

        @pl.when(i == 0)
        def _():
            dpw_ref[...] = jnp.zeros_like(dpw_ref)
            dps_ref[...] = jnp.zeros_like(dps_ref)
            dws_ref[...] = jnp.zeros_like(dws_ref)
            dbs_ref[...] = jnp.zeros_like(dbs_ref)

        a = a_ref[...].astype(F32)
        halo = jnp.where(i > 0, halo_ref[...].astype(F32), 0.0)
        ext = jnp.concatenate([halo, a], axis=0)
        ga = ga_ref[...].astype(F32)
        sg = _sig(ga)
        sga = ga * sg
        dsga = sg * (1.0 + ga * (1.0 - sg))
        dya = dy_ref[:, :D_MODEL].astype(F32)
        gan = gan_ref[...].astype(F32)
        dmn_all = jnp.where(i < n_tiles - 1, dyn_ref[...].astype(F32) * ps_ref[...] * (gan * _sig(gan)), 0.0)
        for g, w in enumerate(POOL_SIZES):
            cs = slice(g * GROUP_W, (g + 1) * GROUP_W)
            inv = _inv_count(i * TM, TM, w)
            pooled = _win_sum(ext[:, cs], w, False)[HALO:] * inv - a[:, cs]
            pb = pooled.astype(BF16)
            mixed = jnp.dot(pb, pw_ref[g], preferred_element_type=F32)
            dyg = dya[:, cs]
            psg = ps_ref[:, cs]
            dm = (dyg * psg * sga[:, cs]).astype(BF16)
            dz_ref[:, D_MODEL + g * GROUP_W:D_MODEL + (g + 1) * GROUP_W] = (dyg * mixed * psg * dsga[:, cs]).astype(BF16)
            dps_ref[:, cs] += jnp.sum(dyg * mixed * sga[:, cs], axis=0, keepdims=True)
            dpw_ref[g] += lax.dot_general(pb, dm, (((0,), (0,)), ((), ())), preferred_element_type=F32)
            nt = (((1,), (1,)), ((), ()))
            dpool = lax.dot_general(dm, pw_ref[g], nt, preferred_element_type=F32)
            dpool_n = lax.dot_general(dmn_all[:, cs].astype(BF16), pw_ref[g], nt, preferred_element_type=F32)
            e = jnp.concatenate([dpool * inv, dpool_n * _inv_count((i + 1) * TM, HALO, w)], axis=0)
            dz_ref[:, cs] = (_win_sum(e, w, True)[:TM] - dpool).astype(BF16)

        gb = gb_ref[...].astype(F32)
        sg = _sig(gb)
        sgb = gb * sg
        dsgb = sg * (1.0 + gb * (1.0 - sg))
        u = u_ref[...].astype(F32)
        dyb = dy_ref[:, D_MODEL:].astype(F32)
        tril = lax.broadcasted_iota(jnp.int32, (CHUNK, CHUNK), 0) >= lax.broadcasted_iota(jnp.int32, (CHUNK, CHUNK), 1)
        lane = lax.broadcasted_iota(jnp.int32, (CHUNK, 128), 1)
        for ch in range(TM // CHUNK):
            rs = slice(ch * CHUNK, (ch + 1) * CHUNK)
            for g in range(4):
                cs = slice(g * GROUP_W, (g + 1) * GROUP_W)
                vb = v_ref[rs, cs]
                mixb = jnp.dot(wt_ref[g], vb, preferred_element_type=F32) + bs_ref[g]
                dyu = dyb[rs, cs] * u[rs, cs]
                dmix = dyu * sgb[rs, cs]
                dmb = dmix.astype(BF16)
                o = g * GROUP_W
                dz_ref[rs, 2 * D_MODEL + o:2 * D_MODEL + o + GROUP_W] = (dyb[rs, cs] * mixb * sgb[rs, cs]).astype(BF16)
                dz_ref[rs, 3 * D_MODEL + o:3 * D_MODEL + o + GROUP_W] = jnp.dot(
                    wtt_ref[g], dmb, preferred_element_type=F32).astype(BF16)
                dz_ref[rs, 4 * D_MODEL + o:4 * D_MODEL + o + GROUP_W] = (dyu * mixb * dsgb[rs, cs]).astype(BF16)
                dws = lax.dot_general(dmb, vb, (((1,), (1,)), ((), ())), preferred_element_type=F32)
                dws_ref[g] += jnp.where(tril, dws, 0.0)
                dbs_ref[...] += jnp.where(lane == g, jnp.sum(dmix, axis=1, keepdims=True), 0.0)

    return pl.pallas_call(
        body, name="even_mix_bwd", grid=(n_tiles,),
        in_specs=[pl.BlockSpec((TM, 2 * D_MODEL), lambda i: (i, 0)), _zcol(0), _zcol(1), _zcol(2), _zcol(3), _zcol(4),
                  _prev_halo(0), _next_halo(0, s), _next_halo(1, s),
                  _full((4, GROUP_W, GROUP_W)), _full((1, D_MODEL)), _full((4, CHUNK, CHUNK)), _full((4, CHUNK, CHUNK)),
                  _full((4, CHUNK, 1))] + tail_specs,
        out_specs=[pl.BlockSpec((TM, EVEN_IN), lambda i: (i, 0)), _full((4, GROUP_W, GROUP_W)), _full((1, D_MODEL)),
                   _full((4, CHUNK, CHUNK)), _full((CHUNK, 128))],
        out_shape=[jax.ShapeDtypeStruct((s, EVEN_IN), BF16), jax.ShapeDtypeStruct((4, GROUP_W, GROUP_W), F32),
                   jax.ShapeDtypeStruct((1, D_MODEL), F32), jax.ShapeDtypeStruct((4, CHUNK, CHUNK), F32),
                   jax.ShapeDtypeStruct((CHUNK, 128), F32)],
        compiler_params=_cp(("arbitrary",), 56),
    )(dy, z, z, z, z, z, z, dy, z, pw, ps, wt, wtt, bs, *tail_args)


STAT_W = 128
Q_BLOCKS = 2
Q_ROWS = Q_BLOCKS * ATTN_BLOCK


def _band(d):
    row = lax.broadcasted_iota(jnp.int32, (ATTN_BLOCK, 2 * ATTN_BLOCK), 0)
    col = lax.broadcasted_iota(jnp.int32, (ATTN_BLOCK, 2 * ATTN_BLOCK), 1)
    steps = row + ATTN_BLOCK - col
    return (steps >= 0) & (steps <= ATTN_BLOCK), col >= ATTN_BLOCK, -(steps * d).astype(F32)


def _attn_fwd(z, gi):
    s = z.shape[0]
    d = DILATIONS[gi]
    nb = s // d // ATTN_BLOCK
    nq = nb // Q_BLOCKS

    def spec(which, prev=False):
        cb = which * 3 + gi
        if prev:
            return pl.BlockSpec((ATTN_BLOCK, D_MODEL), lambda r, i: (r * nb + jnp.maximum(Q_BLOCKS * i - 1, 0), cb))
        return pl.BlockSpec((Q_ROWS, D_MODEL), lambda r, i: (r * nq + i, cb))

    def body(q_ref, kp_ref, kc_ref, vp_ref, vc_ref, o_ref, lse_ref):
        i = pl.program_id(1)
        inner, own, negdist = _band(d)
        lane = lax.broadcasted_iota(jnp.int32, (ATTN_BLOCK, STAT_W), 1)
        for b in range(Q_BLOCKS):
            rows = slice(b * ATTN_BLOCK, (b + 1) * ATTN_BLOCK)
            valid = (inner & ((i > 0) | own)) if b == 0 else inner
            stat = jnp.zeros((ATTN_BLOCK, STAT_W), F32)
            for h in range(N_SLOTS):
                sl = slice(h * HEAD_DIM, (h + 1) * HEAD_DIM)
                if b == 0:
                    k = jnp.concatenate([kp_ref[:, sl], kc_ref[:ATTN_BLOCK, sl]], axis=0)
                    v = jnp.concatenate([vp_ref[:, sl], vc_ref[:ATTN_BLOCK, sl]], axis=0)
                else:
                    k = kc_ref[(b - 1) * ATTN_BLOCK:(b + 1) * ATTN_BLOCK, sl]
                    v = vc_ref[(b - 1) * ATTN_BLOCK:(b + 1) * ATTN_BLOCK, sl]
                sc = lax.dot_general(q_ref[rows, sl], k, (((1,), (1,)), ((), ())), preferred_element_type=F32) * SCALE
                sc = jnp.where(valid, sc + SLOPES[gi][h] * negdist, NEG)
                m = jnp.max(sc, axis=-1, keepdims=True)
                p = jnp.exp(sc - m)
                l = jnp.sum(p, axis=-1, keepdims=True)
                o = jnp.dot((p * (1.0 / l)).astype(BF16), v, preferred_element_type=F32)
                o_ref[rows, sl] = o.astype(BF16)
                stat = jnp.where(lane == h, m + jnp.log(l), stat)
            lse_ref[rows, :] = stat

    return pl.pallas_call(
        body, name=f"attn_fwd_d{d}", grid=(d, nq),
        in_specs=[spec(0), spec(1, True), spec(1), spec(2, True), spec(2)],
        out_specs=[pl.BlockSpec((Q_ROWS, D_MODEL), lambda r, i: (r * nq + i, 0)),
                   pl.BlockSpec((Q_ROWS, STAT_W), lambda r, i: (r * nq + i, 0))],
        out_shape=[jax.ShapeDtypeStruct((s, D_MODEL), BF16), jax.ShapeDtypeStruct((s, STAT_W), F32)],
        compiler_params=_cp(("parallel", "parallel"), 32),
    )(z, z, z, z, z)


def _attn_bwd(z, dyc, ltot, dst, dz, gi):
    s = z.shape[0]
    d = DILATIONS[gi]
    nb = s // d // ATTN_BLOCK
    nq = nb // Q_BLOCKS
    n_steps = d * nq

    def rev(cb, width=D_MODEL, prev=False):
        if prev:
            return pl.BlockSpec((ATTN_BLOCK, width), lambda r, n: (r * nb + jnp.maximum(Q_BLOCKS * (nq - 1 - n) - 1, 0), cb))
        return pl.BlockSpec((Q_ROWS, width), lambda r, n: (r * nq + nq - 1 - n, cb))

    def body(q_ref, kp_ref, kc_ref, vp_ref, vc_ref, dy_ref, l_ref, d_ref, dz_in, dz_out, dq_s, dk_s, dv_s, ck_s, cv_s, sems):
        del dz_in
        r = pl.program_id(0)
        n = pl.program_id(1)
        i = nq - 1 - n
        step = r * nq + n

        slot = lax.rem(step, 2)

        def out_copies(sl_):
            rows = pl.ds(pl.multiple_of((r * nq + i) * Q_ROWS, Q_ROWS), Q_ROWS)
            return [pltpu.make_async_copy(stage_ref.at[sl_], dz_out.at[rows, pl.ds((which * 3 + gi) * D_MODEL, D_MODEL)],
                                          sems.at[sl_, which])
                    for which, stage_ref in enumerate((dq_s, dk_s, dv_s))]

        @pl.when(step >= 2)
        def _():
            for cp in out_copies(slot):
                cp.wait()

        @pl.when(n == 0)
        def _():
            ck_s[...] = jnp.zeros_like(ck_s)
            cv_s[...] = jnp.zeros_like(cv_s)

        row = lax.broadcasted_iota(jnp.int32, (Q_ROWS, Q_ROWS + ATTN_BLOCK), 0)
        col = lax.broadcasted_iota(jnp.int32, (Q_ROWS, Q_ROWS + ATTN_BLOCK), 1)
        steps = row + ATTN_BLOCK - col
        valid = (steps >= 0) & (steps <= ATTN_BLOCK) & ((i > 0) | (col >= ATTN_BLOCK))
        negdist = -(steps * d).astype(F32)
        nt = (((1,), (1,)), ((), ()))
        tn = (((0,), (0,)), ((), ()))
        for h in range(N_SLOTS):
            sl = slice(h * HEAD_DIM, (h + 1) * HEAD_DIM)
            q = q_ref[:, sl]
            k = jnp.concatenate([kp_ref[:, sl], kc_ref[:, sl]], axis=0)
            v = jnp.concatenate([vp_ref[:, sl], vc_ref[:, sl]], axis=0)
            dy = dy_ref[:, sl]
            sc = lax.dot_general(q, k, nt, preferred_element_type=F32) * SCALE + SLOPES[gi][h] * negdist
            p = jnp.where(valid, jnp.exp(sc - l_ref[:, h:h + 1]), 0.0)
            dp = lax.dot_general(dy, v, nt, preferred_element_type=F32)
            ds = (p * (dp - d_ref[:, h:h + 1])).astype(BF16)
            dq_s[slot, :, sl] = (jnp.dot(ds, k, preferred_element_type=F32) * SCALE).astype(BF16)
            dk = lax.dot_general(ds, q, tn, preferred_element_type=F32) * SCALE
            dv = lax.dot_general(p.astype(BF16), dy, tn, preferred_element_type=F32)
            dk_s[slot, :Q_ROWS - ATTN_BLOCK, sl] = dk[ATTN_BLOCK:Q_ROWS].astype(BF16)
            dv_s[slot, :Q_ROWS - ATTN_BLOCK, sl] = dv[ATTN_BLOCK:Q_ROWS].astype(BF16)
            dk_s[slot, Q_ROWS - ATTN_BLOCK:, sl] = (ck_s[:, sl] + dk[Q_ROWS:]).astype(BF16)
            dv_s[slot, Q_ROWS - ATTN_BLOCK:, sl] = (cv_s[:, sl] + dv[Q_ROWS:]).astype(BF16)
            ck_s[:, sl] = dk[:ATTN_BLOCK]
            cv_s[:, sl] = dv[:ATTN_BLOCK]

        for cp in out_copies(slot):
            cp.start()

        @pl.when(step == n_steps - 1)
        def _():
            for cp in out_copies(1 - slot) + out_copies(slot):
                cp.wait()

    assert n_steps >= 2
    stage = pltpu.VMEM((2, Q_ROWS, D_MODEL), BF16)
    carry = pltpu.VMEM((ATTN_BLOCK, D_MODEL), F32)
    return pl.pallas_call(
        body, name=f"attn_bwd_d{d}", grid=(d, nq),
        in_specs=[rev(gi), rev(3 + gi, prev=True), rev(3 + gi), rev(6 + gi, prev=True), rev(6 + gi),
                  rev(0), rev(0, STAT_W), rev(0, STAT_W), ANY],
        out_specs=ANY,
        out_shape=jax.ShapeDtypeStruct((s, ODD_IN), BF16),
        scratch_shapes=[stage, stage, stage, carry, carry, pltpu.SemaphoreType.DMA((2, 3))],
        input_output_aliases={8: 0},
        compiler_params=_cp(("arbitrary", "arbitrary"), 32),
    )(z, z, z, z, z, dyc, ltot, dst, dz)


def _odd_mix_fwd(z, os_, lses, cw):
    s = z.shape[0]

    def body(o0, o1, o2, l0, l1, l2, gc_ref, db_ref, dc_ref, dx_ref, gd_ref, hc_ref, hx_ref, cw_ref, y_ref, yc_ref, lt_ref,
             lt4_ref, lt16_ref, scr_o, scr_o2, scr_l):
        i = pl.program_id(0)
        _from_classes(scr_l, l1, 4)
        lse1 = _get(scr_l)
        _from_classes(scr_l, l2, 16)
        ls = [l0[...], lse1, _get(scr_l)]
        lmax = jnp.maximum(jnp.maximum(ls[0], ls[1]), ls[2])
        es = [jnp.exp(l - lmax) for l in ls]
        den = es[0] + es[1] + es[2]
        alpha = [e / den for e in es]
        ltot = lmax + jnp.log(den)
        lt_ref[...] = ltot
        _put(scr_l, ltot)
        _to_classes(lt4_ref, scr_l, 4, F32)
        _to_classes(lt16_ref, scr_l, 16, F32)
        _from_classes(scr_o, o1, 4)
        _from_classes(scr_o2, o2, 16)
        for h in range(N_SLOTS):
            sl = slice(h * HEAD_DIM, (h + 1) * HEAD_DIM)
            yc = (alpha[0][:, h:h + 1] * o0[:, sl].astype(F32) + alpha[1][:, h:h + 1] * scr_o[h]
                  + alpha[2][:, h:h + 1] * scr_o2[h])
            yc_ref[:, sl] = yc.astype(BF16)
            gc = gc_ref[:, sl].astype(F32)
            y_ref[:, sl] = (yc * (gc * _sig(gc))).astype(BF16)
            zc = dc_ref[:, sl].astype(F32) * dx_ref[:, sl].astype(F32)
            halo = jnp.where(i > 0, hc_ref[:, sl].astype(F32) * hx_ref[:, sl].astype(F32), 0.0)
            ext = jnp.concatenate([halo, zc], axis=0)
            z1 = pltpu.roll(ext, 1, 0)[HALO:]
            z2 = pltpu.roll(ext, 2, 0)[HALO:]
            conv = cw_ref[0:1, sl] * z2 + cw_ref[1:2, sl] * z1 + cw_ref[2:3, sl] * zc
            gd = gd_ref[:, sl].astype(F32)
            y_ref[:, D_MODEL + h * HEAD_DIM:D_MODEL + (h + 1) * HEAD_DIM] = (
                db_ref[:, sl].astype(F32) * conv * (gd * _sig(gd))).astype(BF16)

    row = pl.BlockSpec((TMO, D_MODEL), lambda i: (i, 0))
    stat = pl.BlockSpec((TMO, STAT_W), lambda i: (i, 0))
    y, ycr, lt, lt4, lt16 = pl.pallas_call(
        body, name="odd_mix_fwd", grid=(s // TMO,),
        in_specs=[row, _class_spec(4, TMO, D_MODEL), _class_spec(16, TMO, D_MODEL),
                  stat, _class_spec(4, TMO, STAT_W), _class_spec(16, TMO, STAT_W),
                  _zcol(9, TMO), _zcol(10, TMO), _zcol(11, TMO), _zcol(12, TMO), _zcol(13, TMO), _prev_halo(11, TMO),
                  _prev_halo(12, TMO), _full((3, D_MODEL))],
        out_specs=[pl.BlockSpec((TMO, 2 * D_MODEL), lambda i: (i, 0)), row, stat, _class_spec(4, TMO, STAT_W),
                   _class_spec(16, TMO, STAT_W)],
        out_shape=[jax.ShapeDtypeStruct((s, 2 * D_MODEL), BF16), jax.ShapeDtypeStruct((s, D_MODEL), BF16),
                   jax.ShapeDtypeStruct((s, STAT_W), F32), jax.ShapeDtypeStruct((4, s // 4, STAT_W), F32),
                   jax.ShapeDtypeStruct((16, s // 16, STAT_W), F32)],
        scratch_shapes=[_token_scratch(TMO, D_MODEL), _token_scratch(TMO, D_MODEL), _token_scratch(TMO, STAT_W)],
        compiler_params=_cp(("parallel",), 48),
    )(os_[0], _class_major(os_[1], 4), _class_major(os_[2], 16), lses[0], _class_major(lses[1], 4),
      _class_major(lses[2], 16), z, z, z, z, z, z, z, cw)
    return y, ycr, [lt, lt4.reshape(s, STAT_W), lt16.reshape(s, STAT_W)]


def _odd_mix_bwd(dy, z, ycr, cw):
    s = z.shape[0]
    n_tiles = s // TMO
    rest = ODD_IN - QKV_BLOCKS * D_MODEL

    def body(dy_ref, yc_ref, gc_ref, db_ref, dc_ref, dx_ref, gd_ref, hc_ref, hx_ref, dyn_ref, dbn_ref, gdn_ref, cw_ref,
             dz_ref, dyc_ref, dyc4_ref, dyc16_ref, dd_ref, dd4_ref, dd16_ref, dcw_ref, stages, sems, scr_o, scr_l):
        i = pl.program_id(0)
        slot = lax.rem(i, 2)

        def out_copy(sl_):
            return pltpu.make_async_copy(
                stages.at[sl_], dz_ref.at[pl.ds(pl.multiple_of(i * TMO, TMO), TMO), pl.ds(QKV_BLOCKS * D_MODEL, rest)],
                sems.at[sl_])

        @pl.when(i >= 2)
        def _():
            out_copy(slot).wait()

        stage = stages.at[slot]

        @pl.when(i == 0)
        def _():
            dcw_ref[...] = jnp.zeros_like(dcw_ref)

        lane = lax.broadcasted_iota(jnp.int32, (TMO, STAT_W), 1)
        stat = jnp.zeros((TMO, STAT_W), F32)
        nrow = TMO + HALO
        for h in range(N_SLOTS):
            sl = slice(h * HEAD_DIM, (h + 1) * HEAD_DIM)
            sd = slice(D_MODEL + h * HEAD_DIM, D_MODEL + (h + 1) * HEAD_DIM)
            dyc_in = dy_ref[:, sl].astype(F32)
            gc = gc_ref[:, sl].astype(F32)
            sg = _sig(gc)
            yc = yc_ref[:, sl].astype(F32)
            dyc = dyc_in * (gc * sg)
            dyc_ref[:, sl] = dyc.astype(BF16)
            scr_o[h] = dyc
            stage[:, sl] = (dyc_in * yc * (sg * (1.0 + gc * (1.0 - sg)))).astype(BF16)
            stat = jnp.where(lane == h, jnp.sum(dyc * yc, axis=-1, keepdims=True), stat)
            dc = dc_ref[:, sl].astype(F32)
            dx = dx_ref[:, sl].astype(F32)
            zc = dc * dx
            halo = jnp.where(i > 0, hc_ref[:, sl].astype(F32) * hx_ref[:, sl].astype(F32), 0.0)
            ext = jnp.concatenate([halo, zc], axis=0)
            z1 = pltpu.roll(ext, 1, 0)[HALO:]
            z2 = pltpu.roll(ext, 2, 0)[HALO:]
            w0, w1, w2 = cw_ref[0:1, sl], cw_ref[1:2, sl], cw_ref[2:3, sl]
            conv = w0 * z2 + w1 * z1 + w2 * zc
            gd = gd_ref[:, sl].astype(F32)
            sg = _sig(gd)
            sgd = gd * sg
            db = db_ref[:, sl].astype(F32)
            dyd = dy_ref[:, sd].astype(F32)
            dconv = dyd * db * sgd
            gdn = gdn_ref[:, sl].astype(F32)
            dconv_n = jnp.where(i < n_tiles - 1,
                                dyn_ref[:, sl].astype(F32) * dbn_ref[:, sl].astype(F32) * (gdn * _sig(gdn)), 0.0)
            extn = jnp.concatenate([dconv, dconv_n], axis=0)
            dzc = w2 * dconv + w1 * pltpu.roll(extn, nrow - 1, 0)[:TMO] + w0 * pltpu.roll(extn, nrow - 2, 0)[:TMO]
            stage[:, sd] = (dyd * conv * sgd).astype(BF16)
            stage[:, 2 * D_MODEL + h * HEAD_DIM:2 * D_MODEL + (h + 1) * HEAD_DIM] = (dzc * dx).astype(BF16)
            stage[:, 3 * D_MODEL + h * HEAD_DIM:3 * D_MODEL + (h + 1) * HEAD_DIM] = (dzc * dc).astype(BF16)
            stage[:, 4 * D_MODEL + h * HEAD_DIM:4 * D_MODEL + (h + 1) * HEAD_DIM] = (
                dyd * db * conv * (sg * (1.0 + gd * (1.0 - sg)))).astype(BF16)
            for tap, shifted in enumerate((z2, z1, zc)):
                dcw_ref[tap:tap + 1, sl] += jnp.sum(dconv * shifted, axis=0, keepdims=True)
        _to_classes(dyc4_ref, scr_o, 4, BF16)
        _to_classes(dyc16_ref, scr_o, 16, BF16)
        dd_ref[...] = stat
        _put(scr_l, stat)
        _to_classes(dd4_ref, scr_l, 4, F32)
        _to_classes(dd16_ref, scr_l, 16, F32)

        out_copy(slot).start()

        @pl.when(i == n_tiles - 1)
        def _():
            out_copy(1 - slot).wait()
            out_copy(slot).wait()

    assert n_tiles >= 2
    row = pl.BlockSpec((TMO, D_MODEL), lambda i: (i, 0))
    stat = pl.BlockSpec((TMO, STAT_W), lambda i: (i, 0))
    dz, dyc, dyc4, dyc16, dd, dd4, dd16, g_conv = pl.pallas_call(
        body, name="odd_mix_bwd", grid=(n_tiles,),
        in_specs=[pl.BlockSpec((TMO, 2 * D_MODEL), lambda i: (i, 0)), row, _zcol(9, TMO), _zcol(10, TMO), _zcol(11, TMO),
                  _zcol(12, TMO), _zcol(13, TMO), _prev_halo(11, TMO), _prev_halo(12, TMO), _next_halo(1, s, TMO),
                  _next_halo(10, s, TMO), _next_halo(13, s, TMO), _full((3, D_MODEL))],
        out_specs=[ANY, row, _class_spec(4, TMO, D_MODEL), _class_spec(16, TMO, D_MODEL),
                   stat, _class_spec(4, TMO, STAT_W), _class_spec(16, TMO, STAT_W), _full((3, D_MODEL))],
        out_shape=[jax.ShapeDtypeStruct((s, ODD_IN), BF16), jax.ShapeDtypeStruct((s, D_MODEL), BF16),
                   jax.ShapeDtypeStruct((4, s // 4, D_MODEL), BF16), jax.ShapeDtypeStruct((16, s // 16, D_MODEL), BF16),
                   jax.ShapeDtypeStruct((s, STAT_W), F32), jax.ShapeDtypeStruct((4, s // 4, STAT_W), F32),
                   jax.ShapeDtypeStruct((16, s // 16, STAT_W), F32), jax.ShapeDtypeStruct((3, D_MODEL), F32)],
        scratch_shapes=[pltpu.VMEM((2, TMO, rest), BF16), pltpu.SemaphoreType.DMA((2,)), _token_scratch(TMO, D_MODEL),
                        _token_scratch(TMO, STAT_W)],
        compiler_params=_cp(("arbitrary",), 48),
    )(dy, ycr, z, z, z, z, z, z, z, dy, z, z, cw)
    dyc = [dyc, dyc4.reshape(s, D_MODEL), dyc16.reshape(s, D_MODEL)]
    dd = [dd, dd4.reshape(s, STAT_W), dd16.reshape(s, STAT_W)]
    return dz, dyc, dd, g_conv


def _cols_of_order(order):
    if order == 0:
        return (lambda j: jnp.where(j < 3, 3 * j, j + 6)), 8
    return (lambda j: 3 * j + order), 3


class _Hooks:
    def before_even(self):
        return None

    def odd_weights(self, w, x1):
        return w

    def odd_grads_ready(self, g_w_in_o, g_w_out_o):
        return None

    def even_mix_done(self, dz_e):
        return None

    def backward_done(self, dx0):
        return None


def _local_step(x, target, w, hooks=_Hooks()):
    tril = jnp.tril(jnp.ones((CHUNK, CHUNK), bool))
    wt = jnp.where(tril[None], w["ws"], 0.0).astype(BF16)
    wtt = jnp.swapaxes(wt, 1, 2)
    bs = w["bs"].reshape(4, CHUNK, 1)

    h_e = _rms_fwd("rms_fwd_even", x, w["even_norm"], after=hooks.before_even())
    z_e = _mm_nn("even_in_proj", h_e, w["w_in_e"], IN_ROWS, 1280, BF16)
    y_e = _even_mix_fwd(z_e, w["pool_w"], w["pool_scale"], wt, bs)
    x1 = _mm_nn("even_out_proj", y_e, w["w_out_e"], MM_ROWS, 1024, F32, resid=x)
    w = hooks.odd_weights(w, x1)
    h_o = _rms_fwd_orders("rms_fwd_odd", x1, w["odd_norm"])
    z_o = None
    for o in range(3):
        cols, n_cols = _cols_of_order(o)
        z_o = _mm_nn(f"odd_in_proj_o{o}", h_o[o], w["w_in_o"], IN_ROWS, D_MODEL, BF16, col_map=cols, n_cols=n_cols,
                     into=z_o)
    att = [_attn_fwd(z_o, gi) for gi in range(3)]
    y_o, ycr, ltot = _odd_mix_fwd(z_o, [a[0] for a in att], [a[1] for a in att], w["conv_w"])
    dx2, loss8, g_final = _out_proj_loss(y_o, w["w_out_o"], x1, w["final_norm"], target)

    g_w_out_o = _mm_tn("odd_out_proj_dw", y_o, dx2, 1024, DW_OUT_TOKENS)
    dy_o = _mm_nt("odd_out_proj_dy", dx2, w["w_out_o"], MM_ROWS, 1024, BF16)
    dz_o, dyc, dst, g_conv = _odd_mix_bwd(dy_o, z_o, ycr, w["conv_w"])
    for gi in range(3):
        dz_o = _attn_bwd(z_o, dyc[gi], ltot[gi], dst[gi], dz_o, gi)
    g_w_in_o, dh_o = None, []
    for o in range(3):
        cols, n_cols = _cols_of_order(o)
        g_w_in_o = _mm_tn(f"odd_in_proj_dw_o{o}", h_o[o], dz_o, D_MODEL, DW_IN_TOKENS, col_map=cols, n_cols=n_cols,
                          into=g_w_in_o)
        dh_o.append(_mm_nt(f"odd_in_proj_dh_o{o}", dz_o, w["w_in_o"], IN_ROWS, D_MODEL, BF16, k_map=cols, nk=n_cols))
    dx1, g_odd_norm = _rms_bwd("rms_bwd_odd", dh_o[0], x1, w["odd_norm"], dx2, dh4=dh_o[1], dh16=dh_o[2])
    after = hooks.odd_grads_ready(g_w_in_o, g_w_out_o)
    g_w_out_e = _mm_tn("even_out_proj_dw", y_e, dx1, 1024, DW_OUT_TOKENS, after=after)
    dy_e = _mm_nt("even_out_proj_dy", dx1, w["w_out_e"], MM_ROWS, 1024, BF16)
    dz_e, g_pw, g_ps, g_ws, g_bs = _even_mix_bwd(dy_e, z_e, w["pool_w"], w["pool_scale"], wt, wtt, bs)
    after = hooks.even_mix_done(dz_e)
    g_w_in_e = _mm_tn("even_in_proj_dw", h_e, dz_e, 1280, DW_IN_TOKENS, after=after)
    dh_e = _mm_nt("even_in_proj_dh", dz_e, w["w_in_e"], MM_ROWS, 2560, F32)
    dx0, g_even_norm = _rms_bwd("rms_bwd_even", dh_e, x, w["even_norm"], dx1)
    hooks.backward_done(dx0)

    grads = dict(w_in_e=g_w_in_e, pool_w=g_pw, w_out_e=g_w_out_e, w_in_o=g_w_in_o, w_out_o=g_w_out_o,
                 even_norm=g_even_norm, pool_scale=g_ps, ws=g_ws, bs=g_bs[:, :4].T, final_norm=g_final,
                 odd_norm=g_odd_norm, conv_w=g_conv)
    return loss8[0, 0], dx0, grads


class _Big(NamedTuple):
    name: str
    full: tuple
    haxis: int
    kaxis: int
    sub: int


BIGS = (
    _Big("w_in_e", (1024, 5120), 0, 1, 2),
    _Big("pool_w", (4, 256, 256), 0, 1, 1),
    _Big("w_out_e", (2048, 1024), 1, 0, 1),
    _Big("w_in_o", (1024, 14336), 0, 1, 4),
    _Big("w_out_o", (2048, 1024), 1, 0, 1),
)
N_BIG = len(BIGS)


def _shape(b, half=False, shard=False):
    return tuple(n // (2 if (half and ax == b.haxis) else 1) // (4 if (shard and ax == b.kaxis) else 1)
                 for ax, n in enumerate(b.full))


def _at(ref, b, h=None, k=None):
    idx = []
    for ax, n in enumerate(b.full):
        if ax == b.haxis and h is not None:
            idx.append(pl.ds(h * (n // 2), n // 2))
        elif ax == b.kaxis and k is not None:
            idx.append(pl.ds(k * (n // 4), n // 4))
        else:
            idx.append(slice(None))
    return ref.at[tuple(idx)]


def _place():
    x, y, c = lax.axis_index("x"), lax.axis_index("y"), lax.axis_index("c")
    chips = [(1 - x, y), (x, 1 - y), (1 - x, 1 - y)]
    return x, y, c, 2 * x + y, chips, [2 * cx + cy for cx, cy in chips]


def _piece_shape(b):
    return (4, 2) + _shape(b, half=True, shard=True)


def _gather_weights(bigs, shards, tiny):
    nb = len(bigs)

    def body(*refs):
        ins, tiny_in = refs[:nb], refs[nb]
        outs, tiny_out = refs[nb + 1:2 * nb + 1], refs[2 * nb + 1]
        send, recv, loc = refs[2 * nb + 2:]
        x, y, c, k_me, chips, ks = _place()
        sib = (x, y, 1 - c)

        def rc(src, dst, sem, to):
            return pltpu.make_async_remote_copy(src_ref=src, dst_ref=dst, send_sem=send.at[sem], recv_sem=recv.at[sem],
                                                device_id=to, device_id_type=MESH)

        own = pltpu.make_async_copy(tiny_in, tiny_out.at[k_me], loc)
        own.start()
        sends = []
        for j, chip in enumerate(chips):
            for a, b in enumerate(bigs):
                sends.append(rc(_at(ins[a], b, h=c), outs[a].at[k_me, c], 6 * a + j, (*chip, c)))
            sends.append(rc(tiny_in, tiny_out.at[k_me], 6 * nb + j, (*chip, c)))
        for cp in sends:
            cp.start()
        for j in range(3):
            for a in range(nb):
                piece = outs[a].at[ks[j], c]
                rc(piece, piece, 6 * a + j, sib).wait_recv()
                fwd = rc(piece, piece, 6 * a + 3 + j, sib)
                fwd.start()
                sends.append(fwd)
            rc(tiny_in, tiny_out.at[ks[j]], 6 * nb + j, sib).wait_recv()
        for j in range(3):
            for a in range(nb):
                piece = outs[a].at[ks[j], 1 - c]
                rc(piece, piece, 6 * a + 3 + j, sib).wait_recv()
        for cp in sends:
            cp.wait_send()
        own.wait()

    n_sem = 6 * nb + 3
    return pl.pallas_call(
        body, name="gather_even_weights",
        in_specs=[ANY] * (nb + 1), out_specs=[ANY] * (nb + 1),
        out_shape=[jax.ShapeDtypeStruct(_piece_shape(b), BF16) for b in bigs]
        + [jax.ShapeDtypeStruct((4,) + tiny.shape, F32)],
        scratch_shapes=[pltpu.SemaphoreType.DMA((n_sem,)), pltpu.SemaphoreType.DMA((n_sem,)), pltpu.SemaphoreType.DMA(())],
    )(*shards, tiny)


def _assemble(b, pieces, shard, k_arr):
    blk = _blk(b)

    def body(k_ref, p_ref, s_ref, o_ref):
        mine = pl.program_id(0) == k_ref[0]

        @pl.when(mine)
        def _():
            o_ref[...] = s_ref[...]

        @pl.when(jnp.logical_not(mine))
        def _():
            o_ref[...] = p_ref[...]

    return pl.pallas_call(
        body, name=f"assemble_{b.name}",
        grid_spec=pltpu.PrefetchScalarGridSpec(
            num_scalar_prefetch=1, grid=(4, 2, b.sub),
            in_specs=[pl.BlockSpec((None, None) + blk, lambda k, h, st, k_ref: (k, h) + _bidx(b, 0, 0, st)),
                      pl.BlockSpec(blk, lambda k, h, st, k_ref: _bidx(b, h, 0, st))],
            out_specs=pl.BlockSpec(blk, lambda k, h, st, k_ref: _bidx(b, h, k, st))),
        out_shape=jax.ShapeDtypeStruct(b.full, BF16),
        compiler_params=_cp(("arbitrary", "arbitrary", "arbitrary"), 32),
    )(k_arr, pieces, shard)


def _copies_to_chips(bigs):
    def copies(srcs, lands, send, recv, waiting=False):
        _, _, c, k_me, chips, _ = _place()
        return [pltpu.make_async_remote_copy(
                    src_ref=_at(srcs[a], b, h=c), dst_ref=lands[a].at[k_me, c], send_sem=send.at[3 * a + j],
                    recv_sem=recv.at[3 * a + j], device_id=(*chips[j], c), device_id_type=MESH)
                for j in range(3) for a, b in enumerate(bigs)]
    return copies


def _copies_swap_halves(bigs):
    def copies(srcs, lands, send, recv, waiting=False):
        x, y, c, _, _, _ = _place()
        return [pltpu.make_async_remote_copy(
                    src_ref=_at(srcs[a], b, h=1 - c), dst_ref=lands[a], send_sem=send.at[a], recv_sem=recv.at[a],
                    device_id=(x, y, 1 - c), device_id_type=MESH)
                for a, b in enumerate(bigs)]
    return copies


def _copies_partials(bigs):
    def copies(srcs, lands, send, recv, waiting=False):
        _, _, c, _, chips, ks = _place()
        return [pltpu.make_async_remote_copy(
                    src_ref=_at(srcs[a], b, k=ks[j]), dst_ref=lands[a].at[j], send_sem=send.at[3 * a + j],
                    recv_sem=recv.at[3 * a + j], device_id=(*chips[j], c), device_id_type=MESH)
                for j in range(3) for a, b in enumerate(bigs)]
    return copies


def _exchange(name, srcs, land_shapes, copies_of, n_copies):
    ns = len(srcs)

    def body(*refs):
        copies = copies_of(refs[:ns], refs[ns:ns + len(land_shapes)], refs[-2], refs[-1])
        for cp in copies:
            cp.start()
        for cp in copies:
            cp.wait()

    return pl.pallas_call(
        body, name=name, in_specs=[ANY] * ns, out_specs=[ANY] * len(land_shapes), out_shape=land_shapes,
        scratch_shapes=[pltpu.SemaphoreType.DMA((n_copies,)), pltpu.SemaphoreType.DMA((n_copies,))],
    )(*srcs)


HBM = pl.BlockSpec(memory_space=pltpu.HBM)
SEM = pl.BlockSpec(memory_space=pltpu.SEMAPHORE)
SIDE_EFFECT = pltpu.SideEffectType.DATAFLOW_SIDE_EFFECTING


def _in_hbm(a):
    return pltpu.with_memory_space_constraint(a, pltpu.HBM)


def _exchange_start(name, srcs, land_shapes, copies_of, n_copies, after=None):
    ns, nl = len(srcs), len(land_shapes)
    lands = [lax.empty(sh.shape, sh.dtype) for sh in land_shapes]
    tail = [] if after is None else [after]
    n_in = ns + nl + len(tail)

    def body(*refs):
        send, recv, token = refs[n_in], refs[n_in + 1], refs[-1]
        for cp in copies_of(refs[:ns], refs[ns:ns + nl], send, recv):
            cp.start()
        token[...] = jnp.zeros_like(token)

    thru = [pltpu.HBM(a.shape, a.dtype) for a in (*srcs, *lands)]
    send, recv, *bufs, token = pl.pallas_call(
        body, name=name,
        out_shape=(pltpu.SemaphoreType.DMA((n_copies,)), pltpu.SemaphoreType.DMA((n_copies,)), *thru,
                   jax.ShapeDtypeStruct((8, 128), F32)),
        in_specs=[HBM] * (ns + nl) + [ANY] * len(tail),
        out_specs=(SEM, SEM, *([HBM] * (ns + nl)), pl.BlockSpec(memory_space=pltpu.VMEM)),
        input_output_aliases={i: 2 + i for i in range(ns + nl)},
        compiler_params=pltpu.CompilerParams(has_side_effects=SIDE_EFFECT),
    )(*[_in_hbm(a) for a in (*srcs, *lands)], *tail)
    return (send, recv, bufs, ns), token


def _exchange_wait(name, state, copies_of, after):
    send, recv, bufs, ns = state
    n = len(bufs)

    def body(*refs):
        ins = refs[:n]
        for cp in copies_of(ins[:ns], ins[ns:], refs[n], refs[n + 1], waiting=True):
            cp.wait_send()
            cp.wait_recv()

    out = pl.pallas_call(
        body, name=name, out_shape=tuple(pltpu.HBM(a.shape, a.dtype) for a in bufs),
        in_specs=[HBM] * n + [SEM, SEM, ANY], out_specs=tuple([HBM] * n),
        input_output_aliases={i: i for i in range(n)},
        compiler_params=pltpu.CompilerParams(has_side_effects=SIDE_EFFECT),
    )(*bufs, send, recv, after)
    return list(out[:ns]), list(out[ns:])


def _finish_gather(bigs, pieces):
    nb = len(bigs)

    def body(*refs):
        outs, send, recv = refs[nb:2 * nb], refs[2 * nb], refs[2 * nb + 1]
        x, y, c, _, _, ks = _place()
        fwd = [pltpu.make_async_remote_copy(
                   src_ref=outs[a].at[ks[j], c], dst_ref=outs[a].at[ks[j], c], send_sem=send.at[3 * a + j],
                   recv_sem=recv.at[3 * a + j], device_id=(x, y, 1 - c), device_id_type=MESH)
               for j in range(3) for a in range(nb)]
        for cp in fwd:
            cp.start()
        for cp in fwd:
            cp.wait()

    return pl.pallas_call(
        body, name="gather_odd_finish", in_specs=[ANY] * nb, out_specs=[ANY] * nb,
        out_shape=[jax.ShapeDtypeStruct(_piece_shape(b), BF16) for b in bigs],
        scratch_shapes=[pltpu.SemaphoreType.DMA((3 * nb,)), pltpu.SemaphoreType.DMA((3 * nb,))],
        input_output_aliases={a: a for a in range(nb)},
    )(*pieces)


def _blk(b):
    win = _shape(b, half=True, shard=True)
    return (win[0] // b.sub,) + win[1:]


def _bidx(b, h, k, st):
    idx = [0] * len(b.full)
    idx[b.haxis] = h
    idx[b.kaxis] = k
    idx[0] = idx[0] * b.sub + st
    return tuple(idx)


def _chip_sum(b, g, got, c_arr):
    blk = _blk(b)

    def body(c_ref, g_ref, r_ref, o_ref):
        del c_ref
        o_ref[...] = (g_ref[...] + r_ref[...]).astype(BF16)

    half = pl.BlockSpec(blk, lambda k, st, c_ref: _bidx(b, 0, k, st))
    return pl.pallas_call(
        body, name=f"rs_chip_sum_{b.name}",
        grid_spec=pltpu.PrefetchScalarGridSpec(
            num_scalar_prefetch=1, grid=(4, b.sub),
            in_specs=[pl.BlockSpec(blk, lambda k, st, c_ref: _bidx(b, c_ref[0], k, st)), half], out_specs=half),
        out_shape=jax.ShapeDtypeStruct(_shape(b, half=True), BF16),
        compiler_params=_cp(("arbitrary", "arbitrary"), 40),
    )(c_arr, g, got)


def _half_shapes(bigs):
    return [jax.ShapeDtypeStruct(_shape(b, half=True), F32) for b in bigs]


def _partial_shapes(bigs):
    return [jax.ShapeDtypeStruct((3,) + _shape(b, half=True, shard=True), BF16) for b in bigs]


def _shard_sum(b, mine, got, ck_arr):
    blk = _blk(b)

    def body(ck_ref, m_ref, r0, r1, r2, o_ref):
        del ck_ref
        o_ref[...] = (m_ref[...].astype(F32) + r0[...].astype(F32)) + (r1[...].astype(F32) + r2[...].astype(F32))

    def peer(j):
        return pl.BlockSpec((None,) + blk, lambda st, ck: (j,) + _bidx(b, 0, 0, st))

    return pl.pallas_call(
        body, name=f"rs_shard_sum_{b.name}",
        grid_spec=pltpu.PrefetchScalarGridSpec(
            num_scalar_prefetch=1, grid=(b.sub,),
            in_specs=[pl.BlockSpec(blk, lambda st, ck: _bidx(b, 0, ck[1], st)), peer(0), peer(1), peer(2)],
            out_specs=pl.BlockSpec(blk, lambda st, ck: _bidx(b, ck[0], 0, st))),
        out_shape=jax.ShapeDtypeStruct(_shape(b, shard=True), F32),
        compiler_params=_cp(("arbitrary",), 40),
    )(ck_arr, mine, got, got, got)


def _share_halves(name, bigs, gs):
    nb = len(bigs)

    def body(*refs):
        outs, send, recv = refs[nb:2 * nb], refs[2 * nb], refs[2 * nb + 1]
        x, y, c, _, _, _ = _place()
        copies = [pltpu.make_async_remote_copy(src_ref=_at(outs[a], b, h=c), dst_ref=_at(outs[a], b, h=c),
                                               send_sem=send.at[a], recv_sem=recv.at[a], device_id=(x, y, 1 - c),
                                               device_id_type=MESH)
                  for a, b in enumerate(bigs)]
        for cp in copies:
            cp.start()
        for cp in copies:
            cp.wait()

    return pl.pallas_call(
        body, name=name, in_specs=[ANY] * nb, out_specs=[ANY] * nb,
        out_shape=[jax.ShapeDtypeStruct(_shape(b, shard=True), F32) for b in bigs],
        scratch_shapes=[pltpu.SemaphoreType.DMA((nb,)), pltpu.SemaphoreType.DMA((nb,))],
        input_output_aliases={a: a for a in range(nb)},
    )(*gs)


def _gather_small(block):
    m_per, n = block.shape

    def body(x_ref, out_ref, send_sems, recv_sems, local_sem):
        x, y, c = lax.axis_index("x"), lax.axis_index("y"), lax.axis_index("c")
        me, sibling = (x, y, c), (x, y, 1 - c)
        chips = [(1 - x, y), (x, 1 - y), (1 - x, 1 - y)]

        def rows(px, py, pc):
            return out_ref.at[pl.ds((4 * px + 2 * py + pc) * m_per, m_per), :]

        def copy(k, blk, to, src=None):
            return pltpu.make_async_remote_copy(
                src_ref=rows(*blk) if src is None else src, dst_ref=rows(*blk), send_sem=send_sems.at[k],
                recv_sem=recv_sems.at[k], device_id=to, device_id_type=MESH)

        mine = pltpu.make_async_copy(x_ref, rows(*me), local_sem)
        mine.start()
        first = [copy(0, me, sibling, src=x_ref)]
        first += [copy(1 + j, me, (*chip, c), src=x_ref) for j, chip in enumerate(chips)]
        for cp in first:
            cp.start()
        passed = [copy(4 + j, (*chip, c), sibling) for j, chip in enumerate(chips)]
        for j, chip in enumerate(chips):
            copy(1 + j, (*chip, c), me).wait_recv()
            passed[j].start()
        copy(0, sibling, me).wait_recv()
        for j, chip in enumerate(chips):
            copy(4 + j, (*chip, 1 - c), me).wait_recv()
        for cp in first + passed:
            cp.wait_send()
        mine.wait()

    return pl.pallas_call(
        body, name="gather_small_grads",
        out_shape=jax.ShapeDtypeStruct((8 * m_per, n), block.dtype),
        in_specs=[pl.BlockSpec(memory_space=pltpu.VMEM)], out_specs=pl.BlockSpec(memory_space=pltpu.VMEM),
        scratch_shapes=[pltpu.SemaphoreType.DMA((7,)), pltpu.SemaphoreType.DMA((7,)), pltpu.SemaphoreType.DMA],
    )(block)


def _sum_small(stack):
    _, m_per, n = stack.shape

    def body(x_ref, o_ref):
        acc = x_ref[0]
        for dev in range(1, 8):
            acc = acc + x_ref[dev]
        o_ref[...] = acc

    return pl.pallas_call(body, name="sum_small_grads", out_shape=jax.ShapeDtypeStruct((m_per, n), F32))(stack)


def _adam_update(w_ref, g_ref, m_ref, v_ref, d_ref, mo_ref, vo_ref):
    gg = g_ref[...]
    mn = ADAM_B1 * m_ref[...] + (1.0 - ADAM_B1) * gg
    vn = ADAM_B2 * v_ref[...] + (1.0 - ADAM_B2) * (gg * gg)
    m_hat = mn / (1.0 - ADAM_B1 ** ADAM_STEP)
    v_hat = vn / (1.0 - ADAM_B2 ** ADAM_STEP)
    d_ref[...] = -ADAM_LR * (m_hat / (jnp.sqrt(v_hat) + ADAM_EPS) + ADAM_WD * w_ref[...])
    mo_ref[...] = mn
    vo_ref[...] = vn


def _adamw(name, w, g, m, v, rows):
    shape = w.shape

    def body(*refs):
        _adam_update(*refs)

    spec = pl.BlockSpec((rows,) + shape[1:], lambda i: (i,) + (0,) * (len(shape) - 1))
    return pl.pallas_call(
        body, name=name, grid=(shape[0] // rows,), in_specs=[spec] * 4, out_specs=[spec] * 3,
        out_shape=[jax.ShapeDtypeStruct(shape, F32)] * 3, compiler_params=_cp(("parallel",), 48),
    )(w, g, m, v)


def _adamw_small(ws, gs, ms, vs):
    n = len(ws)

    def body(*refs):
        for a in range(n):
            _adam_update(*[refs[q * n + a] for q in range(7)])

    outs = pl.pallas_call(
        body, name="adamw_small", out_shape=[jax.ShapeDtypeStruct(w.shape, F32) for w in ws] * 3,
    )(*ws, *gs, *ms, *vs)
    return outs[:n], outs[n:2 * n], outs[2 * n:]


ADAM_ROWS = dict(w_in_e=256, pool_w=4, w_out_e=256, w_in_o=128, w_out_o=256)


def _pack(parts, rows):
    flat = jnp.concatenate([p.reshape(-1).astype(F32) for p in parts])
    return jnp.pad(flat, (0, rows * 128 - flat.shape[0])).reshape(rows, 128)


def _unpack(buf, shapes):
    flat = buf.reshape(-1)
    out, off = [], 0
    for shp in shapes:
        n = 1
        for dim in shp:
            n *= dim
        out.append(flat[off:off + n].reshape(shp))
        off += n
    return out


WEIGHTS = ("even_norm", "even_w_in", "even_pool_w", "even_pool_scale", "even_ws", "even_bs", "even_w_out", "odd_norm",
           "odd_w_in", "odd_conv_w", "odd_w_out", "final_norm")
BIG_OF = dict(w_in_e="even_w_in", pool_w="even_pool_w", w_out_e="even_w_out", w_in_o="odd_w_in", w_out_o="odd_w_out")
SMALL = ("even_norm", "even_pool_scale", "even_ws", "even_bs", "final_norm", "odd_norm", "odd_conv_w")
SMALL_GRAD_ROWS = 576


def kernel(x, even_norm, even_w_in, even_pool_w, even_pool_scale, even_ws, even_bs, even_w_out, odd_norm, odd_w_in, odd_conv_w, odd_w_out, final_norm, loss_target, m_even_norm, m_even_w_in, m_even_pool_w, m_even_pool_scale, m_even_ws, m_even_bs, m_even_w_out, m_odd_norm, m_odd_w_in, m_odd_conv_w, m_odd_w_out, m_final_norm, v_even_norm, v_even_w_in, v_even_pool_w, v_even_pool_scale, v_even_ws, v_even_bs, v_even_w_out, v_odd_norm, v_odd_w_in, v_odd_conv_w, v_odd_w_out, v_final_norm):
    wv = dict(zip(WEIGHTS, (even_norm, even_w_in, even_pool_w, even_pool_scale, even_ws, even_bs, even_w_out, odd_norm,
                            odd_w_in, odd_conv_w, odd_w_out, final_norm)))
    mv = dict(zip(WEIGHTS, (m_even_norm, m_even_w_in, m_even_pool_w, m_even_pool_scale, m_even_ws, m_even_bs,
                            m_even_w_out, m_odd_norm, m_odd_w_in, m_odd_conv_w, m_odd_w_out, m_final_norm)))
    vv = dict(zip(WEIGHTS, (v_even_norm, v_even_w_in, v_even_pool_w, v_even_pool_scale, v_even_ws, v_even_bs,
                            v_even_w_out, v_odd_norm, v_odd_w_in, v_odd_conv_w, v_odd_w_out, v_final_norm)))
    c = lax.axis_index("c")
    k_me = 2 * lax.axis_index("x") + lax.axis_index("y")

    c_arr = jnp.reshape(c, (1,)).astype(jnp.int32)
    ck_arr = jnp.stack([c, k_me]).astype(jnp.int32)
    even_bigs, odd_bigs = BIGS[:3], BIGS[3:]

    shards = {b.name: wv[BIG_OF[b.name]][0].astype(BF16) for b in BIGS}
    tiny = jnp.concatenate([odd_conv_w[0], odd_norm], axis=0)
    *pieces_even, tiny_all = _gather_weights(even_bigs, [shards[b.name] for b in even_bigs], tiny)
    tiny_full = jnp.transpose(tiny_all, (1, 0, 2)).reshape(4, D_MODEL)
    to_chips, swap_odd, partials_odd = _copies_to_chips(odd_bigs), _copies_swap_halves(odd_bigs), _copies_partials(odd_bigs)
    gather_state, gather_token = _exchange_start(
        "gather_odd_start", [shards[b.name] for b in odd_bigs],
        [jax.ShapeDtypeStruct(_piece_shape(b), BF16) for b in odd_bigs], to_chips, 3 * len(odd_bigs), after=pieces_even[-1])
    k_arr = jnp.reshape(k_me, (1,)).astype(jnp.int32)
    w = {b.name: _assemble(b, p, shards[b.name], k_arr) for b, p in zip(even_bigs, pieces_even)}
    w.update(even_norm=even_norm, pool_scale=even_pool_scale, ws=even_ws[0], bs=even_bs[0],
             final_norm=final_norm.reshape(1, D_MODEL), conv_w=tiny_full[:3], odd_norm=tiny_full[3:4])

    class Hooks(_Hooks):
        def before_even(self):
            return gather_token

        def odd_weights(self, w, x1):
            srcs, lands = _exchange_wait("gather_odd_wait", gather_state, to_chips, after=x1)
            pieces = _finish_gather(odd_bigs, lands)
            return dict(w, **{b.name: _assemble(b, p, s, k_arr) for b, p, s in zip(odd_bigs, pieces, srcs)})

        def odd_grads_ready(self, g_w_in_o, g_w_out_o):
            self.swap, token = _exchange_start("rs_odd_swap_start", [g_w_in_o, g_w_out_o], _half_shapes(odd_bigs),
                                               swap_odd, len(odd_bigs))
            return token

        def even_mix_done(self, dz_e):
            grads, got = _exchange_wait("rs_odd_swap_wait", self.swap, swap_odd, after=dz_e)
            sums = [_chip_sum(b, g, r, c_arr) for b, g, r in zip(odd_bigs, grads, got)]
            self.partials, token = _exchange_start("rs_odd_partials_start", sums, _partial_shapes(odd_bigs), partials_odd,
                                                   3 * len(odd_bigs))
            return token

        def backward_done(self, dx0):
            self.sums, self.parts = _exchange_wait("rs_odd_partials_wait", self.partials, partials_odd, after=dx0)

    hooks = Hooks()
    loss, dx, g = _local_step(x[0], loss_target[0], w, hooks)

    got = _exchange("rs_even_swap", [g[b.name] for b in even_bigs], _half_shapes(even_bigs), _copies_swap_halves(even_bigs),
                    len(even_bigs))
    sums = [_chip_sum(b, g[b.name], r, c_arr) for b, r in zip(even_bigs, got)]
    partials_even = _copies_partials(even_bigs)
    even_state, _ = _exchange_start("rs_even_partials_start", sums, _partial_shapes(even_bigs), partials_even,
                                    3 * len(even_bigs))

    delta, new_m, new_v, g_shard = {}, {}, {}, {}

    def finish(bigs, sums_, parts_, name):
        halves = [_shard_sum(b, sm, p, ck_arr) for b, sm, p in zip(bigs, sums_, parts_)]
        for b, gs in zip(bigs, _share_halves(name, bigs, halves)):
            n = BIG_OF[b.name]
            g_shard[b.name] = gs
            d_, m_, v_ = _adamw(f"adamw_{b.name}", wv[n][0], gs, mv[n][0], vv[n][0], ADAM_ROWS[b.name])
            delta[n], new_m[n], new_v[n] = d_[None], m_[None], v_[None]

    finish(odd_bigs, hooks.sums, hooks.parts, "rs_share_halves_odd")

    small_g = _pack([g["even_norm"], g["pool_scale"], g["ws"], g["bs"], g["final_norm"], g["odd_norm"], g["conv_w"], loss],
                    SMALL_GRAD_ROWS)
    small_g = _sum_small(_gather_small(small_g).reshape(8, SMALL_GRAD_ROWS, 128))
    g_en, g_ps, g_ws, g_bs, g_fn, g_on, g_cw, loss = _unpack(
        small_g, [(1, D_MODEL), (1, D_MODEL), (1, 4, CHUNK, CHUNK), (1, 4, CHUNK), (D_MODEL,), (1, D_MODEL), (1, 3, D_MODEL), ()])
    g_on = lax.dynamic_slice(g_on, (0, k_me * 256), (1, 256))
    g_cw = lax.dynamic_slice(g_cw, (0, 0, k_me * 256), (1, 3, 256))
    grad = dict(even_norm=g_en, even_pool_scale=g_ps, even_ws=g_ws, even_bs=g_bs, final_norm=g_fn, odd_norm=g_on,
                odd_conv_w=g_cw)
    flat = [(wv[n].size // wv[n].shape[-1], wv[n].shape[-1]) for n in SMALL]
    outs = _adamw_small(*[[src[n].reshape(shp) for n, shp in zip(SMALL, flat)] for src in (wv, grad, mv, vv)])
    for dst, arrs in zip((delta, new_m, new_v), outs):
        for n, arr in zip(SMALL, arrs):
            dst[n] = arr.reshape(wv[n].shape)

    behind = new_v["odd_w_in"][0, :8, :128] + outs[2][2][:8, :]
    sums, parts = _exchange_wait("rs_even_partials_wait", even_state, partials_even, after=behind)
    finish(even_bigs, sums, parts, "rs_share_halves_even")
    for b in BIGS:
        grad[BIG_OF[b.name]] = g_shard[b.name][None]

    return (loss, dx[None], *[grad[n] for n in WEIGHTS], *[delta[n] for n in WEIGHTS], *[new_m[n] for n in WEIGHTS],
            *[new_v[n] for n in WEIGHTS])
```
